```python
import math
import jax, jax.numpy as jnp
from jax import lax
import numpy as np

D_MODEL = 1024
BATCH = 16
SEQ = 2048
DEPTH = 2

N_A = max(1, DEPTH // 2)
N_B = DEPTH - N_A

D_RNN = 1280
RNN_BLOCKS = 10
RNN_BW = D_RNN // RNN_BLOCKS
CONV_WIDTH = 4
LRU_C = 8.0

N_HEADS = 8
QK_NOPE = 128
QK_ROPE = 64
V_DIM = 128
KV_RANK = 256
Q_RANK = 384
ROPE_THETA = 10000.0
Q_BLOCK = 128
ATTN_SCALE = (QK_NOPE + QK_ROPE) ** -0.5
EPS = 1e-6

kernel_name = "yoco_rglru_mla_hybrid"


def rms_norm(x, g):
    xf = x.astype(jnp.float32)
    y = xf * lax.rsqrt(jnp.mean(xf * xf, axis=-1, keepdims=True) + EPS)
    return (y * g.astype(jnp.float32)).astype(x.dtype)


def rope_tables(seq_len):
    pos = jnp.arange(seq_len, dtype=jnp.float32)
    inv = ROPE_THETA ** (-jnp.arange(0, QK_ROPE, 2, dtype=jnp.float32) / QK_ROPE)
    ang = pos[:, None] * inv[None, :]
    return jnp.cos(ang), jnp.sin(ang)


def apply_rope(x, cos, sin):
    xf = x.astype(jnp.float32)
    x1, x2 = jnp.split(xf, 2, axis=-1)
    out = jnp.concatenate([x1 * cos - x2 * sin, x2 * cos + x1 * sin], axis=-1)
    return out.astype(x.dtype)


def causal_depthwise_conv(x, w, b):
    c = x.shape[-1]
    y = lax.conv_general_dilated(
        x, w[:, None, :].astype(x.dtype), window_strides=(1,),
        padding=[(CONV_WIDTH - 1, 0)], dimension_numbers=("NWC", "WIO", "NWC"),
        feature_group_count=c)
    return y + b


def _lin_combine(left, right):
    a1, b1 = left
    a2, b2 = right
    return a1 * a2, a2 * b1 + b2


def rglru_layer(x, norm_g, w_in, conv_w, conv_b, w_rg, b_rg, w_ig, b_ig, lam, w_out):
    bsz, seq, _ = x.shape
    h = rms_norm(x, norm_g)
    u = h @ w_in
    xb, gate = u[..., :D_RNN], u[..., D_RNN:]
    xb = causal_depthwise_conv(xb, conv_w, conv_b)
    xblk = xb.reshape(bsz, seq, RNN_BLOCKS, RNN_BW)
    r = jax.nn.sigmoid(jnp.einsum("bsnc,ncd->bsnd", xblk, w_rg).reshape(bsz, seq, D_RNN) + b_rg)
    i = jax.nn.sigmoid(jnp.einsum("bsnc,ncd->bsnd", xblk, w_ig).reshape(bsz, seq, D_RNN) + b_ig)
    log_a = -LRU_C * r.astype(jnp.float32) * jax.nn.softplus(-lam.astype(jnp.float32))
    a = jnp.exp(log_a)
    bterm = jnp.sqrt(-jnp.expm1(2.0 * log_a)) * (i * xb).astype(jnp.float32)
    _, hs = lax.associative_scan(_lin_combine, (a, bterm), axis=1)
    y = hs.astype(x.dtype) * jax.nn.silu(gate)
    return y @ w_out


def mla_shared_kv(x_stream, norm_kv, w_dkv, kv_norm, w_uk, w_uv, cos, sin):
    h = rms_norm(x_stream, norm_kv)
    ckr = h @ w_dkv
    c_kv = rms_norm(ckr[..., :KV_RANK], kv_norm)
    k_rope = apply_rope(ckr[..., KV_RANK:], cos[None], sin[None])
    k_nope = jnp.einsum("bsc,chd->bshd", c_kv, w_uk)
    v = jnp.einsum("bsc,chd->bshd", c_kv, w_uv)
    return k_nope, k_rope, v


def causal_block_attention(q_nope, q_rope, k_nope, k_rope, v):
    bsz, seq, nh, _ = q_nope.shape
    nb = seq // Q_BLOCK
    qn = q_nope.reshape(bsz, nb, Q_BLOCK, nh, QK_NOPE).transpose(1, 0, 2, 3, 4)
    qr = q_rope.reshape(bsz, nb, Q_BLOCK, nh, QK_ROPE).transpose(1, 0, 2, 3, 4)
    kpos = jnp.arange(seq)

    def one_block(args):
        qn_b, qr_b, bi = args
        s = jnp.einsum("bqhd,bkhd->bhqk", qn_b, k_nope, preferred_element_type=jnp.float32)
        s = s + jnp.einsum("bqhr,bkr->bhqk", qr_b, k_rope, preferred_element_type=jnp.float32)
        s = s * ATTN_SCALE
        qpos = bi * Q_BLOCK + jnp.arange(Q_BLOCK)
        mask = kpos[None, :] <= qpos[:, None]
        s = jnp.where(mask[None, None], s, -jnp.inf)
        p = jax.nn.softmax(s, axis=-1)
        return jnp.einsum("bhqk,bkhd->bqhd", p.astype(v.dtype), v)

    o = lax.map(one_block, (qn, qr, jnp.arange(nb)))
    return o.transpose(1, 0, 2, 3, 4).reshape(bsz, seq, nh, V_DIM)


def mla_layer(x, norm_g, w_in, q_norm, w_uq, w_out, k_nope, k_rope, v, cos, sin):
    bsz, seq, _ = x.shape
    h = rms_norm(x, norm_g)
    u = h @ w_in
    c_q = rms_norm(u[..., :Q_RANK], q_norm)
    gate = u[..., Q_RANK:]
    q = jnp.einsum("bsc,chd->bshd", c_q, w_uq)
    q_nope = q[..., :QK_NOPE]
    q_rope = apply_rope(q[..., QK_NOPE:], cos[None, :, None], sin[None, :, None])
    o = causal_block_attention(q_nope, q_rope, k_nope, k_rope, v)
    y = o.reshape(bsz, seq, N_HEADS * V_DIM) * jax.nn.silu(gate)
    return y @ w_out


def _fwd_setup_inputs(seed: int = 0) -> dict:
    key = jax.random.key(seed)
    ks = jax.random.split(key, 24)
    f32 = jnp.float32
    nrm = lambda k, shape, fan_in: jax.random.normal(k, shape, f32) * (fan_in ** -0.5)
    gain = lambda k, shape: 1.0 + 0.02 * jax.random.normal(k, shape, f32)
    small = lambda k, shape: 0.01 * jax.random.normal(k, shape, f32)

    u = jax.random.uniform(ks[8], (N_A, D_RNN), f32, 0.9, 0.999)
    a0 = u ** (1.0 / LRU_C)
    lam = jnp.log(a0) - jnp.log1p(-a0)

    return {
        "x": jax.random.normal(ks[0], (BATCH, SEQ, D_MODEL), f32),
        "norm_a": gain(ks[1], (N_A, D_MODEL)),
        "w_in_a": nrm(ks[2], (N_A, D_MODEL, 2 * D_RNN), D_MODEL),
        "conv_w": nrm(ks[3], (N_A, CONV_WIDTH, D_RNN), CONV_WIDTH),
        "conv_b": small(ks[4], (N_A, D_RNN)),
        "w_rg": nrm(ks[5], (N_A, RNN_BLOCKS, RNN_BW, RNN_BW), RNN_BW),
        "b_rg": small(ks[6], (N_A, D_RNN)),
        "w_ig": nrm(ks[7], (N_A, RNN_BLOCKS, RNN_BW, RNN_BW), RNN_BW),
        "b_ig": small(ks[9], (N_A, D_RNN)),
        "lru_lambda": lam,
        "w_out_a": nrm(ks[10], (N_A, D_RNN, D_MODEL), D_RNN),
        "norm_kv": gain(ks[11], (D_MODEL,)),
        "w_dkv": nrm(ks[12], (D_MODEL, KV_RANK + QK_ROPE), D_MODEL),
        "kv_norm": gain(ks[13], (KV_RANK,)),
        "w_uk": nrm(ks[14], (KV_RANK, N_HEADS, QK_NOPE), KV_RANK),
        "w_uv": nrm(ks[15], (KV_RANK, N_HEADS, V_DIM), KV_RANK),
        "norm_b": gain(ks[16], (N_B, D_MODEL)),
        "w_in_b": nrm(ks[17], (N_B, D_MODEL, Q_RANK + N_HEADS * V_DIM), D_MODEL),
        "q_norm": gain(ks[18], (N_B, Q_RANK)),
        "w_uq": nrm(ks[19], (N_B, Q_RANK, N_HEADS, QK_NOPE + QK_ROPE), Q_RANK),
        "w_out_b": nrm(ks[20], (N_B, N_HEADS * V_DIM, D_MODEL), N_HEADS * V_DIM),
        "final_norm": gain(ks[21], (D_MODEL,)),
    }


def _fwd_reference(x, norm_a, w_in_a, conv_w, conv_b, w_rg, b_rg, w_ig, b_ig, lru_lambda, w_out_a,
              norm_kv, w_dkv, kv_norm, w_uk, w_uv,
              norm_b, w_in_b, q_norm, w_uq, w_out_b, final_norm):
    seq = x.shape[1]
    cos, sin = rope_tables(seq)
    k_nope = k_rope = v = None
    for layer in range(DEPTH):
        if layer < N_A:
            x = x + rglru_layer(x, norm_a[layer], w_in_a[layer], conv_w[layer], conv_b[layer],
                                w_rg[layer], b_rg[layer], w_ig[layer], b_ig[layer],
                                lru_lambda[layer], w_out_a[layer])
        else:
            if layer == N_A:
                k_nope, k_rope, v = mla_shared_kv(x, norm_kv, w_dkv, kv_norm, w_uk, w_uv, cos, sin)
            j = layer - N_A
            x = x + mla_layer(x, norm_b[j], w_in_b[j], q_norm[j], w_uq[j], w_out_b[j],
                              k_nope, k_rope, v, cos, sin)
    return rms_norm(x, final_norm)


import jax as _jax
import jax.numpy as _jnp

TWIN_FORMAT = 'train_step'
FWD_PARAMS = ['x', 'norm_a', 'w_in_a', 'conv_w', 'conv_b', 'w_rg', 'b_rg', 'w_ig', 'b_ig', 'lru_lambda', 'w_out_a', 'norm_kv', 'w_dkv', 'kv_norm', 'w_uk', 'w_uv', 'norm_b', 'w_in_b', 'q_norm', 'w_uq', 'w_out_b', 'final_norm']
TWIN_WEIGHTS = ['norm_a', 'w_in_a', 'conv_w', 'conv_b', 'w_rg', 'b_rg', 'w_ig', 'b_ig', 'lru_lambda', 'w_out_a', 'norm_kv', 'w_dkv', 'kv_norm', 'w_uk', 'w_uv', 'norm_b', 'w_in_b', 'q_norm', 'w_uq', 'w_out_b', 'final_norm']
TWIN_DIFF_INPUT = 'x'
TWIN_INPUTS = ['x', 'norm_a', 'w_in_a', 'conv_w', 'conv_b', 'w_rg', 'b_rg', 'w_ig', 'b_ig', 'lru_lambda', 'w_out_a', 'norm_kv', 'w_dkv', 'kv_norm', 'w_uk', 'w_uv', 'norm_b', 'w_in_b', 'q_norm', 'w_uq', 'w_out_b', 'final_norm', 'loss_target', 'm_norm_a', 'm_w_in_a', 'm_conv_w', 'm_conv_b', 'm_w_rg', 'm_b_rg', 'm_w_ig', 'm_b_ig', 'm_lru_lambda', 'm_w_out_a', 'm_norm_kv', 'm_w_dkv', 'm_kv_norm', 'm_w_uk', 'm_w_uv', 'm_norm_b', 'm_w_in_b', 'm_q_norm', 'm_w_uq', 'm_w_out_b', 'm_final_norm', 'v_norm_a', 'v_w_in_a', 'v_conv_w', 'v_conv_b', 'v_w_rg', 'v_b_rg', 'v_w_ig', 'v_b_ig', 'v_lru_lambda', 'v_w_out_a', 'v_norm_kv', 'v_w_dkv', 'v_kv_norm', 'v_w_uk', 'v_w_uv', 'v_norm_b', 'v_w_in_b', 'v_q_norm', 'v_w_uq', 'v_w_out_b', 'v_final_norm']
TWIN_OUTPUTS = ['loss', 'grad_x', 'grad_norm_a', 'grad_w_in_a', 'grad_conv_w', 'grad_conv_b', 'grad_w_rg', 'grad_b_rg', 'grad_w_ig', 'grad_b_ig', 'grad_lru_lambda', 'grad_w_out_a', 'grad_norm_kv', 'grad_w_dkv', 'grad_kv_norm', 'grad_w_uk', 'grad_w_uv', 'grad_norm_b', 'grad_w_in_b', 'grad_q_norm', 'grad_w_uq', 'grad_w_out_b', 'grad_final_norm', 'delta_norm_a', 'delta_w_in_a', 'delta_conv_w', 'delta_conv_b', 'delta_w_rg', 'delta_b_rg', 'delta_w_ig', 'delta_b_ig', 'delta_lru_lambda', 'delta_w_out_a', 'delta_norm_kv', 'delta_w_dkv', 'delta_kv_norm', 'delta_w_uk', 'delta_w_uv', 'delta_norm_b', 'delta_w_in_b', 'delta_q_norm', 'delta_w_uq', 'delta_w_out_b', 'delta_final_norm', 'new_m_norm_a', 'new_m_w_in_a', 'new_m_conv_w', 'new_m_conv_b', 'new_m_w_rg', 'new_m_b_rg', 'new_m_w_ig', 'new_m_b_ig', 'new_m_lru_lambda', 'new_m_w_out_a', 'new_m_norm_kv', 'new_m_w_dkv', 'new_m_kv_norm', 'new_m_w_uk', 'new_m_w_uv', 'new_m_norm_b', 'new_m_w_in_b', 'new_m_q_norm', 'new_m_w_uq', 'new_m_w_out_b', 'new_m_final_norm', 'new_v_norm_a', 'new_v_w_in_a', 'new_v_conv_w', 'new_v_conv_b', 'new_v_w_rg', 'new_v_b_rg', 'new_v_w_ig', 'new_v_b_ig', 'new_v_lru_lambda', 'new_v_w_out_a', 'new_v_norm_kv', 'new_v_w_dkv', 'new_v_kv_norm', 'new_v_w_uk', 'new_v_w_uv', 'new_v_norm_b', 'new_v_w_in_b', 'new_v_q_norm', 'new_v_w_uq', 'new_v_w_out_b', 'new_v_final_norm']
TWIN_LEAF_KINDS = {'loss': 'loss', 'grad_x': 'grad_x', 'grad_norm_a': 'grad_w', 'grad_w_in_a': 'grad_w', 'grad_conv_w': 'grad_w', 'grad_conv_b': 'grad_w', 'grad_w_rg': 'grad_w', 'grad_b_rg': 'grad_w', 'grad_w_ig': 'grad_w', 'grad_b_ig': 'grad_w', 'grad_lru_lambda': 'grad_w', 'grad_w_out_a': 'grad_w', 'grad_norm_kv': 'grad_w', 'grad_w_dkv': 'grad_w', 'grad_kv_norm': 'grad_w', 'grad_w_uk': 'grad_w', 'grad_w_uv': 'grad_w', 'grad_norm_b': 'grad_w', 'grad_w_in_b': 'grad_w', 'grad_q_norm': 'grad_w', 'grad_w_uq': 'grad_w', 'grad_w_out_b': 'grad_w', 'grad_final_norm': 'grad_w', 'delta_norm_a': 'delta_w', 'delta_w_in_a': 'delta_w', 'delta_conv_w': 'delta_w', 'delta_conv_b': 'delta_w', 'delta_w_rg': 'delta_w', 'delta_b_rg': 'delta_w', 'delta_w_ig': 'delta_w', 'delta_b_ig': 'delta_w', 'delta_lru_lambda': 'delta_w', 'delta_w_out_a': 'delta_w', 'delta_norm_kv': 'delta_w', 'delta_w_dkv': 'delta_w', 'delta_kv_norm': 'delta_w', 'delta_w_uk': 'delta_w', 'delta_w_uv': 'delta_w', 'delta_norm_b': 'delta_w', 'delta_w_in_b': 'delta_w', 'delta_q_norm': 'delta_w', 'delta_w_uq': 'delta_w', 'delta_w_out_b': 'delta_w', 'delta_final_norm': 'delta_w', 'new_m_norm_a': 'new_m', 'new_m_w_in_a': 'new_m', 'new_m_conv_w': 'new_m', 'new_m_conv_b': 'new_m', 'new_m_w_rg': 'new_m', 'new_m_b_rg': 'new_m', 'new_m_w_ig': 'new_m', 'new_m_b_ig': 'new_m', 'new_m_lru_lambda': 'new_m', 'new_m_w_out_a': 'new_m', 'new_m_norm_kv': 'new_m', 'new_m_w_dkv': 'new_m', 'new_m_kv_norm': 'new_m', 'new_m_w_uk': 'new_m', 'new_m_w_uv': 'new_m', 'new_m_norm_b': 'new_m', 'new_m_w_in_b': 'new_m', 'new_m_q_norm': 'new_m', 'new_m_w_uq': 'new_m', 'new_m_w_out_b': 'new_m', 'new_m_final_norm': 'new_m', 'new_v_norm_a': 'new_v', 'new_v_w_in_a': 'new_v', 'new_v_conv_w': 'new_v', 'new_v_conv_b': 'new_v', 'new_v_w_rg': 'new_v', 'new_v_b_rg': 'new_v', 'new_v_w_ig': 'new_v', 'new_v_b_ig': 'new_v', 'new_v_lru_lambda': 'new_v', 'new_v_w_out_a': 'new_v', 'new_v_norm_kv': 'new_v', 'new_v_w_dkv': 'new_v', 'new_v_kv_norm': 'new_v', 'new_v_w_uk': 'new_v', 'new_v_w_uv': 'new_v', 'new_v_norm_b': 'new_v', 'new_v_w_in_b': 'new_v', 'new_v_q_norm': 'new_v', 'new_v_w_uq': 'new_v', 'new_v_w_out_b': 'new_v', 'new_v_final_norm': 'new_v'}


def _forward(args):
    return _fwd_reference(*[args[k] for k in FWD_PARAMS])


def _output_shape():
    out = _jax.eval_shape(lambda: _forward(_fwd_setup_inputs(0)))
    return out.shape, out.dtype

N_MICROBATCH = 1
ADAM_LR = 0.001
ADAM_B1 = 0.9
ADAM_B2 = 0.999
ADAM_EPS = 1e-08
ADAM_WD = 0.01
ADAM_STEP = 10
PER_EXAMPLE_BATCH_AXIS = {'x': 0, 'loss_target': 0}
SHARED_INPUTS = []
_WEIGHT_DTYPES = {'norm_a': _jnp.float32, 'w_in_a': _jnp.float32, 'conv_w': _jnp.float32, 'conv_b': _jnp.float32, 'w_rg': _jnp.float32, 'b_rg': _jnp.float32, 'w_ig': _jnp.float32, 'b_ig': _jnp.float32, 'lru_lambda': _jnp.float32, 'w_out_a': _jnp.float32, 'norm_kv': _jnp.float32, 'w_dkv': _jnp.float32, 'kv_norm': _jnp.float32, 'w_uk': _jnp.float32, 'w_uv': _jnp.float32, 'norm_b': _jnp.float32, 'w_in_b': _jnp.float32, 'q_norm': _jnp.float32, 'w_uq': _jnp.float32, 'w_out_b': _jnp.float32, 'final_norm': _jnp.float32}
MOMENT_SCALE = {'norm_a': 8.753354e-02, 'w_in_a': 5.914097e-02, 'conv_w': 6.113615e-02, 'conv_b': 7.827960e-01, 'w_rg': 2.077564e-02, 'b_rg': 1.588756e-02, 'w_ig': 3.733685e-02, 'b_ig': 2.294690e-02, 'lru_lambda': 3.131217e-02, 'w_out_a': 6.736548e-02, 'norm_kv': 2.764381e-02, 'w_dkv': 5.032046e-02, 'kv_norm': 5.655001e-02, 'w_uk': 1.517031e-02, 'w_uv': 2.258807e-02, 'norm_b': 3.017211e-02, 'w_in_b': 2.533704e-02, 'q_norm': 2.999762e-02, 'w_uq': 1.467283e-02, 'w_out_b': 2.263937e-02, 'final_norm': 3.199274e+01}


def _to_microbatches(a, axis):
    t = _jnp.moveaxis(a, axis, 0)
    t = t.reshape((N_MICROBATCH, t.shape[0] // N_MICROBATCH) + t.shape[1:])
    return _jnp.moveaxis(t, 1, axis + 1)


def setup_inputs(seed: int = 0) -> dict:
    inp = _fwd_setup_inputs(seed)
    key = _jax.random.fold_in(_jax.random.key(seed), 7919)
    shape, _ = _output_shape()
    out = dict(inp)
    out["loss_target"] = _jax.random.normal(_jax.random.fold_in(key, 0), shape, _jnp.float32)
    for i, name in enumerate(TWIN_WEIGHTS):
        w = inp[name].astype(_jnp.float32)
        if MOMENT_SCALE is None:
            s = _jnp.sqrt(_jnp.mean(_jnp.square(w)) + 1e-30)
        else:
            s = MOMENT_SCALE[name]
        km, kv = _jax.random.split(_jax.random.fold_in(key, i + 1))
        out[name] = w
        out["m_" + name] = s * _jax.random.normal(km, w.shape, _jnp.float32)
        out["v_" + name] = (s * s) * _jax.random.uniform(kv, w.shape, _jnp.float32, 0.5, 1.5)
    if N_MICROBATCH > 1:
        for name, axis in PER_EXAMPLE_BATCH_AXIS.items():
            out[name] = _to_microbatches(out[name], axis)
    return {'x': out['x'], 'norm_a': out['norm_a'], 'w_in_a': out['w_in_a'], 'conv_w': out['conv_w'], 'conv_b': out['conv_b'], 'w_rg': out['w_rg'], 'b_rg': out['b_rg'], 'w_ig': out['w_ig'], 'b_ig': out['b_ig'], 'lru_lambda': out['lru_lambda'], 'w_out_a': out['w_out_a'], 'norm_kv': out['norm_kv'], 'w_dkv': out['w_dkv'], 'kv_norm': out['kv_norm'], 'w_uk': out['w_uk'], 'w_uv': out['w_uv'], 'norm_b': out['norm_b'], 'w_in_b': out['w_in_b'], 'q_norm': out['q_norm'], 'w_uq': out['w_uq'], 'w_out_b': out['w_out_b'], 'final_norm': out['final_norm'], 'loss_target': out['loss_target'], 'm_norm_a': out['m_norm_a'], 'm_w_in_a': out['m_w_in_a'], 'm_conv_w': out['m_conv_w'], 'm_conv_b': out['m_conv_b'], 'm_w_rg': out['m_w_rg'], 'm_b_rg': out['m_b_rg'], 'm_w_ig': out['m_w_ig'], 'm_b_ig': out['m_b_ig'], 'm_lru_lambda': out['m_lru_lambda'], 'm_w_out_a': out['m_w_out_a'], 'm_norm_kv': out['m_norm_kv'], 'm_w_dkv': out['m_w_dkv'], 'm_kv_norm': out['m_kv_norm'], 'm_w_uk': out['m_w_uk'], 'm_w_uv': out['m_w_uv'], 'm_norm_b': out['m_norm_b'], 'm_w_in_b': out['m_w_in_b'], 'm_q_norm': out['m_q_norm'], 'm_w_uq': out['m_w_uq'], 'm_w_out_b': out['m_w_out_b'], 'm_final_norm': out['m_final_norm'], 'v_norm_a': out['v_norm_a'], 'v_w_in_a': out['v_w_in_a'], 'v_conv_w': out['v_conv_w'], 'v_conv_b': out['v_conv_b'], 'v_w_rg': out['v_w_rg'], 'v_b_rg': out['v_b_rg'], 'v_w_ig': out['v_w_ig'], 'v_b_ig': out['v_b_ig'], 'v_lru_lambda': out['v_lru_lambda'], 'v_w_out_a': out['v_w_out_a'], 'v_norm_kv': out['v_norm_kv'], 'v_w_dkv': out['v_w_dkv'], 'v_kv_norm': out['v_kv_norm'], 'v_w_uk': out['v_w_uk'], 'v_w_uv': out['v_w_uv'], 'v_norm_b': out['v_norm_b'], 'v_w_in_b': out['v_w_in_b'], 'v_q_norm': out['v_q_norm'], 'v_w_uq': out['v_w_uq'], 'v_w_out_b': out['v_w_out_b'], 'v_final_norm': out['v_final_norm']}


def _loss(weights, diff, rest, loss_target):
    with _jax.named_scope("forward"):
        args = {**rest, TWIN_DIFF_INPUT: diff, **{k: w.astype(_WEIGHT_DTYPES[k]) for k, w in weights.items()}}
        y = _forward(args)
    with _jax.named_scope("loss_head"):
        err = _jnp.square(y.astype(_jnp.float32) - loss_target)
        return 0.5 * _jnp.sum(_jnp.mean(err, axis=-1)) if err.ndim else 0.5 * err


def _adamw(w, g, m, v):
    m = ADAM_B1 * m + (1.0 - ADAM_B1) * g
    v = ADAM_B2 * v + (1.0 - ADAM_B2) * _jnp.square(g)
    m_hat = m / (1.0 - ADAM_B1 ** ADAM_STEP)
    v_hat = v / (1.0 - ADAM_B2 ** ADAM_STEP)
    delta = -ADAM_LR * (m_hat / (_jnp.sqrt(v_hat) + ADAM_EPS) + ADAM_WD * w)
    return delta, m, v


def reference(x, norm_a, w_in_a, conv_w, conv_b, w_rg, b_rg, w_ig, b_ig, lru_lambda, w_out_a, norm_kv, w_dkv, kv_norm, w_uk, w_uv, norm_b, w_in_b, q_norm, w_uq, w_out_b, final_norm, loss_target, m_norm_a, m_w_in_a, m_conv_w, m_conv_b, m_w_rg, m_b_rg, m_w_ig, m_b_ig, m_lru_lambda, m_w_out_a, m_norm_kv, m_w_dkv, m_kv_norm, m_w_uk, m_w_uv, m_norm_b, m_w_in_b, m_q_norm, m_w_uq, m_w_out_b, m_final_norm, v_norm_a, v_w_in_a, v_conv_w, v_conv_b, v_w_rg, v_b_rg, v_w_ig, v_b_ig, v_lru_lambda, v_w_out_a, v_norm_kv, v_w_dkv, v_kv_norm, v_w_uk, v_w_uv, v_norm_b, v_w_in_b, v_q_norm, v_w_uq, v_w_out_b, v_final_norm):
    given = dict(x=x, norm_a=norm_a, w_in_a=w_in_a, conv_w=conv_w, conv_b=conv_b, w_rg=w_rg, b_rg=b_rg, w_ig=w_ig, b_ig=b_ig, lru_lambda=lru_lambda, w_out_a=w_out_a, norm_kv=norm_kv, w_dkv=w_dkv, kv_norm=kv_norm, w_uk=w_uk, w_uv=w_uv, norm_b=norm_b, w_in_b=w_in_b, q_norm=q_norm, w_uq=w_uq, w_out_b=w_out_b, final_norm=final_norm, loss_target=loss_target, m_norm_a=m_norm_a, m_w_in_a=m_w_in_a, m_conv_w=m_conv_w, m_conv_b=m_conv_b, m_w_rg=m_w_rg, m_b_rg=m_b_rg, m_w_ig=m_w_ig, m_b_ig=m_b_ig, m_lru_lambda=m_lru_lambda, m_w_out_a=m_w_out_a, m_norm_kv=m_norm_kv, m_w_dkv=m_w_dkv, m_kv_norm=m_kv_norm, m_w_uk=m_w_uk, m_w_uv=m_w_uv, m_norm_b=m_norm_b, m_w_in_b=m_w_in_b, m_q_norm=m_q_norm, m_w_uq=m_w_uq, m_w_out_b=m_w_out_b, m_final_norm=m_final_norm, v_norm_a=v_norm_a, v_w_in_a=v_w_in_a, v_conv_w=v_conv_w, v_conv_b=v_conv_b, v_w_rg=v_w_rg, v_b_rg=v_b_rg, v_w_ig=v_w_ig, v_b_ig=v_b_ig, v_lru_lambda=v_lru_lambda, v_w_out_a=v_w_out_a, v_norm_kv=v_norm_kv, v_w_dkv=v_w_dkv, v_kv_norm=v_kv_norm, v_w_uk=v_w_uk, v_w_uv=v_w_uv, v_norm_b=v_norm_b, v_w_in_b=v_w_in_b, v_q_norm=v_q_norm, v_w_uq=v_w_uq, v_w_out_b=v_w_out_b, v_final_norm=v_final_norm)
    weights = {n: given[n] for n in TWIN_WEIGHTS}
    shared = {n: given[n] for n in SHARED_INPUTS}
    per_example = {n: given[n] for n in ['x']}
    grad_fn = _jax.value_and_grad(_loss, argnums=(0, 1))

    def one_microbatch(ex, loss_target):
        ex = dict(ex)
        diff = ex.pop(TWIN_DIFF_INPUT)
        return grad_fn(weights, diff, {**shared, **ex}, loss_target)

    if N_MICROBATCH == 1:
        loss, (grad_w, grad_x) = one_microbatch(per_example, given["loss_target"])
    else:
        def body(carry, xs):
            loss_sum, grad_sum = carry
            l_k, (gw_k, gx_k) = one_microbatch(xs[0], xs[1])
            with _jax.named_scope("update"):
                return (loss_sum + l_k, _jax.tree.map(_jnp.add, grad_sum, gw_k)), gx_k

        init = (_jnp.zeros((), _jnp.float32), _jax.tree.map(_jnp.zeros_like, weights))
        (loss, grad_w), grad_x = _jax.lax.scan(body, init, (per_example, given["loss_target"]))
    with _jax.named_scope("update"):
        delta_w, new_m, new_v = {}, {}, {}
        for n in TWIN_WEIGHTS:
            delta_w[n], new_m[n], new_v[n] = _adamw(weights[n], grad_w[n], given["m_" + n], given["v_" + n])
    return (loss, grad_x, *[grad_w[n] for n in TWIN_WEIGHTS], *[delta_w[n] for n in TWIN_WEIGHTS],
            *[new_m[n] for n in TWIN_WEIGHTS], *[new_v[n] for n in TWIN_WEIGHTS])
```

```python
import functools
import math

import jax
import jax.numpy as jnp
from jax import lax
from jax.experimental import pallas as pl
from jax.experimental.pallas import tpu as pltpu

F32, BF16 = jnp.float32, jnp.bfloat16
EPS = 1e-6
LRU_C = 8.0
ROPE_THETA = 10000.0
QK_NOPE, QK_ROPE = 128, 64
ATTN_SCALE = (QK_NOPE + QK_ROPE) ** -0.5
LANE = 128
SUBLANE = 8
VMEM_LIMIT = 60000 * 1024
ADAM_LR, ADAM_B1, ADAM_B2, ADAM_EPS, ADAM_WD, ADAM_STEP = 0.001, 0.9, 0.999, 1e-08, 0.01, 10
MESH = pl.DeviceIdType.MESH
PACK_ALIGN = 16 * 1024
PACK_COLS = 1024

SHARDED = (("norm_a", 1), ("w_in_a", 2), ("conv_w", 2), ("conv_b", 1), ("b_rg", 1), ("b_ig", 1), ("lru_lambda", 1),
           ("w_out_a", 1), ("w_dkv", 0), ("w_uk", 0), ("w_uv", 0), ("w_in_b", 2), ("w_uq", 1), ("w_out_b", 1))
SHARD_AXIS = dict(SHARDED)
BIG = ("w_in_a", "w_out_a", "w_dkv", "w_uk", "w_uv", "w_in_b", "w_uq", "w_out_b")
SMALL = ("norm_a", "conv_w", "conv_b", "b_rg", "b_ig", "lru_lambda")
REPL = ("w_rg", "w_ig", "norm_kv", "kv_norm", "norm_b", "q_norm", "final_norm")
WEIGHTS = ("norm_a", "w_in_a", "conv_w", "conv_b", "w_rg", "b_rg", "w_ig", "b_ig", "lru_lambda", "w_out_a", "norm_kv",
           "w_dkv", "kv_norm", "w_uk", "w_uv", "norm_b", "w_in_b", "q_norm", "w_uq", "w_out_b", "final_norm")


def _sds(shape, dtype=F32):
    return jax.ShapeDtypeStruct(tuple(shape), dtype)


def _params(n_grid):
    return pltpu.CompilerParams(dimension_semantics=("arbitrary",) * n_grid, vmem_limit_bytes=VMEM_LIMIT)


def _full(shape):
    nd = len(shape)
    return pl.BlockSpec(tuple(shape), lambda *g: (0,) * nd)


def _dot(a, b):
    return jnp.dot(a, b, preferred_element_type=F32)


def _dot_nt(a, b):
    return lax.dot_general(a, b, (((1,), (1,)), ((), ())), preferred_element_type=F32)


def _dot_tn(a, b):
    return lax.dot_general(a, b, (((0,), (0,)), ((), ())), preferred_element_type=F32)


def _rinv(x):
    return lax.rsqrt(jnp.mean(x * x, axis=-1, keepdims=True) + EPS)


def _rms_bwd(x, rinv, g, dy):
    z = dy * g
    dx = rinv * z - x * (rinv * rinv * rinv) * jnp.mean(z * x, axis=-1, keepdims=True)
    dg = jnp.sum(dy * (x * rinv), axis=0, keepdims=True)
    return dx, dg


def _softplus(z):
    return jnp.maximum(z, 0.0) + jnp.log1p(jnp.exp(-jnp.abs(z)))


def _neg_expm1(z):
    series = -z * (1 + z / 2 * (1 + z / 3 * (1 + z / 4 * (1 + z / 5 * (1 + z / 6)))))
    return jnp.where(z > -0.3, series, 1.0 - jnp.exp(z))


def _swap_halves(x):
    w = x.shape[1]
    lane = lax.broadcasted_iota(jnp.int32, x.shape, 1)
    return jnp.where(lane % QK_ROPE < QK_ROPE // 2, pltpu.roll(x, w - QK_ROPE // 2, 1), pltpu.roll(x, QK_ROPE // 2, 1))


def _rope_tables(seq):
    pos = jnp.arange(seq, dtype=F32)
    inv = ROPE_THETA ** (-jnp.arange(0, QK_ROPE, 2, dtype=F32) / QK_ROPE)
    ang = pos[:, None] * inv[None, :]
    cos, sin = jnp.cos(ang), jnp.sin(ang)
    zero = jnp.zeros((seq, LANE - QK_ROPE), F32)
    return jnp.concatenate([cos, cos, zero], 1), jnp.concatenate([-sin, sin, zero], 1)


def _gates(xb, wrg_ref, brg, wig_ref, big, nblocks):
    xbb = xb.astype(BF16)
    rg = [_dot(xbb[:, n * LANE:(n + 1) * LANE], wrg_ref[n]) for n in range(nblocks)]
    ig = [_dot(xbb[:, n * LANE:(n + 1) * LANE], wig_ref[n]) for n in range(nblocks)]
    r = jax.nn.sigmoid(jnp.concatenate(rg, axis=1) + brg)
    i = jax.nn.sigmoid(jnp.concatenate(ig, axis=1) + big)
    return r, i


def _conv(xpad, cw_ref, cb, tb):
    return (cb + cw_ref[3:4, :] * xpad[pl.ds(8, tb), :] + cw_ref[2:3, :] * xpad[pl.ds(7, tb), :]
            + cw_ref[1:2, :] * xpad[pl.ds(6, tb), :] + cw_ref[0:1, :] * xpad[pl.ds(5, tb), :])


def _fa_fwd(x, w, seq, tb):
    t_all, d = x.shape
    dr = w["conv_b"].shape[1]
    nblocks = w["w_rg"].shape[0]
    nblk = seq // tb
    nt = tb // SUBLANE

    def body(x_ref, na, win, cw, cb, wrg, brg, wig, big, lam, wout, x1_ref, u_ref, hs_ref, h_ref, y_ref, xpad, a_s, b_s, carry):
        @pl.when(pl.program_id(1) == 0)
        def _():
            xpad[pl.ds(0, 8), :] = jnp.zeros((8, dr), F32)
            carry[...] = jnp.zeros((8, dr), F32)

        xv = x_ref[...]
        h = (xv * _rinv(xv) * na[...]).astype(BF16)
        h_ref[...] = h
        u = _dot(h, win[...])
        u_ref[...] = u
        xpre, gate = u[:, :dr], u[:, dr:]
        xpad[pl.ds(8, tb), :] = xpre
        xb = _conv(xpad, cw, cb[...], tb)
        xpad[pl.ds(0, 8), :] = xpre[tb - 8:, :]
        r, i = _gates(xb, wrg, brg[...], wig, big[...], nblocks)
        log_a = -LRU_C * r * _softplus(-lam[...])
        a_s[...] = jnp.exp(log_a)
        b_s[...] = jnp.sqrt(_neg_expm1(2.0 * log_a)) * (i * xb)
        row = lax.broadcasted_iota(jnp.int32, (8, dr), 0)

        def step(t, c):
            r0 = pl.multiple_of(t * 8, 8)
            a = a_s[pl.ds(r0, 8), :]
            b = b_s[pl.ds(r0, 8), :]
            for s in (1, 2, 4):
                m = row >= s
                a_sh = jnp.where(m, pltpu.roll(a, s, 0), 1.0)
                b_sh = jnp.where(m, pltpu.roll(b, s, 0), 0.0)
                b = a * b_sh + b
                a = a * a_sh
            hh = b + a * c
            hs_ref[pl.ds(r0, 8), :] = hh
            return jnp.broadcast_to(hh[7:8, :], hh.shape)

        carry[...] = lax.fori_loop(0, nt, step, carry[...])
        y = (hs_ref[...] * (gate * jax.nn.sigmoid(gate))).astype(BF16)
        y_ref[...] = y
        x1_ref[...] = xv + _dot(y, wout[...])

    tok = lambda c: pl.BlockSpec((tb, c), lambda b, j: (b * nblk + j, 0))
    consts = [w["norm_a"], w["w_in_a"], w["conv_w"], w["conv_b"], w["w_rg"], w["b_rg"], w["w_ig"], w["b_ig"], w["lru_lambda"], w["w_out_a"]]
    return pl.pallas_call(
        body, name="fa_fwd", grid=(t_all // seq, nblk),
        in_specs=[tok(d)] + [_full(c.shape) for c in consts],
        out_specs=[tok(d), tok(2 * dr), tok(dr), tok(d), tok(dr)],
        out_shape=[_sds((t_all, d)), _sds((t_all, 2 * dr)), _sds((t_all, dr)), _sds((t_all, d), BF16), _sds((t_all, dr), BF16)],
        scratch_shapes=[pltpu.VMEM((tb + 8, dr), F32), pltpu.VMEM((tb, dr), F32), pltpu.VMEM((tb, dr), F32), pltpu.VMEM((8, dr), F32)],
        compiler_params=_params(2),
    )(x, *consts)


def _fb_fwd(x1, w, cos_t, sin_t, seq, tb):
    t_all, d = x1.shape
    kvr = w["kv_norm"].shape[1]
    qr = w["q_norm"].shape[1]
    hv = w["w_uk"].shape[1]
    nheads = hv // LANE
    npos = seq // tb

    def body(x_ref, nkv, nb, wdkv, kvn, wuk, wuv, winb, qn, wuqn, wuqr, cos_ref, sin_ref,
             qn_o, qr_o, kn_o, kr_o, v_o, ub_o, ckr_o, hb_o, hk_o, cq_o, ckv_o):
        xv = x_ref[...]
        xh = xv * _rinv(xv)
        hk = (xh * nkv[...]).astype(BF16)
        hb = (xh * nb[...]).astype(BF16)
        hk_o[...] = hk
        hb_o[...] = hb
        cos, sin = cos_ref[...], sin_ref[...]
        ckr = _dot(hk, wdkv[...])
        ckr_o[...] = ckr
        ckv_pre = ckr[:, :kvr]
        ckv = (ckv_pre * _rinv(ckv_pre) * kvn[...]).astype(BF16)
        ckv_o[...] = ckv
        kr = ckr[:, kvr:]
        kr_o[...] = (kr * cos + _swap_halves(kr) * sin).astype(BF16)
        kn_o[...] = _dot(ckv, wuk[...]).astype(BF16)
        v_o[...] = _dot(ckv, wuv[...]).astype(BF16)
        ub = _dot(hb, winb[...])
        ub_o[...] = ub
        cq_pre = ub[:, :qr]
        cq = (cq_pre * _rinv(cq_pre) * qn[...]).astype(BF16)
        cq_o[...] = cq
        qn_o[...] = _dot(cq, wuqn[...]).astype(BF16)
        qrope = _dot(cq, wuqr[...])
        qr_o[...] = (qrope * jnp.tile(cos, (1, nheads)) + _swap_halves(qrope) * jnp.tile(sin, (1, nheads))).astype(BF16)

    tok = lambda c: pl.BlockSpec((tb, c), lambda i: (i, 0))
    pos = pl.BlockSpec((tb, LANE), lambda i: (i % npos, 0))
    consts = [w["norm_kv"], w["norm_b"], w["w_dkv_p"], w["kv_norm"], w["w_uk"], w["w_uv"], w["w_in_b"], w["q_norm"], w["w_uq_n"], w["w_uq_r"]]
    outs = [(hv, BF16), (hv, BF16), (hv, BF16), (LANE, BF16), (hv, BF16), (qr + hv, F32), (kvr + LANE, F32), (d, BF16), (d, BF16), (qr, BF16), (kvr, BF16)]
    return pl.pallas_call(
        body, name="fb_fwd", grid=(t_all // tb,),
        in_specs=[tok(d)] + [_full(c.shape) for c in consts] + [pos, pos],
        out_specs=[tok(c) for c, _ in outs],
        out_shape=[_sds((t_all, c), dt) for c, dt in outs],
        compiler_params=_params(1),
    )(x1, *consts, cos_t, sin_t)


def _causal_mask(i, j, ta):
    rows = i * ta + lax.broadcasted_iota(jnp.int32, (ta, ta), 0)
    cols = j * ta + lax.broadcasted_iota(jnp.int32, (ta, ta), 1)
    return cols <= rows


def _attn_fwd(qn, qr, kn, kr, v, seq, ta):
    t_all, hv = qn.shape
    nheads, nb, na = hv // LANE, t_all // seq, seq // ta

    def body(qn_ref, qr_ref, kn_ref, kr_ref, v_ref, o_ref, lse_ref, m_s, l_s, acc_s):
        i, j = pl.program_id(2), pl.program_id(3)

        @pl.when(j == 0)
        def _():
            m_s[...] = jnp.full((ta, LANE), -1e30, F32)
            l_s[...] = jnp.zeros((ta, LANE), F32)
            acc_s[...] = jnp.zeros((ta, LANE), F32)

        @pl.when(j <= i)
        def _():
            q = jnp.concatenate([qn_ref[...], qr_ref[...]], axis=1)
            k = jnp.concatenate([kn_ref[...], kr_ref[...]], axis=1)
            s = jnp.where(_causal_mask(i, j, ta), _dot_nt(q, k) * ATTN_SCALE, -1e30)
            m_prev = m_s[...]
            m_new = jnp.maximum(m_prev, jnp.max(s, axis=1, keepdims=True))
            p = jnp.exp(s - jnp.tile(m_new, (1, ta // LANE)))
            alpha = jnp.exp(m_prev - m_new)
            l_s[...] = alpha * l_s[...] + jnp.sum(p, axis=1, keepdims=True)
            acc_s[...] = alpha * acc_s[...] + _dot(p.astype(BF16), v_ref[...])
            m_s[...] = m_new

        @pl.when(j == na - 1)
        def _():
            o_ref[...] = acc_s[...] / l_s[...]
            lse_ref[...] = m_s[...] + jnp.log(l_s[...])

    qspec = pl.BlockSpec((ta, LANE), lambda b, h, i, j: (b * na + i, h))
    kspec = pl.BlockSpec((ta, LANE), lambda b, h, i, j: (b * na + jnp.minimum(i, j), h))
    krspec = pl.BlockSpec((ta, LANE), lambda b, h, i, j: (b * na + jnp.minimum(i, j), 0))
    return pl.pallas_call(
        body, name="attn_fwd", grid=(nb, nheads, na, na),
        in_specs=[qspec, qspec, kspec, krspec, kspec],
        out_specs=[qspec, qspec],
        out_shape=[_sds((t_all, hv)), _sds((t_all, hv))],
        scratch_shapes=[pltpu.VMEM((ta, LANE), F32)] * 3,
        compiler_params=_params(4),
    )(qn, qr, kn, kr, v)


def _attn_bwd(qn, qr, kn, kr, v, do, lse, delta, seq, ta):
    t_all, hv = qn.shape
    nheads, nb, na = hv // LANE, t_all // seq, seq // ta

    def body(qn_ref, qr_ref, kn_ref, kr_ref, v_ref, do_ref, lse_ref, dl_ref, dqn_ref, dqr_ref, dkn_ref, dkr_ref, dv_ref, dk_s, dv_s):
        j, i = pl.program_id(2), pl.program_id(3)

        @pl.when((j == 0) & (i == 0))
        def _():
            dqn_ref[...] = jnp.zeros((seq, LANE), F32)
            dqr_ref[...] = jnp.zeros((seq, LANE), F32)

        @pl.when(i == 0)
        def _():
            dk_s[...] = jnp.zeros((ta, 2 * LANE), F32)
            dv_s[...] = jnp.zeros((ta, LANE), F32)

        @pl.when(i >= j)
        def _():
            q = jnp.concatenate([qn_ref[...], qr_ref[...]], axis=1)
            k = jnp.concatenate([kn_ref[...], kr_ref[...]], axis=1)
            do_b = do_ref[...]
            s = jnp.where(_causal_mask(i, j, ta), _dot_nt(q, k) * ATTN_SCALE, -1e30)
            p = jnp.exp(s - jnp.tile(lse_ref[...], (1, ta // LANE)))
            dv_s[...] += _dot_tn(p.astype(BF16), do_b)
            dp = _dot_nt(do_b, v_ref[...])
            ds = (p * (dp - jnp.tile(dl_ref[...], (1, ta // LANE))) * ATTN_SCALE).astype(BF16)
            dk_s[...] += _dot_tn(ds, q)
            dq = _dot(ds, k)
            rows = pl.ds(pl.multiple_of(i * ta, ta), ta)
            dqn_ref[rows, :] += dq[:, :LANE]
            dqr_ref[rows, :] += dq[:, LANE:]

        @pl.when(i == na - 1)
        def _():
            dkn_ref[...] = dk_s[:, :LANE]
            dkr_ref[...] = dk_s[:, LANE:]
            dv_ref[...] = dv_s[...]

    qspec = pl.BlockSpec((ta, LANE), lambda b, h, j, i: (b * na + jnp.maximum(i, j), h))
    kspec = pl.BlockSpec((ta, LANE), lambda b, h, j, i: (b * na + j, h))
    krspec = pl.BlockSpec((ta, LANE), lambda b, h, j, i: (b * na + j, 0))
    dqspec = pl.BlockSpec((seq, LANE), lambda b, h, j, i: (b, h))
    return pl.pallas_call(
        body, name="attn_bwd", grid=(nb, nheads, na, na),
        in_specs=[qspec, qspec, kspec, krspec, kspec, qspec, qspec, qspec],
        out_specs=[dqspec, dqspec, kspec, kspec, kspec],
        out_shape=[_sds((t_all, hv))] * 5,
        scratch_shapes=[pltpu.VMEM((ta, 2 * LANE), F32), pltpu.VMEM((ta, LANE), F32)],
        compiler_params=_params(4),
    )(qn, qr, kn, kr, v, do, lse, delta)


def _head(o, ub, x1, target, w, tb):
    t_all, d = x1.shape
    hv = o.shape[1]
    qr = ub.shape[1] - hv
    nheads = hv // LANE

    def body(o_ref, ub_ref, x1_ref, tg_ref, wob, wobt, gf, loss_ref, dgf_ref, yb_ref, dx2_ref, do_ref, dg_ref, dl_ref):
        @pl.when(pl.program_id(0) == 0)
        def _():
            loss_ref[...] = jnp.zeros((1, LANE), F32)
            dgf_ref[...] = jnp.zeros((1, d), F32)

        ov = o_ref[...]
        g = ub_ref[:, qr:]
        sg = jax.nn.sigmoid(g)
        silu = g * sg
        yb = (ov * silu).astype(BF16)
        yb_ref[...] = yb
        x2 = x1_ref[...] + _dot(yb, wob[...])
        rinv = _rinv(x2)
        err = x2 * rinv * gf[...] - tg_ref[...]
        loss_ref[...] += (0.5 / d) * jnp.sum(jnp.sum(err * err, axis=1, keepdims=True), axis=0, keepdims=True)
        dx2, dgf = _rms_bwd(x2, rinv, gf[...], err * (1.0 / d))
        dgf_ref[...] += dgf
        dx2_ref[...] = dx2
        dyb = _dot(dx2.astype(BF16), wobt[...])
        dov = dyb * silu
        do_ref[...] = dov.astype(BF16)
        dg_ref[...] = dyb * ov * (sg * (1.0 + g * (1.0 - sg)))
        prod = dov * ov
        dl_ref[...] = jnp.concatenate(
            [jnp.broadcast_to(jnp.sum(prod[:, n * LANE:(n + 1) * LANE], axis=1, keepdims=True), (tb, LANE)) for n in range(nheads)], axis=1)

    tok = lambda c: pl.BlockSpec((tb, c), lambda i: (i, 0))
    consts = [w["w_out_b"], w["w_out_b_t"], w["final_norm"]]
    return pl.pallas_call(
        body, name="head", grid=(t_all // tb,),
        in_specs=[tok(hv), tok(qr + hv), tok(d), tok(d)] + [_full(c.shape) for c in consts],
        out_specs=[_full((1, LANE)), _full((1, d)), tok(hv), tok(d), tok(hv), tok(hv), tok(hv)],
        out_shape=[_sds((1, LANE)), _sds((1, d)), _sds((t_all, hv), BF16), _sds((t_all, d)), _sds((t_all, hv), BF16), _sds((t_all, hv)), _sds((t_all, hv))],
        compiler_params=_params(1),
    )(o, ub, x1, target, *consts)


def _fb_bwd(dqn, dqr, dkn, dkr, dv, dgate, ub, ckr, x1, dx2, w, cos_t, sin_t, seq, tb):
    t_all, d = x1.shape
    hv = dqn.shape[1]
    nheads = hv // LANE
    qr = ub.shape[1] - hv
    kvr = ckr.shape[1] - LANE
    npos = seq // tb

    def body(dqn_ref, dqr_ref, dkn_ref, dkr_ref, dv_ref, dg_ref, ub_ref, ckr_ref, x1_ref, dx2_ref,
             wuqnt, wuqrt, qn, winbt, nb, wukt, wuvt, kvn, wdkvt, nkv, cos_ref, sin_ref,
             dx1_ref, dqrp_ref, dub_ref, dckr_ref, dqn_g, dnb_g, dkvn_g, dnkv_g):
        @pl.when(pl.program_id(0) == 0)
        def _():
            dqn_g[...] = jnp.zeros((1, qr), F32)
            dnb_g[...] = jnp.zeros((1, d), F32)
            dkvn_g[...] = jnp.zeros((1, kvr), F32)
            dnkv_g[...] = jnp.zeros((1, d), F32)

        cos, sin = cos_ref[...], sin_ref[...]
        xv = x1_ref[...]
        rinv1 = _rinv(xv)
        dqr_v = dqr_ref[...]
        dqr_pre = (dqr_v * jnp.tile(cos, (1, nheads)) + _swap_halves(dqr_v * jnp.tile(sin, (1, nheads)))).astype(BF16)
        dqrp_ref[...] = dqr_pre
        dcq = _dot(dqn_ref[...].astype(BF16), wuqnt[...]) + _dot(dqr_pre, wuqrt[...])
        cq_pre = ub_ref[:, :qr]
        dcq_pre, g1 = _rms_bwd(cq_pre, _rinv(cq_pre), qn[...], dcq)
        dqn_g[...] += g1
        dub = jnp.concatenate([dcq_pre, dg_ref[...]], axis=1).astype(BF16)
        dub_ref[...] = dub
        dx1_b, g2 = _rms_bwd(xv, rinv1, nb[...], _dot(dub, winbt[...]))
        dnb_g[...] += g2
        dkr_all = dkr_ref[...]
        dkr_sum = dkr_all[:, :LANE]
        for n in range(1, nheads):
            dkr_sum = dkr_sum + dkr_all[:, n * LANE:(n + 1) * LANE]
        dckr_rope = dkr_sum * cos + _swap_halves(dkr_sum * sin)
        dckv = _dot(dkn_ref[...].astype(BF16), wukt[...]) + _dot(dv_ref[...].astype(BF16), wuvt[...])
        ckv_pre = ckr_ref[:, :kvr]
        dckv_pre, g3 = _rms_bwd(ckv_pre, _rinv(ckv_pre), kvn[...], dckv)
        dkvn_g[...] += g3
        dckr = jnp.concatenate([dckv_pre, dckr_rope], axis=1).astype(BF16)
        dckr_ref[...] = dckr
        dx1_kv, g4 = _rms_bwd(xv, rinv1, nkv[...], _dot(dckr, wdkvt[...]))
        dnkv_g[...] += g4
        dx1_ref[...] = dx2_ref[...] + dx1_b + dx1_kv

    tok = lambda c: pl.BlockSpec((tb, c), lambda i: (i, 0))
    pos = pl.BlockSpec((tb, LANE), lambda i: (i % npos, 0))
    consts = [w["w_uq_n_t"], w["w_uq_r_t"], w["q_norm"], w["w_in_b_t"], w["norm_b"], w["w_uk_t"], w["w_uv_t"], w["kv_norm"], w["w_dkv_p_t"], w["norm_kv"]]
    return pl.pallas_call(
        body, name="fb_bwd", grid=(t_all // tb,),
        in_specs=[tok(hv)] * 6 + [tok(qr + hv), tok(kvr + LANE), tok(d), tok(d)] + [_full(c.shape) for c in consts] + [pos, pos],
        out_specs=[tok(d), tok(hv), tok(qr + hv), tok(kvr + LANE), _full((1, qr)), _full((1, d)), _full((1, kvr)), _full((1, d))],
        out_shape=[_sds((t_all, d)), _sds((t_all, hv), BF16), _sds((t_all, qr + hv), BF16), _sds((t_all, kvr + LANE), BF16),
                   _sds((1, qr)), _sds((1, d)), _sds((1, kvr)), _sds((1, d))],
        compiler_params=_params(1),
    )(dqn, dqr, dkn, dkr, dv, dgate, ub, ckr, x1, dx2, *consts, cos_t, sin_t)


def _fa_bwd(dx1, x, u, hs, w, seq, tb):
    t_all, d = x.shape
    dr = hs.shape[1]
    nblocks = w["w_rg"].shape[0]
    nblk = seq // tb
    nt = tb // SUBLANE
    per8 = tb // 8

    def body(dx1_ref, x_ref, u_ref, uh_ref, hs_ref, hh_ref, na, woutt, cw, cb, wrg, brg, wig, big, wrgt, wigt, lam, wint,
             gx_ref, du_ref, xb_ref, drg_ref, dig_ref, dna_g, dcw_g, dcb_g, dbrg_g, dbig_g, dlam_g,
             xpad, hpad, a_s, d_s, g_s, dxpad, carry):
        b, jj = pl.program_id(0), pl.program_id(1)
        first_block = jj == nblk - 1

        @pl.when((b == 0) & (jj == 0))
        def _():
            dna_g[...] = jnp.zeros((1, d), F32)
            dcw_g[...] = jnp.zeros((4, dr), F32)
            dcb_g[...] = jnp.zeros((1, dr), F32)
            dbrg_g[...] = jnp.zeros((1, dr), F32)
            dbig_g[...] = jnp.zeros((1, dr), F32)
            dlam_g[...] = jnp.zeros((1, dr), F32)

        @pl.when(jj == 0)
        def _():
            dxpad[pl.ds(tb, 8), :] = jnp.zeros((8, dr), F32)
            carry[...] = jnp.zeros((8, dr), F32)

        keep = jnp.where(first_block, 0.0, 1.0)
        dx1v = dx1_ref[...]
        gate = u_ref[:, dr:]
        xpad[pl.ds(0, 8), :] = uh_ref[...] * keep
        xpad[pl.ds(8, tb), :] = u_ref[:, :dr]
        hpad[pl.ds(0, 8), :] = hh_ref[...] * keep
        hpad[pl.ds(8, tb), :] = hs_ref[...]
        xb = _conv(xpad, cw, cb[...], tb)
        xb_ref[...] = xb.astype(BF16)
        r, i = _gates(xb, wrg, brg[...], wig, big[...], nblocks)
        sp = _softplus(-lam[...])
        log_a = -LRU_C * r * sp
        a = jnp.exp(log_a)
        nem = _neg_expm1(2.0 * log_a)
        mult = jnp.sqrt(nem)
        sg = jax.nn.sigmoid(gate)
        dy = _dot(dx1v.astype(BF16), woutt[...])
        hsv = hs_ref[...]
        dgate = dy * hsv * (sg * (1.0 + gate * (1.0 - sg)))
        a_s[...] = a
        d_s[...] = dy * (gate * sg)
        row = lax.broadcasted_iota(jnp.int32, (8, dr), 0)

        def step(k, c):
            r0 = pl.multiple_of((nt - 1 - k) * 8, 8)
            av = a_s[pl.ds(r0, 8), :]
            dv = d_s[pl.ds(r0, 8), :]
            qv = av * dv
            for s in (1, 2, 4):
                m = row < 8 - s
                a_sh = jnp.where(m, pltpu.roll(av, 8 - s, 0), 1.0)
                q_sh = jnp.where(m, pltpu.roll(qv, 8 - s, 0), 0.0)
                qv = qv + av * q_sh
                av = av * a_sh
            qv = qv + av * c
            g_s[pl.ds(r0, 8), :] = dv + jnp.where(row < 7, pltpu.roll(qv, 7, 0), c)
            return jnp.broadcast_to(qv[0:1, :], qv.shape)

        carry[...] = lax.fori_loop(0, nt, step, carry[...])
        g = g_s[...]
        ix = i * xb
        dlog_a = g * hpad[pl.ds(7, tb), :] * a - g * ix * ((1.0 - nem) / mult)
        dix = g * mult
        dlam_g[...] += -jax.nn.sigmoid(-lam[...]) * jnp.sum(dlog_a * (-LRU_C * r), axis=0, keepdims=True)
        drg = dlog_a * (-LRU_C * sp) * r * (1.0 - r)
        dig = dix * xb * i * (1.0 - i)
        dbrg_g[...] += jnp.sum(drg, axis=0, keepdims=True)
        dbig_g[...] += jnp.sum(dig, axis=0, keepdims=True)
        drgb, digb = drg.astype(BF16), dig.astype(BF16)
        drg_ref[...] = drgb
        dig_ref[...] = digb
        back = [_dot(drgb[:, n * LANE:(n + 1) * LANE], wrgt[n]) + _dot(digb[:, n * LANE:(n + 1) * LANE], wigt[n]) for n in range(nblocks)]
        dxb = dix * i + jnp.concatenate(back, axis=1)
        dcb_g[...] += jnp.sum(dxb, axis=0, keepdims=True)
        for k in range(4):
            dcw_g[k:k + 1, :] += jnp.sum(dxb * xpad[pl.ds(5 + k, tb), :], axis=0, keepdims=True)
        dxpad[pl.ds(0, tb), :] = dxb
        dxpre = (cw[3:4, :] * dxb + cw[2:3, :] * dxpad[pl.ds(1, tb), :] + cw[1:2, :] * dxpad[pl.ds(2, tb), :]
                 + cw[0:1, :] * dxpad[pl.ds(3, tb), :])
        dxpad[pl.ds(tb, 8), :] = dxb[:8, :]
        du = jnp.concatenate([dxpre, dgate], axis=1).astype(BF16)
        du_ref[...] = du
        xv = x_ref[...]
        dxa, g1 = _rms_bwd(xv, _rinv(xv), na[...], _dot(du, wint[...]))
        dna_g[...] += g1
        gx_ref[...] = dx1v + dxa

    blk = lambda b, j: b * nblk + (nblk - 1 - j)
    tok = lambda c: pl.BlockSpec((tb, c), lambda b, j: (blk(b, j), 0))
    halo = pl.BlockSpec((8, dr), lambda b, j: (jnp.maximum(blk(b, j) * per8 - 1, 0), 0))
    consts = [w["norm_a"], w["w_out_a_t"], w["conv_w"], w["conv_b"], w["w_rg"], w["b_rg"], w["w_ig"], w["b_ig"], w["w_rg_t"], w["w_ig_t"],
              w["lru_lambda"], w["w_in_a_t"]]
    vec = lambda c: _full((1, c))
    return pl.pallas_call(
        body, name="fa_bwd", grid=(t_all // seq, nblk),
        in_specs=[tok(d), tok(d), tok(2 * dr), halo, tok(dr), halo] + [_full(c.shape) for c in consts],
        out_specs=[tok(d), tok(2 * dr), tok(dr), tok(dr), tok(dr), vec(d), _full((4, dr)), vec(dr), vec(dr), vec(dr), vec(dr)],
        out_shape=[_sds((t_all, d)), _sds((t_all, 2 * dr), BF16), _sds((t_all, dr), BF16), _sds((t_all, dr), BF16), _sds((t_all, dr), BF16),
                   _sds((1, d)), _sds((4, dr)), _sds((1, dr)), _sds((1, dr)), _sds((1, dr)), _sds((1, dr))],
        scratch_shapes=[pltpu.VMEM((tb + 8, dr), F32), pltpu.VMEM((tb + 8, dr), F32), pltpu.VMEM((tb, dr), F32), pltpu.VMEM((tb, dr), F32),
                        pltpu.VMEM((tb, dr), F32), pltpu.VMEM((tb + 8, dr), F32), pltpu.VMEM((8, dr), F32)],
        compiler_params=_params(2),
    )(dx1, x, u, u, hs, hs, *consts)


def _mm_tn(a, b, name, bt):
    t_all, m = a.shape
    n = b.shape[1]
    bn = n if n <= 1536 else n // 2

    def body(a_ref, b_ref, o_ref):
        @pl.when(pl.program_id(1) == 0)
        def _():
            o_ref[...] = jnp.zeros((m, bn), F32)

        o_ref[...] += _dot_tn(a_ref[...].astype(BF16), b_ref[...].astype(BF16))

    return pl.pallas_call(
        body, name=name, grid=(n // bn, t_all // bt),
        in_specs=[pl.BlockSpec((bt, m), lambda j, t: (t, 0)), pl.BlockSpec((bt, bn), lambda j, t: (t, j))],
        out_specs=pl.BlockSpec((m, bn), lambda j, t: (0, j)),
        out_shape=_sds((m, n)),
        compiler_params=_params(2),
    )(a, b)


def _mm_tn_blocks(a, b, name, bt):
    t_all, c = a.shape
    nblocks = c // LANE

    def body(a_ref, b_ref, o_ref):
        @pl.when(pl.program_id(1) == 0)
        def _():
            o_ref[...] = jnp.zeros((1, LANE, LANE), F32)

        o_ref[0] += _dot_tn(a_ref[...], b_ref[...])

    spec = pl.BlockSpec((bt, LANE), lambda n, t: (t, n))
    return pl.pallas_call(
        body, name=name, grid=(nblocks, t_all // bt), in_specs=[spec, spec],
        out_specs=pl.BlockSpec((1, LANE, LANE), lambda n, t: (n, 0, 0)),
        out_shape=_sds((nblocks, LANE, LANE)),
        compiler_params=_params(2),
    )(a, b)


def _prepare(wf):
    t = lambda a: a.T
    w = {}
    for k in ("norm_a", "conv_b", "b_rg", "b_ig", "lru_lambda", "norm_b", "q_norm"):
        w[k] = wf[k].astype(F32)
    for k in ("norm_kv", "kv_norm", "final_norm"):
        w[k] = wf[k].astype(F32)[None, :]
    w["conv_w"] = wf["conv_w"][0].astype(F32)
    w["w_in_a"] = wf["w_in_a"][0].astype(BF16)
    w["w_out_a"] = wf["w_out_a"][0].astype(BF16)
    w["w_rg"] = wf["w_rg"][0].astype(BF16)
    w["w_ig"] = wf["w_ig"][0].astype(BF16)
    w["w_rg_t"] = jnp.swapaxes(w["w_rg"], 1, 2)
    w["w_ig_t"] = jnp.swapaxes(w["w_ig"], 1, 2)
    dkv = wf["w_dkv"].astype(BF16)
    w["w_dkv_p"] = jnp.pad(dkv, ((0, 0), (0, LANE - QK_ROPE)))
    kvr, nheads = wf["w_uk"].shape[0], wf["w_uk"].shape[1]
    w["w_uk"] = wf["w_uk"].astype(BF16).reshape(kvr, nheads * LANE)
    w["w_uv"] = wf["w_uv"].astype(BF16).reshape(kvr, nheads * LANE)
    w["w_in_b"] = wf["w_in_b"][0].astype(BF16)
    uq = wf["w_uq"][0].astype(BF16)
    qr = uq.shape[0]
    w["w_uq_n"] = uq[:, :, :QK_NOPE].reshape(qr, nheads * LANE)
    w["w_uq_r"] = jnp.pad(uq[:, :, QK_NOPE:], ((0, 0), (0, 0), (0, LANE - QK_ROPE))).reshape(qr, nheads * LANE)
    w["w_out_b"] = wf["w_out_b"][0].astype(BF16)
    for k in ("w_in_a", "w_out_a", "w_dkv_p", "w_uk", "w_uv", "w_in_b", "w_uq_n", "w_uq_r", "w_out_b"):
        w[k + "_t"] = t(w[k])
    return w


def _local_step(x, target, wf, tb_a=256, tb_b=256, ta=512, bt=512):
    nb, seq, d = x.shape
    t_all = nb * seq
    tb_a, tb_b, ta, bt = min(tb_a, seq), min(tb_b, seq), min(ta, seq), min(bt, t_all)
    w = _prepare(wf)
    cos_t, sin_t = _rope_tables(seq)
    x0 = x.reshape(t_all, d)
    x1, u, hs, h, y = _fa_fwd(x0, w, seq, tb_a)
    qn, qr, kn, kr, v, ub, ckr, hb, hk, cq, ckv = _fb_fwd(x1, w, cos_t, sin_t, seq, tb_b)
    o, lse = _attn_fwd(qn, qr, kn, kr, v, seq, ta)
    loss, d_final_norm, yb, dx2, do, dgate, delta = _head(o, ub, x1, target.reshape(t_all, d), w, tb_b)
    dqn, dqr, dkn, dkr, dv = _attn_bwd(qn, qr, kn, kr, v, do, lse, delta, seq, ta)
    dx1, dqr_pre, dub, dckr, d_q_norm, d_norm_b, d_kv_norm, d_norm_kv = _fb_bwd(
        dqn, dqr, dkn, dkr, dv, dgate, ub, ckr, x1, dx2, w, cos_t, sin_t, seq, tb_b)
    gx, du, xb, drg, dig, d_norm_a, d_conv_w, d_conv_b, d_b_rg, d_b_ig, d_lam = _fa_bwd(dx1, x0, u, hs, w, seq, tb_a)

    nheads = w["w_uk"].shape[1] // LANE
    qrank, kvr = cq.shape[1], ckv.shape[1]
    g_uq_n = _mm_tn(cq, dqn, "dw_uq_n", bt).reshape(qrank, nheads, LANE)
    g_uq_r = _mm_tn(cq, dqr_pre, "dw_uq_r", bt).reshape(qrank, nheads, LANE)[:, :, :QK_ROPE]
    grads = {
        "norm_a": d_norm_a, "w_in_a": _mm_tn(h, du, "dw_in_a", bt)[None], "conv_w": d_conv_w[None], "conv_b": d_conv_b,
        "w_rg": _mm_tn_blocks(xb, drg, "dw_rg", bt)[None], "b_rg": d_b_rg, "w_ig": _mm_tn_blocks(xb, dig, "dw_ig", bt)[None], "b_ig": d_b_ig,
        "lru_lambda": d_lam, "w_out_a": _mm_tn(y, dx1, "dw_out_a", bt)[None], "norm_kv": d_norm_kv[0],
        "w_dkv": _mm_tn(hk, dckr, "dw_dkv", bt)[:, :kvr + QK_ROPE], "kv_norm": d_kv_norm[0],
        "w_uk": _mm_tn(ckv, dkn, "dw_uk", bt).reshape(kvr, nheads, LANE), "w_uv": _mm_tn(ckv, dv, "dw_uv", bt).reshape(kvr, nheads, LANE),
        "norm_b": d_norm_b, "w_in_b": _mm_tn(hb, dub, "dw_in_b", bt)[None], "q_norm": d_q_norm,
        "w_uq": jnp.concatenate([g_uq_n, g_uq_r], axis=2)[None], "w_out_b": _mm_tn(yb, dx2, "dw_out_b", bt)[None],
        "final_norm": d_final_norm[0],
    }
    return loss[0, 0], gx.reshape(nb, seq, d), grads


ANY = pl.BlockSpec(memory_space=pl.ANY)


def _place():
    return lax.axis_index("x"), lax.axis_index("y"), lax.axis_index("c")


def _all_gather8(block, name):
    def body(x_ref, out_ref, send_sems, recv_sems, local_sem):
        x, y, c = _place()
        me, sibling = (x, y, c), (x, y, 1 - c)
        chips = [(1 - x, y), (x, 1 - y), (1 - x, 1 - y)]

        def slot(px, py, pc):
            return out_ref.at[4 * px + 2 * py + pc]

        def copy(k, blk, to, src=None):
            return pltpu.make_async_remote_copy(
                src_ref=slot(*blk) if src is None else src, dst_ref=slot(*blk), send_sem=send_sems.at[k], recv_sem=recv_sems.at[k],
                device_id=to, device_id_type=MESH)

        mine = pltpu.make_async_copy(x_ref, slot(*me), local_sem)
        mine.start()
        first = [copy(0, me, sibling, src=x_ref)] + [copy(1 + j, me, (*chip, c), src=x_ref) for j, chip in enumerate(chips)]
        for cp in first:
            cp.start()
        passed = [copy(4 + j, (*chip, c), sibling) for j, chip in enumerate(chips)]
        for j, chip in enumerate(chips):
            copy(1 + j, (*chip, c), me).wait_recv()
            passed[j].start()
        copy(0, sibling, me).wait_recv()
        for j, chip in enumerate(chips):
            copy(4 + j, (*chip, 1 - c), me).wait_recv()
        for cp in first + passed:
            cp.wait_send()
        mine.wait()

    return pl.pallas_call(
        body, name=name, out_shape=_sds((8,) + block.shape, block.dtype), in_specs=[ANY], out_specs=ANY,
        scratch_shapes=[pltpu.SemaphoreType.DMA((7,)), pltpu.SemaphoreType.DMA((7,)), pltpu.SemaphoreType.DMA],
    )(block)


def _swap_sibling(src, name):
    def body(src_ref, out_ref, send_sem, recv_sem):
        x, y, c = _place()
        cp = pltpu.make_async_remote_copy(src_ref=src_ref, dst_ref=out_ref, send_sem=send_sem, recv_sem=recv_sem,
                                          device_id=(x, y, 1 - c), device_id_type=MESH)
        cp.start()
        cp.wait()

    return pl.pallas_call(
        body, name=name, out_shape=_sds(src.shape, src.dtype), in_specs=[ANY], out_specs=ANY,
        scratch_shapes=[pltpu.SemaphoreType.DMA, pltpu.SemaphoreType.DMA],
    )(src)


def _scatter_chips(src, name):
    def body(src_ref, out_ref, send_sems, recv_sems):
        x, y, c = _place()
        chips = [(1 - x, y), (x, 1 - y), (1 - x, 1 - y)]
        copies = [pltpu.make_async_remote_copy(src_ref=src_ref.at[2 * px + py], dst_ref=out_ref.at[k], send_sem=send_sems.at[k],
                                               recv_sem=recv_sems.at[k], device_id=(px, py, c), device_id_type=MESH)
                  for k, (px, py) in enumerate(chips)]
        for cp in copies:
            cp.start()
        for cp in copies:
            cp.wait()

    return pl.pallas_call(
        body, name=name, out_shape=_sds((3,) + src.shape[1:], src.dtype), in_specs=[ANY], out_specs=ANY,
        scratch_shapes=[pltpu.SemaphoreType.DMA((3,)), pltpu.SemaphoreType.DMA((3,))],
    )(src)


def _row_block(rows, cap=512):
    best = SUBLANE
    for r in range(SUBLANE, min(rows, cap) + 1, SUBLANE):
        if rows % r == 0:
            best = r
    return best


def _rowwise(fn, name, ins, out_dtypes):
    rows, cols = ins[0].shape
    rb = _row_block(rows)
    n_in = len(ins)

    def body(*refs):
        outs = fn(*[r[...] for r in refs[:n_in]])
        for r, o in zip(refs[n_in:], outs):
            r[...] = o.astype(r.dtype)

    spec = pl.BlockSpec((rb, cols), lambda i: (i, 0))
    return pl.pallas_call(
        body, name=name, grid=(rows // rb,), in_specs=[spec] * n_in, out_specs=[spec] * len(out_dtypes),
        out_shape=[_sds((rows, cols), dt) for dt in out_dtypes], compiler_params=_params(1),
    )(*ins)


def _adamw(w, g, m, v):
    m = ADAM_B1 * m + (1.0 - ADAM_B1) * g
    v = ADAM_B2 * v + (1.0 - ADAM_B2) * (g * g)
    m_hat = m / (1.0 - ADAM_B1 ** ADAM_STEP)
    v_hat = v / (1.0 - ADAM_B2 ** ADAM_STEP)
    return -ADAM_LR * (m_hat / (jnp.sqrt(v_hat) + ADAM_EPS) + ADAM_WD * w), m, v


def _round_up(n, k):
    return -(-n // k) * k


def _flat_pad(parts, total):
    flat = jnp.concatenate([p.reshape(-1) for p in parts])
    return jnp.pad(flat, (0, total - flat.shape[0]))


def _gather_weights(shards):
    x, y, c = _place()
    small = jnp.concatenate([shards[k].reshape(-1) for k in SMALL])
    as16 = lax.bitcast_convert_type(small, BF16).reshape(-1)
    n16 = sum(shards[k].size for k in BIG) + as16.shape[0]
    half = _round_up(-(-n16 // 2), PACK_ALIGN)
    packed = _flat_pad([shards[k].astype(BF16) for k in BIG] + [as16], 2 * half).reshape(2, half // PACK_COLS, PACK_COLS)
    mine = lax.dynamic_index_in_dim(packed, c, axis=0, keepdims=False)
    every = _all_gather8(mine, "ag_weights").reshape(4, 2 * half)
    full, off = {}, 0

    def join(name, per_chip):
        return jnp.concatenate([per_chip[p] for p in range(4)], axis=SHARD_AXIS[name])

    for k in BIG:
        n = shards[k].size
        full[k] = join(k, every[:, off:off + n].reshape((4,) + shards[k].shape))
        off += n
    for k in SMALL:
        n = shards[k].size
        bits = every[:, off:off + 2 * n].reshape(4, n, 2)
        full[k] = join(k, lax.bitcast_convert_type(bits, F32).reshape((4,) + shards[k].shape))
        off += 2 * n
    return full


def _shard_sizes(shards, repl):
    n_sh = sum(shards[k].size for k, _ in SHARDED)
    n_rep = sum(repl[k].size for k in REPL)
    return _round_up(-(-n_sh // 2), PACK_ALIGN), _round_up(-(-n_rep // 8), PACK_ALIGN)


def _pack_local(sh, rep, lh, lr):
    flat = jnp.concatenate([_flat_pad([sh[k] for k, _ in SHARDED], 2 * lh), _flat_pad([rep[k] for k in REPL], 8 * lr)])
    return flat.reshape(-1, PACK_COLS)


def _unpack_local(buf, shards, repl, lh):
    flat = buf.reshape(-1)
    out, off = {}, 0
    for k, _ in SHARDED:
        out[k] = flat[off:off + shards[k].size].reshape(shards[k].shape)
        off += shards[k].size
    off = 2 * lh
    for k in REPL:
        out[k] = flat[off:off + repl[k].size].reshape(repl[k].shape)
        off += repl[k].size
    return out


def _reduce_grads(grads, shards, repl, lh, lr):
    x, y, c = _place()
    rows_h, rows_r = lh // PACK_COLS, lr // PACK_COLS
    per_chip = []
    for k, ax in SHARDED:
        g = grads[k]
        n = g.shape[ax] // 4
        per_chip.append(jnp.stack([lax.slice_in_dim(g, p * n, (p + 1) * n, axis=ax).reshape(-1) for p in range(4)]))
    sharded = jnp.concatenate(per_chip, axis=1)
    sharded = jnp.pad(sharded, ((0, 0), (0, 2 * lh - sharded.shape[1]))).reshape(4, 2, rows_h, PACK_COLS)
    rep = _flat_pad([grads[k] for k in REPL], 8 * lr).reshape(4, 2, rows_r, PACK_COLS)
    blocks = jnp.concatenate([sharded, rep], axis=2)
    rows = rows_h + rows_r
    keep = lax.dynamic_index_in_dim(blocks, c, axis=1, keepdims=False)
    give = lax.dynamic_index_in_dim(blocks, 1 - c, axis=1, keepdims=False)
    got = _swap_sibling(give, "rs_sibling")
    chip_f32, chip_b16 = _rowwise(lambda a, b: (a + b, a + b), "rs_sum_sibling",
                                  [keep.reshape(4 * rows, PACK_COLS), got.reshape(4 * rows, PACK_COLS)], [F32, BF16])
    others = _scatter_chips(chip_b16.reshape(4, rows, PACK_COLS), "rs_chips")
    own = lax.dynamic_index_in_dim(chip_f32.reshape(4, rows, PACK_COLS), 2 * x + y, axis=0, keepdims=False)
    (mine,) = _rowwise(lambda a, b0, b1, b2: (((a + b0.astype(F32)) + b1.astype(F32)) + b2.astype(F32),), "rs_sum_chips",
                       [own, others[0], others[1], others[2]], [F32])
    theirs = _swap_sibling(mine, "rs_return")
    half0 = jnp.where(c == 0, mine, theirs)
    half1 = jnp.where(c == 0, theirs, mine)
    rep_all = _all_gather8(mine[rows_h:], "ag_rep")
    return jnp.concatenate([half0[:rows_h], half1[:rows_h], rep_all.reshape(8 * rows_r, PACK_COLS)], axis=0)


def kernel(x, norm_a, w_in_a, conv_w, conv_b, w_rg, b_rg, w_ig, b_ig, lru_lambda, w_out_a, norm_kv, w_dkv, kv_norm, w_uk, w_uv, norm_b, w_in_b, q_norm, w_uq, w_out_b, final_norm, loss_target, m_norm_a, m_w_in_a, m_conv_w, m_conv_b, m_w_rg, m_b_rg, m_w_ig, m_b_ig, m_lru_lambda, m_w_out_a, m_norm_kv, m_w_dkv, m_kv_norm, m_w_uk, m_w_uv, m_norm_b, m_w_in_b, m_q_norm, m_w_uq, m_w_out_b, m_final_norm, v_norm_a, v_w_in_a, v_conv_w, v_conv_b, v_w_rg, v_b_rg, v_w_ig, v_b_ig, v_lru_lambda, v_w_out_a, v_norm_kv, v_w_dkv, v_kv_norm, v_w_uk, v_w_uv, v_norm_b, v_w_in_b, v_q_norm, v_w_uq, v_w_out_b, v_final_norm):
    given = dict(locals())
    wts = {k: given[k] for k in WEIGHTS}
    shards = {k: wts[k] for k, _ in SHARDED}
    repl = {k: wts[k] for k in REPL}
    full = dict(repl)
    full.update(_gather_weights(shards))
    loss, grad_x, grads = _local_step(x, loss_target, full)
    loss = lax.psum(loss, ("x", "y", "c"))

    lh, lr = _shard_sizes(shards, repl)
    g_buf = _reduce_grads(grads, shards, repl, lh, lr)
    w_buf = _pack_local(shards, repl, lh, lr)
    m_buf = _pack_local({k: given["m_" + k] for k in shards}, {k: given["m_" + k] for k in repl}, lh, lr)
    v_buf = _pack_local({k: given["v_" + k] for k in shards}, {k: given["v_" + k] for k in repl}, lh, lr)
    d_buf, m_new, v_new = _rowwise(_adamw, "adamw", [w_buf, g_buf, m_buf, v_buf], [F32, F32, F32])
    outs = [_unpack_local(b, shards, repl, lh) for b in (g_buf, d_buf, m_new, v_new)]
    return (loss, grad_x, *[o[k] for o in outs for k in WEIGHTS])
```

```python
import jax
import jax.numpy as jnp
from jax import lax
from jax.experimental import pallas as pl
from jax.experimental.pallas import tpu as pltpu

F32, BF16 = jnp.float32, jnp.bfloat16
EPS = 1e-6
LRU_C = 8.0
ROPE_THETA = 10000.0
QK_NOPE, QK_ROPE = 128, 64
ATTN_SCALE = (QK_NOPE + QK_ROPE) ** -0.5
LANE = 128
SUBLANE = 8
ROW_ALIGN = 32
VMEM_LIMIT = 60000 * 1024
ADAM_LR, ADAM_B1, ADAM_B2, ADAM_EPS, ADAM_WD, ADAM_STEP = 0.001, 0.9, 0.999, 1e-08, 0.01, 10
MESH = pl.DeviceIdType.MESH
ANY = pl.BlockSpec(memory_space=pl.ANY)
N_CHIPS = 4
TOKENS_A, TOKENS_B, TOKENS_ATTN, TOKENS_MM = 256, 256, 512, 512

SMALL = ("norm_a", "conv_w", "conv_b", "b_rg", "b_ig", "lru_lambda")
REPL = ("w_rg", "w_ig", "norm_kv", "kv_norm", "norm_b", "q_norm", "final_norm")
WEIGHTS = ("norm_a", "w_in_a", "conv_w", "conv_b", "w_rg", "b_rg", "w_ig", "b_ig", "lru_lambda", "w_out_a", "norm_kv",
           "w_dkv", "kv_norm", "w_uk", "w_uv", "norm_b", "w_in_b", "q_norm", "w_uq", "w_out_b", "final_norm")
W_ORDER = ("in_b", "in_a", "out_a", "out_b", "uk", "uv", "uq_n", "uq_r", "dkv")
G_ORDER = ("in_b", "rest", "in_a", "out_a", "out_b", "uk", "uv", "uq_n", "uq_r")


def _sds(shape, dtype=F32):
    return jax.ShapeDtypeStruct(tuple(shape), dtype)


def _params(n_grid):
    return pltpu.CompilerParams(dimension_semantics=("arbitrary",) * n_grid, vmem_limit_bytes=VMEM_LIMIT)


def _full(shape):
    nd = len(shape)
    return pl.BlockSpec(tuple(shape), lambda *g: (0,) * nd)


def _round_up(n, k):
    return -(-n // k) * k


def _row_block(rows, cap=512):
    best = SUBLANE
    for r in range(SUBLANE, min(rows, cap) + 1, SUBLANE):
        if rows % r == 0:
            best = r
    return best


def _place():
    return lax.axis_index("x"), lax.axis_index("y"), lax.axis_index("c")


class _Layout:
    def __init__(self, d, dr, qr, kvr, hv, n_small, n_repl):
        assert hv == d, "the packed rows are D_MODEL wide, which must equal heads * 128"
        self.d, self.dr, self.qr, self.kvr, self.hv = d, dr, qr, kvr, hv
        per_chip = {"in_b": (qr + hv) // N_CHIPS, "in_a": 2 * dr // N_CHIPS, "out_a": dr // N_CHIPS, "out_b": hv // N_CHIPS,
                    "uk": kvr // N_CHIPS, "uv": kvr // N_CHIPS, "uq_n": qr // N_CHIPS, "uq_r": qr // N_CHIPS,
                    "dkv": (d // N_CHIPS) * (kvr + LANE) // d}
        assert all(r % ROW_ALIGN == 0 for r in per_chip.values()), per_chip
        self.small_rows = _round_up(-(-n_small // d), SUBLANE)
        self.repl_rows = _round_up(-(-n_repl // (N_CHIPS * d)), SUBLANE)
        rest = _round_up(per_chip["dkv"] + self.small_rows + self.repl_rows, ROW_ALIGN)
        while (N_CHIPS * (per_chip["in_b"] + rest)) % min(512, N_CHIPS * per_chip["in_a"]) != 0:
            rest += ROW_ALIGN
        per_chip["rest"] = rest
        self.rows = per_chip
        self.w_off, off = {}, 0
        for k in W_ORDER:
            self.w_off[k] = off
            off += per_chip[k]
        self.w_rows = _round_up(off, ROW_ALIGN)
        self.g_off, self.c_off, off = {}, {}, 0
        for k in G_ORDER:
            self.c_off[k] = off
            self.g_off[k] = N_CHIPS * off
            off += per_chip[k]
        self.c_rows = off
        self.g_rows = N_CHIPS * off


def _dot(a, b):
    return jnp.dot(a, b, preferred_element_type=F32)


def _dot_nt(a, b):
    return lax.dot_general(a, b, (((1,), (1,)), ((), ())), preferred_element_type=F32)


def _dot_tn(a, b):
    return lax.dot_general(a, b, (((0,), (0,)), ((), ())), preferred_element_type=F32)


def _rinv(x):
    return lax.rsqrt(jnp.mean(x * x, axis=-1, keepdims=True) + EPS)


def _rms_bwd(x, rinv, g, dy):
    z = dy * g
    dx = rinv * z - x * (rinv * rinv * rinv) * jnp.mean(z * x, axis=-1, keepdims=True)
    dg = jnp.sum(dy * (x * rinv), axis=0, keepdims=True)
    return dx, dg


def _softplus(z):
    return jnp.maximum(z, 0.0) + jnp.log1p(jnp.exp(-jnp.abs(z)))


def _neg_expm1(z):
    series = -z * (1 + z / 2 * (1 + z / 3 * (1 + z / 4 * (1 + z / 5 * (1 + z / 6)))))
    return jnp.where(z > -0.3, series, 1.0 - jnp.exp(z))


def _swap_halves(x):
    w = x.shape[1]
    lane = lax.broadcasted_iota(jnp.int32, x.shape, 1)
    return jnp.where(lane % QK_ROPE < QK_ROPE // 2, pltpu.roll(x, w - QK_ROPE // 2, 1), pltpu.roll(x, QK_ROPE // 2, 1))


def _rope_tables(seq):
    pos = jnp.arange(seq, dtype=F32)
    inv = ROPE_THETA ** (-jnp.arange(0, QK_ROPE, 2, dtype=F32) / QK_ROPE)
    ang = pos[:, None] * inv[None, :]
    cos, sin = jnp.cos(ang), jnp.sin(ang)
    zero = jnp.zeros((seq, LANE - QK_ROPE), F32)
    return jnp.concatenate([cos, cos, zero], 1), jnp.concatenate([-sin, sin, zero], 1)


def _fetch(wg_ref, lay, key, dst, sems, k0):
    rows = lay.rows[key]
    return [pltpu.make_async_copy(wg_ref.at[p, pl.ds(lay.w_off[key], rows), :], dst.at[pl.ds(p * rows, rows), :], sems.at[k0 + p])
            for p in range(N_CHIPS)]


def _gates(xb, wrg_ref, brg, wig_ref, big, nblocks):
    xbb = xb.astype(BF16)
    rg = [_dot(xbb[:, n * LANE:(n + 1) * LANE], wrg_ref[n]) for n in range(nblocks)]
    ig = [_dot(xbb[:, n * LANE:(n + 1) * LANE], wig_ref[n]) for n in range(nblocks)]
    r = jax.nn.sigmoid(jnp.concatenate(rg, axis=1) + brg)
    i = jax.nn.sigmoid(jnp.concatenate(ig, axis=1) + big)
    return r, i


def _conv(xpad, cw_ref, cb, tb):
    return (cb + cw_ref[3:4, :] * xpad[pl.ds(8, tb), :] + cw_ref[2:3, :] * xpad[pl.ds(7, tb), :]
            + cw_ref[1:2, :] * xpad[pl.ds(6, tb), :] + cw_ref[0:1, :] * xpad[pl.ds(5, tb), :])


def _fa_fwd(x, wg, w, lay, seq, tb):
    t_all, d = x.shape
    dr = lay.dr
    nblocks = w["w_rg"].shape[0]
    nblk = seq // tb
    nt = tb // SUBLANE

    def body(x_ref, wg_ref, na, cw, cb, wrg, brg, wig, big, lam, x1_ref, u_ref, hs_ref, h_ref, y_ref,
             wint, wout, xpad, a_s, b_s, carry, sems):
        @pl.when((pl.program_id(0) == 0) & (pl.program_id(1) == 0))
        def _():
            cps = _fetch(wg_ref, lay, "in_a", wint, sems, 0) + _fetch(wg_ref, lay, "out_a", wout, sems, N_CHIPS)
            for cp in cps:
                cp.start()
            for cp in cps:
                cp.wait()

        @pl.when(pl.program_id(1) == 0)
        def _():
            xpad[pl.ds(0, 8), :] = jnp.zeros((8, dr), F32)
            carry[...] = jnp.zeros((8, dr), F32)

        xv = x_ref[...]
        h = (xv * _rinv(xv) * na[...]).astype(BF16)
        h_ref[...] = h
        u = _dot_nt(h, wint[...])
        u_ref[...] = u
        xpre, gate = u[:, :dr], u[:, dr:]
        xpad[pl.ds(8, tb), :] = xpre
        xb = _conv(xpad, cw, cb[...], tb)
        xpad[pl.ds(0, 8), :] = xpre[tb - 8:, :]
        r, i = _gates(xb, wrg, brg[...], wig, big[...], nblocks)
        log_a = -LRU_C * r * _softplus(-lam[...])
        a_s[...] = jnp.exp(log_a)
        b_s[...] = jnp.sqrt(_neg_expm1(2.0 * log_a)) * (i * xb)
        row = lax.broadcasted_iota(jnp.int32, (8, dr), 0)

        def step(t, c):
            r0 = pl.multiple_of(t * 8, 8)
            a = a_s[pl.ds(r0, 8), :]
            b = b_s[pl.ds(r0, 8), :]
            for s in (1, 2, 4):
                m = row >= s
                a_sh = jnp.where(m, pltpu.roll(a, s, 0), 1.0)
                b_sh = jnp.where(m, pltpu.roll(b, s, 0), 0.0)
                b = a * b_sh + b
                a = a * a_sh
            hh = b + a * c
            hs_ref[pl.ds(r0, 8), :] = hh
            return jnp.broadcast_to(hh[7:8, :], hh.shape)

        carry[...] = lax.fori_loop(0, nt, step, carry[...])
        y = (hs_ref[...] * (gate * jax.nn.sigmoid(gate))).astype(BF16)
        y_ref[...] = y
        x1_ref[...] = xv + _dot(y, wout[...])

    tok = lambda c: pl.BlockSpec((tb, c), lambda b, j: (b * nblk + j, 0))
    consts = [w["norm_a"], w["conv_w"], w["conv_b"], w["w_rg"], w["b_rg"], w["w_ig"], w["b_ig"], w["lru_lambda"]]
    return pl.pallas_call(
        body, name="fa_fwd", grid=(t_all // seq, nblk),
        in_specs=[tok(d), ANY] + [_full(c.shape) for c in consts],
        out_specs=[tok(d), tok(2 * dr), tok(dr), tok(d), tok(dr)],
        out_shape=[_sds((t_all, d)), _sds((t_all, 2 * dr)), _sds((t_all, dr)), _sds((t_all, d), BF16), _sds((t_all, dr), BF16)],
        scratch_shapes=[pltpu.VMEM((2 * dr, d), BF16), pltpu.VMEM((dr, d), BF16), pltpu.VMEM((tb + 8, dr), F32), pltpu.VMEM((tb, dr), F32),
                        pltpu.VMEM((tb, dr), F32), pltpu.VMEM((8, dr), F32), pltpu.SemaphoreType.DMA((2 * N_CHIPS,))],
        compiler_params=_params(2),
    )(x, wg, *consts)


def _fb_fwd(x1, wg, w, lay, cos_t, sin_t, seq, tb):
    t_all, d = x1.shape
    kvr, qr, hv = lay.kvr, lay.qr, lay.hv
    nheads = hv // LANE
    npos = seq // tb

    def body(x_ref, wg_ref, nkv, nb, wdkv, kvn, qn, cos_ref, sin_ref,
             qn_o, qr_o, kn_o, kr_o, v_o, ub_o, ckr_o, hb_o, hk_o, cq_o, ckv_o, winb, wuk, wuv, wuqn, wuqr, sems):
        @pl.when(pl.program_id(0) == 0)
        def _():
            cps = []
            for n, (key, dst) in enumerate((("in_b", winb), ("uk", wuk), ("uv", wuv), ("uq_n", wuqn), ("uq_r", wuqr))):
                cps += _fetch(wg_ref, lay, key, dst, sems, n * N_CHIPS)
            for cp in cps:
                cp.start()
            for cp in cps:
                cp.wait()

        xv = x_ref[...]
        xh = xv * _rinv(xv)
        hk = (xh * nkv[...]).astype(BF16)
        hb = (xh * nb[...]).astype(BF16)
        hk_o[...] = hk
        hb_o[...] = hb
        cos, sin = cos_ref[...], sin_ref[...]
        ckr = _dot(hk, wdkv[...])
        ckr_o[...] = ckr
        ckv_pre = ckr[:, :kvr]
        ckv = (ckv_pre * _rinv(ckv_pre) * kvn[...]).astype(BF16)
        ckv_o[...] = ckv
        kr = ckr[:, kvr:]
        kr_o[...] = (kr * cos + _swap_halves(kr) * sin).astype(BF16)
        kn_o[...] = _dot(ckv, wuk[...]).astype(BF16)
        v_o[...] = _dot(ckv, wuv[...]).astype(BF16)
        ub = _dot_nt(hb, winb[...])
        ub_o[...] = ub
        cq_pre = ub[:, :qr]
        cq = (cq_pre * _rinv(cq_pre) * qn[...]).astype(BF16)
        cq_o[...] = cq
        qn_o[...] = _dot(cq, wuqn[...]).astype(BF16)
        qrope = _dot(cq, wuqr[...])
        qr_o[...] = (qrope * jnp.tile(cos, (1, nheads)) + _swap_halves(qrope) * jnp.tile(sin, (1, nheads))).astype(BF16)

    tok = lambda c: pl.BlockSpec((tb, c), lambda i: (i, 0))
    pos = pl.BlockSpec((tb, LANE), lambda i: (i % npos, 0))
    consts = [w["norm_kv"], w["norm_b"], w["w_dkv_p"], w["kv_norm"], w["q_norm"]]
    outs = [(hv, BF16), (hv, BF16), (hv, BF16), (LANE, BF16), (hv, BF16), (qr + hv, F32), (kvr + LANE, F32), (d, BF16), (d, BF16), (qr, BF16), (kvr, BF16)]
    return pl.pallas_call(
        body, name="fb_fwd", grid=(t_all // tb,),
        in_specs=[tok(d), ANY] + [_full(c.shape) for c in consts] + [pos, pos],
        out_specs=[tok(c) for c, _ in outs],
        out_shape=[_sds((t_all, c), dt) for c, dt in outs],
        scratch_shapes=[pltpu.VMEM((qr + hv, d), BF16), pltpu.VMEM((kvr, d), BF16), pltpu.VMEM((kvr, d), BF16), pltpu.VMEM((qr, d), BF16),
                        pltpu.VMEM((qr, d), BF16), pltpu.SemaphoreType.DMA((5 * N_CHIPS,))],
        compiler_params=_params(1),
    )(x1, wg, *consts, cos_t, sin_t)


def _causal_mask(i, j, ta):
    rows = i * ta + lax.broadcasted_iota(jnp.int32, (ta, ta), 0)
    cols = j * ta + lax.broadcasted_iota(jnp.int32, (ta, ta), 1)
    return cols <= rows


def _attn_fwd(qn, qr, kn, kr, v, seq, ta):
    t_all, hv = qn.shape
    nheads, nb, na = hv // LANE, t_all // seq, seq // ta

    def body(qn_ref, qr_ref, kn_ref, kr_ref, v_ref, o_ref, lse_ref, m_s, l_s, acc_s):
        i, j = pl.program_id(2), pl.program_id(3)

        @pl.when(j == 0)
        def _():
            m_s[...] = jnp.full((ta, LANE), -1e30, F32)
            l_s[...] = jnp.zeros((ta, LANE), F32)
            acc_s[...] = jnp.zeros((ta, LANE), F32)

        @pl.when(j <= i)
        def _():
            q = jnp.concatenate([qn_ref[...], qr_ref[...]], axis=1)
            k = jnp.concatenate([kn_ref[...], kr_ref[...]], axis=1)
            s = jnp.where(_causal_mask(i, j, ta), _dot_nt(q, k) * ATTN_SCALE, -1e30)
            m_prev = m_s[...]
            m_new = jnp.maximum(m_prev, jnp.max(s, axis=1, keepdims=True))
            p = jnp.exp(s - jnp.tile(m_new, (1, ta // LANE)))
            alpha = jnp.exp(m_prev - m_new)
            l_s[...] = alpha * l_s[...] + jnp.sum(p, axis=1, keepdims=True)
            acc_s[...] = alpha * acc_s[...] + _dot(p.astype(BF16), v_ref[...])
            m_s[...] = m_new

        @pl.when(j == na - 1)
        def _():
            o_ref[...] = acc_s[...] / l_s[...]
            lse_ref[...] = m_s[...] + jnp.log(l_s[...])

    qspec = pl.BlockSpec((ta, LANE), lambda b, h, i, j: (b * na + i, h))
    kspec = pl.BlockSpec((ta, LANE), lambda b, h, i, j: (b * na + jnp.minimum(i, j), h))
    krspec = pl.BlockSpec((ta, LANE), lambda b, h, i, j: (b * na + jnp.minimum(i, j), 0))
    return pl.pallas_call(
        body, name="attn_fwd", grid=(nb, nheads, na, na),
        in_specs=[qspec, qspec, kspec, krspec, kspec],
        out_specs=[qspec, qspec],
        out_shape=[_sds((t_all, hv)), _sds((t_all, hv))],
        scratch_shapes=[pltpu.VMEM((ta, LANE), F32)] * 3,
        compiler_params=_params(4),
    )(qn, qr, kn, kr, v)


def _attn_bwd(qn, qr, kn, kr, v, do, lse, delta, seq, ta):
    t_all, hv = qn.shape
    nheads, nb, na = hv // LANE, t_all // seq, seq // ta

    def body(qn_ref, qr_ref, kn_ref, kr_ref, v_ref, do_ref, lse_ref, dl_ref, dqn_ref, dqr_ref, dkn_ref, dkr_ref, dv_ref, dk_s, dv_s):
        j, i = pl.program_id(2), pl.program_id(3)

        @pl.when((j == 0) & (i == 0))
        def _():
            dqn_ref[...] = jnp.zeros((seq, LANE), F32)
            dqr_ref[...] = jnp.zeros((seq, LANE), F32)

        @pl.when(i == 0)
        def _():
            dk_s[...] = jnp.zeros((ta, 2 * LANE), F32)
            dv_s[...] = jnp.zeros((ta, LANE), F32)

        @pl.when(i >= j)
        def _():
            q = jnp.concatenate([qn_ref[...], qr_ref[...]], axis=1)
            k = jnp.concatenate([kn_ref[...], kr_ref[...]], axis=1)
            do_b = do_ref[...]
            s = jnp.where(_causal_mask(i, j, ta), _dot_nt(q, k) * ATTN_SCALE, -1e30)
            p = jnp.exp(s - jnp.tile(lse_ref[...], (1, ta // LANE)))
            dv_s[...] += _dot_tn(p.astype(BF16), do_b)
            dp = _dot_nt(do_b, v_ref[...])
            ds = (p * (dp - jnp.tile(dl_ref[...], (1, ta // LANE))) * ATTN_SCALE).astype(BF16)
            dk_s[...] += _dot_tn(ds, q)
            dq = _dot(ds, k)
            rows = pl.ds(pl.multiple_of(i * ta, ta), ta)
            dqn_ref[rows, :] += dq[:, :LANE]
            dqr_ref[rows, :] += dq[:, LANE:]

        @pl.when(i == na - 1)
        def _():
            dkn_ref[...] = dk_s[:, :LANE]
            dkr_ref[...] = dk_s[:, LANE:]
            dv_ref[...] = dv_s[...]

    qspec = pl.BlockSpec((ta, LANE), lambda b, h, j, i: (b * na + jnp.maximum(i, j), h))
    kspec = pl.BlockSpec((ta, LANE), lambda b, h, j, i: (b * na + j, h))
    krspec = pl.BlockSpec((ta, LANE), lambda b, h, j, i: (b * na + j, 0))
    dqspec = pl.BlockSpec((seq, LANE), lambda b, h, j, i: (b, h))
    return pl.pallas_call(
        body, name="attn_bwd", grid=(nb, nheads, na, na),
        in_specs=[qspec, qspec, kspec, krspec, kspec, qspec, qspec, qspec],
        out_specs=[dqspec, dqspec, kspec, kspec, kspec],
        out_shape=[_sds((t_all, hv))] * 5,
        scratch_shapes=[pltpu.VMEM((ta, 2 * LANE), F32), pltpu.VMEM((ta, LANE), F32)],
        compiler_params=_params(4),
    )(qn, qr, kn, kr, v, do, lse, delta)


def _head(o, ub, x1, target, wg, w, lay, tb):
    t_all, d = x1.shape
    hv, qr = lay.hv, lay.qr
    nheads = hv // LANE

    def body(o_ref, ub_ref, x1_ref, tg_ref, wg_ref, gf, loss_ref, dgf_ref, yb_ref, dx2_ref, do_ref, dg_ref, dl_ref, wob, sems):
        @pl.when(pl.program_id(0) == 0)
        def _():
            cps = _fetch(wg_ref, lay, "out_b", wob, sems, 0)
            for cp in cps:
                cp.start()
            loss_ref[...] = jnp.zeros((1, LANE), F32)
            dgf_ref[...] = jnp.zeros((1, d), F32)
            for cp in cps:
                cp.wait()

        ov = o_ref[...]
        g = ub_ref[:, qr:]
        sg = jax.nn.sigmoid(g)
        silu = g * sg
        yb = (ov * silu).astype(BF16)
        yb_ref[...] = yb
        x2 = x1_ref[...] + _dot(yb, wob[...])
        rinv = _rinv(x2)
        err = x2 * rinv * gf[...] - tg_ref[...]
        loss_ref[...] += (0.5 / d) * jnp.sum(jnp.sum(err * err, axis=1, keepdims=True), axis=0, keepdims=True)
        dx2, dgf = _rms_bwd(x2, rinv, gf[...], err * (1.0 / d))
        dgf_ref[...] += dgf
        dx2_ref[...] = dx2
        dyb = _dot_nt(dx2.astype(BF16), wob[...])
        dov = dyb * silu
        do_ref[...] = dov.astype(BF16)
        dg_ref[...] = dyb * ov * (sg * (1.0 + g * (1.0 - sg)))
        prod = dov * ov
        dl_ref[...] = jnp.concatenate(
            [jnp.broadcast_to(jnp.sum(prod[:, n * LANE:(n + 1) * LANE], axis=1, keepdims=True), (tb, LANE)) for n in range(nheads)], axis=1)

    tok = lambda c: pl.BlockSpec((tb, c), lambda i: (i, 0))
    return pl.pallas_call(
        body, name="head", grid=(t_all // tb,),
        in_specs=[tok(hv), tok(qr + hv), tok(d), tok(d), ANY, _full((1, d))],
        out_specs=[_full((1, LANE)), _full((1, d)), tok(hv), tok(d), tok(hv), tok(hv), tok(hv)],
        out_shape=[_sds((1, LANE)), _sds((1, d)), _sds((t_all, hv), BF16), _sds((t_all, d)), _sds((t_all, hv), BF16), _sds((t_all, hv)), _sds((t_all, hv))],
        scratch_shapes=[pltpu.VMEM((hv, d), BF16), pltpu.SemaphoreType.DMA((N_CHIPS,))],
        compiler_params=_params(1),
    )(o, ub, x1, target, wg, w["final_norm"])


def _fb_bwd(dqn, dqr, dkn, dkr, dv, dgate, ub, ckr, x1, dx2, wg, w, lay, cos_t, sin_t, seq, tb):
    t_all, d = x1.shape
    hv, qr, kvr = lay.hv, lay.qr, lay.kvr
    nheads = hv // LANE
    npos = seq // tb

    def body(dqn_ref, dqr_ref, dkn_ref, dkr_ref, dv_ref, dg_ref, ub_ref, ckr_ref, x1_ref, dx2_ref, wg_ref,
             qn, nb, kvn, wdkv, nkv, cos_ref, sin_ref,
             dx1_ref, dqrp_ref, dub_ref, dckr_ref, dqn_g, dnb_g, dkvn_g, dnkv_g, winb, wuk, wuv, wuqn, wuqr, sems):
        @pl.when(pl.program_id(0) == 0)
        def _():
            cps = []
            for n, (key, dst) in enumerate((("in_b", winb), ("uk", wuk), ("uv", wuv), ("uq_n", wuqn), ("uq_r", wuqr))):
                cps += _fetch(wg_ref, lay, key, dst, sems, n * N_CHIPS)
            for cp in cps:
                cp.start()
            dqn_g[...] = jnp.zeros((1, qr), F32)
            dnb_g[...] = jnp.zeros((1, d), F32)
            dkvn_g[...] = jnp.zeros((1, kvr), F32)
            dnkv_g[...] = jnp.zeros((1, d), F32)
            for cp in cps:
                cp.wait()

        cos, sin = cos_ref[...], sin_ref[...]
        xv = x1_ref[...]
        rinv1 = _rinv(xv)
        dqr_v = dqr_ref[...]
        dqr_pre = (dqr_v * jnp.tile(cos, (1, nheads)) + _swap_halves(dqr_v * jnp.tile(sin, (1, nheads)))).astype(BF16)
        dqrp_ref[...] = dqr_pre
        dcq = _dot_nt(dqn_ref[...].astype(BF16), wuqn[...]) + _dot_nt(dqr_pre, wuqr[...])
        cq_pre = ub_ref[:, :qr]
        dcq_pre, g1 = _rms_bwd(cq_pre, _rinv(cq_pre), qn[...], dcq)
        dqn_g[...] += g1
        dub = jnp.concatenate([dcq_pre, dg_ref[...]], axis=1).astype(BF16)
        dub_ref[...] = dub
        dx1_b, g2 = _rms_bwd(xv, rinv1, nb[...], _dot(dub, winb[...]))
        dnb_g[...] += g2
        dkr_all = dkr_ref[...]
        dkr_sum = dkr_all[:, :LANE]
        for n in range(1, nheads):
            dkr_sum = dkr_sum + dkr_all[:, n * LANE:(n + 1) * LANE]
        dckr_rope = dkr_sum * cos + _swap_halves(dkr_sum * sin)
        dckv = _dot_nt(dkn_ref[...].astype(BF16), wuk[...]) + _dot_nt(dv_ref[...].astype(BF16), wuv[...])
        ckv_pre = ckr_ref[:, :kvr]
        dckv_pre, g3 = _rms_bwd(ckv_pre, _rinv(ckv_pre), kvn[...], dckv)
        dkvn_g[...] += g3
        dckr = jnp.concatenate([dckv_pre, dckr_rope], axis=1).astype(BF16)
        dckr_ref[...] = dckr
        dx1_kv, g4 = _rms_bwd(xv, rinv1, nkv[...], _dot_nt(dckr, wdkv[...]))
        dnkv_g[...] += g4
        dx1_ref[...] = dx2_ref[...] + dx1_b + dx1_kv

    tok = lambda c: pl.BlockSpec((tb, c), lambda i: (i, 0))
    pos = pl.BlockSpec((tb, LANE), lambda i: (i % npos, 0))
    consts = [w["q_norm"], w["norm_b"], w["kv_norm"], w["w_dkv_p"], w["norm_kv"]]
    return pl.pallas_call(
        body, name="fb_bwd", grid=(t_all // tb,),
        in_specs=[tok(hv)] * 6 + [tok(qr + hv), tok(kvr + LANE), tok(d), tok(d), ANY] + [_full(c.shape) for c in consts] + [pos, pos],
        out_specs=[tok(d), tok(hv), tok(qr + hv), tok(kvr + LANE), _full((1, qr)), _full((1, d)), _full((1, kvr)), _full((1, d))],
        out_shape=[_sds((t_all, d)), _sds((t_all, hv), BF16), _sds((t_all, qr + hv), BF16), _sds((t_all, kvr + LANE), BF16),
                   _sds((1, qr)), _sds((1, d)), _sds((1, kvr)), _sds((1, d))],
        scratch_shapes=[pltpu.VMEM((qr + hv, d), BF16), pltpu.VMEM((kvr, d), BF16), pltpu.VMEM((kvr, d), BF16), pltpu.VMEM((qr, d), BF16),
                        pltpu.VMEM((qr, d), BF16), pltpu.SemaphoreType.DMA((5 * N_CHIPS,))],
        compiler_params=_params(1),
    )(dqn, dqr, dkn, dkr, dv, dgate, ub, ckr, x1, dx2, wg, *consts, cos_t, sin_t)


def _fa_bwd(dx1, x, u, hs, wg, w, lay, seq, tb):
    t_all, d = x.shape
    dr = lay.dr
    nblocks = w["w_rg"].shape[0]
    nblk = seq // tb
    nt = tb // SUBLANE
    per8 = tb // 8

    def body(dx1_ref, x_ref, u_ref, uh_ref, hs_ref, hh_ref, wg_ref, na, cw, cb, wrg, brg, wig, big, lam,
             gx_ref, du_ref, dna_g, dcw_g, dcb_g, dbrg_g, dbig_g, dlam_g, dwrg_g, dwig_g,
             wint, wout, xpad, hpad, a_s, d_s, g_s, dxpad, carry, sems):
        b, jj = pl.program_id(0), pl.program_id(1)
        first_block = jj == nblk - 1

        @pl.when((b == 0) & (jj == 0))
        def _():
            cps = _fetch(wg_ref, lay, "in_a", wint, sems, 0) + _fetch(wg_ref, lay, "out_a", wout, sems, N_CHIPS)
            for cp in cps:
                cp.start()
            dna_g[...] = jnp.zeros((1, d), F32)
            dcw_g[...] = jnp.zeros((4, dr), F32)
            dcb_g[...] = jnp.zeros((1, dr), F32)
            dbrg_g[...] = jnp.zeros((1, dr), F32)
            dbig_g[...] = jnp.zeros((1, dr), F32)
            dlam_g[...] = jnp.zeros((1, dr), F32)
            dwrg_g[...] = jnp.zeros((nblocks, LANE, LANE), F32)
            dwig_g[...] = jnp.zeros((nblocks, LANE, LANE), F32)
            for cp in cps:
                cp.wait()

        @pl.when(jj == 0)
        def _():
            dxpad[pl.ds(tb, 8), :] = jnp.zeros((8, dr), F32)
            carry[...] = jnp.zeros((8, dr), F32)

        keep = jnp.where(first_block, 0.0, 1.0)
        dx1v = dx1_ref[...]
        gate = u_ref[:, dr:]
        xpad[pl.ds(0, 8), :] = uh_ref[...] * keep
        xpad[pl.ds(8, tb), :] = u_ref[:, :dr]
        hpad[pl.ds(0, 8), :] = hh_ref[...] * keep
        hpad[pl.ds(8, tb), :] = hs_ref[...]
        xb = _conv(xpad, cw, cb[...], tb)
        xbb = xb.astype(BF16)
        r, i = _gates(xb, wrg, brg[...], wig, big[...], nblocks)
        sp = _softplus(-lam[...])
        log_a = -LRU_C * r * sp
        a = jnp.exp(log_a)
        nem = _neg_expm1(2.0 * log_a)
        mult = jnp.sqrt(nem)
        sg = jax.nn.sigmoid(gate)
        dy = _dot_nt(dx1v.astype(BF16), wout[...])
        hsv = hs_ref[...]
        dgate = dy * hsv * (sg * (1.0 + gate * (1.0 - sg)))
        a_s[...] = a
        d_s[...] = dy * (gate * sg)
        row = lax.broadcasted_iota(jnp.int32, (8, dr), 0)

        def step(k, c):
            r0 = pl.multiple_of((nt - 1 - k) * 8, 8)
            av = a_s[pl.ds(r0, 8), :]
            dv = d_s[pl.ds(r0, 8), :]
            qv = av * dv
            for s in (1, 2, 4):
                m = row < 8 - s
                a_sh = jnp.where(m, pltpu.roll(av, 8 - s, 0), 1.0)
                q_sh = jnp.where(m, pltpu.roll(qv, 8 - s, 0), 0.0)
                qv = qv + av * q_sh
                av = av * a_sh
            qv = qv + av * c
            g_s[pl.ds(r0, 8), :] = dv + jnp.where(row < 7, pltpu.roll(qv, 7, 0), c)
            return jnp.broadcast_to(qv[0:1, :], qv.shape)

        carry[...] = lax.fori_loop(0, nt, step, carry[...])
        g = g_s[...]
        ix = i * xb
        dlog_a = g * hpad[pl.ds(7, tb), :] * a - g * ix * ((1.0 - nem) / mult)
        dix = g * mult
        dlam_g[...] += -jax.nn.sigmoid(-lam[...]) * jnp.sum(dlog_a * (-LRU_C * r), axis=0, keepdims=True)
        drg = dlog_a * (-LRU_C * sp) * r * (1.0 - r)
        dig = dix * xb * i * (1.0 - i)
        dbrg_g[...] += jnp.sum(drg, axis=0, keepdims=True)
        dbig_g[...] += jnp.sum(dig, axis=0, keepdims=True)
        drgb, digb = drg.astype(BF16), dig.astype(BF16)
        back = []
        for n in range(nblocks):
            cols = slice(n * LANE, (n + 1) * LANE)
            dwrg_g[n] += _dot_tn(xbb[:, cols], drgb[:, cols])
            dwig_g[n] += _dot_tn(xbb[:, cols], digb[:, cols])
            back.append(_dot_nt(drgb[:, cols], wrg[n]) + _dot_nt(digb[:, cols], wig[n]))
        dxb = dix * i + jnp.concatenate(back, axis=1)
        dcb_g[...] += jnp.sum(dxb, axis=0, keepdims=True)
        for k in range(4):
            dcw_g[k:k + 1, :] += jnp.sum(dxb * xpad[pl.ds(5 + k, tb), :], axis=0, keepdims=True)
        dxpad[pl.ds(0, tb), :] = dxb
        dxpre = (cw[3:4, :] * dxb + cw[2:3, :] * dxpad[pl.ds(1, tb), :] + cw[1:2, :] * dxpad[pl.ds(2, tb), :]
                 + cw[0:1, :] * dxpad[pl.ds(3, tb), :])
        dxpad[pl.ds(tb, 8), :] = dxb[:8, :]
        du = jnp.concatenate([dxpre, dgate], axis=1).astype(BF16)
        du_ref[...] = du
        xv = x_ref[...]
        dxa, g1 = _rms_bwd(xv, _rinv(xv), na[...], _dot(du, wint[...]))
        dna_g[...] += g1
        gx_ref[...] = dx1v + dxa

    blk = lambda b, j: b * nblk + (nblk - 1 - j)
    tok = lambda c: pl.BlockSpec((tb, c), lambda b, j: (blk(b, j), 0))
    halo = pl.BlockSpec((8, dr), lambda b, j: (jnp.maximum(blk(b, j) * per8 - 1, 0), 0))
    consts = [w["norm_a"], w["conv_w"], w["conv_b"], w["w_rg"], w["b_rg"], w["w_ig"], w["b_ig"], w["lru_lambda"]]
    vec = lambda c: _full((1, c))
    blocks3 = (nblocks, LANE, LANE)
    return pl.pallas_call(
        body, name="fa_bwd", grid=(t_all // seq, nblk),
        in_specs=[tok(d), tok(d), tok(2 * dr), halo, tok(dr), halo, ANY] + [_full(c.shape) for c in consts],
        out_specs=[tok(d), tok(2 * dr), vec(d), _full((4, dr)), vec(dr), vec(dr), vec(dr), vec(dr), _full(blocks3), _full(blocks3)],
        out_shape=[_sds((t_all, d)), _sds((t_all, 2 * dr), BF16), _sds((1, d)), _sds((4, dr)), _sds((1, dr)), _sds((1, dr)), _sds((1, dr)),
                   _sds((1, dr)), _sds(blocks3), _sds(blocks3)],
        scratch_shapes=[pltpu.VMEM((2 * dr, d), BF16), pltpu.VMEM((dr, d), BF16), pltpu.VMEM((tb + 8, dr), F32), pltpu.VMEM((tb + 8, dr), F32),
                        pltpu.VMEM((tb, dr), F32), pltpu.VMEM((tb, dr), F32), pltpu.VMEM((tb, dr), F32), pltpu.VMEM((tb + 8, dr), F32),
                        pltpu.VMEM((8, dr), F32), pltpu.SemaphoreType.DMA((2 * N_CHIPS,))],
        compiler_params=_params(2),
    )(dx1, x, u, u, hs, hs, wg, *consts)


def _pick_bm(off, rows, cap=1408):
    best = None
    for bm in range(LANE, min(rows, cap) + 1, LANE):
        if rows % bm == 0 and off % bm == 0:
            best = bm
    assert best is not None, (off, rows)
    return best


def _mm_into(gbuf, a, b, off, name, bt):
    t_all, m = a.shape
    n = b.shape[1]
    bm = _pick_bm(off, m)

    def body(a_ref, b_ref, g_ref, o_ref):
        del g_ref

        @pl.when(pl.program_id(1) == 0)
        def _():
            o_ref[...] = jnp.zeros((bm, n), F32)

        o_ref[...] += _dot_tn(a_ref[...].astype(BF16), b_ref[...].astype(BF16))

    return pl.pallas_call(
        body, name=name, grid=(m // bm, t_all // bt),
        in_specs=[pl.BlockSpec((bt, bm), lambda i, t: (t, i)), pl.BlockSpec((bt, n), lambda i, t: (t, 0)), ANY],
        out_specs=pl.BlockSpec((bm, n), lambda i, t: (off // bm + i, 0)),
        out_shape=_sds(gbuf.shape), input_output_aliases={2: 0},
        compiler_params=_params(2),
    )(a, b, gbuf)


def _mm_tn(a, b, name, bt):
    t_all, m = a.shape
    n = b.shape[1]

    def body(a_ref, b_ref, o_ref):
        @pl.when(pl.program_id(0) == 0)
        def _():
            o_ref[...] = jnp.zeros((m, n), F32)

        o_ref[...] += _dot_tn(a_ref[...].astype(BF16), b_ref[...].astype(BF16))

    return pl.pallas_call(
        body, name=name, grid=(t_all // bt,),
        in_specs=[pl.BlockSpec((bt, m), lambda t: (t, 0)), pl.BlockSpec((bt, n), lambda t: (t, 0))],
        out_specs=_full((m, n)), out_shape=_sds((m, n)),
        compiler_params=_params(1),
    )(a, b)


def _all_gather8(blocks, name):
    nb = len(blocks)

    def body(*refs):
        x_refs, out_refs = refs[:nb], refs[nb:2 * nb]
        send_sems, recv_sems, local_sems = refs[2 * nb:]
        x, y, c = _place()
        me, sibling = (x, y, c), (x, y, 1 - c)
        chips = [(1 - x, y), (x, 1 - y), (1 - x, 1 - y)]
        waits = []
        for n in range(nb):
            x_ref, out_ref = x_refs[n], out_refs[n]

            def slot(px, py, pc, out_ref=out_ref):
                return out_ref.at[4 * px + 2 * py + pc]

            def copy(k, blk, to, src=None, n=n, slot=slot):
                return pltpu.make_async_remote_copy(
                    src_ref=slot(*blk) if src is None else src, dst_ref=slot(*blk), send_sem=send_sems.at[n, k], recv_sem=recv_sems.at[n, k],
                    device_id=to, device_id_type=MESH)

            mine = pltpu.make_async_copy(x_ref, slot(*me), local_sems.at[n])
            mine.start()
            first = [copy(0, me, sibling, src=x_ref)] + [copy(1 + j, me, (*chip, c), src=x_ref) for j, chip in enumerate(chips)]
            for cp in first:
                cp.start()
            waits.append((copy, mine, first))
        for n in range(nb):
            copy, mine, first = waits[n]
            passed = [copy(4 + j, (*chip, c), sibling) for j, chip in enumerate(chips)]
            for j, chip in enumerate(chips):
                copy(1 + j, (*chip, c), me).wait_recv()
                passed[j].start()
            copy(0, sibling, me).wait_recv()
            for j, chip in enumerate(chips):
                copy(4 + j, (*chip, 1 - c), me).wait_recv()
            for cp in first + passed:
                cp.wait_send()
            mine.wait()

    return pl.pallas_call(
        body, name=name, out_shape=[_sds((8,) + b.shape, b.dtype) for b in blocks], in_specs=[ANY] * nb, out_specs=[ANY] * nb,
        scratch_shapes=[pltpu.SemaphoreType.DMA((nb, 7)), pltpu.SemaphoreType.DMA((nb, 7)), pltpu.SemaphoreType.DMA((nb,))],
    )(*blocks)


def _swap_sibling(src, name, half_cols=False):
    rows, cols = src.shape
    half = cols // 2 if half_cols else cols

    def body(src_ref, out_ref, send_sem, recv_sem):
        x, y, c = _place()
        part = src_ref.at[:, pl.ds(pl.multiple_of((1 - c) * half, LANE), half)] if half_cols else src_ref
        cp = pltpu.make_async_remote_copy(src_ref=part, dst_ref=out_ref, send_sem=send_sem, recv_sem=recv_sem,
                                          device_id=(x, y, 1 - c), device_id_type=MESH)
        cp.start()
        cp.wait()

    return pl.pallas_call(
        body, name=name, out_shape=_sds((rows, half), src.dtype), in_specs=[ANY], out_specs=ANY,
        scratch_shapes=[pltpu.SemaphoreType.DMA, pltpu.SemaphoreType.DMA],
    )(src)


def _scatter_chips(part16, part32, lay):
    half = part16.shape[1]

    def body(p16_ref, p32_ref, got_ref, own_ref, send_sems, recv_sems, local_sem):
        x, y, c = _place()
        chips = [(1 - x, y), (x, 1 - y), (1 - x, 1 - y)]

        def rows_of(ref, key, chip):
            start = pl.multiple_of(lay.g_off[key] + chip * lay.rows[key], ROW_ALIGN)
            return ref.at[pl.ds(start, lay.rows[key]), :]

        def compact(ref, key):
            return ref.at[pl.ds(lay.c_off[key], lay.rows[key]), :]

        for key in G_ORDER:
            pltpu.make_async_copy(rows_of(p32_ref, key, 2 * x + y), compact(own_ref, key), local_sem).start()
        for k, (px, py) in enumerate(chips):
            for key in G_ORDER:
                pltpu.make_async_remote_copy(src_ref=rows_of(p16_ref, key, 2 * px + py), dst_ref=compact(got_ref.at[k], key),
                                             send_sem=send_sems.at[k], recv_sem=recv_sems.at[k], device_id=(px, py, c), device_id_type=MESH).start()
        for k, (px, py) in enumerate(chips):
            pltpu.make_async_remote_copy(src_ref=got_ref.at[k], dst_ref=got_ref.at[k], send_sem=send_sems.at[k], recv_sem=recv_sems.at[k],
                                         device_id=(px, py, c), device_id_type=MESH).wait()
        pltpu.make_async_copy(own_ref, own_ref, local_sem).wait()

    return pl.pallas_call(
        body, name="rs_chips", out_shape=[_sds((3, lay.c_rows, half), BF16), _sds((lay.c_rows, half), F32)], in_specs=[ANY, ANY], out_specs=[ANY, ANY],
        scratch_shapes=[pltpu.SemaphoreType.DMA((3,)), pltpu.SemaphoreType.DMA((3,)), pltpu.SemaphoreType.DMA],
    )(part16, part32)


def _sum_sibling(gbuf, got, cidx):
    rows, d = gbuf.shape
    half = d // 2
    rb = _row_block(rows)

    def body(c_ref, g_ref, r_ref, o32_ref, o16_ref):
        del c_ref
        s = g_ref[...] + r_ref[...]
        o32_ref[...] = s
        o16_ref[...] = s.astype(BF16)

    plain = pl.BlockSpec((rb, half), lambda i, c: (i, 0))
    return pl.pallas_call(
        body, name="rs_sum_sibling",
        grid_spec=pltpu.PrefetchScalarGridSpec(num_scalar_prefetch=1, grid=(rows // rb,),
                                               in_specs=[pl.BlockSpec((rb, half), lambda i, c: (i, c[0])), plain], out_specs=[plain, plain]),
        out_shape=[_sds((rows, half)), _sds((rows, half), BF16)], compiler_params=_params(1),
    )(cidx, gbuf, got)


def _sum_chips(own, got):
    rows, half = own.shape
    rb = _row_block(rows)

    def body(a_ref, b_ref, o_ref):
        o_ref[...] = ((a_ref[...] + b_ref[0].astype(F32)) + b_ref[1].astype(F32)) + b_ref[2].astype(F32)

    spec = pl.BlockSpec((rb, half), lambda i: (i, 0))
    return pl.pallas_call(
        body, name="rs_sum_chips", grid=(rows // rb,), in_specs=[spec, pl.BlockSpec((3, rb, half), lambda i: (0, i, 0))], out_specs=spec,
        out_shape=_sds((rows, half)), compiler_params=_params(1),
    )(own, got)


def _adamw(w, g, m, v):
    m = ADAM_B1 * m + (1.0 - ADAM_B1) * g
    v = ADAM_B2 * v + (1.0 - ADAM_B2) * (g * g)
    m_hat = m / (1.0 - ADAM_B1 ** ADAM_STEP)
    v_hat = v / (1.0 - ADAM_B2 ** ADAM_STEP)
    return -ADAM_LR * (m_hat / (jnp.sqrt(v_hat) + ADAM_EPS) + ADAM_WD * w), m, v


def _adamw_rows(name, w, g, m, v):
    _, rows, cols = w.shape
    rb = _row_block(rows, 256)

    def body(w_ref, g_ref, m_ref, v_ref, d_ref, mo_ref, vo_ref):
        d_ref[...], mo_ref[...], vo_ref[...] = _adamw(w_ref[...], g_ref[...], m_ref[...], v_ref[...])

    spec = pl.BlockSpec((1, rb, cols), lambda i: (0, i, 0))
    return pl.pallas_call(
        body, name=name, grid=(rows // rb,), in_specs=[spec] * 4, out_specs=[spec] * 3, out_shape=[_sds(w.shape)] * 3,
        compiler_params=_params(1),
    )(w, g, m, v)


def _adamw_group(ws, gs, ms, vs):
    n = len(ws)

    def body(*refs):
        for k in range(n):
            w_ref, g_ref, m_ref, v_ref = (refs[j * n + k] for j in range(4))
            outs = _adamw(w_ref[...], g_ref[...], m_ref[...], v_ref[...])
            for j in range(3):
                refs[(4 + j) * n + k][...] = outs[j]

    outs = pl.pallas_call(
        body, name="adamw_small", out_shape=[_sds(w.shape) for w in ws] * 3,
        compiler_params=pltpu.CompilerParams(vmem_limit_bytes=VMEM_LIMIT),
    )(*ws, *gs, *ms, *vs)
    return outs[:n], outs[n:2 * n], outs[2 * n:]


def _gather_weights(sh, lay):
    x, y, c = _place()
    d = lay.d
    uq = sh["w_uq"][0].astype(BF16)
    parts = {
        "in_b": sh["w_in_b"][0].T.astype(BF16), "in_a": sh["w_in_a"][0].T.astype(BF16), "out_a": sh["w_out_a"][0].astype(BF16),
        "out_b": sh["w_out_b"][0].astype(BF16), "uk": sh["w_uk"].astype(BF16).reshape(-1, d), "uv": sh["w_uv"].astype(BF16).reshape(-1, d),
        "uq_n": uq[:, :, :QK_NOPE].reshape(-1, d), "uq_r": jnp.pad(uq[:, :, QK_NOPE:], ((0, 0), (0, 0), (0, LANE - QK_ROPE))).reshape(-1, d),
        "dkv": jnp.pad(sh["w_dkv"].astype(BF16), ((0, 0), (0, LANE - QK_ROPE))).reshape(-1, d),
    }
    stack = jnp.concatenate([parts[k] for k in W_ORDER], axis=0)
    stack = jnp.pad(stack, ((0, lay.w_rows - stack.shape[0]), (0, 0))).reshape(2, lay.w_rows // 2, d)
    small = jnp.concatenate([sh[k].reshape(-1) for k in SMALL])
    n_small = small.shape[0]
    width = _round_up(n_small, 2 * SUBLANE * LANE) // (2 * SUBLANE)
    small = jnp.pad(small, (0, 2 * SUBLANE * width - n_small)).reshape(2, SUBLANE, width)
    wg, sg = _all_gather8([lax.dynamic_index_in_dim(stack, c, 0, keepdims=False), lax.dynamic_index_in_dim(small, c, 0, keepdims=False)], "ag_weights")
    wg = wg.reshape(N_CHIPS, lay.w_rows, d)
    sg = sg.reshape(N_CHIPS, 2 * SUBLANE * width)
    full, off = {}, 0
    for k in SMALL:
        n = sh[k].size
        piece = sg[:, off:off + n]
        off += n
        if k == "conv_w":
            full[k] = piece.reshape(N_CHIPS, 4, n // 4).transpose(1, 0, 2).reshape(4, n)
        else:
            full[k] = piece.reshape(1, N_CHIPS * n)
    rows = lay.rows["dkv"]
    w_dkv_p = wg[:, lay.w_off["dkv"]:lay.w_off["dkv"] + rows, :].reshape(d, lay.kvr + LANE)
    return wg, w_dkv_p, full


def _chip_split(g, taps=False):
    if taps:
        n = g.shape[1] // N_CHIPS
        return g.reshape(4, N_CHIPS, n).transpose(1, 0, 2).reshape(N_CHIPS, 4 * n)
    return g.reshape(N_CHIPS, -1)


def kernel(x, norm_a, w_in_a, conv_w, conv_b, w_rg, b_rg, w_ig, b_ig, lru_lambda, w_out_a, norm_kv, w_dkv, kv_norm, w_uk, w_uv, norm_b, w_in_b, q_norm, w_uq, w_out_b, final_norm, loss_target, m_norm_a, m_w_in_a, m_conv_w, m_conv_b, m_w_rg, m_b_rg, m_w_ig, m_b_ig, m_lru_lambda, m_w_out_a, m_norm_kv, m_w_dkv, m_kv_norm, m_w_uk, m_w_uv, m_norm_b, m_w_in_b, m_q_norm, m_w_uq, m_w_out_b, m_final_norm, v_norm_a, v_w_in_a, v_conv_w, v_conv_b, v_w_rg, v_b_rg, v_w_ig, v_b_ig, v_lru_lambda, v_w_out_a, v_norm_kv, v_w_dkv, v_kv_norm, v_w_uk, v_w_uv, v_norm_b, v_w_in_b, v_q_norm, v_w_uq, v_w_out_b, v_final_norm):
    given = dict(locals())
    sh = {k: given[k] for k in WEIGHTS}
    xi, yi, ci = _place()
    nb, seq, d = x.shape
    t_all = nb * seq
    tb_a, tb_b, ta, bt = min(TOKENS_A, seq), min(TOKENS_B, seq), min(TOKENS_ATTN, seq), min(TOKENS_MM, t_all)
    dr = conv_b.shape[1] * N_CHIPS
    qr, kvr, nheads = q_norm.shape[1], kv_norm.shape[0], w_uk.shape[1]
    hv = nheads * LANE
    n_small = sum(sh[k].size for k in SMALL)
    n_repl = sum(sh[k].size for k in REPL)
    lay = _Layout(d, dr, qr, kvr, hv, n_small, n_repl)
    half = d // 2

    wg, w_dkv_p, w = _gather_weights(sh, lay)
    w.update({"w_rg": w_rg[0].astype(BF16), "w_ig": w_ig[0].astype(BF16), "w_dkv_p": w_dkv_p, "norm_kv": norm_kv[None, :],
              "kv_norm": kv_norm[None, :], "final_norm": final_norm[None, :], "norm_b": norm_b, "q_norm": q_norm})
    cos_t, sin_t = _rope_tables(seq)

    x0 = x.reshape(t_all, d)
    x1, u, hs, h, y = _fa_fwd(x0, wg, w, lay, seq, tb_a)
    qn, qrp, kn, kr, v, ub, ckr, hb, hk, cq, ckv = _fb_fwd(x1, wg, w, lay, cos_t, sin_t, seq, tb_b)
    o, lse = _attn_fwd(qn, qrp, kn, kr, v, seq, ta)
    loss, g_final_norm, yb, dx2, do, dgate, delta = _head(o, ub, x1, loss_target.reshape(t_all, d), wg, w, lay, tb_b)
    dqn, dqr, dkn, dkr, dv = _attn_bwd(qn, qrp, kn, kr, v, do, lse, delta, seq, ta)
    dx1, dqr_pre, dub, dckr, g_q_norm, g_norm_b, g_kv_norm, g_norm_kv = _fb_bwd(
        dqn, dqr, dkn, dkr, dv, dgate, ub, ckr, x1, dx2, wg, w, lay, cos_t, sin_t, seq, tb_b)
    gx, du, g_norm_a, g_conv_w, g_conv_b, g_b_rg, g_b_ig, g_lam, g_w_rg, g_w_ig = _fa_bwd(dx1, x0, u, hs, wg, w, lay, seq, tb_a)
    loss = lax.psum(loss[0, 0], ("x", "y", "c"))

    g_dkv = _mm_tn(hk, dckr, "dw_dkv", bt)
    small = jnp.concatenate([_chip_split(g_norm_a), _chip_split(g_conv_w, taps=True), _chip_split(g_conv_b), _chip_split(g_b_rg),
                             _chip_split(g_b_ig), _chip_split(g_lam)], axis=1)
    small = jnp.pad(small, ((0, 0), (0, lay.small_rows * d - small.shape[1]))).reshape(N_CHIPS, lay.small_rows, d)
    repl_parts = {"w_rg": g_w_rg, "w_ig": g_w_ig, "norm_kv": g_norm_kv, "kv_norm": g_kv_norm, "norm_b": g_norm_b, "q_norm": g_q_norm,
                  "final_norm": g_final_norm}
    repl = jnp.concatenate([repl_parts[k].reshape(-1) for k in REPL])
    repl = jnp.pad(repl, (0, N_CHIPS * lay.repl_rows * d - n_repl)).reshape(N_CHIPS, lay.repl_rows, d)
    pad_rows = lay.rows["rest"] - lay.rows["dkv"] - lay.small_rows - lay.repl_rows
    rest = jnp.concatenate([g_dkv.reshape(N_CHIPS, lay.rows["dkv"], d), small, repl, jnp.zeros((N_CHIPS, pad_rows, d), F32)], axis=1)
    gbuf = lax.dynamic_update_slice(lax.empty((lay.g_rows, d), F32), rest.reshape(N_CHIPS * lay.rows["rest"], d), (lay.g_off["rest"], 0))
    for key, a, b in (("in_b", dub, hb), ("in_a", du, h), ("out_a", y, dx1), ("out_b", yb, dx2), ("uk", ckv, dkn), ("uv", ckv, dv),
                      ("uq_n", cq, dqn), ("uq_r", cq, dqr_pre)):
        gbuf = _mm_into(gbuf, a, b, lay.g_off[key], "dw_" + key, bt)

    got = _swap_sibling(gbuf, "rs_sibling", half_cols=True)
    part32, part16 = _sum_sibling(gbuf, got, jnp.reshape(ci, (1,)).astype(jnp.int32))
    others, own = _scatter_chips(part16, part32, lay)
    mine = _sum_chips(own, others)
    theirs = _swap_sibling(mine, "rs_return")
    red = jnp.concatenate([jnp.where(ci == 0, mine, theirs), jnp.where(ci == 0, theirs, mine)], axis=1)
    r0 = lay.c_off["rest"] + lay.rows["dkv"] + lay.small_rows
    (rep_all,) = _all_gather8([mine[r0:r0 + lay.repl_rows]], "ag_rep")
    rep_flat = rep_all.reshape(N_CHIPS, 2, lay.repl_rows, half).transpose(0, 2, 1, 3).reshape(-1)

    def rows(key):
        return red[lay.c_off[key]:lay.c_off[key] + lay.rows[key]]

    grads = {"w_in_b": rows("in_b").T[None], "w_in_a": rows("in_a").T[None], "w_out_a": rows("out_a")[None], "w_out_b": rows("out_b")[None],
             "w_uk": rows("uk").reshape(w_uk.shape), "w_uv": rows("uv").reshape(w_uv.shape)}
    uq_n = rows("uq_n").reshape(qr // N_CHIPS, nheads, LANE)
    uq_r = rows("uq_r").reshape(qr // N_CHIPS, nheads, LANE)[:, :, :QK_ROPE]
    grads["w_uq"] = jnp.concatenate([uq_n, uq_r], axis=2)[None]
    rest_red = red[lay.c_off["rest"]:lay.c_off["rest"] + lay.rows["rest"]]
    grads["w_dkv"] = rest_red[:lay.rows["dkv"]].reshape(d // N_CHIPS, kvr + LANE)[:, :kvr + QK_ROPE]
    small_red = rest_red[lay.rows["dkv"]:lay.rows["dkv"] + lay.small_rows].reshape(-1)
    off = 0
    for k in SMALL:
        n = sh[k].size
        grads[k] = small_red[off:off + n].reshape(sh[k].shape)
        off += n
    off = 0
    for k in REPL:
        n = sh[k].size
        grads[k] = rep_flat[off:off + n].reshape(sh[k].shape)
        off += n

    new = {}
    for k in ("w_in_a", "w_in_b", "w_out_a", "w_out_b"):
        new[k] = _adamw_rows("adamw_" + k, sh[k], grads[k], given["m_" + k], given["v_" + k])
    rest_names = [k for k in WEIGHTS if k not in new]
    as2d = lambda a: a[None, :] if a.ndim == 1 else a
    ds, ms, vs = _adamw_group([as2d(sh[k]) for k in rest_names], [as2d(grads[k]) for k in rest_names],
                              [as2d(given["m_" + k]) for k in rest_names], [as2d(given["v_" + k]) for k in rest_names])
    for n, k in enumerate(rest_names):
        new[k] = tuple(a.reshape(sh[k].shape) for a in (ds[n], ms[n], vs[n]))
    return (loss, gx.reshape(nb, seq, d), *[grads[k] for k in WEIGHTS], *[new[k][0] for k in WEIGHTS], *[new[k][1] for k in WEIGHTS],
            *[new[k][2] for k in WEIGHTS])
```

```python
import jax
import jax.numpy as jnp
from jax import lax
from jax.experimental import pallas as pl
from jax.experimental.pallas import tpu as pltpu

F32, BF16 = jnp.float32, jnp.bfloat16
EPS = 1e-6
LRU_C = 8.0
ROPE_THETA = 10000.0
QK_NOPE, QK_ROPE = 128, 64
ATTN_SCALE = (QK_NOPE + QK_ROPE) ** -0.5
LN2 = 0.6931471805599453
Q_SCALE = ATTN_SCALE / LN2
ATTN_ROWS = 32
LANE = 128
SUBLANE = 8
ROW_ALIGN = 32
VMEM_LIMIT = 60000 * 1024
ADAM_LR, ADAM_B1, ADAM_B2, ADAM_EPS, ADAM_WD, ADAM_STEP = 0.001, 0.9, 0.999, 1e-08, 0.01, 10
MESH = pl.DeviceIdType.MESH
ANY = pl.BlockSpec(memory_space=pl.ANY)
N_CHIPS = 4
TOKENS_A, TOKENS_B, TOKENS_ATTN, TOKENS_MM = 256, 256, 512, 512

SMALL = ("norm_a", "conv_w", "conv_b", "b_rg", "b_ig", "lru_lambda")
REPL = ("w_rg", "w_ig", "norm_kv", "kv_norm", "norm_b", "q_norm", "final_norm")
WEIGHTS = ("norm_a", "w_in_a", "conv_w", "conv_b", "w_rg", "b_rg", "w_ig", "b_ig", "lru_lambda", "w_out_a", "norm_kv",
           "w_dkv", "kv_norm", "w_uk", "w_uv", "norm_b", "w_in_b", "q_norm", "w_uq", "w_out_b", "final_norm")
W_ORDER = ("in_b", "in_a", "out_a", "out_b", "uk", "uv", "uq_n", "uq_r", "dkv")
G_ORDER = ("in_b", "rest", "in_a", "out_a", "out_b", "uk", "uv", "uq_n", "uq_r")


def _sds(shape, dtype=F32):
    return jax.ShapeDtypeStruct(tuple(shape), dtype)


def _params(n_grid):
    return pltpu.CompilerParams(dimension_semantics=("arbitrary",) * n_grid, vmem_limit_bytes=VMEM_LIMIT)


def _full(shape):
    nd = len(shape)
    return pl.BlockSpec(tuple(shape), lambda *g: (0,) * nd)


def _round_up(n, k):
    return -(-n // k) * k


def _row_block(rows, cap=512):
    best = SUBLANE
    for r in range(SUBLANE, min(rows, cap) + 1, SUBLANE):
        if rows % r == 0:
            best = r
    return best


def _place():
    return lax.axis_index("x"), lax.axis_index("y"), lax.axis_index("c")


class _Layout:
    def __init__(self, d, dr, qr, kvr, hv, n_small, n_repl):
        assert hv == d, "the packed rows are D_MODEL wide, which must equal heads * 128"
        self.d, self.dr, self.qr, self.kvr, self.hv = d, dr, qr, kvr, hv
        per_chip = {"in_b": (qr + hv) // N_CHIPS, "in_a": 2 * dr // N_CHIPS, "out_a": dr // N_CHIPS, "out_b": hv // N_CHIPS,
                    "uk": kvr // N_CHIPS, "uv": kvr // N_CHIPS, "uq_n": qr // N_CHIPS, "uq_r": qr // N_CHIPS,
                    "dkv": (d // N_CHIPS) * (kvr + LANE) // d}
        assert all(r % ROW_ALIGN == 0 for r in per_chip.values()), per_chip
        self.small_rows = _round_up(-(-n_small // d), SUBLANE)
        self.repl_rows = _round_up(-(-n_repl // (N_CHIPS * d)), SUBLANE)
        per_chip["rest"] = _round_up(per_chip["dkv"] + self.small_rows + self.repl_rows, ROW_ALIGN)
        self.rows = per_chip
        self.w_off, off = {}, 0
        for k in W_ORDER:
            self.w_off[k] = off
            off += per_chip[k]
        self.w_rows = _round_up(off, ROW_ALIGN)
        self.g_off, self.c_off, off = {}, {}, 0
        for k in G_ORDER:
            self.c_off[k] = off
            self.g_off[k] = N_CHIPS * off
            off += per_chip[k]
        self.c_rows = off
        self.g_rows = N_CHIPS * off


def _dot(a, b):
    return jnp.dot(a, b, preferred_element_type=F32)


def _dot_nt(a, b):
    return lax.dot_general(a, b, (((1,), (1,)), ((), ())), preferred_element_type=F32)


def _dot_tn(a, b):
    return lax.dot_general(a, b, (((0,), (0,)), ((), ())), preferred_element_type=F32)


def _rinv(x):
    return lax.rsqrt(jnp.mean(x * x, axis=-1, keepdims=True) + EPS)


def _rms_bwd(x, rinv, g, dy):
    z = dy * g
    dx = rinv * z - x * (rinv * rinv * rinv) * jnp.mean(z * x, axis=-1, keepdims=True)
    dg = jnp.sum(dy * (x * rinv), axis=0, keepdims=True)
    return dx, dg


def _softplus(z):
    return jnp.maximum(z, 0.0) + jnp.log1p(jnp.exp(-jnp.abs(z)))


def _neg_expm1(z):
    series = -z * (1 + z / 2 * (1 + z / 3 * (1 + z / 4 * (1 + z / 5 * (1 + z / 6)))))
    return jnp.where(z > -0.3, series, 1.0 - jnp.exp(z))


def _swap_halves(x):
    w = x.shape[1]
    lane = lax.broadcasted_iota(jnp.int32, x.shape, 1)
    return jnp.where(lane % QK_ROPE < QK_ROPE // 2, pltpu.roll(x, w - QK_ROPE // 2, 1), pltpu.roll(x, QK_ROPE // 2, 1))


def _rope_tables(seq):
    pos = jnp.arange(seq, dtype=F32)
    inv = ROPE_THETA ** (-jnp.arange(0, QK_ROPE, 2, dtype=F32) / QK_ROPE)
    ang = pos[:, None] * inv[None, :]
    cos, sin = jnp.cos(ang), jnp.sin(ang)
    zero = jnp.zeros((seq, LANE - QK_ROPE), F32)
    return jnp.concatenate([cos, cos, zero], 1), jnp.concatenate([-sin, sin, zero], 1)


def _fetch(wg_ref, lay, key, dst, sems, k0):
    rows = lay.rows[key]
    return [pltpu.make_async_copy(wg_ref.at[p, pl.ds(lay.w_off[key], rows), :], dst.at[pl.ds(p * rows, rows), :], sems.at[k0 + p])
            for p in range(N_CHIPS)]


def _gates(xb, wrg_ref, brg, wig_ref, big, nblocks):
    xbb = xb.astype(BF16)
    rg = [_dot(xbb[:, n * LANE:(n + 1) * LANE], wrg_ref[n]) for n in range(nblocks)]
    ig = [_dot(xbb[:, n * LANE:(n + 1) * LANE], wig_ref[n]) for n in range(nblocks)]
    r = jax.nn.sigmoid(jnp.concatenate(rg, axis=1) + brg)
    i = jax.nn.sigmoid(jnp.concatenate(ig, axis=1) + big)
    return r, i


def _conv(xpad, cw_ref, cb, tb):
    return (cb + cw_ref[3:4, :] * xpad[pl.ds(8, tb), :] + cw_ref[2:3, :] * xpad[pl.ds(7, tb), :]
            + cw_ref[1:2, :] * xpad[pl.ds(6, tb), :] + cw_ref[0:1, :] * xpad[pl.ds(5, tb), :])


def _fa_fwd(x, wg, w, lay, seq, tb):
    t_all, d = x.shape
    dr = lay.dr
    nblocks = w["w_rg"].shape[0]
    nblk = seq // tb
    nt = tb // SUBLANE

    def body(x_ref, wg_ref, na, cw, cb, wrg, brg, wig, big, lam, x1_ref, u_ref, hs_ref, h_ref, y_ref,
             wint, wout, xpad, a_s, b_s, carry, sems):
        @pl.when((pl.program_id(0) == 0) & (pl.program_id(1) == 0))
        def _():
            cps = _fetch(wg_ref, lay, "in_a", wint, sems, 0) + _fetch(wg_ref, lay, "out_a", wout, sems, N_CHIPS)
            for cp in cps:
                cp.start()
            for cp in cps:
                cp.wait()

        @pl.when(pl.program_id(1) == 0)
        def _():
            xpad[pl.ds(0, 8), :] = jnp.zeros((8, dr), F32)
            carry[...] = jnp.zeros((8, dr), F32)

        xv = x_ref[...]
        h = (xv * _rinv(xv) * na[...]).astype(BF16)
        h_ref[...] = h
        u = _dot_nt(h, wint[...])
        u_ref[...] = u
        xpre, gate = u[:, :dr], u[:, dr:]
        xpad[pl.ds(8, tb), :] = xpre
        xb = _conv(xpad, cw, cb[...], tb)
        xpad[pl.ds(0, 8), :] = xpre[tb - 8:, :]
        r, i = _gates(xb, wrg, brg[...], wig, big[...], nblocks)
        log_a = -LRU_C * r * _softplus(-lam[...])
        a_s[...] = jnp.exp(log_a)
        b_s[...] = jnp.sqrt(_neg_expm1(2.0 * log_a)) * (i * xb)
        row = lax.broadcasted_iota(jnp.int32, (8, dr), 0)

        def step(t, c):
            r0 = pl.multiple_of(t * 8, 8)
            a = a_s[pl.ds(r0, 8), :]
            b = b_s[pl.ds(r0, 8), :]
            for s in (1, 2, 4):
                m = row >= s
                a_sh = jnp.where(m, pltpu.roll(a, s, 0), 1.0)
                b_sh = jnp.where(m, pltpu.roll(b, s, 0), 0.0)
                b = a * b_sh + b
                a = a * a_sh
            hh = b + a * c
            hs_ref[pl.ds(r0, 8), :] = hh
            return jnp.broadcast_to(hh[7:8, :], hh.shape)

        carry[...] = lax.fori_loop(0, nt, step, carry[...])
        y = (hs_ref[...] * (gate * jax.nn.sigmoid(gate))).astype(BF16)
        y_ref[...] = y
        x1_ref[...] = xv + _dot(y, wout[...])

    tok = lambda c: pl.BlockSpec((tb, c), lambda b, j: (b * nblk + j, 0))
    consts = [w["norm_a"], w["conv_w"], w["conv_b"], w["w_rg"], w["b_rg"], w["w_ig"], w["b_ig"], w["lru_lambda"]]
    return pl.pallas_call(
        body, name="fa_fwd", grid=(t_all // seq, nblk),
        in_specs=[tok(d), ANY] + [_full(c.shape) for c in consts],
        out_specs=[tok(d), tok(2 * dr), tok(dr), tok(d), tok(dr)],
        out_shape=[_sds((t_all, d)), _sds((t_all, 2 * dr)), _sds((t_all, dr)), _sds((t_all, d), BF16), _sds((t_all, dr), BF16)],
        scratch_shapes=[pltpu.VMEM((2 * dr, d), BF16), pltpu.VMEM((dr, d), BF16), pltpu.VMEM((tb + 8, dr), F32), pltpu.VMEM((tb, dr), F32),
                        pltpu.VMEM((tb, dr), F32), pltpu.VMEM((8, dr), F32), pltpu.SemaphoreType.DMA((2 * N_CHIPS,))],
        compiler_params=_params(2),
    )(x, wg, *consts)


def _fb_fwd(x1, wg, w, lay, cos_t, sin_t, seq, tb):
    t_all, d = x1.shape
    kvr, qr, hv = lay.kvr, lay.qr, lay.hv
    nheads = hv // LANE
    npos = seq // tb

    def body(x_ref, wg_ref, nkv, nb, wdkv, kvn, qn, cos_ref, sin_ref,
             qn_o, qr_o, kn_o, kr_o, v_o, ub_o, ckr_o, hb_o, hk_o, cq_o, ckv_o, winb, wuk, wuv, wuqn, wuqr, sems):
        @pl.when(pl.program_id(0) == 0)
        def _():
            cps = []
            for n, (key, dst) in enumerate((("in_b", winb), ("uk", wuk), ("uv", wuv), ("uq_n", wuqn), ("uq_r", wuqr))):
                cps += _fetch(wg_ref, lay, key, dst, sems, n * N_CHIPS)
            for cp in cps:
                cp.start()
            for cp in cps:
                cp.wait()

        xv = x_ref[...]
        xh = xv * _rinv(xv)
        hk = (xh * nkv[...]).astype(BF16)
        hb = (xh * nb[...]).astype(BF16)
        hk_o[...] = hk
        hb_o[...] = hb
        cos, sin = cos_ref[...], sin_ref[...]
        ckr = _dot(hk, wdkv[...])
        ckr_o[...] = ckr
        ckv_pre = ckr[:, :kvr]
        ckv = (ckv_pre * _rinv(ckv_pre) * kvn[...]).astype(BF16)
        ckv_o[...] = ckv
        kr = ckr[:, kvr:]
        kr_o[...] = (kr * cos + _swap_halves(kr) * sin).astype(BF16)
        kn_o[...] = _dot(ckv, wuk[...]).astype(BF16)
        v_o[...] = _dot(ckv, wuv[...]).astype(BF16)
        ub = _dot_nt(hb, winb[...])
        ub_o[...] = ub
        cq_pre = ub[:, :qr]
        cq = (cq_pre * _rinv(cq_pre) * qn[...]).astype(BF16)
        cq_o[...] = cq
        qn_o[...] = (_dot(cq, wuqn[...]) * Q_SCALE).astype(BF16)
        qrope = _dot(cq, wuqr[...]) * Q_SCALE
        qr_o[...] = (qrope * jnp.tile(cos, (1, nheads)) + _swap_halves(qrope) * jnp.tile(sin, (1, nheads))).astype(BF16)

    tok = lambda c: pl.BlockSpec((tb, c), lambda i: (i, 0))
    pos = pl.BlockSpec((tb, LANE), lambda i: (i % npos, 0))
    consts = [w["norm_kv"], w["norm_b"], w["w_dkv_p"], w["kv_norm"], w["q_norm"]]
    outs = [(hv, BF16), (hv, BF16), (hv, BF16), (LANE, BF16), (hv, BF16), (qr + hv, F32), (kvr + LANE, F32), (d, BF16), (d, BF16), (qr, BF16), (kvr, BF16)]
    return pl.pallas_call(
        body, name="fb_fwd", grid=(t_all // tb,),
        in_specs=[tok(d), ANY] + [_full(c.shape) for c in consts] + [pos, pos],
        out_specs=[tok(c) for c, _ in outs],
        out_shape=[_sds((t_all, c), dt) for c, dt in outs],
        scratch_shapes=[pltpu.VMEM((qr + hv, d), BF16), pltpu.VMEM((kvr, d), BF16), pltpu.VMEM((kvr, d), BF16), pltpu.VMEM((qr, d), BF16),
                        pltpu.VMEM((qr, d), BF16), pltpu.SemaphoreType.DMA((5 * N_CHIPS,))],
        compiler_params=_params(1),
    )(x1, wg, *consts, cos_t, sin_t)


def _causal_mask(row0, col0, nrows, ncols):
    rows = row0 + lax.broadcasted_iota(jnp.int32, (nrows, ncols), 0)
    cols = col0 + lax.broadcasted_iota(jnp.int32, (nrows, ncols), 1)
    return cols <= rows


def _attn_fwd(qn, qr, kn, kr, v, seq, ta):
    t_all, hv = qn.shape
    nheads, nb, na = hv // LANE, t_all // seq, seq // ta

    reps = ta // LANE
    nchunks = ta // ATTN_ROWS

    def body(qn_ref, qr_ref, kn_ref, kr_ref, v_ref, o_ref, lse_ref, s_s, p_s, m_s, l_s, al_s, acc_s):
        i = pl.program_id(2)
        m_s[...] = jnp.full((ta, LANE), -1e30, F32)
        l_s[...] = jnp.zeros((ta, LANE), F32)
        acc_s[...] = jnp.zeros((ta, LANE), F32)
        q = jnp.concatenate([qn_ref[...], qr_ref[...]], axis=1)

        def tile(j, diagonal):
            cols = pl.ds(pl.multiple_of(j * ta, ta), ta)
            k = jnp.concatenate([kn_ref[cols, :], kr_ref[cols, :]], axis=1)
            s_s[...] = _dot_nt(q, k)

            def chunk(c, carry):
                rows = pl.ds(pl.multiple_of(c * ATTN_ROWS, ATTN_ROWS), ATTN_ROWS)
                s = s_s[rows, :]
                if diagonal:
                    s = jnp.where(_causal_mask(c * ATTN_ROWS, 0, ATTN_ROWS, ta), s, -1e30)
                m_prev = m_s[rows, :]
                m_new = jnp.maximum(m_prev, jnp.max(s, axis=1, keepdims=True))
                p = jnp.exp2(s - jnp.tile(m_new, (1, reps)))
                alpha = jnp.exp2(m_prev - m_new)
                l_s[rows, :] = alpha * l_s[rows, :] + jnp.sum(p, axis=1, keepdims=True)
                p_s[rows, :] = p.astype(BF16)
                m_s[rows, :] = m_new
                al_s[rows, :] = alpha
                return carry

            lax.fori_loop(0, nchunks, chunk, 0)
            acc_s[...] = al_s[...] * acc_s[...] + _dot(p_s[...], v_ref[cols, :])

        def off_diagonal(j, carry):
            tile(j, False)
            return carry

        lax.fori_loop(0, i, off_diagonal, 0)
        tile(i, True)
        o_ref[...] = acc_s[...] / l_s[...]
        lse_ref[...] = m_s[...] + jnp.log2(l_s[...])

    qspec = pl.BlockSpec((ta, LANE), lambda b, h, i: (b * na + i, h))
    kspec = pl.BlockSpec((seq, LANE), lambda b, h, i: (b, h))
    krspec = pl.BlockSpec((seq, LANE), lambda b, h, i: (b, 0))
    return pl.pallas_call(
        body, name="attn_fwd", grid=(nb, nheads, na),
        in_specs=[qspec, qspec, kspec, krspec, kspec],
        out_specs=[qspec, qspec],
        out_shape=[_sds((t_all, hv)), _sds((t_all, hv))],
        scratch_shapes=[pltpu.VMEM((ta, ta), F32), pltpu.VMEM((ta, ta), BF16)] + [pltpu.VMEM((ta, LANE), F32)] * 4,
        compiler_params=_params(3),
    )(qn, qr, kn, kr, v)


def _attn_bwd(qn, qr, kn, kr, v, do, lse, delta, seq, ta):
    t_all, hv = qn.shape
    nheads, nb, na = hv // LANE, t_all // seq, seq // ta

    reps = ta // LANE
    nchunks = ta // ATTN_ROWS

    def body(qn_ref, qr_ref, kn_ref, kr_ref, v_ref, do_ref, lse_ref, dl_ref, dqn_ref, dqr_ref, dkn_ref, dkr_ref, dv_ref,
             s_s, dp_s, p_s, ds_s, dk_s, dv_s):
        j = pl.program_id(2)

        @pl.when(j == 0)
        def _():
            dqn_ref[...] = jnp.zeros((seq, LANE), F32)
            dqr_ref[...] = jnp.zeros((seq, LANE), F32)

        dk_s[...] = jnp.zeros((ta, 2 * LANE), F32)
        dv_s[...] = jnp.zeros((ta, LANE), F32)
        k = jnp.concatenate([kn_ref[...], kr_ref[...]], axis=1)
        vv = v_ref[...]

        def tile(i, diagonal):
            rows_i = pl.ds(pl.multiple_of(i * ta, ta), ta)
            q = jnp.concatenate([qn_ref[rows_i, :], qr_ref[rows_i, :]], axis=1)
            do_b = do_ref[rows_i, :]
            s_s[...] = _dot_nt(q, k)
            dp_s[...] = _dot_nt(do_b, vv)

            def chunk(c, carry):
                rows = pl.ds(pl.multiple_of(c * ATTN_ROWS, ATTN_ROWS), ATTN_ROWS)
                seq_rows = pl.ds(pl.multiple_of(i * ta + c * ATTN_ROWS, ATTN_ROWS), ATTN_ROWS)
                s = s_s[rows, :]
                if diagonal:
                    s = jnp.where(_causal_mask(c * ATTN_ROWS, 0, ATTN_ROWS, ta), s, -1e30)
                p = jnp.exp2(s - jnp.tile(lse_ref[seq_rows, :], (1, reps)))
                p_s[rows, :] = p.astype(BF16)
                ds_s[rows, :] = (p * (dp_s[rows, :] - jnp.tile(dl_ref[seq_rows, :], (1, reps)))).astype(BF16)
                return carry

            lax.fori_loop(0, nchunks, chunk, 0)
            dv_s[...] += _dot_tn(p_s[...], do_b)
            ds = ds_s[...]
            dk_s[...] += _dot_tn(ds, q)
            dq = _dot(ds, k)
            dqn_ref[rows_i, :] += dq[:, :LANE]
            dqr_ref[rows_i, :] += dq[:, LANE:]

        def off_diagonal(i, carry):
            tile(i, False)
            return carry

        tile(j, True)
        lax.fori_loop(j + 1, na, off_diagonal, 0)
        dkn_ref[...] = dk_s[:, :LANE] * LN2
        dkr_ref[...] = dk_s[:, LANE:] * LN2
        dv_ref[...] = dv_s[...]

    qspec = pl.BlockSpec((seq, LANE), lambda b, h, j: (b, h))
    kspec = pl.BlockSpec((ta, LANE), lambda b, h, j: (b * na + j, h))
    krspec = pl.BlockSpec((ta, LANE), lambda b, h, j: (b * na + j, 0))
    return pl.pallas_call(
        body, name="attn_bwd", grid=(nb, nheads, na),
        in_specs=[qspec, qspec, kspec, krspec, kspec, qspec, qspec, qspec],
        out_specs=[qspec, qspec, kspec, kspec, kspec],
        out_shape=[_sds((t_all, hv))] * 5,
        scratch_shapes=[pltpu.VMEM((ta, ta), F32), pltpu.VMEM((ta, ta), F32), pltpu.VMEM((ta, ta), BF16), pltpu.VMEM((ta, ta), BF16),
                        pltpu.VMEM((ta, 2 * LANE), F32), pltpu.VMEM((ta, LANE), F32)],
        compiler_params=_params(3),
    )(qn, qr, kn, kr, v, do, lse, delta)


def _head(o, ub, x1, target, wg, w, lay, tb):
    t_all, d = x1.shape
    hv, qr = lay.hv, lay.qr
    nheads = hv // LANE

    def body(o_ref, ub_ref, x1_ref, tg_ref, wg_ref, gf, loss_ref, dgf_ref, yb_ref, dx2_ref, do_ref, dg_ref, dl_ref, wob, sems):
        @pl.when(pl.program_id(0) == 0)
        def _():
            cps = _fetch(wg_ref, lay, "out_b", wob, sems, 0)
            for cp in cps:
                cp.start()
            loss_ref[...] = jnp.zeros((1, LANE), F32)
            dgf_ref[...] = jnp.zeros((1, d), F32)
            for cp in cps:
                cp.wait()

        ov = o_ref[...]
        g = ub_ref[:, qr:]
        sg = jax.nn.sigmoid(g)
        silu = g * sg
        yb = (ov * silu).astype(BF16)
        yb_ref[...] = yb
        x2 = x1_ref[...] + _dot(yb, wob[...])
        rinv = _rinv(x2)
        err = x2 * rinv * gf[...] - tg_ref[...]
        loss_ref[...] += (0.5 / d) * jnp.sum(jnp.sum(err * err, axis=1, keepdims=True), axis=0, keepdims=True)
        dx2, dgf = _rms_bwd(x2, rinv, gf[...], err * (1.0 / d))
        dgf_ref[...] += dgf
        dx2_ref[...] = dx2
        dyb = _dot_nt(dx2.astype(BF16), wob[...])
        dov = dyb * silu
        do_ref[...] = dov.astype(BF16)
        dg_ref[...] = dyb * ov * (sg * (1.0 + g * (1.0 - sg)))
        prod = dov * ov
        dl_ref[...] = jnp.concatenate(
            [jnp.broadcast_to(jnp.sum(prod[:, n * LANE:(n + 1) * LANE], axis=1, keepdims=True), (tb, LANE)) for n in range(nheads)], axis=1)

    tok = lambda c: pl.BlockSpec((tb, c), lambda i: (i, 0))
    return pl.pallas_call(
        body, name="head", grid=(t_all // tb,),
        in_specs=[tok(hv), tok(qr + hv), tok(d), tok(d), ANY, _full((1, d))],
        out_specs=[_full((1, LANE)), _full((1, d)), tok(hv), tok(d), tok(hv), tok(hv), tok(hv)],
        out_shape=[_sds((1, LANE)), _sds((1, d)), _sds((t_all, hv), BF16), _sds((t_all, d)), _sds((t_all, hv), BF16), _sds((t_all, hv)), _sds((t_all, hv))],
        scratch_shapes=[pltpu.VMEM((hv, d), BF16), pltpu.SemaphoreType.DMA((N_CHIPS,))],
        compiler_params=_params(1),
    )(o, ub, x1, target, wg, w["final_norm"])


def _fb_bwd(dqn, dqr, dkn, dkr, dv, dgate, ub, ckr, x1, dx2, wg, w, lay, cos_t, sin_t, seq, tb):
    t_all, d = x1.shape
    hv, qr, kvr = lay.hv, lay.qr, lay.kvr
    nheads = hv // LANE
    npos = seq // tb

    def body(dqn_ref, dqr_ref, dkn_ref, dkr_ref, dv_ref, dg_ref, ub_ref, ckr_ref, x1_ref, dx2_ref, wg_ref,
             qn, nb, kvn, wdkv, nkv, cos_ref, sin_ref,
             dx1_ref, dqrp_ref, dqnp_ref, dub_ref, dckr_ref, dqn_g, dnb_g, dkvn_g, dnkv_g, winb, wuk, wuv, wuqn, wuqr, sems):
        @pl.when(pl.program_id(0) == 0)
        def _():
            cps = []
            for n, (key, dst) in enumerate((("in_b", winb), ("uk", wuk), ("uv", wuv), ("uq_n", wuqn), ("uq_r", wuqr))):
                cps += _fetch(wg_ref, lay, key, dst, sems, n * N_CHIPS)
            for cp in cps:
                cp.start()
            dqn_g[...] = jnp.zeros((1, qr), F32)
            dnb_g[...] = jnp.zeros((1, d), F32)
            dkvn_g[...] = jnp.zeros((1, kvr), F32)
            dnkv_g[...] = jnp.zeros((1, d), F32)
            for cp in cps:
                cp.wait()

        cos, sin = cos_ref[...], sin_ref[...]
        xv = x1_ref[...]
        rinv1 = _rinv(xv)
        dqr_v = dqr_ref[...] * ATTN_SCALE
        dqr_pre = (dqr_v * jnp.tile(cos, (1, nheads)) + _swap_halves(dqr_v * jnp.tile(sin, (1, nheads)))).astype(BF16)
        dqrp_ref[...] = dqr_pre
        dqn_pre = (dqn_ref[...] * ATTN_SCALE).astype(BF16)
        dqnp_ref[...] = dqn_pre
        dcq = _dot_nt(dqn_pre, wuqn[...]) + _dot_nt(dqr_pre, wuqr[...])
        cq_pre = ub_ref[:, :qr]
        dcq_pre, g1 = _rms_bwd(cq_pre, _rinv(cq_pre), qn[...], dcq)
        dqn_g[...] += g1
        dub = jnp.concatenate([dcq_pre, dg_ref[...]], axis=1).astype(BF16)
        dub_ref[...] = dub
        dx1_b, g2 = _rms_bwd(xv, rinv1, nb[...], _dot(dub, winb[...]))
        dnb_g[...] += g2
        dkr_all = dkr_ref[...]
        dkr_sum = dkr_all[:, :LANE]
        for n in range(1, nheads):
            dkr_sum = dkr_sum + dkr_all[:, n * LANE:(n + 1) * LANE]
        dckr_rope = dkr_sum * cos + _swap_halves(dkr_sum * sin)
        dckv = _dot_nt(dkn_ref[...].astype(BF16), wuk[...]) + _dot_nt(dv_ref[...].astype(BF16), wuv[...])
        ckv_pre = ckr_ref[:, :kvr]
        dckv_pre, g3 = _rms_bwd(ckv_pre, _rinv(ckv_pre), kvn[...], dckv)
        dkvn_g[...] += g3
        dckr = jnp.concatenate([dckv_pre, dckr_rope], axis=1).astype(BF16)
        dckr_ref[...] = dckr
        dx1_kv, g4 = _rms_bwd(xv, rinv1, nkv[...], _dot_nt(dckr, wdkv[...]))
        dnkv_g[...] += g4
        dx1_ref[...] = dx2_ref[...] + dx1_b + dx1_kv

    tok = lambda c: pl.BlockSpec((tb, c), lambda i: (i, 0))
    pos = pl.BlockSpec((tb, LANE), lambda i: (i % npos, 0))
    consts = [w["q_norm"], w["norm_b"], w["kv_norm"], w["w_dkv_p"], w["norm_kv"]]
    return pl.pallas_call(
        body, name="fb_bwd", grid=(t_all // tb,),
        in_specs=[tok(hv)] * 6 + [tok(qr + hv), tok(kvr + LANE), tok(d), tok(d), ANY] + [_full(c.shape) for c in consts] + [pos, pos],
        out_specs=[tok(d), tok(hv), tok(hv), tok(qr + hv), tok(kvr + LANE), _full((1, qr)), _full((1, d)), _full((1, kvr)), _full((1, d))],
        out_shape=[_sds((t_all, d)), _sds((t_all, hv), BF16), _sds((t_all, hv), BF16), _sds((t_all, qr + hv), BF16), _sds((t_all, kvr + LANE), BF16),
                   _sds((1, qr)), _sds((1, d)), _sds((1, kvr)), _sds((1, d))],
        scratch_shapes=[pltpu.VMEM((qr + hv, d), BF16), pltpu.VMEM((kvr, d), BF16), pltpu.VMEM((kvr, d), BF16), pltpu.VMEM((qr, d), BF16),
                        pltpu.VMEM((qr, d), BF16), pltpu.SemaphoreType.DMA((5 * N_CHIPS,))],
        compiler_params=_params(1),
    )(dqn, dqr, dkn, dkr, dv, dgate, ub, ckr, x1, dx2, wg, *consts, cos_t, sin_t)


def _fa_bwd(dx1, x, u, hs, wg, w, lay, seq, tb):
    t_all, d = x.shape
    dr = lay.dr
    nblocks = w["w_rg"].shape[0]
    nblk = seq // tb
    nt = tb // SUBLANE
    per8 = tb // 8

    def body(dx1_ref, x_ref, u_ref, uh_ref, hs_ref, hh_ref, wg_ref, na, cw, cb, wrg, brg, wig, big, lam,
             gx_ref, du_ref, dna_g, dcw_g, dcb_g, dbrg_g, dbig_g, dlam_g, dwrg_g, dwig_g,
             wint, wout, xpad, hpad, a_s, d_s, g_s, dxpad, carry, sems):
        b, jj = pl.program_id(0), pl.program_id(1)
        first_block = jj == nblk - 1

        @pl.when((b == 0) & (jj == 0))
        def _():
            cps = _fetch(wg_ref, lay, "in_a", wint, sems, 0) + _fetch(wg_ref, lay, "out_a", wout, sems, N_CHIPS)
            for cp in cps:
                cp.start()
            dna_g[...] = jnp.zeros((1, d), F32)
            dcw_g[...] = jnp.zeros((4, dr), F32)
            dcb_g[...] = jnp.zeros((1, dr), F32)
            dbrg_g[...] = jnp.zeros((1, dr), F32)
            dbig_g[...] = jnp.zeros((1, dr), F32)
            dlam_g[...] = jnp.zeros((1, dr), F32)
            dwrg_g[...] = jnp.zeros((nblocks, LANE, LANE), F32)
            dwig_g[...] = jnp.zeros((nblocks, LANE, LANE), F32)
            for cp in cps:
                cp.wait()

        @pl.when(jj == 0)
        def _():
            dxpad[pl.ds(tb, 8), :] = jnp.zeros((8, dr), F32)
            carry[...] = jnp.zeros((8, dr), F32)

        keep = jnp.where(first_block, 0.0, 1.0)
        dx1v = dx1_ref[...]
        gate = u_ref[:, dr:]
        xpad[pl.ds(0, 8), :] = uh_ref[...] * keep
        xpad[pl.ds(8, tb), :] = u_ref[:, :dr]
        hpad[pl.ds(0, 8), :] = hh_ref[...] * keep
        hpad[pl.ds(8, tb), :] = hs_ref[...]
        xb = _conv(xpad, cw, cb[...], tb)
        xbb = xb.astype(BF16)
        r, i = _gates(xb, wrg, brg[...], wig, big[...], nblocks)
        sp = _softplus(-lam[...])
        log_a = -LRU_C * r * sp
        a = jnp.exp(log_a)
        nem = _neg_expm1(2.0 * log_a)
        mult = jnp.sqrt(nem)
        sg = jax.nn.sigmoid(gate)
        dy = _dot_nt(dx1v.astype(BF16), wout[...])
        hsv = hs_ref[...]
        dgate = dy * hsv * (sg * (1.0 + gate * (1.0 - sg)))
        a_s[...] = a
        d_s[...] = dy * (gate * sg)
        row = lax.broadcasted_iota(jnp.int32, (8, dr), 0)

        def step(k, c):
            r0 = pl.multiple_of((nt - 1 - k) * 8, 8)
            av = a_s[pl.ds(r0, 8), :]
            dv = d_s[pl.ds(r0, 8), :]
            qv = av * dv
            for s in (1, 2, 4):
                m = row < 8 - s
                a_sh = jnp.where(m, pltpu.roll(av, 8 - s, 0), 1.0)
                q_sh = jnp.where(m, pltpu.roll(qv, 8 - s, 0), 0.0)
                qv = qv + av * q_sh
                av = av * a_sh
            qv = qv + av * c
            g_s[pl.ds(r0, 8), :] = dv + jnp.where(row < 7, pltpu.roll(qv, 7, 0), c)
            return jnp.broadcast_to(qv[0:1, :], qv.shape)

        carry[...] = lax.fori_loop(0, nt, step, carry[...])
        g = g_s[...]
        ix = i * xb
        dlog_a = g * hpad[pl.ds(7, tb), :] * a - g * ix * ((1.0 - nem) / mult)
        dix = g * mult
        dlam_g[...] += -jax.nn.sigmoid(-lam[...]) * jnp.sum(dlog_a * (-LRU_C * r), axis=0, keepdims=True)
        drg = dlog_a * (-LRU_C * sp) * r * (1.0 - r)
        dig = dix * xb * i * (1.0 - i)
        dbrg_g[...] += jnp.sum(drg, axis=0, keepdims=True)
        dbig_g[...] += jnp.sum(dig, axis=0, keepdims=True)
        drgb, digb = drg.astype(BF16), dig.astype(BF16)
        back = []
        for n in range(nblocks):
            cols = slice(n * LANE, (n + 1) * LANE)
            dwrg_g[n] += _dot_tn(xbb[:, cols], drgb[:, cols])
            dwig_g[n] += _dot_tn(xbb[:, cols], digb[:, cols])
            back.append(_dot_nt(drgb[:, cols], wrg[n]) + _dot_nt(digb[:, cols], wig[n]))
        dxb = dix * i + jnp.concatenate(back, axis=1)
        dcb_g[...] += jnp.sum(dxb, axis=0, keepdims=True)
        for k in range(4):
            dcw_g[k:k + 1, :] += jnp.sum(dxb * xpad[pl.ds(5 + k, tb), :], axis=0, keepdims=True)
        dxpad[pl.ds(0, tb), :] = dxb
        dxpre = (cw[3:4, :] * dxb + cw[2:3, :] * dxpad[pl.ds(1, tb), :] + cw[1:2, :] * dxpad[pl.ds(2, tb), :]
                 + cw[0:1, :] * dxpad[pl.ds(3, tb), :])
        dxpad[pl.ds(tb, 8), :] = dxb[:8, :]
        du = jnp.concatenate([dxpre, dgate], axis=1).astype(BF16)
        du_ref[...] = du
        xv = x_ref[...]
        dxa, g1 = _rms_bwd(xv, _rinv(xv), na[...], _dot(du, wint[...]))
        dna_g[...] += g1
        gx_ref[...] = dx1v + dxa

    blk = lambda b, j: b * nblk + (nblk - 1 - j)
    tok = lambda c: pl.BlockSpec((tb, c), lambda b, j: (blk(b, j), 0))
    halo = pl.BlockSpec((8, dr), lambda b, j: (jnp.maximum(blk(b, j) * per8 - 1, 0), 0))
    consts = [w["norm_a"], w["conv_w"], w["conv_b"], w["w_rg"], w["b_rg"], w["w_ig"], w["b_ig"], w["lru_lambda"]]
    vec = lambda c: _full((1, c))
    blocks3 = (nblocks, LANE, LANE)
    return pl.pallas_call(
        body, name="fa_bwd", grid=(t_all // seq, nblk),
        in_specs=[tok(d), tok(d), tok(2 * dr), halo, tok(dr), halo, ANY] + [_full(c.shape) for c in consts],
        out_specs=[tok(d), tok(2 * dr), vec(d), _full((4, dr)), vec(dr), vec(dr), vec(dr), vec(dr), _full(blocks3), _full(blocks3)],
        out_shape=[_sds((t_all, d)), _sds((t_all, 2 * dr), BF16), _sds((1, d)), _sds((4, dr)), _sds((1, dr)), _sds((1, dr)), _sds((1, dr)),
                   _sds((1, dr)), _sds(blocks3), _sds(blocks3)],
        scratch_shapes=[pltpu.VMEM((2 * dr, d), BF16), pltpu.VMEM((dr, d), BF16), pltpu.VMEM((tb + 8, dr), F32), pltpu.VMEM((tb + 8, dr), F32),
                        pltpu.VMEM((tb, dr), F32), pltpu.VMEM((tb, dr), F32), pltpu.VMEM((tb, dr), F32), pltpu.VMEM((tb + 8, dr), F32),
                        pltpu.VMEM((8, dr), F32), pltpu.SemaphoreType.DMA((2 * N_CHIPS,))],
        compiler_params=_params(2),
    )(dx1, x, u, u, hs, hs, wg, *consts)


def _mm_into(gbuf, a, b, off, name, bt):
    t_all, m = a.shape
    n = b.shape[1]
    nh = n // 2
    nt = t_all // bt

    def body(a_ref, b_ref, g_ref, o_ref, acc, sems):
        del g_ref
        half, t = pl.program_id(0), pl.program_id(1)

        def out_copy(h):
            return pltpu.make_async_copy(acc.at[h], o_ref.at[pl.ds(off, m), pl.ds(h * nh, nh)], sems.at[h])

        prod = _dot_tn(a_ref[...].astype(BF16), b_ref[...].astype(BF16))
        for h in range(2):
            @pl.when((half == h) & (t == 0))
            def _():
                acc[h] = prod

            @pl.when((half == h) & (t > 0))
            def _():
                acc[h] += prod

            @pl.when((half == h) & (t == nt - 1))
            def _():
                out_copy(h).start()

        @pl.when((half == 1) & (t == nt - 1))
        def _():
            out_copy(0).wait()
            out_copy(1).wait()

    return pl.pallas_call(
        body, name=name, grid=(2, nt),
        in_specs=[pl.BlockSpec((bt, m), lambda h, t: (t, 0)), pl.BlockSpec((bt, nh), lambda h, t: (t, h)), ANY],
        out_specs=ANY, out_shape=_sds(gbuf.shape), input_output_aliases={2: 0},
        scratch_shapes=[pltpu.VMEM((2, m, nh), F32), pltpu.SemaphoreType.DMA((2,))],
        compiler_params=_params(2),
    )(a, b, gbuf)


def _mm_tn(a, b, name, bt):
    t_all, m = a.shape
    n = b.shape[1]

    def body(a_ref, b_ref, o_ref):
        @pl.when(pl.program_id(0) == 0)
        def _():
            o_ref[...] = jnp.zeros((m, n), F32)

        o_ref[...] += _dot_tn(a_ref[...].astype(BF16), b_ref[...].astype(BF16))

    return pl.pallas_call(
        body, name=name, grid=(t_all // bt,),
        in_specs=[pl.BlockSpec((bt, m), lambda t: (t, 0)), pl.BlockSpec((bt, n), lambda t: (t, 0))],
        out_specs=_full((m, n)), out_shape=_sds((m, n)),
        compiler_params=_params(1),
    )(a, b)


def _all_gather8(blocks, name):
    nb = len(blocks)

    def body(*refs):
        x_refs, out_refs = refs[:nb], refs[nb:2 * nb]
        send_sems, recv_sems, local_sems = refs[2 * nb:]
        x, y, c = _place()
        me, sibling = (x, y, c), (x, y, 1 - c)
        chips = [(1 - x, y), (x, 1 - y), (1 - x, 1 - y)]
        waits = []
        for n in range(nb):
            x_ref, out_ref = x_refs[n], out_refs[n]

            def slot(px, py, pc, out_ref=out_ref):
                return out_ref.at[4 * px + 2 * py + pc]

            def copy(k, blk, to, src=None, n=n, slot=slot):
                return pltpu.make_async_remote_copy(
                    src_ref=slot(*blk) if src is None else src, dst_ref=slot(*blk), send_sem=send_sems.at[n, k], recv_sem=recv_sems.at[n, k],
                    device_id=to, device_id_type=MESH)

            mine = pltpu.make_async_copy(x_ref, slot(*me), local_sems.at[n])
            mine.start()
            first = [copy(0, me, sibling, src=x_ref)] + [copy(1 + j, me, (*chip, c), src=x_ref) for j, chip in enumerate(chips)]
            for cp in first:
                cp.start()
            waits.append((copy, mine, first))
        for n in range(nb):
            copy, mine, first = waits[n]
            passed = [copy(4 + j, (*chip, c), sibling) for j, chip in enumerate(chips)]
            for j, chip in enumerate(chips):
                copy(1 + j, (*chip, c), me).wait_recv()
                passed[j].start()
            copy(0, sibling, me).wait_recv()
            for j, chip in enumerate(chips):
                copy(4 + j, (*chip, 1 - c), me).wait_recv()
            for cp in first + passed:
                cp.wait_send()
            mine.wait()

    return pl.pallas_call(
        body, name=name, out_shape=[_sds((8,) + b.shape, b.dtype) for b in blocks], in_specs=[ANY] * nb, out_specs=[ANY] * nb,
        scratch_shapes=[pltpu.SemaphoreType.DMA((nb, 7)), pltpu.SemaphoreType.DMA((nb, 7)), pltpu.SemaphoreType.DMA((nb,))],
    )(*blocks)


def _swap_sibling(src, name, half_cols=False):
    rows, cols = src.shape
    half = cols // 2 if half_cols else cols

    def body(src_ref, out_ref, send_sem, recv_sem):
        x, y, c = _place()
        part = src_ref.at[:, pl.ds(pl.multiple_of((1 - c) * half, LANE), half)] if half_cols else src_ref
        cp = pltpu.make_async_remote_copy(src_ref=part, dst_ref=out_ref, send_sem=send_sem, recv_sem=recv_sem,
                                          device_id=(x, y, 1 - c), device_id_type=MESH)
        cp.start()
        cp.wait()

    return pl.pallas_call(
        body, name=name, out_shape=_sds((rows, half), src.dtype), in_specs=[ANY], out_specs=ANY,
        scratch_shapes=[pltpu.SemaphoreType.DMA, pltpu.SemaphoreType.DMA],
    )(src)


def _scatter_chips(part16, part32, lay):
    half = part16.shape[1]

    def body(p16_ref, p32_ref, got_ref, own_ref, send_sems, recv_sems, local_sem):
        x, y, c = _place()
        chips = [(1 - x, y), (x, 1 - y), (1 - x, 1 - y)]

        def rows_of(ref, key, chip):
            start = pl.multiple_of(lay.g_off[key] + chip * lay.rows[key], ROW_ALIGN)
            return ref.at[pl.ds(start, lay.rows[key]), :]

        def compact(ref, key):
            return ref.at[pl.ds(lay.c_off[key], lay.rows[key]), :]

        for key in G_ORDER:
            pltpu.make_async_copy(rows_of(p32_ref, key, 2 * x + y), compact(own_ref, key), local_sem).start()
        for k, (px, py) in enumerate(chips):
            for key in G_ORDER:
                pltpu.make_async_remote_copy(src_ref=rows_of(p16_ref, key, 2 * px + py), dst_ref=compact(got_ref.at[k], key),
                                             send_sem=send_sems.at[k], recv_sem=recv_sems.at[k], device_id=(px, py, c), device_id_type=MESH).start()
        for k, (px, py) in enumerate(chips):
            pltpu.make_async_remote_copy(src_ref=got_ref.at[k], dst_ref=got_ref.at[k], send_sem=send_sems.at[k], recv_sem=recv_sems.at[k],
                                         device_id=(px, py, c), device_id_type=MESH).wait()
        pltpu.make_async_copy(own_ref, own_ref, local_sem).wait()

    return pl.pallas_call(
        body, name="rs_chips", out_shape=[_sds((3, lay.c_rows, half), BF16), _sds((lay.c_rows, half), F32)], in_specs=[ANY, ANY], out_specs=[ANY, ANY],
        scratch_shapes=[pltpu.SemaphoreType.DMA((3,)), pltpu.SemaphoreType.DMA((3,)), pltpu.SemaphoreType.DMA],
    )(part16, part32)


def _sum_sibling(gbuf, got, cidx):
    rows, d = gbuf.shape
    half = d // 2
    rb = _row_block(rows)

    def body(c_ref, g_ref, r_ref, o32_ref, o16_ref):
        del c_ref
        s = g_ref[...] + r_ref[...]
        o32_ref[...] = s
        o16_ref[...] = s.astype(BF16)

    plain = pl.BlockSpec((rb, half), lambda i, c: (i, 0))
    return pl.pallas_call(
        body, name="rs_sum_sibling",
        grid_spec=pltpu.PrefetchScalarGridSpec(num_scalar_prefetch=1, grid=(rows // rb,),
                                               in_specs=[pl.BlockSpec((rb, half), lambda i, c: (i, c[0])), plain], out_specs=[plain, plain]),
        out_shape=[_sds((rows, half)), _sds((rows, half), BF16)], compiler_params=_params(1),
    )(cidx, gbuf, got)


def _sum_chips(own, got):
    rows, half = own.shape
    rb = _row_block(rows)

    def body(a_ref, b_ref, o_ref):
        o_ref[...] = ((a_ref[...] + b_ref[0].astype(F32)) + b_ref[1].astype(F32)) + b_ref[2].astype(F32)

    spec = pl.BlockSpec((rb, half), lambda i: (i, 0))
    return pl.pallas_call(
        body, name="rs_sum_chips", grid=(rows // rb,), in_specs=[spec, pl.BlockSpec((3, rb, half), lambda i: (0, i, 0))], out_specs=spec,
        out_shape=_sds((rows, half)), compiler_params=_params(1),
    )(own, got)


def _adamw(w, g, m, v):
    m = ADAM_B1 * m + (1.0 - ADAM_B1) * g
    v = ADAM_B2 * v + (1.0 - ADAM_B2) * (g * g)
    m_hat = m / (1.0 - ADAM_B1 ** ADAM_STEP)
    v_hat = v / (1.0 - ADAM_B2 ** ADAM_STEP)
    return -ADAM_LR * (m_hat / (jnp.sqrt(v_hat) + ADAM_EPS) + ADAM_WD * w), m, v


def _adamw_rows(name, w, g, m, v):
    _, rows, cols = w.shape
    rb = _row_block(rows, 256)

    def body(w_ref, g_ref, m_ref, v_ref, d_ref, mo_ref, vo_ref):
        d_ref[...], mo_ref[...], vo_ref[...] = _adamw(w_ref[...], g_ref[...], m_ref[...], v_ref[...])

    spec = pl.BlockSpec((1, rb, cols), lambda i: (0, i, 0))
    return pl.pallas_call(
        body, name=name, grid=(rows // rb,), in_specs=[spec] * 4, out_specs=[spec] * 3, out_shape=[_sds(w.shape)] * 3,
        compiler_params=_params(1),
    )(w, g, m, v)


def _adamw_group(ws, gs, ms, vs):
    n = len(ws)

    def body(*refs):
        for k in range(n):
            w_ref, g_ref, m_ref, v_ref = (refs[j * n + k] for j in range(4))
            outs = _adamw(w_ref[...], g_ref[...], m_ref[...], v_ref[...])
            for j in range(3):
                refs[(4 + j) * n + k][...] = outs[j]

    outs = pl.pallas_call(
        body, name="adamw_small", out_shape=[_sds(w.shape) for w in ws] * 3,
        compiler_params=pltpu.CompilerParams(vmem_limit_bytes=VMEM_LIMIT),
    )(*ws, *gs, *ms, *vs)
    return outs[:n], outs[n:2 * n], outs[2 * n:]


def _gather_weights(sh, lay):
    x, y, c = _place()
    d = lay.d
    uq = sh["w_uq"][0].astype(BF16)
    parts = {
        "in_b": sh["w_in_b"][0].T.astype(BF16), "in_a": sh["w_in_a"][0].T.astype(BF16), "out_a": sh["w_out_a"][0].astype(BF16),
        "out_b": sh["w_out_b"][0].astype(BF16), "uk": sh["w_uk"].astype(BF16).reshape(-1, d), "uv": sh["w_uv"].astype(BF16).reshape(-1, d),
        "uq_n": uq[:, :, :QK_NOPE].reshape(-1, d), "uq_r": jnp.pad(uq[:, :, QK_NOPE:], ((0, 0), (0, 0), (0, LANE - QK_ROPE))).reshape(-1, d),
        "dkv": jnp.pad(sh["w_dkv"].astype(BF16), ((0, 0), (0, LANE - QK_ROPE))).reshape(-1, d),
    }
    stack = jnp.concatenate([parts[k] for k in W_ORDER], axis=0)
    stack = jnp.pad(stack, ((0, lay.w_rows - stack.shape[0]), (0, 0))).reshape(2, lay.w_rows // 2, d)
    small = jnp.concatenate([sh[k].reshape(-1) for k in SMALL])
    n_small = small.shape[0]
    width = _round_up(n_small, 2 * SUBLANE * LANE) // (2 * SUBLANE)
    small = jnp.pad(small, (0, 2 * SUBLANE * width - n_small)).reshape(2, SUBLANE, width)
    wg, sg = _all_gather8([lax.dynamic_index_in_dim(stack, c, 0, keepdims=False), lax.dynamic_index_in_dim(small, c, 0, keepdims=False)], "ag_weights")
    wg = wg.reshape(N_CHIPS, lay.w_rows, d)
    sg = sg.reshape(N_CHIPS, 2 * SUBLANE * width)
    full, off = {}, 0
    for k in SMALL:
        n = sh[k].size
        piece = sg[:, off:off + n]
        off += n
        if k == "conv_w":
            full[k] = piece.reshape(N_CHIPS, 4, n // 4).transpose(1, 0, 2).reshape(4, n)
        else:
            full[k] = piece.reshape(1, N_CHIPS * n)
    rows = lay.rows["dkv"]
    w_dkv_p = wg[:, lay.w_off["dkv"]:lay.w_off["dkv"] + rows, :].reshape(d, lay.kvr + LANE)
    return wg, w_dkv_p, full


def _chip_split(g, taps=False):
    if taps:
        n = g.shape[1] // N_CHIPS
        return g.reshape(4, N_CHIPS, n).transpose(1, 0, 2).reshape(N_CHIPS, 4 * n)
    return g.reshape(N_CHIPS, -1)


def kernel(x, norm_a, w_in_a, conv_w, conv_b, w_rg, b_rg, w_ig, b_ig, lru_lambda, w_out_a, norm_kv, w_dkv, kv_norm, w_uk, w_uv, norm_b, w_in_b, q_norm, w_uq, w_out_b, final_norm, loss_target, m_norm_a, m_w_in_a, m_conv_w, m_conv_b, m_w_rg, m_b_rg, m_w_ig, m_b_ig, m_lru_lambda, m_w_out_a, m_norm_kv, m_w_dkv, m_kv_norm, m_w_uk, m_w_uv, m_norm_b, m_w_in_b, m_q_norm, m_w_uq, m_w_out_b, m_final_norm, v_norm_a, v_w_in_a, v_conv_w, v_conv_b, v_w_rg, v_b_rg, v_w_ig, v_b_ig, v_lru_lambda, v_w_out_a, v_norm_kv, v_w_dkv, v_kv_norm, v_w_uk, v_w_uv, v_norm_b, v_w_in_b, v_q_norm, v_w_uq, v_w_out_b, v_final_norm):
    given = dict(locals())
    sh = {k: given[k] for k in WEIGHTS}
    xi, yi, ci = _place()
    nb, seq, d = x.shape
    t_all = nb * seq
    tb_a, tb_b, ta, bt = min(TOKENS_A, seq), min(TOKENS_B, seq), min(TOKENS_ATTN, seq), min(TOKENS_MM, t_all)
    dr = conv_b.shape[1] * N_CHIPS
    qr, kvr, nheads = q_norm.shape[1], kv_norm.shape[0], w_uk.shape[1]
    hv = nheads * LANE
    n_small = sum(sh[k].size for k in SMALL)
    n_repl = sum(sh[k].size for k in REPL)
    lay = _Layout(d, dr, qr, kvr, hv, n_small, n_repl)
    half = d // 2

    wg, w_dkv_p, w = _gather_weights(sh, lay)
    w.update({"w_rg": w_rg[0].astype(BF16), "w_ig": w_ig[0].astype(BF16), "w_dkv_p": w_dkv_p, "norm_kv": norm_kv[None, :],
              "kv_norm": kv_norm[None, :], "final_norm": final_norm[None, :], "norm_b": norm_b, "q_norm": q_norm})
    cos_t, sin_t = _rope_tables(seq)

    x0 = x.reshape(t_all, d)
    x1, u, hs, h, y = _fa_fwd(x0, wg, w, lay, seq, tb_a)
    qn, qrp, kn, kr, v, ub, ckr, hb, hk, cq, ckv = _fb_fwd(x1, wg, w, lay, cos_t, sin_t, seq, tb_b)
    o, lse = _attn_fwd(qn, qrp, kn, kr, v, seq, ta)
    loss, g_final_norm, yb, dx2, do, dgate, delta = _head(o, ub, x1, loss_target.reshape(t_all, d), wg, w, lay, tb_b)
    dqn, dqr, dkn, dkr, dv = _attn_bwd(qn, qrp, kn, kr, v, do, lse, delta, seq, ta)
    dx1, dqr_pre, dqn_pre, dub, dckr, g_q_norm, g_norm_b, g_kv_norm, g_norm_kv = _fb_bwd(
        dqn, dqr, dkn, dkr, dv, dgate, ub, ckr, x1, dx2, wg, w, lay, cos_t, sin_t, seq, tb_b)
    gx, du, g_norm_a, g_conv_w, g_conv_b, g_b_rg, g_b_ig, g_lam, g_w_rg, g_w_ig = _fa_bwd(dx1, x0, u, hs, wg, w, lay, seq, tb_a)
    loss = lax.psum(loss[0, 0], ("x", "y", "c"))

    g_dkv = _mm_tn(hk, dckr, "dw_dkv", bt)
    small = jnp.concatenate([_chip_split(g_norm_a), _chip_split(g_conv_w, taps=True), _chip_split(g_conv_b), _chip_split(g_b_rg),
                             _chip_split(g_b_ig), _chip_split(g_lam)], axis=1)
    small = jnp.pad(small, ((0, 0), (0, lay.small_rows * d - small.shape[1]))).reshape(N_CHIPS, lay.small_rows, d)
    repl_parts = {"w_rg": g_w_rg, "w_ig": g_w_ig, "norm_kv": g_norm_kv, "kv_norm": g_kv_norm, "norm_b": g_norm_b, "q_norm": g_q_norm,
                  "final_norm": g_final_norm}
    repl = jnp.concatenate([repl_parts[k].reshape(-1) for k in REPL])
    repl = jnp.pad(repl, (0, N_CHIPS * lay.repl_rows * d - n_repl)).reshape(N_CHIPS, lay.repl_rows, d)
    pad_rows = lay.rows["rest"] - lay.rows["dkv"] - lay.small_rows - lay.repl_rows
    rest = jnp.concatenate([g_dkv.reshape(N_CHIPS, lay.rows["dkv"], d), small, repl, jnp.zeros((N_CHIPS, pad_rows, d), F32)], axis=1)
    gbuf = lax.dynamic_update_slice(lax.empty((lay.g_rows, d), F32), rest.reshape(N_CHIPS * lay.rows["rest"], d), (lay.g_off["rest"], 0))
    for key, a, b in (("in_b", dub, hb), ("in_a", du, h), ("out_a", y, dx1), ("out_b", yb, dx2), ("uk", ckv, dkn), ("uv", ckv, dv),
                      ("uq_n", cq, dqn_pre), ("uq_r", cq, dqr_pre)):
        gbuf = _mm_into(gbuf, a, b, lay.g_off[key], "dw_" + key, bt)

    got = _swap_sibling(gbuf, "rs_sibling", half_cols=True)
    part32, part16 = _sum_sibling(gbuf, got, jnp.reshape(ci, (1,)).astype(jnp.int32))
    others, own = _scatter_chips(part16, part32, lay)
    mine = _sum_chips(own, others)
    theirs = _swap_sibling(mine, "rs_return")
    red = jnp.concatenate([jnp.where(ci == 0, mine, theirs), jnp.where(ci == 0, theirs, mine)], axis=1)
    r0 = lay.c_off["rest"] + lay.rows["dkv"] + lay.small_rows
    (rep_all,) = _all_gather8([mine[r0:r0 + lay.repl_rows]], "ag_rep")
    rep_flat = rep_all.reshape(N_CHIPS, 2, lay.repl_rows, half).transpose(0, 2, 1, 3).reshape(-1)

    def rows(key):
        return red[lay.c_off[key]:lay.c_off[key] + lay.rows[key]]

    grads = {"w_in_b": rows("in_b").T[None], "w_in_a": rows("in_a").T[None], "w_out_a": rows("out_a")[None], "w_out_b": rows("out_b")[None],
             "w_uk": rows("uk").reshape(w_uk.shape), "w_uv": rows("uv").reshape(w_uv.shape)}
    uq_n = rows("uq_n").reshape(qr // N_CHIPS, nheads, LANE)
    uq_r = rows("uq_r").reshape(qr // N_CHIPS, nheads, LANE)[:, :, :QK_ROPE]
    grads["w_uq"] = jnp.concatenate([uq_n, uq_r], axis=2)[None]
    rest_red = red[lay.c_off["rest"]:lay.c_off["rest"] + lay.rows["rest"]]
    grads["w_dkv"] = rest_red[:lay.rows["dkv"]].reshape(d // N_CHIPS, kvr + LANE)[:, :kvr + QK_ROPE]
    small_red = rest_red[lay.rows["dkv"]:lay.rows["dkv"] + lay.small_rows].reshape(-1)
    off = 0
    for k in SMALL:
        n = sh[k].size
        grads[k] = small_red[off:off + n].reshape(sh[k].shape)
        off += n
    off = 0
    for k in REPL:
        n = sh[k].size
        grads[k] = rep_flat[off:off + n].reshape(sh[k].shape)
        off += n

    new = {}
    for k in ("w_in_a", "w_in_b", "w_out_a", "w_out_b"):
        new[k] = _adamw_rows("adamw_" + k, sh[k], grads[k], given["m_" + k], given["v_" + k])
    rest_names = [k for k in WEIGHTS if k not in new]
    as2d = lambda a: a[None, :] if a.ndim == 1 else a
    ds, ms, vs = _adamw_group([as2d(sh[k]) for k in rest_names], [as2d(grads[k]) for k in rest_names],
                              [as2d(given["m_" + k]) for k in rest_names], [as2d(given["v_" + k]) for k in rest_names])
    for n, k in enumerate(rest_names):
        new[k] = tuple(a.reshape(sh[k].shape) for a in (ds[n], ms[n], vs[n]))
    return (loss, gx.reshape(nb, seq, d), *[grads[k] for k in WEIGHTS], *[new[k][0] for k in WEIGHTS], *[new[k][1] for k in WEIGHTS],
            *[new[k][2] for k in WEIGHTS])
```

```python
import jax
import jax.numpy as jnp
from jax import lax
from jax.experimental import pallas as pl
from jax.experimental.pallas import tpu as pltpu

F32, BF16 = jnp.float32, jnp.bfloat16
EPS = 1e-6
LRU_C = 8.0
ROPE_THETA = 10000.0
QK_NOPE, QK_ROPE = 128, 64
ATTN_SCALE = (QK_NOPE + QK_ROPE) ** -0.5
LN2 = 0.6931471805599453
Q_SCALE = ATTN_SCALE / LN2
ATTN_ROWS = 64
LANE = 128
SUBLANE = 8
ROW_ALIGN = 32
VMEM_LIMIT = 60000 * 1024
ADAM_LR, ADAM_B1, ADAM_B2, ADAM_EPS, ADAM_WD, ADAM_STEP = 0.001, 0.9, 0.999, 1e-08, 0.01, 10
MESH = pl.DeviceIdType.MESH
ANY = pl.BlockSpec(memory_space=pl.ANY)
N_CHIPS = 4
TOKENS_A, TOKENS_B, TOKENS_ATTN, TOKENS_MM = 256, 256, 512, 512

SMALL = ("norm_a", "conv_w", "conv_b", "b_rg", "b_ig", "lru_lambda")
REPL = ("w_rg", "w_ig", "norm_kv", "kv_norm", "norm_b", "q_norm", "final_norm")
WEIGHTS = ("norm_a", "w_in_a", "conv_w", "conv_b", "w_rg", "b_rg", "w_ig", "b_ig", "lru_lambda", "w_out_a", "norm_kv",
           "w_dkv", "kv_norm", "w_uk", "w_uv", "norm_b", "w_in_b", "q_norm", "w_uq", "w_out_b", "final_norm")
W_ORDER = ("in_b", "in_a", "out_a", "out_b", "uk", "uv", "uq_n", "uq_r", "dkv")
G_ORDER = ("in_b", "rest", "in_a", "out_a", "out_b", "uk", "uv", "uq_n", "uq_r")


def _sds(shape, dtype=F32):
    return jax.ShapeDtypeStruct(tuple(shape), dtype)


def _params(n_grid):
    return pltpu.CompilerParams(dimension_semantics=("arbitrary",) * n_grid, vmem_limit_bytes=VMEM_LIMIT)


def _full(shape):
    nd = len(shape)
    return pl.BlockSpec(tuple(shape), lambda *g: (0,) * nd)


def _round_up(n, k):
    return -(-n // k) * k


def _row_block(rows, cap=512):
    best = SUBLANE
    for r in range(SUBLANE, min(rows, cap) + 1, SUBLANE):
        if rows % r == 0:
            best = r
    return best


def _place():
    return lax.axis_index("x"), lax.axis_index("y"), lax.axis_index("c")


class _Layout:
    def __init__(self, d, dr, qr, kvr, hv, n_small, n_repl):
        assert hv == d, "the packed rows are D_MODEL wide, which must equal heads * 128"
        self.d, self.dr, self.qr, self.kvr, self.hv = d, dr, qr, kvr, hv
        per_chip = {"in_b": (qr + hv) // N_CHIPS, "in_a": 2 * dr // N_CHIPS, "out_a": dr // N_CHIPS, "out_b": hv // N_CHIPS,
                    "uk": kvr // N_CHIPS, "uv": kvr // N_CHIPS, "uq_n": qr // N_CHIPS, "uq_r": qr // N_CHIPS,
                    "dkv": (d // N_CHIPS) * (kvr + LANE) // d}
        assert all(r % ROW_ALIGN == 0 for r in per_chip.values()), per_chip
        self.small_rows = _round_up(-(-n_small // d), SUBLANE)
        self.repl_rows = _round_up(-(-n_repl // (N_CHIPS * d)), SUBLANE)
        per_chip["rest"] = _round_up(per_chip["dkv"] + self.small_rows + self.repl_rows, ROW_ALIGN)
        self.rows = per_chip
        self.w_off, off = {}, 0
        for k in W_ORDER:
            self.w_off[k] = off
            off += per_chip[k]
        self.w_rows = _round_up(off, ROW_ALIGN)
        self.g_off, self.c_off, off = {}, {}, 0
        for k in G_ORDER:
            self.c_off[k] = off
            self.g_off[k] = N_CHIPS * off
            off += per_chip[k]
        self.c_rows = off
        self.g_rows = N_CHIPS * off


def _dot(a, b):
    return jnp.dot(a, b, preferred_element_type=F32)


def _dot_nt(a, b):
    return lax.dot_general(a, b, (((1,), (1,)), ((), ())), preferred_element_type=F32)


def _dot_tn(a, b):
    return lax.dot_general(a, b, (((0,), (0,)), ((), ())), preferred_element_type=F32)


def _rinv(x):
    return lax.rsqrt(jnp.mean(x * x, axis=-1, keepdims=True) + EPS)


def _rms_bwd(x, rinv, g, dy):
    z = dy * g
    dx = rinv * z - x * (rinv * rinv * rinv) * jnp.mean(z * x, axis=-1, keepdims=True)
    dg = jnp.sum(dy * (x * rinv), axis=0, keepdims=True)
    return dx, dg


def _softplus(z):
    return jnp.maximum(z, 0.0) + jnp.log1p(jnp.exp(-jnp.abs(z)))


def _neg_expm1(z):
    series = -z * (1 + z / 2 * (1 + z / 3 * (1 + z / 4 * (1 + z / 5 * (1 + z / 6)))))
    return jnp.where(z > -0.3, series, 1.0 - jnp.exp(z))


def _swap_halves(x):
    w = x.shape[1]
    lane = lax.broadcasted_iota(jnp.int32, x.shape, 1)
    return jnp.where(lane % QK_ROPE < QK_ROPE // 2, pltpu.roll(x, w - QK_ROPE // 2, 1), pltpu.roll(x, QK_ROPE // 2, 1))


def _rope_tables(seq):
    pos = jnp.arange(seq, dtype=F32)
    inv = ROPE_THETA ** (-jnp.arange(0, QK_ROPE, 2, dtype=F32) / QK_ROPE)
    ang = pos[:, None] * inv[None, :]
    cos, sin = jnp.cos(ang), jnp.sin(ang)
    zero = jnp.zeros((seq, LANE - QK_ROPE), F32)
    return jnp.concatenate([cos, cos, zero], 1), jnp.concatenate([-sin, sin, zero], 1)


def _fetch(wg_ref, lay, key, dst, sems, k0):
    rows = lay.rows[key]
    return [pltpu.make_async_copy(wg_ref.at[p, pl.ds(lay.w_off[key], rows), :], dst.at[pl.ds(p * rows, rows), :], sems.at[k0 + p])
            for p in range(N_CHIPS)]


def _gates(xb, wrg_ref, brg, wig_ref, big, nblocks):
    xbb = xb.astype(BF16)
    rg = [_dot(xbb[:, n * LANE:(n + 1) * LANE], wrg_ref[n]) for n in range(nblocks)]
    ig = [_dot(xbb[:, n * LANE:(n + 1) * LANE], wig_ref[n]) for n in range(nblocks)]
    r = jax.nn.sigmoid(jnp.concatenate(rg, axis=1) + brg)
    i = jax.nn.sigmoid(jnp.concatenate(ig, axis=1) + big)
    return r, i


def _conv(xpad, cw_ref, cb, tb):
    return (cb + cw_ref[3:4, :] * xpad[pl.ds(8, tb), :] + cw_ref[2:3, :] * xpad[pl.ds(7, tb), :]
            + cw_ref[1:2, :] * xpad[pl.ds(6, tb), :] + cw_ref[0:1, :] * xpad[pl.ds(5, tb), :])


def _fa_fwd(x, wg, w, lay, seq, tb):
    t_all, d = x.shape
    dr = lay.dr
    nblocks = w["w_rg"].shape[0]
    nblk = seq // tb
    nt = tb // SUBLANE

    def body(x_ref, wg_ref, na, cw, cb, wrg, brg, wig, big, lam, x1_ref, u_ref, hs_ref, h_ref, y_ref,
             wint, wout, xpad, a_s, b_s, carry, sems):
        @pl.when((pl.program_id(0) == 0) & (pl.program_id(1) == 0))
        def _():
            cps = _fetch(wg_ref, lay, "in_a", wint, sems, 0) + _fetch(wg_ref, lay, "out_a", wout, sems, N_CHIPS)
            for cp in cps:
                cp.start()
            for cp in cps:
                cp.wait()

        @pl.when(pl.program_id(1) == 0)
        def _():
            xpad[pl.ds(0, 8), :] = jnp.zeros((8, dr), F32)
            carry[...] = jnp.zeros((8, dr), F32)

        xv = x_ref[...]
        h = (xv * _rinv(xv) * na[...]).astype(BF16)
        h_ref[...] = h
        u = _dot_nt(h, wint[...])
        u_ref[...] = u
        xpre, gate = u[:, :dr], u[:, dr:]
        xpad[pl.ds(8, tb), :] = xpre
        xb = _conv(xpad, cw, cb[...], tb)
        xpad[pl.ds(0, 8), :] = xpre[tb - 8:, :]
        r, i = _gates(xb, wrg, brg[...], wig, big[...], nblocks)
        log_a = -LRU_C * r * _softplus(-lam[...])
        a_s[...] = jnp.exp(log_a)
        b_s[...] = jnp.sqrt(_neg_expm1(2.0 * log_a)) * (i * xb)
        row = lax.broadcasted_iota(jnp.int32, (8, dr), 0)

        def step(t, c):
            r0 = pl.multiple_of(t * 8, 8)
            a = a_s[pl.ds(r0, 8), :]
            b = b_s[pl.ds(r0, 8), :]
            for s in (1, 2, 4):
                m = row >= s
                a_sh = jnp.where(m, pltpu.roll(a, s, 0), 1.0)
                b_sh = jnp.where(m, pltpu.roll(b, s, 0), 0.0)
                b = a * b_sh + b
                a = a * a_sh
            hh = b + a * c
            hs_ref[pl.ds(r0, 8), :] = hh
            return jnp.broadcast_to(hh[7:8, :], hh.shape)

        carry[...] = lax.fori_loop(0, nt, step, carry[...])
        y = (hs_ref[...] * (gate * jax.nn.sigmoid(gate))).astype(BF16)
        y_ref[...] = y
        x1_ref[...] = xv + _dot(y, wout[...])

    tok = lambda c: pl.BlockSpec((tb, c), lambda b, j: (b * nblk + j, 0))
    consts = [w["norm_a"], w["conv_w"], w["conv_b"], w["w_rg"], w["b_rg"], w["w_ig"], w["b_ig"], w["lru_lambda"]]
    return pl.pallas_call(
        body, name="fa_fwd", grid=(t_all // seq, nblk),
        in_specs=[tok(d), ANY] + [_full(c.shape) for c in consts],
        out_specs=[tok(d), tok(2 * dr), tok(dr), tok(d), tok(dr)],
        out_shape=[_sds((t_all, d)), _sds((t_all, 2 * dr)), _sds((t_all, dr)), _sds((t_all, d), BF16), _sds((t_all, dr), BF16)],
        scratch_shapes=[pltpu.VMEM((2 * dr, d), BF16), pltpu.VMEM((dr, d), BF16), pltpu.VMEM((tb + 8, dr), F32), pltpu.VMEM((tb, dr), F32),
                        pltpu.VMEM((tb, dr), F32), pltpu.VMEM((8, dr), F32), pltpu.SemaphoreType.DMA((2 * N_CHIPS,))],
        compiler_params=_params(2),
    )(x, wg, *consts)


def _fb_fwd(x1, wg, w, lay, cos_t, sin_t, seq, tb):
    t_all, d = x1.shape
    kvr, qr, hv = lay.kvr, lay.qr, lay.hv
    nheads = hv // LANE
    npos = seq // tb

    def body(x_ref, wg_ref, nkv, nb, wdkv, kvn, qn, cos_ref, sin_ref,
             qn_o, qr_o, kn_o, kr_o, v_o, ub_o, ckr_o, hb_o, hk_o, cq_o, ckv_o, winb, wuk, wuv, wuqn, wuqr, sems):
        @pl.when(pl.program_id(0) == 0)
        def _():
            cps = []
            for n, (key, dst) in enumerate((("in_b", winb), ("uk", wuk), ("uv", wuv), ("uq_n", wuqn), ("uq_r", wuqr))):
                cps += _fetch(wg_ref, lay, key, dst, sems, n * N_CHIPS)
            for cp in cps:
                cp.start()
            for cp in cps:
                cp.wait()

        xv = x_ref[...]
        xh = xv * _rinv(xv)
        hk = (xh * nkv[...]).astype(BF16)
        hb = (xh * nb[...]).astype(BF16)
        hk_o[...] = hk
        hb_o[...] = hb
        cos, sin = cos_ref[...], sin_ref[...]
        ckr = _dot(hk, wdkv[...])
        ckr_o[...] = ckr
        ckv_pre = ckr[:, :kvr]
        ckv = (ckv_pre * _rinv(ckv_pre) * kvn[...]).astype(BF16)
        ckv_o[...] = ckv
        kr = ckr[:, kvr:]
        kr_o[...] = (kr * cos + _swap_halves(kr) * sin).astype(BF16)
        kn_o[...] = _dot(ckv, wuk[...]).astype(BF16)
        v_o[...] = _dot(ckv, wuv[...]).astype(BF16)
        ub = _dot_nt(hb, winb[...])
        ub_o[...] = ub
        cq_pre = ub[:, :qr]
        cq = (cq_pre * _rinv(cq_pre) * qn[...]).astype(BF16)
        cq_o[...] = cq
        qn_o[...] = (_dot(cq, wuqn[...]) * Q_SCALE).astype(BF16)
        qrope = _dot(cq, wuqr[...]) * Q_SCALE
        qr_o[...] = (qrope * jnp.tile(cos, (1, nheads)) + _swap_halves(qrope) * jnp.tile(sin, (1, nheads))).astype(BF16)

    tok = lambda c: pl.BlockSpec((tb, c), lambda i: (i, 0))
    pos = pl.BlockSpec((tb, LANE), lambda i: (i % npos, 0))
    consts = [w["norm_kv"], w["norm_b"], w["w_dkv_p"], w["kv_norm"], w["q_norm"]]
    outs = [(hv, BF16), (hv, BF16), (hv, BF16), (LANE, BF16), (hv, BF16), (qr + hv, F32), (kvr + LANE, F32), (d, BF16), (d, BF16), (qr, BF16), (kvr, BF16)]
    return pl.pallas_call(
        body, name="fb_fwd", grid=(t_all // tb,),
        in_specs=[tok(d), ANY] + [_full(c.shape) for c in consts] + [pos, pos],
        out_specs=[tok(c) for c, _ in outs],
        out_shape=[_sds((t_all, c), dt) for c, dt in outs],
        scratch_shapes=[pltpu.VMEM((qr + hv, d), BF16), pltpu.VMEM((kvr, d), BF16), pltpu.VMEM((kvr, d), BF16), pltpu.VMEM((qr, d), BF16),
                        pltpu.VMEM((qr, d), BF16), pltpu.SemaphoreType.DMA((5 * N_CHIPS,))],
        compiler_params=_params(1),
    )(x1, wg, *consts, cos_t, sin_t)


def _causal_mask(row0, col0, nrows, ncols):
    rows = row0 + lax.broadcasted_iota(jnp.int32, (nrows, ncols), 0)
    cols = col0 + lax.broadcasted_iota(jnp.int32, (nrows, ncols), 1)
    return cols <= rows


def _attn_fwd(qn, qr, kn, kr, v, seq, ta):
    t_all, hv = qn.shape
    nheads, nb, na = hv // LANE, t_all // seq, seq // ta

    reps = ta // LANE

    def body(qn_ref, qr_ref, kn_ref, kr_ref, v_ref, o_ref, lse_ref, m_s, l_s, acc_s):
        i = pl.program_id(2)
        m_s[...] = jnp.full((ta, LANE), -1e30, F32)
        l_s[...] = jnp.zeros((ta, LANE), F32)
        acc_s[...] = jnp.zeros((ta, LANE), F32)
        q = jnp.concatenate([qn_ref[...], qr_ref[...]], axis=1)

        def tile(j, diagonal):
            cols = pl.ds(pl.multiple_of(j * ta, ta), ta)
            k = jnp.concatenate([kn_ref[cols, :], kr_ref[cols, :]], axis=1)
            s = _dot_nt(q, k)
            if diagonal:
                s = jnp.where(_causal_mask(0, 0, ta, ta), s, -1e30)
            m_prev = m_s[...]
            m_new = jnp.maximum(m_prev, jnp.max(s, axis=1, keepdims=True))
            p = jnp.exp2(s - jnp.tile(m_new, (1, reps)))
            alpha = jnp.exp2(m_prev - m_new)
            l_s[...] = alpha * l_s[...] + jnp.sum(p, axis=1, keepdims=True)
            acc_s[...] = alpha * acc_s[...] + _dot(p.astype(BF16), v_ref[cols, :])
            m_s[...] = m_new

        def off_diagonal(j, carry):
            tile(j, False)
            return carry

        lax.fori_loop(0, i, off_diagonal, 0)
        tile(i, True)
        o_ref[...] = acc_s[...] / l_s[...]
        lse_ref[...] = m_s[...] + jnp.log2(l_s[...])

    qspec = pl.BlockSpec((ta, LANE), lambda b, h, i: (b * na + i, h))
    kspec = pl.BlockSpec((seq, LANE), lambda b, h, i: (b, h))
    krspec = pl.BlockSpec((seq, LANE), lambda b, h, i: (b, 0))
    return pl.pallas_call(
        body, name="attn_fwd", grid=(nb, nheads, na),
        in_specs=[qspec, qspec, kspec, krspec, kspec],
        out_specs=[qspec, qspec],
        out_shape=[_sds((t_all, hv)), _sds((t_all, hv))],
        scratch_shapes=[pltpu.VMEM((ta, LANE), F32)] * 3,
        compiler_params=_params(3),
    )(qn, qr, kn, kr, v)


def _attn_bwd(qn, qr, kn, kr, v, do, lse, delta, seq, ta):
    t_all, hv = qn.shape
    nheads, nb, na = hv // LANE, t_all // seq, seq // ta

    reps = ta // LANE
    nchunks = ta // ATTN_ROWS

    def body(qn_ref, qr_ref, kn_ref, kr_ref, v_ref, do_ref, lse_ref, dl_ref, dqn_ref, dqr_ref, dkn_ref, dkr_ref, dv_ref,
             s_s, dp_s, p_s, ds_s, dk_s, dv_s):
        j = pl.program_id(2)

        @pl.when(j == 0)
        def _():
            dqn_ref[...] = jnp.zeros((seq, LANE), F32)
            dqr_ref[...] = jnp.zeros((seq, LANE), F32)

        dk_s[...] = jnp.zeros((ta, 2 * LANE), F32)
        dv_s[...] = jnp.zeros((ta, LANE), F32)
        k = jnp.concatenate([kn_ref[...], kr_ref[...]], axis=1)
        vv = v_ref[...]

        def tile(i, diagonal):
            rows_i = pl.ds(pl.multiple_of(i * ta, ta), ta)
            q = jnp.concatenate([qn_ref[rows_i, :], qr_ref[rows_i, :]], axis=1)
            do_b = do_ref[rows_i, :]
            s_s[...] = _dot_nt(q, k)
            dp_s[...] = _dot_nt(do_b, vv)

            def chunk(c, carry):
                rows = pl.ds(pl.multiple_of(c * ATTN_ROWS, ATTN_ROWS), ATTN_ROWS)
                seq_rows = pl.ds(pl.multiple_of(i * ta + c * ATTN_ROWS, ATTN_ROWS), ATTN_ROWS)
                s = s_s[rows, :]
                if diagonal:
                    s = jnp.where(_causal_mask(c * ATTN_ROWS, 0, ATTN_ROWS, ta), s, -1e30)
                p = jnp.exp2(s - jnp.tile(lse_ref[seq_rows, :], (1, reps)))
                p_s[rows, :] = p.astype(BF16)
                ds_s[rows, :] = (p * (dp_s[rows, :] - jnp.tile(dl_ref[seq_rows, :], (1, reps)))).astype(BF16)
                return carry

            lax.fori_loop(0, nchunks, chunk, 0, unroll=True)
            dv_s[...] += _dot_tn(p_s[...], do_b)
            ds = ds_s[...]
            dk_s[...] += _dot_tn(ds, q)
            dq = _dot(ds, k)
            dqn_ref[rows_i, :] += dq[:, :LANE]
            dqr_ref[rows_i, :] += dq[:, LANE:]

        def off_diagonal(i, carry):
            tile(i, False)
            return carry

        tile(j, True)
        lax.fori_loop(j + 1, na, off_diagonal, 0)
        dkn_ref[...] = (dk_s[:, :LANE] * LN2).astype(BF16)
        dkr_ref[...] = dk_s[:, LANE:] * LN2
        dv_ref[...] = dv_s[...].astype(BF16)

    qspec = pl.BlockSpec((seq, LANE), lambda b, h, j: (b, h))
    kspec = pl.BlockSpec((ta, LANE), lambda b, h, j: (b * na + j, h))
    krspec = pl.BlockSpec((ta, LANE), lambda b, h, j: (b * na + j, 0))
    return pl.pallas_call(
        body, name="attn_bwd", grid=(nb, nheads, na),
        in_specs=[qspec, qspec, kspec, krspec, kspec, qspec, qspec, qspec],
        out_specs=[qspec, qspec, kspec, kspec, kspec],
        out_shape=[_sds((t_all, hv)), _sds((t_all, hv)), _sds((t_all, hv), BF16), _sds((t_all, hv)), _sds((t_all, hv), BF16)],
        scratch_shapes=[pltpu.VMEM((ta, ta), F32), pltpu.VMEM((ta, ta), F32), pltpu.VMEM((ta, ta), BF16), pltpu.VMEM((ta, ta), BF16),
                        pltpu.VMEM((ta, 2 * LANE), F32), pltpu.VMEM((ta, LANE), F32)],
        compiler_params=_params(3),
    )(qn, qr, kn, kr, v, do, lse, delta)


def _head(o, ub, x1, target, wg, w, lay, tb):
    t_all, d = x1.shape
    hv, qr = lay.hv, lay.qr
    nheads = hv // LANE

    def body(o_ref, ub_ref, x1_ref, tg_ref, wg_ref, gf, loss_ref, dgf_ref, yb_ref, dx2_ref, do_ref, dg_ref, dl_ref, wob, sems):
        @pl.when(pl.program_id(0) == 0)
        def _():
            cps = _fetch(wg_ref, lay, "out_b", wob, sems, 0)
            for cp in cps:
                cp.start()
            loss_ref[...] = jnp.zeros((1, LANE), F32)
            dgf_ref[...] = jnp.zeros((1, d), F32)
            for cp in cps:
                cp.wait()

        ov = o_ref[...]
        g = ub_ref[:, qr:]
        sg = jax.nn.sigmoid(g)
        silu = g * sg
        yb = (ov * silu).astype(BF16)
        yb_ref[...] = yb
        x2 = x1_ref[...] + _dot(yb, wob[...])
        rinv = _rinv(x2)
        err = x2 * rinv * gf[...] - tg_ref[...]
        loss_ref[...] += (0.5 / d) * jnp.sum(jnp.sum(err * err, axis=1, keepdims=True), axis=0, keepdims=True)
        dx2, dgf = _rms_bwd(x2, rinv, gf[...], err * (1.0 / d))
        dgf_ref[...] += dgf
        dx2_ref[...] = dx2
        dyb = _dot_nt(dx2.astype(BF16), wob[...])
        dov = dyb * silu
        do_ref[...] = dov.astype(BF16)
        dg_ref[...] = dyb * ov * (sg * (1.0 + g * (1.0 - sg)))
        prod = dov * ov
        dl_ref[...] = jnp.concatenate(
            [jnp.broadcast_to(jnp.sum(prod[:, n * LANE:(n + 1) * LANE], axis=1, keepdims=True), (tb, LANE)) for n in range(nheads)], axis=1)

    tok = lambda c: pl.BlockSpec((tb, c), lambda i: (i, 0))
    return pl.pallas_call(
        body, name="head", grid=(t_all // tb,),
        in_specs=[tok(hv), tok(qr + hv), tok(d), tok(d), ANY, _full((1, d))],
        out_specs=[_full((1, LANE)), _full((1, d)), tok(hv), tok(d), tok(hv), tok(hv), tok(hv)],
        out_shape=[_sds((1, LANE)), _sds((1, d)), _sds((t_all, hv), BF16), _sds((t_all, d)), _sds((t_all, hv), BF16), _sds((t_all, hv)), _sds((t_all, hv))],
        scratch_shapes=[pltpu.VMEM((hv, d), BF16), pltpu.SemaphoreType.DMA((N_CHIPS,))],
        compiler_params=_params(1),
    )(o, ub, x1, target, wg, w["final_norm"])


def _fb_bwd(dqn, dqr, dkn, dkr, dv, dgate, ub, ckr, x1, dx2, wg, w, lay, cos_t, sin_t, seq, tb):
    t_all, d = x1.shape
    hv, qr, kvr = lay.hv, lay.qr, lay.kvr
    nheads = hv // LANE
    npos = seq // tb

    def body(dqn_ref, dqr_ref, dkn_ref, dkr_ref, dv_ref, dg_ref, ub_ref, ckr_ref, x1_ref, dx2_ref, wg_ref,
             qn, nb, kvn, wdkv, nkv, cos_ref, sin_ref,
             dx1_ref, dqrp_ref, dqnp_ref, dub_ref, dckr_ref, dqn_g, dnb_g, dkvn_g, dnkv_g, winb, wuk, wuv, wuqn, wuqr, sems):
        @pl.when(pl.program_id(0) == 0)
        def _():
            cps = []
            for n, (key, dst) in enumerate((("in_b", winb), ("uk", wuk), ("uv", wuv), ("uq_n", wuqn), ("uq_r", wuqr))):
                cps += _fetch(wg_ref, lay, key, dst, sems, n * N_CHIPS)
            for cp in cps:
                cp.start()
            dqn_g[...] = jnp.zeros((1, qr), F32)
            dnb_g[...] = jnp.zeros((1, d), F32)
            dkvn_g[...] = jnp.zeros((1, kvr), F32)
            dnkv_g[...] = jnp.zeros((1, d), F32)
            for cp in cps:
                cp.wait()

        cos, sin = cos_ref[...], sin_ref[...]
        xv = x1_ref[...]
        rinv1 = _rinv(xv)
        dqr_v = dqr_ref[...] * ATTN_SCALE
        dqr_pre = (dqr_v * jnp.tile(cos, (1, nheads)) + _swap_halves(dqr_v * jnp.tile(sin, (1, nheads)))).astype(BF16)
        dqrp_ref[...] = dqr_pre
        dqn_pre = (dqn_ref[...] * ATTN_SCALE).astype(BF16)
        dqnp_ref[...] = dqn_pre
        dcq = _dot_nt(dqn_pre, wuqn[...]) + _dot_nt(dqr_pre, wuqr[...])
        cq_pre = ub_ref[:, :qr]
        dcq_pre, g1 = _rms_bwd(cq_pre, _rinv(cq_pre), qn[...], dcq)
        dqn_g[...] += g1
        dub = jnp.concatenate([dcq_pre, dg_ref[...]], axis=1).astype(BF16)
        dub_ref[...] = dub
        dx1_b, g2 = _rms_bwd(xv, rinv1, nb[...], _dot(dub, winb[...]))
        dnb_g[...] += g2
        dkr_all = dkr_ref[...]
        dkr_sum = dkr_all[:, :LANE]
        for n in range(1, nheads):
            dkr_sum = dkr_sum + dkr_all[:, n * LANE:(n + 1) * LANE]
        dckr_rope = dkr_sum * cos + _swap_halves(dkr_sum * sin)
        dckv = _dot_nt(dkn_ref[...].astype(BF16), wuk[...]) + _dot_nt(dv_ref[...].astype(BF16), wuv[...])
        ckv_pre = ckr_ref[:, :kvr]
        dckv_pre, g3 = _rms_bwd(ckv_pre, _rinv(ckv_pre), kvn[...], dckv)
        dkvn_g[...] += g3
        dckr = jnp.concatenate([dckv_pre, dckr_rope], axis=1).astype(BF16)
        dckr_ref[...] = dckr
        dx1_kv, g4 = _rms_bwd(xv, rinv1, nkv[...], _dot_nt(dckr, wdkv[...]))
        dnkv_g[...] += g4
        dx1_ref[...] = dx2_ref[...] + dx1_b + dx1_kv

    tok = lambda c: pl.BlockSpec((tb, c), lambda i: (i, 0))
    pos = pl.BlockSpec((tb, LANE), lambda i: (i % npos, 0))
    consts = [w["q_norm"], w["norm_b"], w["kv_norm"], w["w_dkv_p"], w["norm_kv"]]
    return pl.pallas_call(
        body, name="fb_bwd", grid=(t_all // tb,),
        in_specs=[tok(hv)] * 6 + [tok(qr + hv), tok(kvr + LANE), tok(d), tok(d), ANY] + [_full(c.shape) for c in consts] + [pos, pos],
        out_specs=[tok(d), tok(hv), tok(hv), tok(qr + hv), tok(kvr + LANE), _full((1, qr)), _full((1, d)), _full((1, kvr)), _full((1, d))],
        out_shape=[_sds((t_all, d)), _sds((t_all, hv), BF16), _sds((t_all, hv), BF16), _sds((t_all, qr + hv), BF16), _sds((t_all, kvr + LANE), BF16),
                   _sds((1, qr)), _sds((1, d)), _sds((1, kvr)), _sds((1, d))],
        scratch_shapes=[pltpu.VMEM((qr + hv, d), BF16), pltpu.VMEM((kvr, d), BF16), pltpu.VMEM((kvr, d), BF16), pltpu.VMEM((qr, d), BF16),
                        pltpu.VMEM((qr, d), BF16), pltpu.SemaphoreType.DMA((5 * N_CHIPS,))],
        compiler_params=_params(1),
    )(dqn, dqr, dkn, dkr, dv, dgate, ub, ckr, x1, dx2, wg, *consts, cos_t, sin_t)


def _fa_bwd(dx1, x, u, hs, wg, w, lay, seq, tb):
    t_all, d = x.shape
    dr = lay.dr
    nblocks = w["w_rg"].shape[0]
    nblk = seq // tb
    nt = tb // SUBLANE
    per8 = tb // 8

    def body(dx1_ref, x_ref, u_ref, uh_ref, hs_ref, hh_ref, wg_ref, na, cw, cb, wrg, brg, wig, big, lam,
             gx_ref, du_ref, dna_g, dcw_g, dcb_g, dbrg_g, dbig_g, dlam_g, dwrg_g, dwig_g,
             wint, wout, xpad, hpad, a_s, d_s, g_s, dxpad, carry, sems):
        b, jj = pl.program_id(0), pl.program_id(1)
        first_block = jj == nblk - 1

        @pl.when((b == 0) & (jj == 0))
        def _():
            cps = _fetch(wg_ref, lay, "in_a", wint, sems, 0) + _fetch(wg_ref, lay, "out_a", wout, sems, N_CHIPS)
            for cp in cps:
                cp.start()
            dna_g[...] = jnp.zeros((1, d), F32)
            dcw_g[...] = jnp.zeros((4, dr), F32)
            dcb_g[...] = jnp.zeros((1, dr), F32)
            dbrg_g[...] = jnp.zeros((1, dr), F32)
            dbig_g[...] = jnp.zeros((1, dr), F32)
            dlam_g[...] = jnp.zeros((1, dr), F32)
            dwrg_g[...] = jnp.zeros((nblocks, LANE, LANE), F32)
            dwig_g[...] = jnp.zeros((nblocks, LANE, LANE), F32)
            for cp in cps:
                cp.wait()

        @pl.when(jj == 0)
        def _():
            dxpad[pl.ds(tb, 8), :] = jnp.zeros((8, dr), F32)
            carry[...] = jnp.zeros((8, dr), F32)

        keep = jnp.where(first_block, 0.0, 1.0)
        dx1v = dx1_ref[...]
        gate = u_ref[:, dr:]
        xpad[pl.ds(0, 8), :] = uh_ref[...] * keep
        xpad[pl.ds(8, tb), :] = u_ref[:, :dr]
        hpad[pl.ds(0, 8), :] = hh_ref[...] * keep
        hpad[pl.ds(8, tb), :] = hs_ref[...]
        xb = _conv(xpad, cw, cb[...], tb)
        xbb = xb.astype(BF16)
        r, i = _gates(xb, wrg, brg[...], wig, big[...], nblocks)
        sp = _softplus(-lam[...])
        log_a = -LRU_C * r * sp
        a = jnp.exp(log_a)
        nem = _neg_expm1(2.0 * log_a)
        mult = jnp.sqrt(nem)
        sg = jax.nn.sigmoid(gate)
        dy = _dot_nt(dx1v.astype(BF16), wout[...])
        hsv = hs_ref[...]
        dgate = dy * hsv * (sg * (1.0 + gate * (1.0 - sg)))
        a_s[...] = a
        d_s[...] = dy * (gate * sg)
        row = lax.broadcasted_iota(jnp.int32, (8, dr), 0)

        def step(k, c):
            r0 = pl.multiple_of((nt - 1 - k) * 8, 8)
            av = a_s[pl.ds(r0, 8), :]
            dv = d_s[pl.ds(r0, 8), :]
            qv = av * dv
            for s in (1, 2, 4):
                m = row < 8 - s
                a_sh = jnp.where(m, pltpu.roll(av, 8 - s, 0), 1.0)
                q_sh = jnp.where(m, pltpu.roll(qv, 8 - s, 0), 0.0)
                qv = qv + av * q_sh
                av = av * a_sh
            qv = qv + av * c
            g_s[pl.ds(r0, 8), :] = dv + jnp.where(row < 7, pltpu.roll(qv, 7, 0), c)
            return jnp.broadcast_to(qv[0:1, :], qv.shape)

        carry[...] = lax.fori_loop(0, nt, step, carry[...])
        g = g_s[...]
        ix = i * xb
        dlog_a = g * hpad[pl.ds(7, tb), :] * a - g * ix * ((1.0 - nem) / mult)
        dix = g * mult
        dlam_g[...] += -jax.nn.sigmoid(-lam[...]) * jnp.sum(dlog_a * (-LRU_C * r), axis=0, keepdims=True)
        drg = dlog_a * (-LRU_C * sp) * r * (1.0 - r)
        dig = dix * xb * i * (1.0 - i)
        dbrg_g[...] += jnp.sum(drg, axis=0, keepdims=True)
        dbig_g[...] += jnp.sum(dig, axis=0, keepdims=True)
        drgb, digb = drg.astype(BF16), dig.astype(BF16)
        back = []
        for n in range(nblocks):
            cols = slice(n * LANE, (n + 1) * LANE)
            dwrg_g[n] += _dot_tn(xbb[:, cols], drgb[:, cols])
            dwig_g[n] += _dot_tn(xbb[:, cols], digb[:, cols])
            back.append(_dot_nt(drgb[:, cols], wrg[n]) + _dot_nt(digb[:, cols], wig[n]))
        dxb = dix * i + jnp.concatenate(back, axis=1)
        dcb_g[...] += jnp.sum(dxb, axis=0, keepdims=True)
        for k in range(4):
            dcw_g[k:k + 1, :] += jnp.sum(dxb * xpad[pl.ds(5 + k, tb), :], axis=0, keepdims=True)
        dxpad[pl.ds(0, tb), :] = dxb
        dxpre = (cw[3:4, :] * dxb + cw[2:3, :] * dxpad[pl.ds(1, tb), :] + cw[1:2, :] * dxpad[pl.ds(2, tb), :]
                 + cw[0:1, :] * dxpad[pl.ds(3, tb), :])
        dxpad[pl.ds(tb, 8), :] = dxb[:8, :]
        du = jnp.concatenate([dxpre, dgate], axis=1).astype(BF16)
        du_ref[...] = du
        xv = x_ref[...]
        dxa, g1 = _rms_bwd(xv, _rinv(xv), na[...], _dot(du, wint[...]))
        dna_g[...] += g1
        gx_ref[...] = dx1v + dxa

    blk = lambda b, j: b * nblk + (nblk - 1 - j)
    tok = lambda c: pl.BlockSpec((tb, c), lambda b, j: (blk(b, j), 0))
    halo = pl.BlockSpec((8, dr), lambda b, j: (jnp.maximum(blk(b, j) * per8 - 1, 0), 0))
    consts = [w["norm_a"], w["conv_w"], w["conv_b"], w["w_rg"], w["b_rg"], w["w_ig"], w["b_ig"], w["lru_lambda"]]
    vec = lambda c: _full((1, c))
    blocks3 = (nblocks, LANE, LANE)
    return pl.pallas_call(
        body, name="fa_bwd", grid=(t_all // seq, nblk),
        in_specs=[tok(d), tok(d), tok(2 * dr), halo, tok(dr), halo, ANY] + [_full(c.shape) for c in consts],
        out_specs=[tok(d), tok(2 * dr), vec(d), _full((4, dr)), vec(dr), vec(dr), vec(dr), vec(dr), _full(blocks3), _full(blocks3)],
        out_shape=[_sds((t_all, d)), _sds((t_all, 2 * dr), BF16), _sds((1, d)), _sds((4, dr)), _sds((1, dr)), _sds((1, dr)), _sds((1, dr)),
                   _sds((1, dr)), _sds(blocks3), _sds(blocks3)],
        scratch_shapes=[pltpu.VMEM((2 * dr, d), BF16), pltpu.VMEM((dr, d), BF16), pltpu.VMEM((tb + 8, dr), F32), pltpu.VMEM((tb + 8, dr), F32),
                        pltpu.VMEM((tb, dr), F32), pltpu.VMEM((tb, dr), F32), pltpu.VMEM((tb, dr), F32), pltpu.VMEM((tb + 8, dr), F32),
                        pltpu.VMEM((8, dr), F32), pltpu.SemaphoreType.DMA((2 * N_CHIPS,))],
        compiler_params=_params(2),
    )(dx1, x, u, u, hs, hs, wg, *consts)


def _mm_into(gbuf, a, b, off, name, bt):
    t_all, m = a.shape
    n = b.shape[1]
    nsplit = 2 if m >= 1024 and (m // 2) % LANE == 0 else 1
    mh = m // nsplit
    nt = t_all // bt

    def body(a_ref, b_ref, g_ref, o_ref, acc, sems):
        del g_ref
        part, t = pl.program_id(0), pl.program_id(1)

        def out_copy(h):
            return pltpu.make_async_copy(acc.at[h], o_ref.at[pl.ds(off + h * mh, mh), :], sems.at[h])

        prod = _dot_tn(a_ref[...].astype(BF16), b_ref[...].astype(BF16))
        for h in range(nsplit):
            @pl.when((part == h) & (t == 0))
            def _():
                acc[h] = prod

            @pl.when((part == h) & (t > 0))
            def _():
                acc[h] += prod

            @pl.when((part == h) & (t == nt - 1))
            def _():
                out_copy(h).start()

        @pl.when((part == nsplit - 1) & (t == nt - 1))
        def _():
            for h in range(nsplit):
                out_copy(h).wait()

    return pl.pallas_call(
        body, name=name, grid=(nsplit, nt),
        in_specs=[pl.BlockSpec((bt, mh), lambda h, t: (t, h)), pl.BlockSpec((bt, n), lambda h, t: (t, 0)), ANY],
        out_specs=ANY, out_shape=_sds(gbuf.shape), input_output_aliases={2: 0},
        scratch_shapes=[pltpu.VMEM((nsplit, mh, n), F32), pltpu.SemaphoreType.DMA((nsplit,))],
        compiler_params=_params(2),
    )(a, b, gbuf)


def _mm_tn(a, b, name, bt):
    t_all, m = a.shape
    n = b.shape[1]

    def body(a_ref, b_ref, o_ref):
        @pl.when(pl.program_id(0) == 0)
        def _():
            o_ref[...] = jnp.zeros((m, n), F32)

        o_ref[...] += _dot_tn(a_ref[...].astype(BF16), b_ref[...].astype(BF16))

    return pl.pallas_call(
        body, name=name, grid=(t_all // bt,),
        in_specs=[pl.BlockSpec((bt, m), lambda t: (t, 0)), pl.BlockSpec((bt, n), lambda t: (t, 0))],
        out_specs=_full((m, n)), out_shape=_sds((m, n)),
        compiler_params=_params(1),
    )(a, b)


def _all_gather8(blocks, name):
    nb = len(blocks)

    def body(*refs):
        x_refs, out_refs = refs[:nb], refs[nb:2 * nb]
        send_sems, recv_sems, local_sems = refs[2 * nb:]
        x, y, c = _place()
        me, sibling = (x, y, c), (x, y, 1 - c)
        chips = [(1 - x, y), (x, 1 - y), (1 - x, 1 - y)]
        waits = []
        for n in range(nb):
            x_ref, out_ref = x_refs[n], out_refs[n]

            def slot(px, py, pc, out_ref=out_ref):
                return out_ref.at[4 * px + 2 * py + pc]

            def copy(k, blk, to, src=None, n=n, slot=slot):
                return pltpu.make_async_remote_copy(
                    src_ref=slot(*blk) if src is None else src, dst_ref=slot(*blk), send_sem=send_sems.at[n, k], recv_sem=recv_sems.at[n, k],
                    device_id=to, device_id_type=MESH)

            mine = pltpu.make_async_copy(x_ref, slot(*me), local_sems.at[n])
            mine.start()
            first = [copy(0, me, sibling, src=x_ref)] + [copy(1 + j, me, (*chip, c), src=x_ref) for j, chip in enumerate(chips)]
            for cp in first:
                cp.start()
            waits.append((copy, mine, first))
        for n in range(nb):
            copy, mine, first = waits[n]
            passed = [copy(4 + j, (*chip, c), sibling) for j, chip in enumerate(chips)]
            for j, chip in enumerate(chips):
                copy(1 + j, (*chip, c), me).wait_recv()
                passed[j].start()
            copy(0, sibling, me).wait_recv()
            for j, chip in enumerate(chips):
                copy(4 + j, (*chip, 1 - c), me).wait_recv()
            for cp in first + passed:
                cp.wait_send()
            mine.wait()

    return pl.pallas_call(
        body, name=name, out_shape=[_sds((8,) + b.shape, b.dtype) for b in blocks], in_specs=[ANY] * nb, out_specs=[ANY] * nb,
        scratch_shapes=[pltpu.SemaphoreType.DMA((nb, 7)), pltpu.SemaphoreType.DMA((nb, 7)), pltpu.SemaphoreType.DMA((nb,))],
    )(*blocks)


def _swap_sibling(src, name, half_cols=False):
    rows, cols = src.shape
    half = cols // 2 if half_cols else cols

    def body(src_ref, out_ref, send_sem, recv_sem):
        x, y, c = _place()
        part = src_ref.at[:, pl.ds(pl.multiple_of((1 - c) * half, LANE), half)] if half_cols else src_ref
        cp = pltpu.make_async_remote_copy(src_ref=part, dst_ref=out_ref, send_sem=send_sem, recv_sem=recv_sem,
                                          device_id=(x, y, 1 - c), device_id_type=MESH)
        cp.start()
        cp.wait()

    return pl.pallas_call(
        body, name=name, out_shape=_sds((rows, half), src.dtype), in_specs=[ANY], out_specs=ANY,
        scratch_shapes=[pltpu.SemaphoreType.DMA, pltpu.SemaphoreType.DMA],
    )(src)


def _scatter_chips(part16, part32, lay):
    half = part16.shape[1]

    def body(p16_ref, p32_ref, got_ref, own_ref, send_sems, recv_sems, local_sem):
        x, y, c = _place()
        chips = [(1 - x, y), (x, 1 - y), (1 - x, 1 - y)]

        def rows_of(ref, key, chip):
            start = pl.multiple_of(lay.g_off[key] + chip * lay.rows[key], ROW_ALIGN)
            return ref.at[pl.ds(start, lay.rows[key]), :]

        def compact(ref, key):
            return ref.at[pl.ds(lay.c_off[key], lay.rows[key]), :]

        for key in G_ORDER:
            pltpu.make_async_copy(rows_of(p32_ref, key, 2 * x + y), compact(own_ref, key), local_sem).start()
        for k, (px, py) in enumerate(chips):
            for key in G_ORDER:
                pltpu.make_async_remote_copy(src_ref=rows_of(p16_ref, key, 2 * px + py), dst_ref=compact(got_ref.at[k], key),
                                             send_sem=send_sems.at[k], recv_sem=recv_sems.at[k], device_id=(px, py, c), device_id_type=MESH).start()
        for k, (px, py) in enumerate(chips):
            pltpu.make_async_remote_copy(src_ref=got_ref.at[k], dst_ref=got_ref.at[k], send_sem=send_sems.at[k], recv_sem=recv_sems.at[k],
                                         device_id=(px, py, c), device_id_type=MESH).wait()
        pltpu.make_async_copy(own_ref, own_ref, local_sem).wait()

    return pl.pallas_call(
        body, name="rs_chips", out_shape=[_sds((3, lay.c_rows, half), BF16), _sds((lay.c_rows, half), F32)], in_specs=[ANY, ANY], out_specs=[ANY, ANY],
        scratch_shapes=[pltpu.SemaphoreType.DMA((3,)), pltpu.SemaphoreType.DMA((3,)), pltpu.SemaphoreType.DMA],
    )(part16, part32)


def _sum_sibling(gbuf, got, cidx):
    rows, d = gbuf.shape
    half = d // 2
    rb = _row_block(rows)

    def body(c_ref, g_ref, r_ref, o32_ref, o16_ref):
        del c_ref
        s = g_ref[...] + r_ref[...]
        o32_ref[...] = s
        o16_ref[...] = s.astype(BF16)

    plain = pl.BlockSpec((rb, half), lambda i, c: (i, 0))
    return pl.pallas_call(
        body, name="rs_sum_sibling",
        grid_spec=pltpu.PrefetchScalarGridSpec(num_scalar_prefetch=1, grid=(rows // rb,),
                                               in_specs=[pl.BlockSpec((rb, half), lambda i, c: (i, c[0])), plain], out_specs=[plain, plain]),
        out_shape=[_sds((rows, half)), _sds((rows, half), BF16)], compiler_params=_params(1),
    )(cidx, gbuf, got)


def _sum_chips(own, got):
    rows, half = own.shape
    rb = _row_block(rows)

    def body(a_ref, b_ref, o_ref):
        o_ref[...] = ((a_ref[...] + b_ref[0].astype(F32)) + b_ref[1].astype(F32)) + b_ref[2].astype(F32)

    spec = pl.BlockSpec((rb, half), lambda i: (i, 0))
    return pl.pallas_call(
        body, name="rs_sum_chips", grid=(rows // rb,), in_specs=[spec, pl.BlockSpec((3, rb, half), lambda i: (0, i, 0))], out_specs=spec,
        out_shape=_sds((rows, half)), compiler_params=_params(1),
    )(own, got)


def _adamw(w, g, m, v):
    m = ADAM_B1 * m + (1.0 - ADAM_B1) * g
    v = ADAM_B2 * v + (1.0 - ADAM_B2) * (g * g)
    m_hat = m / (1.0 - ADAM_B1 ** ADAM_STEP)
    v_hat = v / (1.0 - ADAM_B2 ** ADAM_STEP)
    return -ADAM_LR * (m_hat / (jnp.sqrt(v_hat) + ADAM_EPS) + ADAM_WD * w), m, v


def _adamw_rows(name, w, g, m, v):
    _, rows, cols = w.shape
    rb = _row_block(rows, 256)

    def body(w_ref, g_ref, m_ref, v_ref, d_ref, mo_ref, vo_ref):
        d_ref[...], mo_ref[...], vo_ref[...] = _adamw(w_ref[...], g_ref[...], m_ref[...], v_ref[...])

    spec = pl.BlockSpec((1, rb, cols), lambda i: (0, i, 0))
    return pl.pallas_call(
        body, name=name, grid=(rows // rb,), in_specs=[spec] * 4, out_specs=[spec] * 3, out_shape=[_sds(w.shape)] * 3,
        compiler_params=_params(1),
    )(w, g, m, v)


def _adamw_group(ws, gs, ms, vs):
    n = len(ws)

    def body(*refs):
        for k in range(n):
            w_ref, g_ref, m_ref, v_ref = (refs[j * n + k] for j in range(4))
            outs = _adamw(w_ref[...], g_ref[...], m_ref[...], v_ref[...])
            for j in range(3):
                refs[(4 + j) * n + k][...] = outs[j]

    outs = pl.pallas_call(
        body, name="adamw_small", out_shape=[_sds(w.shape) for w in ws] * 3,
        compiler_params=pltpu.CompilerParams(vmem_limit_bytes=VMEM_LIMIT),
    )(*ws, *gs, *ms, *vs)
    return outs[:n], outs[n:2 * n], outs[2 * n:]


def _gather_weights(sh, lay):
    x, y, c = _place()
    d = lay.d
    uq = sh["w_uq"][0].astype(BF16)
    parts = {
        "in_b": sh["w_in_b"][0].T.astype(BF16), "in_a": sh["w_in_a"][0].T.astype(BF16), "out_a": sh["w_out_a"][0].astype(BF16),
        "out_b": sh["w_out_b"][0].astype(BF16), "uk": sh["w_uk"].astype(BF16).reshape(-1, d), "uv": sh["w_uv"].astype(BF16).reshape(-1, d),
        "uq_n": uq[:, :, :QK_NOPE].reshape(-1, d), "uq_r": jnp.pad(uq[:, :, QK_NOPE:], ((0, 0), (0, 0), (0, LANE - QK_ROPE))).reshape(-1, d),
        "dkv": jnp.pad(sh["w_dkv"].astype(BF16), ((0, 0), (0, LANE - QK_ROPE))).reshape(-1, d),
    }
    stack = jnp.concatenate([parts[k] for k in W_ORDER], axis=0)
    stack = jnp.pad(stack, ((0, lay.w_rows - stack.shape[0]), (0, 0))).reshape(2, lay.w_rows // 2, d)
    small = jnp.concatenate([sh[k].reshape(-1) for k in SMALL])
    n_small = small.shape[0]
    width = _round_up(n_small, 2 * SUBLANE * LANE) // (2 * SUBLANE)
    small = jnp.pad(small, (0, 2 * SUBLANE * width - n_small)).reshape(2, SUBLANE, width)
    wg, sg = _all_gather8([lax.dynamic_index_in_dim(stack, c, 0, keepdims=False), lax.dynamic_index_in_dim(small, c, 0, keepdims=False)], "ag_weights")
    wg = wg.reshape(N_CHIPS, lay.w_rows, d)
    sg = sg.reshape(N_CHIPS, 2 * SUBLANE * width)
    full, off = {}, 0
    for k in SMALL:
        n = sh[k].size
        piece = sg[:, off:off + n]
        off += n
        if k == "conv_w":
            full[k] = piece.reshape(N_CHIPS, 4, n // 4).transpose(1, 0, 2).reshape(4, n)
        else:
            full[k] = piece.reshape(1, N_CHIPS * n)
    rows = lay.rows["dkv"]
    w_dkv_p = wg[:, lay.w_off["dkv"]:lay.w_off["dkv"] + rows, :].reshape(d, lay.kvr + LANE)
    return wg, w_dkv_p, full


def _chip_split(g, taps=False):
    if taps:
        n = g.shape[1] // N_CHIPS
        return g.reshape(4, N_CHIPS, n).transpose(1, 0, 2).reshape(N_CHIPS, 4 * n)
    return g.reshape(N_CHIPS, -1)


def kernel(x, norm_a, w_in_a, conv_w, conv_b, w_rg, b_rg, w_ig, b_ig, lru_lambda, w_out_a, norm_kv, w_dkv, kv_norm, w_uk, w_uv, norm_b, w_in_b, q_norm, w_uq, w_out_b, final_norm, loss_target, m_norm_a, m_w_in_a, m_conv_w, m_conv_b, m_w_rg, m_b_rg, m_w_ig, m_b_ig, m_lru_lambda, m_w_out_a, m_norm_kv, m_w_dkv, m_kv_norm, m_w_uk, m_w_uv, m_norm_b, m_w_in_b, m_q_norm, m_w_uq, m_w_out_b, m_final_norm, v_norm_a, v_w_in_a, v_conv_w, v_conv_b, v_w_rg, v_b_rg, v_w_ig, v_b_ig, v_lru_lambda, v_w_out_a, v_norm_kv, v_w_dkv, v_kv_norm, v_w_uk, v_w_uv, v_norm_b, v_w_in_b, v_q_norm, v_w_uq, v_w_out_b, v_final_norm):
    given = dict(locals())
    sh = {k: given[k] for k in WEIGHTS}
    xi, yi, ci = _place()
    nb, seq, d = x.shape
    t_all = nb * seq
    tb_a, tb_b, ta, bt = min(TOKENS_A, seq), min(TOKENS_B, seq), min(TOKENS_ATTN, seq), min(TOKENS_MM, t_all)
    dr = conv_b.shape[1] * N_CHIPS
    qr, kvr, nheads = q_norm.shape[1], kv_norm.shape[0], w_uk.shape[1]
    hv = nheads * LANE
    n_small = sum(sh[k].size for k in SMALL)
    n_repl = sum(sh[k].size for k in REPL)
    lay = _Layout(d, dr, qr, kvr, hv, n_small, n_repl)
    half = d // 2

    wg, w_dkv_p, w = _gather_weights(sh, lay)
    w.update({"w_rg": w_rg[0].astype(BF16), "w_ig": w_ig[0].astype(BF16), "w_dkv_p": w_dkv_p, "norm_kv": norm_kv[None, :],
              "kv_norm": kv_norm[None, :], "final_norm": final_norm[None, :], "norm_b": norm_b, "q_norm": q_norm})
    cos_t, sin_t = _rope_tables(seq)

    x0 = x.reshape(t_all, d)
    x1, u, hs, h, y = _fa_fwd(x0, wg, w, lay, seq, tb_a)
    qn, qrp, kn, kr, v, ub, ckr, hb, hk, cq, ckv = _fb_fwd(x1, wg, w, lay, cos_t, sin_t, seq, tb_b)
    o, lse = _attn_fwd(qn, qrp, kn, kr, v, seq, ta)
    loss, g_final_norm, yb, dx2, do, dgate, delta = _head(o, ub, x1, loss_target.reshape(t_all, d), wg, w, lay, tb_b)
    dqn, dqr, dkn, dkr, dv = _attn_bwd(qn, qrp, kn, kr, v, do, lse, delta, seq, ta)
    dx1, dqr_pre, dqn_pre, dub, dckr, g_q_norm, g_norm_b, g_kv_norm, g_norm_kv = _fb_bwd(
        dqn, dqr, dkn, dkr, dv, dgate, ub, ckr, x1, dx2, wg, w, lay, cos_t, sin_t, seq, tb_b)
    gx, du, g_norm_a, g_conv_w, g_conv_b, g_b_rg, g_b_ig, g_lam, g_w_rg, g_w_ig = _fa_bwd(dx1, x0, u, hs, wg, w, lay, seq, tb_a)
    loss = lax.psum(loss[0, 0], ("x", "y", "c"))

    g_dkv = _mm_tn(hk, dckr, "dw_dkv", bt)
    small = jnp.concatenate([_chip_split(g_norm_a), _chip_split(g_conv_w, taps=True), _chip_split(g_conv_b), _chip_split(g_b_rg),
                             _chip_split(g_b_ig), _chip_split(g_lam)], axis=1)
    small = jnp.pad(small, ((0, 0), (0, lay.small_rows * d - small.shape[1]))).reshape(N_CHIPS, lay.small_rows, d)
    repl_parts = {"w_rg": g_w_rg, "w_ig": g_w_ig, "norm_kv": g_norm_kv, "kv_norm": g_kv_norm, "norm_b": g_norm_b, "q_norm": g_q_norm,
                  "final_norm": g_final_norm}
    repl = jnp.concatenate([repl_parts[k].reshape(-1) for k in REPL])
    repl = jnp.pad(repl, (0, N_CHIPS * lay.repl_rows * d - n_repl)).reshape(N_CHIPS, lay.repl_rows, d)
    pad_rows = lay.rows["rest"] - lay.rows["dkv"] - lay.small_rows - lay.repl_rows
    rest = jnp.concatenate([g_dkv.reshape(N_CHIPS, lay.rows["dkv"], d), small, repl, jnp.zeros((N_CHIPS, pad_rows, d), F32)], axis=1)
    gbuf = lax.dynamic_update_slice(lax.empty((lay.g_rows, d), F32), rest.reshape(N_CHIPS * lay.rows["rest"], d), (lay.g_off["rest"], 0))
    for key, a, b in (("in_b", dub, hb), ("in_a", du, h), ("out_a", y, dx1), ("out_b", yb, dx2), ("uk", ckv, dkn), ("uv", ckv, dv),
                      ("uq_n", cq, dqn_pre), ("uq_r", cq, dqr_pre)):
        gbuf = _mm_into(gbuf, a, b, lay.g_off[key], "dw_" + key, bt)

    got = _swap_sibling(gbuf, "rs_sibling", half_cols=True)
    part32, part16 = _sum_sibling(gbuf, got, jnp.reshape(ci, (1,)).astype(jnp.int32))
    others, own = _scatter_chips(part16, part32, lay)
    mine = _sum_chips(own, others)
    theirs = _swap_sibling(mine, "rs_return")
    red = jnp.concatenate([jnp.where(ci == 0, mine, theirs), jnp.where(ci == 0, theirs, mine)], axis=1)
    r0 = lay.c_off["rest"] + lay.rows["dkv"] + lay.small_rows
    (rep_all,) = _all_gather8([mine[r0:r0 + lay.repl_rows]], "ag_rep")
    rep_flat = rep_all.reshape(N_CHIPS, 2, lay.repl_rows, half).transpose(0, 2, 1, 3).reshape(-1)

    def rows(key):
        return red[lay.c_off[key]:lay.c_off[key] + lay.rows[key]]

    grads = {"w_in_b": rows("in_b").T[None], "w_in_a": rows("in_a").T[None], "w_out_a": rows("out_a")[None], "w_out_b": rows("out_b")[None],
             "w_uk": rows("uk").reshape(w_uk.shape), "w_uv": rows("uv").reshape(w_uv.shape)}
    uq_n = rows("uq_n").reshape(qr // N_CHIPS, nheads, LANE)
    uq_r = rows("uq_r").reshape(qr // N_CHIPS, nheads, LANE)[:, :, :QK_ROPE]
    grads["w_uq"] = jnp.concatenate([uq_n, uq_r], axis=2)[None]
    rest_red = red[lay.c_off["rest"]:lay.c_off["rest"] + lay.rows["rest"]]
    grads["w_dkv"] = rest_red[:lay.rows["dkv"]].reshape(d // N_CHIPS, kvr + LANE)[:, :kvr + QK_ROPE]
    small_red = rest_red[lay.rows["dkv"]:lay.rows["dkv"] + lay.small_rows].reshape(-1)
    off = 0
    for k in SMALL:
        n = sh[k].size
        grads[k] = small_red[off:off + n].reshape(sh[k].shape)
        off += n
    off = 0
    for k in REPL:
        n = sh[k].size
        grads[k] = rep_flat[off:off + n].reshape(sh[k].shape)
        off += n

    new = {}
    for k in ("w_in_a", "w_in_b", "w_out_a", "w_out_b"):
        new[k] = _adamw_rows("adamw_" + k, sh[k], grads[k], given["m_" + k], given["v_" + k])
    rest_names = [k for k in WEIGHTS if k not in new]
    as2d = lambda a: a[None, :] if a.ndim == 1 else a
    ds, ms, vs = _adamw_group([as2d(sh[k]) for k in rest_names], [as2d(grads[k]) for k in rest_names],
                              [as2d(given["m_" + k]) for k in rest_names], [as2d(given["v_" + k]) for k in rest_names])
    for n, k in enumerate(rest_names):
        new[k] = tuple(a.reshape(sh[k].shape) for a in (ds[n], ms[n], vs[n]))
    return (loss, gx.reshape(nb, seq, d), *[grads[k] for k in WEIGHTS], *[new[k][0] for k in WEIGHTS], *[new[k][1] for k in WEIGHTS],
            *[new[k][2] for k in WEIGHTS])
```

```python
import jax
import jax.numpy as jnp
from jax import lax
from jax.experimental import pallas as pl
from jax.experimental.pallas import tpu as pltpu

F32, BF16 = jnp.float32, jnp.bfloat16
EPS = 1e-6
LRU_C = 8.0
ROPE_THETA = 10000.0
QK_NOPE, QK_ROPE = 128, 64
ATTN_SCALE = (QK_NOPE + QK_ROPE) ** -0.5
LN2 = 0.6931471805599453
Q_SCALE = ATTN_SCALE / LN2
ATTN_ROWS = 64
LANE = 128
SUBLANE = 8
ROW_ALIGN = 32
VMEM_LIMIT = 60000 * 1024
ADAM_LR, ADAM_B1, ADAM_B2, ADAM_EPS, ADAM_WD, ADAM_STEP = 0.001, 0.9, 0.999, 1e-08, 0.01, 10
MESH = pl.DeviceIdType.MESH
ANY = pl.BlockSpec(memory_space=pl.ANY)
N_CHIPS = 4
TOKENS_A, TOKENS_B, TOKENS_ATTN, TOKENS_MM = 256, 256, 512, 512

SMALL = ("norm_a", "conv_w", "conv_b", "b_rg", "b_ig", "lru_lambda")
REPL = ("w_rg", "w_ig", "norm_kv", "kv_norm", "norm_b", "q_norm", "final_norm")
WEIGHTS = ("norm_a", "w_in_a", "conv_w", "conv_b", "w_rg", "b_rg", "w_ig", "b_ig", "lru_lambda", "w_out_a", "norm_kv",
           "w_dkv", "kv_norm", "w_uk", "w_uv", "norm_b", "w_in_b", "q_norm", "w_uq", "w_out_b", "final_norm")
W_GROUPS = {"a": ("in_a", "out_a"), "b": ("in_b", "out_b", "uk", "uv", "uq_n", "uq_r", "dkv")}
G_GROUPS = {"early": ("in_b", "out_a", "out_b", "uk", "uv", "uq_n", "uq_r"), "late": ("in_a", "rest")}


def _sds(shape, dtype=F32):
    return jax.ShapeDtypeStruct(tuple(shape), dtype)


def _params(n_grid):
    return pltpu.CompilerParams(dimension_semantics=("arbitrary",) * n_grid, vmem_limit_bytes=VMEM_LIMIT)


def _full(shape):
    nd = len(shape)
    return pl.BlockSpec(tuple(shape), lambda *g: (0,) * nd)


def _round_up(n, k):
    return -(-n // k) * k


def _row_block(rows, cap=512):
    best = SUBLANE
    for r in range(SUBLANE, min(rows, cap) + 1, SUBLANE):
        if rows % r == 0:
            best = r
    return best


def _place():
    return lax.axis_index("x"), lax.axis_index("y"), lax.axis_index("c")


class _Layout:
    def __init__(self, d, dr, qr, kvr, hv, n_small, n_repl):
        assert hv == d, "the packed rows are D_MODEL wide, which must equal heads * 128"
        self.d, self.dr, self.qr, self.kvr, self.hv = d, dr, qr, kvr, hv
        per_chip = {"in_b": (qr + hv) // N_CHIPS, "in_a": 2 * dr // N_CHIPS, "out_a": dr // N_CHIPS, "out_b": hv // N_CHIPS,
                    "uk": kvr // N_CHIPS, "uv": kvr // N_CHIPS, "uq_n": qr // N_CHIPS, "uq_r": qr // N_CHIPS,
                    "dkv": (d // N_CHIPS) * (kvr + LANE) // d}
        assert all(r % ROW_ALIGN == 0 for r in per_chip.values()), per_chip
        self.small_rows = _round_up(-(-n_small // d), SUBLANE)
        self.repl_rows = _round_up(-(-n_repl // (N_CHIPS * d)), SUBLANE)
        per_chip["rest"] = _round_up(per_chip["dkv"] + self.small_rows + self.repl_rows, ROW_ALIGN)
        self.rows = per_chip
        self.w_off, self.w_rows = {}, {}
        for group, order in W_GROUPS.items():
            off = 0
            for k in order:
                self.w_off[k] = off
                off += per_chip[k]
            assert off % ROW_ALIGN == 0, (group, off)
            self.w_rows[group] = off
        self.g_off, self.c_off, self.c_rows, self.g_rows = {}, {}, {}, {}
        for group, order in G_GROUPS.items():
            off = 0
            for k in order:
                self.c_off[k] = off
                self.g_off[k] = N_CHIPS * off
                off += per_chip[k]
            self.c_rows[group] = off
            self.g_rows[group] = N_CHIPS * off


def _dot(a, b):
    return jnp.dot(a, b, preferred_element_type=F32)


def _dot_nt(a, b):
    return lax.dot_general(a, b, (((1,), (1,)), ((), ())), preferred_element_type=F32)


def _dot_tn(a, b):
    return lax.dot_general(a, b, (((0,), (0,)), ((), ())), preferred_element_type=F32)


def _rinv(x):
    return lax.rsqrt(jnp.mean(x * x, axis=-1, keepdims=True) + EPS)


def _rms_bwd(x, rinv, g, dy):
    z = dy * g
    dx = rinv * z - x * (rinv * rinv * rinv) * jnp.mean(z * x, axis=-1, keepdims=True)
    dg = jnp.sum(dy * (x * rinv), axis=0, keepdims=True)
    return dx, dg


def _softplus(z):
    return jnp.maximum(z, 0.0) + jnp.log1p(jnp.exp(-jnp.abs(z)))


def _neg_expm1(z):
    series = -z * (1 + z / 2 * (1 + z / 3 * (1 + z / 4 * (1 + z / 5 * (1 + z / 6)))))
    return jnp.where(z > -0.3, series, 1.0 - jnp.exp(z))


def _swap_halves(x):
    w = x.shape[1]
    lane = lax.broadcasted_iota(jnp.int32, x.shape, 1)
    return jnp.where(lane % QK_ROPE < QK_ROPE // 2, pltpu.roll(x, w - QK_ROPE // 2, 1), pltpu.roll(x, QK_ROPE // 2, 1))


def _rope_tables(seq):
    pos = jnp.arange(seq, dtype=F32)
    inv = ROPE_THETA ** (-jnp.arange(0, QK_ROPE, 2, dtype=F32) / QK_ROPE)
    ang = pos[:, None] * inv[None, :]
    cos, sin = jnp.cos(ang), jnp.sin(ang)
    zero = jnp.zeros((seq, LANE - QK_ROPE), F32)
    return jnp.concatenate([cos, cos, zero], 1), jnp.concatenate([-sin, sin, zero], 1)


def _fetch(wg_ref, lay, key, dst, sems, k0):
    rows = lay.rows[key]
    return [pltpu.make_async_copy(wg_ref.at[p, pl.ds(lay.w_off[key], rows), :], dst.at[pl.ds(p * rows, rows), :], sems.at[k0 + p])
            for p in range(N_CHIPS)]


def _gates(xb, wrg_ref, brg, wig_ref, big, nblocks):
    xbb = xb.astype(BF16)
    rg = [_dot(xbb[:, n * LANE:(n + 1) * LANE], wrg_ref[n]) for n in range(nblocks)]
    ig = [_dot(xbb[:, n * LANE:(n + 1) * LANE], wig_ref[n]) for n in range(nblocks)]
    r = jax.nn.sigmoid(jnp.concatenate(rg, axis=1) + brg)
    i = jax.nn.sigmoid(jnp.concatenate(ig, axis=1) + big)
    return r, i


def _conv(xpad, cw_ref, cb, tb):
    return (cb + cw_ref[3:4, :] * xpad[pl.ds(8, tb), :] + cw_ref[2:3, :] * xpad[pl.ds(7, tb), :]
            + cw_ref[1:2, :] * xpad[pl.ds(6, tb), :] + cw_ref[0:1, :] * xpad[pl.ds(5, tb), :])


def _fa_fwd(x, wg, wb_half, w, lay, seq, tb):
    t_all, d = x.shape
    dr = lay.dr
    nblocks = w["w_rg"].shape[0]
    nblk = seq // tb
    nt = tb // SUBLANE
    nsteps = (t_all // seq) * nblk

    def body(x_ref, wg_ref, wbh_ref, na, cw, cb, wrg, brg, wig, big, lam, x1_ref, u_ref, hs_ref, h_ref, y_ref, wb_ref,
             wint, wout, xpad, a_s, b_s, carry, sems, send_sems, recv_sems, local_sem):
        step_no = pl.program_id(0) * nblk + pl.program_id(1)
        gather = _Gather8(wbh_ref, wb_ref, send_sems, recv_sems, local_sem)

        @pl.when(step_no == 0)
        def _():
            gather.start()
            cps = _fetch(wg_ref, lay, "in_a", wint, sems, 0) + _fetch(wg_ref, lay, "out_a", wout, sems, N_CHIPS)
            for cp in cps:
                cp.start()
            for cp in cps:
                cp.wait()

        @pl.when(step_no == nsteps // 2)
        def _():
            gather.forward()

        @pl.when(pl.program_id(1) == 0)
        def _():
            xpad[pl.ds(0, 8), :] = jnp.zeros((8, dr), F32)
            carry[...] = jnp.zeros((8, dr), F32)

        xv = x_ref[...]
        h = (xv * _rinv(xv) * na[...]).astype(BF16)
        h_ref[...] = h
        u = _dot_nt(h, wint[...])
        u_ref[...] = u
        xpre, gate = u[:, :dr], u[:, dr:]
        xpad[pl.ds(8, tb), :] = xpre
        xb = _conv(xpad, cw, cb[...], tb)
        xpad[pl.ds(0, 8), :] = xpre[tb - 8:, :]
        r, i = _gates(xb, wrg, brg[...], wig, big[...], nblocks)
        log_a = -LRU_C * r * _softplus(-lam[...])
        a_s[...] = jnp.exp(log_a)
        b_s[...] = jnp.sqrt(_neg_expm1(2.0 * log_a)) * (i * xb)
        row = lax.broadcasted_iota(jnp.int32, (8, dr), 0)

        def step(t, c):
            r0 = pl.multiple_of(t * 8, 8)
            a = a_s[pl.ds(r0, 8), :]
            b = b_s[pl.ds(r0, 8), :]
            for s in (1, 2, 4):
                m = row >= s
                a_sh = jnp.where(m, pltpu.roll(a, s, 0), 1.0)
                b_sh = jnp.where(m, pltpu.roll(b, s, 0), 0.0)
                b = a * b_sh + b
                a = a * a_sh
            hh = b + a * c
            hs_ref[pl.ds(r0, 8), :] = hh
            return jnp.broadcast_to(hh[7:8, :], hh.shape)

        carry[...] = lax.fori_loop(0, nt, step, carry[...])
        y = (hs_ref[...] * (gate * jax.nn.sigmoid(gate))).astype(BF16)
        y_ref[...] = y
        x1_ref[...] = xv + _dot(y, wout[...])

        @pl.when(step_no == nsteps - 1)
        def _():
            gather.finish()

    tok = lambda c: pl.BlockSpec((tb, c), lambda b, j: (b * nblk + j, 0))
    consts = [w["norm_a"], w["conv_w"], w["conv_b"], w["w_rg"], w["b_rg"], w["w_ig"], w["b_ig"], w["lru_lambda"]]
    return pl.pallas_call(
        body, name="fa_fwd", grid=(t_all // seq, nblk),
        in_specs=[tok(d), ANY, ANY] + [_full(c.shape) for c in consts],
        out_specs=[tok(d), tok(2 * dr), tok(dr), tok(d), tok(dr), ANY],
        out_shape=[_sds((t_all, d)), _sds((t_all, 2 * dr)), _sds((t_all, dr)), _sds((t_all, d), BF16), _sds((t_all, dr), BF16),
                   _sds((8,) + wb_half.shape, BF16)],
        scratch_shapes=[pltpu.VMEM((2 * dr, d), BF16), pltpu.VMEM((dr, d), BF16), pltpu.VMEM((tb + 8, dr), F32), pltpu.VMEM((tb, dr), F32),
                        pltpu.VMEM((tb, dr), F32), pltpu.VMEM((8, dr), F32), pltpu.SemaphoreType.DMA((2 * N_CHIPS,))] + GATHER_SEMS,
        compiler_params=_params(2),
    )(x, wg, wb_half, *consts)


def _fb_fwd(x1, wg, w, lay, cos_t, sin_t, seq, tb):
    t_all, d = x1.shape
    kvr, qr, hv = lay.kvr, lay.qr, lay.hv
    nheads = hv // LANE
    npos = seq // tb

    def body(x_ref, wg_ref, nkv, nb, wdkv, kvn, qn, cos_ref, sin_ref,
             qn_o, qr_o, kn_o, kr_o, v_o, ub_o, ckr_o, hb_o, hk_o, cq_o, ckv_o, winb, wuk, wuv, wuqn, wuqr, sems):
        @pl.when(pl.program_id(0) == 0)
        def _():
            cps = []
            for n, (key, dst) in enumerate((("in_b", winb), ("uk", wuk), ("uv", wuv), ("uq_n", wuqn), ("uq_r", wuqr))):
                cps += _fetch(wg_ref, lay, key, dst, sems, n * N_CHIPS)
            for cp in cps:
                cp.start()
            for cp in cps:
                cp.wait()

        xv = x_ref[...]
        xh = xv * _rinv(xv)
        hk = (xh * nkv[...]).astype(BF16)
        hb = (xh * nb[...]).astype(BF16)
        hk_o[...] = hk
        hb_o[...] = hb
        cos, sin = cos_ref[...], sin_ref[...]
        ckr = _dot(hk, wdkv[...])
        ckr_o[...] = ckr
        ckv_pre = ckr[:, :kvr]
        ckv = (ckv_pre * _rinv(ckv_pre) * kvn[...]).astype(BF16)
        ckv_o[...] = ckv
        kr = ckr[:, kvr:]
        kr_o[...] = (kr * cos + _swap_halves(kr) * sin).astype(BF16)
        kn_o[...] = _dot(ckv, wuk[...]).astype(BF16)
        v_o[...] = _dot(ckv, wuv[...]).astype(BF16)
        ub = _dot_nt(hb, winb[...])
        ub_o[...] = ub
        cq_pre = ub[:, :qr]
        cq = (cq_pre * _rinv(cq_pre) * qn[...]).astype(BF16)
        cq_o[...] = cq
        qn_o[...] = (_dot(cq, wuqn[...]) * Q_SCALE).astype(BF16)
        qrope = _dot(cq, wuqr[...]) * Q_SCALE
        qr_o[...] = (qrope * jnp.tile(cos, (1, nheads)) + _swap_halves(qrope) * jnp.tile(sin, (1, nheads))).astype(BF16)

    tok = lambda c: pl.BlockSpec((tb, c), lambda i: (i, 0))
    pos = pl.BlockSpec((tb, LANE), lambda i: (i % npos, 0))
    consts = [w["norm_kv"], w["norm_b"], w["w_dkv_p"], w["kv_norm"], w["q_norm"]]
    outs = [(hv, BF16), (hv, BF16), (hv, BF16), (LANE, BF16), (hv, BF16), (qr + hv, F32), (kvr + LANE, F32), (d, BF16), (d, BF16), (qr, BF16), (kvr, BF16)]
    return pl.pallas_call(
        body, name="fb_fwd", grid=(t_all // tb,),
        in_specs=[tok(d), ANY] + [_full(c.shape) for c in consts] + [pos, pos],
        out_specs=[tok(c) for c, _ in outs],
        out_shape=[_sds((t_all, c), dt) for c, dt in outs],
        scratch_shapes=[pltpu.VMEM((qr + hv, d), BF16), pltpu.VMEM((kvr, d), BF16), pltpu.VMEM((kvr, d), BF16), pltpu.VMEM((qr, d), BF16),
                        pltpu.VMEM((qr, d), BF16), pltpu.SemaphoreType.DMA((5 * N_CHIPS,))],
        compiler_params=_params(1),
    )(x1, wg, *consts, cos_t, sin_t)


def _causal_mask(row0, col0, nrows, ncols):
    rows = row0 + lax.broadcasted_iota(jnp.int32, (nrows, ncols), 0)
    cols = col0 + lax.broadcasted_iota(jnp.int32, (nrows, ncols), 1)
    return cols <= rows


def _attn_fwd(qn, qr, kn, kr, v, seq, ta):
    t_all, hv = qn.shape
    nheads, nb, na = hv // LANE, t_all // seq, seq // ta

    reps = ta // LANE

    def body(qn_ref, qr_ref, kn_ref, kr_ref, v_ref, o_ref, lse_ref, m_s, l_s, acc_s):
        i = pl.program_id(2)
        m_s[...] = jnp.full((ta, LANE), -1e30, F32)
        l_s[...] = jnp.zeros((ta, LANE), F32)
        acc_s[...] = jnp.zeros((ta, LANE), F32)
        q = jnp.concatenate([qn_ref[...], qr_ref[...]], axis=1)

        def tile(j, diagonal):
            cols = pl.ds(pl.multiple_of(j * ta, ta), ta)
            k = jnp.concatenate([kn_ref[cols, :], kr_ref[cols, :]], axis=1)
            s = _dot_nt(q, k)
            if diagonal:
                s = jnp.where(_causal_mask(0, 0, ta, ta), s, -1e30)
            m_prev = m_s[...]
            m_new = jnp.maximum(m_prev, jnp.max(s, axis=1, keepdims=True))
            p = jnp.exp2(s - jnp.tile(m_new, (1, reps)))
            alpha = jnp.exp2(m_prev - m_new)
            l_s[...] = alpha * l_s[...] + jnp.sum(p, axis=1, keepdims=True)
            acc_s[...] = alpha * acc_s[...] + _dot(p.astype(BF16), v_ref[cols, :])
            m_s[...] = m_new

        def off_diagonal(j, carry):
            tile(j, False)
            return carry

        lax.fori_loop(0, i, off_diagonal, 0)
        tile(i, True)
        o_ref[...] = acc_s[...] / l_s[...]
        lse_ref[...] = m_s[...] + jnp.log2(l_s[...])

    qspec = pl.BlockSpec((ta, LANE), lambda b, h, i: (b * na + i, h))
    kspec = pl.BlockSpec((seq, LANE), lambda b, h, i: (b, h))
    krspec = pl.BlockSpec((seq, LANE), lambda b, h, i: (b, 0))
    return pl.pallas_call(
        body, name="attn_fwd", grid=(nb, nheads, na),
        in_specs=[qspec, qspec, kspec, krspec, kspec],
        out_specs=[qspec, qspec],
        out_shape=[_sds((t_all, hv)), _sds((t_all, hv))],
        scratch_shapes=[pltpu.VMEM((ta, LANE), F32)] * 3,
        compiler_params=_params(3),
    )(qn, qr, kn, kr, v)


def _attn_bwd(qn, qr, kn, kr, v, do, lse, delta, seq, ta):
    t_all, hv = qn.shape
    nheads, nb, na = hv // LANE, t_all // seq, seq // ta

    reps = ta // LANE
    nchunks = ta // ATTN_ROWS

    def body(qn_ref, qr_ref, kn_ref, kr_ref, v_ref, do_ref, lse_ref, dl_ref, dqn_ref, dqr_ref, dkn_ref, dkr_ref, dv_ref,
             s_s, dp_s, p_s, ds_s, dk_s, dv_s):
        j = pl.program_id(2)

        @pl.when(j == 0)
        def _():
            dqn_ref[...] = jnp.zeros((seq, LANE), F32)
            dqr_ref[...] = jnp.zeros((seq, LANE), F32)

        dk_s[...] = jnp.zeros((ta, 2 * LANE), F32)
        dv_s[...] = jnp.zeros((ta, LANE), F32)
        k = jnp.concatenate([kn_ref[...], kr_ref[...]], axis=1)
        vv = v_ref[...]

        def tile(i, diagonal):
            rows_i = pl.ds(pl.multiple_of(i * ta, ta), ta)
            q = jnp.concatenate([qn_ref[rows_i, :], qr_ref[rows_i, :]], axis=1)
            do_b = do_ref[rows_i, :]
            s_s[...] = _dot_nt(q, k)
            dp_s[...] = _dot_nt(do_b, vv)

            def chunk(c, carry):
                rows = pl.ds(pl.multiple_of(c * ATTN_ROWS, ATTN_ROWS), ATTN_ROWS)
                seq_rows = pl.ds(pl.multiple_of(i * ta + c * ATTN_ROWS, ATTN_ROWS), ATTN_ROWS)
                s = s_s[rows, :]
                if diagonal:
                    s = jnp.where(_causal_mask(c * ATTN_ROWS, 0, ATTN_ROWS, ta), s, -1e30)
                p = jnp.exp2(s - jnp.tile(lse_ref[seq_rows, :], (1, reps)))
                p_s[rows, :] = p.astype(BF16)
                ds_s[rows, :] = (p * (dp_s[rows, :] - jnp.tile(dl_ref[seq_rows, :], (1, reps)))).astype(BF16)
                return carry

            lax.fori_loop(0, nchunks, chunk, 0, unroll=True)
            dv_s[...] += _dot_tn(p_s[...], do_b)
            ds = ds_s[...]
            dk_s[...] += _dot_tn(ds, q)
            dq = _dot(ds, k)
            dqn_ref[rows_i, :] += dq[:, :LANE]
            dqr_ref[rows_i, :] += dq[:, LANE:]

        def off_diagonal(i, carry):
            tile(i, False)
            return carry

        tile(j, True)
        lax.fori_loop(j + 1, na, off_diagonal, 0)
        dkn_ref[...] = (dk_s[:, :LANE] * LN2).astype(BF16)
        dkr_ref[...] = dk_s[:, LANE:] * LN2
        dv_ref[...] = dv_s[...].astype(BF16)

    qspec = pl.BlockSpec((seq, LANE), lambda b, h, j: (b, h))
    kspec = pl.BlockSpec((ta, LANE), lambda b, h, j: (b * na + j, h))
    krspec = pl.BlockSpec((ta, LANE), lambda b, h, j: (b * na + j, 0))
    return pl.pallas_call(
        body, name="attn_bwd", grid=(nb, nheads, na),
        in_specs=[qspec, qspec, kspec, krspec, kspec, qspec, qspec, qspec],
        out_specs=[qspec, qspec, kspec, kspec, kspec],
        out_shape=[_sds((t_all, hv)), _sds((t_all, hv)), _sds((t_all, hv), BF16), _sds((t_all, hv)), _sds((t_all, hv), BF16)],
        scratch_shapes=[pltpu.VMEM((ta, ta), F32), pltpu.VMEM((ta, ta), F32), pltpu.VMEM((ta, ta), BF16), pltpu.VMEM((ta, ta), BF16),
                        pltpu.VMEM((ta, 2 * LANE), F32), pltpu.VMEM((ta, LANE), F32)],
        compiler_params=_params(3),
    )(qn, qr, kn, kr, v, do, lse, delta)


def _head(o, ub, x1, target, wg, w, lay, tb):
    t_all, d = x1.shape
    hv, qr = lay.hv, lay.qr
    nheads = hv // LANE

    def body(o_ref, ub_ref, x1_ref, tg_ref, wg_ref, gf, loss_ref, dgf_ref, yb_ref, dx2_ref, do_ref, dg_ref, dl_ref, wob, sems):
        @pl.when(pl.program_id(0) == 0)
        def _():
            cps = _fetch(wg_ref, lay, "out_b", wob, sems, 0)
            for cp in cps:
                cp.start()
            loss_ref[...] = jnp.zeros((1, LANE), F32)
            dgf_ref[...] = jnp.zeros((1, d), F32)
            for cp in cps:
                cp.wait()

        ov = o_ref[...]
        g = ub_ref[:, qr:]
        sg = jax.nn.sigmoid(g)
        silu = g * sg
        yb = (ov * silu).astype(BF16)
        yb_ref[...] = yb
        x2 = x1_ref[...] + _dot(yb, wob[...])
        rinv = _rinv(x2)
        err = x2 * rinv * gf[...] - tg_ref[...]
        loss_ref[...] += (0.5 / d) * jnp.sum(jnp.sum(err * err, axis=1, keepdims=True), axis=0, keepdims=True)
        dx2, dgf = _rms_bwd(x2, rinv, gf[...], err * (1.0 / d))
        dgf_ref[...] += dgf
        dx2_ref[...] = dx2
        dyb = _dot_nt(dx2.astype(BF16), wob[...])
        dov = dyb * silu
        do_ref[...] = dov.astype(BF16)
        dg_ref[...] = dyb * ov * (sg * (1.0 + g * (1.0 - sg)))
        prod = dov * ov
        dl_ref[...] = jnp.concatenate(
            [jnp.broadcast_to(jnp.sum(prod[:, n * LANE:(n + 1) * LANE], axis=1, keepdims=True), (tb, LANE)) for n in range(nheads)], axis=1)

    tok = lambda c: pl.BlockSpec((tb, c), lambda i: (i, 0))
    return pl.pallas_call(
        body, name="head", grid=(t_all // tb,),
        in_specs=[tok(hv), tok(qr + hv), tok(d), tok(d), ANY, _full((1, d))],
        out_specs=[_full((1, LANE)), _full((1, d)), tok(hv), tok(d), tok(hv), tok(hv), tok(hv)],
        out_shape=[_sds((1, LANE)), _sds((1, d)), _sds((t_all, hv), BF16), _sds((t_all, d)), _sds((t_all, hv), BF16), _sds((t_all, hv)), _sds((t_all, hv))],
        scratch_shapes=[pltpu.VMEM((hv, d), BF16), pltpu.SemaphoreType.DMA((N_CHIPS,))],
        compiler_params=_params(1),
    )(o, ub, x1, target, wg, w["final_norm"])


def _fb_bwd(dqn, dqr, dkn, dkr, dv, dgate, ub, ckr, x1, dx2, wg, w, lay, cos_t, sin_t, seq, tb):
    t_all, d = x1.shape
    hv, qr, kvr = lay.hv, lay.qr, lay.kvr
    nheads = hv // LANE
    npos = seq // tb

    def body(dqn_ref, dqr_ref, dkn_ref, dkr_ref, dv_ref, dg_ref, ub_ref, ckr_ref, x1_ref, dx2_ref, wg_ref,
             qn, nb, kvn, wdkv, nkv, cos_ref, sin_ref,
             dx1_ref, dqrp_ref, dqnp_ref, dub_ref, dckr_ref, dqn_g, dnb_g, dkvn_g, dnkv_g, winb, wuk, wuv, wuqn, wuqr, sems):
        @pl.when(pl.program_id(0) == 0)
        def _():
            cps = []
            for n, (key, dst) in enumerate((("in_b", winb), ("uk", wuk), ("uv", wuv), ("uq_n", wuqn), ("uq_r", wuqr))):
                cps += _fetch(wg_ref, lay, key, dst, sems, n * N_CHIPS)
            for cp in cps:
                cp.start()
            dqn_g[...] = jnp.zeros((1, qr), F32)
            dnb_g[...] = jnp.zeros((1, d), F32)
            dkvn_g[...] = jnp.zeros((1, kvr), F32)
            dnkv_g[...] = jnp.zeros((1, d), F32)
            for cp in cps:
                cp.wait()

        cos, sin = cos_ref[...], sin_ref[...]
        xv = x1_ref[...]
        rinv1 = _rinv(xv)
        dqr_v = dqr_ref[...] * ATTN_SCALE
        dqr_pre = (dqr_v * jnp.tile(cos, (1, nheads)) + _swap_halves(dqr_v * jnp.tile(sin, (1, nheads)))).astype(BF16)
        dqrp_ref[...] = dqr_pre
        dqn_pre = (dqn_ref[...] * ATTN_SCALE).astype(BF16)
        dqnp_ref[...] = dqn_pre
        dcq = _dot_nt(dqn_pre, wuqn[...]) + _dot_nt(dqr_pre, wuqr[...])
        cq_pre = ub_ref[:, :qr]
        dcq_pre, g1 = _rms_bwd(cq_pre, _rinv(cq_pre), qn[...], dcq)
        dqn_g[...] += g1
        dub = jnp.concatenate([dcq_pre, dg_ref[...]], axis=1).astype(BF16)
        dub_ref[...] = dub
        dx1_b, g2 = _rms_bwd(xv, rinv1, nb[...], _dot(dub, winb[...]))
        dnb_g[...] += g2
        dkr_all = dkr_ref[...]
        dkr_sum = dkr_all[:, :LANE]
        for n in range(1, nheads):
            dkr_sum = dkr_sum + dkr_all[:, n * LANE:(n + 1) * LANE]
        dckr_rope = dkr_sum * cos + _swap_halves(dkr_sum * sin)
        dckv = _dot_nt(dkn_ref[...].astype(BF16), wuk[...]) + _dot_nt(dv_ref[...].astype(BF16), wuv[...])
        ckv_pre = ckr_ref[:, :kvr]
        dckv_pre, g3 = _rms_bwd(ckv_pre, _rinv(ckv_pre), kvn[...], dckv)
        dkvn_g[...] += g3
        dckr = jnp.concatenate([dckv_pre, dckr_rope], axis=1).astype(BF16)
        dckr_ref[...] = dckr
        dx1_kv, g4 = _rms_bwd(xv, rinv1, nkv[...], _dot_nt(dckr, wdkv[...]))
        dnkv_g[...] += g4
        dx1_ref[...] = dx2_ref[...] + dx1_b + dx1_kv

    tok = lambda c: pl.BlockSpec((tb, c), lambda i: (i, 0))
    pos = pl.BlockSpec((tb, LANE), lambda i: (i % npos, 0))
    consts = [w["q_norm"], w["norm_b"], w["kv_norm"], w["w_dkv_p"], w["norm_kv"]]
    return pl.pallas_call(
        body, name="fb_bwd", grid=(t_all // tb,),
        in_specs=[tok(hv)] * 6 + [tok(qr + hv), tok(kvr + LANE), tok(d), tok(d), ANY] + [_full(c.shape) for c in consts] + [pos, pos],
        out_specs=[tok(d), tok(hv), tok(hv), tok(qr + hv), tok(kvr + LANE), _full((1, qr)), _full((1, d)), _full((1, kvr)), _full((1, d))],
        out_shape=[_sds((t_all, d)), _sds((t_all, hv), BF16), _sds((t_all, hv), BF16), _sds((t_all, qr + hv), BF16), _sds((t_all, kvr + LANE), BF16),
                   _sds((1, qr)), _sds((1, d)), _sds((1, kvr)), _sds((1, d))],
        scratch_shapes=[pltpu.VMEM((qr + hv, d), BF16), pltpu.VMEM((kvr, d), BF16), pltpu.VMEM((kvr, d), BF16), pltpu.VMEM((qr, d), BF16),
                        pltpu.VMEM((qr, d), BF16), pltpu.SemaphoreType.DMA((5 * N_CHIPS,))],
        compiler_params=_params(1),
    )(dqn, dqr, dkn, dkr, dv, dgate, ub, ckr, x1, dx2, wg, *consts, cos_t, sin_t)


def _fa_bwd(dx1, x, u, hs, wg, part16, part32, w, lay, seq, tb):
    t_all, d = x.shape
    dr = lay.dr
    nblocks = w["w_rg"].shape[0]
    nblk = seq // tb
    nt = tb // SUBLANE
    per8 = tb // 8

    def body(dx1_ref, x_ref, u_ref, uh_ref, hs_ref, hh_ref, wg_ref, p16_ref, p32_ref, na, cw, cb, wrg, brg, wig, big, lam,
             gx_ref, du_ref, dna_g, dcw_g, dcb_g, dbrg_g, dbig_g, dlam_g, dwrg_g, dwig_g, got_ref, own_ref,
             wint, wout, xpad, hpad, a_s, d_s, g_s, dxpad, carry, sems, send_sems, recv_sems, local_sem):
        b, jj = pl.program_id(0), pl.program_id(1)
        first_block = jj == nblk - 1
        scatter = _Scatter(p16_ref, p32_ref, got_ref, own_ref, send_sems, recv_sems, local_sem, lay, G_GROUPS["early"])

        @pl.when((b == 0) & (jj == 0))
        def _():
            scatter.start()
            cps = _fetch(wg_ref, lay, "in_a", wint, sems, 0) + _fetch(wg_ref, lay, "out_a", wout, sems, N_CHIPS)
            for cp in cps:
                cp.start()
            dna_g[...] = jnp.zeros((1, d), F32)
            dcw_g[...] = jnp.zeros((4, dr), F32)
            dcb_g[...] = jnp.zeros((1, dr), F32)
            dbrg_g[...] = jnp.zeros((1, dr), F32)
            dbig_g[...] = jnp.zeros((1, dr), F32)
            dlam_g[...] = jnp.zeros((1, dr), F32)
            dwrg_g[...] = jnp.zeros((nblocks, LANE, LANE), F32)
            dwig_g[...] = jnp.zeros((nblocks, LANE, LANE), F32)
            for cp in cps:
                cp.wait()

        @pl.when(jj == 0)
        def _():
            dxpad[pl.ds(tb, 8), :] = jnp.zeros((8, dr), F32)
            carry[...] = jnp.zeros((8, dr), F32)

        keep = jnp.where(first_block, 0.0, 1.0)
        dx1v = dx1_ref[...]
        gate = u_ref[:, dr:]
        xpad[pl.ds(0, 8), :] = uh_ref[...] * keep
        xpad[pl.ds(8, tb), :] = u_ref[:, :dr]
        hpad[pl.ds(0, 8), :] = hh_ref[...] * keep
        hpad[pl.ds(8, tb), :] = hs_ref[...]
        xb = _conv(xpad, cw, cb[...], tb)
        xbb = xb.astype(BF16)
        r, i = _gates(xb, wrg, brg[...], wig, big[...], nblocks)
        sp = _softplus(-lam[...])
        log_a = -LRU_C * r * sp
        a = jnp.exp(log_a)
        nem = _neg_expm1(2.0 * log_a)
        mult = jnp.sqrt(nem)
        sg = jax.nn.sigmoid(gate)
        dy = _dot_nt(dx1v.astype(BF16), wout[...])
        hsv = hs_ref[...]
        dgate = dy * hsv * (sg * (1.0 + gate * (1.0 - sg)))
        a_s[...] = a
        d_s[...] = dy * (gate * sg)
        row = lax.broadcasted_iota(jnp.int32, (8, dr), 0)

        def step(k, c):
            r0 = pl.multiple_of((nt - 1 - k) * 8, 8)
            av = a_s[pl.ds(r0, 8), :]
            dv = d_s[pl.ds(r0, 8), :]
            qv = av * dv
            for s in (1, 2, 4):
                m = row < 8 - s
                a_sh = jnp.where(m, pltpu.roll(av, 8 - s, 0), 1.0)
                q_sh = jnp.where(m, pltpu.roll(qv, 8 - s, 0), 0.0)
                qv = qv + av * q_sh
                av = av * a_sh
            qv = qv + av * c
            g_s[pl.ds(r0, 8), :] = dv + jnp.where(row < 7, pltpu.roll(qv, 7, 0), c)
            return jnp.broadcast_to(qv[0:1, :], qv.shape)

        carry[...] = lax.fori_loop(0, nt, step, carry[...])
        g = g_s[...]
        ix = i * xb
        dlog_a = g * hpad[pl.ds(7, tb), :] * a - g * ix * ((1.0 - nem) / mult)
        dix = g * mult
        dlam_g[...] += -jax.nn.sigmoid(-lam[...]) * jnp.sum(dlog_a * (-LRU_C * r), axis=0, keepdims=True)
        drg = dlog_a * (-LRU_C * sp) * r * (1.0 - r)
        dig = dix * xb * i * (1.0 - i)
        dbrg_g[...] += jnp.sum(drg, axis=0, keepdims=True)
        dbig_g[...] += jnp.sum(dig, axis=0, keepdims=True)
        drgb, digb = drg.astype(BF16), dig.astype(BF16)
        back = []
        for n in range(nblocks):
            cols = slice(n * LANE, (n + 1) * LANE)
            dwrg_g[n] += _dot_tn(xbb[:, cols], drgb[:, cols])
            dwig_g[n] += _dot_tn(xbb[:, cols], digb[:, cols])
            back.append(_dot_nt(drgb[:, cols], wrg[n]) + _dot_nt(digb[:, cols], wig[n]))
        dxb = dix * i + jnp.concatenate(back, axis=1)
        dcb_g[...] += jnp.sum(dxb, axis=0, keepdims=True)
        for k in range(4):
            dcw_g[k:k + 1, :] += jnp.sum(dxb * xpad[pl.ds(5 + k, tb), :], axis=0, keepdims=True)
        dxpad[pl.ds(0, tb), :] = dxb
        dxpre = (cw[3:4, :] * dxb + cw[2:3, :] * dxpad[pl.ds(1, tb), :] + cw[1:2, :] * dxpad[pl.ds(2, tb), :]
                 + cw[0:1, :] * dxpad[pl.ds(3, tb), :])
        dxpad[pl.ds(tb, 8), :] = dxb[:8, :]
        du = jnp.concatenate([dxpre, dgate], axis=1).astype(BF16)
        du_ref[...] = du
        xv = x_ref[...]
        dxa, g1 = _rms_bwd(xv, _rinv(xv), na[...], _dot(du, wint[...]))
        dna_g[...] += g1
        gx_ref[...] = dx1v + dxa

        @pl.when((b == t_all // seq - 1) & (jj == nblk - 1))
        def _():
            scatter.finish()

    blk = lambda b, j: b * nblk + (nblk - 1 - j)
    tok = lambda c: pl.BlockSpec((tb, c), lambda b, j: (blk(b, j), 0))
    halo = pl.BlockSpec((8, dr), lambda b, j: (jnp.maximum(blk(b, j) * per8 - 1, 0), 0))
    consts = [w["norm_a"], w["conv_w"], w["conv_b"], w["w_rg"], w["b_rg"], w["w_ig"], w["b_ig"], w["lru_lambda"]]
    vec = lambda c: _full((1, c))
    blocks3 = (nblocks, LANE, LANE)
    return pl.pallas_call(
        body, name="fa_bwd", grid=(t_all // seq, nblk),
        in_specs=[tok(d), tok(d), tok(2 * dr), halo, tok(dr), halo, ANY, ANY, ANY] + [_full(c.shape) for c in consts],
        out_specs=[tok(d), tok(2 * dr), vec(d), _full((4, dr)), vec(dr), vec(dr), vec(dr), vec(dr), _full(blocks3), _full(blocks3), ANY, ANY],
        out_shape=[_sds((t_all, d)), _sds((t_all, 2 * dr), BF16), _sds((1, d)), _sds((4, dr)), _sds((1, dr)), _sds((1, dr)), _sds((1, dr)),
                   _sds((1, dr)), _sds(blocks3), _sds(blocks3)] + _scatter_shapes(lay, "early", part16.shape[1]),
        scratch_shapes=[pltpu.VMEM((2 * dr, d), BF16), pltpu.VMEM((dr, d), BF16), pltpu.VMEM((tb + 8, dr), F32), pltpu.VMEM((tb + 8, dr), F32),
                        pltpu.VMEM((tb, dr), F32), pltpu.VMEM((tb, dr), F32), pltpu.VMEM((tb, dr), F32), pltpu.VMEM((tb + 8, dr), F32),
                        pltpu.VMEM((8, dr), F32), pltpu.SemaphoreType.DMA((2 * N_CHIPS,))] + SCATTER_SEMS,
        compiler_params=_params(2),
    )(dx1, x, u, u, hs, hs, wg, part16, part32, *consts)


def _mm_into(gbuf, a, b, off, name, bt):
    t_all, m = a.shape
    n = b.shape[1]
    nsplit = 2 if m >= 1024 and (m // 2) % LANE == 0 else 1
    mh = m // nsplit
    nt = t_all // bt

    def body(a_ref, b_ref, g_ref, o_ref, acc, sems):
        del g_ref
        part, t = pl.program_id(0), pl.program_id(1)

        def out_copy(h):
            return pltpu.make_async_copy(acc.at[h], o_ref.at[pl.ds(off + h * mh, mh), :], sems.at[h])

        prod = _dot_tn(a_ref[...].astype(BF16), b_ref[...].astype(BF16))
        for h in range(nsplit):
            @pl.when((part == h) & (t == 0))
            def _():
                acc[h] = prod

            @pl.when((part == h) & (t > 0))
            def _():
                acc[h] += prod

            @pl.when((part == h) & (t == nt - 1))
            def _():
                out_copy(h).start()

        @pl.when((part == nsplit - 1) & (t == nt - 1))
        def _():
            for h in range(nsplit):
                out_copy(h).wait()

    return pl.pallas_call(
        body, name=name, grid=(nsplit, nt),
        in_specs=[pl.BlockSpec((bt, mh), lambda h, t: (t, h)), pl.BlockSpec((bt, n), lambda h, t: (t, 0)), ANY],
        out_specs=ANY, out_shape=_sds(gbuf.shape), input_output_aliases={2: 0},
        scratch_shapes=[pltpu.VMEM((nsplit, mh, n), F32), pltpu.SemaphoreType.DMA((nsplit,))],
        compiler_params=_params(2),
    )(a, b, gbuf)


def _mm_tn(a, b, name, bt):
    t_all, m = a.shape
    n = b.shape[1]

    def body(a_ref, b_ref, o_ref):
        @pl.when(pl.program_id(0) == 0)
        def _():
            o_ref[...] = jnp.zeros((m, n), F32)

        o_ref[...] += _dot_tn(a_ref[...].astype(BF16), b_ref[...].astype(BF16))

    return pl.pallas_call(
        body, name=name, grid=(t_all // bt,),
        in_specs=[pl.BlockSpec((bt, m), lambda t: (t, 0)), pl.BlockSpec((bt, n), lambda t: (t, 0))],
        out_specs=_full((m, n)), out_shape=_sds((m, n)),
        compiler_params=_params(1),
    )(a, b)


class _Gather8:
    def __init__(self, x_ref, out_ref, send_sems, recv_sems, local_sem):
        x, y, c = _place()
        self.c, self.me, self.sibling = c, (x, y, c), (x, y, 1 - c)
        self.chips = [(1 - x, y), (x, 1 - y), (1 - x, 1 - y)]
        self.x_ref, self.out_ref, self.send_sems, self.recv_sems, self.local_sem = x_ref, out_ref, send_sems, recv_sems, local_sem

    def _slot(self, px, py, pc):
        return self.out_ref.at[4 * px + 2 * py + pc]

    def _copy(self, k, blk, to, src=None):
        return pltpu.make_async_remote_copy(
            src_ref=self._slot(*blk) if src is None else src, dst_ref=self._slot(*blk), send_sem=self.send_sems.at[k],
            recv_sem=self.recv_sems.at[k], device_id=to, device_id_type=MESH)

    def _mine(self):
        return pltpu.make_async_copy(self.x_ref, self._slot(*self.me), self.local_sem)

    def _first(self):
        return [self._copy(0, self.me, self.sibling, src=self.x_ref)] + [
            self._copy(1 + j, self.me, (*chip, self.c), src=self.x_ref) for j, chip in enumerate(self.chips)]

    def _passed(self):
        return [self._copy(4 + j, (*chip, self.c), self.sibling) for j, chip in enumerate(self.chips)]

    def start(self):
        self._mine().start()
        for cp in self._first():
            cp.start()

    def forward(self):
        passed = self._passed()
        for j, chip in enumerate(self.chips):
            self._copy(1 + j, (*chip, self.c), self.me).wait_recv()
            passed[j].start()

    def finish(self):
        self._copy(0, self.sibling, self.me).wait_recv()
        for j, chip in enumerate(self.chips):
            self._copy(4 + j, (*chip, 1 - self.c), self.me).wait_recv()
        for cp in self._first() + self._passed():
            cp.wait_send()
        self._mine().wait()


class _Scatter:
    def __init__(self, p16_ref, p32_ref, got_ref, own_ref, send_sems, recv_sems, local_sem, lay, order):
        self.x, self.y, self.c = _place()
        self.chips = [(1 - self.x, self.y), (self.x, 1 - self.y), (1 - self.x, 1 - self.y)]
        self.refs = (p16_ref, p32_ref, got_ref, own_ref, send_sems, recv_sems, local_sem)
        self.lay, self.order = lay, order

    def _rows_of(self, ref, key, chip):
        start = pl.multiple_of(self.lay.g_off[key] + chip * self.lay.rows[key], ROW_ALIGN)
        return ref.at[pl.ds(start, self.lay.rows[key]), :]

    def _compact(self, ref, key):
        return ref.at[pl.ds(self.lay.c_off[key], self.lay.rows[key]), :]

    def start(self):
        p16_ref, p32_ref, got_ref, own_ref, send_sems, recv_sems, local_sem = self.refs
        for key in self.order:
            pltpu.make_async_copy(self._rows_of(p32_ref, key, 2 * self.x + self.y), self._compact(own_ref, key), local_sem).start()
        for k, (px, py) in enumerate(self.chips):
            for key in self.order:
                pltpu.make_async_remote_copy(
                    src_ref=self._rows_of(p16_ref, key, 2 * px + py), dst_ref=self._compact(got_ref.at[k], key), send_sem=send_sems.at[k],
                    recv_sem=recv_sems.at[k], device_id=(px, py, self.c), device_id_type=MESH).start()

    def finish(self):
        _, _, got_ref, own_ref, send_sems, recv_sems, local_sem = self.refs
        for k, (px, py) in enumerate(self.chips):
            pltpu.make_async_remote_copy(src_ref=got_ref.at[k], dst_ref=got_ref.at[k], send_sem=send_sems.at[k], recv_sem=recv_sems.at[k],
                                         device_id=(px, py, self.c), device_id_type=MESH).wait()
        pltpu.make_async_copy(own_ref, own_ref, local_sem).wait()


SCATTER_SEMS = [pltpu.SemaphoreType.DMA((3,)), pltpu.SemaphoreType.DMA((3,)), pltpu.SemaphoreType.DMA]
GATHER_SEMS = [pltpu.SemaphoreType.DMA((7,)), pltpu.SemaphoreType.DMA((7,)), pltpu.SemaphoreType.DMA]


def _all_gather8(blocks, name):
    nb = len(blocks)

    def body(*refs):
        x_refs, out_refs = refs[:nb], refs[nb:2 * nb]
        send_sems, recv_sems, local_sems = refs[2 * nb:]
        gathers = [_Gather8(x_refs[n], out_refs[n], send_sems.at[n], recv_sems.at[n], local_sems.at[n]) for n in range(nb)]
        for g in gathers:
            g.start()
        for g in gathers:
            g.forward()
        for g in gathers:
            g.finish()

    return pl.pallas_call(
        body, name=name, out_shape=[_sds((8,) + b.shape, b.dtype) for b in blocks], in_specs=[ANY] * nb, out_specs=[ANY] * nb,
        scratch_shapes=[pltpu.SemaphoreType.DMA((nb, 7)), pltpu.SemaphoreType.DMA((nb, 7)), pltpu.SemaphoreType.DMA((nb,))],
    )(*blocks)


def _swap_sibling(srcs, name, half_cols=False):
    n = len(srcs)
    halves = [s.shape[1] // 2 if half_cols else s.shape[1] for s in srcs]

    def body(*refs):
        src_refs, out_refs, send_sems, recv_sems = refs[:n], refs[n:2 * n], refs[2 * n], refs[2 * n + 1]
        x, y, c = _place()
        copies = []
        for k in range(n):
            part = src_refs[k].at[:, pl.ds(pl.multiple_of((1 - c) * halves[k], LANE), halves[k])] if half_cols else src_refs[k]
            copies.append(pltpu.make_async_remote_copy(src_ref=part, dst_ref=out_refs[k], send_sem=send_sems.at[k], recv_sem=recv_sems.at[k],
                                                       device_id=(x, y, 1 - c), device_id_type=MESH))
        for cp in copies:
            cp.start()
        for cp in copies:
            cp.wait()

    return pl.pallas_call(
        body, name=name, out_shape=[_sds((s.shape[0], h), s.dtype) for s, h in zip(srcs, halves)], in_specs=[ANY] * n, out_specs=[ANY] * n,
        scratch_shapes=[pltpu.SemaphoreType.DMA((n,)), pltpu.SemaphoreType.DMA((n,))],
    )(*srcs)


def _scatter_shapes(lay, group, half):
    return [_sds((3, lay.c_rows[group], half), BF16), _sds((lay.c_rows[group], half), F32)]


def _scatter_chips(part16, part32, lay, group, name):
    def body(p16_ref, p32_ref, got_ref, own_ref, send_sems, recv_sems, local_sem):
        sc = _Scatter(p16_ref, p32_ref, got_ref, own_ref, send_sems, recv_sems, local_sem, lay, G_GROUPS[group])
        sc.start()
        sc.finish()

    return pl.pallas_call(
        body, name=name, out_shape=_scatter_shapes(lay, group, part16.shape[1]), in_specs=[ANY, ANY], out_specs=[ANY, ANY],
        scratch_shapes=SCATTER_SEMS,
    )(part16, part32)


def _sum_sibling(gbuf, got, cidx, name):
    rows, d = gbuf.shape
    half = d // 2
    rb = _row_block(rows)

    def body(c_ref, g_ref, r_ref, o32_ref, o16_ref):
        del c_ref
        s = g_ref[...] + r_ref[...]
        o32_ref[...] = s
        o16_ref[...] = s.astype(BF16)

    plain = pl.BlockSpec((rb, half), lambda i, c: (i, 0))
    return pl.pallas_call(
        body, name=name,
        grid_spec=pltpu.PrefetchScalarGridSpec(num_scalar_prefetch=1, grid=(rows // rb,),
                                               in_specs=[pl.BlockSpec((rb, half), lambda i, c: (i, c[0])), plain], out_specs=[plain, plain]),
        out_shape=[_sds((rows, half)), _sds((rows, half), BF16)], compiler_params=_params(1),
    )(cidx, gbuf, got)


def _sum_chips(own, got, name):
    rows, half = own.shape
    rb = _row_block(rows)

    def body(a_ref, b_ref, o_ref):
        o_ref[...] = ((a_ref[...] + b_ref[0].astype(F32)) + b_ref[1].astype(F32)) + b_ref[2].astype(F32)

    spec = pl.BlockSpec((rb, half), lambda i: (i, 0))
    return pl.pallas_call(
        body, name=name, grid=(rows // rb,), in_specs=[spec, pl.BlockSpec((3, rb, half), lambda i: (0, i, 0))], out_specs=spec,
        out_shape=_sds((rows, half)), compiler_params=_params(1),
    )(own, got)


def _adamw(w, g, m, v):
    m = ADAM_B1 * m + (1.0 - ADAM_B1) * g
    v = ADAM_B2 * v + (1.0 - ADAM_B2) * (g * g)
    m_hat = m / (1.0 - ADAM_B1 ** ADAM_STEP)
    v_hat = v / (1.0 - ADAM_B2 ** ADAM_STEP)
    return -ADAM_LR * (m_hat / (jnp.sqrt(v_hat) + ADAM_EPS) + ADAM_WD * w), m, v


def _adamw_rows(name, w, g, m, v):
    _, rows, cols = w.shape
    rb = _row_block(rows, 256)

    def body(w_ref, g_ref, m_ref, v_ref, d_ref, mo_ref, vo_ref):
        d_ref[...], mo_ref[...], vo_ref[...] = _adamw(w_ref[...], g_ref[...], m_ref[...], v_ref[...])

    spec = pl.BlockSpec((1, rb, cols), lambda i: (0, i, 0))
    return pl.pallas_call(
        body, name=name, grid=(rows // rb,), in_specs=[spec] * 4, out_specs=[spec] * 3, out_shape=[_sds(w.shape)] * 3,
        compiler_params=_params(1),
    )(w, g, m, v)


def _adamw_group(ws, gs, ms, vs):
    n = len(ws)

    def body(*refs):
        for k in range(n):
            w_ref, g_ref, m_ref, v_ref = (refs[j * n + k] for j in range(4))
            outs = _adamw(w_ref[...], g_ref[...], m_ref[...], v_ref[...])
            for j in range(3):
                refs[(4 + j) * n + k][...] = outs[j]

    outs = pl.pallas_call(
        body, name="adamw_small", out_shape=[_sds(w.shape) for w in ws] * 3,
        compiler_params=pltpu.CompilerParams(vmem_limit_bytes=VMEM_LIMIT),
    )(*ws, *gs, *ms, *vs)
    return outs[:n], outs[n:2 * n], outs[2 * n:]


def _gather_weights(sh, lay):
    x, y, c = _place()
    d = lay.d
    uq = sh["w_uq"][0].astype(BF16)
    parts = {
        "in_b": sh["w_in_b"][0].T.astype(BF16), "in_a": sh["w_in_a"][0].T.astype(BF16), "out_a": sh["w_out_a"][0].astype(BF16),
        "out_b": sh["w_out_b"][0].astype(BF16), "uk": sh["w_uk"].astype(BF16).reshape(-1, d), "uv": sh["w_uv"].astype(BF16).reshape(-1, d),
        "uq_n": uq[:, :, :QK_NOPE].reshape(-1, d), "uq_r": jnp.pad(uq[:, :, QK_NOPE:], ((0, 0), (0, 0), (0, LANE - QK_ROPE))).reshape(-1, d),
        "dkv": jnp.pad(sh["w_dkv"].astype(BF16), ((0, 0), (0, LANE - QK_ROPE))).reshape(-1, d),
    }
    halves = {}
    for group, order in W_GROUPS.items():
        stack = jnp.concatenate([parts[k] for k in order], axis=0).reshape(2, lay.w_rows[group] // 2, d)
        halves[group] = lax.dynamic_index_in_dim(stack, c, 0, keepdims=False)
    small = jnp.concatenate([sh[k].reshape(-1) for k in SMALL])
    n_small = small.shape[0]
    width = _round_up(n_small, 2 * SUBLANE * LANE) // (2 * SUBLANE)
    small = jnp.pad(small, (0, 2 * SUBLANE * width - n_small)).reshape(2, SUBLANE, width)
    wg, sg = _all_gather8([halves["a"], lax.dynamic_index_in_dim(small, c, 0, keepdims=False)], "ag_weights")
    wg = wg.reshape(N_CHIPS, lay.w_rows["a"], d)
    sg = sg.reshape(N_CHIPS, 2 * SUBLANE * width)
    full, off = {}, 0
    for k in SMALL:
        n = sh[k].size
        piece = sg[:, off:off + n]
        off += n
        if k == "conv_w":
            full[k] = piece.reshape(N_CHIPS, 4, n // 4).transpose(1, 0, 2).reshape(4, n)
        else:
            full[k] = piece.reshape(1, N_CHIPS * n)
    return wg, halves["b"], full


def _chip_split(g, taps=False):
    if taps:
        n = g.shape[1] // N_CHIPS
        return g.reshape(4, N_CHIPS, n).transpose(1, 0, 2).reshape(N_CHIPS, 4 * n)
    return g.reshape(N_CHIPS, -1)


def kernel(x, norm_a, w_in_a, conv_w, conv_b, w_rg, b_rg, w_ig, b_ig, lru_lambda, w_out_a, norm_kv, w_dkv, kv_norm, w_uk, w_uv, norm_b, w_in_b, q_norm, w_uq, w_out_b, final_norm, loss_target, m_norm_a, m_w_in_a, m_conv_w, m_conv_b, m_w_rg, m_b_rg, m_w_ig, m_b_ig, m_lru_lambda, m_w_out_a, m_norm_kv, m_w_dkv, m_kv_norm, m_w_uk, m_w_uv, m_norm_b, m_w_in_b, m_q_norm, m_w_uq, m_w_out_b, m_final_norm, v_norm_a, v_w_in_a, v_conv_w, v_conv_b, v_w_rg, v_b_rg, v_w_ig, v_b_ig, v_lru_lambda, v_w_out_a, v_norm_kv, v_w_dkv, v_kv_norm, v_w_uk, v_w_uv, v_norm_b, v_w_in_b, v_q_norm, v_w_uq, v_w_out_b, v_final_norm):
    given = dict(locals())
    sh = {k: given[k] for k in WEIGHTS}
    xi, yi, ci = _place()
    nb, seq, d = x.shape
    t_all = nb * seq
    tb_a, tb_b, ta, bt = min(TOKENS_A, seq), min(TOKENS_B, seq), min(TOKENS_ATTN, seq), min(TOKENS_MM, t_all)
    dr = conv_b.shape[1] * N_CHIPS
    qr, kvr, nheads = q_norm.shape[1], kv_norm.shape[0], w_uk.shape[1]
    hv = nheads * LANE
    n_small = sum(sh[k].size for k in SMALL)
    n_repl = sum(sh[k].size for k in REPL)
    lay = _Layout(d, dr, qr, kvr, hv, n_small, n_repl)
    half = d // 2

    wga, wb_half, w = _gather_weights(sh, lay)
    w.update({"w_rg": w_rg[0].astype(BF16), "w_ig": w_ig[0].astype(BF16), "norm_kv": norm_kv[None, :],
              "kv_norm": kv_norm[None, :], "final_norm": final_norm[None, :], "norm_b": norm_b, "q_norm": q_norm})
    cos_t, sin_t = _rope_tables(seq)
    cidx = jnp.reshape(ci, (1,)).astype(jnp.int32)

    x0 = x.reshape(t_all, d)
    x1, u, hs, h, y, wgb = _fa_fwd(x0, wga, wb_half, w, lay, seq, tb_a)
    wgb = wgb.reshape(N_CHIPS, lay.w_rows["b"], d)
    w["w_dkv_p"] = wgb[:, lay.w_off["dkv"]:lay.w_off["dkv"] + lay.rows["dkv"], :].reshape(d, kvr + LANE)
    qn, qrp, kn, kr, v, ub, ckr, hb, hk, cq, ckv = _fb_fwd(x1, wgb, w, lay, cos_t, sin_t, seq, tb_b)
    o, lse = _attn_fwd(qn, qrp, kn, kr, v, seq, ta)
    loss, g_final_norm, yb, dx2, do, dgate, delta = _head(o, ub, x1, loss_target.reshape(t_all, d), wgb, w, lay, tb_b)
    dqn, dqr, dkn, dkr, dv = _attn_bwd(qn, qrp, kn, kr, v, do, lse, delta, seq, ta)
    dx1, dqr_pre, dqn_pre, dub, dckr, g_q_norm, g_norm_b, g_kv_norm, g_norm_kv = _fb_bwd(
        dqn, dqr, dkn, dkr, dv, dgate, ub, ckr, x1, dx2, wgb, w, lay, cos_t, sin_t, seq, tb_b)
    loss = lax.psum(loss[0, 0], ("x", "y", "c"))

    gbuf = lax.empty((lay.g_rows["early"], d), F32)
    for key, a, b in (("in_b", dub, hb), ("out_a", y, dx1), ("out_b", yb, dx2), ("uk", ckv, dkn), ("uv", ckv, dv), ("uq_n", cq, dqn_pre),
                      ("uq_r", cq, dqr_pre)):
        gbuf = _mm_into(gbuf, a, b, lay.g_off[key], "dw_" + key, bt)
    g_dkv = _mm_tn(hk, dckr, "dw_dkv", bt)
    (got,) = _swap_sibling([gbuf], "rs_sibling_early", half_cols=True)
    part32, part16 = _sum_sibling(gbuf, got, cidx, "rs_sum_sibling_early")
    gx, du, g_norm_a, g_conv_w, g_conv_b, g_b_rg, g_b_ig, g_lam, g_w_rg, g_w_ig, others, own = _fa_bwd(
        dx1, x0, u, hs, wga, part16, part32, w, lay, seq, tb_a)
    mine_early = _sum_chips(own, others, "rs_sum_chips_early")

    small = jnp.concatenate([_chip_split(g_norm_a), _chip_split(g_conv_w, taps=True), _chip_split(g_conv_b), _chip_split(g_b_rg),
                             _chip_split(g_b_ig), _chip_split(g_lam)], axis=1)
    small = jnp.pad(small, ((0, 0), (0, lay.small_rows * d - small.shape[1]))).reshape(N_CHIPS, lay.small_rows, d)
    repl_parts = {"w_rg": g_w_rg, "w_ig": g_w_ig, "norm_kv": g_norm_kv, "kv_norm": g_kv_norm, "norm_b": g_norm_b, "q_norm": g_q_norm,
                  "final_norm": g_final_norm}
    repl = jnp.concatenate([repl_parts[k].reshape(-1) for k in REPL])
    repl = jnp.pad(repl, (0, N_CHIPS * lay.repl_rows * d - n_repl)).reshape(N_CHIPS, lay.repl_rows, d)
    pad_rows = lay.rows["rest"] - lay.rows["dkv"] - lay.small_rows - lay.repl_rows
    rest = jnp.concatenate([g_dkv.reshape(N_CHIPS, lay.rows["dkv"], d), small, repl, jnp.zeros((N_CHIPS, pad_rows, d), F32)], axis=1)
    gbuf = lax.dynamic_update_slice(lax.empty((lay.g_rows["late"], d), F32), rest.reshape(N_CHIPS * lay.rows["rest"], d), (lay.g_off["rest"], 0))
    gbuf = _mm_into(gbuf, du, h, lay.g_off["in_a"], "dw_in_a", bt)
    (got,) = _swap_sibling([gbuf], "rs_sibling_late", half_cols=True)
    part32, part16 = _sum_sibling(gbuf, got, cidx, "rs_sum_sibling_late")
    others, own = _scatter_chips(part16, part32, lay, "late", "rs_chips_late")
    mine_late = _sum_chips(own, others, "rs_sum_chips_late")

    theirs_early, theirs_late = _swap_sibling([mine_early, mine_late], "rs_return")
    red = {}
    for group, mine, theirs in (("early", mine_early, theirs_early), ("late", mine_late, theirs_late)):
        red[group] = jnp.concatenate([jnp.where(ci == 0, mine, theirs), jnp.where(ci == 0, theirs, mine)], axis=1)
    r0 = lay.c_off["rest"] + lay.rows["dkv"] + lay.small_rows
    (rep_all,) = _all_gather8([mine_late[r0:r0 + lay.repl_rows]], "ag_rep")
    rep_flat = rep_all.reshape(N_CHIPS, 2, lay.repl_rows, half).transpose(0, 2, 1, 3).reshape(-1)

    def rows(key):
        group = "late" if key in G_GROUPS["late"] else "early"
        return red[group][lay.c_off[key]:lay.c_off[key] + lay.rows[key]]

    grads = {"w_in_b": rows("in_b").T[None], "w_in_a": rows("in_a").T[None], "w_out_a": rows("out_a")[None], "w_out_b": rows("out_b")[None],
             "w_uk": rows("uk").reshape(w_uk.shape), "w_uv": rows("uv").reshape(w_uv.shape)}
    uq_n = rows("uq_n").reshape(qr // N_CHIPS, nheads, LANE)
    uq_r = rows("uq_r").reshape(qr // N_CHIPS, nheads, LANE)[:, :, :QK_ROPE]
    grads["w_uq"] = jnp.concatenate([uq_n, uq_r], axis=2)[None]
    rest_red = rows("rest")
    grads["w_dkv"] = rest_red[:lay.rows["dkv"]].reshape(d // N_CHIPS, kvr + LANE)[:, :kvr + QK_ROPE]
    small_red = rest_red[lay.rows["dkv"]:lay.rows["dkv"] + lay.small_rows].reshape(-1)
    off = 0
    for k in SMALL:
        n = sh[k].size
        grads[k] = small_red[off:off + n].reshape(sh[k].shape)
        off += n
    off = 0
    for k in REPL:
        n = sh[k].size
        grads[k] = rep_flat[off:off + n].reshape(sh[k].shape)
        off += n

    new = {}
    for k in ("w_in_a", "w_in_b", "w_out_a", "w_out_b"):
        new[k] = _adamw_rows("adamw_" + k, sh[k], grads[k], given["m_" + k], given["v_" + k])
    rest_names = [k for k in WEIGHTS if k not in new]
    as2d = lambda a: a[None, :] if a.ndim == 1 else a
    ds, ms, vs = _adamw_group([as2d(sh[k]) for k in rest_names], [as2d(grads[k]) for k in rest_names],
                              [as2d(given["m_" + k]) for k in rest_names], [as2d(given["v_" + k]) for k in rest_names])
    for n, k in enumerate(rest_names):
        new[k] = tuple(a.reshape(sh[k].shape) for a in (ds[n], ms[n], vs[n]))
    return (loss, gx.reshape(nb, seq, d), *[grads[k] for k in WEIGHTS], *[new[k][0] for k in WEIGHTS], *[new[k][1] for k in WEIGHTS],
            *[new[k][2] for k in WEIGHTS])
```

```python
import jax
import jax.numpy as jnp
from jax import lax
from jax.experimental import pallas as pl
from jax.experimental.pallas import tpu as pltpu

F32, BF16 = jnp.float32, jnp.bfloat16
EPS = 1e-6
LRU_C = 8.0
ROPE_THETA = 10000.0
QK_NOPE, QK_ROPE = 128, 64
ATTN_SCALE = (QK_NOPE + QK_ROPE) ** -0.5
LN2 = 0.6931471805599453
Q_SCALE = ATTN_SCALE / LN2
ATTN_ROWS = 64
LANE = 128
SUBLANE = 8
ROW_ALIGN = 32
VMEM_LIMIT = 60000 * 1024
ADAM_LR, ADAM_B1, ADAM_B2, ADAM_EPS, ADAM_WD, ADAM_STEP = 0.001, 0.9, 0.999, 1e-08, 0.01, 10
MESH = pl.DeviceIdType.MESH
ANY = pl.BlockSpec(memory_space=pl.ANY)
N_CHIPS = 4
TOKENS_A, TOKENS_B, TOKENS_ATTN, TOKENS_MM = 256, 256, 512, 1024

SMALL = ("norm_a", "conv_w", "conv_b", "b_rg", "b_ig", "lru_lambda")
REPL = ("w_rg", "w_ig", "norm_kv", "kv_norm", "norm_b", "q_norm", "final_norm")
WEIGHTS = ("norm_a", "w_in_a", "conv_w", "conv_b", "w_rg", "b_rg", "w_ig", "b_ig", "lru_lambda", "w_out_a", "norm_kv",
           "w_dkv", "kv_norm", "w_uk", "w_uv", "norm_b", "w_in_b", "q_norm", "w_uq", "w_out_b", "final_norm")
W_GROUPS = {"a": ("in_a", "out_a"), "b": ("in_b", "out_b", "uk", "uv", "uq_n", "uq_r", "dkv")}
G_GROUPS = {"early": ("in_b", "out_a", "out_b", "uk", "uv", "uq_n", "uq_r"), "late": ("in_a", "rest")}


def _sds(shape, dtype=F32):
    return jax.ShapeDtypeStruct(tuple(shape), dtype)


def _params(n_grid):
    return pltpu.CompilerParams(dimension_semantics=("arbitrary",) * n_grid, vmem_limit_bytes=VMEM_LIMIT)


def _full(shape):
    nd = len(shape)
    return pl.BlockSpec(tuple(shape), lambda *g: (0,) * nd)


def _round_up(n, k):
    return -(-n // k) * k


def _row_block(rows, cap=512):
    best = SUBLANE
    for r in range(SUBLANE, min(rows, cap) + 1, SUBLANE):
        if rows % r == 0:
            best = r
    return best


def _place():
    return lax.axis_index("x"), lax.axis_index("y"), lax.axis_index("c")


class _Layout:
    def __init__(self, d, dr, qr, kvr, hv, n_small, n_repl):
        assert hv == d, "the packed rows are D_MODEL wide, which must equal heads * 128"
        self.d, self.dr, self.qr, self.kvr, self.hv = d, dr, qr, kvr, hv
        per_chip = {"in_b": (qr + hv) // N_CHIPS, "in_a": 2 * dr // N_CHIPS, "out_a": dr // N_CHIPS, "out_b": hv // N_CHIPS,
                    "uk": kvr // N_CHIPS, "uv": kvr // N_CHIPS, "uq_n": qr // N_CHIPS, "uq_r": qr // N_CHIPS,
                    "dkv": (d // N_CHIPS) * (kvr + LANE) // d}
        assert all(r % ROW_ALIGN == 0 for r in per_chip.values()), per_chip
        self.small_rows = _round_up(-(-n_small // d), SUBLANE)
        self.repl_rows = _round_up(-(-n_repl // (N_CHIPS * d)), SUBLANE)
        per_chip["rest"] = _round_up(per_chip["dkv"] + self.small_rows + self.repl_rows, ROW_ALIGN)
        self.rows = per_chip
        self.w_off, self.w_rows = {}, {}
        for group, order in W_GROUPS.items():
            off = 0
            for k in order:
                self.w_off[k] = off
                off += per_chip[k]
            assert off % ROW_ALIGN == 0, (group, off)
            self.w_rows[group] = off
        self.g_off, self.c_off, self.c_rows, self.g_rows = {}, {}, {}, {}
        for group, order in G_GROUPS.items():
            off = 0
            for k in order:
                self.c_off[k] = off
                self.g_off[k] = N_CHIPS * off
                off += per_chip[k]
            self.c_rows[group] = off
            self.g_rows[group] = N_CHIPS * off


def _dot(a, b):
    return jnp.dot(a, b, preferred_element_type=F32)


def _dot_nt(a, b):
    return lax.dot_general(a, b, (((1,), (1,)), ((), ())), preferred_element_type=F32)


def _dot_tn(a, b):
    return lax.dot_general(a, b, (((0,), (0,)), ((), ())), preferred_element_type=F32)


def _rinv(x):
    return lax.rsqrt(jnp.mean(x * x, axis=-1, keepdims=True) + EPS)


def _rms_bwd(x, rinv, g, dy):
    z = dy * g
    dx = rinv * z - x * (rinv * rinv * rinv) * jnp.mean(z * x, axis=-1, keepdims=True)
    dg = jnp.sum(dy * (x * rinv), axis=0, keepdims=True)
    return dx, dg


def _softplus(z):
    return jnp.maximum(z, 0.0) + jnp.log1p(jnp.exp(-jnp.abs(z)))


def _sigmoid(x):
    return 0.5 * jnp.tanh(0.5 * x) + 0.5


def _decay(log_a):
    a = jnp.exp(log_a)
    a2 = a * a
    return a, a2, -jnp.tanh(log_a) * (a2 + 1.0)


def _swap_halves(x):
    w = x.shape[1]
    lane = lax.broadcasted_iota(jnp.int32, x.shape, 1)
    return jnp.where(lane % QK_ROPE < QK_ROPE // 2, pltpu.roll(x, w - QK_ROPE // 2, 1), pltpu.roll(x, QK_ROPE // 2, 1))


def _rope_tables(seq):
    pos = jnp.arange(seq, dtype=F32)
    inv = ROPE_THETA ** (-jnp.arange(0, QK_ROPE, 2, dtype=F32) / QK_ROPE)
    ang = pos[:, None] * inv[None, :]
    cos, sin = jnp.cos(ang), jnp.sin(ang)
    zero = jnp.zeros((seq, LANE - QK_ROPE), F32)
    return jnp.concatenate([cos, cos, zero], 1), jnp.concatenate([-sin, sin, zero], 1)


def _fetch(wg_ref, lay, key, dst, sems, k0):
    rows = lay.rows[key]
    return [pltpu.make_async_copy(wg_ref.at[p, pl.ds(lay.w_off[key], rows), :], dst.at[pl.ds(p * rows, rows), :], sems.at[k0 + p])
            for p in range(N_CHIPS)]


def _gates(xb, wrg_ref, brg, wig_ref, big, nblocks):
    xbb = xb.astype(BF16)
    rg = [_dot(xbb[:, n * LANE:(n + 1) * LANE], wrg_ref[n]) for n in range(nblocks)]
    ig = [_dot(xbb[:, n * LANE:(n + 1) * LANE], wig_ref[n]) for n in range(nblocks)]
    r = _sigmoid(jnp.concatenate(rg, axis=1) + brg)
    i = _sigmoid(jnp.concatenate(ig, axis=1) + big)
    return r, i


def _conv(xpad, cw_ref, cb, tb):
    return (cb + cw_ref[3:4, :] * xpad[pl.ds(8, tb), :] + cw_ref[2:3, :] * xpad[pl.ds(7, tb), :]
            + cw_ref[1:2, :] * xpad[pl.ds(6, tb), :] + cw_ref[0:1, :] * xpad[pl.ds(5, tb), :])


def _fa_fwd(x, wg, wb_half, w, lay, seq, tb):
    t_all, d = x.shape
    dr = lay.dr
    nblocks = w["w_rg"].shape[0]
    nblk = seq // tb
    nt = tb // SUBLANE
    nsteps = (t_all // seq) * nblk

    def body(x_ref, wg_ref, wbh_ref, na, cw, cb, wrg, brg, wig, big, lam, x1_ref, u_ref, hs_ref, h_ref, y_ref, xb_ref, wb_ref,
             wint, wout, xpad, a_s, b_s, carry, sems, send_sems, recv_sems, local_sem):
        step_no = pl.program_id(0) * nblk + pl.program_id(1)
        gather = _Gather8(wbh_ref, wb_ref, send_sems, recv_sems, local_sem)

        @pl.when(step_no == 0)
        def _():
            gather.start()
            cps = _fetch(wg_ref, lay, "in_a", wint, sems, 0) + _fetch(wg_ref, lay, "out_a", wout, sems, N_CHIPS)
            for cp in cps:
                cp.start()
            for cp in cps:
                cp.wait()

        @pl.when(step_no == nsteps // 2)
        def _():
            gather.forward()

        @pl.when(pl.program_id(1) == 0)
        def _():
            xpad[pl.ds(0, 8), :] = jnp.zeros((8, dr), F32)
            carry[...] = jnp.zeros((8, dr), F32)

        xv = x_ref[...]
        h = (xv * _rinv(xv) * na[...]).astype(BF16)
        h_ref[...] = h
        u = _dot_nt(h, wint[...])
        u_ref[...] = u
        xpre, gate = u[:, :dr], u[:, dr:]
        xpad[pl.ds(8, tb), :] = xpre
        xb = _conv(xpad, cw, cb[...], tb)
        xb_ref[...] = xb
        xpad[pl.ds(0, 8), :] = xpre[tb - 8:, :]
        r, i = _gates(xb, wrg, brg[...], wig, big[...], nblocks)
        log_a = -LRU_C * r * _softplus(-lam[...])
        a, _, nem = _decay(log_a)
        a_s[...] = a
        b_s[...] = jnp.sqrt(nem) * (i * xb)
        row = lax.broadcasted_iota(jnp.int32, (8, dr), 0)

        def step(t, c):
            r0 = pl.multiple_of(t * 8, 8)
            a = a_s[pl.ds(r0, 8), :]
            b = b_s[pl.ds(r0, 8), :]
            for s in (1, 2, 4):
                m = row >= s
                a_sh = jnp.where(m, pltpu.roll(a, s, 0), 1.0)
                b_sh = jnp.where(m, pltpu.roll(b, s, 0), 0.0)
                b = a * b_sh + b
                a = a * a_sh
            hh = b + a * c
            hs_ref[pl.ds(r0, 8), :] = hh
            return jnp.broadcast_to(hh[7:8, :], hh.shape)

        carry[...] = lax.fori_loop(0, nt, step, carry[...])
        y = (hs_ref[...] * (gate * _sigmoid(gate))).astype(BF16)
        y_ref[...] = y
        x1_ref[...] = xv + _dot(y, wout[...])

        @pl.when(step_no == nsteps - 1)
        def _():
            gather.finish()

    tok = lambda c: pl.BlockSpec((tb, c), lambda b, j: (b * nblk + j, 0))
    consts = [w["norm_a"], w["conv_w"], w["conv_b"], w["w_rg"], w["b_rg"], w["w_ig"], w["b_ig"], w["lru_lambda"]]
    return pl.pallas_call(
        body, name="fa_fwd", grid=(t_all // seq, nblk),
        in_specs=[tok(d), ANY, ANY] + [_full(c.shape) for c in consts],
        out_specs=[tok(d), tok(2 * dr), tok(dr), tok(d), tok(dr), tok(dr), ANY],
        out_shape=[_sds((t_all, d)), _sds((t_all, 2 * dr)), _sds((t_all, dr)), _sds((t_all, d), BF16), _sds((t_all, dr), BF16),
                   _sds((t_all, dr)), _sds((8,) + wb_half.shape, BF16)],
        scratch_shapes=[pltpu.VMEM((2 * dr, d), BF16), pltpu.VMEM((dr, d), BF16), pltpu.VMEM((tb + 8, dr), F32), pltpu.VMEM((tb, dr), F32),
                        pltpu.VMEM((tb, dr), F32), pltpu.VMEM((8, dr), F32), pltpu.SemaphoreType.DMA((2 * N_CHIPS,))] + GATHER_SEMS,
        compiler_params=_params(2),
    )(x, wg, wb_half, *consts)


def _fb_fwd(x1, wg, w, lay, cos_t, sin_t, seq, tb):
    t_all, d = x1.shape
    kvr, qr, hv = lay.kvr, lay.qr, lay.hv
    nheads = hv // LANE
    npos = seq // tb

    def body(x_ref, wg_ref, nkv, nb, wdkv, kvn, qn, cos_ref, sin_ref,
             qn_o, qr_o, kn_o, kr_o, v_o, ub_o, ckr_o, hb_o, hk_o, cq_o, ckv_o, winb, wuk, wuv, wuqn, wuqr, sems):
        @pl.when(pl.program_id(0) == 0)
        def _():
            cps = []
            for n, (key, dst) in enumerate((("in_b", winb), ("uk", wuk), ("uv", wuv), ("uq_n", wuqn), ("uq_r", wuqr))):
                cps += _fetch(wg_ref, lay, key, dst, sems, n * N_CHIPS)
            for cp in cps:
                cp.start()
            for cp in cps:
                cp.wait()

        xv = x_ref[...]
        xh = xv * _rinv(xv)
        hk = (xh * nkv[...]).astype(BF16)
        hb = (xh * nb[...]).astype(BF16)
        hk_o[...] = hk
        hb_o[...] = hb
        cos, sin = cos_ref[...], sin_ref[...]
        ckr = _dot(hk, wdkv[...])
        ckr_o[...] = ckr
        ckv_pre = ckr[:, :kvr]
        ckv = (ckv_pre * _rinv(ckv_pre) * kvn[...]).astype(BF16)
        ckv_o[...] = ckv
        kr = ckr[:, kvr:]
        kr_o[...] = (kr * cos + _swap_halves(kr) * sin).astype(BF16)
        kn_o[...] = _dot(ckv, wuk[...]).astype(BF16)
        v_o[...] = _dot(ckv, wuv[...]).astype(BF16)
        ub = _dot_nt(hb, winb[...])
        ub_o[...] = ub
        cq_pre = ub[:, :qr]
        cq = (cq_pre * _rinv(cq_pre) * qn[...]).astype(BF16)
        cq_o[...] = cq
        qn_o[...] = (_dot(cq, wuqn[...]) * Q_SCALE).astype(BF16)
        qrope = _dot(cq, wuqr[...]) * Q_SCALE
        qr_o[...] = (qrope * jnp.tile(cos, (1, nheads)) + _swap_halves(qrope) * jnp.tile(sin, (1, nheads))).astype(BF16)

    tok = lambda c: pl.BlockSpec((tb, c), lambda i: (i, 0))
    pos = pl.BlockSpec((tb, LANE), lambda i: (i % npos, 0))
    consts = [w["norm_kv"], w["norm_b"], w["w_dkv_p"], w["kv_norm"], w["q_norm"]]
    outs = [(hv, BF16), (hv, BF16), (hv, BF16), (LANE, BF16), (hv, BF16), (qr + hv, F32), (kvr + LANE, F32), (d, BF16), (d, BF16), (qr, BF16), (kvr, BF16)]
    return pl.pallas_call(
        body, name="fb_fwd", grid=(t_all // tb,),
        in_specs=[tok(d), ANY] + [_full(c.shape) for c in consts] + [pos, pos],
        out_specs=[tok(c) for c, _ in outs],
        out_shape=[_sds((t_all, c), dt) for c, dt in outs],
        scratch_shapes=[pltpu.VMEM((qr + hv, d), BF16), pltpu.VMEM((kvr, d), BF16), pltpu.VMEM((kvr, d), BF16), pltpu.VMEM((qr, d), BF16),
                        pltpu.VMEM((qr, d), BF16), pltpu.SemaphoreType.DMA((5 * N_CHIPS,))],
        compiler_params=_params(1),
    )(x1, wg, *consts, cos_t, sin_t)


def _causal_mask(row0, col0, nrows, ncols):
    rows = row0 + lax.broadcasted_iota(jnp.int32, (nrows, ncols), 0)
    cols = col0 + lax.broadcasted_iota(jnp.int32, (nrows, ncols), 1)
    return cols <= rows


def _attn_fwd(qn, qr, kn, kr, v, seq, ta):
    t_all, hv = qn.shape
    nheads, nb, na = hv // LANE, t_all // seq, seq // ta

    reps = ta // LANE

    def body(qn_ref, qr_ref, kn_ref, kr_ref, v_ref, o_ref, lse_ref, m_s, l_s, acc_s):
        i = pl.program_id(2)
        m_s[...] = jnp.full((ta, LANE), -1e30, F32)
        l_s[...] = jnp.zeros((ta, LANE), F32)
        acc_s[...] = jnp.zeros((ta, LANE), F32)
        q = jnp.concatenate([qn_ref[...], qr_ref[...]], axis=1)

        def tile(j, diagonal):
            cols = pl.ds(pl.multiple_of(j * ta, ta), ta)
            k = jnp.concatenate([kn_ref[cols, :], kr_ref[cols, :]], axis=1)
            s = _dot_nt(q, k)
            if diagonal:
                s = jnp.where(_causal_mask(0, 0, ta, ta), s, -1e30)
            m_prev = m_s[...]
            m_new = jnp.maximum(m_prev, jnp.max(s, axis=1, keepdims=True))
            p = jnp.exp2(s - jnp.tile(m_new, (1, reps)))
            alpha = jnp.exp2(m_prev - m_new)
            l_s[...] = alpha * l_s[...] + jnp.sum(p, axis=1, keepdims=True)
            acc_s[...] = alpha * acc_s[...] + _dot(p.astype(BF16), v_ref[cols, :])
            m_s[...] = m_new

        def off_diagonal(j, carry):
            tile(j, False)
            return carry

        lax.fori_loop(0, i, off_diagonal, 0)
        tile(i, True)
        o_ref[...] = acc_s[...] / l_s[...]
        lse_ref[...] = m_s[...] + jnp.log2(l_s[...])

    qspec = pl.BlockSpec((ta, LANE), lambda b, h, i: (b * na + i, h))
    kspec = pl.BlockSpec((seq, LANE), lambda b, h, i: (b, h))
    krspec = pl.BlockSpec((seq, LANE), lambda b, h, i: (b, 0))
    return pl.pallas_call(
        body, name="attn_fwd", grid=(nb, nheads, na),
        in_specs=[qspec, qspec, kspec, krspec, kspec],
        out_specs=[qspec, qspec],
        out_shape=[_sds((t_all, hv)), _sds((t_all, hv))],
        scratch_shapes=[pltpu.VMEM((ta, LANE), F32)] * 3,
        compiler_params=_params(3),
    )(qn, qr, kn, kr, v)


def _attn_bwd(qn, qr, kn, kr, v, do, lse, delta, seq, ta):
    t_all, hv = qn.shape
    nheads, nb, na = hv // LANE, t_all // seq, seq // ta

    reps = ta // LANE
    nchunks = ta // ATTN_ROWS

    def body(qn_ref, qr_ref, kn_ref, kr_ref, v_ref, do_ref, lse_ref, dl_ref, dqn_ref, dqr_ref, dkn_ref, dkr_ref, dv_ref,
             s_s, dp_s, p_s, ds_s, dk_s, dv_s):
        j = pl.program_id(2)

        @pl.when(j == 0)
        def _():
            dqn_ref[...] = jnp.zeros((seq, LANE), F32)
            dqr_ref[...] = jnp.zeros((seq, LANE), F32)

        dk_s[...] = jnp.zeros((ta, 2 * LANE), F32)
        dv_s[...] = jnp.zeros((ta, LANE), F32)
        k = jnp.concatenate([kn_ref[...], kr_ref[...]], axis=1)
        vv = v_ref[...]

        def tile(i, diagonal):
            rows_i = pl.ds(pl.multiple_of(i * ta, ta), ta)
            q = jnp.concatenate([qn_ref[rows_i, :], qr_ref[rows_i, :]], axis=1)
            do_b = do_ref[rows_i, :]
            s_s[...] = _dot_nt(q, k)
            dp_s[...] = _dot_nt(do_b, vv)

            def chunk(c, carry):
                rows = pl.ds(pl.multiple_of(c * ATTN_ROWS, ATTN_ROWS), ATTN_ROWS)
                seq_rows = pl.ds(pl.multiple_of(i * ta + c * ATTN_ROWS, ATTN_ROWS), ATTN_ROWS)
                s = s_s[rows, :]
                if diagonal:
                    s = jnp.where(_causal_mask(c * ATTN_ROWS, 0, ATTN_ROWS, ta), s, -1e30)
                p = jnp.exp2(s - jnp.tile(lse_ref[seq_rows, :], (1, reps)))
                p_s[rows, :] = p.astype(BF16)
                ds_s[rows, :] = (p * (dp_s[rows, :] - jnp.tile(dl_ref[seq_rows, :], (1, reps)))).astype(BF16)
                return carry

            lax.fori_loop(0, nchunks, chunk, 0, unroll=True)
            dv_s[...] += _dot_tn(p_s[...], do_b)
            ds = ds_s[...]
            dk_s[...] += _dot_tn(ds, q)
            dq = _dot(ds, k)
            dqn_ref[rows_i, :] += dq[:, :LANE]
            dqr_ref[rows_i, :] += dq[:, LANE:]

        def off_diagonal(i, carry):
            tile(i, False)
            return carry

        tile(j, True)
        lax.fori_loop(j + 1, na, off_diagonal, 0)
        dkn_ref[...] = (dk_s[:, :LANE] * LN2).astype(BF16)
        dkr_ref[...] = dk_s[:, LANE:] * LN2
        dv_ref[...] = dv_s[...].astype(BF16)

    qspec = pl.BlockSpec((seq, LANE), lambda b, h, j: (b, h))
    kspec = pl.BlockSpec((ta, LANE), lambda b, h, j: (b * na + j, h))
    krspec = pl.BlockSpec((ta, LANE), lambda b, h, j: (b * na + j, 0))
    return pl.pallas_call(
        body, name="attn_bwd", grid=(nb, nheads, na),
        in_specs=[qspec, qspec, kspec, krspec, kspec, qspec, qspec, qspec],
        out_specs=[qspec, qspec, kspec, kspec, kspec],
        out_shape=[_sds((t_all, hv)), _sds((t_all, hv)), _sds((t_all, hv), BF16), _sds((t_all, hv)), _sds((t_all, hv), BF16)],
        scratch_shapes=[pltpu.VMEM((ta, ta), F32), pltpu.VMEM((ta, ta), F32), pltpu.VMEM((ta, ta), BF16), pltpu.VMEM((ta, ta), BF16),
                        pltpu.VMEM((ta, 2 * LANE), F32), pltpu.VMEM((ta, LANE), F32)],
        compiler_params=_params(3),
    )(qn, qr, kn, kr, v, do, lse, delta)


def _head(o, ub, x1, target, wg, w, lay, tb):
    t_all, d = x1.shape
    hv, qr = lay.hv, lay.qr
    nheads = hv // LANE

    def body(o_ref, ub_ref, x1_ref, tg_ref, wg_ref, gf, loss_ref, dgf_ref, yb_ref, dx2_ref, do_ref, dg_ref, dl_ref, wob, sems):
        @pl.when(pl.program_id(0) == 0)
        def _():
            cps = _fetch(wg_ref, lay, "out_b", wob, sems, 0)
            for cp in cps:
                cp.start()
            loss_ref[...] = jnp.zeros((1, LANE), F32)
            dgf_ref[...] = jnp.zeros((1, d), F32)
            for cp in cps:
                cp.wait()

        ov = o_ref[...]
        g = ub_ref[:, qr:]
        sg = _sigmoid(g)
        silu = g * sg
        yb = (ov * silu).astype(BF16)
        yb_ref[...] = yb
        x2 = x1_ref[...] + _dot(yb, wob[...])
        rinv = _rinv(x2)
        err = x2 * rinv * gf[...] - tg_ref[...]
        loss_ref[...] += (0.5 / d) * jnp.sum(jnp.sum(err * err, axis=1, keepdims=True), axis=0, keepdims=True)
        dx2, dgf = _rms_bwd(x2, rinv, gf[...], err * (1.0 / d))
        dgf_ref[...] += dgf
        dx2_ref[...] = dx2
        dyb = _dot_nt(dx2.astype(BF16), wob[...])
        dov = dyb * silu
        do_ref[...] = dov.astype(BF16)
        dg_ref[...] = dyb * ov * (sg * (1.0 + g * (1.0 - sg)))
        prod = dov * ov
        dl_ref[...] = jnp.concatenate(
            [jnp.broadcast_to(jnp.sum(prod[:, n * LANE:(n + 1) * LANE], axis=1, keepdims=True), (tb, LANE)) for n in range(nheads)], axis=1)

    tok = lambda c: pl.BlockSpec((tb, c), lambda i: (i, 0))
    return pl.pallas_call(
        body, name="head", grid=(t_all // tb,),
        in_specs=[tok(hv), tok(qr + hv), tok(d), tok(d), ANY, _full((1, d))],
        out_specs=[_full((1, LANE)), _full((1, d)), tok(hv), tok(d), tok(hv), tok(hv), tok(hv)],
        out_shape=[_sds((1, LANE)), _sds((1, d)), _sds((t_all, hv), BF16), _sds((t_all, d)), _sds((t_all, hv), BF16), _sds((t_all, hv)), _sds((t_all, hv))],
        scratch_shapes=[pltpu.VMEM((hv, d), BF16), pltpu.SemaphoreType.DMA((N_CHIPS,))],
        compiler_params=_params(1),
    )(o, ub, x1, target, wg, w["final_norm"])


def _fb_bwd(dqn, dqr, dkn, dkr, dv, dgate, ub, ckr, x1, dx2, wg, w, lay, cos_t, sin_t, seq, tb):
    t_all, d = x1.shape
    hv, qr, kvr = lay.hv, lay.qr, lay.kvr
    nheads = hv // LANE
    npos = seq // tb

    def body(dqn_ref, dqr_ref, dkn_ref, dkr_ref, dv_ref, dg_ref, ub_ref, ckr_ref, x1_ref, dx2_ref, wg_ref,
             qn, nb, kvn, wdkv, nkv, cos_ref, sin_ref,
             dx1_ref, dqrp_ref, dqnp_ref, dub_ref, dckr_ref, dqn_g, dnb_g, dkvn_g, dnkv_g, winb, wuk, wuv, wuqn, wuqr, sems):
        @pl.when(pl.program_id(0) == 0)
        def _():
            cps = []
            for n, (key, dst) in enumerate((("in_b", winb), ("uk", wuk), ("uv", wuv), ("uq_n", wuqn), ("uq_r", wuqr))):
                cps += _fetch(wg_ref, lay, key, dst, sems, n * N_CHIPS)
            for cp in cps:
                cp.start()
            dqn_g[...] = jnp.zeros((1, qr), F32)
            dnb_g[...] = jnp.zeros((1, d), F32)
            dkvn_g[...] = jnp.zeros((1, kvr), F32)
            dnkv_g[...] = jnp.zeros((1, d), F32)
            for cp in cps:
                cp.wait()

        cos, sin = cos_ref[...], sin_ref[...]
        xv = x1_ref[...]
        rinv1 = _rinv(xv)
        dqr_v = dqr_ref[...] * ATTN_SCALE
        dqr_pre = (dqr_v * jnp.tile(cos, (1, nheads)) + _swap_halves(dqr_v * jnp.tile(sin, (1, nheads)))).astype(BF16)
        dqrp_ref[...] = dqr_pre
        dqn_pre = (dqn_ref[...] * ATTN_SCALE).astype(BF16)
        dqnp_ref[...] = dqn_pre
        dcq = _dot_nt(dqn_pre, wuqn[...]) + _dot_nt(dqr_pre, wuqr[...])
        cq_pre = ub_ref[:, :qr]
        dcq_pre, g1 = _rms_bwd(cq_pre, _rinv(cq_pre), qn[...], dcq)
        dqn_g[...] += g1
        dub = jnp.concatenate([dcq_pre, dg_ref[...]], axis=1).astype(BF16)
        dub_ref[...] = dub
        dx1_b, g2 = _rms_bwd(xv, rinv1, nb[...], _dot(dub, winb[...]))
        dnb_g[...] += g2
        dkr_all = dkr_ref[...]
        dkr_sum = dkr_all[:, :LANE]
        for n in range(1, nheads):
            dkr_sum = dkr_sum + dkr_all[:, n * LANE:(n + 1) * LANE]
        dckr_rope = dkr_sum * cos + _swap_halves(dkr_sum * sin)
        dckv = _dot_nt(dkn_ref[...].astype(BF16), wuk[...]) + _dot_nt(dv_ref[...].astype(BF16), wuv[...])
        ckv_pre = ckr_ref[:, :kvr]
        dckv_pre, g3 = _rms_bwd(ckv_pre, _rinv(ckv_pre), kvn[...], dckv)
        dkvn_g[...] += g3
        dckr = jnp.concatenate([dckv_pre, dckr_rope], axis=1).astype(BF16)
        dckr_ref[...] = dckr
        dx1_kv, g4 = _rms_bwd(xv, rinv1, nkv[...], _dot_nt(dckr, wdkv[...]))
        dnkv_g[...] += g4
        dx1_ref[...] = dx2_ref[...] + dx1_b + dx1_kv

    tok = lambda c: pl.BlockSpec((tb, c), lambda i: (i, 0))
    pos = pl.BlockSpec((tb, LANE), lambda i: (i % npos, 0))
    consts = [w["q_norm"], w["norm_b"], w["kv_norm"], w["w_dkv_p"], w["norm_kv"]]
    return pl.pallas_call(
        body, name="fb_bwd", grid=(t_all // tb,),
        in_specs=[tok(hv)] * 6 + [tok(qr + hv), tok(kvr + LANE), tok(d), tok(d), ANY] + [_full(c.shape) for c in consts] + [pos, pos],
        out_specs=[tok(d), tok(hv), tok(hv), tok(qr + hv), tok(kvr + LANE), _full((1, qr)), _full((1, d)), _full((1, kvr)), _full((1, d))],
        out_shape=[_sds((t_all, d)), _sds((t_all, hv), BF16), _sds((t_all, hv), BF16), _sds((t_all, qr + hv), BF16), _sds((t_all, kvr + LANE), BF16),
                   _sds((1, qr)), _sds((1, d)), _sds((1, kvr)), _sds((1, d))],
        scratch_shapes=[pltpu.VMEM((qr + hv, d), BF16), pltpu.VMEM((kvr, d), BF16), pltpu.VMEM((kvr, d), BF16), pltpu.VMEM((qr, d), BF16),
                        pltpu.VMEM((qr, d), BF16), pltpu.SemaphoreType.DMA((5 * N_CHIPS,))],
        compiler_params=_params(1),
    )(dqn, dqr, dkn, dkr, dv, dgate, ub, ckr, x1, dx2, wg, *consts, cos_t, sin_t)


def _fa_bwd(dx1, x, u, xb, hs, wg, part16, part32, w, lay, seq, tb):
    t_all, d = x.shape
    dr = lay.dr
    nblocks = w["w_rg"].shape[0]
    nblk = seq // tb
    nt = tb // SUBLANE
    per8 = tb // 8

    def body(dx1_ref, x_ref, u_ref, xb_ref, hs_ref, hh_ref, wg_ref, p16_ref, p32_ref, na, cw, wrg, brg, wig, big, lam,
             gx_ref, du_ref, dna_g, dcw_g, dcb_g, dbrg_g, dbig_g, dlam_g, dwrg_g, dwig_g, got_ref, own_ref,
             wint, wout, hpad, a_s, d_s, g_s, dxpad, carry, sems, send_sems, recv_sems, local_sem):
        b, jj = pl.program_id(0), pl.program_id(1)
        first_block = jj == nblk - 1
        scatter = _Scatter(p16_ref, p32_ref, got_ref, own_ref, send_sems, recv_sems, local_sem, lay, G_GROUPS["early"])

        @pl.when((b == 0) & (jj == 0))
        def _():
            scatter.start()
            cps = _fetch(wg_ref, lay, "in_a", wint, sems, 0) + _fetch(wg_ref, lay, "out_a", wout, sems, N_CHIPS)
            for cp in cps:
                cp.start()
            dna_g[...] = jnp.zeros((1, d), F32)
            dcw_g[...] = jnp.zeros((4, dr), F32)
            dcb_g[...] = jnp.zeros((1, dr), F32)
            dbrg_g[...] = jnp.zeros((1, dr), F32)
            dbig_g[...] = jnp.zeros((1, dr), F32)
            dlam_g[...] = jnp.zeros((1, dr), F32)
            dwrg_g[...] = jnp.zeros((nblocks, LANE, LANE), F32)
            dwig_g[...] = jnp.zeros((nblocks, LANE, LANE), F32)
            for cp in cps:
                cp.wait()

        @pl.when(jj == 0)
        def _():
            dxpad[pl.ds(tb, 8), :] = jnp.zeros((8, dr), F32)
            carry[...] = jnp.zeros((8, dr), F32)

        keep = jnp.where(first_block, 0.0, 1.0)
        dx1v = dx1_ref[...]
        gate = u_ref[:, dr:]
        xpre = u_ref[:, :dr]
        hpad[pl.ds(0, 8), :] = hh_ref[...] * keep
        hpad[pl.ds(8, tb), :] = hs_ref[...]
        xb = xb_ref[...]
        xbb = xb.astype(BF16)
        r, i = _gates(xb, wrg, brg[...], wig, big[...], nblocks)
        sp = _softplus(-lam[...])
        log_a = -LRU_C * r * sp
        a, a2, nem = _decay(log_a)
        mult = jnp.sqrt(nem)
        sg = _sigmoid(gate)
        dy = _dot_nt(dx1v.astype(BF16), wout[...])
        hsv = hs_ref[...]
        dgate = dy * hsv * (sg * (1.0 + gate * (1.0 - sg)))
        a_s[...] = a
        d_s[...] = dy * (gate * sg)
        row = lax.broadcasted_iota(jnp.int32, (8, dr), 0)

        def step(k, c):
            r0 = pl.multiple_of((nt - 1 - k) * 8, 8)
            av = a_s[pl.ds(r0, 8), :]
            dv = d_s[pl.ds(r0, 8), :]
            qv = av * dv
            for s in (1, 2, 4):
                m = row < 8 - s
                a_sh = jnp.where(m, pltpu.roll(av, 8 - s, 0), 1.0)
                q_sh = jnp.where(m, pltpu.roll(qv, 8 - s, 0), 0.0)
                qv = qv + av * q_sh
                av = av * a_sh
            qv = qv + av * c
            g_s[pl.ds(r0, 8), :] = dv + jnp.where(row < 7, pltpu.roll(qv, 7, 0), c)
            return jnp.broadcast_to(qv[0:1, :], qv.shape)

        carry[...] = lax.fori_loop(0, nt, step, carry[...])
        g = g_s[...]
        ix = i * xb
        dlog_a = g * (hpad[pl.ds(7, tb), :] * a - ix * (a2 * lax.rsqrt(nem)))
        dix = g * mult
        dlam_g[...] += -jax.nn.sigmoid(-lam[...]) * jnp.sum(dlog_a * (-LRU_C * r), axis=0, keepdims=True)
        drg = dlog_a * (-LRU_C * sp) * r * (1.0 - r)
        dig = dix * xb * i * (1.0 - i)
        dbrg_g[...] += jnp.sum(drg, axis=0, keepdims=True)
        dbig_g[...] += jnp.sum(dig, axis=0, keepdims=True)
        drgb, digb = drg.astype(BF16), dig.astype(BF16)
        back = []
        for n in range(nblocks):
            cols = slice(n * LANE, (n + 1) * LANE)
            dwrg_g[n] += _dot_tn(xbb[:, cols], drgb[:, cols])
            dwig_g[n] += _dot_tn(xbb[:, cols], digb[:, cols])
            back.append(_dot_nt(drgb[:, cols], wrg[n]) + _dot_nt(digb[:, cols], wig[n]))
        dxb = dix * i + jnp.concatenate(back, axis=1)
        dcb_g[...] += jnp.sum(dxb, axis=0, keepdims=True)
        dxpad[pl.ds(0, tb), :] = dxb
        later = [dxb, dxpad[pl.ds(1, tb), :], dxpad[pl.ds(2, tb), :], dxpad[pl.ds(3, tb), :]]
        dxpad[pl.ds(tb, 8), :] = dxb[:8, :]
        dxpre = cw[3:4, :] * later[0] + cw[2:3, :] * later[1] + cw[1:2, :] * later[2] + cw[0:1, :] * later[3]
        for m in range(4):
            dcw_g[3 - m:4 - m, :] += jnp.sum(later[m] * xpre, axis=0, keepdims=True)
        du = jnp.concatenate([dxpre, dgate], axis=1).astype(BF16)
        du_ref[...] = du
        xv = x_ref[...]
        dxa, g1 = _rms_bwd(xv, _rinv(xv), na[...], _dot(du, wint[...]))
        dna_g[...] += g1
        gx_ref[...] = dx1v + dxa

        @pl.when((b == t_all // seq - 1) & (jj == nblk - 1))
        def _():
            scatter.finish()

    blk = lambda b, j: b * nblk + (nblk - 1 - j)
    tok = lambda c: pl.BlockSpec((tb, c), lambda b, j: (blk(b, j), 0))
    halo = pl.BlockSpec((8, dr), lambda b, j: (jnp.maximum(blk(b, j) * per8 - 1, 0), 0))
    consts = [w["norm_a"], w["conv_w"], w["w_rg"], w["b_rg"], w["w_ig"], w["b_ig"], w["lru_lambda"]]
    vec = lambda c: _full((1, c))
    blocks3 = (nblocks, LANE, LANE)
    return pl.pallas_call(
        body, name="fa_bwd", grid=(t_all // seq, nblk),
        in_specs=[tok(d), tok(d), tok(2 * dr), tok(dr), tok(dr), halo, ANY, ANY, ANY] + [_full(c.shape) for c in consts],
        out_specs=[tok(d), tok(2 * dr), vec(d), _full((4, dr)), vec(dr), vec(dr), vec(dr), vec(dr), _full(blocks3), _full(blocks3), ANY, ANY],
        out_shape=[_sds((t_all, d)), _sds((t_all, 2 * dr), BF16), _sds((1, d)), _sds((4, dr)), _sds((1, dr)), _sds((1, dr)), _sds((1, dr)),
                   _sds((1, dr)), _sds(blocks3), _sds(blocks3)] + _scatter_shapes(lay, "early", part16.shape[1]),
        scratch_shapes=[pltpu.VMEM((2 * dr, d), BF16), pltpu.VMEM((dr, d), BF16), pltpu.VMEM((tb + 8, dr), F32),
                        pltpu.VMEM((tb, dr), F32), pltpu.VMEM((tb, dr), F32), pltpu.VMEM((tb, dr), F32), pltpu.VMEM((tb + 8, dr), F32),
                        pltpu.VMEM((8, dr), F32), pltpu.SemaphoreType.DMA((2 * N_CHIPS,))] + SCATTER_SEMS,
        compiler_params=_params(2),
    )(dx1, x, u, xb, hs, hs, wg, part16, part32, *consts)


def _mm_into(gbuf, a, b, off, name, bt):
    t_all, m = a.shape
    n = b.shape[1]
    nsplit = 2 if m >= 1024 and (m // 2) % LANE == 0 else 1
    mh = m // nsplit
    nt = t_all // bt

    def body(a_ref, b_ref, g_ref, o_ref, acc, sems):
        del g_ref
        part, t = pl.program_id(0), pl.program_id(1)

        def out_copy(h):
            return pltpu.make_async_copy(acc.at[h], o_ref.at[pl.ds(off + h * mh, mh), :], sems.at[h])

        prod = _dot_tn(a_ref[...].astype(BF16), b_ref[...].astype(BF16))
        for h in range(nsplit):
            @pl.when((part == h) & (t == 0))
            def _():
                acc[h] = prod

            @pl.when((part == h) & (t > 0))
            def _():
                acc[h] += prod

            @pl.when((part == h) & (t == nt - 1))
            def _():
                out_copy(h).start()

        @pl.when((part == nsplit - 1) & (t == nt - 1))
        def _():
            for h in range(nsplit):
                out_copy(h).wait()

    return pl.pallas_call(
        body, name=name, grid=(nsplit, nt),
        in_specs=[pl.BlockSpec((bt, mh), lambda h, t: (t, h)), pl.BlockSpec((bt, n), lambda h, t: (t, 0)), ANY],
        out_specs=ANY, out_shape=_sds(gbuf.shape), input_output_aliases={2: 0},
        scratch_shapes=[pltpu.VMEM((nsplit, mh, n), F32), pltpu.SemaphoreType.DMA((nsplit,))],
        compiler_params=_params(2),
    )(a, b, gbuf)


def _mm_tn(a, b, name, bt):
    t_all, m = a.shape
    n = b.shape[1]

    def body(a_ref, b_ref, o_ref):
        @pl.when(pl.program_id(0) == 0)
        def _():
            o_ref[...] = jnp.zeros((m, n), F32)

        o_ref[...] += _dot_tn(a_ref[...].astype(BF16), b_ref[...].astype(BF16))

    return pl.pallas_call(
        body, name=name, grid=(t_all // bt,),
        in_specs=[pl.BlockSpec((bt, m), lambda t: (t, 0)), pl.BlockSpec((bt, n), lambda t: (t, 0))],
        out_specs=_full((m, n)), out_shape=_sds((m, n)),
        compiler_params=_params(1),
    )(a, b)


class _Gather8:
    def __init__(self, x_ref, out_ref, send_sems, recv_sems, local_sem):
        x, y, c = _place()
        self.c, self.me, self.sibling = c, (x, y, c), (x, y, 1 - c)
        self.chips = [(1 - x, y), (x, 1 - y), (1 - x, 1 - y)]
        self.x_ref, self.out_ref, self.send_sems, self.recv_sems, self.local_sem = x_ref, out_ref, send_sems, recv_sems, local_sem

    def _slot(self, px, py, pc):
        return self.out_ref.at[4 * px + 2 * py + pc]

    def _copy(self, k, blk, to, src=None):
        return pltpu.make_async_remote_copy(
            src_ref=self._slot(*blk) if src is None else src, dst_ref=self._slot(*blk), send_sem=self.send_sems.at[k],
            recv_sem=self.recv_sems.at[k], device_id=to, device_id_type=MESH)

    def _mine(self):
        return pltpu.make_async_copy(self.x_ref, self._slot(*self.me), self.local_sem)

    def _first(self):
        return [self._copy(0, self.me, self.sibling, src=self.x_ref)] + [
            self._copy(1 + j, self.me, (*chip, self.c), src=self.x_ref) for j, chip in enumerate(self.chips)]

    def _passed(self):
        return [self._copy(4 + j, (*chip, self.c), self.sibling) for j, chip in enumerate(self.chips)]

    def start(self):
        self._mine().start()
        for cp in self._first():
            cp.start()

    def forward(self):
        passed = self._passed()
        for j, chip in enumerate(self.chips):
            self._copy(1 + j, (*chip, self.c), self.me).wait_recv()
            passed[j].start()

    def finish(self):
        self._copy(0, self.sibling, self.me).wait_recv()
        for j, chip in enumerate(self.chips):
            self._copy(4 + j, (*chip, 1 - self.c), self.me).wait_recv()
        for cp in self._first() + self._passed():
            cp.wait_send()
        self._mine().wait()


class _Scatter:
    def __init__(self, p16_ref, p32_ref, got_ref, own_ref, send_sems, recv_sems, local_sem, lay, order):
        self.x, self.y, self.c = _place()
        self.chips = [(1 - self.x, self.y), (self.x, 1 - self.y), (1 - self.x, 1 - self.y)]
        self.refs = (p16_ref, p32_ref, got_ref, own_ref, send_sems, recv_sems, local_sem)
        self.lay, self.order = lay, order

    def _rows_of(self, ref, key, chip):
        start = pl.multiple_of(self.lay.g_off[key] + chip * self.lay.rows[key], ROW_ALIGN)
        return ref.at[pl.ds(start, self.lay.rows[key]), :]

    def _compact(self, ref, key):
        return ref.at[pl.ds(self.lay.c_off[key], self.lay.rows[key]), :]

    def start(self):
        p16_ref, p32_ref, got_ref, own_ref, send_sems, recv_sems, local_sem = self.refs
        for key in self.order:
            pltpu.make_async_copy(self._rows_of(p32_ref, key, 2 * self.x + self.y), self._compact(own_ref, key), local_sem).start()
        for k, (px, py) in enumerate(self.chips):
            for key in self.order:
                pltpu.make_async_remote_copy(
                    src_ref=self._rows_of(p16_ref, key, 2 * px + py), dst_ref=self._compact(got_ref.at[k], key), send_sem=send_sems.at[k],
                    recv_sem=recv_sems.at[k], device_id=(px, py, self.c), device_id_type=MESH).start()

    def finish(self):
        _, _, got_ref, own_ref, send_sems, recv_sems, local_sem = self.refs
        for k, (px, py) in enumerate(self.chips):
            pltpu.make_async_remote_copy(src_ref=got_ref.at[k], dst_ref=got_ref.at[k], send_sem=send_sems.at[k], recv_sem=recv_sems.at[k],
                                         device_id=(px, py, self.c), device_id_type=MESH).wait()
        pltpu.make_async_copy(own_ref, own_ref, local_sem).wait()


SCATTER_SEMS = [pltpu.SemaphoreType.DMA((3,)), pltpu.SemaphoreType.DMA((3,)), pltpu.SemaphoreType.DMA]
GATHER_SEMS = [pltpu.SemaphoreType.DMA((7,)), pltpu.SemaphoreType.DMA((7,)), pltpu.SemaphoreType.DMA]


def _all_gather8(blocks, name):
    nb = len(blocks)

    def body(*refs):
        x_refs, out_refs = refs[:nb], refs[nb:2 * nb]
        send_sems, recv_sems, local_sems = refs[2 * nb:]
        gathers = [_Gather8(x_refs[n], out_refs[n], send_sems.at[n], recv_sems.at[n], local_sems.at[n]) for n in range(nb)]
        for g in gathers:
            g.start()
        for g in gathers:
            g.forward()
        for g in gathers:
            g.finish()

    return pl.pallas_call(
        body, name=name, out_shape=[_sds((8,) + b.shape, b.dtype) for b in blocks], in_specs=[ANY] * nb, out_specs=[ANY] * nb,
        scratch_shapes=[pltpu.SemaphoreType.DMA((nb, 7)), pltpu.SemaphoreType.DMA((nb, 7)), pltpu.SemaphoreType.DMA((nb,))],
    )(*blocks)


def _swap_sibling(srcs, name, half_cols=False):
    n = len(srcs)
    halves = [s.shape[1] // 2 if half_cols else s.shape[1] for s in srcs]

    def body(*refs):
        src_refs, out_refs, send_sems, recv_sems = refs[:n], refs[n:2 * n], refs[2 * n], refs[2 * n + 1]
        x, y, c = _place()
        copies = []
        for k in range(n):
            part = src_refs[k].at[:, pl.ds(pl.multiple_of((1 - c) * halves[k], LANE), halves[k])] if half_cols else src_refs[k]
            copies.append(pltpu.make_async_remote_copy(src_ref=part, dst_ref=out_refs[k], send_sem=send_sems.at[k], recv_sem=recv_sems.at[k],
                                                       device_id=(x, y, 1 - c), device_id_type=MESH))
        for cp in copies:
            cp.start()
        for cp in copies:
            cp.wait()

    return pl.pallas_call(
        body, name=name, out_shape=[_sds((s.shape[0], h), s.dtype) for s, h in zip(srcs, halves)], in_specs=[ANY] * n, out_specs=[ANY] * n,
        scratch_shapes=[pltpu.SemaphoreType.DMA((n,)), pltpu.SemaphoreType.DMA((n,))],
    )(*srcs)


def _scatter_shapes(lay, group, half):
    return [_sds((3, lay.c_rows[group], half), BF16), _sds((lay.c_rows[group], half), F32)]


def _scatter_chips(part16, part32, lay, group, name):
    def body(p16_ref, p32_ref, got_ref, own_ref, send_sems, recv_sems, local_sem):
        sc = _Scatter(p16_ref, p32_ref, got_ref, own_ref, send_sems, recv_sems, local_sem, lay, G_GROUPS[group])
        sc.start()
        sc.finish()

    return pl.pallas_call(
        body, name=name, out_shape=_scatter_shapes(lay, group, part16.shape[1]), in_specs=[ANY, ANY], out_specs=[ANY, ANY],
        scratch_shapes=SCATTER_SEMS,
    )(part16, part32)


def _sum_sibling(gbuf, got, cidx, name):
    rows, d = gbuf.shape
    half = d // 2
    rb = _row_block(rows)

    def body(c_ref, g_ref, r_ref, o32_ref, o16_ref):
        del c_ref
        s = g_ref[...] + r_ref[...]
        o32_ref[...] = s
        o16_ref[...] = s.astype(BF16)

    plain = pl.BlockSpec((rb, half), lambda i, c: (i, 0))
    return pl.pallas_call(
        body, name=name,
        grid_spec=pltpu.PrefetchScalarGridSpec(num_scalar_prefetch=1, grid=(rows // rb,),
                                               in_specs=[pl.BlockSpec((rb, half), lambda i, c: (i, c[0])), plain], out_specs=[plain, plain]),
        out_shape=[_sds((rows, half)), _sds((rows, half), BF16)], compiler_params=_params(1),
    )(cidx, gbuf, got)


def _sum_chips(own, got, name):
    rows, half = own.shape
    rb = _row_block(rows)

    def body(a_ref, b_ref, o_ref):
        o_ref[...] = ((a_ref[...] + b_ref[0].astype(F32)) + b_ref[1].astype(F32)) + b_ref[2].astype(F32)

    spec = pl.BlockSpec((rb, half), lambda i: (i, 0))
    return pl.pallas_call(
        body, name=name, grid=(rows // rb,), in_specs=[spec, pl.BlockSpec((3, rb, half), lambda i: (0, i, 0))], out_specs=spec,
        out_shape=_sds((rows, half)), compiler_params=_params(1),
    )(own, got)


def _adamw(w, g, m, v):
    m = ADAM_B1 * m + (1.0 - ADAM_B1) * g
    v = ADAM_B2 * v + (1.0 - ADAM_B2) * (g * g)
    m_hat = m / (1.0 - ADAM_B1 ** ADAM_STEP)
    v_hat = v / (1.0 - ADAM_B2 ** ADAM_STEP)
    return -ADAM_LR * (m_hat / (jnp.sqrt(v_hat) + ADAM_EPS) + ADAM_WD * w), m, v


def _adamw_rows(name, w, g, m, v):
    _, rows, cols = w.shape
    rb = _row_block(rows, 256)

    def body(w_ref, g_ref, m_ref, v_ref, d_ref, mo_ref, vo_ref):
        d_ref[...], mo_ref[...], vo_ref[...] = _adamw(w_ref[...], g_ref[...], m_ref[...], v_ref[...])

    spec = pl.BlockSpec((1, rb, cols), lambda i: (0, i, 0))
    return pl.pallas_call(
        body, name=name, grid=(rows // rb,), in_specs=[spec] * 4, out_specs=[spec] * 3, out_shape=[_sds(w.shape)] * 3,
        compiler_params=_params(1),
    )(w, g, m, v)


def _adamw_group(ws, gs, ms, vs):
    n = len(ws)

    def body(*refs):
        for k in range(n):
            w_ref, g_ref, m_ref, v_ref = (refs[j * n + k] for j in range(4))
            outs = _adamw(w_ref[...], g_ref[...], m_ref[...], v_ref[...])
            for j in range(3):
                refs[(4 + j) * n + k][...] = outs[j]

    outs = pl.pallas_call(
        body, name="adamw_small", out_shape=[_sds(w.shape) for w in ws] * 3,
        compiler_params=pltpu.CompilerParams(vmem_limit_bytes=VMEM_LIMIT),
    )(*ws, *gs, *ms, *vs)
    return outs[:n], outs[n:2 * n], outs[2 * n:]


def _gather_weights(sh, lay):
    x, y, c = _place()
    d = lay.d
    uq = sh["w_uq"][0].astype(BF16)
    parts = {
        "in_b": sh["w_in_b"][0].T.astype(BF16), "in_a": sh["w_in_a"][0].T.astype(BF16), "out_a": sh["w_out_a"][0].astype(BF16),
        "out_b": sh["w_out_b"][0].astype(BF16), "uk": sh["w_uk"].astype(BF16).reshape(-1, d), "uv": sh["w_uv"].astype(BF16).reshape(-1, d),
        "uq_n": uq[:, :, :QK_NOPE].reshape(-1, d), "uq_r": jnp.pad(uq[:, :, QK_NOPE:], ((0, 0), (0, 0), (0, LANE - QK_ROPE))).reshape(-1, d),
        "dkv": jnp.pad(sh["w_dkv"].astype(BF16), ((0, 0), (0, LANE - QK_ROPE))).reshape(-1, d),
    }
    halves = {}
    for group, order in W_GROUPS.items():
        stack = jnp.concatenate([parts[k] for k in order], axis=0).reshape(2, lay.w_rows[group] // 2, d)
        halves[group] = lax.dynamic_index_in_dim(stack, c, 0, keepdims=False)
    small = jnp.concatenate([sh[k].reshape(-1) for k in SMALL])
    n_small = small.shape[0]
    width = _round_up(n_small, 2 * SUBLANE * LANE) // (2 * SUBLANE)
    small = jnp.pad(small, (0, 2 * SUBLANE * width - n_small)).reshape(2, SUBLANE, width)
    wg, sg = _all_gather8([halves["a"], lax.dynamic_index_in_dim(small, c, 0, keepdims=False)], "ag_weights")
    wg = wg.reshape(N_CHIPS, lay.w_rows["a"], d)
    sg = sg.reshape(N_CHIPS, 2 * SUBLANE * width)
    full, off = {}, 0
    for k in SMALL:
        n = sh[k].size
        piece = sg[:, off:off + n]
        off += n
        if k == "conv_w":
            full[k] = piece.reshape(N_CHIPS, 4, n // 4).transpose(1, 0, 2).reshape(4, n)
        else:
            full[k] = piece.reshape(1, N_CHIPS * n)
    return wg, halves["b"], full


def _chip_split(g, taps=False):
    if taps:
        n = g.shape[1] // N_CHIPS
        return g.reshape(4, N_CHIPS, n).transpose(1, 0, 2).reshape(N_CHIPS, 4 * n)
    return g.reshape(N_CHIPS, -1)


def kernel(x, norm_a, w_in_a, conv_w, conv_b, w_rg, b_rg, w_ig, b_ig, lru_lambda, w_out_a, norm_kv, w_dkv, kv_norm, w_uk, w_uv, norm_b, w_in_b, q_norm, w_uq, w_out_b, final_norm, loss_target, m_norm_a, m_w_in_a, m_conv_w, m_conv_b, m_w_rg, m_b_rg, m_w_ig, m_b_ig, m_lru_lambda, m_w_out_a, m_norm_kv, m_w_dkv, m_kv_norm, m_w_uk, m_w_uv, m_norm_b, m_w_in_b, m_q_norm, m_w_uq, m_w_out_b, m_final_norm, v_norm_a, v_w_in_a, v_conv_w, v_conv_b, v_w_rg, v_b_rg, v_w_ig, v_b_ig, v_lru_lambda, v_w_out_a, v_norm_kv, v_w_dkv, v_kv_norm, v_w_uk, v_w_uv, v_norm_b, v_w_in_b, v_q_norm, v_w_uq, v_w_out_b, v_final_norm):
    given = dict(locals())
    sh = {k: given[k] for k in WEIGHTS}
    xi, yi, ci = _place()
    nb, seq, d = x.shape
    t_all = nb * seq
    tb_a, tb_b, ta, bt = min(TOKENS_A, seq), min(TOKENS_B, seq), min(TOKENS_ATTN, seq), min(TOKENS_MM, t_all)
    dr = conv_b.shape[1] * N_CHIPS
    qr, kvr, nheads = q_norm.shape[1], kv_norm.shape[0], w_uk.shape[1]
    hv = nheads * LANE
    n_small = sum(sh[k].size for k in SMALL)
    n_repl = sum(sh[k].size for k in REPL)
    lay = _Layout(d, dr, qr, kvr, hv, n_small, n_repl)
    half = d // 2

    wga, wb_half, w = _gather_weights(sh, lay)
    w.update({"w_rg": w_rg[0].astype(BF16), "w_ig": w_ig[0].astype(BF16), "norm_kv": norm_kv[None, :],
              "kv_norm": kv_norm[None, :], "final_norm": final_norm[None, :], "norm_b": norm_b, "q_norm": q_norm})
    cos_t, sin_t = _rope_tables(seq)
    cidx = jnp.reshape(ci, (1,)).astype(jnp.int32)

    x0 = x.reshape(t_all, d)
    x1, u, hs, h, y, xb, wgb = _fa_fwd(x0, wga, wb_half, w, lay, seq, tb_a)
    wgb = wgb.reshape(N_CHIPS, lay.w_rows["b"], d)
    w["w_dkv_p"] = wgb[:, lay.w_off["dkv"]:lay.w_off["dkv"] + lay.rows["dkv"], :].reshape(d, kvr + LANE)
    qn, qrp, kn, kr, v, ub, ckr, hb, hk, cq, ckv = _fb_fwd(x1, wgb, w, lay, cos_t, sin_t, seq, tb_b)
    o, lse = _attn_fwd(qn, qrp, kn, kr, v, seq, ta)
    loss, g_final_norm, yb, dx2, do, dgate, delta = _head(o, ub, x1, loss_target.reshape(t_all, d), wgb, w, lay, tb_b)
    dqn, dqr, dkn, dkr, dv = _attn_bwd(qn, qrp, kn, kr, v, do, lse, delta, seq, ta)
    dx1, dqr_pre, dqn_pre, dub, dckr, g_q_norm, g_norm_b, g_kv_norm, g_norm_kv = _fb_bwd(
        dqn, dqr, dkn, dkr, dv, dgate, ub, ckr, x1, dx2, wgb, w, lay, cos_t, sin_t, seq, tb_b)
    loss = lax.psum(loss[0, 0], ("x", "y", "c"))

    gbuf = lax.empty((lay.g_rows["early"], d), F32)
    for key, a, b in (("in_b", dub, hb), ("out_a", y, dx1), ("out_b", yb, dx2), ("uk", ckv, dkn), ("uv", ckv, dv), ("uq_n", cq, dqn_pre),
                      ("uq_r", cq, dqr_pre)):
        gbuf = _mm_into(gbuf, a, b, lay.g_off[key], "dw_" + key, bt)
    g_dkv = _mm_tn(hk, dckr, "dw_dkv", bt)
    (got,) = _swap_sibling([gbuf], "rs_sibling_early", half_cols=True)
    part32, part16 = _sum_sibling(gbuf, got, cidx, "rs_sum_sibling_early")
    gx, du, g_norm_a, g_conv_w, g_conv_b, g_b_rg, g_b_ig, g_lam, g_w_rg, g_w_ig, others, own = _fa_bwd(
        dx1, x0, u, xb, hs, wga, part16, part32, w, lay, seq, tb_a)
    mine_early = _sum_chips(own, others, "rs_sum_chips_early")

    small = jnp.concatenate([_chip_split(g_norm_a), _chip_split(g_conv_w, taps=True), _chip_split(g_conv_b), _chip_split(g_b_rg),
                             _chip_split(g_b_ig), _chip_split(g_lam)], axis=1)
    small = jnp.pad(small, ((0, 0), (0, lay.small_rows * d - small.shape[1]))).reshape(N_CHIPS, lay.small_rows, d)
    repl_parts = {"w_rg": g_w_rg, "w_ig": g_w_ig, "norm_kv": g_norm_kv, "kv_norm": g_kv_norm, "norm_b": g_norm_b, "q_norm": g_q_norm,
                  "final_norm": g_final_norm}
    repl = jnp.concatenate([repl_parts[k].reshape(-1) for k in REPL])
    repl = jnp.pad(repl, (0, N_CHIPS * lay.repl_rows * d - n_repl)).reshape(N_CHIPS, lay.repl_rows, d)
    pad_rows = lay.rows["rest"] - lay.rows["dkv"] - lay.small_rows - lay.repl_rows
    rest = jnp.concatenate([g_dkv.reshape(N_CHIPS, lay.rows["dkv"], d), small, repl, jnp.zeros((N_CHIPS, pad_rows, d), F32)], axis=1)
    gbuf = lax.dynamic_update_slice(lax.empty((lay.g_rows["late"], d), F32), rest.reshape(N_CHIPS * lay.rows["rest"], d), (lay.g_off["rest"], 0))
    gbuf = _mm_into(gbuf, du, h, lay.g_off["in_a"], "dw_in_a", bt)
    (got,) = _swap_sibling([gbuf], "rs_sibling_late", half_cols=True)
    part32, part16 = _sum_sibling(gbuf, got, cidx, "rs_sum_sibling_late")
    others, own = _scatter_chips(part16, part32, lay, "late", "rs_chips_late")
    mine_late = _sum_chips(own, others, "rs_sum_chips_late")

    theirs_early, theirs_late = _swap_sibling([mine_early, mine_late], "rs_return")
    red = {}
    for group, mine, theirs in (("early", mine_early, theirs_early), ("late", mine_late, theirs_late)):
        red[group] = jnp.concatenate([jnp.where(ci == 0, mine, theirs), jnp.where(ci == 0, theirs, mine)], axis=1)
    r0 = lay.c_off["rest"] + lay.rows["dkv"] + lay.small_rows
    (rep_all,) = _all_gather8([mine_late[r0:r0 + lay.repl_rows]], "ag_rep")
    rep_flat = rep_all.reshape(N_CHIPS, 2, lay.repl_rows, half).transpose(0, 2, 1, 3).reshape(-1)

    def rows(key):
        group = "late" if key in G_GROUPS["late"] else "early"
        return red[group][lay.c_off[key]:lay.c_off[key] + lay.rows[key]]

    grads = {"w_in_b": rows("in_b").T[None], "w_in_a": rows("in_a").T[None], "w_out_a": rows("out_a")[None], "w_out_b": rows("out_b")[None],
             "w_uk": rows("uk").reshape(w_uk.shape), "w_uv": rows("uv").reshape(w_uv.shape)}
    uq_n = rows("uq_n").reshape(qr // N_CHIPS, nheads, LANE)
    uq_r = rows("uq_r").reshape(qr // N_CHIPS, nheads, LANE)[:, :, :QK_ROPE]
    grads["w_uq"] = jnp.concatenate([uq_n, uq_r], axis=2)[None]
    rest_red = rows("rest")
    grads["w_dkv"] = rest_red[:lay.rows["dkv"]].reshape(d // N_CHIPS, kvr + LANE)[:, :kvr + QK_ROPE]
    small_red = rest_red[lay.rows["dkv"]:lay.rows["dkv"] + lay.small_rows].reshape(-1)
    off = 0
    for k in SMALL:
        n = sh[k].size
        grads[k] = small_red[off:off + n].reshape(sh[k].shape)
        off += n
    off = 0
    for k in REPL:
        n = sh[k].size
        grads[k] = rep_flat[off:off + n].reshape(sh[k].shape)
        off += n

    new = {}
    for k in ("w_in_a", "w_in_b", "w_out_a", "w_out_b"):
        new[k] = _adamw_rows("adamw_" + k, sh[k], grads[k], given["m_" + k], given["v_" + k])
    rest_names = [k for k in WEIGHTS if k not in new]
    as2d = lambda a: a[None, :] if a.ndim == 1 else a
    ds, ms, vs = _adamw_group([as2d(sh[k]) for k in rest_names], [as2d(grads[k]) for k in rest_names],
                              [as2d(given["m_" + k]) for k in rest_names], [as2d(given["v_" + k]) for k in rest_names])
    for n, k in enumerate(rest_names):
        new[k] = tuple(a.reshape(sh[k].shape) for a in (ds[n], ms[n], vs[n]))
    return (loss, gx.reshape(nb, seq, d), *[grads[k] for k in WEIGHTS], *[new[k][0] for k in WEIGHTS], *[new[k][1] for k in WEIGHTS],
            *[new[k][2] for k in WEIGHTS])
```

```python
import jax
import jax.numpy as jnp
from jax import lax
from jax.experimental import pallas as pl
from jax.experimental.pallas import tpu as pltpu

F32, BF16 = jnp.float32, jnp.bfloat16
EPS = 1e-6
LRU_C = 8.0
ROPE_THETA = 10000.0
QK_NOPE, QK_ROPE = 128, 64
ATTN_SCALE = (QK_NOPE + QK_ROPE) ** -0.5
LN2 = 0.6931471805599453
Q_SCALE = ATTN_SCALE / LN2
ATTN_ROWS = 64
LANE = 128
SUBLANE = 8
ROW_ALIGN = 32
VMEM_LIMIT = 60000 * 1024
ADAM_LR, ADAM_B1, ADAM_B2, ADAM_EPS, ADAM_WD, ADAM_STEP = 0.001, 0.9, 0.999, 1e-08, 0.01, 10
MESH = pl.DeviceIdType.MESH
ANY = pl.BlockSpec(memory_space=pl.ANY)
N_CHIPS = 4
TOKENS_A, TOKENS_B, TOKENS_ATTN, TOKENS_MM = 256, 256, 512, 2048

SMALL = ("norm_a", "conv_w", "conv_b", "b_rg", "b_ig", "lru_lambda")
REPL = ("w_rg", "w_ig", "norm_kv", "kv_norm", "norm_b", "q_norm", "final_norm")
WEIGHTS = ("norm_a", "w_in_a", "conv_w", "conv_b", "w_rg", "b_rg", "w_ig", "b_ig", "lru_lambda", "w_out_a", "norm_kv",
           "w_dkv", "kv_norm", "w_uk", "w_uv", "norm_b", "w_in_b", "q_norm", "w_uq", "w_out_b", "final_norm")
W_GROUPS = {"a": ("in_a", "out_a"), "b": ("in_b", "out_b", "uk", "uv", "uq_n", "uq_r", "dkv")}
G_GROUPS = {"early": ("in_b", "out_a", "out_b", "uk", "uv", "uq_n", "uq_r"), "late": ("in_a", "rest")}


def _sds(shape, dtype=F32):
    return jax.ShapeDtypeStruct(tuple(shape), dtype)


def _params(n_grid):
    return pltpu.CompilerParams(dimension_semantics=("arbitrary",) * n_grid, vmem_limit_bytes=VMEM_LIMIT)


def _full(shape):
    nd = len(shape)
    return pl.BlockSpec(tuple(shape), lambda *g: (0,) * nd)


def _round_up(n, k):
    return -(-n // k) * k


def _row_block(rows, cap=512):
    best = SUBLANE
    for r in range(SUBLANE, min(rows, cap) + 1, SUBLANE):
        if rows % r == 0:
            best = r
    return best


def _place():
    return lax.axis_index("x"), lax.axis_index("y"), lax.axis_index("c")


class _Layout:
    def __init__(self, d, dr, qr, kvr, hv, n_small, n_repl):
        assert hv == d, "the packed rows are D_MODEL wide, which must equal heads * 128"
        self.d, self.dr, self.qr, self.kvr, self.hv = d, dr, qr, kvr, hv
        per_chip = {"in_b": (qr + hv) // N_CHIPS, "in_a": 2 * dr // N_CHIPS, "out_a": dr // N_CHIPS, "out_b": hv // N_CHIPS,
                    "uk": kvr // N_CHIPS, "uv": kvr // N_CHIPS, "uq_n": qr // N_CHIPS, "uq_r": qr // N_CHIPS,
                    "dkv": (d // N_CHIPS) * (kvr + LANE) // d}
        assert all(r % ROW_ALIGN == 0 for r in per_chip.values()), per_chip
        self.small_rows = _round_up(-(-n_small // d), SUBLANE)
        self.repl_rows = _round_up(-(-n_repl // (N_CHIPS * d)), SUBLANE)
        per_chip["rest"] = _round_up(per_chip["dkv"] + self.small_rows + self.repl_rows, ROW_ALIGN)
        self.rows = per_chip
        self.w_off, self.w_rows = {}, {}
        for group, order in W_GROUPS.items():
            off = 0
            for k in order:
                self.w_off[k] = off
                off += per_chip[k]
            assert off % ROW_ALIGN == 0, (group, off)
            self.w_rows[group] = off
        self.g_off, self.c_off, self.c_rows, self.g_rows = {}, {}, {}, {}
        for group, order in G_GROUPS.items():
            off = 0
            for k in order:
                self.c_off[k] = off
                self.g_off[k] = N_CHIPS * off
                off += per_chip[k]
            self.c_rows[group] = off
            self.g_rows[group] = N_CHIPS * off


def _dot(a, b):
    return jnp.dot(a, b, preferred_element_type=F32)


def _dot_nt(a, b):
    return lax.dot_general(a, b, (((1,), (1,)), ((), ())), preferred_element_type=F32)


def _dot_tn(a, b):
    return lax.dot_general(a, b, (((0,), (0,)), ((), ())), preferred_element_type=F32)


def _rinv(x):
    return lax.rsqrt(jnp.mean(x * x, axis=-1, keepdims=True) + EPS)


def _rms_bwd(x, rinv, g, dy):
    z = dy * g
    dx = rinv * z - x * (rinv * rinv * rinv) * jnp.mean(z * x, axis=-1, keepdims=True)
    dg = jnp.sum(dy * (x * rinv), axis=0, keepdims=True)
    return dx, dg


def _softplus(z):
    return jnp.maximum(z, 0.0) + jnp.log1p(jnp.exp(-jnp.abs(z)))


def _sigmoid(x):
    return 0.5 * jnp.tanh(0.5 * x) + 0.5


def _decay(log_a):
    a = jnp.exp(log_a)
    a2 = a * a
    return a, a2, -jnp.tanh(log_a) * (a2 + 1.0)


def _swap_halves(x):
    w = x.shape[1]
    lane = lax.broadcasted_iota(jnp.int32, x.shape, 1)
    return jnp.where(lane % QK_ROPE < QK_ROPE // 2, pltpu.roll(x, w - QK_ROPE // 2, 1), pltpu.roll(x, QK_ROPE // 2, 1))


def _rope_tables(seq):
    pos = jnp.arange(seq, dtype=F32)
    inv = ROPE_THETA ** (-jnp.arange(0, QK_ROPE, 2, dtype=F32) / QK_ROPE)
    ang = pos[:, None] * inv[None, :]
    cos, sin = jnp.cos(ang), jnp.sin(ang)
    zero = jnp.zeros((seq, LANE - QK_ROPE), F32)
    return jnp.concatenate([cos, cos, zero], 1), jnp.concatenate([-sin, sin, zero], 1)


def _fetch(wg_ref, lay, key, dst, sems, k0):
    rows = lay.rows[key]
    return [pltpu.make_async_copy(wg_ref.at[p, pl.ds(lay.w_off[key], rows), :], dst.at[pl.ds(p * rows, rows), :], sems.at[k0 + p])
            for p in range(N_CHIPS)]


def _gates(xb, wrg_ref, brg, wig_ref, big, nblocks):
    xbb = xb.astype(BF16)
    rg = [_dot(xbb[:, n * LANE:(n + 1) * LANE], wrg_ref[n]) for n in range(nblocks)]
    ig = [_dot(xbb[:, n * LANE:(n + 1) * LANE], wig_ref[n]) for n in range(nblocks)]
    r = _sigmoid(jnp.concatenate(rg, axis=1) + brg)
    i = _sigmoid(jnp.concatenate(ig, axis=1) + big)
    return r, i


def _conv(xpad, cw_ref, cb, tb):
    return (cb + cw_ref[3:4, :] * xpad[pl.ds(8, tb), :] + cw_ref[2:3, :] * xpad[pl.ds(7, tb), :]
            + cw_ref[1:2, :] * xpad[pl.ds(6, tb), :] + cw_ref[0:1, :] * xpad[pl.ds(5, tb), :])


def _fa_fwd(x, wg, wb_half, w, lay, seq, tb):
    t_all, d = x.shape
    dr = lay.dr
    nblocks = w["w_rg"].shape[0]
    nblk = seq // tb
    nt = tb // SUBLANE
    nsteps = (t_all // seq) * nblk

    def body(x_ref, wg_ref, wbh_ref, na, cw, cb, wrg, brg, wig, big, lam, x1_ref, u_ref, hs_ref, h_ref, y_ref, xb_ref, wb_ref,
             wint, wout, xpad, a_s, b_s, carry, sems, send_sems, recv_sems, local_sem):
        step_no = pl.program_id(0) * nblk + pl.program_id(1)
        gather = _Gather8(wbh_ref, wb_ref, send_sems, recv_sems, local_sem)

        @pl.when(step_no == 0)
        def _():
            gather.start()
            cps = _fetch(wg_ref, lay, "in_a", wint, sems, 0) + _fetch(wg_ref, lay, "out_a", wout, sems, N_CHIPS)
            for cp in cps:
                cp.start()
            for cp in cps:
                cp.wait()

        @pl.when(step_no == nsteps // 2)
        def _():
            gather.forward()

        @pl.when(pl.program_id(1) == 0)
        def _():
            xpad[pl.ds(0, 8), :] = jnp.zeros((8, dr), F32)
            carry[...] = jnp.zeros((8, dr), F32)

        xv = x_ref[...]
        h = (xv * _rinv(xv) * na[...]).astype(BF16)
        h_ref[...] = h
        u = _dot_nt(h, wint[...])
        u_ref[...] = u
        xpre, gate = u[:, :dr], u[:, dr:]
        xpad[pl.ds(8, tb), :] = xpre
        xb = _conv(xpad, cw, cb[...], tb)
        xb_ref[...] = xb
        xpad[pl.ds(0, 8), :] = xpre[tb - 8:, :]
        r, i = _gates(xb, wrg, brg[...], wig, big[...], nblocks)
        log_a = -LRU_C * r * _softplus(-lam[...])
        a, _, nem = _decay(log_a)
        a_s[...] = a
        b_s[...] = jnp.sqrt(nem) * (i * xb)
        row = lax.broadcasted_iota(jnp.int32, (8, dr), 0)

        def step(t, c):
            r0 = pl.multiple_of(t * 8, 8)
            a = a_s[pl.ds(r0, 8), :]
            b = b_s[pl.ds(r0, 8), :]
            for s in (1, 2, 4):
                m = row >= s
                a_sh = jnp.where(m, pltpu.roll(a, s, 0), 1.0)
                b_sh = jnp.where(m, pltpu.roll(b, s, 0), 0.0)
                b = a * b_sh + b
                a = a * a_sh
            hh = b + a * c
            hs_ref[pl.ds(r0, 8), :] = hh
            return jnp.broadcast_to(hh[7:8, :], hh.shape)

        carry[...] = lax.fori_loop(0, nt, step, carry[...])
        y = (hs_ref[...] * (gate * _sigmoid(gate))).astype(BF16)
        y_ref[...] = y
        x1_ref[...] = xv + _dot(y, wout[...])

        @pl.when(step_no == nsteps - 1)
        def _():
            gather.finish()

    tok = lambda c: pl.BlockSpec((tb, c), lambda b, j: (b * nblk + j, 0))
    consts = [w["norm_a"], w["conv_w"], w["conv_b"], w["w_rg"], w["b_rg"], w["w_ig"], w["b_ig"], w["lru_lambda"]]
    return pl.pallas_call(
        body, name="fa_fwd", grid=(t_all // seq, nblk),
        in_specs=[tok(d), ANY, ANY] + [_full(c.shape) for c in consts],
        out_specs=[tok(d), tok(2 * dr), tok(dr), tok(d), tok(dr), tok(dr), ANY],
        out_shape=[_sds((t_all, d)), _sds((t_all, 2 * dr)), _sds((t_all, dr)), _sds((t_all, d), BF16), _sds((t_all, dr), BF16),
                   _sds((t_all, dr)), _sds((8,) + wb_half.shape, BF16)],
        scratch_shapes=[pltpu.VMEM((2 * dr, d), BF16), pltpu.VMEM((dr, d), BF16), pltpu.VMEM((tb + 8, dr), F32), pltpu.VMEM((tb, dr), F32),
                        pltpu.VMEM((tb, dr), F32), pltpu.VMEM((8, dr), F32), pltpu.SemaphoreType.DMA((2 * N_CHIPS,))] + GATHER_SEMS,
        compiler_params=_params(2),
    )(x, wg, wb_half, *consts)


def _fb_fwd(x1, wg, w, lay, cos_t, sin_t, seq, tb):
    t_all, d = x1.shape
    kvr, qr, hv = lay.kvr, lay.qr, lay.hv
    nheads = hv // LANE
    npos = seq // tb

    def body(x_ref, wg_ref, nkv, nb, wdkv, kvn, qn, cos_ref, sin_ref,
             qn_o, qr_o, kn_o, kr_o, v_o, ub_o, ckr_o, hb_o, hk_o, cq_o, ckv_o, winb, wuk, wuv, wuqn, wuqr, sems):
        @pl.when(pl.program_id(0) == 0)
        def _():
            cps = []
            for n, (key, dst) in enumerate((("in_b", winb), ("uk", wuk), ("uv", wuv), ("uq_n", wuqn), ("uq_r", wuqr))):
                cps += _fetch(wg_ref, lay, key, dst, sems, n * N_CHIPS)
            for cp in cps:
                cp.start()
            for cp in cps:
                cp.wait()

        xv = x_ref[...]
        xh = xv * _rinv(xv)
        hk = (xh * nkv[...]).astype(BF16)
        hb = (xh * nb[...]).astype(BF16)
        hk_o[...] = hk
        hb_o[...] = hb
        cos, sin = cos_ref[...], sin_ref[...]
        ckr = _dot(hk, wdkv[...])
        ckr_o[...] = ckr
        ckv_pre = ckr[:, :kvr]
        ckv = (ckv_pre * _rinv(ckv_pre) * kvn[...]).astype(BF16)
        ckv_o[...] = ckv
        kr = ckr[:, kvr:]
        kr_o[...] = (kr * cos + _swap_halves(kr) * sin).astype(BF16)
        kn_o[...] = _dot(ckv, wuk[...]).astype(BF16)
        v_o[...] = _dot(ckv, wuv[...]).astype(BF16)
        ub = _dot_nt(hb, winb[...])
        ub_o[...] = ub
        cq_pre = ub[:, :qr]
        cq = (cq_pre * _rinv(cq_pre) * qn[...]).astype(BF16)
        cq_o[...] = cq
        qn_o[...] = (_dot(cq, wuqn[...]) * Q_SCALE).astype(BF16)
        qrope = _dot(cq, wuqr[...]) * Q_SCALE
        qr_o[...] = (qrope * jnp.tile(cos, (1, nheads)) + _swap_halves(qrope) * jnp.tile(sin, (1, nheads))).astype(BF16)

    tok = lambda c: pl.BlockSpec((tb, c), lambda i: (i, 0))
    pos = pl.BlockSpec((tb, LANE), lambda i: (i % npos, 0))
    consts = [w["norm_kv"], w["norm_b"], w["w_dkv_p"], w["kv_norm"], w["q_norm"]]
    outs = [(hv, BF16), (hv, BF16), (hv, BF16), (LANE, BF16), (hv, BF16), (qr + hv, F32), (kvr + LANE, F32), (d, BF16), (d, BF16), (qr, BF16), (kvr, BF16)]
    return pl.pallas_call(
        body, name="fb_fwd", grid=(t_all // tb,),
        in_specs=[tok(d), ANY] + [_full(c.shape) for c in consts] + [pos, pos],
        out_specs=[tok(c) for c, _ in outs],
        out_shape=[_sds((t_all, c), dt) for c, dt in outs],
        scratch_shapes=[pltpu.VMEM((qr + hv, d), BF16), pltpu.VMEM((kvr, d), BF16), pltpu.VMEM((kvr, d), BF16), pltpu.VMEM((qr, d), BF16),
                        pltpu.VMEM((qr, d), BF16), pltpu.SemaphoreType.DMA((5 * N_CHIPS,))],
        compiler_params=_params(1),
    )(x1, wg, *consts, cos_t, sin_t)


def _causal_mask(row0, col0, nrows, ncols):
    rows = row0 + lax.broadcasted_iota(jnp.int32, (nrows, ncols), 0)
    cols = col0 + lax.broadcasted_iota(jnp.int32, (nrows, ncols), 1)
    return cols <= rows


def _attn_fwd(qn, qr, kn, kr, v, seq, ta):
    t_all, hv = qn.shape
    nheads, nb, na = hv // LANE, t_all // seq, seq // ta

    reps = ta // LANE

    def body(qn_ref, qr_ref, kn_ref, kr_ref, v_ref, o_ref, lse_ref, m_s, acc_s):
        i = pl.program_id(2)
        m_s[...] = jnp.full((ta, LANE), -1e30, F32)
        acc_s[...] = jnp.zeros((ta, 2 * LANE), F32)
        q = jnp.concatenate([qn_ref[...], qr_ref[...]], axis=1)
        ones = jnp.ones((ta, LANE), BF16)

        def tile(j, diagonal):
            cols = pl.ds(pl.multiple_of(j * ta, ta), ta)
            k = jnp.concatenate([kn_ref[cols, :], kr_ref[cols, :]], axis=1)
            s = _dot_nt(q, k)
            if diagonal:
                s = jnp.where(_causal_mask(0, 0, ta, ta), s, -1e30)
            m_prev = m_s[...]
            m_new = jnp.maximum(m_prev, jnp.max(s, axis=1, keepdims=True))
            p = jnp.exp2(s - jnp.tile(m_new, (1, reps)))
            alpha = jnp.exp2(m_prev - m_new)
            pv = _dot(p.astype(BF16), jnp.concatenate([v_ref[cols, :], ones], axis=1))
            acc_s[...] = jnp.tile(alpha, (1, 2)) * acc_s[...] + pv
            m_s[...] = m_new

        def off_diagonal(j, carry):
            tile(j, False)
            return carry

        lax.fori_loop(0, i, off_diagonal, 0)
        tile(i, True)
        l = acc_s[:, LANE:]
        o_ref[...] = (acc_s[:, :LANE] / l).astype(BF16)
        lse_ref[...] = m_s[...] + jnp.log2(l)

    qspec = pl.BlockSpec((ta, LANE), lambda b, h, i: (b * na + i, h))
    kspec = pl.BlockSpec((seq, LANE), lambda b, h, i: (b, h))
    krspec = pl.BlockSpec((seq, LANE), lambda b, h, i: (b, 0))
    return pl.pallas_call(
        body, name="attn_fwd", grid=(nb, nheads, na),
        in_specs=[qspec, qspec, kspec, krspec, kspec],
        out_specs=[qspec, qspec],
        out_shape=[_sds((t_all, hv), BF16), _sds((t_all, hv))],
        scratch_shapes=[pltpu.VMEM((ta, LANE), F32), pltpu.VMEM((ta, 2 * LANE), F32)],
        compiler_params=_params(3),
    )(qn, qr, kn, kr, v)


def _attn_bwd(qn, qr, kn, kr, v, do, lse, delta, seq, ta):
    t_all, hv = qn.shape
    nheads, nb, na = hv // LANE, t_all // seq, seq // ta

    reps = ta // LANE
    nchunks = ta // ATTN_ROWS

    def body(qn_ref, qr_ref, kn_ref, kr_ref, v_ref, do_ref, lse_ref, dl_ref, dqn_ref, dqr_ref, dkn_ref, dkr_ref, dv_ref,
             s_s, dp_s, p_s, ds_s, dk_s, dv_s):
        j = pl.program_id(2)

        @pl.when(j == 0)
        def _():
            dqn_ref[...] = jnp.zeros((seq, LANE), F32)
            dqr_ref[...] = jnp.zeros((seq, LANE), F32)

        dk_s[...] = jnp.zeros((ta, 2 * LANE), F32)
        dv_s[...] = jnp.zeros((ta, LANE), F32)
        k = jnp.concatenate([kn_ref[...], kr_ref[...]], axis=1)
        vv = v_ref[...]

        def tile(i, diagonal):
            rows_i = pl.ds(pl.multiple_of(i * ta, ta), ta)
            q = jnp.concatenate([qn_ref[rows_i, :], qr_ref[rows_i, :]], axis=1)
            do_b = do_ref[rows_i, :]
            s_s[...] = _dot_nt(q, k)
            dp_s[...] = _dot_nt(do_b, vv)

            def chunk(c, carry):
                rows = pl.ds(pl.multiple_of(c * ATTN_ROWS, ATTN_ROWS), ATTN_ROWS)
                seq_rows = pl.ds(pl.multiple_of(i * ta + c * ATTN_ROWS, ATTN_ROWS), ATTN_ROWS)
                s = s_s[rows, :]
                if diagonal:
                    s = jnp.where(_causal_mask(c * ATTN_ROWS, 0, ATTN_ROWS, ta), s, -1e30)
                p = jnp.exp2(s - jnp.tile(lse_ref[seq_rows, :], (1, reps)))
                p_s[rows, :] = p.astype(BF16)
                ds_s[rows, :] = (p * (dp_s[rows, :] - jnp.tile(dl_ref[seq_rows, :], (1, reps)))).astype(BF16)
                return carry

            lax.fori_loop(0, nchunks, chunk, 0, unroll=True)
            dv_s[...] += _dot_tn(p_s[...], do_b)
            ds = ds_s[...]
            dk_s[...] += _dot_tn(ds, q)
            dq = _dot(ds, k)
            dqn_ref[rows_i, :] += dq[:, :LANE]
            dqr_ref[rows_i, :] += dq[:, LANE:]

        def off_diagonal(i, carry):
            tile(i, False)
            return carry

        tile(j, True)
        lax.fori_loop(j + 1, na, off_diagonal, 0)
        dkn_ref[...] = (dk_s[:, :LANE] * LN2).astype(BF16)
        dkr_ref[...] = dk_s[:, LANE:] * LN2
        dv_ref[...] = dv_s[...].astype(BF16)

    qspec = pl.BlockSpec((seq, LANE), lambda b, h, j: (b, h))
    kspec = pl.BlockSpec((ta, LANE), lambda b, h, j: (b * na + j, h))
    krspec = pl.BlockSpec((ta, LANE), lambda b, h, j: (b * na + j, 0))
    return pl.pallas_call(
        body, name="attn_bwd", grid=(nb, nheads, na),
        in_specs=[qspec, qspec, kspec, krspec, kspec, qspec, qspec, qspec],
        out_specs=[qspec, qspec, kspec, kspec, kspec],
        out_shape=[_sds((t_all, hv)), _sds((t_all, hv)), _sds((t_all, hv), BF16), _sds((t_all, hv)), _sds((t_all, hv), BF16)],
        scratch_shapes=[pltpu.VMEM((ta, ta), F32), pltpu.VMEM((ta, ta), F32), pltpu.VMEM((ta, ta), BF16), pltpu.VMEM((ta, ta), BF16),
                        pltpu.VMEM((ta, 2 * LANE), F32), pltpu.VMEM((ta, LANE), F32)],
        compiler_params=_params(3),
    )(qn, qr, kn, kr, v, do, lse, delta)


def _head(o, ub, x1, target, wg, w, lay, tb):
    t_all, d = x1.shape
    hv, qr = lay.hv, lay.qr
    nheads = hv // LANE

    def body(o_ref, ub_ref, x1_ref, tg_ref, wg_ref, gf, loss_ref, dgf_ref, yb_ref, dx2_ref, do_ref, dg_ref, dl_ref, wob, sems):
        @pl.when(pl.program_id(0) == 0)
        def _():
            cps = _fetch(wg_ref, lay, "out_b", wob, sems, 0)
            for cp in cps:
                cp.start()
            loss_ref[...] = jnp.zeros((1, LANE), F32)
            dgf_ref[...] = jnp.zeros((1, d), F32)
            for cp in cps:
                cp.wait()

        ov = o_ref[...].astype(F32)
        g = ub_ref[:, qr:]
        sg = _sigmoid(g)
        silu = g * sg
        yb = (ov * silu).astype(BF16)
        yb_ref[...] = yb
        x2 = x1_ref[...] + _dot(yb, wob[...])
        rinv = _rinv(x2)
        err = x2 * rinv * gf[...] - tg_ref[...]
        loss_ref[...] += (0.5 / d) * jnp.sum(jnp.sum(err * err, axis=1, keepdims=True), axis=0, keepdims=True)
        dx2, dgf = _rms_bwd(x2, rinv, gf[...], err * (1.0 / d))
        dgf_ref[...] += dgf
        dx2_ref[...] = dx2
        dyb = _dot_nt(dx2.astype(BF16), wob[...])
        dov = dyb * silu
        do_ref[...] = dov.astype(BF16)
        dg_ref[...] = (dyb * ov * (sg * (1.0 + g * (1.0 - sg)))).astype(BF16)
        prod = dov * ov
        dl_ref[...] = jnp.concatenate(
            [jnp.broadcast_to(jnp.sum(prod[:, n * LANE:(n + 1) * LANE], axis=1, keepdims=True), (tb, LANE)) for n in range(nheads)], axis=1)

    tok = lambda c: pl.BlockSpec((tb, c), lambda i: (i, 0))
    return pl.pallas_call(
        body, name="head", grid=(t_all // tb,),
        in_specs=[tok(hv), tok(qr + hv), tok(d), tok(d), ANY, _full((1, d))],
        out_specs=[_full((1, LANE)), _full((1, d)), tok(hv), tok(d), tok(hv), tok(hv), tok(hv)],
        out_shape=[_sds((1, LANE)), _sds((1, d)), _sds((t_all, hv), BF16), _sds((t_all, d)), _sds((t_all, hv), BF16), _sds((t_all, hv), BF16),
                   _sds((t_all, hv))],
        scratch_shapes=[pltpu.VMEM((hv, d), BF16), pltpu.SemaphoreType.DMA((N_CHIPS,))],
        compiler_params=_params(1),
    )(o, ub, x1, target, wg, w["final_norm"])


def _fb_bwd(dqn, dqr, dkn, dkr, dv, dgate, ub, ckr, x1, dx2, wg, w, lay, cos_t, sin_t, seq, tb):
    t_all, d = x1.shape
    hv, qr, kvr = lay.hv, lay.qr, lay.kvr
    nheads = hv // LANE
    npos = seq // tb

    def body(dqn_ref, dqr_ref, dkn_ref, dkr_ref, dv_ref, dg_ref, ub_ref, ckr_ref, x1_ref, dx2_ref, wg_ref,
             qn, nb, kvn, wdkv, nkv, cos_ref, sin_ref,
             dx1_ref, dqrp_ref, dqnp_ref, dub_ref, dckr_ref, dqn_g, dnb_g, dkvn_g, dnkv_g, winb, wuk, wuv, wuqn, wuqr, sems):
        @pl.when(pl.program_id(0) == 0)
        def _():
            cps = []
            for n, (key, dst) in enumerate((("in_b", winb), ("uk", wuk), ("uv", wuv), ("uq_n", wuqn), ("uq_r", wuqr))):
                cps += _fetch(wg_ref, lay, key, dst, sems, n * N_CHIPS)
            for cp in cps:
                cp.start()
            dqn_g[...] = jnp.zeros((1, qr), F32)
            dnb_g[...] = jnp.zeros((1, d), F32)
            dkvn_g[...] = jnp.zeros((1, kvr), F32)
            dnkv_g[...] = jnp.zeros((1, d), F32)
            for cp in cps:
                cp.wait()

        cos, sin = cos_ref[...], sin_ref[...]
        xv = x1_ref[...]
        rinv1 = _rinv(xv)
        dqr_v = dqr_ref[...] * ATTN_SCALE
        dqr_pre = (dqr_v * jnp.tile(cos, (1, nheads)) + _swap_halves(dqr_v * jnp.tile(sin, (1, nheads)))).astype(BF16)
        dqrp_ref[...] = dqr_pre
        dqn_pre = (dqn_ref[...] * ATTN_SCALE).astype(BF16)
        dqnp_ref[...] = dqn_pre
        dcq = _dot_nt(dqn_pre, wuqn[...]) + _dot_nt(dqr_pre, wuqr[...])
        cq_pre = ub_ref[:, :qr]
        dcq_pre, g1 = _rms_bwd(cq_pre, _rinv(cq_pre), qn[...], dcq)
        dqn_g[...] += g1
        dub = jnp.concatenate([dcq_pre.astype(BF16), dg_ref[...]], axis=1)
        dub_ref[...] = dub
        dx1_b, g2 = _rms_bwd(xv, rinv1, nb[...], _dot(dub, winb[...]))
        dnb_g[...] += g2
        dkr_all = dkr_ref[...]
        dkr_sum = dkr_all[:, :LANE]
        for n in range(1, nheads):
            dkr_sum = dkr_sum + dkr_all[:, n * LANE:(n + 1) * LANE]
        dckr_rope = dkr_sum * cos + _swap_halves(dkr_sum * sin)
        dckv = _dot_nt(dkn_ref[...].astype(BF16), wuk[...]) + _dot_nt(dv_ref[...].astype(BF16), wuv[...])
        ckv_pre = ckr_ref[:, :kvr]
        dckv_pre, g3 = _rms_bwd(ckv_pre, _rinv(ckv_pre), kvn[...], dckv)
        dkvn_g[...] += g3
        dckr = jnp.concatenate([dckv_pre, dckr_rope], axis=1).astype(BF16)
        dckr_ref[...] = dckr
        dx1_kv, g4 = _rms_bwd(xv, rinv1, nkv[...], _dot_nt(dckr, wdkv[...]))
        dnkv_g[...] += g4
        dx1_ref[...] = dx2_ref[...] + dx1_b + dx1_kv

    tok = lambda c: pl.BlockSpec((tb, c), lambda i: (i, 0))
    pos = pl.BlockSpec((tb, LANE), lambda i: (i % npos, 0))
    consts = [w["q_norm"], w["norm_b"], w["kv_norm"], w["w_dkv_p"], w["norm_kv"]]
    return pl.pallas_call(
        body, name="fb_bwd", grid=(t_all // tb,),
        in_specs=[tok(hv)] * 6 + [tok(qr + hv), tok(kvr + LANE), tok(d), tok(d), ANY] + [_full(c.shape) for c in consts] + [pos, pos],
        out_specs=[tok(d), tok(hv), tok(hv), tok(qr + hv), tok(kvr + LANE), _full((1, qr)), _full((1, d)), _full((1, kvr)), _full((1, d))],
        out_shape=[_sds((t_all, d)), _sds((t_all, hv), BF16), _sds((t_all, hv), BF16), _sds((t_all, qr + hv), BF16), _sds((t_all, kvr + LANE), BF16),
                   _sds((1, qr)), _sds((1, d)), _sds((1, kvr)), _sds((1, d))],
        scratch_shapes=[pltpu.VMEM((qr + hv, d), BF16), pltpu.VMEM((kvr, d), BF16), pltpu.VMEM((kvr, d), BF16), pltpu.VMEM((qr, d), BF16),
                        pltpu.VMEM((qr, d), BF16), pltpu.SemaphoreType.DMA((5 * N_CHIPS,))],
        compiler_params=_params(1),
    )(dqn, dqr, dkn, dkr, dv, dgate, ub, ckr, x1, dx2, wg, *consts, cos_t, sin_t)


def _fa_bwd(dx1, x, u, xb, hs, wg, part16, part32, w, lay, seq, tb):
    t_all, d = x.shape
    dr = lay.dr
    nblocks = w["w_rg"].shape[0]
    nblk = seq // tb
    nt = tb // SUBLANE
    per8 = tb // 8

    def body(dx1_ref, x_ref, u_ref, xb_ref, hs_ref, hh_ref, wg_ref, p16_ref, p32_ref, na, cw, wrg, brg, wig, big, lam,
             gx_ref, du_ref, dna_g, dcw_g, dcb_g, dbrg_g, dbig_g, dlam_g, dwrg_g, dwig_g, got_ref, own_ref,
             wint, wout, hpad, a_s, d_s, g_s, dxpad, carry, sems, send_sems, recv_sems, local_sem):
        b, jj = pl.program_id(0), pl.program_id(1)
        first_block = jj == nblk - 1
        scatter = _Scatter(p16_ref, p32_ref, got_ref, own_ref, send_sems, recv_sems, local_sem, lay, G_GROUPS["early"])

        @pl.when((b == 0) & (jj == 0))
        def _():
            scatter.start()
            cps = _fetch(wg_ref, lay, "in_a", wint, sems, 0) + _fetch(wg_ref, lay, "out_a", wout, sems, N_CHIPS)
            for cp in cps:
                cp.start()
            dna_g[...] = jnp.zeros((1, d), F32)
            dcw_g[...] = jnp.zeros((4, dr), F32)
            dcb_g[...] = jnp.zeros((1, dr), F32)
            dbrg_g[...] = jnp.zeros((1, dr), F32)
            dbig_g[...] = jnp.zeros((1, dr), F32)
            dlam_g[...] = jnp.zeros((1, dr), F32)
            dwrg_g[...] = jnp.zeros((nblocks, LANE, LANE), F32)
            dwig_g[...] = jnp.zeros((nblocks, LANE, LANE), F32)
            for cp in cps:
                cp.wait()

        @pl.when(jj == 0)
        def _():
            dxpad[pl.ds(tb, 8), :] = jnp.zeros((8, dr), F32)
            carry[...] = jnp.zeros((8, dr), F32)

        keep = jnp.where(first_block, 0.0, 1.0)
        dx1v = dx1_ref[...]
        gate = u_ref[:, dr:]
        xpre = u_ref[:, :dr]
        hpad[pl.ds(0, 8), :] = hh_ref[...] * keep
        hpad[pl.ds(8, tb), :] = hs_ref[...]
        xb = xb_ref[...]
        xbb = xb.astype(BF16)
        r, i = _gates(xb, wrg, brg[...], wig, big[...], nblocks)
        sp = _softplus(-lam[...])
        log_a = -LRU_C * r * sp
        a, a2, nem = _decay(log_a)
        mult = jnp.sqrt(nem)
        sg = _sigmoid(gate)
        dy = _dot_nt(dx1v.astype(BF16), wout[...])
        hsv = hs_ref[...]
        dgate = dy * hsv * (sg * (1.0 + gate * (1.0 - sg)))
        a_s[...] = a
        d_s[...] = dy * (gate * sg)
        row = lax.broadcasted_iota(jnp.int32, (8, dr), 0)

        def step(k, c):
            r0 = pl.multiple_of((nt - 1 - k) * 8, 8)
            av = a_s[pl.ds(r0, 8), :]
            dv = d_s[pl.ds(r0, 8), :]
            qv = av * dv
            for s in (1, 2, 4):
                m = row < 8 - s
                a_sh = jnp.where(m, pltpu.roll(av, 8 - s, 0), 1.0)
                q_sh = jnp.where(m, pltpu.roll(qv, 8 - s, 0), 0.0)
                qv = qv + av * q_sh
                av = av * a_sh
            qv = qv + av * c
            g_s[pl.ds(r0, 8), :] = dv + jnp.where(row < 7, pltpu.roll(qv, 7, 0), c)
            return jnp.broadcast_to(qv[0:1, :], qv.shape)

        carry[...] = lax.fori_loop(0, nt, step, carry[...])
        g = g_s[...]
        ix = i * xb
        dlog_a = g * (hpad[pl.ds(7, tb), :] * a - ix * (a2 * lax.rsqrt(nem)))
        dix = g * mult
        dlam_g[...] += -jax.nn.sigmoid(-lam[...]) * jnp.sum(dlog_a * (-LRU_C * r), axis=0, keepdims=True)
        drg = dlog_a * (-LRU_C * sp) * r * (1.0 - r)
        dig = dix * xb * i * (1.0 - i)
        dbrg_g[...] += jnp.sum(drg, axis=0, keepdims=True)
        dbig_g[...] += jnp.sum(dig, axis=0, keepdims=True)
        drgb, digb = drg.astype(BF16), dig.astype(BF16)
        back = []
        for n in range(nblocks):
            cols = slice(n * LANE, (n + 1) * LANE)
            dwrg_g[n] += _dot_tn(xbb[:, cols], drgb[:, cols])
            dwig_g[n] += _dot_tn(xbb[:, cols], digb[:, cols])
            back.append(_dot_nt(drgb[:, cols], wrg[n]) + _dot_nt(digb[:, cols], wig[n]))
        dxb = dix * i + jnp.concatenate(back, axis=1)
        dcb_g[...] += jnp.sum(dxb, axis=0, keepdims=True)
        dxpad[pl.ds(0, tb), :] = dxb
        later = [dxb, dxpad[pl.ds(1, tb), :], dxpad[pl.ds(2, tb), :], dxpad[pl.ds(3, tb), :]]
        dxpad[pl.ds(tb, 8), :] = dxb[:8, :]
        dxpre = cw[3:4, :] * later[0] + cw[2:3, :] * later[1] + cw[1:2, :] * later[2] + cw[0:1, :] * later[3]
        for m in range(4):
            dcw_g[3 - m:4 - m, :] += jnp.sum(later[m] * xpre, axis=0, keepdims=True)
        du = jnp.concatenate([dxpre, dgate], axis=1).astype(BF16)
        du_ref[...] = du
        xv = x_ref[...]
        dxa, g1 = _rms_bwd(xv, _rinv(xv), na[...], _dot(du, wint[...]))
        dna_g[...] += g1
        gx_ref[...] = dx1v + dxa

        @pl.when((b == t_all // seq - 1) & (jj == nblk - 1))
        def _():
            scatter.finish()

    blk = lambda b, j: b * nblk + (nblk - 1 - j)
    tok = lambda c: pl.BlockSpec((tb, c), lambda b, j: (blk(b, j), 0))
    halo = pl.BlockSpec((8, dr), lambda b, j: (jnp.maximum(blk(b, j) * per8 - 1, 0), 0))
    consts = [w["norm_a"], w["conv_w"], w["w_rg"], w["b_rg"], w["w_ig"], w["b_ig"], w["lru_lambda"]]
    vec = lambda c: _full((1, c))
    blocks3 = (nblocks, LANE, LANE)
    return pl.pallas_call(
        body, name="fa_bwd", grid=(t_all // seq, nblk),
        in_specs=[tok(d), tok(d), tok(2 * dr), tok(dr), tok(dr), halo, ANY, ANY, ANY] + [_full(c.shape) for c in consts],
        out_specs=[tok(d), tok(2 * dr), vec(d), _full((4, dr)), vec(dr), vec(dr), vec(dr), vec(dr), _full(blocks3), _full(blocks3), ANY, ANY],
        out_shape=[_sds((t_all, d)), _sds((t_all, 2 * dr), BF16), _sds((1, d)), _sds((4, dr)), _sds((1, dr)), _sds((1, dr)), _sds((1, dr)),
                   _sds((1, dr)), _sds(blocks3), _sds(blocks3)] + _scatter_shapes(lay, "early", part16.shape[1]),
        scratch_shapes=[pltpu.VMEM((2 * dr, d), BF16), pltpu.VMEM((dr, d), BF16), pltpu.VMEM((tb + 8, dr), F32),
                        pltpu.VMEM((tb, dr), F32), pltpu.VMEM((tb, dr), F32), pltpu.VMEM((tb, dr), F32), pltpu.VMEM((tb + 8, dr), F32),
                        pltpu.VMEM((8, dr), F32), pltpu.SemaphoreType.DMA((2 * N_CHIPS,))] + SCATTER_SEMS,
        compiler_params=_params(2),
    )(dx1, x, u, xb, hs, hs, wg, part16, part32, *consts)


def _mm_into(gbuf, a, b, off, name, bt):
    t_all, m = a.shape
    n = b.shape[1]
    nsplit = 2 if m >= 1024 and (m // 2) % LANE == 0 else 1
    mh = m // nsplit
    nt = t_all // bt

    def body(a_ref, b_ref, g_ref, o_ref, acc, sems):
        del g_ref
        part, t = pl.program_id(0), pl.program_id(1)

        def out_copy(h):
            return pltpu.make_async_copy(acc.at[h], o_ref.at[pl.ds(off + h * mh, mh), :], sems.at[h])

        prod = _dot_tn(a_ref[...].astype(BF16), b_ref[...].astype(BF16))
        for h in range(nsplit):
            @pl.when((part == h) & (t == 0))
            def _():
                acc[h] = prod

            @pl.when((part == h) & (t > 0))
            def _():
                acc[h] += prod

            @pl.when((part == h) & (t == nt - 1))
            def _():
                out_copy(h).start()

        @pl.when((part == nsplit - 1) & (t == nt - 1))
        def _():
            for h in range(nsplit):
                out_copy(h).wait()

    return pl.pallas_call(
        body, name=name, grid=(nsplit, nt),
        in_specs=[pl.BlockSpec((bt, mh), lambda h, t: (t, h)), pl.BlockSpec((bt, n), lambda h, t: (t, 0)), ANY],
        out_specs=ANY, out_shape=_sds(gbuf.shape), input_output_aliases={2: 0},
        scratch_shapes=[pltpu.VMEM((nsplit, mh, n), F32), pltpu.SemaphoreType.DMA((nsplit,))],
        compiler_params=_params(2),
    )(a, b, gbuf)


def _mm_tn(a, b, name, bt):
    t_all, m = a.shape
    n = b.shape[1]

    def body(a_ref, b_ref, o_ref):
        @pl.when(pl.program_id(0) == 0)
        def _():
            o_ref[...] = jnp.zeros((m, n), F32)

        o_ref[...] += _dot_tn(a_ref[...].astype(BF16), b_ref[...].astype(BF16))

    return pl.pallas_call(
        body, name=name, grid=(t_all // bt,),
        in_specs=[pl.BlockSpec((bt, m), lambda t: (t, 0)), pl.BlockSpec((bt, n), lambda t: (t, 0))],
        out_specs=_full((m, n)), out_shape=_sds((m, n)),
        compiler_params=_params(1),
    )(a, b)


class _Gather8:
    def __init__(self, x_ref, out_ref, send_sems, recv_sems, local_sem):
        x, y, c = _place()
        self.c, self.me, self.sibling = c, (x, y, c), (x, y, 1 - c)
        self.chips = [(1 - x, y), (x, 1 - y), (1 - x, 1 - y)]
        self.x_ref, self.out_ref, self.send_sems, self.recv_sems, self.local_sem = x_ref, out_ref, send_sems, recv_sems, local_sem

    def _slot(self, px, py, pc):
        return self.out_ref.at[4 * px + 2 * py + pc]

    def _copy(self, k, blk, to, src=None):
        return pltpu.make_async_remote_copy(
            src_ref=self._slot(*blk) if src is None else src, dst_ref=self._slot(*blk), send_sem=self.send_sems.at[k],
            recv_sem=self.recv_sems.at[k], device_id=to, device_id_type=MESH)

    def _mine(self):
        return pltpu.make_async_copy(self.x_ref, self._slot(*self.me), self.local_sem)

    def _first(self):
        return [self._copy(0, self.me, self.sibling, src=self.x_ref)] + [
            self._copy(1 + j, self.me, (*chip, self.c), src=self.x_ref) for j, chip in enumerate(self.chips)]

    def _passed(self):
        return [self._copy(4 + j, (*chip, self.c), self.sibling) for j, chip in enumerate(self.chips)]

    def start(self):
        self._mine().start()
        for cp in self._first():
            cp.start()

    def forward(self):
        passed = self._passed()
        for j, chip in enumerate(self.chips):
            self._copy(1 + j, (*chip, self.c), self.me).wait_recv()
            passed[j].start()

    def finish(self):
        self._copy(0, self.sibling, self.me).wait_recv()
        for j, chip in enumerate(self.chips):
            self._copy(4 + j, (*chip, 1 - self.c), self.me).wait_recv()
        for cp in self._first() + self._passed():
            cp.wait_send()
        self._mine().wait()


class _Scatter:
    def __init__(self, p16_ref, p32_ref, got_ref, own_ref, send_sems, recv_sems, local_sem, lay, order):
        self.x, self.y, self.c = _place()
        self.chips = [(1 - self.x, self.y), (self.x, 1 - self.y), (1 - self.x, 1 - self.y)]
        self.refs = (p16_ref, p32_ref, got_ref, own_ref, send_sems, recv_sems, local_sem)
        self.lay, self.order = lay, order

    def _rows_of(self, ref, key, chip):
        start = pl.multiple_of(self.lay.g_off[key] + chip * self.lay.rows[key], ROW_ALIGN)
        return ref.at[pl.ds(start, self.lay.rows[key]), :]

    def _compact(self, ref, key):
        return ref.at[pl.ds(self.lay.c_off[key], self.lay.rows[key]), :]

    def start(self):
        p16_ref, p32_ref, got_ref, own_ref, send_sems, recv_sems, local_sem = self.refs
        for key in self.order:
            pltpu.make_async_copy(self._rows_of(p32_ref, key, 2 * self.x + self.y), self._compact(own_ref, key), local_sem).start()
        for k, (px, py) in enumerate(self.chips):
            for key in self.order:
                pltpu.make_async_remote_copy(
                    src_ref=self._rows_of(p16_ref, key, 2 * px + py), dst_ref=self._compact(got_ref.at[k], key), send_sem=send_sems.at[k],
                    recv_sem=recv_sems.at[k], device_id=(px, py, self.c), device_id_type=MESH).start()

    def finish(self):
        _, _, got_ref, own_ref, send_sems, recv_sems, local_sem = self.refs
        for k, (px, py) in enumerate(self.chips):
            pltpu.make_async_remote_copy(src_ref=got_ref.at[k], dst_ref=got_ref.at[k], send_sem=send_sems.at[k], recv_sem=recv_sems.at[k],
                                         device_id=(px, py, self.c), device_id_type=MESH).wait()
        pltpu.make_async_copy(own_ref, own_ref, local_sem).wait()


SCATTER_SEMS = [pltpu.SemaphoreType.DMA((3,)), pltpu.SemaphoreType.DMA((3,)), pltpu.SemaphoreType.DMA]
GATHER_SEMS = [pltpu.SemaphoreType.DMA((7,)), pltpu.SemaphoreType.DMA((7,)), pltpu.SemaphoreType.DMA]


def _all_gather8(blocks, name):
    nb = len(blocks)

    def body(*refs):
        x_refs, out_refs = refs[:nb], refs[nb:2 * nb]
        send_sems, recv_sems, local_sems = refs[2 * nb:]
        gathers = [_Gather8(x_refs[n], out_refs[n], send_sems.at[n], recv_sems.at[n], local_sems.at[n]) for n in range(nb)]
        for g in gathers:
            g.start()
        for g in gathers:
            g.forward()
        for g in gathers:
            g.finish()

    return pl.pallas_call(
        body, name=name, out_shape=[_sds((8,) + b.shape, b.dtype) for b in blocks], in_specs=[ANY] * nb, out_specs=[ANY] * nb,
        scratch_shapes=[pltpu.SemaphoreType.DMA((nb, 7)), pltpu.SemaphoreType.DMA((nb, 7)), pltpu.SemaphoreType.DMA((nb,))],
    )(*blocks)


def _swap_sibling(srcs, name, half_cols=False):
    n = len(srcs)
    halves = [s.shape[1] // 2 if half_cols else s.shape[1] for s in srcs]

    def body(*refs):
        src_refs, out_refs, send_sems, recv_sems = refs[:n], refs[n:2 * n], refs[2 * n], refs[2 * n + 1]
        x, y, c = _place()
        copies = []
        for k in range(n):
            part = src_refs[k].at[:, pl.ds(pl.multiple_of((1 - c) * halves[k], LANE), halves[k])] if half_cols else src_refs[k]
            copies.append(pltpu.make_async_remote_copy(src_ref=part, dst_ref=out_refs[k], send_sem=send_sems.at[k], recv_sem=recv_sems.at[k],
                                                       device_id=(x, y, 1 - c), device_id_type=MESH))
        for cp in copies:
            cp.start()
        for cp in copies:
            cp.wait()

    return pl.pallas_call(
        body, name=name, out_shape=[_sds((s.shape[0], h), s.dtype) for s, h in zip(srcs, halves)], in_specs=[ANY] * n, out_specs=[ANY] * n,
        scratch_shapes=[pltpu.SemaphoreType.DMA((n,)), pltpu.SemaphoreType.DMA((n,))],
    )(*srcs)


def _scatter_shapes(lay, group, half):
    return [_sds((3, lay.c_rows[group], half), BF16), _sds((lay.c_rows[group], half), F32)]


def _scatter_chips(part16, part32, lay, group, name):
    def body(p16_ref, p32_ref, got_ref, own_ref, send_sems, recv_sems, local_sem):
        sc = _Scatter(p16_ref, p32_ref, got_ref, own_ref, send_sems, recv_sems, local_sem, lay, G_GROUPS[group])
        sc.start()
        sc.finish()

    return pl.pallas_call(
        body, name=name, out_shape=_scatter_shapes(lay, group, part16.shape[1]), in_specs=[ANY, ANY], out_specs=[ANY, ANY],
        scratch_shapes=SCATTER_SEMS,
    )(part16, part32)


def _sum_sibling(gbuf, got, cidx, name):
    rows, d = gbuf.shape
    half = d // 2
    rb = _row_block(rows)

    def body(c_ref, g_ref, r_ref, o32_ref, o16_ref):
        del c_ref
        s = g_ref[...] + r_ref[...]
        o32_ref[...] = s
        o16_ref[...] = s.astype(BF16)

    plain = pl.BlockSpec((rb, half), lambda i, c: (i, 0))
    return pl.pallas_call(
        body, name=name,
        grid_spec=pltpu.PrefetchScalarGridSpec(num_scalar_prefetch=1, grid=(rows // rb,),
                                               in_specs=[pl.BlockSpec((rb, half), lambda i, c: (i, c[0])), plain], out_specs=[plain, plain]),
        out_shape=[_sds((rows, half)), _sds((rows, half), BF16)], compiler_params=_params(1),
    )(cidx, gbuf, got)


def _sum_chips(own, got, name):
    rows, half = own.shape
    rb = _row_block(rows)

    def body(a_ref, b_ref, o_ref):
        o_ref[...] = ((a_ref[...] + b_ref[0].astype(F32)) + b_ref[1].astype(F32)) + b_ref[2].astype(F32)

    spec = pl.BlockSpec((rb, half), lambda i: (i, 0))
    return pl.pallas_call(
        body, name=name, grid=(rows // rb,), in_specs=[spec, pl.BlockSpec((3, rb, half), lambda i: (0, i, 0))], out_specs=spec,
        out_shape=_sds((rows, half)), compiler_params=_params(1),
    )(own, got)


def _adamw(w, g, m, v):
    m = ADAM_B1 * m + (1.0 - ADAM_B1) * g
    v = ADAM_B2 * v + (1.0 - ADAM_B2) * (g * g)
    m_hat = m / (1.0 - ADAM_B1 ** ADAM_STEP)
    v_hat = v / (1.0 - ADAM_B2 ** ADAM_STEP)
    return -ADAM_LR * (m_hat / (jnp.sqrt(v_hat) + ADAM_EPS) + ADAM_WD * w), m, v


def _adamw_rows(name, w, g, m, v):
    _, rows, cols = w.shape
    rb = _row_block(rows, 256)

    def body(w_ref, g_ref, m_ref, v_ref, d_ref, mo_ref, vo_ref):
        d_ref[...], mo_ref[...], vo_ref[...] = _adamw(w_ref[...], g_ref[...], m_ref[...], v_ref[...])

    spec = pl.BlockSpec((1, rb, cols), lambda i: (0, i, 0))
    return pl.pallas_call(
        body, name=name, grid=(rows // rb,), in_specs=[spec] * 4, out_specs=[spec] * 3, out_shape=[_sds(w.shape)] * 3,
        compiler_params=_params(1),
    )(w, g, m, v)


def _adamw_group(ws, gs, ms, vs):
    n = len(ws)

    def body(*refs):
        for k in range(n):
            w_ref, g_ref, m_ref, v_ref = (refs[j * n + k] for j in range(4))
            outs = _adamw(w_ref[...], g_ref[...], m_ref[...], v_ref[...])
            for j in range(3):
                refs[(4 + j) * n + k][...] = outs[j]

    outs = pl.pallas_call(
        body, name="adamw_small", out_shape=[_sds(w.shape) for w in ws] * 3,
        compiler_params=pltpu.CompilerParams(vmem_limit_bytes=VMEM_LIMIT),
    )(*ws, *gs, *ms, *vs)
    return outs[:n], outs[n:2 * n], outs[2 * n:]


def _gather_weights(sh, lay):
    x, y, c = _place()
    d = lay.d
    uq = sh["w_uq"][0].astype(BF16)
    parts = {
        "in_b": sh["w_in_b"][0].T.astype(BF16), "in_a": sh["w_in_a"][0].T.astype(BF16), "out_a": sh["w_out_a"][0].astype(BF16),
        "out_b": sh["w_out_b"][0].astype(BF16), "uk": sh["w_uk"].astype(BF16).reshape(-1, d), "uv": sh["w_uv"].astype(BF16).reshape(-1, d),
        "uq_n": uq[:, :, :QK_NOPE].reshape(-1, d), "uq_r": jnp.pad(uq[:, :, QK_NOPE:], ((0, 0), (0, 0), (0, LANE - QK_ROPE))).reshape(-1, d),
        "dkv": jnp.pad(sh["w_dkv"].astype(BF16), ((0, 0), (0, LANE - QK_ROPE))).reshape(-1, d),
    }
    halves = {}
    for group, order in W_GROUPS.items():
        stack = jnp.concatenate([parts[k] for k in order], axis=0).reshape(2, lay.w_rows[group] // 2, d)
        halves[group] = lax.dynamic_index_in_dim(stack, c, 0, keepdims=False)
    small = jnp.concatenate([sh[k].reshape(-1) for k in SMALL])
    n_small = small.shape[0]
    width = _round_up(n_small, 2 * SUBLANE * LANE) // (2 * SUBLANE)
    small = jnp.pad(small, (0, 2 * SUBLANE * width - n_small)).reshape(2, SUBLANE, width)
    wg, sg = _all_gather8([halves["a"], lax.dynamic_index_in_dim(small, c, 0, keepdims=False)], "ag_weights")
    wg = wg.reshape(N_CHIPS, lay.w_rows["a"], d)
    sg = sg.reshape(N_CHIPS, 2 * SUBLANE * width)
    full, off = {}, 0
    for k in SMALL:
        n = sh[k].size
        piece = sg[:, off:off + n]
        off += n
        if k == "conv_w":
            full[k] = piece.reshape(N_CHIPS, 4, n // 4).transpose(1, 0, 2).reshape(4, n)
        else:
            full[k] = piece.reshape(1, N_CHIPS * n)
    return wg, halves["b"], full


def _chip_split(g, taps=False):
    if taps:
        n = g.shape[1] // N_CHIPS
        return g.reshape(4, N_CHIPS, n).transpose(1, 0, 2).reshape(N_CHIPS, 4 * n)
    return g.reshape(N_CHIPS, -1)


def kernel(x, norm_a, w_in_a, conv_w, conv_b, w_rg, b_rg, w_ig, b_ig, lru_lambda, w_out_a, norm_kv, w_dkv, kv_norm, w_uk, w_uv, norm_b, w_in_b, q_norm, w_uq, w_out_b, final_norm, loss_target, m_norm_a, m_w_in_a, m_conv_w, m_conv_b, m_w_rg, m_b_rg, m_w_ig, m_b_ig, m_lru_lambda, m_w_out_a, m_norm_kv, m_w_dkv, m_kv_norm, m_w_uk, m_w_uv, m_norm_b, m_w_in_b, m_q_norm, m_w_uq, m_w_out_b, m_final_norm, v_norm_a, v_w_in_a, v_conv_w, v_conv_b, v_w_rg, v_b_rg, v_w_ig, v_b_ig, v_lru_lambda, v_w_out_a, v_norm_kv, v_w_dkv, v_kv_norm, v_w_uk, v_w_uv, v_norm_b, v_w_in_b, v_q_norm, v_w_uq, v_w_out_b, v_final_norm):
    given = dict(locals())
    sh = {k: given[k] for k in WEIGHTS}
    xi, yi, ci = _place()
    nb, seq, d = x.shape
    t_all = nb * seq
    tb_a, tb_b, ta, bt = min(TOKENS_A, seq), min(TOKENS_B, seq), min(TOKENS_ATTN, seq), min(TOKENS_MM, t_all)
    dr = conv_b.shape[1] * N_CHIPS
    qr, kvr, nheads = q_norm.shape[1], kv_norm.shape[0], w_uk.shape[1]
    hv = nheads * LANE
    n_small = sum(sh[k].size for k in SMALL)
    n_repl = sum(sh[k].size for k in REPL)
    lay = _Layout(d, dr, qr, kvr, hv, n_small, n_repl)
    half = d // 2

    wga, wb_half, w = _gather_weights(sh, lay)
    w.update({"w_rg": w_rg[0].astype(BF16), "w_ig": w_ig[0].astype(BF16), "norm_kv": norm_kv[None, :],
              "kv_norm": kv_norm[None, :], "final_norm": final_norm[None, :], "norm_b": norm_b, "q_norm": q_norm})
    cos_t, sin_t = _rope_tables(seq)
    cidx = jnp.reshape(ci, (1,)).astype(jnp.int32)

    x0 = x.reshape(t_all, d)
    x1, u, hs, h, y, xb, wgb = _fa_fwd(x0, wga, wb_half, w, lay, seq, tb_a)
    wgb = wgb.reshape(N_CHIPS, lay.w_rows["b"], d)
    w["w_dkv_p"] = wgb[:, lay.w_off["dkv"]:lay.w_off["dkv"] + lay.rows["dkv"], :].reshape(d, kvr + LANE)
    qn, qrp, kn, kr, v, ub, ckr, hb, hk, cq, ckv = _fb_fwd(x1, wgb, w, lay, cos_t, sin_t, seq, tb_b)
    o, lse = _attn_fwd(qn, qrp, kn, kr, v, seq, ta)
    loss, g_final_norm, yb, dx2, do, dgate, delta = _head(o, ub, x1, loss_target.reshape(t_all, d), wgb, w, lay, tb_b)
    dqn, dqr, dkn, dkr, dv = _attn_bwd(qn, qrp, kn, kr, v, do, lse, delta, seq, ta)
    dx1, dqr_pre, dqn_pre, dub, dckr, g_q_norm, g_norm_b, g_kv_norm, g_norm_kv = _fb_bwd(
        dqn, dqr, dkn, dkr, dv, dgate, ub, ckr, x1, dx2, wgb, w, lay, cos_t, sin_t, seq, tb_b)
    loss = lax.psum(loss[0, 0], ("x", "y", "c"))

    gbuf = lax.empty((lay.g_rows["early"], d), F32)
    for key, a, b in (("in_b", dub, hb), ("out_a", y, dx1), ("out_b", yb, dx2), ("uk", ckv, dkn), ("uv", ckv, dv), ("uq_n", cq, dqn_pre),
                      ("uq_r", cq, dqr_pre)):
        gbuf = _mm_into(gbuf, a, b, lay.g_off[key], "dw_" + key, bt)
    g_dkv = _mm_tn(hk, dckr, "dw_dkv", bt)
    (got,) = _swap_sibling([gbuf], "rs_sibling_early", half_cols=True)
    part32, part16 = _sum_sibling(gbuf, got, cidx, "rs_sum_sibling_early")
    gx, du, g_norm_a, g_conv_w, g_conv_b, g_b_rg, g_b_ig, g_lam, g_w_rg, g_w_ig, others, own = _fa_bwd(
        dx1, x0, u, xb, hs, wga, part16, part32, w, lay, seq, tb_a)
    mine_early = _sum_chips(own, others, "rs_sum_chips_early")

    small = jnp.concatenate([_chip_split(g_norm_a), _chip_split(g_conv_w, taps=True), _chip_split(g_conv_b), _chip_split(g_b_rg),
                             _chip_split(g_b_ig), _chip_split(g_lam)], axis=1)
    small = jnp.pad(small, ((0, 0), (0, lay.small_rows * d - small.shape[1]))).reshape(N_CHIPS, lay.small_rows, d)
    repl_parts = {"w_rg": g_w_rg, "w_ig": g_w_ig, "norm_kv": g_norm_kv, "kv_norm": g_kv_norm, "norm_b": g_norm_b, "q_norm": g_q_norm,
                  "final_norm": g_final_norm}
    repl = jnp.concatenate([repl_parts[k].reshape(-1) for k in REPL])
    repl = jnp.pad(repl, (0, N_CHIPS * lay.repl_rows * d - n_repl)).reshape(N_CHIPS, lay.repl_rows, d)
    pad_rows = lay.rows["rest"] - lay.rows["dkv"] - lay.small_rows - lay.repl_rows
    rest = jnp.concatenate([g_dkv.reshape(N_CHIPS, lay.rows["dkv"], d), small, repl, jnp.zeros((N_CHIPS, pad_rows, d), F32)], axis=1)
    gbuf = lax.dynamic_update_slice(lax.empty((lay.g_rows["late"], d), F32), rest.reshape(N_CHIPS * lay.rows["rest"], d), (lay.g_off["rest"], 0))
    gbuf = _mm_into(gbuf, du, h, lay.g_off["in_a"], "dw_in_a", bt)
    (got,) = _swap_sibling([gbuf], "rs_sibling_late", half_cols=True)
    part32, part16 = _sum_sibling(gbuf, got, cidx, "rs_sum_sibling_late")
    others, own = _scatter_chips(part16, part32, lay, "late", "rs_chips_late")
    mine_late = _sum_chips(own, others, "rs_sum_chips_late")

    theirs_early, theirs_late = _swap_sibling([mine_early, mine_late], "rs_return")
    red = {}
    for group, mine, theirs in (("early", mine_early, theirs_early), ("late", mine_late, theirs_late)):
        red[group] = jnp.concatenate([jnp.where(ci == 0, mine, theirs), jnp.where(ci == 0, theirs, mine)], axis=1)
    r0 = lay.c_off["rest"] + lay.rows["dkv"] + lay.small_rows
    (rep_all,) = _all_gather8([mine_late[r0:r0 + lay.repl_rows]], "ag_rep")
    rep_flat = rep_all.reshape(N_CHIPS, 2, lay.repl_rows, half).transpose(0, 2, 1, 3).reshape(-1)

    def rows(key):
        group = "late" if key in G_GROUPS["late"] else "early"
        return red[group][lay.c_off[key]:lay.c_off[key] + lay.rows[key]]

    grads = {"w_in_b": rows("in_b").T[None], "w_in_a": rows("in_a").T[None], "w_out_a": rows("out_a")[None], "w_out_b": rows("out_b")[None],
             "w_uk": rows("uk").reshape(w_uk.shape), "w_uv": rows("uv").reshape(w_uv.shape)}
    uq_n = rows("uq_n").reshape(qr // N_CHIPS, nheads, LANE)
    uq_r = rows("uq_r").reshape(qr // N_CHIPS, nheads, LANE)[:, :, :QK_ROPE]
    grads["w_uq"] = jnp.concatenate([uq_n, uq_r], axis=2)[None]
    rest_red = rows("rest")
    grads["w_dkv"] = rest_red[:lay.rows["dkv"]].reshape(d // N_CHIPS, kvr + LANE)[:, :kvr + QK_ROPE]
    small_red = rest_red[lay.rows["dkv"]:lay.rows["dkv"] + lay.small_rows].reshape(-1)
    off = 0
    for k in SMALL:
        n = sh[k].size
        grads[k] = small_red[off:off + n].reshape(sh[k].shape)
        off += n
    off = 0
    for k in REPL:
        n = sh[k].size
        grads[k] = rep_flat[off:off + n].reshape(sh[k].shape)
        off += n

    new = {}
    for k in ("w_in_a", "w_in_b", "w_out_a", "w_out_b"):
        new[k] = _adamw_rows("adamw_" + k, sh[k], grads[k], given["m_" + k], given["v_" + k])
    rest_names = [k for k in WEIGHTS if k not in new]
    as2d = lambda a: a[None, :] if a.ndim == 1 else a
    ds, ms, vs = _adamw_group([as2d(sh[k]) for k in rest_names], [as2d(grads[k]) for k in rest_names],
                              [as2d(given["m_" + k]) for k in rest_names], [as2d(given["v_" + k]) for k in rest_names])
    for n, k in enumerate(rest_names):
        new[k] = tuple(a.reshape(sh[k].shape) for a in (ds[n], ms[n], vs[n]))
    return (loss, gx.reshape(nb, seq, d), *[grads[k] for k in WEIGHTS], *[new[k][0] for k in WEIGHTS], *[new[k][1] for k in WEIGHTS],
            *[new[k][2] for k in WEIGHTS])
```

```python
import jax
import jax.numpy as jnp
from jax import lax
from jax.experimental import pallas as pl
from jax.experimental.pallas import tpu as pltpu

F32, BF16 = jnp.float32, jnp.bfloat16
EPS = 1e-6
LRU_C = 8.0
ROPE_THETA = 10000.0
QK_NOPE, QK_ROPE = 128, 64
ATTN_SCALE = (QK_NOPE + QK_ROPE) ** -0.5
LN2 = 0.6931471805599453
Q_SCALE = ATTN_SCALE / LN2
ATTN_ROWS = 64
LANE = 128
SUBLANE = 8
ROW_ALIGN = 32
VMEM_LIMIT = 60000 * 1024
ADAM_LR, ADAM_B1, ADAM_B2, ADAM_EPS, ADAM_WD, ADAM_STEP = 0.001, 0.9, 0.999, 1e-08, 0.01, 10
MESH = pl.DeviceIdType.MESH
ANY = pl.BlockSpec(memory_space=pl.ANY)
N_CHIPS = 4
TOKENS_A, TOKENS_B, TOKENS_ATTN, TOKENS_MM = 256, 512, 512, 2048

SMALL = ("norm_a", "conv_w", "conv_b", "b_rg", "b_ig", "lru_lambda")
REPL = ("w_rg", "w_ig", "norm_kv", "kv_norm", "norm_b", "q_norm", "final_norm")
TRANSPOSED = ("w_in_b", "w_dkv")
WEIGHTS = ("norm_a", "w_in_a", "conv_w", "conv_b", "w_rg", "b_rg", "w_ig", "b_ig", "lru_lambda", "w_out_a", "norm_kv",
           "w_dkv", "kv_norm", "w_uk", "w_uv", "norm_b", "w_in_b", "q_norm", "w_uq", "w_out_b", "final_norm")
W_GROUPS = {"a": ("in_a", "out_a"), "b": ("in_b", "out_b", "uk", "uv", "uq_n", "uq_r", "dkv")}
G_GROUPS = {"early": ("in_b", "out_a", "out_b", "uk", "uv", "uq_n", "uq_r"), "late": ("in_a", "rest")}


def _sds(shape, dtype=F32):
    return jax.ShapeDtypeStruct(tuple(shape), dtype)


def _params(n_grid):
    return pltpu.CompilerParams(dimension_semantics=("arbitrary",) * n_grid, vmem_limit_bytes=VMEM_LIMIT)


def _full(shape):
    nd = len(shape)
    return pl.BlockSpec(tuple(shape), lambda *g: (0,) * nd)


def _round_up(n, k):
    return -(-n // k) * k


def _row_block(rows, cap=512):
    best = SUBLANE
    for r in range(SUBLANE, min(rows, cap) + 1, SUBLANE):
        if rows % r == 0:
            best = r
    return best


def _place():
    return lax.axis_index("x"), lax.axis_index("y"), lax.axis_index("c")


class _Layout:
    def __init__(self, d, dr, qr, kvr, hv, n_small, n_repl):
        assert hv == d, "the packed rows are D_MODEL wide, which must equal heads * 128"
        self.d, self.dr, self.qr, self.kvr, self.hv = d, dr, qr, kvr, hv
        per_chip = {"in_b": (qr + hv) // N_CHIPS, "in_a": 2 * dr // N_CHIPS, "out_a": dr // N_CHIPS, "out_b": hv // N_CHIPS,
                    "uk": kvr // N_CHIPS, "uv": kvr // N_CHIPS, "uq_n": qr // N_CHIPS, "uq_r": qr // N_CHIPS,
                    "dkv": (d // N_CHIPS) * (kvr + LANE) // d}
        assert all(r % ROW_ALIGN == 0 for r in per_chip.values()), per_chip
        self.small_rows = _round_up(-(-n_small // d), SUBLANE)
        self.repl_rows = _round_up(-(-n_repl // (N_CHIPS * d)), SUBLANE)
        per_chip["rest"] = _round_up(per_chip["dkv"] + self.small_rows + self.repl_rows, ROW_ALIGN)
        self.rows = per_chip
        self.w_off, self.w_rows = {}, {}
        for group, order in W_GROUPS.items():
            off = 0
            for k in order:
                self.w_off[k] = off
                off += per_chip[k]
            assert off % ROW_ALIGN == 0, (group, off)
            self.w_rows[group] = off
        self.g_off, self.c_off, self.c_rows, self.g_rows = {}, {}, {}, {}
        for group, order in G_GROUPS.items():
            off = 0
            for k in order:
                self.c_off[k] = off
                self.g_off[k] = N_CHIPS * off
                off += per_chip[k]
            self.c_rows[group] = off
            self.g_rows[group] = N_CHIPS * off


def _dot(a, b):
    return jnp.dot(a, b, preferred_element_type=F32)


def _dot_nt(a, b):
    return lax.dot_general(a, b, (((1,), (1,)), ((), ())), preferred_element_type=F32)


def _dot_tn(a, b):
    return lax.dot_general(a, b, (((0,), (0,)), ((), ())), preferred_element_type=F32)


def _rinv(x):
    return lax.rsqrt(jnp.mean(x * x, axis=-1, keepdims=True) + EPS)


def _rms_bwd(x, rinv, g, dy):
    z = dy * g
    dx = rinv * z - x * (rinv * rinv * rinv) * jnp.mean(z * x, axis=-1, keepdims=True)
    dg = jnp.sum(dy * (x * rinv), axis=0, keepdims=True)
    return dx, dg


def _softplus(z):
    return jnp.maximum(z, 0.0) + jnp.log1p(jnp.exp(-jnp.abs(z)))


def _sigmoid(x):
    return 0.5 * jnp.tanh(0.5 * x) + 0.5


def _decay(log_a):
    a = jnp.exp(log_a)
    a2 = a * a
    return a, a2, -jnp.tanh(log_a) * (a2 + 1.0)


def _swap_halves(x):
    w = x.shape[1]
    lane = lax.broadcasted_iota(jnp.int32, x.shape, 1)
    return jnp.where(lane % QK_ROPE < QK_ROPE // 2, pltpu.roll(x, w - QK_ROPE // 2, 1), pltpu.roll(x, QK_ROPE // 2, 1))


def _rope_tables(seq):
    pos = jnp.arange(seq, dtype=F32)
    inv = ROPE_THETA ** (-jnp.arange(0, QK_ROPE, 2, dtype=F32) / QK_ROPE)
    ang = pos[:, None] * inv[None, :]
    cos, sin = jnp.cos(ang), jnp.sin(ang)
    zero = jnp.zeros((seq, LANE - QK_ROPE), F32)
    return jnp.concatenate([cos, cos, zero], 1), jnp.concatenate([-sin, sin, zero], 1)


def _fetch(wg_ref, lay, key, dst, sems, k0):
    rows = lay.rows[key]
    return [pltpu.make_async_copy(wg_ref.at[p, pl.ds(lay.w_off[key], rows), :], dst.at[pl.ds(p * rows, rows), :], sems.at[k0 + p])
            for p in range(N_CHIPS)]


def _gates(xb, wrg_ref, brg, wig_ref, big, nblocks):
    xbb = xb.astype(BF16)
    rg = [_dot(xbb[:, n * LANE:(n + 1) * LANE], wrg_ref[n]) for n in range(nblocks)]
    ig = [_dot(xbb[:, n * LANE:(n + 1) * LANE], wig_ref[n]) for n in range(nblocks)]
    r = _sigmoid(jnp.concatenate(rg, axis=1) + brg)
    i = _sigmoid(jnp.concatenate(ig, axis=1) + big)
    return r, i


def _conv(xpad, cw_ref, cb, tb):
    return (cb + cw_ref[3:4, :] * xpad[pl.ds(8, tb), :] + cw_ref[2:3, :] * xpad[pl.ds(7, tb), :]
            + cw_ref[1:2, :] * xpad[pl.ds(6, tb), :] + cw_ref[0:1, :] * xpad[pl.ds(5, tb), :])


def _fa_fwd(x, wg, wb_half, w, lay, seq, tb):
    t_all, d = x.shape
    dr = lay.dr
    nblocks = w["w_rg"].shape[0]
    nblk = seq // tb
    nt = tb // SUBLANE
    nsteps = (t_all // seq) * nblk

    def body(x_ref, wg_ref, wbh_ref, na, cw, cb, wrg, brg, wig, big, lam, x1_ref, u_ref, hs_ref, h_ref, y_ref, xb_ref, wb_ref,
             wint, wout, xpad, a_s, b_s, carry, sems, send_sems, recv_sems, local_sem):
        step_no = pl.program_id(0) * nblk + pl.program_id(1)
        gather = _Gather8(wbh_ref, wb_ref, send_sems, recv_sems, local_sem)

        @pl.when(step_no == 0)
        def _():
            gather.start()
            cps = _fetch(wg_ref, lay, "in_a", wint, sems, 0) + _fetch(wg_ref, lay, "out_a", wout, sems, N_CHIPS)
            for cp in cps:
                cp.start()
            for cp in cps:
                cp.wait()

        @pl.when(step_no == nsteps // 2)
        def _():
            gather.forward()

        @pl.when(pl.program_id(1) == 0)
        def _():
            xpad[pl.ds(0, 8), :] = jnp.zeros((8, dr), F32)
            carry[...] = jnp.zeros((8, dr), F32)

        xv = x_ref[...]
        h = (xv * _rinv(xv) * na[...]).astype(BF16)
        h_ref[...] = h
        u = _dot_nt(h, wint[...])
        u_ref[...] = u
        xpre, gate = u[:, :dr], u[:, dr:]
        xpad[pl.ds(8, tb), :] = xpre
        xb = _conv(xpad, cw, cb[...], tb)
        xb_ref[...] = xb
        xpad[pl.ds(0, 8), :] = xpre[tb - 8:, :]
        r, i = _gates(xb, wrg, brg[...], wig, big[...], nblocks)
        log_a = -LRU_C * r * _softplus(-lam[...])
        a, _, nem = _decay(log_a)
        a_s[...] = a
        b_s[...] = jnp.sqrt(nem) * (i * xb)
        row = lax.broadcasted_iota(jnp.int32, (8, dr), 0)

        def step(t, c):
            r0 = pl.multiple_of(t * 8, 8)
            a = a_s[pl.ds(r0, 8), :]
            b = b_s[pl.ds(r0, 8), :]
            for s in (1, 2, 4):
                m = row >= s
                a_sh = jnp.where(m, pltpu.roll(a, s, 0), 1.0)
                b_sh = jnp.where(m, pltpu.roll(b, s, 0), 0.0)
                b = a * b_sh + b
                a = a * a_sh
            hh = b + a * c
            hs_ref[pl.ds(r0, 8), :] = hh
            return jnp.broadcast_to(hh[7:8, :], hh.shape)

        carry[...] = lax.fori_loop(0, nt, step, carry[...])
        y = (hs_ref[...] * (gate * _sigmoid(gate))).astype(BF16)
        y_ref[...] = y
        x1_ref[...] = xv + _dot(y, wout[...])

        @pl.when(step_no == nsteps - 1)
        def _():
            gather.finish()

    tok = lambda c: pl.BlockSpec((tb, c), lambda b, j: (b * nblk + j, 0))
    consts = [w["norm_a"], w["conv_w"], w["conv_b"], w["w_rg"], w["b_rg"], w["w_ig"], w["b_ig"], w["lru_lambda"]]
    return pl.pallas_call(
        body, name="fa_fwd", grid=(t_all // seq, nblk),
        in_specs=[tok(d), ANY, ANY] + [_full(c.shape) for c in consts],
        out_specs=[tok(d), tok(2 * dr), tok(dr), tok(d), tok(dr), tok(dr), ANY],
        out_shape=[_sds((t_all, d)), _sds((t_all, 2 * dr)), _sds((t_all, dr)), _sds((t_all, d), BF16), _sds((t_all, dr), BF16),
                   _sds((t_all, dr)), _sds((8,) + wb_half.shape, BF16)],
        scratch_shapes=[pltpu.VMEM((2 * dr, d), BF16), pltpu.VMEM((dr, d), BF16), pltpu.VMEM((tb + 8, dr), F32), pltpu.VMEM((tb, dr), F32),
                        pltpu.VMEM((tb, dr), F32), pltpu.VMEM((8, dr), F32), pltpu.SemaphoreType.DMA((2 * N_CHIPS,))] + GATHER_SEMS,
        compiler_params=_params(2),
    )(x, wg, wb_half, *consts)


def _fb_fwd(x1, wg, w, lay, cos_t, sin_t, seq, tb):
    t_all, d = x1.shape
    kvr, qr, hv = lay.kvr, lay.qr, lay.hv
    nheads = hv // LANE
    npos = seq // tb

    def body(x_ref, wg_ref, nkv, nb, wdkv, kvn, qn, cos_ref, sin_ref,
             qn_o, qr_o, kn_o, kr_o, v_o, ub_o, ckr_o, hb_o, hk_o, cq_o, ckv_o, winb, wuk, wuv, wuqn, wuqr, sems):
        @pl.when(pl.program_id(0) == 0)
        def _():
            cps = []
            for n, (key, dst) in enumerate((("in_b", winb), ("uk", wuk), ("uv", wuv), ("uq_n", wuqn), ("uq_r", wuqr))):
                cps += _fetch(wg_ref, lay, key, dst, sems, n * N_CHIPS)
            for cp in cps:
                cp.start()
            for cp in cps:
                cp.wait()

        xv = x_ref[...]
        xh = xv * _rinv(xv)
        hk = (xh * nkv[...]).astype(BF16)
        hb = (xh * nb[...]).astype(BF16)
        hk_o[...] = hk
        hb_o[...] = hb
        cos, sin = cos_ref[...], sin_ref[...]
        ckr = _dot(hk, wdkv[...])
        ckr_o[...] = ckr
        ckv_pre = ckr[:, :kvr]
        ckv = (ckv_pre * _rinv(ckv_pre) * kvn[...]).astype(BF16)
        ckv_o[...] = ckv
        kr = ckr[:, kvr:]
        kr_o[...] = (kr * cos + _swap_halves(kr) * sin).astype(BF16)
        kn_o[...] = _dot(ckv, wuk[...]).astype(BF16)
        v_o[...] = _dot(ckv, wuv[...]).astype(BF16)
        ub = _dot_nt(hb, winb[...])
        ub_o[...] = ub
        cq_pre = ub[:, :qr]
        cq = (cq_pre * _rinv(cq_pre) * qn[...]).astype(BF16)
        cq_o[...] = cq
        qn_o[...] = (_dot(cq, wuqn[...]) * Q_SCALE).astype(BF16)
        qrope = _dot(cq, wuqr[...]) * Q_SCALE
        qr_o[...] = (qrope * jnp.tile(cos, (1, nheads)) + _swap_halves(qrope) * jnp.tile(sin, (1, nheads))).astype(BF16)

    tok = lambda c: pl.BlockSpec((tb, c), lambda i: (i, 0))
    pos = pl.BlockSpec((tb, LANE), lambda i: (i % npos, 0))
    consts = [w["norm_kv"], w["norm_b"], w["w_dkv_p"], w["kv_norm"], w["q_norm"]]
    outs = [(hv, BF16), (hv, BF16), (hv, BF16), (LANE, BF16), (hv, BF16), (qr + hv, F32), (kvr + LANE, F32), (d, BF16), (d, BF16), (qr, BF16), (kvr, BF16)]
    return pl.pallas_call(
        body, name="fb_fwd", grid=(t_all // tb,),
        in_specs=[tok(d), ANY] + [_full(c.shape) for c in consts] + [pos, pos],
        out_specs=[tok(c) for c, _ in outs],
        out_shape=[_sds((t_all, c), dt) for c, dt in outs],
        scratch_shapes=[pltpu.VMEM((qr + hv, d), BF16), pltpu.VMEM((kvr, d), BF16), pltpu.VMEM((kvr, d), BF16), pltpu.VMEM((qr, d), BF16),
                        pltpu.VMEM((qr, d), BF16), pltpu.SemaphoreType.DMA((5 * N_CHIPS,))],
        compiler_params=_params(1),
    )(x1, wg, *consts, cos_t, sin_t)


def _causal_mask(row0, col0, nrows, ncols):
    rows = row0 + lax.broadcasted_iota(jnp.int32, (nrows, ncols), 0)
    cols = col0 + lax.broadcasted_iota(jnp.int32, (nrows, ncols), 1)
    return cols <= rows


def _attn_fwd(qn, qr, kn, kr, v, seq, ta):
    t_all, hv = qn.shape
    nheads, nb, na = hv // LANE, t_all // seq, seq // ta

    reps = ta // LANE

    def body(qn_ref, qr_ref, kn_ref, kr_ref, v_ref, o_ref, lse_ref, m_s, l_s, acc_s):
        i = pl.program_id(2)
        m_s[...] = jnp.full((ta, LANE), -1e30, F32)
        l_s[...] = jnp.zeros((ta, LANE), F32)
        acc_s[...] = jnp.zeros((ta, LANE), F32)
        q = jnp.concatenate([qn_ref[...], qr_ref[...]], axis=1)

        def tile(j, diagonal):
            cols = pl.ds(pl.multiple_of(j * ta, ta), ta)
            k = jnp.concatenate([kn_ref[cols, :], kr_ref[cols, :]], axis=1)
            s = _dot_nt(q, k)
            if diagonal:
                s = jnp.where(_causal_mask(0, 0, ta, ta), s, -1e30)
            m_prev = m_s[...]
            m_new = jnp.maximum(m_prev, jnp.max(s, axis=1, keepdims=True))
            p = jnp.exp2(s - jnp.tile(m_new, (1, reps)))
            alpha = jnp.exp2(m_prev - m_new)
            l_s[...] = alpha * l_s[...] + jnp.sum(p, axis=1, keepdims=True)
            acc_s[...] = alpha * acc_s[...] + _dot(p.astype(BF16), v_ref[cols, :])
            m_s[...] = m_new

        def off_diagonal(j, carry):
            tile(j, False)
            return carry

        lax.fori_loop(0, i, off_diagonal, 0)
        tile(i, True)
        o_ref[...] = (acc_s[...] / l_s[...]).astype(BF16)
        lse_ref[...] = m_s[...] + jnp.log2(l_s[...])

    qspec = pl.BlockSpec((ta, LANE), lambda b, h, i: (b * na + i, h))
    kspec = pl.BlockSpec((seq, LANE), lambda b, h, i: (b, h))
    krspec = pl.BlockSpec((seq, LANE), lambda b, h, i: (b, 0))
    return pl.pallas_call(
        body, name="attn_fwd", grid=(nb, nheads, na),
        in_specs=[qspec, qspec, kspec, krspec, kspec],
        out_specs=[qspec, qspec],
        out_shape=[_sds((t_all, hv), BF16), _sds((t_all, hv))],
        scratch_shapes=[pltpu.VMEM((ta, LANE), F32)] * 3,
        compiler_params=_params(3),
    )(qn, qr, kn, kr, v)


def _attn_bwd(qn, qr, kn, kr, v, do, lse, delta, seq, ta):
    t_all, hv = qn.shape
    nheads, nb, na = hv // LANE, t_all // seq, seq // ta

    reps = ta // LANE
    nchunks = ta // ATTN_ROWS

    def body(qn_ref, qr_ref, kn_ref, kr_ref, v_ref, do_ref, lse_ref, dl_ref, dqn_ref, dqr_ref, dkn_ref, dkr_ref, dv_ref,
             s_s, dp_s, p_s, ds_s, dk_s, dv_s):
        j = pl.program_id(2)

        @pl.when(j == 0)
        def _():
            dqn_ref[...] = jnp.zeros((seq, LANE), F32)
            dqr_ref[...] = jnp.zeros((seq, LANE), F32)

        dk_s[...] = jnp.zeros((ta, 2 * LANE), F32)
        dv_s[...] = jnp.zeros((ta, LANE), F32)
        k = jnp.concatenate([kn_ref[...], kr_ref[...]], axis=1)
        vv = v_ref[...]

        def tile(i, diagonal):
            rows_i = pl.ds(pl.multiple_of(i * ta, ta), ta)
            q = jnp.concatenate([qn_ref[rows_i, :], qr_ref[rows_i, :]], axis=1)
            do_b = do_ref[rows_i, :]
            s_s[...] = _dot_nt(q, k)
            dp_s[...] = _dot_nt(do_b, vv)

            def chunk(c, carry):
                rows = pl.ds(pl.multiple_of(c * ATTN_ROWS, ATTN_ROWS), ATTN_ROWS)
                seq_rows = pl.ds(pl.multiple_of(i * ta + c * ATTN_ROWS, ATTN_ROWS), ATTN_ROWS)
                s = s_s[rows, :]
                if diagonal:
                    s = jnp.where(_causal_mask(c * ATTN_ROWS, 0, ATTN_ROWS, ta), s, -1e30)
                p = jnp.exp2(s - jnp.tile(lse_ref[seq_rows, :], (1, reps)))
                p_s[rows, :] = p.astype(BF16)
                ds_s[rows, :] = (p * (dp_s[rows, :] - jnp.tile(dl_ref[seq_rows, :], (1, reps)))).astype(BF16)
                return carry

            lax.fori_loop(0, nchunks, chunk, 0, unroll=True)
            dv_s[...] += _dot_tn(p_s[...], do_b)
            ds = ds_s[...]
            dk_s[...] += _dot_tn(ds, q)
            dq = _dot(ds, k)
            dqn_ref[rows_i, :] += dq[:, :LANE]
            dqr_ref[rows_i, :] += dq[:, LANE:]

        def off_diagonal(i, carry):
            tile(i, False)
            return carry

        tile(j, True)
        lax.fori_loop(j + 1, na, off_diagonal, 0)
        dkn_ref[...] = (dk_s[:, :LANE] * LN2).astype(BF16)
        dkr_ref[...] = dk_s[:, LANE:] * LN2
        dv_ref[...] = dv_s[...].astype(BF16)

    qspec = pl.BlockSpec((seq, LANE), lambda b, h, j: (b, h))
    kspec = pl.BlockSpec((ta, LANE), lambda b, h, j: (b * na + j, h))
    krspec = pl.BlockSpec((ta, LANE), lambda b, h, j: (b * na + j, 0))
    return pl.pallas_call(
        body, name="attn_bwd", grid=(nb, nheads, na),
        in_specs=[qspec, qspec, kspec, krspec, kspec, qspec, qspec, qspec],
        out_specs=[qspec, qspec, kspec, kspec, kspec],
        out_shape=[_sds((t_all, hv)), _sds((t_all, hv)), _sds((t_all, hv), BF16), _sds((t_all, hv)), _sds((t_all, hv), BF16)],
        scratch_shapes=[pltpu.VMEM((ta, ta), F32), pltpu.VMEM((ta, ta), F32), pltpu.VMEM((ta, ta), BF16), pltpu.VMEM((ta, ta), BF16),
                        pltpu.VMEM((ta, 2 * LANE), F32), pltpu.VMEM((ta, LANE), F32)],
        compiler_params=_params(3),
    )(qn, qr, kn, kr, v, do, lse, delta)


def _head(o, ub, x1, target, wg, w, lay, tb):
    t_all, d = x1.shape
    hv, qr = lay.hv, lay.qr
    nheads = hv // LANE

    def body(o_ref, ub_ref, x1_ref, tg_ref, wg_ref, gf, loss_ref, dgf_ref, yb_ref, dx2_ref, do_ref, dg_ref, dl_ref, wob, sems):
        @pl.when(pl.program_id(0) == 0)
        def _():
            cps = _fetch(wg_ref, lay, "out_b", wob, sems, 0)
            for cp in cps:
                cp.start()
            loss_ref[...] = jnp.zeros((1, LANE), F32)
            dgf_ref[...] = jnp.zeros((1, d), F32)
            for cp in cps:
                cp.wait()

        ov = o_ref[...].astype(F32)
        g = ub_ref[:, qr:]
        sg = _sigmoid(g)
        silu = g * sg
        yb = (ov * silu).astype(BF16)
        yb_ref[...] = yb
        x2 = x1_ref[...] + _dot(yb, wob[...])
        rinv = _rinv(x2)
        err = x2 * rinv * gf[...] - tg_ref[...]
        loss_ref[...] += (0.5 / d) * jnp.sum(jnp.sum(err * err, axis=1, keepdims=True), axis=0, keepdims=True)
        dx2, dgf = _rms_bwd(x2, rinv, gf[...], err * (1.0 / d))
        dgf_ref[...] += dgf
        dx2_ref[...] = dx2
        dyb = _dot_nt(dx2.astype(BF16), wob[...])
        dov = dyb * silu
        do_ref[...] = dov.astype(BF16)
        dg_ref[...] = (dyb * ov * (sg * (1.0 + g * (1.0 - sg)))).astype(BF16)
        prod = dov * ov
        dl_ref[...] = jnp.concatenate(
            [jnp.broadcast_to(jnp.sum(prod[:, n * LANE:(n + 1) * LANE], axis=1, keepdims=True), (tb, LANE)) for n in range(nheads)], axis=1)

    tok = lambda c: pl.BlockSpec((tb, c), lambda i: (i, 0))
    return pl.pallas_call(
        body, name="head", grid=(t_all // tb,),
        in_specs=[tok(hv), tok(qr + hv), tok(d), tok(d), ANY, _full((1, d))],
        out_specs=[_full((1, LANE)), _full((1, d)), tok(hv), tok(d), tok(hv), tok(hv), tok(hv)],
        out_shape=[_sds((1, LANE)), _sds((1, d)), _sds((t_all, hv), BF16), _sds((t_all, d)), _sds((t_all, hv), BF16), _sds((t_all, hv), BF16),
                   _sds((t_all, hv))],
        scratch_shapes=[pltpu.VMEM((hv, d), BF16), pltpu.SemaphoreType.DMA((N_CHIPS,))],
        compiler_params=_params(1),
    )(o, ub, x1, target, wg, w["final_norm"])


def _fb_bwd(dqn, dqr, dkn, dkr, dv, dgate, ub, ckr, x1, dx2, wg, w, lay, cos_t, sin_t, seq, tb):
    t_all, d = x1.shape
    hv, qr, kvr = lay.hv, lay.qr, lay.kvr
    nheads = hv // LANE
    npos = seq // tb

    def body(dqn_ref, dqr_ref, dkn_ref, dkr_ref, dv_ref, dg_ref, ub_ref, ckr_ref, x1_ref, dx2_ref, wg_ref,
             qn, nb, kvn, wdkv, nkv, cos_ref, sin_ref,
             dx1_ref, dqrp_ref, dqnp_ref, dub_ref, dckr_ref, dqn_g, dnb_g, dkvn_g, dnkv_g, winb, wuk, wuv, wuqn, wuqr, sems):
        @pl.when(pl.program_id(0) == 0)
        def _():
            cps = []
            for n, (key, dst) in enumerate((("in_b", winb), ("uk", wuk), ("uv", wuv), ("uq_n", wuqn), ("uq_r", wuqr))):
                cps += _fetch(wg_ref, lay, key, dst, sems, n * N_CHIPS)
            for cp in cps:
                cp.start()
            dqn_g[...] = jnp.zeros((1, qr), F32)
            dnb_g[...] = jnp.zeros((1, d), F32)
            dkvn_g[...] = jnp.zeros((1, kvr), F32)
            dnkv_g[...] = jnp.zeros((1, d), F32)
            for cp in cps:
                cp.wait()

        cos, sin = cos_ref[...], sin_ref[...]
        xv = x1_ref[...]
        rinv1 = _rinv(xv)
        dqr_v = dqr_ref[...] * ATTN_SCALE
        dqr_pre = (dqr_v * jnp.tile(cos, (1, nheads)) + _swap_halves(dqr_v * jnp.tile(sin, (1, nheads)))).astype(BF16)
        dqrp_ref[...] = dqr_pre
        dqn_pre = (dqn_ref[...] * ATTN_SCALE).astype(BF16)
        dqnp_ref[...] = dqn_pre
        dcq = _dot_nt(dqn_pre, wuqn[...]) + _dot_nt(dqr_pre, wuqr[...])
        cq_pre = ub_ref[:, :qr]
        dcq_pre, g1 = _rms_bwd(cq_pre, _rinv(cq_pre), qn[...], dcq)
        dqn_g[...] += g1
        dub = jnp.concatenate([dcq_pre.astype(BF16), dg_ref[...]], axis=1)
        dub_ref[...] = dub
        dx1_b, g2 = _rms_bwd(xv, rinv1, nb[...], _dot(dub, winb[...]))
        dnb_g[...] += g2
        dkr_all = dkr_ref[...]
        dkr_sum = dkr_all[:, :LANE]
        for n in range(1, nheads):
            dkr_sum = dkr_sum + dkr_all[:, n * LANE:(n + 1) * LANE]
        dckr_rope = dkr_sum * cos + _swap_halves(dkr_sum * sin)
        dckv = _dot_nt(dkn_ref[...].astype(BF16), wuk[...]) + _dot_nt(dv_ref[...].astype(BF16), wuv[...])
        ckv_pre = ckr_ref[:, :kvr]
        dckv_pre, g3 = _rms_bwd(ckv_pre, _rinv(ckv_pre), kvn[...], dckv)
        dkvn_g[...] += g3
        dckr = jnp.concatenate([dckv_pre, dckr_rope], axis=1).astype(BF16)
        dckr_ref[...] = dckr
        dx1_kv, g4 = _rms_bwd(xv, rinv1, nkv[...], _dot_nt(dckr, wdkv[...]))
        dnkv_g[...] += g4
        dx1_ref[...] = dx2_ref[...] + dx1_b + dx1_kv

    tok = lambda c: pl.BlockSpec((tb, c), lambda i: (i, 0))
    pos = pl.BlockSpec((tb, LANE), lambda i: (i % npos, 0))
    consts = [w["q_norm"], w["norm_b"], w["kv_norm"], w["w_dkv_p"], w["norm_kv"]]
    return pl.pallas_call(
        body, name="fb_bwd", grid=(t_all // tb,),
        in_specs=[tok(hv)] * 6 + [tok(qr + hv), tok(kvr + LANE), tok(d), tok(d), ANY] + [_full(c.shape) for c in consts] + [pos, pos],
        out_specs=[tok(d), tok(hv), tok(hv), tok(qr + hv), tok(kvr + LANE), _full((1, qr)), _full((1, d)), _full((1, kvr)), _full((1, d))],
        out_shape=[_sds((t_all, d)), _sds((t_all, hv), BF16), _sds((t_all, hv), BF16), _sds((t_all, qr + hv), BF16), _sds((t_all, kvr + LANE), BF16),
                   _sds((1, qr)), _sds((1, d)), _sds((1, kvr)), _sds((1, d))],
        scratch_shapes=[pltpu.VMEM((qr + hv, d), BF16), pltpu.VMEM((kvr, d), BF16), pltpu.VMEM((kvr, d), BF16), pltpu.VMEM((qr, d), BF16),
                        pltpu.VMEM((qr, d), BF16), pltpu.SemaphoreType.DMA((5 * N_CHIPS,))],
        compiler_params=_params(1),
    )(dqn, dqr, dkn, dkr, dv, dgate, ub, ckr, x1, dx2, wg, *consts, cos_t, sin_t)


def _fa_bwd(dx1, x, u, xb, hs, wg, part16, part32, w, lay, seq, tb):
    t_all, d = x.shape
    dr = lay.dr
    nblocks = w["w_rg"].shape[0]
    nblk = seq // tb
    nt = tb // SUBLANE
    per8 = tb // 8

    def body(dx1_ref, x_ref, u_ref, xb_ref, hs_ref, hh_ref, wg_ref, p16_ref, p32_ref, na, cw, wrg, brg, wig, big, lam,
             gx_ref, du_ref, dna_g, dcw_g, dcb_g, dbrg_g, dbig_g, dlam_g, dwrg_g, dwig_g, got_ref, own_ref,
             wint, wout, hpad, a_s, d_s, g_s, dxpad, carry, sems, send_sems, recv_sems, local_sem):
        b, jj = pl.program_id(0), pl.program_id(1)
        first_block = jj == nblk - 1
        scatter = _Scatter(p16_ref, p32_ref, got_ref, own_ref, send_sems, recv_sems, local_sem, lay, G_GROUPS["early"])

        @pl.when((b == 0) & (jj == 0))
        def _():
            scatter.start()
            cps = _fetch(wg_ref, lay, "in_a", wint, sems, 0) + _fetch(wg_ref, lay, "out_a", wout, sems, N_CHIPS)
            for cp in cps:
                cp.start()
            dna_g[...] = jnp.zeros((1, d), F32)
            dcw_g[...] = jnp.zeros((4, dr), F32)
            dcb_g[...] = jnp.zeros((1, dr), F32)
            dbrg_g[...] = jnp.zeros((1, dr), F32)
            dbig_g[...] = jnp.zeros((1, dr), F32)
            dlam_g[...] = jnp.zeros((1, dr), F32)
            dwrg_g[...] = jnp.zeros((nblocks, LANE, LANE), F32)
            dwig_g[...] = jnp.zeros((nblocks, LANE, LANE), F32)
            for cp in cps:
                cp.wait()

        @pl.when(jj == 0)
        def _():
            dxpad[pl.ds(tb, 8), :] = jnp.zeros((8, dr), F32)
            carry[...] = jnp.zeros((8, dr), F32)

        keep = jnp.where(first_block, 0.0, 1.0)
        dx1v = dx1_ref[...]
        gate = u_ref[:, dr:]
        xpre = u_ref[:, :dr]
        hpad[pl.ds(0, 8), :] = hh_ref[...] * keep
        hpad[pl.ds(8, tb), :] = hs_ref[...]
        xb = xb_ref[...]
        xbb = xb.astype(BF16)
        r, i = _gates(xb, wrg, brg[...], wig, big[...], nblocks)
        sp = _softplus(-lam[...])
        log_a = -LRU_C * r * sp
        a, a2, nem = _decay(log_a)
        mult = jnp.sqrt(nem)
        sg = _sigmoid(gate)
        dy = _dot_nt(dx1v.astype(BF16), wout[...])
        hsv = hs_ref[...]
        dgate = dy * hsv * (sg * (1.0 + gate * (1.0 - sg)))
        a_s[...] = a
        d_s[...] = dy * (gate * sg)
        row = lax.broadcasted_iota(jnp.int32, (8, dr), 0)

        def step(k, c):
            r0 = pl.multiple_of((nt - 1 - k) * 8, 8)
            av = a_s[pl.ds(r0, 8), :]
            dv = d_s[pl.ds(r0, 8), :]
            qv = av * dv
            for s in (1, 2, 4):
                m = row < 8 - s
                a_sh = jnp.where(m, pltpu.roll(av, 8 - s, 0), 1.0)
                q_sh = jnp.where(m, pltpu.roll(qv, 8 - s, 0), 0.0)
                qv = qv + av * q_sh
                av = av * a_sh
            qv = qv + av * c
            g_s[pl.ds(r0, 8), :] = dv + jnp.where(row < 7, pltpu.roll(qv, 7, 0), c)
            return jnp.broadcast_to(qv[0:1, :], qv.shape)

        carry[...] = lax.fori_loop(0, nt, step, carry[...])
        g = g_s[...]
        ix = i * xb
        dlog_a = g * (hpad[pl.ds(7, tb), :] * a - ix * (a2 * lax.rsqrt(nem)))
        dix = g * mult
        dlam_g[...] += -jax.nn.sigmoid(-lam[...]) * jnp.sum(dlog_a * (-LRU_C * r), axis=0, keepdims=True)
        drg = dlog_a * (-LRU_C * sp) * r * (1.0 - r)
        dig = dix * xb * i * (1.0 - i)
        dbrg_g[...] += jnp.sum(drg, axis=0, keepdims=True)
        dbig_g[...] += jnp.sum(dig, axis=0, keepdims=True)
        drgb, digb = drg.astype(BF16), dig.astype(BF16)
        back = []
        for n in range(nblocks):
            cols = slice(n * LANE, (n + 1) * LANE)
            dwrg_g[n] += _dot_tn(xbb[:, cols], drgb[:, cols])
            dwig_g[n] += _dot_tn(xbb[:, cols], digb[:, cols])
            back.append(_dot_nt(drgb[:, cols], wrg[n]) + _dot_nt(digb[:, cols], wig[n]))
        dxb = dix * i + jnp.concatenate(back, axis=1)
        dcb_g[...] += jnp.sum(dxb, axis=0, keepdims=True)
        dxpad[pl.ds(0, tb), :] = dxb
        later = [dxb, dxpad[pl.ds(1, tb), :], dxpad[pl.ds(2, tb), :], dxpad[pl.ds(3, tb), :]]
        dxpad[pl.ds(tb, 8), :] = dxb[:8, :]
        dxpre = cw[3:4, :] * later[0] + cw[2:3, :] * later[1] + cw[1:2, :] * later[2] + cw[0:1, :] * later[3]
        for m in range(4):
            dcw_g[3 - m:4 - m, :] += jnp.sum(later[m] * xpre, axis=0, keepdims=True)
        du = jnp.concatenate([dxpre, dgate], axis=1).astype(BF16)
        du_ref[...] = du
        xv = x_ref[...]
        dxa, g1 = _rms_bwd(xv, _rinv(xv), na[...], _dot(du, wint[...]))
        dna_g[...] += g1
        gx_ref[...] = dx1v + dxa

        @pl.when((b == t_all // seq - 1) & (jj == nblk - 1))
        def _():
            scatter.finish()

    blk = lambda b, j: b * nblk + (nblk - 1 - j)
    tok = lambda c: pl.BlockSpec((tb, c), lambda b, j: (blk(b, j), 0))
    halo = pl.BlockSpec((8, dr), lambda b, j: (jnp.maximum(blk(b, j) * per8 - 1, 0), 0))
    consts = [w["norm_a"], w["conv_w"], w["w_rg"], w["b_rg"], w["w_ig"], w["b_ig"], w["lru_lambda"]]
    vec = lambda c: _full((1, c))
    blocks3 = (nblocks, LANE, LANE)
    return pl.pallas_call(
        body, name="fa_bwd", grid=(t_all // seq, nblk),
        in_specs=[tok(d), tok(d), tok(2 * dr), tok(dr), tok(dr), halo, ANY, ANY, ANY] + [_full(c.shape) for c in consts],
        out_specs=[tok(d), tok(2 * dr), vec(d), _full((4, dr)), vec(dr), vec(dr), vec(dr), vec(dr), _full(blocks3), _full(blocks3), ANY, ANY],
        out_shape=[_sds((t_all, d)), _sds((t_all, 2 * dr), BF16), _sds((1, d)), _sds((4, dr)), _sds((1, dr)), _sds((1, dr)), _sds((1, dr)),
                   _sds((1, dr)), _sds(blocks3), _sds(blocks3)] + _scatter_shapes(lay, "early", part16.shape[1]),
        scratch_shapes=[pltpu.VMEM((2 * dr, d), BF16), pltpu.VMEM((dr, d), BF16), pltpu.VMEM((tb + 8, dr), F32),
                        pltpu.VMEM((tb, dr), F32), pltpu.VMEM((tb, dr), F32), pltpu.VMEM((tb, dr), F32), pltpu.VMEM((tb + 8, dr), F32),
                        pltpu.VMEM((8, dr), F32), pltpu.SemaphoreType.DMA((2 * N_CHIPS,))] + SCATTER_SEMS,
        compiler_params=_params(2),
    )(dx1, x, u, xb, hs, hs, wg, part16, part32, *consts)


def _mm_into(gbuf, a, b, off, name, bt):
    t_all, m = a.shape
    n = b.shape[1]
    nsplit = 2 if m >= 1024 and (m // 2) % LANE == 0 else 1
    mh = m // nsplit
    nt = t_all // bt

    def body(a_ref, b_ref, g_ref, o_ref, acc, sems):
        del g_ref
        part, t = pl.program_id(0), pl.program_id(1)

        def out_copy(h):
            return pltpu.make_async_copy(acc.at[h], o_ref.at[pl.ds(off + h * mh, mh), :], sems.at[h])

        prod = _dot_tn(a_ref[...].astype(BF16), b_ref[...].astype(BF16))
        for h in range(nsplit):
            @pl.when((part == h) & (t == 0))
            def _():
                acc[h] = prod

            @pl.when((part == h) & (t > 0))
            def _():
                acc[h] += prod

            @pl.when((part == h) & (t == nt - 1))
            def _():
                out_copy(h).start()

        @pl.when((part == nsplit - 1) & (t == nt - 1))
        def _():
            for h in range(nsplit):
                out_copy(h).wait()

    return pl.pallas_call(
        body, name=name, grid=(nsplit, nt),
        in_specs=[pl.BlockSpec((bt, mh), lambda h, t: (t, h)), pl.BlockSpec((bt, n), lambda h, t: (t, 0)), ANY],
        out_specs=ANY, out_shape=_sds(gbuf.shape), input_output_aliases={2: 0},
        scratch_shapes=[pltpu.VMEM((nsplit, mh, n), F32), pltpu.SemaphoreType.DMA((nsplit,))],
        compiler_params=_params(2),
    )(a, b, gbuf)


def _mm_tn(a, b, name, bt):
    t_all, m = a.shape
    n = b.shape[1]

    def body(a_ref, b_ref, o_ref):
        @pl.when(pl.program_id(0) == 0)
        def _():
            o_ref[...] = jnp.zeros((m, n), F32)

        o_ref[...] += _dot_tn(a_ref[...].astype(BF16), b_ref[...].astype(BF16))

    return pl.pallas_call(
        body, name=name, grid=(t_all // bt,),
        in_specs=[pl.BlockSpec((bt, m), lambda t: (t, 0)), pl.BlockSpec((bt, n), lambda t: (t, 0))],
        out_specs=_full((m, n)), out_shape=_sds((m, n)),
        compiler_params=_params(1),
    )(a, b)


class _Gather8:
    def __init__(self, x_ref, out_ref, send_sems, recv_sems, local_sem):
        x, y, c = _place()
        self.c, self.me, self.sibling = c, (x, y, c), (x, y, 1 - c)
        self.chips = [(1 - x, y), (x, 1 - y), (1 - x, 1 - y)]
        self.x_ref, self.out_ref, self.send_sems, self.recv_sems, self.local_sem = x_ref, out_ref, send_sems, recv_sems, local_sem

    def _slot(self, px, py, pc):
        return self.out_ref.at[4 * px + 2 * py + pc]

    def _copy(self, k, blk, to, src=None):
        return pltpu.make_async_remote_copy(
            src_ref=self._slot(*blk) if src is None else src, dst_ref=self._slot(*blk), send_sem=self.send_sems.at[k],
            recv_sem=self.recv_sems.at[k], device_id=to, device_id_type=MESH)

    def _mine(self):
        return pltpu.make_async_copy(self.x_ref, self._slot(*self.me), self.local_sem)

    def _first(self):
        return [self._copy(0, self.me, self.sibling, src=self.x_ref)] + [
            self._copy(1 + j, self.me, (*chip, self.c), src=self.x_ref) for j, chip in enumerate(self.chips)]

    def _passed(self):
        return [self._copy(4 + j, (*chip, self.c), self.sibling) for j, chip in enumerate(self.chips)]

    def start(self):
        self._mine().start()
        for cp in self._first():
            cp.start()

    def forward(self):
        passed = self._passed()
        for j, chip in enumerate(self.chips):
            self._copy(1 + j, (*chip, self.c), self.me).wait_recv()
            passed[j].start()

    def finish(self):
        self._copy(0, self.sibling, self.me).wait_recv()
        for j, chip in enumerate(self.chips):
            self._copy(4 + j, (*chip, 1 - self.c), self.me).wait_recv()
        for cp in self._first() + self._passed():
            cp.wait_send()
        self._mine().wait()


class _Scatter:
    def __init__(self, p16_ref, p32_ref, got_ref, own_ref, send_sems, recv_sems, local_sem, lay, order):
        self.x, self.y, self.c = _place()
        self.chips = [(1 - self.x, self.y), (self.x, 1 - self.y), (1 - self.x, 1 - self.y)]
        self.refs = (p16_ref, p32_ref, got_ref, own_ref, send_sems, recv_sems, local_sem)
        self.lay, self.order = lay, order

    def _rows_of(self, ref, key, chip):
        start = pl.multiple_of(self.lay.g_off[key] + chip * self.lay.rows[key], ROW_ALIGN)
        return ref.at[pl.ds(start, self.lay.rows[key]), :]

    def _compact(self, ref, key):
        return ref.at[pl.ds(self.lay.c_off[key], self.lay.rows[key]), :]

    def start(self):
        p16_ref, p32_ref, got_ref, own_ref, send_sems, recv_sems, local_sem = self.refs
        for key in self.order:
            pltpu.make_async_copy(self._rows_of(p32_ref, key, 2 * self.x + self.y), self._compact(own_ref, key), local_sem).start()
        for k, (px, py) in enumerate(self.chips):
            for key in self.order:
                pltpu.make_async_remote_copy(
                    src_ref=self._rows_of(p16_ref, key, 2 * px + py), dst_ref=self._compact(got_ref.at[k], key), send_sem=send_sems.at[k],
                    recv_sem=recv_sems.at[k], device_id=(px, py, self.c), device_id_type=MESH).start()

    def finish(self):
        _, _, got_ref, own_ref, send_sems, recv_sems, local_sem = self.refs
        for k, (px, py) in enumerate(self.chips):
            pltpu.make_async_remote_copy(src_ref=got_ref.at[k], dst_ref=got_ref.at[k], send_sem=send_sems.at[k], recv_sem=recv_sems.at[k],
                                         device_id=(px, py, self.c), device_id_type=MESH).wait()
        pltpu.make_async_copy(own_ref, own_ref, local_sem).wait()


SCATTER_SEMS = [pltpu.SemaphoreType.DMA((3,)), pltpu.SemaphoreType.DMA((3,)), pltpu.SemaphoreType.DMA]
GATHER_SEMS = [pltpu.SemaphoreType.DMA((7,)), pltpu.SemaphoreType.DMA((7,)), pltpu.SemaphoreType.DMA]


def _all_gather8(blocks, name):
    nb = len(blocks)

    def body(*refs):
        x_refs, out_refs = refs[:nb], refs[nb:2 * nb]
        send_sems, recv_sems, local_sems = refs[2 * nb:]
        gathers = [_Gather8(x_refs[n], out_refs[n], send_sems.at[n], recv_sems.at[n], local_sems.at[n]) for n in range(nb)]
        for g in gathers:
            g.start()
        for g in gathers:
            g.forward()
        for g in gathers:
            g.finish()

    return pl.pallas_call(
        body, name=name, out_shape=[_sds((8,) + b.shape, b.dtype) for b in blocks], in_specs=[ANY] * nb, out_specs=[ANY] * nb,
        scratch_shapes=[pltpu.SemaphoreType.DMA((nb, 7)), pltpu.SemaphoreType.DMA((nb, 7)), pltpu.SemaphoreType.DMA((nb,))],
    )(*blocks)


def _swap_sibling(srcs, name, half_cols=False):
    n = len(srcs)
    halves = [s.shape[1] // 2 if half_cols else s.shape[1] for s in srcs]

    def body(*refs):
        src_refs, out_refs, send_sems, recv_sems = refs[:n], refs[n:2 * n], refs[2 * n], refs[2 * n + 1]
        x, y, c = _place()
        copies = []
        for k in range(n):
            part = src_refs[k].at[:, pl.ds(pl.multiple_of((1 - c) * halves[k], LANE), halves[k])] if half_cols else src_refs[k]
            copies.append(pltpu.make_async_remote_copy(src_ref=part, dst_ref=out_refs[k], send_sem=send_sems.at[k], recv_sem=recv_sems.at[k],
                                                       device_id=(x, y, 1 - c), device_id_type=MESH))
        for cp in copies:
            cp.start()
        for cp in copies:
            cp.wait()

    return pl.pallas_call(
        body, name=name, out_shape=[_sds((s.shape[0], h), s.dtype) for s, h in zip(srcs, halves)], in_specs=[ANY] * n, out_specs=[ANY] * n,
        scratch_shapes=[pltpu.SemaphoreType.DMA((n,)), pltpu.SemaphoreType.DMA((n,))],
    )(*srcs)


def _scatter_shapes(lay, group, half):
    return [_sds((3, lay.c_rows[group], half), BF16), _sds((lay.c_rows[group], half), F32)]


def _scatter_chips(part16, part32, lay, group, name):
    def body(p16_ref, p32_ref, got_ref, own_ref, send_sems, recv_sems, local_sem):
        sc = _Scatter(p16_ref, p32_ref, got_ref, own_ref, send_sems, recv_sems, local_sem, lay, G_GROUPS[group])
        sc.start()
        sc.finish()

    return pl.pallas_call(
        body, name=name, out_shape=_scatter_shapes(lay, group, part16.shape[1]), in_specs=[ANY, ANY], out_specs=[ANY, ANY],
        scratch_shapes=SCATTER_SEMS,
    )(part16, part32)


def _sum_sibling(gbuf, got, cidx, name):
    rows, d = gbuf.shape
    half = d // 2
    rb = _row_block(rows)

    def body(c_ref, g_ref, r_ref, o32_ref, o16_ref):
        del c_ref
        s = g_ref[...] + r_ref[...]
        o32_ref[...] = s
        o16_ref[...] = s.astype(BF16)

    plain = pl.BlockSpec((rb, half), lambda i, c: (i, 0))
    return pl.pallas_call(
        body, name=name,
        grid_spec=pltpu.PrefetchScalarGridSpec(num_scalar_prefetch=1, grid=(rows // rb,),
                                               in_specs=[pl.BlockSpec((rb, half), lambda i, c: (i, c[0])), plain], out_specs=[plain, plain]),
        out_shape=[_sds((rows, half)), _sds((rows, half), BF16)], compiler_params=_params(1),
    )(cidx, gbuf, got)


def _sum_chips(own, got, name):
    rows, half = own.shape
    rb = _row_block(rows)

    def body(a_ref, b_ref, o_ref):
        o_ref[...] = ((a_ref[...] + b_ref[0].astype(F32)) + b_ref[1].astype(F32)) + b_ref[2].astype(F32)

    spec = pl.BlockSpec((rb, half), lambda i: (i, 0))
    return pl.pallas_call(
        body, name=name, grid=(rows // rb,), in_specs=[spec, pl.BlockSpec((3, rb, half), lambda i: (0, i, 0))], out_specs=spec,
        out_shape=_sds((rows, half)), compiler_params=_params(1),
    )(own, got)


def _adamw(w, g, m, v):
    m = ADAM_B1 * m + (1.0 - ADAM_B1) * g
    v = ADAM_B2 * v + (1.0 - ADAM_B2) * (g * g)
    m_hat = m / (1.0 - ADAM_B1 ** ADAM_STEP)
    v_hat = v / (1.0 - ADAM_B2 ** ADAM_STEP)
    return -ADAM_LR * (m_hat / (jnp.sqrt(v_hat) + ADAM_EPS) + ADAM_WD * w), m, v


def _adamw_rows(name, w, g, m, v):
    _, rows, cols = w.shape
    rb = _row_block(rows, 256)

    def body(w_ref, g_ref, m_ref, v_ref, d_ref, mo_ref, vo_ref):
        d_ref[...], mo_ref[...], vo_ref[...] = _adamw(w_ref[...], g_ref[...], m_ref[...], v_ref[...])

    spec = pl.BlockSpec((1, rb, cols), lambda i: (0, i, 0))
    return pl.pallas_call(
        body, name=name, grid=(rows // rb,), in_specs=[spec] * 4, out_specs=[spec] * 3, out_shape=[_sds(w.shape)] * 3,
        compiler_params=_params(1),
    )(w, g, m, v)


def _adamw_group(ws, gs, ms, vs):
    n = len(ws)

    def body(*refs):
        for k in range(n):
            w_ref, g_ref, m_ref, v_ref = (refs[j * n + k] for j in range(4))
            outs = _adamw(w_ref[...], g_ref[...], m_ref[...], v_ref[...])
            for j in range(3):
                refs[(4 + j) * n + k][...] = outs[j]

    outs = pl.pallas_call(
        body, name="adamw_small", out_shape=[_sds(w.shape) for w in ws] * 3,
        compiler_params=pltpu.CompilerParams(vmem_limit_bytes=VMEM_LIMIT),
    )(*ws, *gs, *ms, *vs)
    return outs[:n], outs[n:2 * n], outs[2 * n:]


def _gather_weights(sh, lay):
    x, y, c = _place()
    d = lay.d
    uq = sh["w_uq"][0].astype(BF16)
    parts = {
        "in_b": sh["w_in_b"][0].T.astype(BF16), "in_a": sh["w_in_a"][0].T.astype(BF16), "out_a": sh["w_out_a"][0].astype(BF16),
        "out_b": sh["w_out_b"][0].astype(BF16), "uk": sh["w_uk"].astype(BF16).reshape(-1, d), "uv": sh["w_uv"].astype(BF16).reshape(-1, d),
        "uq_n": uq[:, :, :QK_NOPE].reshape(-1, d), "uq_r": jnp.pad(uq[:, :, QK_NOPE:], ((0, 0), (0, 0), (0, LANE - QK_ROPE))).reshape(-1, d),
        "dkv": jnp.pad(sh["w_dkv"].astype(BF16), ((0, 0), (0, LANE - QK_ROPE))).reshape(-1, d),
    }
    halves = {}
    for group, order in W_GROUPS.items():
        stack = jnp.concatenate([parts[k] for k in order], axis=0).reshape(2, lay.w_rows[group] // 2, d)
        halves[group] = lax.dynamic_index_in_dim(stack, c, 0, keepdims=False)
    small = jnp.concatenate([sh[k].reshape(-1) for k in SMALL])
    n_small = small.shape[0]
    width = _round_up(n_small, 2 * SUBLANE * LANE) // (2 * SUBLANE)
    small = jnp.pad(small, (0, 2 * SUBLANE * width - n_small)).reshape(2, SUBLANE, width)
    wg, sg = _all_gather8([halves["a"], lax.dynamic_index_in_dim(small, c, 0, keepdims=False)], "ag_weights")
    wg = wg.reshape(N_CHIPS, lay.w_rows["a"], d)
    sg = sg.reshape(N_CHIPS, 2 * SUBLANE * width)
    full, off = {}, 0
    for k in SMALL:
        n = sh[k].size
        piece = sg[:, off:off + n]
        off += n
        if k == "conv_w":
            full[k] = piece.reshape(N_CHIPS, 4, n // 4).transpose(1, 0, 2).reshape(4, n)
        else:
            full[k] = piece.reshape(1, N_CHIPS * n)
    return wg, halves["b"], full


def _chip_split(g, taps=False):
    if taps:
        n = g.shape[1] // N_CHIPS
        return g.reshape(4, N_CHIPS, n).transpose(1, 0, 2).reshape(N_CHIPS, 4 * n)
    return g.reshape(N_CHIPS, -1)


def kernel(x, norm_a, w_in_a, conv_w, conv_b, w_rg, b_rg, w_ig, b_ig, lru_lambda, w_out_a, norm_kv, w_dkv, kv_norm, w_uk, w_uv, norm_b, w_in_b, q_norm, w_uq, w_out_b, final_norm, loss_target, m_norm_a, m_w_in_a, m_conv_w, m_conv_b, m_w_rg, m_b_rg, m_w_ig, m_b_ig, m_lru_lambda, m_w_out_a, m_norm_kv, m_w_dkv, m_kv_norm, m_w_uk, m_w_uv, m_norm_b, m_w_in_b, m_q_norm, m_w_uq, m_w_out_b, m_final_norm, v_norm_a, v_w_in_a, v_conv_w, v_conv_b, v_w_rg, v_b_rg, v_w_ig, v_b_ig, v_lru_lambda, v_w_out_a, v_norm_kv, v_w_dkv, v_kv_norm, v_w_uk, v_w_uv, v_norm_b, v_w_in_b, v_q_norm, v_w_uq, v_w_out_b, v_final_norm):
    given = dict(locals())
    sh = {k: given[k] for k in WEIGHTS}
    xi, yi, ci = _place()
    nb, seq, d = x.shape
    t_all = nb * seq
    tb_a, tb_b, ta, bt = min(TOKENS_A, seq), min(TOKENS_B, seq), min(TOKENS_ATTN, seq), min(TOKENS_MM, t_all)
    dr = conv_b.shape[1] * N_CHIPS
    qr, kvr, nheads = q_norm.shape[1], kv_norm.shape[0], w_uk.shape[1]
    hv = nheads * LANE
    n_small = sum(sh[k].size for k in SMALL)
    n_repl = sum(sh[k].size for k in REPL)
    lay = _Layout(d, dr, qr, kvr, hv, n_small, n_repl)
    half = d // 2

    wga, wb_half, w = _gather_weights(sh, lay)
    w.update({"w_rg": w_rg[0].astype(BF16), "w_ig": w_ig[0].astype(BF16), "norm_kv": norm_kv[None, :],
              "kv_norm": kv_norm[None, :], "final_norm": final_norm[None, :], "norm_b": norm_b, "q_norm": q_norm})
    cos_t, sin_t = _rope_tables(seq)
    cidx = jnp.reshape(ci, (1,)).astype(jnp.int32)

    x0 = x.reshape(t_all, d)
    x1, u, hs, h, y, xb, wgb = _fa_fwd(x0, wga, wb_half, w, lay, seq, tb_a)
    wgb = wgb.reshape(N_CHIPS, lay.w_rows["b"], d)
    w["w_dkv_p"] = wgb[:, lay.w_off["dkv"]:lay.w_off["dkv"] + lay.rows["dkv"], :].reshape(d, kvr + LANE)
    qn, qrp, kn, kr, v, ub, ckr, hb, hk, cq, ckv = _fb_fwd(x1, wgb, w, lay, cos_t, sin_t, seq, tb_b)
    o, lse = _attn_fwd(qn, qrp, kn, kr, v, seq, ta)
    loss, g_final_norm, yb, dx2, do, dgate, delta = _head(o, ub, x1, loss_target.reshape(t_all, d), wgb, w, lay, tb_b)
    dqn, dqr, dkn, dkr, dv = _attn_bwd(qn, qrp, kn, kr, v, do, lse, delta, seq, ta)
    dx1, dqr_pre, dqn_pre, dub, dckr, g_q_norm, g_norm_b, g_kv_norm, g_norm_kv = _fb_bwd(
        dqn, dqr, dkn, dkr, dv, dgate, ub, ckr, x1, dx2, wgb, w, lay, cos_t, sin_t, seq, tb_b)
    loss = lax.psum(loss[0, 0], ("x", "y", "c"))

    gbuf = lax.empty((lay.g_rows["early"], d), F32)
    for key, a, b in (("in_b", dub, hb), ("out_a", y, dx1), ("out_b", yb, dx2), ("uk", ckv, dkn), ("uv", ckv, dv), ("uq_n", cq, dqn_pre),
                      ("uq_r", cq, dqr_pre)):
        gbuf = _mm_into(gbuf, a, b, lay.g_off[key], "dw_" + key, bt)
    g_dkv = _mm_tn(hk, dckr, "dw_dkv", bt)
    (got,) = _swap_sibling([gbuf], "rs_sibling_early", half_cols=True)
    part32, part16 = _sum_sibling(gbuf, got, cidx, "rs_sum_sibling_early")
    gx, du, g_norm_a, g_conv_w, g_conv_b, g_b_rg, g_b_ig, g_lam, g_w_rg, g_w_ig, others, own = _fa_bwd(
        dx1, x0, u, xb, hs, wga, part16, part32, w, lay, seq, tb_a)
    mine_early = _sum_chips(own, others, "rs_sum_chips_early")

    small = jnp.concatenate([_chip_split(g_norm_a), _chip_split(g_conv_w, taps=True), _chip_split(g_conv_b), _chip_split(g_b_rg),
                             _chip_split(g_b_ig), _chip_split(g_lam)], axis=1)
    small = jnp.pad(small, ((0, 0), (0, lay.small_rows * d - small.shape[1]))).reshape(N_CHIPS, lay.small_rows, d)
    repl_parts = {"w_rg": g_w_rg, "w_ig": g_w_ig, "norm_kv": g_norm_kv, "kv_norm": g_kv_norm, "norm_b": g_norm_b, "q_norm": g_q_norm,
                  "final_norm": g_final_norm}
    repl = jnp.concatenate([repl_parts[k].reshape(-1) for k in REPL])
    repl = jnp.pad(repl, (0, N_CHIPS * lay.repl_rows * d - n_repl)).reshape(N_CHIPS, lay.repl_rows, d)
    pad_rows = lay.rows["rest"] - lay.rows["dkv"] - lay.small_rows - lay.repl_rows
    rest = jnp.concatenate([g_dkv.reshape(N_CHIPS, lay.rows["dkv"], d), small, repl, jnp.zeros((N_CHIPS, pad_rows, d), F32)], axis=1)
    gbuf = lax.dynamic_update_slice(lax.empty((lay.g_rows["late"], d), F32), rest.reshape(N_CHIPS * lay.rows["rest"], d), (lay.g_off["rest"], 0))
    gbuf = _mm_into(gbuf, du, h, lay.g_off["in_a"], "dw_in_a", bt)
    (got,) = _swap_sibling([gbuf], "rs_sibling_late", half_cols=True)
    part32, part16 = _sum_sibling(gbuf, got, cidx, "rs_sum_sibling_late")
    others, own = _scatter_chips(part16, part32, lay, "late", "rs_chips_late")
    mine_late = _sum_chips(own, others, "rs_sum_chips_late")

    theirs_early, theirs_late = _swap_sibling([mine_early, mine_late], "rs_return")
    red = {}
    for group, mine, theirs in (("early", mine_early, theirs_early), ("late", mine_late, theirs_late)):
        red[group] = jnp.concatenate([jnp.where(ci == 0, mine, theirs), jnp.where(ci == 0, theirs, mine)], axis=1)
    r0 = lay.c_off["rest"] + lay.rows["dkv"] + lay.small_rows
    (rep_all,) = _all_gather8([mine_late[r0:r0 + lay.repl_rows]], "ag_rep")
    rep_flat = rep_all.reshape(N_CHIPS, 2, lay.repl_rows, half).transpose(0, 2, 1, 3).reshape(-1)

    def rows(key):
        group = "late" if key in G_GROUPS["late"] else "early"
        return red[group][lay.c_off[key]:lay.c_off[key] + lay.rows[key]]

    grads = {"w_in_b": rows("in_b").T[None], "w_in_a": rows("in_a").T[None], "w_out_a": rows("out_a")[None], "w_out_b": rows("out_b")[None],
             "w_uk": rows("uk").reshape(w_uk.shape), "w_uv": rows("uv").reshape(w_uv.shape)}
    uq_n = rows("uq_n").reshape(qr // N_CHIPS, nheads, LANE)
    uq_r = rows("uq_r").reshape(qr // N_CHIPS, nheads, LANE)[:, :, :QK_ROPE]
    grads["w_uq"] = jnp.concatenate([uq_n, uq_r], axis=2)[None]
    rest_red = rows("rest")
    grads["w_dkv"] = rest_red[:lay.rows["dkv"]].reshape(d // N_CHIPS, kvr + LANE)[:, :kvr + QK_ROPE]
    small_red = rest_red[lay.rows["dkv"]:lay.rows["dkv"] + lay.small_rows].reshape(-1)
    off = 0
    for k in SMALL:
        n = sh[k].size
        grads[k] = small_red[off:off + n].reshape(sh[k].shape)
        off += n
    off = 0
    for k in REPL:
        n = sh[k].size
        grads[k] = rep_flat[off:off + n].reshape(sh[k].shape)
        off += n

    new = {}
    for k in ("w_in_a", "w_in_b", "w_out_a", "w_out_b"):
        view = (lambda a: jnp.swapaxes(a, 1, 2)) if k in TRANSPOSED else (lambda a: a)
        outs = _adamw_rows("adamw_" + k, view(sh[k]), view(grads[k]), view(given["m_" + k]), view(given["v_" + k]))
        new[k] = tuple(view(a) for a in outs)
    rest_names = [k for k in WEIGHTS if k not in new]

    def as2d(k, a):
        return a.T if k in TRANSPOSED else a[None, :] if a.ndim == 1 else a

    ds, ms, vs = _adamw_group([as2d(k, sh[k]) for k in rest_names], [as2d(k, grads[k]) for k in rest_names],
                              [as2d(k, given["m_" + k]) for k in rest_names], [as2d(k, given["v_" + k]) for k in rest_names])
    for n, k in enumerate(rest_names):
        new[k] = tuple((a.T if k in TRANSPOSED else a).reshape(sh[k].shape) for a in (ds[n], ms[n], vs[n]))
    return (loss, gx.reshape(nb, seq, d), *[grads[k] for k in WEIGHTS], *[new[k][0] for k in WEIGHTS], *[new[k][1] for k in WEIGHTS],
            *[new[k][2] for k in WEIGHTS])
```

```python
import jax
import jax.numpy as jnp
from jax import lax
from jax.experimental import pallas as pl
from jax.experimental.pallas import tpu as pltpu

F32, BF16 = jnp.float32, jnp.bfloat16
EPS = 1e-6
LRU_C = 8.0
ROPE_THETA = 10000.0
QK_NOPE, QK_ROPE = 128, 64
ATTN_SCALE = (QK_NOPE + QK_ROPE) ** -0.5
LN2 = 0.6931471805599453
Q_SCALE = ATTN_SCALE / LN2
ATTN_ROWS = 64
LANE = 128
SUBLANE = 8
ROW_ALIGN = 32
VMEM_LIMIT = 60000 * 1024
ADAM_LR, ADAM_B1, ADAM_B2, ADAM_EPS, ADAM_WD, ADAM_STEP = 0.001, 0.9, 0.999, 1e-08, 0.01, 10
MESH = pl.DeviceIdType.MESH
ANY = pl.BlockSpec(memory_space=pl.ANY)
N_CHIPS = 4
TOKENS_A, TOKENS_B, TOKENS_ATTN, TOKENS_MM = 256, 512, 512, 2048

SMALL = ("norm_a", "conv_w", "conv_b", "b_rg", "b_ig", "lru_lambda")
REPL = ("w_rg", "w_ig", "norm_kv", "kv_norm", "norm_b", "q_norm", "final_norm")
TRANSPOSED = ("w_in_b", "w_dkv")
WEIGHTS = ("norm_a", "w_in_a", "conv_w", "conv_b", "w_rg", "b_rg", "w_ig", "b_ig", "lru_lambda", "w_out_a", "norm_kv",
           "w_dkv", "kv_norm", "w_uk", "w_uv", "norm_b", "w_in_b", "q_norm", "w_uq", "w_out_b", "final_norm")
W_GROUPS = {"a": ("in_a", "out_a"), "b": ("in_b", "out_b", "uk", "uv", "uq_n", "uq_r", "dkv")}
G_GROUPS = {"early": ("in_b", "out_a", "out_b", "uk", "uv", "uq_n", "uq_r"), "late": ("in_a", "rest")}


def _sds(shape, dtype=F32):
    return jax.ShapeDtypeStruct(tuple(shape), dtype)


def _params(n_grid):
    return pltpu.CompilerParams(dimension_semantics=("arbitrary",) * n_grid, vmem_limit_bytes=VMEM_LIMIT)


def _full(shape):
    nd = len(shape)
    return pl.BlockSpec(tuple(shape), lambda *g: (0,) * nd)


def _round_up(n, k):
    return -(-n // k) * k


def _row_block(rows, cap=512):
    best = SUBLANE
    for r in range(SUBLANE, min(rows, cap) + 1, SUBLANE):
        if rows % r == 0:
            best = r
    return best


def _place():
    return lax.axis_index("x"), lax.axis_index("y"), lax.axis_index("c")


class _Layout:
    def __init__(self, d, dr, qr, kvr, hv, n_small, n_repl):
        assert hv == d, "the packed rows are D_MODEL wide, which must equal heads * 128"
        self.d, self.dr, self.qr, self.kvr, self.hv = d, dr, qr, kvr, hv
        per_chip = {"in_b": (qr + hv) // N_CHIPS, "in_a": 2 * dr // N_CHIPS, "out_a": dr // N_CHIPS, "out_b": hv // N_CHIPS,
                    "uk": kvr // N_CHIPS, "uv": kvr // N_CHIPS, "uq_n": qr // N_CHIPS, "uq_r": qr // N_CHIPS,
                    "dkv": (d // N_CHIPS) * (kvr + LANE) // d}
        assert all(r % ROW_ALIGN == 0 for r in per_chip.values()), per_chip
        self.small_rows = _round_up(-(-n_small // d), SUBLANE)
        self.repl_rows = _round_up(-(-n_repl // (N_CHIPS * d)), SUBLANE)
        per_chip["rest"] = _round_up(per_chip["dkv"] + self.small_rows + self.repl_rows, ROW_ALIGN)
        self.rows = per_chip
        self.w_off, self.w_rows = {}, {}
        for group, order in W_GROUPS.items():
            off = 0
            for k in order:
                self.w_off[k] = off
                off += per_chip[k]
            assert off % ROW_ALIGN == 0, (group, off)
            self.w_rows[group] = off
        self.g_off, self.c_off, self.c_rows, self.g_rows = {}, {}, {}, {}
        for group, order in G_GROUPS.items():
            off = 0
            for k in order:
                self.c_off[k] = off
                self.g_off[k] = N_CHIPS * off
                off += per_chip[k]
            self.c_rows[group] = off
            self.g_rows[group] = N_CHIPS * off


def _dot(a, b):
    return jnp.dot(a, b, preferred_element_type=F32)


def _dot_nt(a, b):
    return lax.dot_general(a, b, (((1,), (1,)), ((), ())), preferred_element_type=F32)


def _dot_tn(a, b):
    return lax.dot_general(a, b, (((0,), (0,)), ((), ())), preferred_element_type=F32)


def _rinv(x):
    return lax.rsqrt(jnp.mean(x * x, axis=-1, keepdims=True) + EPS)


def _rms_bwd(x, rinv, g, dy):
    z = dy * g
    dx = rinv * z - x * (rinv * rinv * rinv) * jnp.mean(z * x, axis=-1, keepdims=True)
    dg = jnp.sum(dy * (x * rinv), axis=0, keepdims=True)
    return dx, dg


def _softplus(z):
    return jnp.maximum(z, 0.0) + jnp.log1p(jnp.exp(-jnp.abs(z)))


def _sigmoid(x):
    return 0.5 * jnp.tanh(0.5 * x) + 0.5


def _decay(log_a):
    a = jnp.exp(log_a)
    a2 = a * a
    return a, a2, -jnp.tanh(log_a) * (a2 + 1.0)


def _swap_halves(x):
    w = x.shape[1]
    lane = lax.broadcasted_iota(jnp.int32, x.shape, 1)
    return jnp.where(lane % QK_ROPE < QK_ROPE // 2, pltpu.roll(x, w - QK_ROPE // 2, 1), pltpu.roll(x, QK_ROPE // 2, 1))


def _rope_tables(seq):
    pos = jnp.arange(seq, dtype=F32)
    inv = ROPE_THETA ** (-jnp.arange(0, QK_ROPE, 2, dtype=F32) / QK_ROPE)
    ang = pos[:, None] * inv[None, :]
    cos, sin = jnp.cos(ang), jnp.sin(ang)
    zero = jnp.zeros((seq, LANE - QK_ROPE), F32)
    return jnp.concatenate([cos, cos, zero], 1), jnp.concatenate([-sin, sin, zero], 1)


def _fetch(wg_ref, lay, key, dst, sems, k0):
    rows = lay.rows[key]
    return [pltpu.make_async_copy(wg_ref.at[p, pl.ds(lay.w_off[key], rows), :], dst.at[pl.ds(p * rows, rows), :], sems.at[k0 + p])
            for p in range(N_CHIPS)]


def _gates(xb, wrg_ref, brg, wig_ref, big, nblocks):
    xbb = xb.astype(BF16)
    rg = [_dot(xbb[:, n * LANE:(n + 1) * LANE], wrg_ref[n]) for n in range(nblocks)]
    ig = [_dot(xbb[:, n * LANE:(n + 1) * LANE], wig_ref[n]) for n in range(nblocks)]
    r = _sigmoid(jnp.concatenate(rg, axis=1) + brg)
    i = _sigmoid(jnp.concatenate(ig, axis=1) + big)
    return r, i


def _conv(xpad, cw_ref, cb, tb):
    return (cb + cw_ref[3:4, :] * xpad[pl.ds(8, tb), :] + cw_ref[2:3, :] * xpad[pl.ds(7, tb), :]
            + cw_ref[1:2, :] * xpad[pl.ds(6, tb), :] + cw_ref[0:1, :] * xpad[pl.ds(5, tb), :])


def _fa_fwd(x, wg, wb_half, w, lay, seq, tb):
    t_all, d = x.shape
    dr = lay.dr
    nblocks = w["w_rg"].shape[0]
    nblk = seq // tb
    nt = tb // SUBLANE
    nsteps = (t_all // seq) * nblk

    def body(x_ref, wg_ref, wbh_ref, na, cw, cb, wrg, brg, wig, big, lam, x1_ref, u_ref, hs_ref, h_ref, y_ref, xb_ref, wb_ref,
             wint, wout, xpad, a_s, b_s, carry, sems, send_sems, recv_sems, local_sem):
        step_no = pl.program_id(0) * nblk + pl.program_id(1)
        gather = _Gather8(wbh_ref, wb_ref, send_sems, recv_sems, local_sem)

        @pl.when(step_no == 0)
        def _():
            gather.start()
            cps = _fetch(wg_ref, lay, "in_a", wint, sems, 0) + _fetch(wg_ref, lay, "out_a", wout, sems, N_CHIPS)
            for cp in cps:
                cp.start()
            for cp in cps:
                cp.wait()

        @pl.when(step_no == nsteps // 2)
        def _():
            gather.forward()

        @pl.when(pl.program_id(1) == 0)
        def _():
            xpad[pl.ds(0, 8), :] = jnp.zeros((8, dr), F32)
            carry[...] = jnp.zeros((8, dr), F32)

        xv = x_ref[...]
        h = (xv * _rinv(xv) * na[...]).astype(BF16)
        h_ref[...] = h
        u = _dot_nt(h, wint[...])
        u_ref[...] = u
        xpre, gate = u[:, :dr], u[:, dr:]
        xpad[pl.ds(8, tb), :] = xpre
        xb = _conv(xpad, cw, cb[...], tb)
        xb_ref[...] = xb
        xpad[pl.ds(0, 8), :] = xpre[tb - 8:, :]
        r, i = _gates(xb, wrg, brg[...], wig, big[...], nblocks)
        log_a = -LRU_C * r * _softplus(-lam[...])
        a, _, nem = _decay(log_a)
        a_s[...] = a
        b_s[...] = jnp.sqrt(nem) * (i * xb)
        row = lax.broadcasted_iota(jnp.int32, (8, dr), 0)

        def step(t, c):
            r0 = pl.multiple_of(t * 8, 8)
            a = a_s[pl.ds(r0, 8), :]
            b = b_s[pl.ds(r0, 8), :]
            for s in (1, 2, 4):
                m = row >= s
                a_sh = jnp.where(m, pltpu.roll(a, s, 0), 1.0)
                b_sh = jnp.where(m, pltpu.roll(b, s, 0), 0.0)
                b = a * b_sh + b
                a = a * a_sh
            hh = b + a * c
            hs_ref[pl.ds(r0, 8), :] = hh
            return jnp.broadcast_to(hh[7:8, :], hh.shape)

        carry[...] = lax.fori_loop(0, nt, step, carry[...])
        y = (hs_ref[...] * (gate * _sigmoid(gate))).astype(BF16)
        y_ref[...] = y
        x1_ref[...] = xv + _dot(y, wout[...])

        @pl.when(step_no == nsteps - 1)
        def _():
            gather.finish()

    tok = lambda c: pl.BlockSpec((tb, c), lambda b, j: (b * nblk + j, 0))
    consts = [w["norm_a"], w["conv_w"], w["conv_b"], w["w_rg"], w["b_rg"], w["w_ig"], w["b_ig"], w["lru_lambda"]]
    return pl.pallas_call(
        body, name="fa_fwd", grid=(t_all // seq, nblk),
        in_specs=[tok(d), ANY, ANY] + [_full(c.shape) for c in consts],
        out_specs=[tok(d), tok(2 * dr), tok(dr), tok(d), tok(dr), tok(dr), ANY],
        out_shape=[_sds((t_all, d)), _sds((t_all, 2 * dr)), _sds((t_all, dr)), _sds((t_all, d), BF16), _sds((t_all, dr), BF16),
                   _sds((t_all, dr)), _sds((8,) + wb_half.shape, BF16)],
        scratch_shapes=[pltpu.VMEM((2 * dr, d), BF16), pltpu.VMEM((dr, d), BF16), pltpu.VMEM((tb + 8, dr), F32), pltpu.VMEM((tb, dr), F32),
                        pltpu.VMEM((tb, dr), F32), pltpu.VMEM((8, dr), F32), pltpu.SemaphoreType.DMA((2 * N_CHIPS,))] + GATHER_SEMS,
        compiler_params=_params(2),
    )(x, wg, wb_half, *consts)


def _fb_fwd(x1, wg, w, lay, cos_t, sin_t, seq, tb):
    t_all, d = x1.shape
    kvr, qr, hv = lay.kvr, lay.qr, lay.hv
    nheads = hv // LANE
    npos = seq // tb

    def body(x_ref, wg_ref, nkv, nb, wdkv, kvn, qn, cos_ref, sin_ref,
             qn_o, qr_o, kn_o, kr_o, v_o, ub_o, ckr_o, hb_o, hk_o, cq_o, ckv_o, winb, wuk, wuv, wuqn, wuqr, sems):
        @pl.when(pl.program_id(0) == 0)
        def _():
            cps = []
            for n, (key, dst) in enumerate((("in_b", winb), ("uk", wuk), ("uv", wuv), ("uq_n", wuqn), ("uq_r", wuqr))):
                cps += _fetch(wg_ref, lay, key, dst, sems, n * N_CHIPS)
            for cp in cps:
                cp.start()
            for cp in cps:
                cp.wait()

        xv = x_ref[...]
        xh = xv * _rinv(xv)
        hk = (xh * nkv[...]).astype(BF16)
        hb = (xh * nb[...]).astype(BF16)
        hk_o[...] = hk
        hb_o[...] = hb
        cos, sin = cos_ref[...], sin_ref[...]
        ckr = _dot(hk, wdkv[...])
        ckr_o[...] = ckr
        ckv_pre = ckr[:, :kvr]
        ckv = (ckv_pre * _rinv(ckv_pre) * kvn[...]).astype(BF16)
        ckv_o[...] = ckv
        kr = ckr[:, kvr:]
        kr_o[...] = (kr * cos + _swap_halves(kr) * sin).astype(BF16)
        kn_o[...] = _dot(ckv, wuk[...]).astype(BF16)
        v_o[...] = _dot(ckv, wuv[...]).astype(BF16)
        ub = _dot_nt(hb, winb[...])
        ub_o[...] = ub
        cq_pre = ub[:, :qr]
        cq = (cq_pre * _rinv(cq_pre) * qn[...]).astype(BF16)
        cq_o[...] = cq
        qn_o[...] = (_dot(cq, wuqn[...]) * Q_SCALE).astype(BF16)
        qrope = _dot(cq, wuqr[...]) * Q_SCALE
        qr_o[...] = (qrope * jnp.tile(cos, (1, nheads)) + _swap_halves(qrope) * jnp.tile(sin, (1, nheads))).astype(BF16)

    tok = lambda c: pl.BlockSpec((tb, c), lambda i: (i, 0))
    pos = pl.BlockSpec((tb, LANE), lambda i: (i % npos, 0))
    consts = [w["norm_kv"], w["norm_b"], w["w_dkv_p"], w["kv_norm"], w["q_norm"]]
    outs = [(hv, BF16), (hv, BF16), (hv, BF16), (LANE, BF16), (hv, BF16), (qr + hv, F32), (kvr + LANE, F32), (d, BF16), (d, BF16), (qr, BF16), (kvr, BF16)]
    return pl.pallas_call(
        body, name="fb_fwd", grid=(t_all // tb,),
        in_specs=[tok(d), ANY] + [_full(c.shape) for c in consts] + [pos, pos],
        out_specs=[tok(c) for c, _ in outs],
        out_shape=[_sds((t_all, c), dt) for c, dt in outs],
        scratch_shapes=[pltpu.VMEM((qr + hv, d), BF16), pltpu.VMEM((kvr, d), BF16), pltpu.VMEM((kvr, d), BF16), pltpu.VMEM((qr, d), BF16),
                        pltpu.VMEM((qr, d), BF16), pltpu.SemaphoreType.DMA((5 * N_CHIPS,))],
        compiler_params=_params(1),
    )(x1, wg, *consts, cos_t, sin_t)


def _causal_mask(row0, col0, nrows, ncols):
    rows = row0 + lax.broadcasted_iota(jnp.int32, (nrows, ncols), 0)
    cols = col0 + lax.broadcasted_iota(jnp.int32, (nrows, ncols), 1)
    return cols <= rows


def _attn_fwd(qn, qr, kn, kr, v, seq, ta):
    t_all, hv = qn.shape
    nheads, nb, na = hv // LANE, t_all // seq, seq // ta

    reps = ta // LANE

    def body(qn_ref, qr_ref, kn_ref, kr_ref, v_ref, o_ref, lse_ref, m_s, l_s, acc_s):
        i = pl.program_id(2)
        m_s[...] = jnp.full((ta, LANE), -1e30, F32)
        l_s[...] = jnp.zeros((ta, LANE), F32)
        acc_s[...] = jnp.zeros((ta, LANE), F32)
        q = jnp.concatenate([qn_ref[...], qr_ref[...]], axis=1)

        def tile(j, diagonal):
            cols = pl.ds(pl.multiple_of(j * ta, ta), ta)
            k = jnp.concatenate([kn_ref[cols, :], kr_ref[cols, :]], axis=1)
            s = _dot_nt(q, k)
            if diagonal:
                s = jnp.where(_causal_mask(0, 0, ta, ta), s, -1e30)
            m_prev = m_s[...]
            m_new = jnp.maximum(m_prev, jnp.max(s, axis=1, keepdims=True))
            p = jnp.exp2(s - jnp.tile(m_new, (1, reps)))
            alpha = jnp.exp2(m_prev - m_new)
            l_s[...] = alpha * l_s[...] + jnp.sum(p, axis=1, keepdims=True)
            acc_s[...] = alpha * acc_s[...] + _dot(p.astype(BF16), v_ref[cols, :])
            m_s[...] = m_new

        def off_diagonal(j, carry):
            tile(j, False)
            return carry

        lax.fori_loop(0, i, off_diagonal, 0)
        tile(i, True)
        o_ref[...] = (acc_s[...] / l_s[...]).astype(BF16)
        lse_ref[...] = m_s[...] + jnp.log2(l_s[...])

    qspec = pl.BlockSpec((ta, LANE), lambda b, h, i: (b * na + i, h))
    kspec = pl.BlockSpec((seq, LANE), lambda b, h, i: (b, h))
    krspec = pl.BlockSpec((seq, LANE), lambda b, h, i: (b, 0))
    return pl.pallas_call(
        body, name="attn_fwd", grid=(nb, nheads, na),
        in_specs=[qspec, qspec, kspec, krspec, kspec],
        out_specs=[qspec, qspec],
        out_shape=[_sds((t_all, hv), BF16), _sds((t_all, hv))],
        scratch_shapes=[pltpu.VMEM((ta, LANE), F32)] * 3,
        compiler_params=_params(3),
    )(qn, qr, kn, kr, v)


def _attn_bwd(qn, qr, kn, kr, v, do, lse, delta, seq, ta):
    t_all, hv = qn.shape
    nheads, nb, na = hv // LANE, t_all // seq, seq // ta

    reps = ta // LANE
    nchunks = ta // ATTN_ROWS

    def body(qn_ref, qr_ref, kn_ref, kr_ref, v_ref, do_ref, lse_ref, dl_ref, dqn_ref, dqr_ref, dkn_ref, dkr_ref, dv_ref,
             s_s, dp_s, p_s, ds_s, dk_s, dv_s):
        j = pl.program_id(2)

        @pl.when(j == 0)
        def _():
            dqn_ref[...] = jnp.zeros((seq, LANE), F32)
            dqr_ref[...] = jnp.zeros((seq, LANE), F32)

        dk_s[...] = jnp.zeros((ta, 2 * LANE), F32)
        dv_s[...] = jnp.zeros((ta, LANE), F32)
        k = jnp.concatenate([kn_ref[...], kr_ref[...]], axis=1)
        vv = v_ref[...]

        def tile(i, diagonal):
            rows_i = pl.ds(pl.multiple_of(i * ta, ta), ta)
            q = jnp.concatenate([qn_ref[rows_i, :], qr_ref[rows_i, :]], axis=1)
            do_b = do_ref[rows_i, :]
            s_s[...] = _dot_nt(q, k)
            dp_s[...] = _dot_nt(do_b, vv)

            def chunk(c, carry):
                rows = pl.ds(pl.multiple_of(c * ATTN_ROWS, ATTN_ROWS), ATTN_ROWS)
                seq_rows = pl.ds(pl.multiple_of(i * ta + c * ATTN_ROWS, ATTN_ROWS), ATTN_ROWS)
                s = s_s[rows, :]
                if diagonal:
                    s = jnp.where(_causal_mask(c * ATTN_ROWS, 0, ATTN_ROWS, ta), s, -1e30)
                p = jnp.exp2(s - jnp.tile(lse_ref[seq_rows, :], (1, reps)))
                p_s[rows, :] = p.astype(BF16)
                ds_s[rows, :] = (p * (dp_s[rows, :] - jnp.tile(dl_ref[seq_rows, :], (1, reps)))).astype(BF16)
                return carry

            lax.fori_loop(0, nchunks, chunk, 0, unroll=True)
            dv_s[...] += _dot_tn(p_s[...], do_b)
            ds = ds_s[...]
            dk_s[...] += _dot_tn(ds, q)
            dq = _dot(ds, k)
            dqn_ref[rows_i, :] += dq[:, :LANE]
            dqr_ref[rows_i, :] += dq[:, LANE:]

        def off_diagonal(i, carry):
            tile(i, False)
            return carry

        tile(j, True)
        lax.fori_loop(j + 1, na, off_diagonal, 0)
        dkn_ref[...] = (dk_s[:, :LANE] * LN2).astype(BF16)
        dkr_ref[...] = dk_s[:, LANE:] * LN2
        dv_ref[...] = dv_s[...].astype(BF16)

    qspec = pl.BlockSpec((seq, LANE), lambda b, h, j: (b, h))
    kspec = pl.BlockSpec((ta, LANE), lambda b, h, j: (b * na + j, h))
    krspec = pl.BlockSpec((ta, LANE), lambda b, h, j: (b * na + j, 0))
    return pl.pallas_call(
        body, name="attn_bwd", grid=(nb, nheads, na),
        in_specs=[qspec, qspec, kspec, krspec, kspec, qspec, qspec, qspec],
        out_specs=[qspec, qspec, kspec, kspec, kspec],
        out_shape=[_sds((t_all, hv)), _sds((t_all, hv)), _sds((t_all, hv), BF16), _sds((t_all, hv)), _sds((t_all, hv), BF16)],
        scratch_shapes=[pltpu.VMEM((ta, ta), F32), pltpu.VMEM((ta, ta), F32), pltpu.VMEM((ta, ta), BF16), pltpu.VMEM((ta, ta), BF16),
                        pltpu.VMEM((ta, 2 * LANE), F32), pltpu.VMEM((ta, LANE), F32)],
        compiler_params=_params(3),
    )(qn, qr, kn, kr, v, do, lse, delta)


def _head(o, ub, x1, target, wg, w, lay, tb):
    t_all, d = x1.shape
    hv, qr = lay.hv, lay.qr
    nheads = hv // LANE

    def body(o_ref, ub_ref, x1_ref, tg_ref, wg_ref, gf, loss_ref, dgf_ref, yb_ref, dx2_ref, do_ref, dg_ref, dl_ref, wob, sems):
        @pl.when(pl.program_id(0) == 0)
        def _():
            cps = _fetch(wg_ref, lay, "out_b", wob, sems, 0)
            for cp in cps:
                cp.start()
            loss_ref[...] = jnp.zeros((1, LANE), F32)
            dgf_ref[...] = jnp.zeros((1, d), F32)
            for cp in cps:
                cp.wait()

        ov = o_ref[...].astype(F32)
        g = ub_ref[:, qr:]
        sg = _sigmoid(g)
        silu = g * sg
        yb = (ov * silu).astype(BF16)
        yb_ref[...] = yb
        x2 = x1_ref[...] + _dot(yb, wob[...])
        rinv = _rinv(x2)
        err = x2 * rinv * gf[...] - tg_ref[...]
        loss_ref[...] += (0.5 / d) * jnp.sum(jnp.sum(err * err, axis=1, keepdims=True), axis=0, keepdims=True)
        dx2, dgf = _rms_bwd(x2, rinv, gf[...], err * (1.0 / d))
        dgf_ref[...] += dgf
        dx2_ref[...] = dx2
        dyb = _dot_nt(dx2.astype(BF16), wob[...])
        dov = dyb * silu
        do_ref[...] = dov.astype(BF16)
        dg_ref[...] = (dyb * ov * (sg * (1.0 + g * (1.0 - sg)))).astype(BF16)
        prod = dov * ov
        dl_ref[...] = jnp.concatenate(
            [jnp.broadcast_to(jnp.sum(prod[:, n * LANE:(n + 1) * LANE], axis=1, keepdims=True), (tb, LANE)) for n in range(nheads)], axis=1)

    tok = lambda c: pl.BlockSpec((tb, c), lambda i: (i, 0))
    return pl.pallas_call(
        body, name="head", grid=(t_all // tb,),
        in_specs=[tok(hv), tok(qr + hv), tok(d), tok(d), ANY, _full((1, d))],
        out_specs=[_full((1, LANE)), _full((1, d)), tok(hv), tok(d), tok(hv), tok(hv), tok(hv)],
        out_shape=[_sds((1, LANE)), _sds((1, d)), _sds((t_all, hv), BF16), _sds((t_all, d)), _sds((t_all, hv), BF16), _sds((t_all, hv), BF16),
                   _sds((t_all, hv))],
        scratch_shapes=[pltpu.VMEM((hv, d), BF16), pltpu.SemaphoreType.DMA((N_CHIPS,))],
        compiler_params=_params(1),
    )(o, ub, x1, target, wg, w["final_norm"])


def _fb_bwd(dqn, dqr, dkn, dkr, dv, dgate, ub, ckr, x1, dx2, wg, w, lay, cos_t, sin_t, seq, tb):
    t_all, d = x1.shape
    hv, qr, kvr = lay.hv, lay.qr, lay.kvr
    nheads = hv // LANE
    npos = seq // tb

    def body(dqn_ref, dqr_ref, dkn_ref, dkr_ref, dv_ref, dg_ref, ub_ref, ckr_ref, x1_ref, dx2_ref, wg_ref,
             qn, nb, kvn, wdkv, nkv, cos_ref, sin_ref,
             dx1_ref, dqrp_ref, dqnp_ref, dub_ref, dckr_ref, dqn_g, dnb_g, dkvn_g, dnkv_g, winb, wuk, wuv, wuqn, wuqr, sems):
        @pl.when(pl.program_id(0) == 0)
        def _():
            cps = []
            for n, (key, dst) in enumerate((("in_b", winb), ("uk", wuk), ("uv", wuv), ("uq_n", wuqn), ("uq_r", wuqr))):
                cps += _fetch(wg_ref, lay, key, dst, sems, n * N_CHIPS)
            for cp in cps:
                cp.start()
            dqn_g[...] = jnp.zeros((1, qr), F32)
            dnb_g[...] = jnp.zeros((1, d), F32)
            dkvn_g[...] = jnp.zeros((1, kvr), F32)
            dnkv_g[...] = jnp.zeros((1, d), F32)
            for cp in cps:
                cp.wait()

        cos, sin = cos_ref[...], sin_ref[...]
        xv = x1_ref[...]
        rinv1 = _rinv(xv)
        dqr_v = dqr_ref[...] * ATTN_SCALE
        dqr_pre = (dqr_v * jnp.tile(cos, (1, nheads)) + _swap_halves(dqr_v * jnp.tile(sin, (1, nheads)))).astype(BF16)
        dqrp_ref[...] = dqr_pre
        dqn_pre = (dqn_ref[...] * ATTN_SCALE).astype(BF16)
        dqnp_ref[...] = dqn_pre
        dcq = _dot_nt(dqn_pre, wuqn[...]) + _dot_nt(dqr_pre, wuqr[...])
        cq_pre = ub_ref[:, :qr]
        dcq_pre, g1 = _rms_bwd(cq_pre, _rinv(cq_pre), qn[...], dcq)
        dqn_g[...] += g1
        dub = jnp.concatenate([dcq_pre.astype(BF16), dg_ref[...]], axis=1)
        dub_ref[...] = dub
        dx1_b, g2 = _rms_bwd(xv, rinv1, nb[...], _dot(dub, winb[...]))
        dnb_g[...] += g2
        dkr_all = dkr_ref[...]
        dkr_sum = dkr_all[:, :LANE]
        for n in range(1, nheads):
            dkr_sum = dkr_sum + dkr_all[:, n * LANE:(n + 1) * LANE]
        dckr_rope = dkr_sum * cos + _swap_halves(dkr_sum * sin)
        dckv = _dot_nt(dkn_ref[...].astype(BF16), wuk[...]) + _dot_nt(dv_ref[...].astype(BF16), wuv[...])
        ckv_pre = ckr_ref[:, :kvr]
        dckv_pre, g3 = _rms_bwd(ckv_pre, _rinv(ckv_pre), kvn[...], dckv)
        dkvn_g[...] += g3
        dckr = jnp.concatenate([dckv_pre, dckr_rope], axis=1).astype(BF16)
        dckr_ref[...] = dckr
        dx1_kv, g4 = _rms_bwd(xv, rinv1, nkv[...], _dot_nt(dckr, wdkv[...]))
        dnkv_g[...] += g4
        dx1_ref[...] = dx2_ref[...] + dx1_b + dx1_kv

    tok = lambda c: pl.BlockSpec((tb, c), lambda i: (i, 0))
    pos = pl.BlockSpec((tb, LANE), lambda i: (i % npos, 0))
    consts = [w["q_norm"], w["norm_b"], w["kv_norm"], w["w_dkv_p"], w["norm_kv"]]
    return pl.pallas_call(
        body, name="fb_bwd", grid=(t_all // tb,),
        in_specs=[tok(hv)] * 6 + [tok(qr + hv), tok(kvr + LANE), tok(d), tok(d), ANY] + [_full(c.shape) for c in consts] + [pos, pos],
        out_specs=[tok(d), tok(hv), tok(hv), tok(qr + hv), tok(kvr + LANE), _full((1, qr)), _full((1, d)), _full((1, kvr)), _full((1, d))],
        out_shape=[_sds((t_all, d)), _sds((t_all, hv), BF16), _sds((t_all, hv), BF16), _sds((t_all, qr + hv), BF16), _sds((t_all, kvr + LANE), BF16),
                   _sds((1, qr)), _sds((1, d)), _sds((1, kvr)), _sds((1, d))],
        scratch_shapes=[pltpu.VMEM((qr + hv, d), BF16), pltpu.VMEM((kvr, d), BF16), pltpu.VMEM((kvr, d), BF16), pltpu.VMEM((qr, d), BF16),
                        pltpu.VMEM((qr, d), BF16), pltpu.SemaphoreType.DMA((5 * N_CHIPS,))],
        compiler_params=_params(1),
    )(dqn, dqr, dkn, dkr, dv, dgate, ub, ckr, x1, dx2, wg, *consts, cos_t, sin_t)


def _fa_bwd(dx1, x, u, xb, hs, wg, g16, g32, w, lay, seq, tb):
    t_all, d = x.shape
    dr = lay.dr
    nblocks = w["w_rg"].shape[0]
    nblk = seq // tb
    nt = tb // SUBLANE
    per8 = tb // 8

    def body(dx1_ref, x_ref, u_ref, xb_ref, hs_ref, hh_ref, wg_ref, g16_ref, g32_ref, na, cw, wrg, brg, wig, big, lam,
             gx_ref, du_ref, dna_g, dcw_g, dcb_g, dbrg_g, dbig_g, dlam_g, dwrg_g, dwig_g, got_ref, sib_ref, own_ref,
             wint, wout, hpad, a_s, d_s, g_s, dxpad, carry, sems, send_sems, recv_sems, local_sem):
        b, jj = pl.program_id(0), pl.program_id(1)
        first_block = jj == nblk - 1
        scatter = _ScatterDirect(g16_ref, g32_ref, got_ref, sib_ref, own_ref, send_sems, recv_sems, local_sem, lay, G_GROUPS["early"])

        @pl.when((b == 0) & (jj == 0))
        def _():
            scatter.start()
            cps = _fetch(wg_ref, lay, "in_a", wint, sems, 0) + _fetch(wg_ref, lay, "out_a", wout, sems, N_CHIPS)
            for cp in cps:
                cp.start()
            dna_g[...] = jnp.zeros((1, d), F32)
            dcw_g[...] = jnp.zeros((4, dr), F32)
            dcb_g[...] = jnp.zeros((1, dr), F32)
            dbrg_g[...] = jnp.zeros((1, dr), F32)
            dbig_g[...] = jnp.zeros((1, dr), F32)
            dlam_g[...] = jnp.zeros((1, dr), F32)
            dwrg_g[...] = jnp.zeros((nblocks, LANE, LANE), F32)
            dwig_g[...] = jnp.zeros((nblocks, LANE, LANE), F32)
            for cp in cps:
                cp.wait()

        @pl.when(jj == 0)
        def _():
            dxpad[pl.ds(tb, 8), :] = jnp.zeros((8, dr), F32)
            carry[...] = jnp.zeros((8, dr), F32)

        keep = jnp.where(first_block, 0.0, 1.0)
        dx1v = dx1_ref[...]
        gate = u_ref[:, dr:]
        xpre = u_ref[:, :dr]
        hpad[pl.ds(0, 8), :] = hh_ref[...] * keep
        hpad[pl.ds(8, tb), :] = hs_ref[...]
        xb = xb_ref[...]
        xbb = xb.astype(BF16)
        r, i = _gates(xb, wrg, brg[...], wig, big[...], nblocks)
        sp = _softplus(-lam[...])
        log_a = -LRU_C * r * sp
        a, a2, nem = _decay(log_a)
        mult = jnp.sqrt(nem)
        sg = _sigmoid(gate)
        dy = _dot_nt(dx1v.astype(BF16), wout[...])
        hsv = hs_ref[...]
        dgate = dy * hsv * (sg * (1.0 + gate * (1.0 - sg)))
        a_s[...] = a
        d_s[...] = dy * (gate * sg)
        row = lax.broadcasted_iota(jnp.int32, (8, dr), 0)

        def step(k, c):
            r0 = pl.multiple_of((nt - 1 - k) * 8, 8)
            av = a_s[pl.ds(r0, 8), :]
            dv = d_s[pl.ds(r0, 8), :]
            qv = av * dv
            for s in (1, 2, 4):
                m = row < 8 - s
                a_sh = jnp.where(m, pltpu.roll(av, 8 - s, 0), 1.0)
                q_sh = jnp.where(m, pltpu.roll(qv, 8 - s, 0), 0.0)
                qv = qv + av * q_sh
                av = av * a_sh
            qv = qv + av * c
            g_s[pl.ds(r0, 8), :] = dv + jnp.where(row < 7, pltpu.roll(qv, 7, 0), c)
            return jnp.broadcast_to(qv[0:1, :], qv.shape)

        carry[...] = lax.fori_loop(0, nt, step, carry[...])
        g = g_s[...]
        ix = i * xb
        dlog_a = g * (hpad[pl.ds(7, tb), :] * a - ix * (a2 * lax.rsqrt(nem)))
        dix = g * mult
        dlam_g[...] += -jax.nn.sigmoid(-lam[...]) * jnp.sum(dlog_a * (-LRU_C * r), axis=0, keepdims=True)
        drg = dlog_a * (-LRU_C * sp) * r * (1.0 - r)
        dig = dix * xb * i * (1.0 - i)
        dbrg_g[...] += jnp.sum(drg, axis=0, keepdims=True)
        dbig_g[...] += jnp.sum(dig, axis=0, keepdims=True)
        drgb, digb = drg.astype(BF16), dig.astype(BF16)
        back = []
        for n in range(nblocks):
            cols = slice(n * LANE, (n + 1) * LANE)
            dwrg_g[n] += _dot_tn(xbb[:, cols], drgb[:, cols])
            dwig_g[n] += _dot_tn(xbb[:, cols], digb[:, cols])
            back.append(_dot_nt(drgb[:, cols], wrg[n]) + _dot_nt(digb[:, cols], wig[n]))
        dxb = dix * i + jnp.concatenate(back, axis=1)
        dcb_g[...] += jnp.sum(dxb, axis=0, keepdims=True)
        dxpad[pl.ds(0, tb), :] = dxb
        later = [dxb, dxpad[pl.ds(1, tb), :], dxpad[pl.ds(2, tb), :], dxpad[pl.ds(3, tb), :]]
        dxpad[pl.ds(tb, 8), :] = dxb[:8, :]
        dxpre = cw[3:4, :] * later[0] + cw[2:3, :] * later[1] + cw[1:2, :] * later[2] + cw[0:1, :] * later[3]
        for m in range(4):
            dcw_g[3 - m:4 - m, :] += jnp.sum(later[m] * xpre, axis=0, keepdims=True)
        du = jnp.concatenate([dxpre, dgate], axis=1).astype(BF16)
        du_ref[...] = du
        xv = x_ref[...]
        dxa, g1 = _rms_bwd(xv, _rinv(xv), na[...], _dot(du, wint[...]))
        dna_g[...] += g1
        gx_ref[...] = dx1v + dxa

        @pl.when((b == t_all // seq - 1) & (jj == nblk - 1))
        def _():
            scatter.finish()

    blk = lambda b, j: b * nblk + (nblk - 1 - j)
    tok = lambda c: pl.BlockSpec((tb, c), lambda b, j: (blk(b, j), 0))
    halo = pl.BlockSpec((8, dr), lambda b, j: (jnp.maximum(blk(b, j) * per8 - 1, 0), 0))
    consts = [w["norm_a"], w["conv_w"], w["w_rg"], w["b_rg"], w["w_ig"], w["b_ig"], w["lru_lambda"]]
    vec = lambda c: _full((1, c))
    blocks3 = (nblocks, LANE, LANE)
    return pl.pallas_call(
        body, name="fa_bwd", grid=(t_all // seq, nblk),
        in_specs=[tok(d), tok(d), tok(2 * dr), tok(dr), tok(dr), halo, ANY, ANY, ANY] + [_full(c.shape) for c in consts],
        out_specs=[tok(d), tok(2 * dr), vec(d), _full((4, dr)), vec(dr), vec(dr), vec(dr), vec(dr), _full(blocks3), _full(blocks3), ANY, ANY, ANY],
        out_shape=[_sds((t_all, d)), _sds((t_all, 2 * dr), BF16), _sds((1, d)), _sds((4, dr)), _sds((1, dr)), _sds((1, dr)), _sds((1, dr)),
                   _sds((1, dr)), _sds(blocks3), _sds(blocks3)] + _scatter_direct_shapes(lay, "early"),
        scratch_shapes=[pltpu.VMEM((2 * dr, d), BF16), pltpu.VMEM((dr, d), BF16), pltpu.VMEM((tb + 8, dr), F32),
                        pltpu.VMEM((tb, dr), F32), pltpu.VMEM((tb, dr), F32), pltpu.VMEM((tb, dr), F32), pltpu.VMEM((tb + 8, dr), F32),
                        pltpu.VMEM((8, dr), F32), pltpu.SemaphoreType.DMA((2 * N_CHIPS,))] + SCATTER_DIRECT_SEMS,
        compiler_params=_params(2),
    )(dx1, x, u, xb, hs, hs, wg, g16, g32, *consts)


def _mm_into(gbufs, a, b, off, name, bt):
    t_all, m = a.shape
    n = b.shape[1]
    nsplit = 2 if m >= 1024 and (m // 2) % LANE == 0 else 1
    mh = m // nsplit
    nt = t_all // bt
    nbuf = len(gbufs)
    twin = nbuf == 2

    def body(a_ref, b_ref, *refs):
        outs, acc, sems = refs[nbuf:2 * nbuf], refs[2 * nbuf], refs[-1]
        acc16 = refs[2 * nbuf + 1] if twin else None
        part, t = pl.program_id(0), pl.program_id(1)

        def out_copies(h):
            dst = pl.ds(off + h * mh, mh)
            copies = [pltpu.make_async_copy(acc.at[h], outs[0].at[dst, :], sems.at[0, h])]
            if twin:
                copies.append(pltpu.make_async_copy(acc16.at[h], outs[1].at[dst, :], sems.at[1, h]))
            return copies

        prod = _dot_tn(a_ref[...].astype(BF16), b_ref[...].astype(BF16))
        for h in range(nsplit):
            @pl.when((part == h) & (t == 0))
            def _():
                acc[h] = prod

            @pl.when((part == h) & (t > 0))
            def _():
                acc[h] += prod

            @pl.when((part == h) & (t == nt - 1))
            def _():
                if twin:
                    acc16[h] = acc[h].astype(BF16)
                for cp in out_copies(h):
                    cp.start()

        @pl.when((part == nsplit - 1) & (t == nt - 1))
        def _():
            for h in range(nsplit):
                for cp in out_copies(h):
                    cp.wait()

    scratch = [pltpu.VMEM((nsplit, mh, n), F32)] + ([pltpu.VMEM((nsplit, mh, n), BF16)] if twin else []) + [pltpu.SemaphoreType.DMA((2, nsplit))]
    return pl.pallas_call(
        body, name=name, grid=(nsplit, nt),
        in_specs=[pl.BlockSpec((bt, mh), lambda h, t: (t, h)), pl.BlockSpec((bt, n), lambda h, t: (t, 0))] + [ANY] * nbuf,
        out_specs=[ANY] * nbuf, out_shape=[_sds(g.shape, g.dtype) for g in gbufs], input_output_aliases={2 + k: k for k in range(nbuf)},
        scratch_shapes=scratch, compiler_params=_params(2),
    )(a, b, *gbufs)


def _mm_tn(a, b, name, bt):
    t_all, m = a.shape
    n = b.shape[1]

    def body(a_ref, b_ref, o_ref):
        @pl.when(pl.program_id(0) == 0)
        def _():
            o_ref[...] = jnp.zeros((m, n), F32)

        o_ref[...] += _dot_tn(a_ref[...].astype(BF16), b_ref[...].astype(BF16))

    return pl.pallas_call(
        body, name=name, grid=(t_all // bt,),
        in_specs=[pl.BlockSpec((bt, m), lambda t: (t, 0)), pl.BlockSpec((bt, n), lambda t: (t, 0))],
        out_specs=_full((m, n)), out_shape=_sds((m, n)),
        compiler_params=_params(1),
    )(a, b)


class _Gather8:
    def __init__(self, x_ref, out_ref, send_sems, recv_sems, local_sem):
        x, y, c = _place()
        self.c, self.me, self.sibling = c, (x, y, c), (x, y, 1 - c)
        self.chips = [(1 - x, y), (x, 1 - y), (1 - x, 1 - y)]
        self.x_ref, self.out_ref, self.send_sems, self.recv_sems, self.local_sem = x_ref, out_ref, send_sems, recv_sems, local_sem

    def _slot(self, px, py, pc):
        return self.out_ref.at[4 * px + 2 * py + pc]

    def _copy(self, k, blk, to, src=None):
        return pltpu.make_async_remote_copy(
            src_ref=self._slot(*blk) if src is None else src, dst_ref=self._slot(*blk), send_sem=self.send_sems.at[k],
            recv_sem=self.recv_sems.at[k], device_id=to, device_id_type=MESH)

    def _mine(self):
        return pltpu.make_async_copy(self.x_ref, self._slot(*self.me), self.local_sem)

    def _first(self):
        return [self._copy(0, self.me, self.sibling, src=self.x_ref)] + [
            self._copy(1 + j, self.me, (*chip, self.c), src=self.x_ref) for j, chip in enumerate(self.chips)]

    def _passed(self):
        return [self._copy(4 + j, (*chip, self.c), self.sibling) for j, chip in enumerate(self.chips)]

    def start(self):
        self._mine().start()
        for cp in self._first():
            cp.start()

    def forward(self):
        passed = self._passed()
        for j, chip in enumerate(self.chips):
            self._copy(1 + j, (*chip, self.c), self.me).wait_recv()
            passed[j].start()

    def finish(self):
        self._copy(0, self.sibling, self.me).wait_recv()
        for j, chip in enumerate(self.chips):
            self._copy(4 + j, (*chip, 1 - self.c), self.me).wait_recv()
        for cp in self._first() + self._passed():
            cp.wait_send()
        self._mine().wait()


class _Scatter:
    def __init__(self, p16_ref, p32_ref, got_ref, own_ref, send_sems, recv_sems, local_sem, lay, order):
        self.x, self.y, self.c = _place()
        self.chips = [(1 - self.x, self.y), (self.x, 1 - self.y), (1 - self.x, 1 - self.y)]
        self.refs = (p16_ref, p32_ref, got_ref, own_ref, send_sems, recv_sems, local_sem)
        self.lay, self.order = lay, order

    def _rows_of(self, ref, key, chip):
        start = pl.multiple_of(self.lay.g_off[key] + chip * self.lay.rows[key], ROW_ALIGN)
        return ref.at[pl.ds(start, self.lay.rows[key]), :]

    def _compact(self, ref, key):
        return ref.at[pl.ds(self.lay.c_off[key], self.lay.rows[key]), :]

    def start(self):
        p16_ref, p32_ref, got_ref, own_ref, send_sems, recv_sems, local_sem = self.refs
        for key in self.order:
            pltpu.make_async_copy(self._rows_of(p32_ref, key, 2 * self.x + self.y), self._compact(own_ref, key), local_sem).start()
        for k, (px, py) in enumerate(self.chips):
            for key in self.order:
                pltpu.make_async_remote_copy(
                    src_ref=self._rows_of(p16_ref, key, 2 * px + py), dst_ref=self._compact(got_ref.at[k], key), send_sem=send_sems.at[k],
                    recv_sem=recv_sems.at[k], device_id=(px, py, self.c), device_id_type=MESH).start()

    def finish(self):
        _, _, got_ref, own_ref, send_sems, recv_sems, local_sem = self.refs
        for k, (px, py) in enumerate(self.chips):
            pltpu.make_async_remote_copy(src_ref=got_ref.at[k], dst_ref=got_ref.at[k], send_sem=send_sems.at[k], recv_sem=recv_sems.at[k],
                                         device_id=(px, py, self.c), device_id_type=MESH).wait()
        pltpu.make_async_copy(own_ref, own_ref, local_sem).wait()


class _ScatterDirect:
    def __init__(self, g16_ref, g32_ref, got_ref, sib_ref, own_ref, send_sems, recv_sems, local_sem, lay, order):
        self.x, self.y, self.c = _place()
        self.chips = [(1 - self.x, self.y), (self.x, 1 - self.y), (1 - self.x, 1 - self.y)]
        self.refs = (g16_ref, g32_ref, got_ref, sib_ref, own_ref, send_sems, recv_sems, local_sem)
        self.lay, self.order, self.half = lay, order, lay.d // 2

    def _src(self, ref, key, chip, h):
        start = pl.multiple_of(self.lay.g_off[key] + chip * self.lay.rows[key], ROW_ALIGN)
        return ref.at[pl.ds(start, self.lay.rows[key]), pl.ds(pl.multiple_of(h * self.half, LANE), self.half)]

    def _compact(self, ref, key):
        return ref.at[pl.ds(self.lay.c_off[key], self.lay.rows[key]), :]

    def start(self):
        g16_ref, g32_ref, got_ref, sib_ref, own_ref, send_sems, recv_sems, local_sem = self.refs
        x, y, c = self.x, self.y, self.c
        for key in self.order:
            pltpu.make_async_copy(self._src(g32_ref, key, 2 * x + y, c), self._compact(own_ref, key), local_sem).start()
            pltpu.make_async_remote_copy(
                src_ref=self._src(g32_ref, key, 2 * x + y, 1 - c), dst_ref=self._compact(sib_ref, key), send_sem=send_sems.at[6],
                recv_sem=recv_sems.at[6], device_id=(x, y, 1 - c), device_id_type=MESH).start()
        for k, (px, py) in enumerate(self.chips):
            for h in range(2):
                for key in self.order:
                    pltpu.make_async_remote_copy(
                        src_ref=self._src(g16_ref, key, 2 * px + py, h), dst_ref=self._compact(got_ref.at[2 * k + c], key),
                        send_sem=send_sems.at[2 * k + h], recv_sem=recv_sems.at[2 * k + c], device_id=(px, py, h), device_id_type=MESH).start()

    def finish(self):
        _, _, got_ref, sib_ref, own_ref, send_sems, recv_sems, local_sem = self.refs
        x, y, c = self.x, self.y, self.c
        for k, (px, py) in enumerate(self.chips):
            for h in range(2):
                whole = pltpu.make_async_remote_copy(src_ref=got_ref.at[2 * k + h], dst_ref=got_ref.at[2 * k + h], send_sem=send_sems.at[2 * k + h],
                                                     recv_sem=recv_sems.at[2 * k + h], device_id=(px, py, h), device_id_type=MESH)
                whole.wait_send()
                whole.wait_recv()
        pltpu.make_async_remote_copy(src_ref=sib_ref, dst_ref=sib_ref, send_sem=send_sems.at[6], recv_sem=recv_sems.at[6],
                                     device_id=(x, y, 1 - c), device_id_type=MESH).wait()
        pltpu.make_async_copy(own_ref, own_ref, local_sem).wait()


def _scatter_direct_shapes(lay, group):
    rows, half = lay.c_rows[group], lay.d // 2
    return [_sds((6, rows, half), BF16), _sds((rows, half), F32), _sds((rows, half), F32)]


SCATTER_DIRECT_SEMS = [pltpu.SemaphoreType.DMA((7,)), pltpu.SemaphoreType.DMA((7,)), pltpu.SemaphoreType.DMA]
SCATTER_SEMS = [pltpu.SemaphoreType.DMA((3,)), pltpu.SemaphoreType.DMA((3,)), pltpu.SemaphoreType.DMA]
GATHER_SEMS = [pltpu.SemaphoreType.DMA((7,)), pltpu.SemaphoreType.DMA((7,)), pltpu.SemaphoreType.DMA]


def _all_gather8(blocks, name):
    nb = len(blocks)

    def body(*refs):
        x_refs, out_refs = refs[:nb], refs[nb:2 * nb]
        send_sems, recv_sems, local_sems = refs[2 * nb:]
        gathers = [_Gather8(x_refs[n], out_refs[n], send_sems.at[n], recv_sems.at[n], local_sems.at[n]) for n in range(nb)]
        for g in gathers:
            g.start()
        for g in gathers:
            g.forward()
        for g in gathers:
            g.finish()

    return pl.pallas_call(
        body, name=name, out_shape=[_sds((8,) + b.shape, b.dtype) for b in blocks], in_specs=[ANY] * nb, out_specs=[ANY] * nb,
        scratch_shapes=[pltpu.SemaphoreType.DMA((nb, 7)), pltpu.SemaphoreType.DMA((nb, 7)), pltpu.SemaphoreType.DMA((nb,))],
    )(*blocks)


def _swap_sibling(srcs, name, half_cols=False):
    n = len(srcs)
    halves = [s.shape[1] // 2 if half_cols else s.shape[1] for s in srcs]

    def body(*refs):
        src_refs, out_refs, send_sems, recv_sems = refs[:n], refs[n:2 * n], refs[2 * n], refs[2 * n + 1]
        x, y, c = _place()
        copies = []
        for k in range(n):
            part = src_refs[k].at[:, pl.ds(pl.multiple_of((1 - c) * halves[k], LANE), halves[k])] if half_cols else src_refs[k]
            copies.append(pltpu.make_async_remote_copy(src_ref=part, dst_ref=out_refs[k], send_sem=send_sems.at[k], recv_sem=recv_sems.at[k],
                                                       device_id=(x, y, 1 - c), device_id_type=MESH))
        for cp in copies:
            cp.start()
        for cp in copies:
            cp.wait()

    return pl.pallas_call(
        body, name=name, out_shape=[_sds((s.shape[0], h), s.dtype) for s, h in zip(srcs, halves)], in_specs=[ANY] * n, out_specs=[ANY] * n,
        scratch_shapes=[pltpu.SemaphoreType.DMA((n,)), pltpu.SemaphoreType.DMA((n,))],
    )(*srcs)


def _return_and_gather(mines, rep_block):
    n = len(mines)

    def body(*refs):
        src_refs, rep_ref, out_refs, rep_out = refs[:n], refs[n], refs[n + 1:2 * n + 1], refs[2 * n + 1]
        send_sems, recv_sems, g_send, g_recv, g_local = refs[2 * n + 2:]
        x, y, c = _place()
        copies = [pltpu.make_async_remote_copy(src_ref=src_refs[k], dst_ref=out_refs[k], send_sem=send_sems.at[k], recv_sem=recv_sems.at[k],
                                               device_id=(x, y, 1 - c), device_id_type=MESH) for k in range(n)]
        gather = _Gather8(rep_ref, rep_out, g_send, g_recv, g_local)
        for cp in copies:
            cp.start()
        gather.start()
        gather.forward()
        gather.finish()
        for cp in copies:
            cp.wait()

    return pl.pallas_call(
        body, name="rs_return", out_shape=[_sds(m.shape, m.dtype) for m in mines] + [_sds((8,) + rep_block.shape, rep_block.dtype)],
        in_specs=[ANY] * (n + 1), out_specs=[ANY] * (n + 1),
        scratch_shapes=[pltpu.SemaphoreType.DMA((n,)), pltpu.SemaphoreType.DMA((n,))] + GATHER_SEMS,
    )(*mines, rep_block)


def _scatter_shapes(lay, group, half):
    return [_sds((3, lay.c_rows[group], half), BF16), _sds((lay.c_rows[group], half), F32)]


def _scatter_chips(part16, part32, lay, group, name):
    def body(p16_ref, p32_ref, got_ref, own_ref, send_sems, recv_sems, local_sem):
        sc = _Scatter(p16_ref, p32_ref, got_ref, own_ref, send_sems, recv_sems, local_sem, lay, G_GROUPS[group])
        sc.start()
        sc.finish()

    return pl.pallas_call(
        body, name=name, out_shape=_scatter_shapes(lay, group, part16.shape[1]), in_specs=[ANY, ANY], out_specs=[ANY, ANY],
        scratch_shapes=SCATTER_SEMS,
    )(part16, part32)


def _sum_sibling(gbuf, got, cidx, name):
    rows, d = gbuf.shape
    half = d // 2
    rb = _row_block(rows)

    def body(c_ref, g_ref, r_ref, o32_ref, o16_ref):
        del c_ref
        s = g_ref[...] + r_ref[...]
        o32_ref[...] = s
        o16_ref[...] = s.astype(BF16)

    plain = pl.BlockSpec((rb, half), lambda i, c: (i, 0))
    return pl.pallas_call(
        body, name=name,
        grid_spec=pltpu.PrefetchScalarGridSpec(num_scalar_prefetch=1, grid=(rows // rb,),
                                               in_specs=[pl.BlockSpec((rb, half), lambda i, c: (i, c[0])), plain], out_specs=[plain, plain]),
        out_shape=[_sds((rows, half)), _sds((rows, half), BF16)], compiler_params=_params(1),
    )(cidx, gbuf, got)


def _sum_devices(own, sib, got, name):
    rows, half = own.shape
    rb = _row_block(rows)
    n = got.shape[0]

    def body(a_ref, s_ref, b_ref, o_ref):
        acc = a_ref[...] + s_ref[...]
        for k in range(n):
            acc = acc + b_ref[k].astype(F32)
        o_ref[...] = acc

    spec = pl.BlockSpec((rb, half), lambda i: (i, 0))
    return pl.pallas_call(
        body, name=name, grid=(rows // rb,), in_specs=[spec, spec, pl.BlockSpec((n, rb, half), lambda i: (0, i, 0))], out_specs=spec,
        out_shape=_sds((rows, half)), compiler_params=_params(1),
    )(own, sib, got)


def _sum_chips(own, got, name):
    rows, half = own.shape
    rb = _row_block(rows)

    def body(a_ref, b_ref, o_ref):
        o_ref[...] = ((a_ref[...] + b_ref[0].astype(F32)) + b_ref[1].astype(F32)) + b_ref[2].astype(F32)

    spec = pl.BlockSpec((rb, half), lambda i: (i, 0))
    return pl.pallas_call(
        body, name=name, grid=(rows // rb,), in_specs=[spec, pl.BlockSpec((3, rb, half), lambda i: (0, i, 0))], out_specs=spec,
        out_shape=_sds((rows, half)), compiler_params=_params(1),
    )(own, got)


def _adamw(w, g, m, v):
    m = ADAM_B1 * m + (1.0 - ADAM_B1) * g
    v = ADAM_B2 * v + (1.0 - ADAM_B2) * (g * g)
    m_hat = m / (1.0 - ADAM_B1 ** ADAM_STEP)
    v_hat = v / (1.0 - ADAM_B2 ** ADAM_STEP)
    return -ADAM_LR * (m_hat / (jnp.sqrt(v_hat) + ADAM_EPS) + ADAM_WD * w), m, v


def _adamw_rows(name, w, g, m, v):
    _, rows, cols = w.shape
    rb = _row_block(rows, 256)

    def body(w_ref, g_ref, m_ref, v_ref, d_ref, mo_ref, vo_ref):
        d_ref[...], mo_ref[...], vo_ref[...] = _adamw(w_ref[...], g_ref[...], m_ref[...], v_ref[...])

    spec = pl.BlockSpec((1, rb, cols), lambda i: (0, i, 0))
    return pl.pallas_call(
        body, name=name, grid=(rows // rb,), in_specs=[spec] * 4, out_specs=[spec] * 3, out_shape=[_sds(w.shape)] * 3,
        compiler_params=_params(1),
    )(w, g, m, v)


def _adamw_group(ws, gs, ms, vs):
    n = len(ws)

    def body(*refs):
        for k in range(n):
            w_ref, g_ref, m_ref, v_ref = (refs[j * n + k] for j in range(4))
            outs = _adamw(w_ref[...], g_ref[...], m_ref[...], v_ref[...])
            for j in range(3):
                refs[(4 + j) * n + k][...] = outs[j]

    outs = pl.pallas_call(
        body, name="adamw_small", out_shape=[_sds(w.shape) for w in ws] * 3,
        compiler_params=pltpu.CompilerParams(vmem_limit_bytes=VMEM_LIMIT),
    )(*ws, *gs, *ms, *vs)
    return outs[:n], outs[n:2 * n], outs[2 * n:]


def _gather_weights(sh, lay):
    x, y, c = _place()
    d = lay.d
    uq = sh["w_uq"][0].astype(BF16)
    parts = {
        "in_b": sh["w_in_b"][0].T.astype(BF16), "in_a": sh["w_in_a"][0].T.astype(BF16), "out_a": sh["w_out_a"][0].astype(BF16),
        "out_b": sh["w_out_b"][0].astype(BF16), "uk": sh["w_uk"].astype(BF16).reshape(-1, d), "uv": sh["w_uv"].astype(BF16).reshape(-1, d),
        "uq_n": uq[:, :, :QK_NOPE].reshape(-1, d), "uq_r": jnp.pad(uq[:, :, QK_NOPE:], ((0, 0), (0, 0), (0, LANE - QK_ROPE))).reshape(-1, d),
        "dkv": jnp.pad(sh["w_dkv"].astype(BF16), ((0, 0), (0, LANE - QK_ROPE))).reshape(-1, d),
    }
    halves = {}
    for group, order in W_GROUPS.items():
        stack = jnp.concatenate([parts[k] for k in order], axis=0).reshape(2, lay.w_rows[group] // 2, d)
        halves[group] = lax.dynamic_index_in_dim(stack, c, 0, keepdims=False)
    small = jnp.concatenate([sh[k].reshape(-1) for k in SMALL])
    n_small = small.shape[0]
    width = _round_up(n_small, 2 * SUBLANE * LANE) // (2 * SUBLANE)
    small = jnp.pad(small, (0, 2 * SUBLANE * width - n_small)).reshape(2, SUBLANE, width)
    wg, sg = _all_gather8([halves["a"], lax.dynamic_index_in_dim(small, c, 0, keepdims=False)], "ag_weights")
    wg = wg.reshape(N_CHIPS, lay.w_rows["a"], d)
    sg = sg.reshape(N_CHIPS, 2 * SUBLANE * width)
    full, off = {}, 0
    for k in SMALL:
        n = sh[k].size
        piece = sg[:, off:off + n]
        off += n
        if k == "conv_w":
            full[k] = piece.reshape(N_CHIPS, 4, n // 4).transpose(1, 0, 2).reshape(4, n)
        else:
            full[k] = piece.reshape(1, N_CHIPS * n)
    return wg, halves["b"], full


def _chip_split(g, taps=False):
    if taps:
        n = g.shape[1] // N_CHIPS
        return g.reshape(4, N_CHIPS, n).transpose(1, 0, 2).reshape(N_CHIPS, 4 * n)
    return g.reshape(N_CHIPS, -1)


def kernel(x, norm_a, w_in_a, conv_w, conv_b, w_rg, b_rg, w_ig, b_ig, lru_lambda, w_out_a, norm_kv, w_dkv, kv_norm, w_uk, w_uv, norm_b, w_in_b, q_norm, w_uq, w_out_b, final_norm, loss_target, m_norm_a, m_w_in_a, m_conv_w, m_conv_b, m_w_rg, m_b_rg, m_w_ig, m_b_ig, m_lru_lambda, m_w_out_a, m_norm_kv, m_w_dkv, m_kv_norm, m_w_uk, m_w_uv, m_norm_b, m_w_in_b, m_q_norm, m_w_uq, m_w_out_b, m_final_norm, v_norm_a, v_w_in_a, v_conv_w, v_conv_b, v_w_rg, v_b_rg, v_w_ig, v_b_ig, v_lru_lambda, v_w_out_a, v_norm_kv, v_w_dkv, v_kv_norm, v_w_uk, v_w_uv, v_norm_b, v_w_in_b, v_q_norm, v_w_uq, v_w_out_b, v_final_norm):
    given = dict(locals())
    sh = {k: given[k] for k in WEIGHTS}
    xi, yi, ci = _place()
    nb, seq, d = x.shape
    t_all = nb * seq
    tb_a, tb_b, ta, bt = min(TOKENS_A, seq), min(TOKENS_B, seq), min(TOKENS_ATTN, seq), min(TOKENS_MM, t_all)
    dr = conv_b.shape[1] * N_CHIPS
    qr, kvr, nheads = q_norm.shape[1], kv_norm.shape[0], w_uk.shape[1]
    hv = nheads * LANE
    n_small = sum(sh[k].size for k in SMALL)
    n_repl = sum(sh[k].size for k in REPL)
    lay = _Layout(d, dr, qr, kvr, hv, n_small, n_repl)
    half = d // 2

    wga, wb_half, w = _gather_weights(sh, lay)
    w.update({"w_rg": w_rg[0].astype(BF16), "w_ig": w_ig[0].astype(BF16), "norm_kv": norm_kv[None, :],
              "kv_norm": kv_norm[None, :], "final_norm": final_norm[None, :], "norm_b": norm_b, "q_norm": q_norm})
    cos_t, sin_t = _rope_tables(seq)
    cidx = jnp.reshape(ci, (1,)).astype(jnp.int32)

    x0 = x.reshape(t_all, d)
    x1, u, hs, h, y, xb, wgb = _fa_fwd(x0, wga, wb_half, w, lay, seq, tb_a)
    wgb = wgb.reshape(N_CHIPS, lay.w_rows["b"], d)
    w["w_dkv_p"] = wgb[:, lay.w_off["dkv"]:lay.w_off["dkv"] + lay.rows["dkv"], :].reshape(d, kvr + LANE)
    qn, qrp, kn, kr, v, ub, ckr, hb, hk, cq, ckv = _fb_fwd(x1, wgb, w, lay, cos_t, sin_t, seq, tb_b)
    o, lse = _attn_fwd(qn, qrp, kn, kr, v, seq, ta)
    loss, g_final_norm, yb, dx2, do, dgate, delta = _head(o, ub, x1, loss_target.reshape(t_all, d), wgb, w, lay, tb_b)
    dqn, dqr, dkn, dkr, dv = _attn_bwd(qn, qrp, kn, kr, v, do, lse, delta, seq, ta)
    dx1, dqr_pre, dqn_pre, dub, dckr, g_q_norm, g_norm_b, g_kv_norm, g_norm_kv = _fb_bwd(
        dqn, dqr, dkn, dkr, dv, dgate, ub, ckr, x1, dx2, wgb, w, lay, cos_t, sin_t, seq, tb_b)
    loss = lax.psum(loss[0, 0], ("x", "y", "c"))

    gbufs = [lax.empty((lay.g_rows["early"], d), F32), lax.empty((lay.g_rows["early"], d), BF16)]
    for key, a, b in (("in_b", dub, hb), ("out_a", y, dx1), ("out_b", yb, dx2), ("uk", ckv, dkn), ("uv", ckv, dv), ("uq_n", cq, dqn_pre),
                      ("uq_r", cq, dqr_pre)):
        gbufs = _mm_into(gbufs, a, b, lay.g_off[key], "dw_" + key, bt)
    g_dkv = _mm_tn(hk, dckr, "dw_dkv", bt)
    gx, du, g_norm_a, g_conv_w, g_conv_b, g_b_rg, g_b_ig, g_lam, g_w_rg, g_w_ig, others, sib, own = _fa_bwd(
        dx1, x0, u, xb, hs, wga, gbufs[1], gbufs[0], w, lay, seq, tb_a)
    mine_early = _sum_devices(own, sib, others, "rs_sum_early")

    small = jnp.concatenate([_chip_split(g_norm_a), _chip_split(g_conv_w, taps=True), _chip_split(g_conv_b), _chip_split(g_b_rg),
                             _chip_split(g_b_ig), _chip_split(g_lam)], axis=1)
    small = jnp.pad(small, ((0, 0), (0, lay.small_rows * d - small.shape[1]))).reshape(N_CHIPS, lay.small_rows, d)
    repl_parts = {"w_rg": g_w_rg, "w_ig": g_w_ig, "norm_kv": g_norm_kv, "kv_norm": g_kv_norm, "norm_b": g_norm_b, "q_norm": g_q_norm,
                  "final_norm": g_final_norm}
    repl = jnp.concatenate([repl_parts[k].reshape(-1) for k in REPL])
    repl = jnp.pad(repl, (0, N_CHIPS * lay.repl_rows * d - n_repl)).reshape(N_CHIPS, lay.repl_rows, d)
    pad_rows = lay.rows["rest"] - lay.rows["dkv"] - lay.small_rows - lay.repl_rows
    rest = jnp.concatenate([g_dkv.reshape(N_CHIPS, lay.rows["dkv"], d), small, repl, jnp.zeros((N_CHIPS, pad_rows, d), F32)], axis=1)
    gbuf = lax.dynamic_update_slice(lax.empty((lay.g_rows["late"], d), F32), rest.reshape(N_CHIPS * lay.rows["rest"], d), (lay.g_off["rest"], 0))
    (gbuf,) = _mm_into([gbuf], du, h, lay.g_off["in_a"], "dw_in_a", bt)
    (got,) = _swap_sibling([gbuf], "rs_sibling_late", half_cols=True)
    part32, part16 = _sum_sibling(gbuf, got, cidx, "rs_sum_sibling_late")
    others, own = _scatter_chips(part16, part32, lay, "late", "rs_chips_late")
    mine_late = _sum_chips(own, others, "rs_sum_chips_late")

    r0 = lay.c_off["rest"] + lay.rows["dkv"] + lay.small_rows
    theirs_early, theirs_late, rep_all = _return_and_gather([mine_early, mine_late], mine_late[r0:r0 + lay.repl_rows])
    red = {}
    for group, mine, theirs in (("early", mine_early, theirs_early), ("late", mine_late, theirs_late)):
        red[group] = jnp.concatenate([jnp.where(ci == 0, mine, theirs), jnp.where(ci == 0, theirs, mine)], axis=1)
    rep_flat =rep_all.reshape(N_CHIPS, 2, lay.repl_rows, half).transpose(0, 2, 1, 3).reshape(-1)

    def rows(key):
        group = "late" if key in G_GROUPS["late"] else "early"
        return red[group][lay.c_off[key]:lay.c_off[key] + lay.rows[key]]

    grads = {"w_in_b": rows("in_b").T[None], "w_in_a": rows("in_a").T[None], "w_out_a": rows("out_a")[None], "w_out_b": rows("out_b")[None],
             "w_uk": rows("uk").reshape(w_uk.shape), "w_uv": rows("uv").reshape(w_uv.shape)}
    uq_n = rows("uq_n").reshape(qr // N_CHIPS, nheads, LANE)
    uq_r = rows("uq_r").reshape(qr // N_CHIPS, nheads, LANE)[:, :, :QK_ROPE]
    grads["w_uq"] = jnp.concatenate([uq_n, uq_r], axis=2)[None]
    rest_red = rows("rest")
    grads["w_dkv"] = rest_red[:lay.rows["dkv"]].reshape(d // N_CHIPS, kvr + LANE)[:, :kvr + QK_ROPE]
    small_red = rest_red[lay.rows["dkv"]:lay.rows["dkv"] + lay.small_rows].reshape(-1)
    off = 0
    for k in SMALL:
        n = sh[k].size
        grads[k] = small_red[off:off + n].reshape(sh[k].shape)
        off += n
    off = 0
    for k in REPL:
        n = sh[k].size
        grads[k] = rep_flat[off:off + n].reshape(sh[k].shape)
        off += n

    new = {}
    for k in ("w_in_a", "w_in_b", "w_out_a", "w_out_b"):
        view = (lambda a: jnp.swapaxes(a, 1, 2)) if k in TRANSPOSED else (lambda a: a)
        outs = _adamw_rows("adamw_" + k, view(sh[k]), view(grads[k]), view(given["m_" + k]), view(given["v_" + k]))
        new[k] = tuple(view(a) for a in outs)
    rest_names = [k for k in WEIGHTS if k not in new]

    def as2d(k, a):
        return a.T if k in TRANSPOSED else a[None, :] if a.ndim == 1 else a

    ds, ms, vs = _adamw_group([as2d(k, sh[k]) for k in rest_names], [as2d(k, grads[k]) for k in rest_names],
                              [as2d(k, given["m_" + k]) for k in rest_names], [as2d(k, given["v_" + k]) for k in rest_names])
    for n, k in enumerate(rest_names):
        new[k] = tuple((a.T if k in TRANSPOSED else a).reshape(sh[k].shape) for a in (ds[n], ms[n], vs[n]))
    return (loss, gx.reshape(nb, seq, d), *[grads[k] for k in WEIGHTS], *[new[k][0] for k in WEIGHTS], *[new[k][1] for k in WEIGHTS],
            *[new[k][2] for k in WEIGHTS])
```

```python
import jax
import jax.numpy as jnp
from jax import lax
from jax.experimental import pallas as pl
from jax.experimental.pallas import tpu as pltpu

F32, BF16 = jnp.float32, jnp.bfloat16
EPS = 1e-6
LRU_C = 8.0
ROPE_THETA = 10000.0
QK_NOPE, QK_ROPE = 128, 64
ATTN_SCALE = (QK_NOPE + QK_ROPE) ** -0.5
LN2 = 0.6931471805599453
Q_SCALE = ATTN_SCALE / LN2
ATTN_HEADS, ATTN_HEADS_BWD = 4, 2
ATTN_ROWS = 64
LANE = 128
SUBLANE = 8
ROW_ALIGN = 32
VMEM_LIMIT = 60000 * 1024
ADAM_LR, ADAM_B1, ADAM_B2, ADAM_EPS, ADAM_WD, ADAM_STEP = 0.001, 0.9, 0.999, 1e-08, 0.01, 10
MESH = pl.DeviceIdType.MESH
ANY = pl.BlockSpec(memory_space=pl.ANY)
N_CHIPS = 4
TOKENS_A, TOKENS_B, TOKENS_ATTN, TOKENS_MM = 256, 512, 512, 2048

SMALL = ("norm_a", "conv_w", "conv_b", "b_rg", "b_ig", "lru_lambda")
REPL = ("w_rg", "w_ig", "norm_kv", "kv_norm", "norm_b", "q_norm", "final_norm")
TRANSPOSED = ("w_in_b", "w_dkv")
WEIGHTS = ("norm_a", "w_in_a", "conv_w", "conv_b", "w_rg", "b_rg", "w_ig", "b_ig", "lru_lambda", "w_out_a", "norm_kv",
           "w_dkv", "kv_norm", "w_uk", "w_uv", "norm_b", "w_in_b", "q_norm", "w_uq", "w_out_b", "final_norm")
W_GROUPS = {"a": ("in_a", "out_a"), "b": ("in_b", "out_b", "uk", "uv", "uq_n", "uq_r", "dkv")}
G_GROUPS = {"early": ("in_b", "out_a", "out_b", "uk", "uv", "uq_n", "uq_r"), "late": ("in_a", "rest")}


def _sds(shape, dtype=F32):
    return jax.ShapeDtypeStruct(tuple(shape), dtype)


def _params(n_grid):
    return pltpu.CompilerParams(dimension_semantics=("arbitrary",) * n_grid, vmem_limit_bytes=VMEM_LIMIT)


def _full(shape):
    nd = len(shape)
    return pl.BlockSpec(tuple(shape), lambda *g: (0,) * nd)


def _round_up(n, k):
    return -(-n // k) * k


def _row_block(rows, cap=512):
    best = SUBLANE
    for r in range(SUBLANE, min(rows, cap) + 1, SUBLANE):
        if rows % r == 0:
            best = r
    return best


def _place():
    return lax.axis_index("x"), lax.axis_index("y"), lax.axis_index("c")


class _Layout:
    def __init__(self, d, dr, qr, kvr, hv, n_small, n_repl):
        assert hv == d, "the packed rows are D_MODEL wide, which must equal heads * 128"
        self.d, self.dr, self.qr, self.kvr, self.hv = d, dr, qr, kvr, hv
        per_chip = {"in_b": (qr + hv) // N_CHIPS, "in_a": 2 * dr // N_CHIPS, "out_a": dr // N_CHIPS, "out_b": hv // N_CHIPS,
                    "uk": kvr // N_CHIPS, "uv": kvr // N_CHIPS, "uq_n": qr // N_CHIPS, "uq_r": qr // N_CHIPS,
                    "dkv": (d // N_CHIPS) * (kvr + LANE) // d}
        assert all(r % ROW_ALIGN == 0 for r in per_chip.values()), per_chip
        self.small_rows = _round_up(-(-n_small // d), SUBLANE)
        self.repl_rows = _round_up(-(-n_repl // (N_CHIPS * d)), SUBLANE)
        per_chip["rest"] = _round_up(per_chip["dkv"] + self.small_rows + self.repl_rows, ROW_ALIGN)
        self.rows = per_chip
        self.w_off, self.w_rows = {}, {}
        for group, order in W_GROUPS.items():
            off = 0
            for k in order:
                self.w_off[k] = off
                off += per_chip[k]
            assert off % ROW_ALIGN == 0, (group, off)
            self.w_rows[group] = off
        self.g_off, self.c_off, self.c_rows, self.g_rows = {}, {}, {}, {}
        for group, order in G_GROUPS.items():
            off = 0
            for k in order:
                self.c_off[k] = off
                self.g_off[k] = N_CHIPS * off
                off += per_chip[k]
            self.c_rows[group] = off
            self.g_rows[group] = N_CHIPS * off


def _dot(a, b):
    return jnp.dot(a, b, preferred_element_type=F32)


def _dot_nt(a, b):
    return lax.dot_general(a, b, (((1,), (1,)), ((), ())), preferred_element_type=F32)


def _dot_tn(a, b):
    return lax.dot_general(a, b, (((0,), (0,)), ((), ())), preferred_element_type=F32)


def _rinv(x):
    return lax.rsqrt(jnp.mean(x * x, axis=-1, keepdims=True) + EPS)


def _rms_bwd(x, rinv, g, dy):
    z = dy * g
    dx = rinv * z - x * (rinv * rinv * rinv) * jnp.mean(z * x, axis=-1, keepdims=True)
    dg = jnp.sum(dy * (x * rinv), axis=0, keepdims=True)
    return dx, dg


def _softplus(z):
    return jnp.maximum(z, 0.0) + jnp.log1p(jnp.exp(-jnp.abs(z)))


def _sigmoid(x):
    return 0.5 * jnp.tanh(0.5 * x) + 0.5


def _decay(log_a):
    a = jnp.exp(log_a)
    a2 = a * a
    return a, a2, -jnp.tanh(log_a) * (a2 + 1.0)


def _swap_halves(x):
    w = x.shape[1]
    lane = lax.broadcasted_iota(jnp.int32, x.shape, 1)
    return jnp.where(lane % QK_ROPE < QK_ROPE // 2, pltpu.roll(x, w - QK_ROPE // 2, 1), pltpu.roll(x, QK_ROPE // 2, 1))


def _rope_tables(seq):
    pos = jnp.arange(seq, dtype=F32)
    inv = ROPE_THETA ** (-jnp.arange(0, QK_ROPE, 2, dtype=F32) / QK_ROPE)
    ang = pos[:, None] * inv[None, :]
    cos, sin = jnp.cos(ang), jnp.sin(ang)
    zero = jnp.zeros((seq, LANE - QK_ROPE), F32)
    return jnp.concatenate([cos, cos, zero], 1), jnp.concatenate([-sin, sin, zero], 1)


def _fetch(wg_ref, lay, key, dst, sems, k0):
    rows = lay.rows[key]
    return [pltpu.make_async_copy(wg_ref.at[p, pl.ds(lay.w_off[key], rows), :], dst.at[pl.ds(p * rows, rows), :], sems.at[k0 + p])
            for p in range(N_CHIPS)]


def _gates(xb, wrg_ref, brg, wig_ref, big, nblocks):
    xbb = xb.astype(BF16)
    rg = [_dot(xbb[:, n * LANE:(n + 1) * LANE], wrg_ref[n]) for n in range(nblocks)]
    ig = [_dot(xbb[:, n * LANE:(n + 1) * LANE], wig_ref[n]) for n in range(nblocks)]
    r = _sigmoid(jnp.concatenate(rg, axis=1) + brg)
    i = _sigmoid(jnp.concatenate(ig, axis=1) + big)
    return r, i


def _conv(xpad, cw_ref, cb, tb):
    return (cb + cw_ref[3:4, :] * xpad[pl.ds(8, tb), :] + cw_ref[2:3, :] * xpad[pl.ds(7, tb), :]
            + cw_ref[1:2, :] * xpad[pl.ds(6, tb), :] + cw_ref[0:1, :] * xpad[pl.ds(5, tb), :])


def _fa_fwd(x, wg, wb_half, w, lay, seq, tb):
    t_all, d = x.shape
    dr = lay.dr
    nblocks = w["w_rg"].shape[0]
    nblk = seq // tb
    nt = tb // SUBLANE
    nsteps = (t_all // seq) * nblk

    def body(x_ref, wg_ref, wbh_ref, na, cw, cb, wrg, brg, wig, big, lam, x1_ref, u_ref, hs_ref, h_ref, y_ref, xb_ref, wb_ref,
             wint, wout, xpad, a_s, b_s, carry, sems, send_sems, recv_sems, local_sem):
        step_no = pl.program_id(0) * nblk + pl.program_id(1)
        gather = _Gather8(wbh_ref, wb_ref, send_sems, recv_sems, local_sem)

        @pl.when(step_no == 0)
        def _():
            gather.start()
            cps = _fetch(wg_ref, lay, "in_a", wint, sems, 0) + _fetch(wg_ref, lay, "out_a", wout, sems, N_CHIPS)
            for cp in cps:
                cp.start()
            for cp in cps:
                cp.wait()

        @pl.when(step_no == nsteps // 2)
        def _():
            gather.forward()

        @pl.when(pl.program_id(1) == 0)
        def _():
            xpad[pl.ds(0, 8), :] = jnp.zeros((8, dr), F32)
            carry[...] = jnp.zeros((8, dr), F32)

        xv = x_ref[...]
        h = (xv * _rinv(xv) * na[...]).astype(BF16)
        h_ref[...] = h
        u = _dot_nt(h, wint[...])
        u_ref[...] = u
        xpre, gate = u[:, :dr], u[:, dr:]
        xpad[pl.ds(8, tb), :] = xpre
        xb = _conv(xpad, cw, cb[...], tb)
        xb_ref[...] = xb
        xpad[pl.ds(0, 8), :] = xpre[tb - 8:, :]
        r, i = _gates(xb, wrg, brg[...], wig, big[...], nblocks)
        log_a = -LRU_C * r * _softplus(-lam[...])
        a, _, nem = _decay(log_a)
        a_s[...] = a
        b_s[...] = jnp.sqrt(nem) * (i * xb)
        row = lax.broadcasted_iota(jnp.int32, (8, dr), 0)

        def step(t, c):
            r0 = pl.multiple_of(t * 8, 8)
            a = a_s[pl.ds(r0, 8), :]
            b = b_s[pl.ds(r0, 8), :]
            for s in (1, 2, 4):
                m = row >= s
                a_sh = jnp.where(m, pltpu.roll(a, s, 0), 1.0)
                b_sh = jnp.where(m, pltpu.roll(b, s, 0), 0.0)
                b = a * b_sh + b
                a = a * a_sh
            hh = b + a * c
            hs_ref[pl.ds(r0, 8), :] = hh
            return jnp.broadcast_to(hh[7:8, :], hh.shape)

        carry[...] = lax.fori_loop(0, nt, step, carry[...])
        y = (hs_ref[...] * (gate * _sigmoid(gate))).astype(BF16)
        y_ref[...] = y
        x1_ref[...] = xv + _dot(y, wout[...])

        @pl.when(step_no == nsteps - 1)
        def _():
            gather.finish()

    tok = lambda c: pl.BlockSpec((tb, c), lambda b, j: (b * nblk + j, 0))
    consts = [w["norm_a"], w["conv_w"], w["conv_b"], w["w_rg"], w["b_rg"], w["w_ig"], w["b_ig"], w["lru_lambda"]]
    return pl.pallas_call(
        body, name="fa_fwd", grid=(t_all // seq, nblk),
        in_specs=[tok(d), ANY, ANY] + [_full(c.shape) for c in consts],
        out_specs=[tok(d), tok(2 * dr), tok(dr), tok(d), tok(dr), tok(dr), ANY],
        out_shape=[_sds((t_all, d)), _sds((t_all, 2 * dr)), _sds((t_all, dr)), _sds((t_all, d), BF16), _sds((t_all, dr), BF16),
                   _sds((t_all, dr)), _sds((8,) + wb_half.shape, BF16)],
        scratch_shapes=[pltpu.VMEM((2 * dr, d), BF16), pltpu.VMEM((dr, d), BF16), pltpu.VMEM((tb + 8, dr), F32), pltpu.VMEM((tb, dr), F32),
                        pltpu.VMEM((tb, dr), F32), pltpu.VMEM((8, dr), F32), pltpu.SemaphoreType.DMA((2 * N_CHIPS,))] + GATHER_SEMS,
        compiler_params=_params(2),
    )(x, wg, wb_half, *consts)


def _fb_fwd(x1, wg, w, lay, cos_t, sin_t, seq, tb):
    t_all, d = x1.shape
    kvr, qr, hv = lay.kvr, lay.qr, lay.hv
    nheads = hv // LANE
    npos = seq // tb

    def body(x_ref, wg_ref, nkv, nb, wdkv, kvn, qn, cos_ref, sin_ref,
             qn_o, qr_o, kn_o, kr_o, v_o, ub_o, ckr_o, hb_o, hk_o, cq_o, ckv_o, winb, wuk, wuv, wuqn, wuqr, sems):
        @pl.when(pl.program_id(0) == 0)
        def _():
            cps = []
            for n, (key, dst) in enumerate((("in_b", winb), ("uk", wuk), ("uv", wuv), ("uq_n", wuqn), ("uq_r", wuqr))):
                cps += _fetch(wg_ref, lay, key, dst, sems, n * N_CHIPS)
            for cp in cps:
                cp.start()
            for cp in cps:
                cp.wait()

        xv = x_ref[...]
        xh = xv * _rinv(xv)
        hk = (xh * nkv[...]).astype(BF16)
        hb = (xh * nb[...]).astype(BF16)
        hk_o[...] = hk
        hb_o[...] = hb
        cos, sin = cos_ref[...], sin_ref[...]
        ckr = _dot(hk, wdkv[...])
        ckr_o[...] = ckr
        ckv_pre = ckr[:, :kvr]
        ckv = (ckv_pre * _rinv(ckv_pre) * kvn[...]).astype(BF16)
        ckv_o[...] = ckv
        kr = ckr[:, kvr:]
        kr_o[...] = (kr * cos + _swap_halves(kr) * sin).astype(BF16)
        kn_o[...] = _dot(ckv, wuk[...]).astype(BF16)
        v_o[...] = _dot(ckv, wuv[...]).astype(BF16)
        ub = _dot_nt(hb, winb[...])
        ub_o[...] = ub
        cq_pre = ub[:, :qr]
        cq = (cq_pre * _rinv(cq_pre) * qn[...]).astype(BF16)
        cq_o[...] = cq
        qn_o[...] = (_dot(cq, wuqn[...]) * Q_SCALE).astype(BF16)
        qrope = _dot(cq, wuqr[...]) * Q_SCALE
        qr_o[...] = (qrope * jnp.tile(cos, (1, nheads)) + _swap_halves(qrope) * jnp.tile(sin, (1, nheads))).astype(BF16)

    tok = lambda c: pl.BlockSpec((tb, c), lambda i: (i, 0))
    pos = pl.BlockSpec((tb, LANE), lambda i: (i % npos, 0))
    consts = [w["norm_kv"], w["norm_b"], w["w_dkv_p"], w["kv_norm"], w["q_norm"]]
    outs = [(hv, BF16), (hv, BF16), (hv, BF16), (LANE, BF16), (hv, BF16), (qr + hv, F32), (kvr + LANE, F32), (d, BF16), (d, BF16), (qr, BF16), (kvr, BF16)]
    return pl.pallas_call(
        body, name="fb_fwd", grid=(t_all // tb,),
        in_specs=[tok(d), ANY] + [_full(c.shape) for c in consts] + [pos, pos],
        out_specs=[tok(c) for c, _ in outs],
        out_shape=[_sds((t_all, c), dt) for c, dt in outs],
        scratch_shapes=[pltpu.VMEM((qr + hv, d), BF16), pltpu.VMEM((kvr, d), BF16), pltpu.VMEM((kvr, d), BF16), pltpu.VMEM((qr, d), BF16),
                        pltpu.VMEM((qr, d), BF16), pltpu.SemaphoreType.DMA((5 * N_CHIPS,))],
        compiler_params=_params(1),
    )(x1, wg, *consts, cos_t, sin_t)


def _causal_mask(row0, col0, nrows, ncols):
    rows = row0 + lax.broadcasted_iota(jnp.int32, (nrows, ncols), 0)
    cols = col0 + lax.broadcasted_iota(jnp.int32, (nrows, ncols), 1)
    return cols <= rows


def _attn_fwd(qn, qr, kn, kr, v, seq, ta):
    t_all, hv = qn.shape
    nheads, nb, na = hv // LANE, t_all // seq, seq // ta

    reps = ta // LANE
    hp = ATTN_HEADS
    wide = hp * LANE

    def body(qn_ref, qr_ref, kn_ref, kr_ref, v_ref, o_ref, lse_ref, m_s, l_s, acc_s):
        i = pl.program_id(2)
        m_s[...] = jnp.full((ta, wide), -1e30, F32)
        l_s[...] = jnp.zeros((ta, wide), F32)
        acc_s[...] = jnp.zeros((ta, wide), F32)
        heads = [slice(n * LANE, (n + 1) * LANE) for n in range(hp)]
        qs = [jnp.concatenate([qn_ref[:, hd], qr_ref[:, hd]], axis=1) for hd in heads]

        def tile(j, diagonal):
            cols = pl.ds(pl.multiple_of(j * ta, ta), ta)
            k_rope = kr_ref[cols, :]
            for q, hd in zip(qs, heads):
                k = jnp.concatenate([kn_ref[cols, hd], k_rope], axis=1)
                s = _dot_nt(q, k)
                if diagonal:
                    s = jnp.where(_causal_mask(0, 0, ta, ta), s, -1e30)
                m_prev = m_s[:, hd]
                m_new = jnp.maximum(m_prev, jnp.max(s, axis=1, keepdims=True))
                p = jnp.exp2(s - jnp.tile(m_new, (1, reps)))
                alpha = jnp.exp2(m_prev - m_new)
                l_s[:, hd] = alpha * l_s[:, hd] + jnp.sum(p, axis=1, keepdims=True)
                acc_s[:, hd] = alpha * acc_s[:, hd] + _dot(p.astype(BF16), v_ref[cols, hd])
                m_s[:, hd] = m_new

        def off_diagonal(j, carry):
            tile(j, False)
            return carry

        lax.fori_loop(0, i, off_diagonal, 0)
        tile(i, True)
        o_ref[...] = (acc_s[...] / l_s[...]).astype(BF16)
        lse_ref[...] = m_s[...] + jnp.log2(l_s[...])

    qspec = pl.BlockSpec((ta, wide), lambda b, h, i: (b * na + i, h))
    kspec = pl.BlockSpec((seq, wide), lambda b, h, i: (b, h))
    krspec = pl.BlockSpec((seq, LANE), lambda b, h, i: (b, 0))
    return pl.pallas_call(
        body, name="attn_fwd", grid=(nb, nheads // hp, na),
        in_specs=[qspec, qspec, kspec, krspec, kspec],
        out_specs=[qspec, qspec],
        out_shape=[_sds((t_all, hv), BF16), _sds((t_all, hv))],
        scratch_shapes=[pltpu.VMEM((ta, wide), F32)] * 3,
        compiler_params=_params(3),
    )(qn, qr, kn, kr, v)


def _attn_bwd(qn, qr, kn, kr, v, do, lse, delta, seq, ta):
    t_all, hv = qn.shape
    nheads, nb, na = hv // LANE, t_all // seq, seq // ta

    reps = ta // LANE
    nchunks = ta // ATTN_ROWS

    hp = ATTN_HEADS_BWD
    wide = hp * LANE
    heads = [slice(n * LANE, (n + 1) * LANE) for n in range(hp)]

    def body(qn_ref, qr_ref, kn_ref, kr_ref, v_ref, do_ref, lse_ref, dl_ref, dqn_ref, dqr_ref, dkn_ref, dkr_ref, dv_ref,
             s_s, dp_s, p_s, ds_s, dk_s, dv_s):
        j = pl.program_id(2)

        @pl.when(j == 0)
        def _():
            dqn_ref[...] = jnp.zeros((seq, wide), F32)
            dqr_ref[...] = jnp.zeros((seq, wide), F32)

        dk_s[...] = jnp.zeros((hp, ta, 2 * LANE), F32)
        dv_s[...] = jnp.zeros((hp, ta, LANE), F32)
        k_rope = kr_ref[...]
        ks = [jnp.concatenate([kn_ref[:, hd], k_rope], axis=1) for hd in heads]

        def tile(i, diagonal):
            rows_i = pl.ds(pl.multiple_of(i * ta, ta), ta)
            for n, hd in enumerate(heads):
                q = jnp.concatenate([qn_ref[rows_i, hd], qr_ref[rows_i, hd]], axis=1)
                do_b = do_ref[rows_i, hd]
                s_s[n] = _dot_nt(q, ks[n])
                dp_s[n] = _dot_nt(do_b, v_ref[:, hd])
                for c in range(nchunks):
                    rows = pl.ds(c * ATTN_ROWS, ATTN_ROWS)
                    seq_rows = pl.ds(pl.multiple_of(i * ta + c * ATTN_ROWS, ATTN_ROWS), ATTN_ROWS)
                    s = s_s[n, rows, :]
                    if diagonal:
                        s = jnp.where(_causal_mask(c * ATTN_ROWS, 0, ATTN_ROWS, ta), s, -1e30)
                    p = jnp.exp2(s - jnp.tile(lse_ref[seq_rows, hd], (1, reps)))
                    p_s[n, rows, :] = p.astype(BF16)
                    ds_s[n, rows, :] = (p * (dp_s[n, rows, :] - jnp.tile(dl_ref[seq_rows, hd], (1, reps)))).astype(BF16)
                dv_s[n] += _dot_tn(p_s[n], do_b)
                ds = ds_s[n]
                dk_s[n] += _dot_tn(ds, q)
                dq = _dot(ds, ks[n])
                dqn_ref[rows_i, hd] += dq[:, :LANE]
                dqr_ref[rows_i, hd] += dq[:, LANE:]

        def off_diagonal(i, carry):
            tile(i, False)
            return carry

        tile(j, True)
        lax.fori_loop(j + 1, na, off_diagonal, 0)
        for n, hd in enumerate(heads):
            dkn_ref[:, hd] = (dk_s[n, :, :LANE] * LN2).astype(BF16)
            dkr_ref[:, hd] = dk_s[n, :, LANE:] * LN2
            dv_ref[:, hd] = dv_s[n].astype(BF16)

    qspec = pl.BlockSpec((seq, wide), lambda b, h, j: (b, h))
    kspec = pl.BlockSpec((ta, wide), lambda b, h, j: (b * na + j, h))
    krspec = pl.BlockSpec((ta, LANE), lambda b, h, j: (b * na + j, 0))
    return pl.pallas_call(
        body, name="attn_bwd", grid=(nb, nheads // hp, na),
        in_specs=[qspec, qspec, kspec, krspec, kspec, qspec, qspec, qspec],
        out_specs=[qspec, qspec, kspec, kspec, kspec],
        out_shape=[_sds((t_all, hv)), _sds((t_all, hv)), _sds((t_all, hv), BF16), _sds((t_all, hv)), _sds((t_all, hv), BF16)],
        scratch_shapes=[pltpu.VMEM((hp, ta, ta), F32), pltpu.VMEM((hp, ta, ta), F32), pltpu.VMEM((hp, ta, ta), BF16), pltpu.VMEM((hp, ta, ta), BF16),
                        pltpu.VMEM((hp, ta, 2 * LANE), F32), pltpu.VMEM((hp, ta, LANE), F32)],
        compiler_params=_params(3),
    )(qn, qr, kn, kr, v, do, lse, delta)


def _head(o, ub, x1, target, wg, w, lay, tb):
    t_all, d = x1.shape
    hv, qr = lay.hv, lay.qr
    nheads = hv // LANE

    def body(o_ref, ub_ref, x1_ref, tg_ref, wg_ref, gf, loss_ref, dgf_ref, yb_ref, dx2_ref, do_ref, dg_ref, dl_ref, wob, sems):
        @pl.when(pl.program_id(0) == 0)
        def _():
            cps = _fetch(wg_ref, lay, "out_b", wob, sems, 0)
            for cp in cps:
                cp.start()
            loss_ref[...] = jnp.zeros((1, LANE), F32)
            dgf_ref[...] = jnp.zeros((1, d), F32)
            for cp in cps:
                cp.wait()

        ov = o_ref[...].astype(F32)
        g = ub_ref[:, qr:]
        sg = _sigmoid(g)
        silu = g * sg
        yb = (ov * silu).astype(BF16)
        yb_ref[...] = yb
        x2 = x1_ref[...] + _dot(yb, wob[...])
        rinv = _rinv(x2)
        err = x2 * rinv * gf[...] - tg_ref[...]
        loss_ref[...] += (0.5 / d) * jnp.sum(jnp.sum(err * err, axis=1, keepdims=True), axis=0, keepdims=True)
        dx2, dgf = _rms_bwd(x2, rinv, gf[...], err * (1.0 / d))
        dgf_ref[...] += dgf
        dx2_ref[...] = dx2
        dyb = _dot_nt(dx2.astype(BF16), wob[...])
        dov = dyb * silu
        do_ref[...] = dov.astype(BF16)
        dg_ref[...] = (dyb * ov * (sg * (1.0 + g * (1.0 - sg)))).astype(BF16)
        prod = dov * ov
        dl_ref[...] = jnp.concatenate(
            [jnp.broadcast_to(jnp.sum(prod[:, n * LANE:(n + 1) * LANE], axis=1, keepdims=True), (tb, LANE)) for n in range(nheads)], axis=1)

    tok = lambda c: pl.BlockSpec((tb, c), lambda i: (i, 0))
    return pl.pallas_call(
        body, name="head", grid=(t_all // tb,),
        in_specs=[tok(hv), tok(qr + hv), tok(d), tok(d), ANY, _full((1, d))],
        out_specs=[_full((1, LANE)), _full((1, d)), tok(hv), tok(d), tok(hv), tok(hv), tok(hv)],
        out_shape=[_sds((1, LANE)), _sds((1, d)), _sds((t_all, hv), BF16), _sds((t_all, d)), _sds((t_all, hv), BF16), _sds((t_all, hv), BF16),
                   _sds((t_all, hv))],
        scratch_shapes=[pltpu.VMEM((hv, d), BF16), pltpu.SemaphoreType.DMA((N_CHIPS,))],
        compiler_params=_params(1),
    )(o, ub, x1, target, wg, w["final_norm"])


def _fb_bwd(dqn, dqr, dkn, dkr, dv, dgate, ub, ckr, x1, dx2, wg, w, lay, cos_t, sin_t, seq, tb):
    t_all, d = x1.shape
    hv, qr, kvr = lay.hv, lay.qr, lay.kvr
    nheads = hv // LANE
    npos = seq // tb

    def body(dqn_ref, dqr_ref, dkn_ref, dkr_ref, dv_ref, dg_ref, ub_ref, ckr_ref, x1_ref, dx2_ref, wg_ref,
             qn, nb, kvn, wdkv, nkv, cos_ref, sin_ref,
             dx1_ref, dqrp_ref, dqnp_ref, dub_ref, dckr_ref, dqn_g, dnb_g, dkvn_g, dnkv_g, winb, wuk, wuv, wuqn, wuqr, sems):
        @pl.when(pl.program_id(0) == 0)
        def _():
            cps = []
            for n, (key, dst) in enumerate((("in_b", winb), ("uk", wuk), ("uv", wuv), ("uq_n", wuqn), ("uq_r", wuqr))):
                cps += _fetch(wg_ref, lay, key, dst, sems, n * N_CHIPS)
            for cp in cps:
                cp.start()
            dqn_g[...] = jnp.zeros((1, qr), F32)
            dnb_g[...] = jnp.zeros((1, d), F32)
            dkvn_g[...] = jnp.zeros((1, kvr), F32)
            dnkv_g[...] = jnp.zeros((1, d), F32)
            for cp in cps:
                cp.wait()

        cos, sin = cos_ref[...], sin_ref[...]
        xv = x1_ref[...]
        rinv1 = _rinv(xv)
        dqr_v = dqr_ref[...] * ATTN_SCALE
        dqr_pre = (dqr_v * jnp.tile(cos, (1, nheads)) + _swap_halves(dqr_v * jnp.tile(sin, (1, nheads)))).astype(BF16)
        dqrp_ref[...] = dqr_pre
        dqn_pre = (dqn_ref[...] * ATTN_SCALE).astype(BF16)
        dqnp_ref[...] = dqn_pre
        dcq = _dot_nt(dqn_pre, wuqn[...]) + _dot_nt(dqr_pre, wuqr[...])
        cq_pre = ub_ref[:, :qr]
        dcq_pre, g1 = _rms_bwd(cq_pre, _rinv(cq_pre), qn[...], dcq)
        dqn_g[...] += g1
        dub = jnp.concatenate([dcq_pre.astype(BF16), dg_ref[...]], axis=1)
        dub_ref[...] = dub
        dx1_b, g2 = _rms_bwd(xv, rinv1, nb[...], _dot(dub, winb[...]))
        dnb_g[...] += g2
        dkr_all = dkr_ref[...]
        dkr_sum = dkr_all[:, :LANE]
        for n in range(1, nheads):
            dkr_sum = dkr_sum + dkr_all[:, n * LANE:(n + 1) * LANE]
        dckr_rope = dkr_sum * cos + _swap_halves(dkr_sum * sin)
        dckv = _dot_nt(dkn_ref[...].astype(BF16), wuk[...]) + _dot_nt(dv_ref[...].astype(BF16), wuv[...])
        ckv_pre = ckr_ref[:, :kvr]
        dckv_pre, g3 = _rms_bwd(ckv_pre, _rinv(ckv_pre), kvn[...], dckv)
        dkvn_g[...] += g3
        dckr = jnp.concatenate([dckv_pre, dckr_rope], axis=1).astype(BF16)
        dckr_ref[...] = dckr
        dx1_kv, g4 = _rms_bwd(xv, rinv1, nkv[...], _dot_nt(dckr, wdkv[...]))
        dnkv_g[...] += g4
        dx1_ref[...] = dx2_ref[...] + dx1_b + dx1_kv

    tok = lambda c: pl.BlockSpec((tb, c), lambda i: (i, 0))
    pos = pl.BlockSpec((tb, LANE), lambda i: (i % npos, 0))
    consts = [w["q_norm"], w["norm_b"], w["kv_norm"], w["w_dkv_p"], w["norm_kv"]]
    return pl.pallas_call(
        body, name="fb_bwd", grid=(t_all // tb,),
        in_specs=[tok(hv)] * 6 + [tok(qr + hv), tok(kvr + LANE), tok(d), tok(d), ANY] + [_full(c.shape) for c in consts] + [pos, pos],
        out_specs=[tok(d), tok(hv), tok(hv), tok(qr + hv), tok(kvr + LANE), _full((1, qr)), _full((1, d)), _full((1, kvr)), _full((1, d))],
        out_shape=[_sds((t_all, d)), _sds((t_all, hv), BF16), _sds((t_all, hv), BF16), _sds((t_all, qr + hv), BF16), _sds((t_all, kvr + LANE), BF16),
                   _sds((1, qr)), _sds((1, d)), _sds((1, kvr)), _sds((1, d))],
        scratch_shapes=[pltpu.VMEM((qr + hv, d), BF16), pltpu.VMEM((kvr, d), BF16), pltpu.VMEM((kvr, d), BF16), pltpu.VMEM((qr, d), BF16),
                        pltpu.VMEM((qr, d), BF16), pltpu.SemaphoreType.DMA((5 * N_CHIPS,))],
        compiler_params=_params(1),
    )(dqn, dqr, dkn, dkr, dv, dgate, ub, ckr, x1, dx2, wg, *consts, cos_t, sin_t)


def _fa_bwd(dx1, x, u, xb, hs, wg, g16, g32, w, lay, seq, tb):
    t_all, d = x.shape
    dr = lay.dr
    nblocks = w["w_rg"].shape[0]
    nblk = seq // tb
    nt = tb // SUBLANE
    per8 = tb // 8

    def body(dx1_ref, x_ref, u_ref, xb_ref, hs_ref, hh_ref, wg_ref, g16_ref, g32_ref, na, cw, wrg, brg, wig, big, lam,
             gx_ref, du_ref, dna_g, dcw_g, dcb_g, dbrg_g, dbig_g, dlam_g, dwrg_g, dwig_g, got_ref, sib_ref, own_ref,
             wint, wout, hpad, a_s, d_s, g_s, dxpad, carry, sems, send_sems, recv_sems, local_sem):
        b, jj = pl.program_id(0), pl.program_id(1)
        first_block = jj == nblk - 1
        scatter = _ScatterDirect(g16_ref, g32_ref, got_ref, sib_ref, own_ref, send_sems, recv_sems, local_sem, lay, G_GROUPS["early"])

        @pl.when((b == 0) & (jj == 0))
        def _():
            scatter.start()
            cps = _fetch(wg_ref, lay, "in_a", wint, sems, 0) + _fetch(wg_ref, lay, "out_a", wout, sems, N_CHIPS)
            for cp in cps:
                cp.start()
            dna_g[...] = jnp.zeros((1, d), F32)
            dcw_g[...] = jnp.zeros((4, dr), F32)
            dcb_g[...] = jnp.zeros((1, dr), F32)
            dbrg_g[...] = jnp.zeros((1, dr), F32)
            dbig_g[...] = jnp.zeros((1, dr), F32)
            dlam_g[...] = jnp.zeros((1, dr), F32)
            dwrg_g[...] = jnp.zeros((nblocks, LANE, LANE), F32)
            dwig_g[...] = jnp.zeros((nblocks, LANE, LANE), F32)
            for cp in cps:
                cp.wait()

        @pl.when(jj == 0)
        def _():
            dxpad[pl.ds(tb, 8), :] = jnp.zeros((8, dr), F32)
            carry[...] = jnp.zeros((8, dr), F32)

        keep = jnp.where(first_block, 0.0, 1.0)
        dx1v = dx1_ref[...]
        gate = u_ref[:, dr:]
        xpre = u_ref[:, :dr]
        hpad[pl.ds(0, 8), :] = hh_ref[...] * keep
        hpad[pl.ds(8, tb), :] = hs_ref[...]
        xb = xb_ref[...]
        xbb = xb.astype(BF16)
        r, i = _gates(xb, wrg, brg[...], wig, big[...], nblocks)
        sp = _softplus(-lam[...])
        log_a = -LRU_C * r * sp
        a, a2, nem = _decay(log_a)
        mult = jnp.sqrt(nem)
        sg = _sigmoid(gate)
        dy = _dot_nt(dx1v.astype(BF16), wout[...])
        hsv = hs_ref[...]
        dgate = dy * hsv * (sg * (1.0 + gate * (1.0 - sg)))
        a_s[...] = a
        d_s[...] = dy * (gate * sg)
        row = lax.broadcasted_iota(jnp.int32, (8, dr), 0)

        def step(k, c):
            r0 = pl.multiple_of((nt - 1 - k) * 8, 8)
            av = a_s[pl.ds(r0, 8), :]
            dv = d_s[pl.ds(r0, 8), :]
            qv = av * dv
            for s in (1, 2, 4):
                m = row < 8 - s
                a_sh = jnp.where(m, pltpu.roll(av, 8 - s, 0), 1.0)
                q_sh = jnp.where(m, pltpu.roll(qv, 8 - s, 0), 0.0)
                qv = qv + av * q_sh
                av = av * a_sh
            qv = qv + av * c
            g_s[pl.ds(r0, 8), :] = dv + jnp.where(row < 7, pltpu.roll(qv, 7, 0), c)
            return jnp.broadcast_to(qv[0:1, :], qv.shape)

        carry[...] = lax.fori_loop(0, nt, step, carry[...])
        g = g_s[...]
        ix = i * xb
        dlog_a = g * (hpad[pl.ds(7, tb), :] * a - ix * (a2 * lax.rsqrt(nem)))
        dix = g * mult
        dlam_g[...] += -jax.nn.sigmoid(-lam[...]) * jnp.sum(dlog_a * (-LRU_C * r), axis=0, keepdims=True)
        drg = dlog_a * (-LRU_C * sp) * r * (1.0 - r)
        dig = dix * xb * i * (1.0 - i)
        dbrg_g[...] += jnp.sum(drg, axis=0, keepdims=True)
        dbig_g[...] += jnp.sum(dig, axis=0, keepdims=True)
        drgb, digb = drg.astype(BF16), dig.astype(BF16)
        back = []
        for n in range(nblocks):
            cols = slice(n * LANE, (n + 1) * LANE)
            dwrg_g[n] += _dot_tn(xbb[:, cols], drgb[:, cols])
            dwig_g[n] += _dot_tn(xbb[:, cols], digb[:, cols])
            back.append(_dot_nt(drgb[:, cols], wrg[n]) + _dot_nt(digb[:, cols], wig[n]))
        dxb = dix * i + jnp.concatenate(back, axis=1)
        dcb_g[...] += jnp.sum(dxb, axis=0, keepdims=True)
        dxpad[pl.ds(0, tb), :] = dxb
        later = [dxb, dxpad[pl.ds(1, tb), :], dxpad[pl.ds(2, tb), :], dxpad[pl.ds(3, tb), :]]
        dxpad[pl.ds(tb, 8), :] = dxb[:8, :]
        dxpre = cw[3:4, :] * later[0] + cw[2:3, :] * later[1] + cw[1:2, :] * later[2] + cw[0:1, :] * later[3]
        for m in range(4):
            dcw_g[3 - m:4 - m, :] += jnp.sum(later[m] * xpre, axis=0, keepdims=True)
        du = jnp.concatenate([dxpre, dgate], axis=1).astype(BF16)
        du_ref[...] = du
        xv = x_ref[...]
        dxa, g1 = _rms_bwd(xv, _rinv(xv), na[...], _dot(du, wint[...]))
        dna_g[...] += g1
        gx_ref[...] = dx1v + dxa

        @pl.when((b == t_all // seq - 1) & (jj == nblk - 1))
        def _():
            scatter.finish()

    blk = lambda b, j: b * nblk + (nblk - 1 - j)
    tok = lambda c: pl.BlockSpec((tb, c), lambda b, j: (blk(b, j), 0))
    halo = pl.BlockSpec((8, dr), lambda b, j: (jnp.maximum(blk(b, j) * per8 - 1, 0), 0))
    consts = [w["norm_a"], w["conv_w"], w["w_rg"], w["b_rg"], w["w_ig"], w["b_ig"], w["lru_lambda"]]
    vec = lambda c: _full((1, c))
    blocks3 = (nblocks, LANE, LANE)
    return pl.pallas_call(
        body, name="fa_bwd", grid=(t_all // seq, nblk),
        in_specs=[tok(d), tok(d), tok(2 * dr), tok(dr), tok(dr), halo, ANY, ANY, ANY] + [_full(c.shape) for c in consts],
        out_specs=[tok(d), tok(2 * dr), vec(d), _full((4, dr)), vec(dr), vec(dr), vec(dr), vec(dr), _full(blocks3), _full(blocks3), ANY, ANY, ANY],
        out_shape=[_sds((t_all, d)), _sds((t_all, 2 * dr), BF16), _sds((1, d)), _sds((4, dr)), _sds((1, dr)), _sds((1, dr)), _sds((1, dr)),
                   _sds((1, dr)), _sds(blocks3), _sds(blocks3)] + _scatter_direct_shapes(lay, "early"),
        scratch_shapes=[pltpu.VMEM((2 * dr, d), BF16), pltpu.VMEM((dr, d), BF16), pltpu.VMEM((tb + 8, dr), F32),
                        pltpu.VMEM((tb, dr), F32), pltpu.VMEM((tb, dr), F32), pltpu.VMEM((tb, dr), F32), pltpu.VMEM((tb + 8, dr), F32),
                        pltpu.VMEM((8, dr), F32), pltpu.SemaphoreType.DMA((2 * N_CHIPS,))] + SCATTER_DIRECT_SEMS,
        compiler_params=_params(2),
    )(dx1, x, u, xb, hs, hs, wg, g16, g32, *consts)


def _mm_into(gbufs, a, b, off, name, bt):
    t_all, m = a.shape
    n = b.shape[1]
    nsplit = 2 if m >= 1024 and (m // 2) % LANE == 0 else 1
    mh = m // nsplit
    nt = t_all // bt
    nbuf = len(gbufs)
    twin = nbuf == 2

    def body(a_ref, b_ref, *refs):
        outs, acc, sems = refs[nbuf:2 * nbuf], refs[2 * nbuf], refs[-1]
        acc16 = refs[2 * nbuf + 1] if twin else None
        part, t = pl.program_id(0), pl.program_id(1)

        def out_copies(h):
            dst = pl.ds(off + h * mh, mh)
            copies = [pltpu.make_async_copy(acc.at[h], outs[0].at[dst, :], sems.at[0, h])]
            if twin:
                copies.append(pltpu.make_async_copy(acc16.at[h], outs[1].at[dst, :], sems.at[1, h]))
            return copies

        prod = _dot_tn(a_ref[...].astype(BF16), b_ref[...].astype(BF16))
        for h in range(nsplit):
            @pl.when((part == h) & (t == 0))
            def _():
                acc[h] = prod

            @pl.when((part == h) & (t > 0))
            def _():
                acc[h] += prod

            @pl.when((part == h) & (t == nt - 1))
            def _():
                if twin:
                    acc16[h] = acc[h].astype(BF16)
                for cp in out_copies(h):
                    cp.start()

        @pl.when((part == nsplit - 1) & (t == nt - 1))
        def _():
            for h in range(nsplit):
                for cp in out_copies(h):
                    cp.wait()

    scratch = [pltpu.VMEM((nsplit, mh, n), F32)] + ([pltpu.VMEM((nsplit, mh, n), BF16)] if twin else []) + [pltpu.SemaphoreType.DMA((2, nsplit))]
    return pl.pallas_call(
        body, name=name, grid=(nsplit, nt),
        in_specs=[pl.BlockSpec((bt, mh), lambda h, t: (t, h)), pl.BlockSpec((bt, n), lambda h, t: (t, 0))] + [ANY] * nbuf,
        out_specs=[ANY] * nbuf, out_shape=[_sds(g.shape, g.dtype) for g in gbufs], input_output_aliases={2 + k: k for k in range(nbuf)},
        scratch_shapes=scratch, compiler_params=_params(2),
    )(a, b, *gbufs)


def _mm_tn(a, b, name, bt):
    t_all, m = a.shape
    n = b.shape[1]

    def body(a_ref, b_ref, o_ref):
        @pl.when(pl.program_id(0) == 0)
        def _():
            o_ref[...] = jnp.zeros((m, n), F32)

        o_ref[...] += _dot_tn(a_ref[...].astype(BF16), b_ref[...].astype(BF16))

    return pl.pallas_call(
        body, name=name, grid=(t_all // bt,),
        in_specs=[pl.BlockSpec((bt, m), lambda t: (t, 0)), pl.BlockSpec((bt, n), lambda t: (t, 0))],
        out_specs=_full((m, n)), out_shape=_sds((m, n)),
        compiler_params=_params(1),
    )(a, b)


class _Gather8:
    def __init__(self, x_ref, out_ref, send_sems, recv_sems, local_sem):
        x, y, c = _place()
        self.c, self.me, self.sibling = c, (x, y, c), (x, y, 1 - c)
        self.chips = [(1 - x, y), (x, 1 - y), (1 - x, 1 - y)]
        self.x_ref, self.out_ref, self.send_sems, self.recv_sems, self.local_sem = x_ref, out_ref, send_sems, recv_sems, local_sem

    def _slot(self, px, py, pc):
        return self.out_ref.at[4 * px + 2 * py + pc]

    def _copy(self, k, blk, to, src=None):
        return pltpu.make_async_remote_copy(
            src_ref=self._slot(*blk) if src is None else src, dst_ref=self._slot(*blk), send_sem=self.send_sems.at[k],
            recv_sem=self.recv_sems.at[k], device_id=to, device_id_type=MESH)

    def _mine(self):
        return pltpu.make_async_copy(self.x_ref, self._slot(*self.me), self.local_sem)

    def _first(self):
        return [self._copy(0, self.me, self.sibling, src=self.x_ref)] + [
            self._copy(1 + j, self.me, (*chip, self.c), src=self.x_ref) for j, chip in enumerate(self.chips)]

    def _passed(self):
        return [self._copy(4 + j, (*chip, self.c), self.sibling) for j, chip in enumerate(self.chips)]

    def start(self):
        self._mine().start()
        for cp in self._first():
            cp.start()

    def forward(self):
        passed = self._passed()
        for j, chip in enumerate(self.chips):
            self._copy(1 + j, (*chip, self.c), self.me).wait_recv()
            passed[j].start()

    def finish(self):
        self._copy(0, self.sibling, self.me).wait_recv()
        for j, chip in enumerate(self.chips):
            self._copy(4 + j, (*chip, 1 - self.c), self.me).wait_recv()
        for cp in self._first() + self._passed():
            cp.wait_send()
        self._mine().wait()


class _Scatter:
    def __init__(self, p16_ref, p32_ref, got_ref, own_ref, send_sems, recv_sems, local_sem, lay, order):
        self.x, self.y, self.c = _place()
        self.chips = [(1 - self.x, self.y), (self.x, 1 - self.y), (1 - self.x, 1 - self.y)]
        self.refs = (p16_ref, p32_ref, got_ref, own_ref, send_sems, recv_sems, local_sem)
        self.lay, self.order = lay, order

    def _rows_of(self, ref, key, chip):
        start = pl.multiple_of(self.lay.g_off[key] + chip * self.lay.rows[key], ROW_ALIGN)
        return ref.at[pl.ds(start, self.lay.rows[key]), :]

    def _compact(self, ref, key):
        return ref.at[pl.ds(self.lay.c_off[key], self.lay.rows[key]), :]

    def start(self):
        p16_ref, p32_ref, got_ref, own_ref, send_sems, recv_sems, local_sem = self.refs
        for key in self.order:
            pltpu.make_async_copy(self._rows_of(p32_ref, key, 2 * self.x + self.y), self._compact(own_ref, key), local_sem).start()
        for k, (px, py) in enumerate(self.chips):
            for key in self.order:
                pltpu.make_async_remote_copy(
                    src_ref=self._rows_of(p16_ref, key, 2 * px + py), dst_ref=self._compact(got_ref.at[k], key), send_sem=send_sems.at[k],
                    recv_sem=recv_sems.at[k], device_id=(px, py, self.c), device_id_type=MESH).start()

    def finish(self):
        _, _, got_ref, own_ref, send_sems, recv_sems, local_sem = self.refs
        for k, (px, py) in enumerate(self.chips):
            pltpu.make_async_remote_copy(src_ref=got_ref.at[k], dst_ref=got_ref.at[k], send_sem=send_sems.at[k], recv_sem=recv_sems.at[k],
                                         device_id=(px, py, self.c), device_id_type=MESH).wait()
        pltpu.make_async_copy(own_ref, own_ref, local_sem).wait()


class _ScatterDirect:
    def __init__(self, g16_ref, g32_ref, got_ref, sib_ref, own_ref, send_sems, recv_sems, local_sem, lay, order):
        self.x, self.y, self.c = _place()
        self.chips = [(1 - self.x, self.y), (self.x, 1 - self.y), (1 - self.x, 1 - self.y)]
        self.refs = (g16_ref, g32_ref, got_ref, sib_ref, own_ref, send_sems, recv_sems, local_sem)
        self.lay, self.order, self.half = lay, order, lay.d // 2

    def _src(self, ref, key, chip, h):
        start = pl.multiple_of(self.lay.g_off[key] + chip * self.lay.rows[key], ROW_ALIGN)
        return ref.at[pl.ds(start, self.lay.rows[key]), pl.ds(pl.multiple_of(h * self.half, LANE), self.half)]

    def _compact(self, ref, key):
        return ref.at[pl.ds(self.lay.c_off[key], self.lay.rows[key]), :]

    def start(self):
        g16_ref, g32_ref, got_ref, sib_ref, own_ref, send_sems, recv_sems, local_sem = self.refs
        x, y, c = self.x, self.y, self.c
        for key in self.order:
            pltpu.make_async_copy(self._src(g32_ref, key, 2 * x + y, c), self._compact(own_ref, key), local_sem).start()
            pltpu.make_async_remote_copy(
                src_ref=self._src(g32_ref, key, 2 * x + y, 1 - c), dst_ref=self._compact(sib_ref, key), send_sem=send_sems.at[6],
                recv_sem=recv_sems.at[6], device_id=(x, y, 1 - c), device_id_type=MESH).start()
        for k, (px, py) in enumerate(self.chips):
            for h in range(2):
                for key in self.order:
                    pltpu.make_async_remote_copy(
                        src_ref=self._src(g16_ref, key, 2 * px + py, h), dst_ref=self._compact(got_ref.at[2 * k + c], key),
                        send_sem=send_sems.at[2 * k + h], recv_sem=recv_sems.at[2 * k + c], device_id=(px, py, h), device_id_type=MESH).start()

    def finish(self):
        _, _, got_ref, sib_ref, own_ref, send_sems, recv_sems, local_sem = self.refs
        x, y, c = self.x, self.y, self.c
        for k, (px, py) in enumerate(self.chips):
            for h in range(2):
                whole = pltpu.make_async_remote_copy(src_ref=got_ref.at[2 * k + h], dst_ref=got_ref.at[2 * k + h], send_sem=send_sems.at[2 * k + h],
                                                     recv_sem=recv_sems.at[2 * k + h], device_id=(px, py, h), device_id_type=MESH)
                whole.wait_send()
                whole.wait_recv()
        pltpu.make_async_remote_copy(src_ref=sib_ref, dst_ref=sib_ref, send_sem=send_sems.at[6], recv_sem=recv_sems.at[6],
                                     device_id=(x, y, 1 - c), device_id_type=MESH).wait()
        pltpu.make_async_copy(own_ref, own_ref, local_sem).wait()


def _scatter_direct_shapes(lay, group):
    rows, half = lay.c_rows[group], lay.d // 2
    return [_sds((6, rows, half), BF16), _sds((rows, half), F32), _sds((rows, half), F32)]


SCATTER_DIRECT_SEMS = [pltpu.SemaphoreType.DMA((7,)), pltpu.SemaphoreType.DMA((7,)), pltpu.SemaphoreType.DMA]
SCATTER_SEMS = [pltpu.SemaphoreType.DMA((3,)), pltpu.SemaphoreType.DMA((3,)), pltpu.SemaphoreType.DMA]
GATHER_SEMS = [pltpu.SemaphoreType.DMA((7,)), pltpu.SemaphoreType.DMA((7,)), pltpu.SemaphoreType.DMA]


def _all_gather8(blocks, name):
    nb = len(blocks)

    def body(*refs):
        x_refs, out_refs = refs[:nb], refs[nb:2 * nb]
        send_sems, recv_sems, local_sems = refs[2 * nb:]
        gathers = [_Gather8(x_refs[n], out_refs[n], send_sems.at[n], recv_sems.at[n], local_sems.at[n]) for n in range(nb)]
        for g in gathers:
            g.start()
        for g in gathers:
            g.forward()
        for g in gathers:
            g.finish()

    return pl.pallas_call(
        body, name=name, out_shape=[_sds((8,) + b.shape, b.dtype) for b in blocks], in_specs=[ANY] * nb, out_specs=[ANY] * nb,
        scratch_shapes=[pltpu.SemaphoreType.DMA((nb, 7)), pltpu.SemaphoreType.DMA((nb, 7)), pltpu.SemaphoreType.DMA((nb,))],
    )(*blocks)


def _swap_sibling(srcs, name, half_cols=False):
    n = len(srcs)
    halves = [s.shape[1] // 2 if half_cols else s.shape[1] for s in srcs]

    def body(*refs):
        src_refs, out_refs, send_sems, recv_sems = refs[:n], refs[n:2 * n], refs[2 * n], refs[2 * n + 1]
        x, y, c = _place()
        copies = []
        for k in range(n):
            part = src_refs[k].at[:, pl.ds(pl.multiple_of((1 - c) * halves[k], LANE), halves[k])] if half_cols else src_refs[k]
            copies.append(pltpu.make_async_remote_copy(src_ref=part, dst_ref=out_refs[k], send_sem=send_sems.at[k], recv_sem=recv_sems.at[k],
                                                       device_id=(x, y, 1 - c), device_id_type=MESH))
        for cp in copies:
            cp.start()
        for cp in copies:
            cp.wait()

    return pl.pallas_call(
        body, name=name, out_shape=[_sds((s.shape[0], h), s.dtype) for s, h in zip(srcs, halves)], in_specs=[ANY] * n, out_specs=[ANY] * n,
        scratch_shapes=[pltpu.SemaphoreType.DMA((n,)), pltpu.SemaphoreType.DMA((n,))],
    )(*srcs)


def _return_and_gather(mines, rep_block):
    n = len(mines)

    def body(*refs):
        src_refs, rep_ref, out_refs, rep_out = refs[:n], refs[n], refs[n + 1:2 * n + 1], refs[2 * n + 1]
        send_sems, recv_sems, g_send, g_recv, g_local = refs[2 * n + 2:]
        x, y, c = _place()
        copies = [pltpu.make_async_remote_copy(src_ref=src_refs[k], dst_ref=out_refs[k], send_sem=send_sems.at[k], recv_sem=recv_sems.at[k],
                                               device_id=(x, y, 1 - c), device_id_type=MESH) for k in range(n)]
        gather = _Gather8(rep_ref, rep_out, g_send, g_recv, g_local)
        for cp in copies:
            cp.start()
        gather.start()
        gather.forward()
        gather.finish()
        for cp in copies:
            cp.wait()

    return pl.pallas_call(
        body, name="rs_return", out_shape=[_sds(m.shape, m.dtype) for m in mines] + [_sds((8,) + rep_block.shape, rep_block.dtype)],
        in_specs=[ANY] * (n + 1), out_specs=[ANY] * (n + 1),
        scratch_shapes=[pltpu.SemaphoreType.DMA((n,)), pltpu.SemaphoreType.DMA((n,))] + GATHER_SEMS,
    )(*mines, rep_block)


def _scatter_shapes(lay, group, half):
    return [_sds((3, lay.c_rows[group], half), BF16), _sds((lay.c_rows[group], half), F32)]


def _scatter_chips(part16, part32, lay, group, name):
    def body(p16_ref, p32_ref, got_ref, own_ref, send_sems, recv_sems, local_sem):
        sc = _Scatter(p16_ref, p32_ref, got_ref, own_ref, send_sems, recv_sems, local_sem, lay, G_GROUPS[group])
        sc.start()
        sc.finish()

    return pl.pallas_call(
        body, name=name, out_shape=_scatter_shapes(lay, group, part16.shape[1]), in_specs=[ANY, ANY], out_specs=[ANY, ANY],
        scratch_shapes=SCATTER_SEMS,
    )(part16, part32)


def _sum_sibling(gbuf, got, cidx, name):
    rows, d = gbuf.shape
    half = d // 2
    rb = _row_block(rows)

    def body(c_ref, g_ref, r_ref, o32_ref, o16_ref):
        del c_ref
        s = g_ref[...] + r_ref[...]
        o32_ref[...] = s
        o16_ref[...] = s.astype(BF16)

    plain = pl.BlockSpec((rb, half), lambda i, c: (i, 0))
    return pl.pallas_call(
        body, name=name,
        grid_spec=pltpu.PrefetchScalarGridSpec(num_scalar_prefetch=1, grid=(rows // rb,),
                                               in_specs=[pl.BlockSpec((rb, half), lambda i, c: (i, c[0])), plain], out_specs=[plain, plain]),
        out_shape=[_sds((rows, half)), _sds((rows, half), BF16)], compiler_params=_params(1),
    )(cidx, gbuf, got)


def _sum_devices(own, sib, got, name):
    rows, half = own.shape
    rb = _row_block(rows)
    n = got.shape[0]

    def body(a_ref, s_ref, b_ref, o_ref):
        acc = a_ref[...] + s_ref[...]
        for k in range(n):
            acc = acc + b_ref[k].astype(F32)
        o_ref[...] = acc

    spec = pl.BlockSpec((rb, half), lambda i: (i, 0))
    return pl.pallas_call(
        body, name=name, grid=(rows // rb,), in_specs=[spec, spec, pl.BlockSpec((n, rb, half), lambda i: (0, i, 0))], out_specs=spec,
        out_shape=_sds((rows, half)), compiler_params=_params(1),
    )(own, sib, got)


def _sum_chips(own, got, name):
    rows, half = own.shape
    rb = _row_block(rows)

    def body(a_ref, b_ref, o_ref):
        o_ref[...] = ((a_ref[...] + b_ref[0].astype(F32)) + b_ref[1].astype(F32)) + b_ref[2].astype(F32)

    spec = pl.BlockSpec((rb, half), lambda i: (i, 0))
    return pl.pallas_call(
        body, name=name, grid=(rows // rb,), in_specs=[spec, pl.BlockSpec((3, rb, half), lambda i: (0, i, 0))], out_specs=spec,
        out_shape=_sds((rows, half)), compiler_params=_params(1),
    )(own, got)


def _adamw(w, g, m, v):
    m = ADAM_B1 * m + (1.0 - ADAM_B1) * g
    v = ADAM_B2 * v + (1.0 - ADAM_B2) * (g * g)
    m_hat = m / (1.0 - ADAM_B1 ** ADAM_STEP)
    v_hat = v / (1.0 - ADAM_B2 ** ADAM_STEP)
    return -ADAM_LR * (m_hat / (jnp.sqrt(v_hat) + ADAM_EPS) + ADAM_WD * w), m, v


def _adamw_rows(name, w, g, m, v):
    _, rows, cols = w.shape
    rb = _row_block(rows, 256)

    def body(w_ref, g_ref, m_ref, v_ref, d_ref, mo_ref, vo_ref):
        d_ref[...], mo_ref[...], vo_ref[...] = _adamw(w_ref[...], g_ref[...], m_ref[...], v_ref[...])

    spec = pl.BlockSpec((1, rb, cols), lambda i: (0, i, 0))
    return pl.pallas_call(
        body, name=name, grid=(rows // rb,), in_specs=[spec] * 4, out_specs=[spec] * 3, out_shape=[_sds(w.shape)] * 3,
        compiler_params=_params(1),
    )(w, g, m, v)


def _adamw_group(ws, gs, ms, vs):
    n = len(ws)

    def body(*refs):
        for k in range(n):
            w_ref, g_ref, m_ref, v_ref = (refs[j * n + k] for j in range(4))
            outs = _adamw(w_ref[...], g_ref[...], m_ref[...], v_ref[...])
            for j in range(3):
                refs[(4 + j) * n + k][...] = outs[j]

    outs = pl.pallas_call(
        body, name="adamw_small", out_shape=[_sds(w.shape) for w in ws] * 3,
        compiler_params=pltpu.CompilerParams(vmem_limit_bytes=VMEM_LIMIT),
    )(*ws, *gs, *ms, *vs)
    return outs[:n], outs[n:2 * n], outs[2 * n:]


def _gather_weights(sh, lay):
    x, y, c = _place()
    d = lay.d
    uq = sh["w_uq"][0].astype(BF16)
    parts = {
        "in_b": sh["w_in_b"][0].T.astype(BF16), "in_a": sh["w_in_a"][0].T.astype(BF16), "out_a": sh["w_out_a"][0].astype(BF16),
        "out_b": sh["w_out_b"][0].astype(BF16), "uk": sh["w_uk"].astype(BF16).reshape(-1, d), "uv": sh["w_uv"].astype(BF16).reshape(-1, d),
        "uq_n": uq[:, :, :QK_NOPE].reshape(-1, d), "uq_r": jnp.pad(uq[:, :, QK_NOPE:], ((0, 0), (0, 0), (0, LANE - QK_ROPE))).reshape(-1, d),
        "dkv": jnp.pad(sh["w_dkv"].astype(BF16), ((0, 0), (0, LANE - QK_ROPE))).reshape(-1, d),
    }
    halves = {}
    for group, order in W_GROUPS.items():
        stack = jnp.concatenate([parts[k] for k in order], axis=0).reshape(2, lay.w_rows[group] // 2, d)
        halves[group] = lax.dynamic_index_in_dim(stack, c, 0, keepdims=False)
    small = jnp.concatenate([sh[k].reshape(-1) for k in SMALL])
    n_small = small.shape[0]
    width = _round_up(n_small, 2 * SUBLANE * LANE) // (2 * SUBLANE)
    small = jnp.pad(small, (0, 2 * SUBLANE * width - n_small)).reshape(2, SUBLANE, width)
    wg, sg = _all_gather8([halves["a"], lax.dynamic_index_in_dim(small, c, 0, keepdims=False)], "ag_weights")
    wg = wg.reshape(N_CHIPS, lay.w_rows["a"], d)
    sg = sg.reshape(N_CHIPS, 2 * SUBLANE * width)
    full, off = {}, 0
    for k in SMALL:
        n = sh[k].size
        piece = sg[:, off:off + n]
        off += n
        if k == "conv_w":
            full[k] = piece.reshape(N_CHIPS, 4, n // 4).transpose(1, 0, 2).reshape(4, n)
        else:
            full[k] = piece.reshape(1, N_CHIPS * n)
    return wg, halves["b"], full


def _chip_split(g, taps=False):
    if taps:
        n = g.shape[1] // N_CHIPS
        return g.reshape(4, N_CHIPS, n).transpose(1, 0, 2).reshape(N_CHIPS, 4 * n)
    return g.reshape(N_CHIPS, -1)


def kernel(x, norm_a, w_in_a, conv_w, conv_b, w_rg, b_rg, w_ig, b_ig, lru_lambda, w_out_a, norm_kv, w_dkv, kv_norm, w_uk, w_uv, norm_b, w_in_b, q_norm, w_uq, w_out_b, final_norm, loss_target, m_norm_a, m_w_in_a, m_conv_w, m_conv_b, m_w_rg, m_b_rg, m_w_ig, m_b_ig, m_lru_lambda, m_w_out_a, m_norm_kv, m_w_dkv, m_kv_norm, m_w_uk, m_w_uv, m_norm_b, m_w_in_b, m_q_norm, m_w_uq, m_w_out_b, m_final_norm, v_norm_a, v_w_in_a, v_conv_w, v_conv_b, v_w_rg, v_b_rg, v_w_ig, v_b_ig, v_lru_lambda, v_w_out_a, v_norm_kv, v_w_dkv, v_kv_norm, v_w_uk, v_w_uv, v_norm_b, v_w_in_b, v_q_norm, v_w_uq, v_w_out_b, v_final_norm):
    given = dict(locals())
    sh = {k: given[k] for k in WEIGHTS}
    xi, yi, ci = _place()
    nb, seq, d = x.shape
    t_all = nb * seq
    tb_a, tb_b, ta, bt = min(TOKENS_A, seq), min(TOKENS_B, seq), min(TOKENS_ATTN, seq), min(TOKENS_MM, t_all)
    dr = conv_b.shape[1] * N_CHIPS
    qr, kvr, nheads = q_norm.shape[1], kv_norm.shape[0], w_uk.shape[1]
    hv = nheads * LANE
    n_small = sum(sh[k].size for k in SMALL)
    n_repl = sum(sh[k].size for k in REPL)
    lay = _Layout(d, dr, qr, kvr, hv, n_small, n_repl)
    half = d // 2

    wga, wb_half, w = _gather_weights(sh, lay)
    w.update({"w_rg": w_rg[0].astype(BF16), "w_ig": w_ig[0].astype(BF16), "norm_kv": norm_kv[None, :],
              "kv_norm": kv_norm[None, :], "final_norm": final_norm[None, :], "norm_b": norm_b, "q_norm": q_norm})
    cos_t, sin_t = _rope_tables(seq)
    cidx = jnp.reshape(ci, (1,)).astype(jnp.int32)

    x0 = x.reshape(t_all, d)
    x1, u, hs, h, y, xb, wgb = _fa_fwd(x0, wga, wb_half, w, lay, seq, tb_a)
    wgb = wgb.reshape(N_CHIPS, lay.w_rows["b"], d)
    w["w_dkv_p"] = wgb[:, lay.w_off["dkv"]:lay.w_off["dkv"] + lay.rows["dkv"], :].reshape(d, kvr + LANE)
    qn, qrp, kn, kr, v, ub, ckr, hb, hk, cq, ckv = _fb_fwd(x1, wgb, w, lay, cos_t, sin_t, seq, tb_b)
    o, lse = _attn_fwd(qn, qrp, kn, kr, v, seq, ta)
    loss, g_final_norm, yb, dx2, do, dgate, delta = _head(o, ub, x1, loss_target.reshape(t_all, d), wgb, w, lay, tb_b)
    dqn, dqr, dkn, dkr, dv = _attn_bwd(qn, qrp, kn, kr, v, do, lse, delta, seq, ta)
    dx1, dqr_pre, dqn_pre, dub, dckr, g_q_norm, g_norm_b, g_kv_norm, g_norm_kv = _fb_bwd(
        dqn, dqr, dkn, dkr, dv, dgate, ub, ckr, x1, dx2, wgb, w, lay, cos_t, sin_t, seq, tb_b)
    loss = lax.psum(loss[0, 0], ("x", "y", "c"))

    gbufs = [lax.empty((lay.g_rows["early"], d), F32), lax.empty((lay.g_rows["early"], d), BF16)]
    for key, a, b in (("in_b", dub, hb), ("out_a", y, dx1), ("out_b", yb, dx2), ("uk", ckv, dkn), ("uv", ckv, dv), ("uq_n", cq, dqn_pre),
                      ("uq_r", cq, dqr_pre)):
        gbufs = _mm_into(gbufs, a, b, lay.g_off[key], "dw_" + key, bt)
    g_dkv = _mm_tn(hk, dckr, "dw_dkv", bt)
    gx, du, g_norm_a, g_conv_w, g_conv_b, g_b_rg, g_b_ig, g_lam, g_w_rg, g_w_ig, others, sib, own = _fa_bwd(
        dx1, x0, u, xb, hs, wga, gbufs[1], gbufs[0], w, lay, seq, tb_a)
    mine_early = _sum_devices(own, sib, others, "rs_sum_early")

    small = jnp.concatenate([_chip_split(g_norm_a), _chip_split(g_conv_w, taps=True), _chip_split(g_conv_b), _chip_split(g_b_rg),
                             _chip_split(g_b_ig), _chip_split(g_lam)], axis=1)
    small = jnp.pad(small, ((0, 0), (0, lay.small_rows * d - small.shape[1]))).reshape(N_CHIPS, lay.small_rows, d)
    repl_parts = {"w_rg": g_w_rg, "w_ig": g_w_ig, "norm_kv": g_norm_kv, "kv_norm": g_kv_norm, "norm_b": g_norm_b, "q_norm": g_q_norm,
                  "final_norm": g_final_norm}
    repl = jnp.concatenate([repl_parts[k].reshape(-1) for k in REPL])
    repl = jnp.pad(repl, (0, N_CHIPS * lay.repl_rows * d - n_repl)).reshape(N_CHIPS, lay.repl_rows, d)
    pad_rows = lay.rows["rest"] - lay.rows["dkv"] - lay.small_rows - lay.repl_rows
    rest = jnp.concatenate([g_dkv.reshape(N_CHIPS, lay.rows["dkv"], d), small, repl, jnp.zeros((N_CHIPS, pad_rows, d), F32)], axis=1)
    gbuf = lax.dynamic_update_slice(lax.empty((lay.g_rows["late"], d), F32), rest.reshape(N_CHIPS * lay.rows["rest"], d), (lay.g_off["rest"], 0))
    (gbuf,) = _mm_into([gbuf], du, h, lay.g_off["in_a"], "dw_in_a", bt)
    (got,) = _swap_sibling([gbuf], "rs_sibling_late", half_cols=True)
    part32, part16 = _sum_sibling(gbuf, got, cidx, "rs_sum_sibling_late")
    others, own = _scatter_chips(part16, part32, lay, "late", "rs_chips_late")
    mine_late = _sum_chips(own, others, "rs_sum_chips_late")

    r0 = lay.c_off["rest"] + lay.rows["dkv"] + lay.small_rows
    theirs_early, theirs_late, rep_all = _return_and_gather([mine_early, mine_late], mine_late[r0:r0 + lay.repl_rows])
    red = {}
    for group, mine, theirs in (("early", mine_early, theirs_early), ("late", mine_late, theirs_late)):
        red[group] = jnp.concatenate([jnp.where(ci == 0, mine, theirs), jnp.where(ci == 0, theirs, mine)], axis=1)
    rep_flat =rep_all.reshape(N_CHIPS, 2, lay.repl_rows, half).transpose(0, 2, 1, 3).reshape(-1)

    def rows(key):
        group = "late" if key in G_GROUPS["late"] else "early"
        return red[group][lay.c_off[key]:lay.c_off[key] + lay.rows[key]]

    grads = {"w_in_b": rows("in_b").T[None], "w_in_a": rows("in_a").T[None], "w_out_a": rows("out_a")[None], "w_out_b": rows("out_b")[None],
             "w_uk": rows("uk").reshape(w_uk.shape), "w_uv": rows("uv").reshape(w_uv.shape)}
    uq_n = rows("uq_n").reshape(qr // N_CHIPS, nheads, LANE)
    uq_r = rows("uq_r").reshape(qr // N_CHIPS, nheads, LANE)[:, :, :QK_ROPE]
    grads["w_uq"] = jnp.concatenate([uq_n, uq_r], axis=2)[None]
    rest_red = rows("rest")
    grads["w_dkv"] = rest_red[:lay.rows["dkv"]].reshape(d // N_CHIPS, kvr + LANE)[:, :kvr + QK_ROPE]
    small_red = rest_red[lay.rows["dkv"]:lay.rows["dkv"] + lay.small_rows].reshape(-1)
    off = 0
    for k in SMALL:
        n = sh[k].size
        grads[k] = small_red[off:off + n].reshape(sh[k].shape)
        off += n
    off = 0
    for k in REPL:
        n = sh[k].size
        grads[k] = rep_flat[off:off + n].reshape(sh[k].shape)
        off += n

    new = {}
    for k in ("w_in_a", "w_in_b", "w_out_a", "w_out_b"):
        view = (lambda a: jnp.swapaxes(a, 1, 2)) if k in TRANSPOSED else (lambda a: a)
        outs = _adamw_rows("adamw_" + k, view(sh[k]), view(grads[k]), view(given["m_" + k]), view(given["v_" + k]))
        new[k] = tuple(view(a) for a in outs)
    rest_names = [k for k in WEIGHTS if k not in new]

    def as2d(k, a):
        return a.T if k in TRANSPOSED else a[None, :] if a.ndim == 1 else a

    ds, ms, vs = _adamw_group([as2d(k, sh[k]) for k in rest_names], [as2d(k, grads[k]) for k in rest_names],
                              [as2d(k, given["m_" + k]) for k in rest_names], [as2d(k, given["v_" + k]) for k in rest_names])
    for n, k in enumerate(rest_names):
        new[k] = tuple((a.T if k in TRANSPOSED else a).reshape(sh[k].shape) for a in (ds[n], ms[n], vs[n]))
    return (loss, gx.reshape(nb, seq, d), *[grads[k] for k in WEIGHTS], *[new[k][0] for k in WEIGHTS], *[new[k][1] for k in WEIGHTS],
            *[new[k][2] for k in WEIGHTS])
```

```python
import jax
import jax.numpy as jnp
from jax import lax
from jax.experimental import pallas as pl
from jax.experimental.pallas import tpu as pltpu

F32, BF16 = jnp.float32, jnp.bfloat16
EPS = 1e-6
LRU_C = 8.0
ROPE_THETA = 10000.0
QK_NOPE, QK_ROPE = 128, 64
ATTN_SCALE = (QK_NOPE + QK_ROPE) ** -0.5
LN2 = 0.6931471805599453
Q_SCALE = ATTN_SCALE / LN2
ATTN_HEADS, ATTN_HEADS_BWD = 4, 2
ATTN_ROWS = 64
LANE = 128
SUBLANE = 8
ROW_ALIGN = 32
VMEM_LIMIT = 60000 * 1024
ADAM_LR, ADAM_B1, ADAM_B2, ADAM_EPS, ADAM_WD, ADAM_STEP = 0.001, 0.9, 0.999, 1e-08, 0.01, 10
MESH = pl.DeviceIdType.MESH
ANY = pl.BlockSpec(memory_space=pl.ANY)
N_CHIPS = 4
TOKENS_A, TOKENS_B, TOKENS_ATTN, TOKENS_MM = 256, 512, 512, 2048

SMALL = ("norm_a", "conv_w", "conv_b", "b_rg", "b_ig", "lru_lambda")
REPL = ("w_rg", "w_ig", "norm_kv", "kv_norm", "norm_b", "q_norm", "final_norm")
TRANSPOSED = ("w_in_b", "w_dkv")
WEIGHTS = ("norm_a", "w_in_a", "conv_w", "conv_b", "w_rg", "b_rg", "w_ig", "b_ig", "lru_lambda", "w_out_a", "norm_kv",
           "w_dkv", "kv_norm", "w_uk", "w_uv", "norm_b", "w_in_b", "q_norm", "w_uq", "w_out_b", "final_norm")
W_GROUPS = {"a": ("in_a", "out_a"), "b": ("in_b", "out_b", "uk", "uv", "uq_n", "uq_r", "dkv")}
G_GROUPS = {"early": ("in_b", "out_a", "out_b", "uk", "uv", "uq_n", "uq_r"), "late": ("in_a", "rest")}


def _sds(shape, dtype=F32):
    return jax.ShapeDtypeStruct(tuple(shape), dtype)


def _params(n_grid):
    return pltpu.CompilerParams(dimension_semantics=("arbitrary",) * n_grid, vmem_limit_bytes=VMEM_LIMIT)


def _full(shape):
    nd = len(shape)
    return pl.BlockSpec(tuple(shape), lambda *g: (0,) * nd)


def _round_up(n, k):
    return -(-n // k) * k


def _row_block(rows, cap=512):
    best = SUBLANE
    for r in range(SUBLANE, min(rows, cap) + 1, SUBLANE):
        if rows % r == 0:
            best = r
    return best


def _place():
    return lax.axis_index("x"), lax.axis_index("y"), lax.axis_index("c")


class _Layout:
    def __init__(self, d, dr, qr, kvr, hv, n_small, n_repl):
        assert hv == d, "the packed rows are D_MODEL wide, which must equal heads * 128"
        self.d, self.dr, self.qr, self.kvr, self.hv = d, dr, qr, kvr, hv
        per_chip = {"in_b": (qr + hv) // N_CHIPS, "in_a": 2 * dr // N_CHIPS, "out_a": dr // N_CHIPS, "out_b": hv // N_CHIPS,
                    "uk": kvr // N_CHIPS, "uv": kvr // N_CHIPS, "uq_n": qr // N_CHIPS, "uq_r": qr // N_CHIPS,
                    "dkv": (d // N_CHIPS) * (kvr + LANE) // d}
        assert all(r % ROW_ALIGN == 0 for r in per_chip.values()), per_chip
        self.small_rows = _round_up(-(-n_small // d), SUBLANE)
        self.repl_rows = _round_up(-(-n_repl // (N_CHIPS * d)), SUBLANE)
        per_chip["rest"] = _round_up(per_chip["dkv"] + self.small_rows + self.repl_rows, ROW_ALIGN)
        self.rows = per_chip
        self.w_off, self.w_rows = {}, {}
        for group, order in W_GROUPS.items():
            off = 0
            for k in order:
                self.w_off[k] = off
                off += per_chip[k]
            assert off % ROW_ALIGN == 0, (group, off)
            self.w_rows[group] = off
        self.g_off, self.c_off, self.c_rows, self.g_rows = {}, {}, {}, {}
        for group, order in G_GROUPS.items():
            off = 0
            for k in order:
                self.c_off[k] = off
                self.g_off[k] = N_CHIPS * off
                off += per_chip[k]
            self.c_rows[group] = off
            self.g_rows[group] = N_CHIPS * off


def _dot(a, b):
    return jnp.dot(a, b, preferred_element_type=F32)


def _dot_nt(a, b):
    return lax.dot_general(a, b, (((1,), (1,)), ((), ())), preferred_element_type=F32)


def _dot_tn(a, b):
    return lax.dot_general(a, b, (((0,), (0,)), ((), ())), preferred_element_type=F32)


def _rinv(x):
    return lax.rsqrt(jnp.mean(x * x, axis=-1, keepdims=True) + EPS)


def _rms_bwd(x, rinv, g, dy):
    z = dy * g
    dx = rinv * z - x * (rinv * rinv * rinv) * jnp.mean(z * x, axis=-1, keepdims=True)
    dg = jnp.sum(dy * (x * rinv), axis=0, keepdims=True)
    return dx, dg


def _softplus(z):
    return jnp.maximum(z, 0.0) + jnp.log1p(jnp.exp(-jnp.abs(z)))


def _sigmoid(x):
    return 0.5 * jnp.tanh(0.5 * x) + 0.5


def _decay(log_a):
    a = jnp.exp(log_a)
    a2 = a * a
    return a, a2, -jnp.tanh(log_a) * (a2 + 1.0)


def _swap_halves(x):
    w = x.shape[1]
    lane = lax.broadcasted_iota(jnp.int32, x.shape, 1)
    return jnp.where(lane % QK_ROPE < QK_ROPE // 2, pltpu.roll(x, w - QK_ROPE // 2, 1), pltpu.roll(x, QK_ROPE // 2, 1))


def _rope_tables(seq):
    pos = jnp.arange(seq, dtype=F32)
    inv = ROPE_THETA ** (-jnp.arange(0, QK_ROPE, 2, dtype=F32) / QK_ROPE)
    ang = pos[:, None] * inv[None, :]
    cos, sin = jnp.cos(ang), jnp.sin(ang)
    zero = jnp.zeros((seq, LANE - QK_ROPE), F32)
    return jnp.concatenate([cos, cos, zero], 1), jnp.concatenate([-sin, sin, zero], 1)


def _fetch(wg_ref, lay, key, dst, sems, k0):
    rows = lay.rows[key]
    return [pltpu.make_async_copy(wg_ref.at[p, pl.ds(lay.w_off[key], rows), :], dst.at[pl.ds(p * rows, rows), :], sems.at[k0 + p])
            for p in range(N_CHIPS)]


def _gates(xb, wrg_ref, brg, wig_ref, big, nblocks):
    xbb = xb.astype(BF16)
    rg = [_dot(xbb[:, n * LANE:(n + 1) * LANE], wrg_ref[n]) for n in range(nblocks)]
    ig = [_dot(xbb[:, n * LANE:(n + 1) * LANE], wig_ref[n]) for n in range(nblocks)]
    r = _sigmoid(jnp.concatenate(rg, axis=1) + brg)
    i = _sigmoid(jnp.concatenate(ig, axis=1) + big)
    return r, i


def _conv(xpad, cw_ref, cb, tb):
    return (cb + cw_ref[3:4, :] * xpad[pl.ds(8, tb), :] + cw_ref[2:3, :] * xpad[pl.ds(7, tb), :]
            + cw_ref[1:2, :] * xpad[pl.ds(6, tb), :] + cw_ref[0:1, :] * xpad[pl.ds(5, tb), :])


def _fa_fwd(x, wg, wb_half, w, lay, seq, tb):
    t_all, d = x.shape
    dr = lay.dr
    nblocks = w["w_rg"].shape[0]
    nblk = seq // tb
    nt = tb // SUBLANE
    nsteps = (t_all // seq) * nblk

    def body(x_ref, wg_ref, wbh_ref, na, cw, cb, wrg, brg, wig, big, lam, x1_ref, u_ref, hs_ref, h_ref, y_ref, xb_ref, wb_ref,
             wint, wout, xpad, a_s, b_s, carry, sems, send_sems, recv_sems, local_sem):
        step_no = pl.program_id(0) * nblk + pl.program_id(1)
        gather = _Gather8(wbh_ref, wb_ref, send_sems, recv_sems, local_sem)

        @pl.when(step_no == 0)
        def _():
            gather.start()
            cps = _fetch(wg_ref, lay, "in_a", wint, sems, 0) + _fetch(wg_ref, lay, "out_a", wout, sems, N_CHIPS)
            for cp in cps:
                cp.start()
            for cp in cps:
                cp.wait()

        @pl.when(step_no == nsteps // 2)
        def _():
            gather.forward()

        @pl.when(pl.program_id(1) == 0)
        def _():
            xpad[pl.ds(0, 8), :] = jnp.zeros((8, dr), F32)
            carry[...] = jnp.zeros((8, dr), F32)

        xv = x_ref[...]
        h = (xv * _rinv(xv) * na[...]).astype(BF16)
        h_ref[...] = h
        u = _dot_nt(h, wint[...])
        u_ref[...] = u
        xpre, gate = u[:, :dr], u[:, dr:]
        xpad[pl.ds(8, tb), :] = xpre
        xb = _conv(xpad, cw, cb[...], tb)
        xb_ref[...] = xb
        xpad[pl.ds(0, 8), :] = xpre[tb - 8:, :]
        r, i = _gates(xb, wrg, brg[...], wig, big[...], nblocks)
        log_a = -LRU_C * r * _softplus(-lam[...])
        a, _, nem = _decay(log_a)
        a_s[...] = a
        b_s[...] = jnp.sqrt(nem) * (i * xb)
        row = lax.broadcasted_iota(jnp.int32, (8, dr), 0)

        def step(t, c):
            r0 = pl.multiple_of(t * 8, 8)
            a = a_s[pl.ds(r0, 8), :]
            b = b_s[pl.ds(r0, 8), :]
            for s in (1, 2, 4):
                m = row >= s
                a_sh = jnp.where(m, pltpu.roll(a, s, 0), 1.0)
                b_sh = jnp.where(m, pltpu.roll(b, s, 0), 0.0)
                b = a * b_sh + b
                a = a * a_sh
            hh = b + a * c
            hs_ref[pl.ds(r0, 8), :] = hh
            return jnp.broadcast_to(hh[7:8, :], hh.shape)

        carry[...] = lax.fori_loop(0, nt, step, carry[...])
        y = (hs_ref[...] * (gate * _sigmoid(gate))).astype(BF16)
        y_ref[...] = y
        x1_ref[...] = xv + _dot(y, wout[...])

        @pl.when(step_no == nsteps - 1)
        def _():
            gather.finish()

    tok = lambda c: pl.BlockSpec((tb, c), lambda b, j: (b * nblk + j, 0))
    consts = [w["norm_a"], w["conv_w"], w["conv_b"], w["w_rg"], w["b_rg"], w["w_ig"], w["b_ig"], w["lru_lambda"]]
    return pl.pallas_call(
        body, name="fa_fwd", grid=(t_all // seq, nblk),
        in_specs=[tok(d), ANY, ANY] + [_full(c.shape) for c in consts],
        out_specs=[tok(d), tok(2 * dr), tok(dr), tok(d), tok(dr), tok(dr), ANY],
        out_shape=[_sds((t_all, d)), _sds((t_all, 2 * dr)), _sds((t_all, dr)), _sds((t_all, d), BF16), _sds((t_all, dr), BF16),
                   _sds((t_all, dr)), _sds((8,) + wb_half.shape, BF16)],
        scratch_shapes=[pltpu.VMEM((2 * dr, d), BF16), pltpu.VMEM((dr, d), BF16), pltpu.VMEM((tb + 8, dr), F32), pltpu.VMEM((tb, dr), F32),
                        pltpu.VMEM((tb, dr), F32), pltpu.VMEM((8, dr), F32), pltpu.SemaphoreType.DMA((2 * N_CHIPS,))] + GATHER_SEMS,
        compiler_params=_params(2),
    )(x, wg, wb_half, *consts)


def _fb_fwd(x1, wg, w, lay, cos_t, sin_t, seq, tb):
    t_all, d = x1.shape
    kvr, qr, hv = lay.kvr, lay.qr, lay.hv
    nheads = hv // LANE
    npos = seq // tb

    def body(x_ref, wg_ref, nkv, nb, wdkv, kvn, qn, cos_ref, sin_ref,
             qn_o, qr_o, kn_o, kr_o, v_o, ub_o, ckr_o, hb_o, hk_o, cq_o, ckv_o, winb, wuk, wuv, wuqn, wuqr, sems):
        @pl.when(pl.program_id(0) == 0)
        def _():
            cps = []
            for n, (key, dst) in enumerate((("in_b", winb), ("uk", wuk), ("uv", wuv), ("uq_n", wuqn), ("uq_r", wuqr))):
                cps += _fetch(wg_ref, lay, key, dst, sems, n * N_CHIPS)
            for cp in cps:
                cp.start()
            for cp in cps:
                cp.wait()

        xv = x_ref[...]
        xh = xv * _rinv(xv)
        hk = (xh * nkv[...]).astype(BF16)
        hb = (xh * nb[...]).astype(BF16)
        hk_o[...] = hk
        hb_o[...] = hb
        cos, sin = cos_ref[...], sin_ref[...]
        ckr = _dot(hk, wdkv[...])
        ckr_o[...] = ckr
        ckv_pre = ckr[:, :kvr]
        ckv = (ckv_pre * _rinv(ckv_pre) * kvn[...]).astype(BF16)
        ckv_o[...] = ckv
        kr = ckr[:, kvr:]
        kr_o[...] = (kr * cos + _swap_halves(kr) * sin).astype(BF16)
        kn_o[...] = _dot(ckv, wuk[...]).astype(BF16)
        v_o[...] = _dot(ckv, wuv[...]).astype(BF16)
        ub = _dot_nt(hb, winb[...])
        ub_o[...] = ub
        cq_pre = ub[:, :qr]
        cq = (cq_pre * _rinv(cq_pre) * qn[...]).astype(BF16)
        cq_o[...] = cq
        qn_o[...] = (_dot(cq, wuqn[...]) * Q_SCALE).astype(BF16)
        qrope = _dot(cq, wuqr[...]) * Q_SCALE
        qr_o[...] = (qrope * jnp.tile(cos, (1, nheads)) + _swap_halves(qrope) * jnp.tile(sin, (1, nheads))).astype(BF16)

    tok = lambda c: pl.BlockSpec((tb, c), lambda i: (i, 0))
    pos = pl.BlockSpec((tb, LANE), lambda i: (i % npos, 0))
    consts = [w["norm_kv"], w["norm_b"], w["w_dkv_p"], w["kv_norm"], w["q_norm"]]
    outs = [(hv, BF16), (hv, BF16), (hv, BF16), (LANE, BF16), (hv, BF16), (qr + hv, F32), (kvr + LANE, F32), (d, BF16), (d, BF16), (qr, BF16), (kvr, BF16)]
    return pl.pallas_call(
        body, name="fb_fwd", grid=(t_all // tb,),
        in_specs=[tok(d), ANY] + [_full(c.shape) for c in consts] + [pos, pos],
        out_specs=[tok(c) for c, _ in outs],
        out_shape=[_sds((t_all, c), dt) for c, dt in outs],
        scratch_shapes=[pltpu.VMEM((qr + hv, d), BF16), pltpu.VMEM((kvr, d), BF16), pltpu.VMEM((kvr, d), BF16), pltpu.VMEM((qr, d), BF16),
                        pltpu.VMEM((qr, d), BF16), pltpu.SemaphoreType.DMA((5 * N_CHIPS,))],
        compiler_params=_params(1),
    )(x1, wg, *consts, cos_t, sin_t)


def _causal_mask(row0, col0, nrows, ncols):
    rows = row0 + lax.broadcasted_iota(jnp.int32, (nrows, ncols), 0)
    cols = col0 + lax.broadcasted_iota(jnp.int32, (nrows, ncols), 1)
    return cols <= rows


def _attn_fwd(qn, qr, kn, kr, v, seq, ta):
    t_all, hv = qn.shape
    nheads, nb, na = hv // LANE, t_all // seq, seq // ta

    reps = ta // LANE
    hp = ATTN_HEADS
    wide = hp * LANE

    def body(qn_ref, qr_ref, kn_ref, kr_ref, v_ref, o_ref, lse_ref, m_s, l_s, acc_s):
        i = pl.program_id(2)
        m_s[...] = jnp.full((ta, wide), -1e30, F32)
        l_s[...] = jnp.zeros((ta, wide), F32)
        acc_s[...] = jnp.zeros((ta, wide), F32)
        heads = [slice(n * LANE, (n + 1) * LANE) for n in range(hp)]
        qs = [jnp.concatenate([qn_ref[:, hd], qr_ref[:, hd]], axis=1) for hd in heads]

        def tile(j, diagonal):
            cols = pl.ds(pl.multiple_of(j * ta, ta), ta)
            k_rope = kr_ref[cols, :]
            for q, hd in zip(qs, heads):
                k = jnp.concatenate([kn_ref[cols, hd], k_rope], axis=1)
                s = _dot_nt(q, k)
                if diagonal:
                    s = jnp.where(_causal_mask(0, 0, ta, ta), s, -1e30)
                m_prev = m_s[:, hd]
                m_new = jnp.maximum(m_prev, jnp.max(s, axis=1, keepdims=True))
                p = jnp.exp2(s - jnp.tile(m_new, (1, reps)))
                alpha = jnp.exp2(m_prev - m_new)
                l_s[:, hd] = alpha * l_s[:, hd] + jnp.sum(p, axis=1, keepdims=True)
                acc_s[:, hd] = alpha * acc_s[:, hd] + _dot(p.astype(BF16), v_ref[cols, hd])
                m_s[:, hd] = m_new

        def off_diagonal(j, carry):
            tile(j, False)
            return carry

        lax.fori_loop(0, i, off_diagonal, 0)
        tile(i, True)
        o_ref[...] = (acc_s[...] / l_s[...]).astype(BF16)
        lse_ref[...] = m_s[...] + jnp.log2(l_s[...])

    qspec = pl.BlockSpec((ta, wide), lambda b, h, i: (b * na + i, h))
    kspec = pl.BlockSpec((seq, wide), lambda b, h, i: (b, h))
    krspec = pl.BlockSpec((seq, LANE), lambda b, h, i: (b, 0))
    return pl.pallas_call(
        body, name="attn_fwd", grid=(nb, nheads // hp, na),
        in_specs=[qspec, qspec, kspec, krspec, kspec],
        out_specs=[qspec, qspec],
        out_shape=[_sds((t_all, hv), BF16), _sds((t_all, hv))],
        scratch_shapes=[pltpu.VMEM((ta, wide), F32)] * 3,
        compiler_params=_params(3),
    )(qn, qr, kn, kr, v)


def _attn_bwd(qn, qr, kn, kr, v, do, lse, delta, seq, ta):
    t_all, hv = qn.shape
    nheads, nb, na = hv // LANE, t_all // seq, seq // ta

    reps = ta // LANE
    nchunks = ta // ATTN_ROWS

    hp = ATTN_HEADS_BWD
    wide = hp * LANE
    heads = [slice(n * LANE, (n + 1) * LANE) for n in range(hp)]

    def body(qn_ref, qr_ref, kn_ref, kr_ref, v_ref, do_ref, lse_ref, dl_ref, dqn_ref, dqr_ref, dkn_ref, dkr_ref, dv_ref,
             s_s, dp_s, p_s, ds_s, dk_s, dv_s):
        j = pl.program_id(2)

        @pl.when(j == 0)
        def _():
            dqn_ref[...] = jnp.zeros((seq, wide), F32)
            dqr_ref[...] = jnp.zeros((seq, wide), F32)

        dk_s[...] = jnp.zeros((hp, ta, 2 * LANE), F32)
        dv_s[...] = jnp.zeros((hp, ta, LANE), F32)
        k_rope = kr_ref[...]
        ks = [jnp.concatenate([kn_ref[:, hd], k_rope], axis=1) for hd in heads]

        def tile(i, diagonal):
            rows_i = pl.ds(pl.multiple_of(i * ta, ta), ta)
            for n, hd in enumerate(heads):
                q = jnp.concatenate([qn_ref[rows_i, hd], qr_ref[rows_i, hd]], axis=1)
                do_b = do_ref[rows_i, hd]
                s_s[n] = _dot_nt(q, ks[n])
                dp_s[n] = _dot_nt(do_b, v_ref[:, hd])
                for c in range(nchunks):
                    rows = pl.ds(c * ATTN_ROWS, ATTN_ROWS)
                    seq_rows = pl.ds(pl.multiple_of(i * ta + c * ATTN_ROWS, ATTN_ROWS), ATTN_ROWS)
                    s = s_s[n, rows, :]
                    if diagonal:
                        s = jnp.where(_causal_mask(c * ATTN_ROWS, 0, ATTN_ROWS, ta), s, -1e30)
                    p = jnp.exp2(s - jnp.tile(lse_ref[seq_rows, hd], (1, reps)))
                    p_s[n, rows, :] = p.astype(BF16)
                    ds_s[n, rows, :] = (p * (dp_s[n, rows, :] - jnp.tile(dl_ref[seq_rows, hd], (1, reps)))).astype(BF16)
                dv_s[n] += _dot_tn(p_s[n], do_b)
                ds = ds_s[n]
                dk_s[n] += _dot_tn(ds, q)
                dq = _dot(ds, ks[n])
                dqn_ref[rows_i, hd] += dq[:, :LANE]
                dqr_ref[rows_i, hd] += dq[:, LANE:]

        def off_diagonal(i, carry):
            tile(i, False)
            return carry

        tile(j, True)
        lax.fori_loop(j + 1, na, off_diagonal, 0)
        for n, hd in enumerate(heads):
            dkn_ref[:, hd] = (dk_s[n, :, :LANE] * LN2).astype(BF16)
            dkr_ref[:, hd] = dk_s[n, :, LANE:] * LN2
            dv_ref[:, hd] = dv_s[n].astype(BF16)

    qspec = pl.BlockSpec((seq, wide), lambda b, h, j: (b, h))
    kspec = pl.BlockSpec((ta, wide), lambda b, h, j: (b * na + j, h))
    krspec = pl.BlockSpec((ta, LANE), lambda b, h, j: (b * na + j, 0))
    return pl.pallas_call(
        body, name="attn_bwd", grid=(nb, nheads // hp, na),
        in_specs=[qspec, qspec, kspec, krspec, kspec, qspec, qspec, qspec],
        out_specs=[qspec, qspec, kspec, kspec, kspec],
        out_shape=[_sds((t_all, hv)), _sds((t_all, hv)), _sds((t_all, hv), BF16), _sds((t_all, hv)), _sds((t_all, hv), BF16)],
        scratch_shapes=[pltpu.VMEM((hp, ta, ta), F32), pltpu.VMEM((hp, ta, ta), F32), pltpu.VMEM((hp, ta, ta), BF16), pltpu.VMEM((hp, ta, ta), BF16),
                        pltpu.VMEM((hp, ta, 2 * LANE), F32), pltpu.VMEM((hp, ta, LANE), F32)],
        compiler_params=_params(3),
    )(qn, qr, kn, kr, v, do, lse, delta)


def _head(o, ub, x1, target, wg, w, lay, tb):
    t_all, d = x1.shape
    hv, qr = lay.hv, lay.qr
    nheads = hv // LANE

    def body(o_ref, ub_ref, x1_ref, tg_ref, wg_ref, gf, loss_ref, dgf_ref, yb_ref, dx2_ref, do_ref, dg_ref, dl_ref, wob, sems):
        @pl.when(pl.program_id(0) == 0)
        def _():
            cps = _fetch(wg_ref, lay, "out_b", wob, sems, 0)
            for cp in cps:
                cp.start()
            loss_ref[...] = jnp.zeros((1, LANE), F32)
            dgf_ref[...] = jnp.zeros((1, d), F32)
            for cp in cps:
                cp.wait()

        ov = o_ref[...].astype(F32)
        g = ub_ref[:, qr:]
        sg = _sigmoid(g)
        silu = g * sg
        yb = (ov * silu).astype(BF16)
        yb_ref[...] = yb
        x2 = x1_ref[...] + _dot(yb, wob[...])
        rinv = _rinv(x2)
        err = x2 * rinv * gf[...] - tg_ref[...]
        loss_ref[...] += (0.5 / d) * jnp.sum(jnp.sum(err * err, axis=1, keepdims=True), axis=0, keepdims=True)
        dx2, dgf = _rms_bwd(x2, rinv, gf[...], err * (1.0 / d))
        dgf_ref[...] += dgf
        dx2_ref[...] = dx2
        dyb = _dot_nt(dx2.astype(BF16), wob[...])
        dov = dyb * silu
        do_ref[...] = dov.astype(BF16)
        dg_ref[...] = (dyb * ov * (sg * (1.0 + g * (1.0 - sg)))).astype(BF16)
        prod = dov * ov
        dl_ref[...] = jnp.concatenate(
            [jnp.broadcast_to(jnp.sum(prod[:, n * LANE:(n + 1) * LANE], axis=1, keepdims=True), (tb, LANE)) for n in range(nheads)], axis=1)

    tok = lambda c: pl.BlockSpec((tb, c), lambda i: (i, 0))
    return pl.pallas_call(
        body, name="head", grid=(t_all // tb,),
        in_specs=[tok(hv), tok(qr + hv), tok(d), tok(d), ANY, _full((1, d))],
        out_specs=[_full((1, LANE)), _full((1, d)), tok(hv), tok(d), tok(hv), tok(hv), tok(hv)],
        out_shape=[_sds((1, LANE)), _sds((1, d)), _sds((t_all, hv), BF16), _sds((t_all, d)), _sds((t_all, hv), BF16), _sds((t_all, hv), BF16),
                   _sds((t_all, hv))],
        scratch_shapes=[pltpu.VMEM((hv, d), BF16), pltpu.SemaphoreType.DMA((N_CHIPS,))],
        compiler_params=_params(1),
    )(o, ub, x1, target, wg, w["final_norm"])


def _fb_bwd(dqn, dqr, dkn, dkr, dv, dgate, ub, ckr, x1, dx2, wg, w, lay, cos_t, sin_t, seq, tb):
    t_all, d = x1.shape
    hv, qr, kvr = lay.hv, lay.qr, lay.kvr
    nheads = hv // LANE
    npos = seq // tb

    def body(dqn_ref, dqr_ref, dkn_ref, dkr_ref, dv_ref, dg_ref, ub_ref, ckr_ref, x1_ref, dx2_ref, wg_ref,
             qn, nb, kvn, wdkv, nkv, cos_ref, sin_ref,
             dx1_ref, dqrp_ref, dqnp_ref, dub_ref, dckr_ref, dqn_g, dnb_g, dkvn_g, dnkv_g, winb, wuk, wuv, wuqn, wuqr, sems):
        @pl.when(pl.program_id(0) == 0)
        def _():
            cps = []
            for n, (key, dst) in enumerate((("in_b", winb), ("uk", wuk), ("uv", wuv), ("uq_n", wuqn), ("uq_r", wuqr))):
                cps += _fetch(wg_ref, lay, key, dst, sems, n * N_CHIPS)
            for cp in cps:
                cp.start()
            dqn_g[...] = jnp.zeros((1, qr), F32)
            dnb_g[...] = jnp.zeros((1, d), F32)
            dkvn_g[...] = jnp.zeros((1, kvr), F32)
            dnkv_g[...] = jnp.zeros((1, d), F32)
            for cp in cps:
                cp.wait()

        cos, sin = cos_ref[...], sin_ref[...]
        xv = x1_ref[...]
        rinv1 = _rinv(xv)
        dqr_v = dqr_ref[...] * ATTN_SCALE
        dqr_pre = (dqr_v * jnp.tile(cos, (1, nheads)) + _swap_halves(dqr_v * jnp.tile(sin, (1, nheads)))).astype(BF16)
        dqrp_ref[...] = dqr_pre
        dqn_pre = (dqn_ref[...] * ATTN_SCALE).astype(BF16)
        dqnp_ref[...] = dqn_pre
        dcq = _dot_nt(dqn_pre, wuqn[...]) + _dot_nt(dqr_pre, wuqr[...])
        cq_pre = ub_ref[:, :qr]
        dcq_pre, g1 = _rms_bwd(cq_pre, _rinv(cq_pre), qn[...], dcq)
        dqn_g[...] += g1
        dub = jnp.concatenate([dcq_pre.astype(BF16), dg_ref[...]], axis=1)
        dub_ref[...] = dub
        dx1_b, g2 = _rms_bwd(xv, rinv1, nb[...], _dot(dub, winb[...]))
        dnb_g[...] += g2
        dkr_all = dkr_ref[...]
        dkr_sum = dkr_all[:, :LANE]
        for n in range(1, nheads):
            dkr_sum = dkr_sum + dkr_all[:, n * LANE:(n + 1) * LANE]
        dckr_rope = dkr_sum * cos + _swap_halves(dkr_sum * sin)
        dckv = _dot_nt(dkn_ref[...].astype(BF16), wuk[...]) + _dot_nt(dv_ref[...].astype(BF16), wuv[...])
        ckv_pre = ckr_ref[:, :kvr]
        dckv_pre, g3 = _rms_bwd(ckv_pre, _rinv(ckv_pre), kvn[...], dckv)
        dkvn_g[...] += g3
        dckr = jnp.concatenate([dckv_pre, dckr_rope], axis=1).astype(BF16)
        dckr_ref[...] = dckr
        dx1_kv, g4 = _rms_bwd(xv, rinv1, nkv[...], _dot_nt(dckr, wdkv[...]))
        dnkv_g[...] += g4
        dx1_ref[...] = dx2_ref[...] + dx1_b + dx1_kv

    tok = lambda c: pl.BlockSpec((tb, c), lambda i: (i, 0))
    pos = pl.BlockSpec((tb, LANE), lambda i: (i % npos, 0))
    consts = [w["q_norm"], w["norm_b"], w["kv_norm"], w["w_dkv_p"], w["norm_kv"]]
    return pl.pallas_call(
        body, name="fb_bwd", grid=(t_all // tb,),
        in_specs=[tok(hv)] * 6 + [tok(qr + hv), tok(kvr + LANE), tok(d), tok(d), ANY] + [_full(c.shape) for c in consts] + [pos, pos],
        out_specs=[tok(d), tok(hv), tok(hv), tok(qr + hv), tok(kvr + LANE), _full((1, qr)), _full((1, d)), _full((1, kvr)), _full((1, d))],
        out_shape=[_sds((t_all, d)), _sds((t_all, hv), BF16), _sds((t_all, hv), BF16), _sds((t_all, qr + hv), BF16), _sds((t_all, kvr + LANE), BF16),
                   _sds((1, qr)), _sds((1, d)), _sds((1, kvr)), _sds((1, d))],
        scratch_shapes=[pltpu.VMEM((qr + hv, d), BF16), pltpu.VMEM((kvr, d), BF16), pltpu.VMEM((kvr, d), BF16), pltpu.VMEM((qr, d), BF16),
                        pltpu.VMEM((qr, d), BF16), pltpu.SemaphoreType.DMA((5 * N_CHIPS,))],
        compiler_params=_params(1),
    )(dqn, dqr, dkn, dkr, dv, dgate, ub, ckr, x1, dx2, wg, *consts, cos_t, sin_t)


def _fa_bwd(dx1, x, u, xb, hs, wg, g16, g32, w, lay, seq, tb):
    t_all, d = x.shape
    dr = lay.dr
    nblocks = w["w_rg"].shape[0]
    nblk = seq // tb
    nt = tb // SUBLANE
    per8 = tb // 8

    def body(dx1_ref, x_ref, u_ref, xb_ref, hs_ref, hh_ref, wg_ref, g16_ref, g32_ref, na, cw, wrg, brg, wig, big, lam,
             gx_ref, du_ref, dna_g, dcw_g, dcb_g, dbrg_g, dbig_g, dlam_g, dwrg_g, dwig_g, got_ref, sib_ref, own_ref,
             wint, wout, hpad, a_s, d_s, g_s, dxpad, carry, sems, send_sems, recv_sems, local_sem):
        b, jj = pl.program_id(0), pl.program_id(1)
        first_block = jj == nblk - 1
        scatter = _ScatterDirect(g16_ref, g32_ref, got_ref, sib_ref, own_ref, send_sems, recv_sems, local_sem, lay, G_GROUPS["early"])

        @pl.when((b == 0) & (jj == 0))
        def _():
            scatter.start()
            cps = _fetch(wg_ref, lay, "in_a", wint, sems, 0) + _fetch(wg_ref, lay, "out_a", wout, sems, N_CHIPS)
            for cp in cps:
                cp.start()
            dna_g[...] = jnp.zeros((1, d), F32)
            dcw_g[...] = jnp.zeros((4, dr), F32)
            dcb_g[...] = jnp.zeros((1, dr), F32)
            dbrg_g[...] = jnp.zeros((1, dr), F32)
            dbig_g[...] = jnp.zeros((1, dr), F32)
            dlam_g[...] = jnp.zeros((1, dr), F32)
            dwrg_g[...] = jnp.zeros((nblocks, LANE, LANE), F32)
            dwig_g[...] = jnp.zeros((nblocks, LANE, LANE), F32)
            for cp in cps:
                cp.wait()

        @pl.when(jj == 0)
        def _():
            dxpad[pl.ds(tb, 8), :] = jnp.zeros((8, dr), F32)
            carry[...] = jnp.zeros((8, dr), F32)

        keep = jnp.where(first_block, 0.0, 1.0)
        dx1v = dx1_ref[...]
        gate = u_ref[:, dr:]
        xpre = u_ref[:, :dr]
        hpad[pl.ds(0, 8), :] = hh_ref[...] * keep
        hpad[pl.ds(8, tb), :] = hs_ref[...]
        xb = xb_ref[...]
        xbb = xb.astype(BF16)
        r, i = _gates(xb, wrg, brg[...], wig, big[...], nblocks)
        sp = _softplus(-lam[...])
        log_a = -LRU_C * r * sp
        a, a2, nem = _decay(log_a)
        mult = jnp.sqrt(nem)
        sg = _sigmoid(gate)
        dy = _dot_nt(dx1v.astype(BF16), wout[...])
        hsv = hs_ref[...]
        dgate = dy * hsv * (sg * (1.0 + gate * (1.0 - sg)))
        a_s[...] = a
        d_s[...] = dy * (gate * sg)
        row = lax.broadcasted_iota(jnp.int32, (8, dr), 0)

        def step(k, c):
            r0 = pl.multiple_of((nt - 1 - k) * 8, 8)
            av = a_s[pl.ds(r0, 8), :]
            dv = d_s[pl.ds(r0, 8), :]
            qv = av * dv
            for s in (1, 2, 4):
                m = row < 8 - s
                a_sh = jnp.where(m, pltpu.roll(av, 8 - s, 0), 1.0)
                q_sh = jnp.where(m, pltpu.roll(qv, 8 - s, 0), 0.0)
                qv = qv + av * q_sh
                av = av * a_sh
            qv = qv + av * c
            g_s[pl.ds(r0, 8), :] = dv + jnp.where(row < 7, pltpu.roll(qv, 7, 0), c)
            return jnp.broadcast_to(qv[0:1, :], qv.shape)

        carry[...] = lax.fori_loop(0, nt, step, carry[...])
        g = g_s[...]
        ix = i * xb
        dlog_a = g * (hpad[pl.ds(7, tb), :] * a - ix * (a2 * lax.rsqrt(nem)))
        dix = g * mult
        dlam_g[...] += -jax.nn.sigmoid(-lam[...]) * jnp.sum(dlog_a * (-LRU_C * r), axis=0, keepdims=True)
        drg = dlog_a * (-LRU_C * sp) * r * (1.0 - r)
        dig = dix * xb * i * (1.0 - i)
        dbrg_g[...] += jnp.sum(drg, axis=0, keepdims=True)
        dbig_g[...] += jnp.sum(dig, axis=0, keepdims=True)
        drgb, digb = drg.astype(BF16), dig.astype(BF16)
        back = []
        for n in range(nblocks):
            cols = slice(n * LANE, (n + 1) * LANE)
            dwrg_g[n] += _dot_tn(xbb[:, cols], drgb[:, cols])
            dwig_g[n] += _dot_tn(xbb[:, cols], digb[:, cols])
            back.append(_dot_nt(drgb[:, cols], wrg[n]) + _dot_nt(digb[:, cols], wig[n]))
        dxb = dix * i + jnp.concatenate(back, axis=1)
        dcb_g[...] += jnp.sum(dxb, axis=0, keepdims=True)
        dxpad[pl.ds(0, tb), :] = dxb
        later = [dxb, dxpad[pl.ds(1, tb), :], dxpad[pl.ds(2, tb), :], dxpad[pl.ds(3, tb), :]]
        dxpad[pl.ds(tb, 8), :] = dxb[:8, :]
        dxpre = cw[3:4, :] * later[0] + cw[2:3, :] * later[1] + cw[1:2, :] * later[2] + cw[0:1, :] * later[3]
        for m in range(4):
            dcw_g[3 - m:4 - m, :] += jnp.sum(later[m] * xpre, axis=0, keepdims=True)
        du = jnp.concatenate([dxpre, dgate], axis=1).astype(BF16)
        du_ref[...] = du
        xv = x_ref[...]
        dxa, g1 = _rms_bwd(xv, _rinv(xv), na[...], _dot(du, wint[...]))
        dna_g[...] += g1
        gx_ref[...] = dx1v + dxa

        @pl.when((b == t_all // seq - 1) & (jj == nblk - 1))
        def _():
            scatter.finish()

    blk = lambda b, j: b * nblk + (nblk - 1 - j)
    tok = lambda c: pl.BlockSpec((tb, c), lambda b, j: (blk(b, j), 0))
    halo = pl.BlockSpec((8, dr), lambda b, j: (jnp.maximum(blk(b, j) * per8 - 1, 0), 0))
    consts = [w["norm_a"], w["conv_w"], w["w_rg"], w["b_rg"], w["w_ig"], w["b_ig"], w["lru_lambda"]]
    vec = lambda c: _full((1, c))
    blocks3 = (nblocks, LANE, LANE)
    return pl.pallas_call(
        body, name="fa_bwd", grid=(t_all // seq, nblk),
        in_specs=[tok(d), tok(d), tok(2 * dr), tok(dr), tok(dr), halo, ANY, ANY, ANY] + [_full(c.shape) for c in consts],
        out_specs=[tok(d), tok(2 * dr), vec(d), _full((4, dr)), vec(dr), vec(dr), vec(dr), vec(dr), _full(blocks3), _full(blocks3), ANY, ANY, ANY],
        out_shape=[_sds((t_all, d)), _sds((t_all, 2 * dr), BF16), _sds((1, d)), _sds((4, dr)), _sds((1, dr)), _sds((1, dr)), _sds((1, dr)),
                   _sds((1, dr)), _sds(blocks3), _sds(blocks3)] + _scatter_direct_shapes(lay, "early"),
        scratch_shapes=[pltpu.VMEM((2 * dr, d), BF16), pltpu.VMEM((dr, d), BF16), pltpu.VMEM((tb + 8, dr), F32),
                        pltpu.VMEM((tb, dr), F32), pltpu.VMEM((tb, dr), F32), pltpu.VMEM((tb, dr), F32), pltpu.VMEM((tb + 8, dr), F32),
                        pltpu.VMEM((8, dr), F32), pltpu.SemaphoreType.DMA((2 * N_CHIPS,))] + SCATTER_DIRECT_SEMS,
        compiler_params=_params(2),
    )(dx1, x, u, xb, hs, hs, wg, g16, g32, *consts)


def _mm_into(gbufs, a, b, off, name, bt):
    t_all, m = a.shape
    n = b.shape[1]
    nsplit = 2 if m >= 1024 and (m // 2) % LANE == 0 else 1
    mh = m // nsplit
    nt = t_all // bt
    nbuf = len(gbufs)
    twin = nbuf == 2

    def body(a_ref, b_ref, *refs):
        outs, acc, sems = refs[nbuf:2 * nbuf], refs[2 * nbuf], refs[-1]
        acc16 = refs[2 * nbuf + 1] if twin else None
        part, t = pl.program_id(0), pl.program_id(1)

        def out_copies(h):
            dst = pl.ds(off + h * mh, mh)
            copies = [pltpu.make_async_copy(acc.at[h], outs[0].at[dst, :], sems.at[0, h])]
            if twin:
                copies.append(pltpu.make_async_copy(acc16.at[h], outs[1].at[dst, :], sems.at[1, h]))
            return copies

        prod = _dot_tn(a_ref[...].astype(BF16), b_ref[...].astype(BF16))
        for h in range(nsplit):
            @pl.when((part == h) & (t == 0))
            def _():
                acc[h] = prod

            @pl.when((part == h) & (t > 0))
            def _():
                acc[h] += prod

            @pl.when((part == h) & (t == nt - 1))
            def _():
                if twin:
                    acc16[h] = acc[h].astype(BF16)
                for cp in out_copies(h):
                    cp.start()

        @pl.when((part == nsplit - 1) & (t == nt - 1))
        def _():
            for h in range(nsplit):
                for cp in out_copies(h):
                    cp.wait()

    scratch = [pltpu.VMEM((nsplit, mh, n), F32)] + ([pltpu.VMEM((nsplit, mh, n), BF16)] if twin else []) + [pltpu.SemaphoreType.DMA((2, nsplit))]
    return pl.pallas_call(
        body, name=name, grid=(nsplit, nt),
        in_specs=[pl.BlockSpec((bt, mh), lambda h, t: (t, h)), pl.BlockSpec((bt, n), lambda h, t: (t, 0))] + [ANY] * nbuf,
        out_specs=[ANY] * nbuf, out_shape=[_sds(g.shape, g.dtype) for g in gbufs], input_output_aliases={2 + k: k for k in range(nbuf)},
        scratch_shapes=scratch, compiler_params=_params(2),
    )(a, b, *gbufs)


def _dw_in_a_exchange(du, h, rest, lay, bt):
    t_all, d = h.shape
    half = d // 2
    rows, rest_rows, c_rows = lay.rows["in_a"], lay.rows["rest"], lay.c_rows["late"]
    c_in, c_rest = lay.c_off["in_a"], lay.c_off["rest"]
    nt = t_all // bt
    xi, yi, _ = _place()
    order = jnp.stack([2 * (1 - xi) + yi, 2 * xi + (1 - yi), 2 * (1 - xi) + (1 - yi), 2 * xi + yi]).astype(jnp.int32)

    def body(order_ref, a_ref, b_ref, rest_ref, got_ref, own_ref, acc, sibbuf, part16, restv, rest_sib, rest_p, rest16, own_v, own_r,
             d2d_send, d2d_recv, ici_send, ici_recv, local_sems):
        x, y, c = _place()
        chips = [(1 - x, y), (x, 1 - y), (1 - x, 1 - y)]
        g, t = pl.program_id(0), pl.program_id(1)
        their_cols = pl.ds(pl.multiple_of((1 - c) * half, LANE), half)

        def my_half(v):
            return jnp.where(c == 0, v[:, :half], v[:, half:])

        def d2d(src, dst, k):
            return pltpu.make_async_remote_copy(src_ref=src, dst_ref=dst, send_sem=d2d_send.at[k], recv_sem=d2d_recv.at[k],
                                                device_id=(x, y, 1 - c), device_id_type=MESH)

        def group_swap(gg):
            return d2d(acc.at[gg % 2, :, their_cols], sibbuf.at[gg], gg)

        def rest_swap():
            return d2d(restv.at[:, their_cols], rest_sib, 4)

        def to_chip(k, src, off, nrows):
            px, py = chips[k]
            return pltpu.make_async_remote_copy(src_ref=src, dst_ref=got_ref.at[k, pl.ds(off, nrows), :], send_sem=ici_send.at[k],
                                                recv_sem=ici_recv.at[k], device_id=(px, py, c), device_id_type=MESH)

        def own_copy(src, off, nrows, k):
            return pltpu.make_async_copy(src, own_ref.at[pl.ds(off, nrows), :], local_sems.at[k])

        def finish_group(gg):
            group_swap(gg).wait()
            part = my_half(acc[gg % 2]) + sibbuf[gg]
            if gg < 3:
                part16[gg] = part.astype(BF16)
                to_chip(gg, part16.at[gg], c_in, rows).start()
            else:
                own_v[...] = part
                own_copy(own_v, c_in, rows, 1).start()

        @pl.when((g == 0) & (t == 0))
        def _():
            load = pltpu.make_async_copy(rest_ref, restv, local_sems.at[0])
            load.start()
            load.wait()
            rest_swap().start()

        prod = _dot_tn(a_ref[...], b_ref[...])
        for gg in range(4):
            @pl.when((g == gg) & (t == 0))
            def _():
                acc[gg % 2] = prod

            @pl.when((g == gg) & (t > 0))
            def _():
                acc[gg % 2] += prod

            @pl.when((g == gg) & (t == nt - 1))
            def _():
                group_swap(gg).start()
                if gg == 0:
                    rest_swap().wait()
                    rest_p[...] = my_half(restv[...]) + rest_sib[...]
                    for k in range(3):
                        chip_rows = pl.ds(pl.multiple_of(order_ref[k] * rest_rows, SUBLANE), rest_rows)
                        rest16[k] = rest_p[chip_rows, :].astype(BF16)
                        to_chip(k, rest16.at[k], c_rest, rest_rows).start()
                    own_r[...] = rest_p[pl.ds(pl.multiple_of(order_ref[3] * rest_rows, SUBLANE), rest_rows), :]
                    own_copy(own_r, c_rest, rest_rows, 2).start()
                else:
                    finish_group(gg - 1)
                if gg == 3:
                    finish_group(3)
                    for k, (px, py) in enumerate(chips):
                        pltpu.make_async_remote_copy(src_ref=got_ref.at[k], dst_ref=got_ref.at[k], send_sem=ici_send.at[k], recv_sem=ici_recv.at[k],
                                                     device_id=(px, py, c), device_id_type=MESH).wait()
                    own_copy(own_v, c_in, rows, 1).wait()
                    own_copy(own_r, c_rest, rest_rows, 2).wait()

    return pl.pallas_call(
        body, name="dw_in_a",
        grid_spec=pltpu.PrefetchScalarGridSpec(
            num_scalar_prefetch=1, grid=(N_CHIPS, nt),
            in_specs=[pl.BlockSpec((bt, rows), lambda g, t, order: (t, order[g])), pl.BlockSpec((bt, d), lambda g, t, order: (t, 0)), ANY],
            out_specs=[ANY, ANY],
            scratch_shapes=[pltpu.VMEM((2, rows, d), F32), pltpu.VMEM((N_CHIPS, rows, half), F32), pltpu.VMEM((3, rows, half), BF16),
                            pltpu.VMEM((N_CHIPS * rest_rows, d), F32), pltpu.VMEM((N_CHIPS * rest_rows, half), F32),
                            pltpu.VMEM((N_CHIPS * rest_rows, half), F32), pltpu.VMEM((3, rest_rows, half), BF16),
                            pltpu.VMEM((rows, half), F32), pltpu.VMEM((rest_rows, half), F32),
                            pltpu.SemaphoreType.DMA((5,)), pltpu.SemaphoreType.DMA((5,)), pltpu.SemaphoreType.DMA((3,)), pltpu.SemaphoreType.DMA((3,)),
                            pltpu.SemaphoreType.DMA((3,))]),
        out_shape=_scatter_shapes(lay, "late", half), compiler_params=_params(2),
    )(order, du, h, rest)


def _mm_tn(a, b, name, bt):
    t_all, m = a.shape
    n = b.shape[1]

    def body(a_ref, b_ref, o_ref):
        @pl.when(pl.program_id(0) == 0)
        def _():
            o_ref[...] = jnp.zeros((m, n), F32)

        o_ref[...] += _dot_tn(a_ref[...].astype(BF16), b_ref[...].astype(BF16))

    return pl.pallas_call(
        body, name=name, grid=(t_all // bt,),
        in_specs=[pl.BlockSpec((bt, m), lambda t: (t, 0)), pl.BlockSpec((bt, n), lambda t: (t, 0))],
        out_specs=_full((m, n)), out_shape=_sds((m, n)),
        compiler_params=_params(1),
    )(a, b)


class _Gather8:
    def __init__(self, x_ref, out_ref, send_sems, recv_sems, local_sem):
        x, y, c = _place()
        self.c, self.me, self.sibling = c, (x, y, c), (x, y, 1 - c)
        self.chips = [(1 - x, y), (x, 1 - y), (1 - x, 1 - y)]
        self.x_ref, self.out_ref, self.send_sems, self.recv_sems, self.local_sem = x_ref, out_ref, send_sems, recv_sems, local_sem

    def _slot(self, px, py, pc):
        return self.out_ref.at[4 * px + 2 * py + pc]

    def _copy(self, k, blk, to, src=None):
        return pltpu.make_async_remote_copy(
            src_ref=self._slot(*blk) if src is None else src, dst_ref=self._slot(*blk), send_sem=self.send_sems.at[k],
            recv_sem=self.recv_sems.at[k], device_id=to, device_id_type=MESH)

    def _mine(self):
        return pltpu.make_async_copy(self.x_ref, self._slot(*self.me), self.local_sem)

    def _first(self):
        return [self._copy(0, self.me, self.sibling, src=self.x_ref)] + [
            self._copy(1 + j, self.me, (*chip, self.c), src=self.x_ref) for j, chip in enumerate(self.chips)]

    def _passed(self):
        return [self._copy(4 + j, (*chip, self.c), self.sibling) for j, chip in enumerate(self.chips)]

    def start(self):
        self._mine().start()
        for cp in self._first():
            cp.start()

    def forward(self):
        passed = self._passed()
        for j, chip in enumerate(self.chips):
            self._copy(1 + j, (*chip, self.c), self.me).wait_recv()
            passed[j].start()

    def finish(self):
        self._copy(0, self.sibling, self.me).wait_recv()
        for j, chip in enumerate(self.chips):
            self._copy(4 + j, (*chip, 1 - self.c), self.me).wait_recv()
        for cp in self._first() + self._passed():
            cp.wait_send()
        self._mine().wait()


class _Scatter:
    def __init__(self, p16_ref, p32_ref, got_ref, own_ref, send_sems, recv_sems, local_sem, lay, order):
        self.x, self.y, self.c = _place()
        self.chips = [(1 - self.x, self.y), (self.x, 1 - self.y), (1 - self.x, 1 - self.y)]
        self.refs = (p16_ref, p32_ref, got_ref, own_ref, send_sems, recv_sems, local_sem)
        self.lay, self.order = lay, order

    def _rows_of(self, ref, key, chip):
        start = pl.multiple_of(self.lay.g_off[key] + chip * self.lay.rows[key], ROW_ALIGN)
        return ref.at[pl.ds(start, self.lay.rows[key]), :]

    def _compact(self, ref, key):
        return ref.at[pl.ds(self.lay.c_off[key], self.lay.rows[key]), :]

    def start(self):
        p16_ref, p32_ref, got_ref, own_ref, send_sems, recv_sems, local_sem = self.refs
        for key in self.order:
            pltpu.make_async_copy(self._rows_of(p32_ref, key, 2 * self.x + self.y), self._compact(own_ref, key), local_sem).start()
        for k, (px, py) in enumerate(self.chips):
            for key in self.order:
                pltpu.make_async_remote_copy(
                    src_ref=self._rows_of(p16_ref, key, 2 * px + py), dst_ref=self._compact(got_ref.at[k], key), send_sem=send_sems.at[k],
                    recv_sem=recv_sems.at[k], device_id=(px, py, self.c), device_id_type=MESH).start()

    def finish(self):
        _, _, got_ref, own_ref, send_sems, recv_sems, local_sem = self.refs
        for k, (px, py) in enumerate(self.chips):
            pltpu.make_async_remote_copy(src_ref=got_ref.at[k], dst_ref=got_ref.at[k], send_sem=send_sems.at[k], recv_sem=recv_sems.at[k],
                                         device_id=(px, py, self.c), device_id_type=MESH).wait()
        pltpu.make_async_copy(own_ref, own_ref, local_sem).wait()


class _ScatterDirect:
    def __init__(self, g16_ref, g32_ref, got_ref, sib_ref, own_ref, send_sems, recv_sems, local_sem, lay, order):
        self.x, self.y, self.c = _place()
        self.chips = [(1 - self.x, self.y), (self.x, 1 - self.y), (1 - self.x, 1 - self.y)]
        self.refs = (g16_ref, g32_ref, got_ref, sib_ref, own_ref, send_sems, recv_sems, local_sem)
        self.lay, self.order, self.half = lay, order, lay.d // 2

    def _src(self, ref, key, chip, h):
        start = pl.multiple_of(self.lay.g_off[key] + chip * self.lay.rows[key], ROW_ALIGN)
        return ref.at[pl.ds(start, self.lay.rows[key]), pl.ds(pl.multiple_of(h * self.half, LANE), self.half)]

    def _compact(self, ref, key):
        return ref.at[pl.ds(self.lay.c_off[key], self.lay.rows[key]), :]

    def start(self):
        g16_ref, g32_ref, got_ref, sib_ref, own_ref, send_sems, recv_sems, local_sem = self.refs
        x, y, c = self.x, self.y, self.c
        for key in self.order:
            pltpu.make_async_copy(self._src(g32_ref, key, 2 * x + y, c), self._compact(own_ref, key), local_sem).start()
            pltpu.make_async_remote_copy(
                src_ref=self._src(g32_ref, key, 2 * x + y, 1 - c), dst_ref=self._compact(sib_ref, key), send_sem=send_sems.at[6],
                recv_sem=recv_sems.at[6], device_id=(x, y, 1 - c), device_id_type=MESH).start()
        for k, (px, py) in enumerate(self.chips):
            for h in range(2):
                for key in self.order:
                    pltpu.make_async_remote_copy(
                        src_ref=self._src(g16_ref, key, 2 * px + py, h), dst_ref=self._compact(got_ref.at[2 * k + c], key),
                        send_sem=send_sems.at[2 * k + h], recv_sem=recv_sems.at[2 * k + c], device_id=(px, py, h), device_id_type=MESH).start()

    def finish(self):
        _, _, got_ref, sib_ref, own_ref, send_sems, recv_sems, local_sem = self.refs
        x, y, c = self.x, self.y, self.c
        for k, (px, py) in enumerate(self.chips):
            for h in range(2):
                whole = pltpu.make_async_remote_copy(src_ref=got_ref.at[2 * k + h], dst_ref=got_ref.at[2 * k + h], send_sem=send_sems.at[2 * k + h],
                                                     recv_sem=recv_sems.at[2 * k + h], device_id=(px, py, h), device_id_type=MESH)
                whole.wait_send()
                whole.wait_recv()
        pltpu.make_async_remote_copy(src_ref=sib_ref, dst_ref=sib_ref, send_sem=send_sems.at[6], recv_sem=recv_sems.at[6],
                                     device_id=(x, y, 1 - c), device_id_type=MESH).wait()
        pltpu.make_async_copy(own_ref, own_ref, local_sem).wait()


def _scatter_direct_shapes(lay, group):
    rows, half = lay.c_rows[group], lay.d // 2
    return [_sds((6, rows, half), BF16), _sds((rows, half), F32), _sds((rows, half), F32)]


SCATTER_DIRECT_SEMS = [pltpu.SemaphoreType.DMA((7,)), pltpu.SemaphoreType.DMA((7,)), pltpu.SemaphoreType.DMA]
SCATTER_SEMS = [pltpu.SemaphoreType.DMA((3,)), pltpu.SemaphoreType.DMA((3,)), pltpu.SemaphoreType.DMA]
GATHER_SEMS = [pltpu.SemaphoreType.DMA((7,)), pltpu.SemaphoreType.DMA((7,)), pltpu.SemaphoreType.DMA]


def _all_gather8(blocks, name):
    nb = len(blocks)

    def body(*refs):
        x_refs, out_refs = refs[:nb], refs[nb:2 * nb]
        send_sems, recv_sems, local_sems = refs[2 * nb:]
        gathers = [_Gather8(x_refs[n], out_refs[n], send_sems.at[n], recv_sems.at[n], local_sems.at[n]) for n in range(nb)]
        for g in gathers:
            g.start()
        for g in gathers:
            g.forward()
        for g in gathers:
            g.finish()

    return pl.pallas_call(
        body, name=name, out_shape=[_sds((8,) + b.shape, b.dtype) for b in blocks], in_specs=[ANY] * nb, out_specs=[ANY] * nb,
        scratch_shapes=[pltpu.SemaphoreType.DMA((nb, 7)), pltpu.SemaphoreType.DMA((nb, 7)), pltpu.SemaphoreType.DMA((nb,))],
    )(*blocks)


def _swap_sibling(srcs, name, half_cols=False):
    n = len(srcs)
    halves = [s.shape[1] // 2 if half_cols else s.shape[1] for s in srcs]

    def body(*refs):
        src_refs, out_refs, send_sems, recv_sems = refs[:n], refs[n:2 * n], refs[2 * n], refs[2 * n + 1]
        x, y, c = _place()
        copies = []
        for k in range(n):
            part = src_refs[k].at[:, pl.ds(pl.multiple_of((1 - c) * halves[k], LANE), halves[k])] if half_cols else src_refs[k]
            copies.append(pltpu.make_async_remote_copy(src_ref=part, dst_ref=out_refs[k], send_sem=send_sems.at[k], recv_sem=recv_sems.at[k],
                                                       device_id=(x, y, 1 - c), device_id_type=MESH))
        for cp in copies:
            cp.start()
        for cp in copies:
            cp.wait()

    return pl.pallas_call(
        body, name=name, out_shape=[_sds((s.shape[0], h), s.dtype) for s, h in zip(srcs, halves)], in_specs=[ANY] * n, out_specs=[ANY] * n,
        scratch_shapes=[pltpu.SemaphoreType.DMA((n,)), pltpu.SemaphoreType.DMA((n,))],
    )(*srcs)


def _return_and_gather(mines, rep_block):
    n = len(mines)

    def body(*refs):
        src_refs, rep_ref, out_refs, rep_out = refs[:n], refs[n], refs[n + 1:2 * n + 1], refs[2 * n + 1]
        send_sems, recv_sems, g_send, g_recv, g_local = refs[2 * n + 2:]
        x, y, c = _place()
        copies = [pltpu.make_async_remote_copy(src_ref=src_refs[k], dst_ref=out_refs[k], send_sem=send_sems.at[k], recv_sem=recv_sems.at[k],
                                               device_id=(x, y, 1 - c), device_id_type=MESH) for k in range(n)]
        gather = _Gather8(rep_ref, rep_out, g_send, g_recv, g_local)
        for cp in copies:
            cp.start()
        gather.start()
        gather.forward()
        gather.finish()
        for cp in copies:
            cp.wait()

    return pl.pallas_call(
        body, name="rs_return", out_shape=[_sds(m.shape, m.dtype) for m in mines] + [_sds((8,) + rep_block.shape, rep_block.dtype)],
        in_specs=[ANY] * (n + 1), out_specs=[ANY] * (n + 1),
        scratch_shapes=[pltpu.SemaphoreType.DMA((n,)), pltpu.SemaphoreType.DMA((n,))] + GATHER_SEMS,
    )(*mines, rep_block)


def _scatter_shapes(lay, group, half):
    return [_sds((3, lay.c_rows[group], half), BF16), _sds((lay.c_rows[group], half), F32)]


def _scatter_chips(part16, part32, lay, group, name):
    def body(p16_ref, p32_ref, got_ref, own_ref, send_sems, recv_sems, local_sem):
        sc = _Scatter(p16_ref, p32_ref, got_ref, own_ref, send_sems, recv_sems, local_sem, lay, G_GROUPS[group])
        sc.start()
        sc.finish()

    return pl.pallas_call(
        body, name=name, out_shape=_scatter_shapes(lay, group, part16.shape[1]), in_specs=[ANY, ANY], out_specs=[ANY, ANY],
        scratch_shapes=SCATTER_SEMS,
    )(part16, part32)


def _sum_sibling(gbuf, got, cidx, name):
    rows, d = gbuf.shape
    half = d // 2
    rb = _row_block(rows)

    def body(c_ref, g_ref, r_ref, o32_ref, o16_ref):
        del c_ref
        s = g_ref[...] + r_ref[...]
        o32_ref[...] = s
        o16_ref[...] = s.astype(BF16)

    plain = pl.BlockSpec((rb, half), lambda i, c: (i, 0))
    return pl.pallas_call(
        body, name=name,
        grid_spec=pltpu.PrefetchScalarGridSpec(num_scalar_prefetch=1, grid=(rows // rb,),
                                               in_specs=[pl.BlockSpec((rb, half), lambda i, c: (i, c[0])), plain], out_specs=[plain, plain]),
        out_shape=[_sds((rows, half)), _sds((rows, half), BF16)], compiler_params=_params(1),
    )(cidx, gbuf, got)


def _sum_devices(own, sib, got, name):
    rows, half = own.shape
    rb = _row_block(rows)
    n = got.shape[0]

    def body(a_ref, s_ref, b_ref, o_ref):
        acc = a_ref[...] + s_ref[...]
        for k in range(n):
            acc = acc + b_ref[k].astype(F32)
        o_ref[...] = acc

    spec = pl.BlockSpec((rb, half), lambda i: (i, 0))
    return pl.pallas_call(
        body, name=name, grid=(rows // rb,), in_specs=[spec, spec, pl.BlockSpec((n, rb, half), lambda i: (0, i, 0))], out_specs=spec,
        out_shape=_sds((rows, half)), compiler_params=_params(1),
    )(own, sib, got)


def _sum_chips(own, got, name):
    rows, half = own.shape
    rb = _row_block(rows)

    def body(a_ref, b_ref, o_ref):
        o_ref[...] = ((a_ref[...] + b_ref[0].astype(F32)) + b_ref[1].astype(F32)) + b_ref[2].astype(F32)

    spec = pl.BlockSpec((rb, half), lambda i: (i, 0))
    return pl.pallas_call(
        body, name=name, grid=(rows // rb,), in_specs=[spec, pl.BlockSpec((3, rb, half), lambda i: (0, i, 0))], out_specs=spec,
        out_shape=_sds((rows, half)), compiler_params=_params(1),
    )(own, got)


def _adamw(w, g, m, v):
    m = ADAM_B1 * m + (1.0 - ADAM_B1) * g
    v = ADAM_B2 * v + (1.0 - ADAM_B2) * (g * g)
    m_hat = m / (1.0 - ADAM_B1 ** ADAM_STEP)
    v_hat = v / (1.0 - ADAM_B2 ** ADAM_STEP)
    return -ADAM_LR * (m_hat / (jnp.sqrt(v_hat) + ADAM_EPS) + ADAM_WD * w), m, v


def _adamw_rows(name, w, g, m, v):
    _, rows, cols = w.shape
    rb = _row_block(rows, 256)

    def body(w_ref, g_ref, m_ref, v_ref, d_ref, mo_ref, vo_ref):
        d_ref[...], mo_ref[...], vo_ref[...] = _adamw(w_ref[...], g_ref[...], m_ref[...], v_ref[...])

    spec = pl.BlockSpec((1, rb, cols), lambda i: (0, i, 0))
    return pl.pallas_call(
        body, name=name, grid=(rows // rb,), in_specs=[spec] * 4, out_specs=[spec] * 3, out_shape=[_sds(w.shape)] * 3,
        compiler_params=_params(1),
    )(w, g, m, v)


def _adamw_group(ws, gs, ms, vs):
    n = len(ws)

    def body(*refs):
        for k in range(n):
            w_ref, g_ref, m_ref, v_ref = (refs[j * n + k] for j in range(4))
            outs = _adamw(w_ref[...], g_ref[...], m_ref[...], v_ref[...])
            for j in range(3):
                refs[(4 + j) * n + k][...] = outs[j]

    outs = pl.pallas_call(
        body, name="adamw_small", out_shape=[_sds(w.shape) for w in ws] * 3,
        compiler_params=pltpu.CompilerParams(vmem_limit_bytes=VMEM_LIMIT),
    )(*ws, *gs, *ms, *vs)
    return outs[:n], outs[n:2 * n], outs[2 * n:]


def _gather_weights(sh, lay):
    x, y, c = _place()
    d = lay.d
    uq = sh["w_uq"][0].astype(BF16)
    parts = {
        "in_b": sh["w_in_b"][0].T.astype(BF16), "in_a": sh["w_in_a"][0].T.astype(BF16), "out_a": sh["w_out_a"][0].astype(BF16),
        "out_b": sh["w_out_b"][0].astype(BF16), "uk": sh["w_uk"].astype(BF16).reshape(-1, d), "uv": sh["w_uv"].astype(BF16).reshape(-1, d),
        "uq_n": uq[:, :, :QK_NOPE].reshape(-1, d), "uq_r": jnp.pad(uq[:, :, QK_NOPE:], ((0, 0), (0, 0), (0, LANE - QK_ROPE))).reshape(-1, d),
        "dkv": jnp.pad(sh["w_dkv"].astype(BF16), ((0, 0), (0, LANE - QK_ROPE))).reshape(-1, d),
    }
    halves = {}
    for group, order in W_GROUPS.items():
        stack = jnp.concatenate([parts[k] for k in order], axis=0).reshape(2, lay.w_rows[group] // 2, d)
        halves[group] = lax.dynamic_index_in_dim(stack, c, 0, keepdims=False)
    small = jnp.concatenate([sh[k].reshape(-1) for k in SMALL])
    n_small = small.shape[0]
    width = _round_up(n_small, 2 * SUBLANE * LANE) // (2 * SUBLANE)
    small = jnp.pad(small, (0, 2 * SUBLANE * width - n_small)).reshape(2, SUBLANE, width)
    wg, sg = _all_gather8([halves["a"], lax.dynamic_index_in_dim(small, c, 0, keepdims=False)], "ag_weights")
    wg = wg.reshape(N_CHIPS, lay.w_rows["a"], d)
    sg = sg.reshape(N_CHIPS, 2 * SUBLANE * width)
    full, off = {}, 0
    for k in SMALL:
        n = sh[k].size
        piece = sg[:, off:off + n]
        off += n
        if k == "conv_w":
            full[k] = piece.reshape(N_CHIPS, 4, n // 4).transpose(1, 0, 2).reshape(4, n)
        else:
            full[k] = piece.reshape(1, N_CHIPS * n)
    return wg, halves["b"], full


def _chip_split(g, taps=False):
    if taps:
        n = g.shape[1] // N_CHIPS
        return g.reshape(4, N_CHIPS, n).transpose(1, 0, 2).reshape(N_CHIPS, 4 * n)
    return g.reshape(N_CHIPS, -1)


def kernel(x, norm_a, w_in_a, conv_w, conv_b, w_rg, b_rg, w_ig, b_ig, lru_lambda, w_out_a, norm_kv, w_dkv, kv_norm, w_uk, w_uv, norm_b, w_in_b, q_norm, w_uq, w_out_b, final_norm, loss_target, m_norm_a, m_w_in_a, m_conv_w, m_conv_b, m_w_rg, m_b_rg, m_w_ig, m_b_ig, m_lru_lambda, m_w_out_a, m_norm_kv, m_w_dkv, m_kv_norm, m_w_uk, m_w_uv, m_norm_b, m_w_in_b, m_q_norm, m_w_uq, m_w_out_b, m_final_norm, v_norm_a, v_w_in_a, v_conv_w, v_conv_b, v_w_rg, v_b_rg, v_w_ig, v_b_ig, v_lru_lambda, v_w_out_a, v_norm_kv, v_w_dkv, v_kv_norm, v_w_uk, v_w_uv, v_norm_b, v_w_in_b, v_q_norm, v_w_uq, v_w_out_b, v_final_norm):
    given = dict(locals())
    sh = {k: given[k] for k in WEIGHTS}
    xi, yi, ci = _place()
    nb, seq, d = x.shape
    t_all = nb * seq
    tb_a, tb_b, ta, bt = min(TOKENS_A, seq), min(TOKENS_B, seq), min(TOKENS_ATTN, seq), min(TOKENS_MM, t_all)
    dr = conv_b.shape[1] * N_CHIPS
    qr, kvr, nheads = q_norm.shape[1], kv_norm.shape[0], w_uk.shape[1]
    hv = nheads * LANE
    n_small = sum(sh[k].size for k in SMALL)
    n_repl = sum(sh[k].size for k in REPL)
    lay = _Layout(d, dr, qr, kvr, hv, n_small, n_repl)
    half = d // 2

    wga, wb_half, w = _gather_weights(sh, lay)
    w.update({"w_rg": w_rg[0].astype(BF16), "w_ig": w_ig[0].astype(BF16), "norm_kv": norm_kv[None, :],
              "kv_norm": kv_norm[None, :], "final_norm": final_norm[None, :], "norm_b": norm_b, "q_norm": q_norm})
    cos_t, sin_t = _rope_tables(seq)
    cidx = jnp.reshape(ci, (1,)).astype(jnp.int32)

    x0 = x.reshape(t_all, d)
    x1, u, hs, h, y, xb, wgb = _fa_fwd(x0, wga, wb_half, w, lay, seq, tb_a)
    wgb = wgb.reshape(N_CHIPS, lay.w_rows["b"], d)
    w["w_dkv_p"] = wgb[:, lay.w_off["dkv"]:lay.w_off["dkv"] + lay.rows["dkv"], :].reshape(d, kvr + LANE)
    qn, qrp, kn, kr, v, ub, ckr, hb, hk, cq, ckv = _fb_fwd(x1, wgb, w, lay, cos_t, sin_t, seq, tb_b)
    o, lse = _attn_fwd(qn, qrp, kn, kr, v, seq, ta)
    loss, g_final_norm, yb, dx2, do, dgate, delta = _head(o, ub, x1, loss_target.reshape(t_all, d), wgb, w, lay, tb_b)
    dqn, dqr, dkn, dkr, dv = _attn_bwd(qn, qrp, kn, kr, v, do, lse, delta, seq, ta)
    dx1, dqr_pre, dqn_pre, dub, dckr, g_q_norm, g_norm_b, g_kv_norm, g_norm_kv = _fb_bwd(
        dqn, dqr, dkn, dkr, dv, dgate, ub, ckr, x1, dx2, wgb, w, lay, cos_t, sin_t, seq, tb_b)
    loss = lax.psum(loss[0, 0], ("x", "y", "c"))

    gbufs = [lax.empty((lay.g_rows["early"], d), F32), lax.empty((lay.g_rows["early"], d), BF16)]
    for key, a, b in (("in_b", dub, hb), ("out_a", y, dx1), ("out_b", yb, dx2), ("uk", ckv, dkn), ("uv", ckv, dv), ("uq_n", cq, dqn_pre),
                      ("uq_r", cq, dqr_pre)):
        gbufs = _mm_into(gbufs, a, b, lay.g_off[key], "dw_" + key, bt)
    g_dkv = _mm_tn(hk, dckr, "dw_dkv", bt)
    gx, du, g_norm_a, g_conv_w, g_conv_b, g_b_rg, g_b_ig, g_lam, g_w_rg, g_w_ig, others, sib, own = _fa_bwd(
        dx1, x0, u, xb, hs, wga, gbufs[1], gbufs[0], w, lay, seq, tb_a)
    mine_early = _sum_devices(own, sib, others, "rs_sum_early")

    small = jnp.concatenate([_chip_split(g_norm_a), _chip_split(g_conv_w, taps=True), _chip_split(g_conv_b), _chip_split(g_b_rg),
                             _chip_split(g_b_ig), _chip_split(g_lam)], axis=1)
    small = jnp.pad(small, ((0, 0), (0, lay.small_rows * d - small.shape[1]))).reshape(N_CHIPS, lay.small_rows, d)
    repl_parts = {"w_rg": g_w_rg, "w_ig": g_w_ig, "norm_kv": g_norm_kv, "kv_norm": g_kv_norm, "norm_b": g_norm_b, "q_norm": g_q_norm,
                  "final_norm": g_final_norm}
    repl = jnp.concatenate([repl_parts[k].reshape(-1) for k in REPL])
    repl = jnp.pad(repl, (0, N_CHIPS * lay.repl_rows * d - n_repl)).reshape(N_CHIPS, lay.repl_rows, d)
    pad_rows = lay.rows["rest"] - lay.rows["dkv"] - lay.small_rows - lay.repl_rows
    rest = jnp.concatenate([g_dkv.reshape(N_CHIPS, lay.rows["dkv"], d), small, repl, jnp.zeros((N_CHIPS, pad_rows, d), F32)], axis=1)
    others, own = _dw_in_a_exchange(du, h, rest.reshape(N_CHIPS * lay.rows["rest"], d), lay, bt)
    mine_late = _sum_chips(own, others, "rs_sum_chips_late")

    r0 = lay.c_off["rest"] + lay.rows["dkv"] + lay.small_rows
    theirs_early, theirs_late, rep_all = _return_and_gather([mine_early, mine_late], mine_late[r0:r0 + lay.repl_rows])
    red = {}
    for group, mine, theirs in (("early", mine_early, theirs_early), ("late", mine_late, theirs_late)):
        red[group] = jnp.concatenate([jnp.where(ci == 0, mine, theirs), jnp.where(ci == 0, theirs, mine)], axis=1)
    rep_flat =rep_all.reshape(N_CHIPS, 2, lay.repl_rows, half).transpose(0, 2, 1, 3).reshape(-1)

    def rows(key):
        group = "late" if key in G_GROUPS["late"] else "early"
        return red[group][lay.c_off[key]:lay.c_off[key] + lay.rows[key]]

    grads = {"w_in_b": rows("in_b").T[None], "w_in_a": rows("in_a").T[None], "w_out_a": rows("out_a")[None], "w_out_b": rows("out_b")[None],
             "w_uk": rows("uk").reshape(w_uk.shape), "w_uv": rows("uv").reshape(w_uv.shape)}
    uq_n = rows("uq_n").reshape(qr // N_CHIPS, nheads, LANE)
    uq_r = rows("uq_r").reshape(qr // N_CHIPS, nheads, LANE)[:, :, :QK_ROPE]
    grads["w_uq"] = jnp.concatenate([uq_n, uq_r], axis=2)[None]
    rest_red = rows("rest")
    grads["w_dkv"] = rest_red[:lay.rows["dkv"]].reshape(d // N_CHIPS, kvr + LANE)[:, :kvr + QK_ROPE]
    small_red = rest_red[lay.rows["dkv"]:lay.rows["dkv"] + lay.small_rows].reshape(-1)
    off = 0
    for k in SMALL:
        n = sh[k].size
        grads[k] = small_red[off:off + n].reshape(sh[k].shape)
        off += n
    off = 0
    for k in REPL:
        n = sh[k].size
        grads[k] = rep_flat[off:off + n].reshape(sh[k].shape)
        off += n

    new = {}
    for k in ("w_in_a", "w_in_b", "w_out_a", "w_out_b"):
        view = (lambda a: jnp.swapaxes(a, 1, 2)) if k in TRANSPOSED else (lambda a: a)
        outs = _adamw_rows("adamw_" + k, view(sh[k]), view(grads[k]), view(given["m_" + k]), view(given["v_" + k]))
        new[k] = tuple(view(a) for a in outs)
    rest_names = [k for k in WEIGHTS if k not in new]

    def as2d(k, a):
        return a.T if k in TRANSPOSED else a[None, :] if a.ndim == 1 else a

    ds, ms, vs = _adamw_group([as2d(k, sh[k]) for k in rest_names], [as2d(k, grads[k]) for k in rest_names],
                              [as2d(k, given["m_" + k]) for k in rest_names], [as2d(k, given["v_" + k]) for k in rest_names])
    for n, k in enumerate(rest_names):
        new[k] = tuple((a.T if k in TRANSPOSED else a).reshape(sh[k].shape) for a in (ds[n], ms[n], vs[n]))
    return (loss, gx.reshape(nb, seq, d), *[grads[k] for k in WEIGHTS], *[new[k][0] for k in WEIGHTS], *[new[k][1] for k in WEIGHTS],
            *[new[k][2] for k in WEIGHTS])
```

```python
import jax
import jax.numpy as jnp
from jax import lax
from jax.experimental import pallas as pl
from jax.experimental.pallas import tpu as pltpu

F32, BF16 = jnp.float32, jnp.bfloat16
EPS = 1e-6
LRU_C = 8.0
ROPE_THETA = 10000.0
QK_NOPE, QK_ROPE = 128, 64
ATTN_SCALE = (QK_NOPE + QK_ROPE) ** -0.5
LN2 = 0.6931471805599453
Q_SCALE = ATTN_SCALE / LN2
ATTN_HEADS, ATTN_HEADS_BWD = 4, 2
ATTN_ROWS = 64
LANE = 128
SUBLANE = 8
ROW_ALIGN = 32
VMEM_LIMIT = 60000 * 1024
ADAM_LR, ADAM_B1, ADAM_B2, ADAM_EPS, ADAM_WD, ADAM_STEP = 0.001, 0.9, 0.999, 1e-08, 0.01, 10
MESH = pl.DeviceIdType.MESH
ANY = pl.BlockSpec(memory_space=pl.ANY)
N_CHIPS = 4
TOKENS_A, TOKENS_B, TOKENS_ATTN, TOKENS_MM = 256, 512, 512, 2048

SMALL = ("norm_a", "conv_w", "conv_b", "b_rg", "b_ig", "lru_lambda")
REPL = ("w_rg", "w_ig", "norm_kv", "kv_norm", "norm_b", "q_norm", "final_norm")
TRANSPOSED = ("w_in_b", "w_dkv")
WEIGHTS = ("norm_a", "w_in_a", "conv_w", "conv_b", "w_rg", "b_rg", "w_ig", "b_ig", "lru_lambda", "w_out_a", "norm_kv",
           "w_dkv", "kv_norm", "w_uk", "w_uv", "norm_b", "w_in_b", "q_norm", "w_uq", "w_out_b", "final_norm")
W_GROUPS = {"a": ("in_a", "out_a"), "b": ("in_b", "out_b", "uk", "uv", "uq_n", "uq_r", "dkv")}
G_GROUPS = {"early": ("in_b", "out_a", "out_b", "uk", "uv", "uq_n", "uq_r"), "late": ("in_a", "rest")}


def _sds(shape, dtype=F32):
    return jax.ShapeDtypeStruct(tuple(shape), dtype)


def _params(n_grid):
    return pltpu.CompilerParams(dimension_semantics=("arbitrary",) * n_grid, vmem_limit_bytes=VMEM_LIMIT)


def _full(shape):
    nd = len(shape)
    return pl.BlockSpec(tuple(shape), lambda *g: (0,) * nd)


def _round_up(n, k):
    return -(-n // k) * k


def _row_block(rows, cap=512):
    best = SUBLANE
    for r in range(SUBLANE, min(rows, cap) + 1, SUBLANE):
        if rows % r == 0:
            best = r
    return best


def _place():
    return lax.axis_index("x"), lax.axis_index("y"), lax.axis_index("c")


class _Layout:
    def __init__(self, d, dr, qr, kvr, hv, n_small, n_repl):
        assert hv == d, "the packed rows are D_MODEL wide, which must equal heads * 128"
        self.d, self.dr, self.qr, self.kvr, self.hv = d, dr, qr, kvr, hv
        per_chip = {"in_b": (qr + hv) // N_CHIPS, "in_a": 2 * dr // N_CHIPS, "out_a": dr // N_CHIPS, "out_b": hv // N_CHIPS,
                    "uk": kvr // N_CHIPS, "uv": kvr // N_CHIPS, "uq_n": qr // N_CHIPS, "uq_r": qr // N_CHIPS,
                    "dkv": (d // N_CHIPS) * (kvr + LANE) // d}
        assert all(r % ROW_ALIGN == 0 for r in per_chip.values()), per_chip
        self.small_rows = _round_up(-(-n_small // d), SUBLANE)
        self.repl_rows = _round_up(-(-n_repl // (N_CHIPS * d)), SUBLANE)
        per_chip["rest"] = _round_up(per_chip["dkv"] + self.small_rows + self.repl_rows, ROW_ALIGN)
        self.rows = per_chip
        self.w_off, self.w_rows = {}, {}
        for group, order in W_GROUPS.items():
            off = 0
            for k in order:
                self.w_off[k] = off
                off += per_chip[k]
            assert off % ROW_ALIGN == 0, (group, off)
            self.w_rows[group] = off
        self.g_off, self.c_off, self.c_rows, self.g_rows = {}, {}, {}, {}
        for group, order in G_GROUPS.items():
            off = 0
            for k in order:
                self.c_off[k] = off
                self.g_off[k] = N_CHIPS * off
                off += per_chip[k]
            self.c_rows[group] = off
            self.g_rows[group] = N_CHIPS * off


def _dot(a, b):
    return jnp.dot(a, b, preferred_element_type=F32)


def _dot_nt(a, b):
    return lax.dot_general(a, b, (((1,), (1,)), ((), ())), preferred_element_type=F32)


def _dot_tn(a, b):
    return lax.dot_general(a, b, (((0,), (0,)), ((), ())), preferred_element_type=F32)


def _rinv(x):
    return lax.rsqrt(jnp.mean(x * x, axis=-1, keepdims=True) + EPS)


def _rms_bwd(x, rinv, g, dy):
    z = dy * g
    dx = rinv * z - x * (rinv * rinv * rinv) * jnp.mean(z * x, axis=-1, keepdims=True)
    dg = jnp.sum(dy * (x * rinv), axis=0, keepdims=True)
    return dx, dg


def _softplus(z):
    return jnp.maximum(z, 0.0) + jnp.log1p(jnp.exp(-jnp.abs(z)))


def _sigmoid(x):
    return 0.5 * jnp.tanh(0.5 * x) + 0.5


def _decay(log_a):
    a = jnp.exp(log_a)
    a2 = a * a
    return a, a2, -jnp.tanh(log_a) * (a2 + 1.0)


def _swap_halves(x):
    w = x.shape[1]
    lane = lax.broadcasted_iota(jnp.int32, x.shape, 1)
    return jnp.where(lane % QK_ROPE < QK_ROPE // 2, pltpu.roll(x, w - QK_ROPE // 2, 1), pltpu.roll(x, QK_ROPE // 2, 1))


def _rope_tables(seq):
    pos = jnp.arange(seq, dtype=F32)
    inv = ROPE_THETA ** (-jnp.arange(0, QK_ROPE, 2, dtype=F32) / QK_ROPE)
    ang = pos[:, None] * inv[None, :]
    cos, sin = jnp.cos(ang), jnp.sin(ang)
    zero = jnp.zeros((seq, LANE - QK_ROPE), F32)
    return jnp.concatenate([cos, cos, zero], 1), jnp.concatenate([-sin, sin, zero], 1)


def _fetch(wg_ref, lay, key, dst, sems, k0):
    rows = lay.rows[key]
    return [pltpu.make_async_copy(wg_ref.at[p, pl.ds(lay.w_off[key], rows), :], dst.at[pl.ds(p * rows, rows), :], sems.at[k0 + p])
            for p in range(N_CHIPS)]


def _gates(xb, wrg_ref, brg, wig_ref, big, nblocks):
    xbb = xb.astype(BF16)
    rg = [_dot(xbb[:, n * LANE:(n + 1) * LANE], wrg_ref[n]) for n in range(nblocks)]
    ig = [_dot(xbb[:, n * LANE:(n + 1) * LANE], wig_ref[n]) for n in range(nblocks)]
    r = _sigmoid(jnp.concatenate(rg, axis=1) + brg)
    i = _sigmoid(jnp.concatenate(ig, axis=1) + big)
    return r, i


def _conv(xpad, cw_ref, cb, tb):
    return (cb + cw_ref[3:4, :] * xpad[pl.ds(8, tb), :] + cw_ref[2:3, :] * xpad[pl.ds(7, tb), :]
            + cw_ref[1:2, :] * xpad[pl.ds(6, tb), :] + cw_ref[0:1, :] * xpad[pl.ds(5, tb), :])


def _fa_fwd(x, wg, wb_half, w, lay, seq, tb):
    t_all, d = x.shape
    dr = lay.dr
    nblocks = w["w_rg"].shape[0]
    nblk = seq // tb
    nt = tb // SUBLANE
    nsteps = (t_all // seq) * nblk

    def body(x_ref, wg_ref, wbh_ref, na, cw, cb, wrg, brg, wig, big, lam, x1_ref, u_ref, hs_ref, h_ref, y_ref, xb_ref, wb_ref,
             wint, wout, xpad, a_s, b_s, carry, sems, send_sems, recv_sems, local_sem):
        step_no = pl.program_id(0) * nblk + pl.program_id(1)
        gather = _Gather8(wbh_ref, wb_ref, send_sems, recv_sems, local_sem)

        @pl.when(step_no == 0)
        def _():
            gather.start()
            cps = _fetch(wg_ref, lay, "in_a", wint, sems, 0) + _fetch(wg_ref, lay, "out_a", wout, sems, N_CHIPS)
            for cp in cps:
                cp.start()
            for cp in cps:
                cp.wait()

        @pl.when(step_no == nsteps // 2)
        def _():
            gather.forward()

        @pl.when(pl.program_id(1) == 0)
        def _():
            xpad[pl.ds(0, 8), :] = jnp.zeros((8, dr), F32)
            carry[...] = jnp.zeros((8, dr), F32)

        xv = x_ref[...]
        h = (xv * _rinv(xv) * na[...]).astype(BF16)
        h_ref[...] = h
        u = _dot_nt(h, wint[...])
        u_ref[...] = u
        xpre, gate = u[:, :dr], u[:, dr:]
        xpad[pl.ds(8, tb), :] = xpre
        xb = _conv(xpad, cw, cb[...], tb)
        xb_ref[...] = xb
        xpad[pl.ds(0, 8), :] = xpre[tb - 8:, :]
        r, i = _gates(xb, wrg, brg[...], wig, big[...], nblocks)
        log_a = -LRU_C * r * _softplus(-lam[...])
        a, _, nem = _decay(log_a)
        a_s[...] = a
        b_s[...] = jnp.sqrt(nem) * (i * xb)
        row = lax.broadcasted_iota(jnp.int32, (8, dr), 0)

        def step(t, c):
            r0 = pl.multiple_of(t * 8, 8)
            a = a_s[pl.ds(r0, 8), :]
            b = b_s[pl.ds(r0, 8), :]
            for s in (1, 2, 4):
                m = row >= s
                a_sh = jnp.where(m, pltpu.roll(a, s, 0), 1.0)
                b_sh = jnp.where(m, pltpu.roll(b, s, 0), 0.0)
                b = a * b_sh + b
                a = a * a_sh
            hh = b + a * c
            hs_ref[pl.ds(r0, 8), :] = hh
            return jnp.broadcast_to(hh[7:8, :], hh.shape)

        carry[...] = lax.fori_loop(0, nt, step, carry[...])
        y = (hs_ref[...] * (gate * _sigmoid(gate))).astype(BF16)
        y_ref[...] = y
        x1_ref[...] = xv + _dot(y, wout[...])

        @pl.when(step_no == nsteps - 1)
        def _():
            gather.finish()

    tok = lambda c: pl.BlockSpec((tb, c), lambda b, j: (b * nblk + j, 0))
    consts = [w["norm_a"], w["conv_w"], w["conv_b"], w["w_rg"], w["b_rg"], w["w_ig"], w["b_ig"], w["lru_lambda"]]
    return pl.pallas_call(
        body, name="fa_fwd", grid=(t_all // seq, nblk),
        in_specs=[tok(d), ANY, ANY] + [_full(c.shape) for c in consts],
        out_specs=[tok(d), tok(2 * dr), tok(dr), tok(d), tok(dr), tok(dr), ANY],
        out_shape=[_sds((t_all, d)), _sds((t_all, 2 * dr)), _sds((t_all, dr)), _sds((t_all, d), BF16), _sds((t_all, dr), BF16),
                   _sds((t_all, dr)), _sds((8,) + wb_half.shape, BF16)],
        scratch_shapes=[pltpu.VMEM((2 * dr, d), BF16), pltpu.VMEM((dr, d), BF16), pltpu.VMEM((tb + 8, dr), F32), pltpu.VMEM((tb, dr), F32),
                        pltpu.VMEM((tb, dr), F32), pltpu.VMEM((8, dr), F32), pltpu.SemaphoreType.DMA((2 * N_CHIPS,))] + GATHER_SEMS,
        compiler_params=_params(2),
    )(x, wg, wb_half, *consts)


def _fb_fwd(x1, wg, w, lay, cos_t, sin_t, seq, tb):
    t_all, d = x1.shape
    kvr, qr, hv = lay.kvr, lay.qr, lay.hv
    nheads = hv // LANE
    npos = seq // tb

    def body(x_ref, wg_ref, nkv, nb, wdkv, kvn, qn, cos_ref, sin_ref,
             qn_o, qr_o, kn_o, kr_o, v_o, ub_o, ckr_o, hb_o, hk_o, cq_o, ckv_o, winb, wuk, wuv, wuqn, wuqr, sems):
        @pl.when(pl.program_id(0) == 0)
        def _():
            cps = []
            for n, (key, dst) in enumerate((("in_b", winb), ("uk", wuk), ("uv", wuv), ("uq_n", wuqn), ("uq_r", wuqr))):
                cps += _fetch(wg_ref, lay, key, dst, sems, n * N_CHIPS)
            for cp in cps:
                cp.start()
            for cp in cps:
                cp.wait()

        xv = x_ref[...]
        xh = xv * _rinv(xv)
        hk = (xh * nkv[...]).astype(BF16)
        hb = (xh * nb[...]).astype(BF16)
        hk_o[...] = hk
        hb_o[...] = hb
        cos, sin = cos_ref[...], sin_ref[...]
        ckr = _dot(hk, wdkv[...])
        ckr_o[...] = ckr
        ckv_pre = ckr[:, :kvr]
        ckv = (ckv_pre * _rinv(ckv_pre) * kvn[...]).astype(BF16)
        ckv_o[...] = ckv
        kr = ckr[:, kvr:]
        kr_o[...] = (kr * cos + _swap_halves(kr) * sin).astype(BF16)
        kn_o[...] = _dot(ckv, wuk[...]).astype(BF16)
        v_o[...] = _dot(ckv, wuv[...]).astype(BF16)
        ub = _dot_nt(hb, winb[...])
        ub_o[...] = ub
        cq_pre = ub[:, :qr]
        cq = (cq_pre * _rinv(cq_pre) * qn[...]).astype(BF16)
        cq_o[...] = cq
        qn_o[...] = (_dot(cq, wuqn[...]) * Q_SCALE).astype(BF16)
        qrope = _dot(cq, wuqr[...]) * Q_SCALE
        qr_o[...] = (qrope * jnp.tile(cos, (1, nheads)) + _swap_halves(qrope) * jnp.tile(sin, (1, nheads))).astype(BF16)

    tok = lambda c: pl.BlockSpec((tb, c), lambda i: (i, 0))
    pos = pl.BlockSpec((tb, LANE), lambda i: (i % npos, 0))
    consts = [w["norm_kv"], w["norm_b"], w["w_dkv_p"], w["kv_norm"], w["q_norm"]]
    outs = [(hv, BF16), (hv, BF16), (hv, BF16), (LANE, BF16), (hv, BF16), (qr + hv, F32), (kvr + LANE, F32), (d, BF16), (d, BF16), (qr, BF16), (kvr, BF16)]
    return pl.pallas_call(
        body, name="fb_fwd", grid=(t_all // tb,),
        in_specs=[tok(d), ANY] + [_full(c.shape) for c in consts] + [pos, pos],
        out_specs=[tok(c) for c, _ in outs],
        out_shape=[_sds((t_all, c), dt) for c, dt in outs],
        scratch_shapes=[pltpu.VMEM((qr + hv, d), BF16), pltpu.VMEM((kvr, d), BF16), pltpu.VMEM((kvr, d), BF16), pltpu.VMEM((qr, d), BF16),
                        pltpu.VMEM((qr, d), BF16), pltpu.SemaphoreType.DMA((5 * N_CHIPS,))],
        compiler_params=_params(1),
    )(x1, wg, *consts, cos_t, sin_t)


def _causal_mask(row0, col0, nrows, ncols):
    rows = row0 + lax.broadcasted_iota(jnp.int32, (nrows, ncols), 0)
    cols = col0 + lax.broadcasted_iota(jnp.int32, (nrows, ncols), 1)
    return cols <= rows


def _attn_fwd(qn, qr, kn, kr, v, seq, ta):
    t_all, hv = qn.shape
    nheads, nb, na = hv // LANE, t_all // seq, seq // ta

    reps = ta // LANE
    hp = ATTN_HEADS
    wide = hp * LANE

    def body(qn_ref, qr_ref, kn_ref, kr_ref, v_ref, o_ref, lse_ref, m_s, l_s, acc_s):
        i = pl.program_id(2)
        m_s[...] = jnp.full((ta, wide), -1e30, F32)
        l_s[...] = jnp.zeros((ta, wide), F32)
        acc_s[...] = jnp.zeros((ta, wide), F32)
        heads = [slice(n * LANE, (n + 1) * LANE) for n in range(hp)]
        qs = [jnp.concatenate([qn_ref[:, hd], qr_ref[:, hd]], axis=1) for hd in heads]

        def tile(j, diagonal):
            cols = pl.ds(pl.multiple_of(j * ta, ta), ta)
            k_rope = kr_ref[cols, :]
            for q, hd in zip(qs, heads):
                k = jnp.concatenate([kn_ref[cols, hd], k_rope], axis=1)
                s = _dot_nt(q, k)
                if diagonal:
                    s = jnp.where(_causal_mask(0, 0, ta, ta), s, -1e30)
                m_prev = m_s[:, hd]
                m_new = jnp.maximum(m_prev, jnp.max(s, axis=1, keepdims=True))
                p = jnp.exp2(s - jnp.tile(m_new, (1, reps)))
                alpha = jnp.exp2(m_prev - m_new)
                l_s[:, hd] = alpha * l_s[:, hd] + jnp.sum(p, axis=1, keepdims=True)
                acc_s[:, hd] = alpha * acc_s[:, hd] + _dot(p.astype(BF16), v_ref[cols, hd])
                m_s[:, hd] = m_new

        def off_diagonal(j, carry):
            tile(j, False)
            return carry

        lax.fori_loop(0, i, off_diagonal, 0)
        tile(i, True)
        o_ref[...] = (acc_s[...] / l_s[...]).astype(BF16)
        lse_ref[...] = m_s[...] + jnp.log2(l_s[...])

    qspec = pl.BlockSpec((ta, wide), lambda b, h, i: (b * na + i, h))
    kspec = pl.BlockSpec((seq, wide), lambda b, h, i: (b, h))
    krspec = pl.BlockSpec((seq, LANE), lambda b, h, i: (b, 0))
    return pl.pallas_call(
        body, name="attn_fwd", grid=(nb, nheads // hp, na),
        in_specs=[qspec, qspec, kspec, krspec, kspec],
        out_specs=[qspec, qspec],
        out_shape=[_sds((t_all, hv), BF16), _sds((t_all, hv))],
        scratch_shapes=[pltpu.VMEM((ta, wide), F32)] * 3,
        compiler_params=_params(3),
    )(qn, qr, kn, kr, v)


def _attn_bwd(qn, qr, kn, kr, v, do, lse, delta, seq, ta):
    t_all, hv = qn.shape
    nheads, nb, na = hv // LANE, t_all // seq, seq // ta

    reps = ta // LANE
    nchunks = ta // ATTN_ROWS

    hp = ATTN_HEADS_BWD
    wide = hp * LANE
    heads = [slice(n * LANE, (n + 1) * LANE) for n in range(hp)]

    def body(qn_ref, qr_ref, kn_ref, kr_ref, v_ref, do_ref, lse_ref, dl_ref, dqn_out, dqr_out, dkn_ref, dkr_ref, dv_ref,
             s_s, dp_s, p_s, ds_s, dk_s, dv_s, dqn_ref, dqr_ref):
        j = pl.program_id(2)

        @pl.when(j == 0)
        def _():
            dqn_ref[...] = jnp.zeros((seq, wide), F32)
            dqr_ref[...] = jnp.zeros((seq, wide), F32)

        dk_s[...] = jnp.zeros((hp, ta, 2 * LANE), F32)
        dv_s[...] = jnp.zeros((hp, ta, LANE), F32)
        k_rope = kr_ref[...]
        ks = [jnp.concatenate([kn_ref[:, hd], k_rope], axis=1) for hd in heads]

        def tile(i, diagonal):
            rows_i = pl.ds(pl.multiple_of(i * ta, ta), ta)
            for n, hd in enumerate(heads):
                q = jnp.concatenate([qn_ref[rows_i, hd], qr_ref[rows_i, hd]], axis=1)
                do_b = do_ref[rows_i, hd]
                s_s[n] = _dot_nt(q, ks[n])
                dp_s[n] = _dot_nt(do_b, v_ref[:, hd])
                for c in range(nchunks):
                    rows = pl.ds(c * ATTN_ROWS, ATTN_ROWS)
                    seq_rows = pl.ds(pl.multiple_of(i * ta + c * ATTN_ROWS, ATTN_ROWS), ATTN_ROWS)
                    s = s_s[n, rows, :]
                    if diagonal:
                        s = jnp.where(_causal_mask(c * ATTN_ROWS, 0, ATTN_ROWS, ta), s, -1e30)
                    p = jnp.exp2(s - jnp.tile(lse_ref[seq_rows, hd], (1, reps)))
                    p_s[n, rows, :] = p.astype(BF16)
                    ds_s[n, rows, :] = (p * (dp_s[n, rows, :] - jnp.tile(dl_ref[seq_rows, hd], (1, reps)))).astype(BF16)
                dv_s[n] += _dot_tn(p_s[n], do_b)
                ds = ds_s[n]
                dk_s[n] += _dot_tn(ds, q)
                dq = _dot(ds, ks[n])
                dqn_ref[rows_i, hd] += dq[:, :LANE]
                dqr_ref[rows_i, hd] += dq[:, LANE:]

        def off_diagonal(i, carry):
            tile(i, False)
            return carry

        tile(j, True)
        lax.fori_loop(j + 1, na, off_diagonal, 0)
        for n, hd in enumerate(heads):
            dkn_ref[:, hd] = (dk_s[n, :, :LANE] * LN2).astype(BF16)
            dkr_ref[:, hd] = (dk_s[n, :, LANE:] * LN2).astype(BF16)
            dv_ref[:, hd] = dv_s[n].astype(BF16)

        @pl.when(j == na - 1)
        def _():
            dqn_out[...] = dqn_ref[...].astype(BF16)
            dqr_out[...] = dqr_ref[...].astype(BF16)

    qspec = pl.BlockSpec((seq, wide), lambda b, h, j: (b, h))
    kspec = pl.BlockSpec((ta, wide), lambda b, h, j: (b * na + j, h))
    krspec = pl.BlockSpec((ta, LANE), lambda b, h, j: (b * na + j, 0))
    return pl.pallas_call(
        body, name="attn_bwd", grid=(nb, nheads // hp, na),
        in_specs=[qspec, qspec, kspec, krspec, kspec, qspec, qspec, qspec],
        out_specs=[qspec, qspec, kspec, kspec, kspec],
        out_shape=[_sds((t_all, hv), BF16)] * 5,
        scratch_shapes=[pltpu.VMEM((hp, ta, ta), F32), pltpu.VMEM((hp, ta, ta), F32), pltpu.VMEM((hp, ta, ta), BF16), pltpu.VMEM((hp, ta, ta), BF16),
                        pltpu.VMEM((hp, ta, 2 * LANE), F32), pltpu.VMEM((hp, ta, LANE), F32), pltpu.VMEM((seq, wide), F32), pltpu.VMEM((seq, wide), F32)],
        compiler_params=_params(3),
    )(qn, qr, kn, kr, v, do, lse, delta)


def _head(o, ub, x1, target, wg, w, lay, tb):
    t_all, d = x1.shape
    hv, qr = lay.hv, lay.qr
    nheads = hv // LANE

    def body(o_ref, ub_ref, x1_ref, tg_ref, wg_ref, gf, loss_ref, dgf_ref, yb_ref, dx2_ref, do_ref, dg_ref, dl_ref, wob, sems):
        @pl.when(pl.program_id(0) == 0)
        def _():
            cps = _fetch(wg_ref, lay, "out_b", wob, sems, 0)
            for cp in cps:
                cp.start()
            loss_ref[...] = jnp.zeros((1, LANE), F32)
            dgf_ref[...] = jnp.zeros((1, d), F32)
            for cp in cps:
                cp.wait()

        ov = o_ref[...].astype(F32)
        g = ub_ref[:, qr:]
        sg = _sigmoid(g)
        silu = g * sg
        yb = (ov * silu).astype(BF16)
        yb_ref[...] = yb
        x2 = x1_ref[...] + _dot(yb, wob[...])
        rinv = _rinv(x2)
        err = x2 * rinv * gf[...] - tg_ref[...]
        loss_ref[...] += (0.5 / d) * jnp.sum(jnp.sum(err * err, axis=1, keepdims=True), axis=0, keepdims=True)
        dx2, dgf = _rms_bwd(x2, rinv, gf[...], err * (1.0 / d))
        dgf_ref[...] += dgf
        dx2_ref[...] = dx2
        dyb = _dot_nt(dx2.astype(BF16), wob[...])
        dov = dyb * silu
        do_ref[...] = dov.astype(BF16)
        dg_ref[...] = (dyb * ov * (sg * (1.0 + g * (1.0 - sg)))).astype(BF16)
        prod = dov * ov
        dl_ref[...] = jnp.concatenate(
            [jnp.broadcast_to(jnp.sum(prod[:, n * LANE:(n + 1) * LANE], axis=1, keepdims=True), (tb, LANE)) for n in range(nheads)], axis=1)

    tok = lambda c: pl.BlockSpec((tb, c), lambda i: (i, 0))
    return pl.pallas_call(
        body, name="head", grid=(t_all // tb,),
        in_specs=[tok(hv), tok(qr + hv), tok(d), tok(d), ANY, _full((1, d))],
        out_specs=[_full((1, LANE)), _full((1, d)), tok(hv), tok(d), tok(hv), tok(hv), tok(hv)],
        out_shape=[_sds((1, LANE)), _sds((1, d)), _sds((t_all, hv), BF16), _sds((t_all, d)), _sds((t_all, hv), BF16), _sds((t_all, hv), BF16),
                   _sds((t_all, hv))],
        scratch_shapes=[pltpu.VMEM((hv, d), BF16), pltpu.SemaphoreType.DMA((N_CHIPS,))],
        compiler_params=_params(1),
    )(o, ub, x1, target, wg, w["final_norm"])


def _fb_bwd(dqn, dqr, dkn, dkr, dv, dgate, ub, ckr, x1, dx2, wg, w, lay, cos_t, sin_t, seq, tb):
    t_all, d = x1.shape
    hv, qr, kvr = lay.hv, lay.qr, lay.kvr
    nheads = hv // LANE
    npos = seq // tb

    def body(dqn_ref, dqr_ref, dkn_ref, dkr_ref, dv_ref, dg_ref, ub_ref, ckr_ref, x1_ref, dx2_ref, wg_ref,
             qn, nb, kvn, wdkv, nkv, cos_ref, sin_ref,
             dx1_ref, dqrp_ref, dqnp_ref, dub_ref, dckr_ref, dqn_g, dnb_g, dkvn_g, dnkv_g, winb, wuk, wuv, wuqn, wuqr, sems):
        @pl.when(pl.program_id(0) == 0)
        def _():
            cps = []
            for n, (key, dst) in enumerate((("in_b", winb), ("uk", wuk), ("uv", wuv), ("uq_n", wuqn), ("uq_r", wuqr))):
                cps += _fetch(wg_ref, lay, key, dst, sems, n * N_CHIPS)
            for cp in cps:
                cp.start()
            dqn_g[...] = jnp.zeros((1, qr), F32)
            dnb_g[...] = jnp.zeros((1, d), F32)
            dkvn_g[...] = jnp.zeros((1, kvr), F32)
            dnkv_g[...] = jnp.zeros((1, d), F32)
            for cp in cps:
                cp.wait()

        cos, sin = cos_ref[...], sin_ref[...]
        xv = x1_ref[...]
        rinv1 = _rinv(xv)
        dqr_v = dqr_ref[...].astype(F32) * ATTN_SCALE
        dqr_pre = (dqr_v * jnp.tile(cos, (1, nheads)) + _swap_halves(dqr_v * jnp.tile(sin, (1, nheads)))).astype(BF16)
        dqrp_ref[...] = dqr_pre
        dqn_pre = (dqn_ref[...].astype(F32) * ATTN_SCALE).astype(BF16)
        dqnp_ref[...] = dqn_pre
        dcq = _dot_nt(dqn_pre, wuqn[...]) + _dot_nt(dqr_pre, wuqr[...])
        cq_pre = ub_ref[:, :qr]
        dcq_pre, g1 = _rms_bwd(cq_pre, _rinv(cq_pre), qn[...], dcq)
        dqn_g[...] += g1
        dub = jnp.concatenate([dcq_pre.astype(BF16), dg_ref[...]], axis=1)
        dub_ref[...] = dub
        dx1_b, g2 = _rms_bwd(xv, rinv1, nb[...], _dot(dub, winb[...]))
        dnb_g[...] += g2
        dkr_all = dkr_ref[...].astype(F32)
        dkr_sum = dkr_all[:, :LANE]
        for n in range(1, nheads):
            dkr_sum = dkr_sum + dkr_all[:, n * LANE:(n + 1) * LANE]
        dckr_rope = dkr_sum * cos + _swap_halves(dkr_sum * sin)
        dckv = _dot_nt(dkn_ref[...].astype(BF16), wuk[...]) + _dot_nt(dv_ref[...].astype(BF16), wuv[...])
        ckv_pre = ckr_ref[:, :kvr]
        dckv_pre, g3 = _rms_bwd(ckv_pre, _rinv(ckv_pre), kvn[...], dckv)
        dkvn_g[...] += g3
        dckr = jnp.concatenate([dckv_pre, dckr_rope], axis=1).astype(BF16)
        dckr_ref[...] = dckr
        dx1_kv, g4 = _rms_bwd(xv, rinv1, nkv[...], _dot_nt(dckr, wdkv[...]))
        dnkv_g[...] += g4
        dx1_ref[...] = dx2_ref[...] + dx1_b + dx1_kv

    tok = lambda c: pl.BlockSpec((tb, c), lambda i: (i, 0))
    pos = pl.BlockSpec((tb, LANE), lambda i: (i % npos, 0))
    consts = [w["q_norm"], w["norm_b"], w["kv_norm"], w["w_dkv_p"], w["norm_kv"]]
    return pl.pallas_call(
        body, name="fb_bwd", grid=(t_all // tb,),
        in_specs=[tok(hv)] * 6 + [tok(qr + hv), tok(kvr + LANE), tok(d), tok(d), ANY] + [_full(c.shape) for c in consts] + [pos, pos],
        out_specs=[tok(d), tok(hv), tok(hv), tok(qr + hv), tok(kvr + LANE), _full((1, qr)), _full((1, d)), _full((1, kvr)), _full((1, d))],
        out_shape=[_sds((t_all, d)), _sds((t_all, hv), BF16), _sds((t_all, hv), BF16), _sds((t_all, qr + hv), BF16), _sds((t_all, kvr + LANE), BF16),
                   _sds((1, qr)), _sds((1, d)), _sds((1, kvr)), _sds((1, d))],
        scratch_shapes=[pltpu.VMEM((qr + hv, d), BF16), pltpu.VMEM((kvr, d), BF16), pltpu.VMEM((kvr, d), BF16), pltpu.VMEM((qr, d), BF16),
                        pltpu.VMEM((qr, d), BF16), pltpu.SemaphoreType.DMA((5 * N_CHIPS,))],
        compiler_params=_params(1),
    )(dqn, dqr, dkn, dkr, dv, dgate, ub, ckr, x1, dx2, wg, *consts, cos_t, sin_t)


def _fa_bwd(dx1, x, u, xb, hs, wg, g16, g32, w, lay, seq, tb):
    t_all, d = x.shape
    dr = lay.dr
    nblocks = w["w_rg"].shape[0]
    nblk = seq // tb
    nt = tb // SUBLANE
    per8 = tb // 8

    def body(dx1_ref, x_ref, u_ref, xb_ref, hs_ref, hh_ref, wg_ref, g16_ref, g32_ref, na, cw, wrg, brg, wig, big, lam,
             gx_ref, du_ref, dna_g, dcw_g, dcb_g, dbrg_g, dbig_g, dlam_g, dwrg_g, dwig_g, got_ref, sib_ref, own_ref,
             wint, wout, hpad, a_s, d_s, g_s, dxpad, carry, sems, send_sems, recv_sems, local_sem):
        b, jj = pl.program_id(0), pl.program_id(1)
        first_block = jj == nblk - 1
        scatter = _ScatterDirect(g16_ref, g32_ref, got_ref, sib_ref, own_ref, send_sems, recv_sems, local_sem, lay, G_GROUPS["early"])

        @pl.when((b == 0) & (jj == 0))
        def _():
            scatter.start()
            cps = _fetch(wg_ref, lay, "in_a", wint, sems, 0) + _fetch(wg_ref, lay, "out_a", wout, sems, N_CHIPS)
            for cp in cps:
                cp.start()
            dna_g[...] = jnp.zeros((1, d), F32)
            dcw_g[...] = jnp.zeros((4, dr), F32)
            dcb_g[...] = jnp.zeros((1, dr), F32)
            dbrg_g[...] = jnp.zeros((1, dr), F32)
            dbig_g[...] = jnp.zeros((1, dr), F32)
            dlam_g[...] = jnp.zeros((1, dr), F32)
            dwrg_g[...] = jnp.zeros((nblocks, LANE, LANE), F32)
            dwig_g[...] = jnp.zeros((nblocks, LANE, LANE), F32)
            for cp in cps:
                cp.wait()

        @pl.when(jj == 0)
        def _():
            dxpad[pl.ds(tb, 8), :] = jnp.zeros((8, dr), F32)
            carry[...] = jnp.zeros((8, dr), F32)

        keep = jnp.where(first_block, 0.0, 1.0)
        dx1v = dx1_ref[...]
        gate = u_ref[:, dr:]
        xpre = u_ref[:, :dr]
        hpad[pl.ds(0, 8), :] = hh_ref[...] * keep
        hpad[pl.ds(8, tb), :] = hs_ref[...]
        xb = xb_ref[...]
        xbb = xb.astype(BF16)
        r, i = _gates(xb, wrg, brg[...], wig, big[...], nblocks)
        sp = _softplus(-lam[...])
        log_a = -LRU_C * r * sp
        a, a2, nem = _decay(log_a)
        mult = jnp.sqrt(nem)
        sg = _sigmoid(gate)
        dy = _dot_nt(dx1v.astype(BF16), wout[...])
        hsv = hs_ref[...]
        dgate = dy * hsv * (sg * (1.0 + gate * (1.0 - sg)))
        a_s[...] = a
        d_s[...] = dy * (gate * sg)
        row = lax.broadcasted_iota(jnp.int32, (8, dr), 0)

        def step(k, c):
            r0 = pl.multiple_of((nt - 1 - k) * 8, 8)
            av = a_s[pl.ds(r0, 8), :]
            dv = d_s[pl.ds(r0, 8), :]
            qv = av * dv
            for s in (1, 2, 4):
                m = row < 8 - s
                a_sh = jnp.where(m, pltpu.roll(av, 8 - s, 0), 1.0)
                q_sh = jnp.where(m, pltpu.roll(qv, 8 - s, 0), 0.0)
                qv = qv + av * q_sh
                av = av * a_sh
            qv = qv + av * c
            g_s[pl.ds(r0, 8), :] = dv + jnp.where(row < 7, pltpu.roll(qv, 7, 0), c)
            return jnp.broadcast_to(qv[0:1, :], qv.shape)

        carry[...] = lax.fori_loop(0, nt, step, carry[...])
        g = g_s[...]
        ix = i * xb
        dlog_a = g * (hpad[pl.ds(7, tb), :] * a - ix * (a2 * lax.rsqrt(nem)))
        dix = g * mult
        dlam_g[...] += -jax.nn.sigmoid(-lam[...]) * jnp.sum(dlog_a * (-LRU_C * r), axis=0, keepdims=True)
        drg = dlog_a * (-LRU_C * sp) * r * (1.0 - r)
        dig = dix * xb * i * (1.0 - i)
        dbrg_g[...] += jnp.sum(drg, axis=0, keepdims=True)
        dbig_g[...] += jnp.sum(dig, axis=0, keepdims=True)
        drgb, digb = drg.astype(BF16), dig.astype(BF16)
        back = []
        for n in range(nblocks):
            cols = slice(n * LANE, (n + 1) * LANE)
            dwrg_g[n] += _dot_tn(xbb[:, cols], drgb[:, cols])
            dwig_g[n] += _dot_tn(xbb[:, cols], digb[:, cols])
            back.append(_dot_nt(drgb[:, cols], wrg[n]) + _dot_nt(digb[:, cols], wig[n]))
        dxb = dix * i + jnp.concatenate(back, axis=1)
        dcb_g[...] += jnp.sum(dxb, axis=0, keepdims=True)
        dxpad[pl.ds(0, tb), :] = dxb
        later = [dxb, dxpad[pl.ds(1, tb), :], dxpad[pl.ds(2, tb), :], dxpad[pl.ds(3, tb), :]]
        dxpad[pl.ds(tb, 8), :] = dxb[:8, :]
        dxpre = cw[3:4, :] * later[0] + cw[2:3, :] * later[1] + cw[1:2, :] * later[2] + cw[0:1, :] * later[3]
        for m in range(4):
            dcw_g[3 - m:4 - m, :] += jnp.sum(later[m] * xpre, axis=0, keepdims=True)
        du = jnp.concatenate([dxpre, dgate], axis=1).astype(BF16)
        du_ref[...] = du
        xv = x_ref[...]
        dxa, g1 = _rms_bwd(xv, _rinv(xv), na[...], _dot(du, wint[...]))
        dna_g[...] += g1
        gx_ref[...] = dx1v + dxa

        @pl.when((b == t_all // seq - 1) & (jj == nblk - 1))
        def _():
            scatter.finish()

    blk = lambda b, j: b * nblk + (nblk - 1 - j)
    tok = lambda c: pl.BlockSpec((tb, c), lambda b, j: (blk(b, j), 0))
    halo = pl.BlockSpec((8, dr), lambda b, j: (jnp.maximum(blk(b, j) * per8 - 1, 0), 0))
    consts = [w["norm_a"], w["conv_w"], w["w_rg"], w["b_rg"], w["w_ig"], w["b_ig"], w["lru_lambda"]]
    vec = lambda c: _full((1, c))
    blocks3 = (nblocks, LANE, LANE)
    return pl.pallas_call(
        body, name="fa_bwd", grid=(t_all // seq, nblk),
        in_specs=[tok(d), tok(d), tok(2 * dr), tok(dr), tok(dr), halo, ANY, ANY, ANY] + [_full(c.shape) for c in consts],
        out_specs=[tok(d), tok(2 * dr), vec(d), _full((4, dr)), vec(dr), vec(dr), vec(dr), vec(dr), _full(blocks3), _full(blocks3), ANY, ANY, ANY],
        out_shape=[_sds((t_all, d)), _sds((t_all, 2 * dr), BF16), _sds((1, d)), _sds((4, dr)), _sds((1, dr)), _sds((1, dr)), _sds((1, dr)),
                   _sds((1, dr)), _sds(blocks3), _sds(blocks3)] + _scatter_direct_shapes(lay, "early"),
        scratch_shapes=[pltpu.VMEM((2 * dr, d), BF16), pltpu.VMEM((dr, d), BF16), pltpu.VMEM((tb + 8, dr), F32),
                        pltpu.VMEM((tb, dr), F32), pltpu.VMEM((tb, dr), F32), pltpu.VMEM((tb, dr), F32), pltpu.VMEM((tb + 8, dr), F32),
                        pltpu.VMEM((8, dr), F32), pltpu.SemaphoreType.DMA((2 * N_CHIPS,))] + SCATTER_DIRECT_SEMS,
        compiler_params=_params(2),
    )(dx1, x, u, xb, hs, hs, wg, g16, g32, *consts)


def _mm_into(gbufs, a, b, off, name, bt):
    t_all, m = a.shape
    n = b.shape[1]
    nsplit = 2 if m >= 1024 and (m // 2) % LANE == 0 else 1
    mh = m // nsplit
    nt = t_all // bt
    nbuf = len(gbufs)
    twin = nbuf == 2

    def body(a_ref, b_ref, *refs):
        outs, acc, sems = refs[nbuf:2 * nbuf], refs[2 * nbuf], refs[-1]
        acc16 = refs[2 * nbuf + 1] if twin else None
        part, t = pl.program_id(0), pl.program_id(1)

        def out_copies(h):
            dst = pl.ds(off + h * mh, mh)
            copies = [pltpu.make_async_copy(acc.at[h], outs[0].at[dst, :], sems.at[0, h])]
            if twin:
                copies.append(pltpu.make_async_copy(acc16.at[h], outs[1].at[dst, :], sems.at[1, h]))
            return copies

        prod = _dot_tn(a_ref[...].astype(BF16), b_ref[...].astype(BF16))
        for h in range(nsplit):
            @pl.when((part == h) & (t == 0))
            def _():
                acc[h] = prod

            @pl.when((part == h) & (t > 0))
            def _():
                acc[h] += prod

            @pl.when((part == h) & (t == nt - 1))
            def _():
                if twin:
                    acc16[h] = acc[h].astype(BF16)
                for cp in out_copies(h):
                    cp.start()

        @pl.when((part == nsplit - 1) & (t == nt - 1))
        def _():
            for h in range(nsplit):
                for cp in out_copies(h):
                    cp.wait()

    scratch = [pltpu.VMEM((nsplit, mh, n), F32)] + ([pltpu.VMEM((nsplit, mh, n), BF16)] if twin else []) + [pltpu.SemaphoreType.DMA((2, nsplit))]
    return pl.pallas_call(
        body, name=name, grid=(nsplit, nt),
        in_specs=[pl.BlockSpec((bt, mh), lambda h, t: (t, h)), pl.BlockSpec((bt, n), lambda h, t: (t, 0))] + [ANY] * nbuf,
        out_specs=[ANY] * nbuf, out_shape=[_sds(g.shape, g.dtype) for g in gbufs], input_output_aliases={2 + k: k for k in range(nbuf)},
        scratch_shapes=scratch, compiler_params=_params(2),
    )(a, b, *gbufs)


def _dw_in_a_exchange(du, h, rest, lay, bt):
    t_all, d = h.shape
    half = d // 2
    rows, rest_rows, c_rows = lay.rows["in_a"], lay.rows["rest"], lay.c_rows["late"]
    c_in, c_rest = lay.c_off["in_a"], lay.c_off["rest"]
    nt = t_all // bt
    xi, yi, _ = _place()
    order = jnp.stack([2 * (1 - xi) + yi, 2 * xi + (1 - yi), 2 * (1 - xi) + (1 - yi), 2 * xi + yi]).astype(jnp.int32)

    def body(order_ref, a_ref, b_ref, rest_ref, got_ref, own_ref, acc, sibbuf, part16, restv, rest_sib, rest_p, rest16, own_v, own_r,
             d2d_send, d2d_recv, ici_send, ici_recv, local_sems):
        x, y, c = _place()
        chips = [(1 - x, y), (x, 1 - y), (1 - x, 1 - y)]
        g, t = pl.program_id(0), pl.program_id(1)
        their_cols = pl.ds(pl.multiple_of((1 - c) * half, LANE), half)

        def my_half(v):
            return jnp.where(c == 0, v[:, :half], v[:, half:])

        def d2d(src, dst, k):
            return pltpu.make_async_remote_copy(src_ref=src, dst_ref=dst, send_sem=d2d_send.at[k], recv_sem=d2d_recv.at[k],
                                                device_id=(x, y, 1 - c), device_id_type=MESH)

        def group_swap(gg):
            return d2d(acc.at[gg % 2, :, their_cols], sibbuf.at[gg], gg)

        def rest_swap():
            return d2d(restv.at[:, their_cols], rest_sib, 4)

        def to_chip(k, src, off, nrows):
            px, py = chips[k]
            return pltpu.make_async_remote_copy(src_ref=src, dst_ref=got_ref.at[k, pl.ds(off, nrows), :], send_sem=ici_send.at[k],
                                                recv_sem=ici_recv.at[k], device_id=(px, py, c), device_id_type=MESH)

        def own_copy(src, off, nrows, k):
            return pltpu.make_async_copy(src, own_ref.at[pl.ds(off, nrows), :], local_sems.at[k])

        def finish_group(gg):
            group_swap(gg).wait()
            part = my_half(acc[gg % 2]) + sibbuf[gg]
            if gg < 3:
                part16[gg] = part.astype(BF16)
                to_chip(gg, part16.at[gg], c_in, rows).start()
            else:
                own_v[...] = part
                own_copy(own_v, c_in, rows, 1).start()

        @pl.when((g == 0) & (t == 0))
        def _():
            load = pltpu.make_async_copy(rest_ref, restv, local_sems.at[0])
            load.start()
            load.wait()
            rest_swap().start()

        prod = _dot_tn(a_ref[...], b_ref[...])
        for gg in range(4):
            @pl.when((g == gg) & (t == 0))
            def _():
                acc[gg % 2] = prod

            @pl.when((g == gg) & (t > 0))
            def _():
                acc[gg % 2] += prod

            @pl.when((g == gg) & (t == nt - 1))
            def _():
                group_swap(gg).start()
                if gg == 0:
                    rest_swap().wait()
                    rest_p[...] = my_half(restv[...]) + rest_sib[...]
                    for k in range(3):
                        chip_rows = pl.ds(pl.multiple_of(order_ref[k] * rest_rows, SUBLANE), rest_rows)
                        rest16[k] = rest_p[chip_rows, :].astype(BF16)
                        to_chip(k, rest16.at[k], c_rest, rest_rows).start()
                    own_r[...] = rest_p[pl.ds(pl.multiple_of(order_ref[3] * rest_rows, SUBLANE), rest_rows), :]
                    own_copy(own_r, c_rest, rest_rows, 2).start()
                else:
                    finish_group(gg - 1)
                if gg == 3:
                    finish_group(3)
                    for k, (px, py) in enumerate(chips):
                        pltpu.make_async_remote_copy(src_ref=got_ref.at[k], dst_ref=got_ref.at[k], send_sem=ici_send.at[k], recv_sem=ici_recv.at[k],
                                                     device_id=(px, py, c), device_id_type=MESH).wait()
                    own_copy(own_v, c_in, rows, 1).wait()
                    own_copy(own_r, c_rest, rest_rows, 2).wait()

    return pl.pallas_call(
        body, name="dw_in_a",
        grid_spec=pltpu.PrefetchScalarGridSpec(
            num_scalar_prefetch=1, grid=(N_CHIPS, nt),
            in_specs=[pl.BlockSpec((bt, rows), lambda g, t, order: (t, order[g])), pl.BlockSpec((bt, d), lambda g, t, order: (t, 0)), ANY],
            out_specs=[ANY, ANY],
            scratch_shapes=[pltpu.VMEM((2, rows, d), F32), pltpu.VMEM((N_CHIPS, rows, half), F32), pltpu.VMEM((3, rows, half), BF16),
                            pltpu.VMEM((N_CHIPS * rest_rows, d), F32), pltpu.VMEM((N_CHIPS * rest_rows, half), F32),
                            pltpu.VMEM((N_CHIPS * rest_rows, half), F32), pltpu.VMEM((3, rest_rows, half), BF16),
                            pltpu.VMEM((rows, half), F32), pltpu.VMEM((rest_rows, half), F32),
                            pltpu.SemaphoreType.DMA((5,)), pltpu.SemaphoreType.DMA((5,)), pltpu.SemaphoreType.DMA((3,)), pltpu.SemaphoreType.DMA((3,)),
                            pltpu.SemaphoreType.DMA((3,))]),
        out_shape=_scatter_shapes(lay, "late", half), compiler_params=_params(2),
    )(order, du, h, rest)


def _mm_tn(a, b, name, bt):
    t_all, m = a.shape
    n = b.shape[1]

    def body(a_ref, b_ref, o_ref):
        @pl.when(pl.program_id(0) == 0)
        def _():
            o_ref[...] = jnp.zeros((m, n), F32)

        o_ref[...] += _dot_tn(a_ref[...].astype(BF16), b_ref[...].astype(BF16))

    return pl.pallas_call(
        body, name=name, grid=(t_all // bt,),
        in_specs=[pl.BlockSpec((bt, m), lambda t: (t, 0)), pl.BlockSpec((bt, n), lambda t: (t, 0))],
        out_specs=_full((m, n)), out_shape=_sds((m, n)),
        compiler_params=_params(1),
    )(a, b)


class _Gather8:
    def __init__(self, x_ref, out_ref, send_sems, recv_sems, local_sem):
        x, y, c = _place()
        self.c, self.me, self.sibling = c, (x, y, c), (x, y, 1 - c)
        self.chips = [(1 - x, y), (x, 1 - y), (1 - x, 1 - y)]
        self.x_ref, self.out_ref, self.send_sems, self.recv_sems, self.local_sem = x_ref, out_ref, send_sems, recv_sems, local_sem

    def _slot(self, px, py, pc):
        return self.out_ref.at[4 * px + 2 * py + pc]

    def _copy(self, k, blk, to, src=None):
        return pltpu.make_async_remote_copy(
            src_ref=self._slot(*blk) if src is None else src, dst_ref=self._slot(*blk), send_sem=self.send_sems.at[k],
            recv_sem=self.recv_sems.at[k], device_id=to, device_id_type=MESH)

    def _mine(self):
        return pltpu.make_async_copy(self.x_ref, self._slot(*self.me), self.local_sem)

    def _first(self):
        return [self._copy(0, self.me, self.sibling, src=self.x_ref)] + [
            self._copy(1 + j, self.me, (*chip, self.c), src=self.x_ref) for j, chip in enumerate(self.chips)]

    def _passed(self):
        return [self._copy(4 + j, (*chip, self.c), self.sibling) for j, chip in enumerate(self.chips)]

    def start(self):
        self._mine().start()
        for cp in self._first():
            cp.start()

    def forward(self):
        passed = self._passed()
        for j, chip in enumerate(self.chips):
            self._copy(1 + j, (*chip, self.c), self.me).wait_recv()
            passed[j].start()

    def finish(self):
        self._copy(0, self.sibling, self.me).wait_recv()
        for j, chip in enumerate(self.chips):
            self._copy(4 + j, (*chip, 1 - self.c), self.me).wait_recv()
        for cp in self._first() + self._passed():
            cp.wait_send()
        self._mine().wait()


class _Scatter:
    def __init__(self, p16_ref, p32_ref, got_ref, own_ref, send_sems, recv_sems, local_sem, lay, order):
        self.x, self.y, self.c = _place()
        self.chips = [(1 - self.x, self.y), (self.x, 1 - self.y), (1 - self.x, 1 - self.y)]
        self.refs = (p16_ref, p32_ref, got_ref, own_ref, send_sems, recv_sems, local_sem)
        self.lay, self.order = lay, order

    def _rows_of(self, ref, key, chip):
        start = pl.multiple_of(self.lay.g_off[key] + chip * self.lay.rows[key], ROW_ALIGN)
        return ref.at[pl.ds(start, self.lay.rows[key]), :]

    def _compact(self, ref, key):
        return ref.at[pl.ds(self.lay.c_off[key], self.lay.rows[key]), :]

    def start(self):
        p16_ref, p32_ref, got_ref, own_ref, send_sems, recv_sems, local_sem = self.refs
        for key in self.order:
            pltpu.make_async_copy(self._rows_of(p32_ref, key, 2 * self.x + self.y), self._compact(own_ref, key), local_sem).start()
        for k, (px, py) in enumerate(self.chips):
            for key in self.order:
                pltpu.make_async_remote_copy(
                    src_ref=self._rows_of(p16_ref, key, 2 * px + py), dst_ref=self._compact(got_ref.at[k], key), send_sem=send_sems.at[k],
                    recv_sem=recv_sems.at[k], device_id=(px, py, self.c), device_id_type=MESH).start()

    def finish(self):
        _, _, got_ref, own_ref, send_sems, recv_sems, local_sem = self.refs
        for k, (px, py) in enumerate(self.chips):
            pltpu.make_async_remote_copy(src_ref=got_ref.at[k], dst_ref=got_ref.at[k], send_sem=send_sems.at[k], recv_sem=recv_sems.at[k],
                                         device_id=(px, py, self.c), device_id_type=MESH).wait()
        pltpu.make_async_copy(own_ref, own_ref, local_sem).wait()


class _ScatterDirect:
    def __init__(self, g16_ref, g32_ref, got_ref, sib_ref, own_ref, send_sems, recv_sems, local_sem, lay, order):
        self.x, self.y, self.c = _place()
        self.chips = [(1 - self.x, self.y), (self.x, 1 - self.y), (1 - self.x, 1 - self.y)]
        self.refs = (g16_ref, g32_ref, got_ref, sib_ref, own_ref, send_sems, recv_sems, local_sem)
        self.lay, self.order, self.half = lay, order, lay.d // 2

    def _src(self, ref, key, chip, h):
        start = pl.multiple_of(self.lay.g_off[key] + chip * self.lay.rows[key], ROW_ALIGN)
        return ref.at[pl.ds(start, self.lay.rows[key]), pl.ds(pl.multiple_of(h * self.half, LANE), self.half)]

    def _compact(self, ref, key):
        return ref.at[pl.ds(self.lay.c_off[key], self.lay.rows[key]), :]

    def start(self):
        g16_ref, g32_ref, got_ref, sib_ref, own_ref, send_sems, recv_sems, local_sem = self.refs
        x, y, c = self.x, self.y, self.c
        for key in self.order:
            pltpu.make_async_copy(self._src(g32_ref, key, 2 * x + y, c), self._compact(own_ref, key), local_sem).start()
            pltpu.make_async_remote_copy(
                src_ref=self._src(g32_ref, key, 2 * x + y, 1 - c), dst_ref=self._compact(sib_ref, key), send_sem=send_sems.at[6],
                recv_sem=recv_sems.at[6], device_id=(x, y, 1 - c), device_id_type=MESH).start()
        for k, (px, py) in enumerate(self.chips):
            for h in range(2):
                for key in self.order:
                    pltpu.make_async_remote_copy(
                        src_ref=self._src(g16_ref, key, 2 * px + py, h), dst_ref=self._compact(got_ref.at[2 * k + c], key),
                        send_sem=send_sems.at[2 * k + h], recv_sem=recv_sems.at[2 * k + c], device_id=(px, py, h), device_id_type=MESH).start()

    def finish(self):
        _, _, got_ref, sib_ref, own_ref, send_sems, recv_sems, local_sem = self.refs
        x, y, c = self.x, self.y, self.c
        for k, (px, py) in enumerate(self.chips):
            for h in range(2):
                whole = pltpu.make_async_remote_copy(src_ref=got_ref.at[2 * k + h], dst_ref=got_ref.at[2 * k + h], send_sem=send_sems.at[2 * k + h],
                                                     recv_sem=recv_sems.at[2 * k + h], device_id=(px, py, h), device_id_type=MESH)
                whole.wait_send()
                whole.wait_recv()
        pltpu.make_async_remote_copy(src_ref=sib_ref, dst_ref=sib_ref, send_sem=send_sems.at[6], recv_sem=recv_sems.at[6],
                                     device_id=(x, y, 1 - c), device_id_type=MESH).wait()
        pltpu.make_async_copy(own_ref, own_ref, local_sem).wait()


def _scatter_direct_shapes(lay, group):
    rows, half = lay.c_rows[group], lay.d // 2
    return [_sds((6, rows, half), BF16), _sds((rows, half), F32), _sds((rows, half), F32)]


SCATTER_DIRECT_SEMS = [pltpu.SemaphoreType.DMA((7,)), pltpu.SemaphoreType.DMA((7,)), pltpu.SemaphoreType.DMA]
SCATTER_SEMS = [pltpu.SemaphoreType.DMA((3,)), pltpu.SemaphoreType.DMA((3,)), pltpu.SemaphoreType.DMA]
GATHER_SEMS = [pltpu.SemaphoreType.DMA((7,)), pltpu.SemaphoreType.DMA((7,)), pltpu.SemaphoreType.DMA]


def _all_gather8(blocks, name):
    nb = len(blocks)

    def body(*refs):
        x_refs, out_refs = refs[:nb], refs[nb:2 * nb]
        send_sems, recv_sems, local_sems = refs[2 * nb:]
        gathers = [_Gather8(x_refs[n], out_refs[n], send_sems.at[n], recv_sems.at[n], local_sems.at[n]) for n in range(nb)]
        for g in gathers:
            g.start()
        for g in gathers:
            g.forward()
        for g in gathers:
            g.finish()

    return pl.pallas_call(
        body, name=name, out_shape=[_sds((8,) + b.shape, b.dtype) for b in blocks], in_specs=[ANY] * nb, out_specs=[ANY] * nb,
        scratch_shapes=[pltpu.SemaphoreType.DMA((nb, 7)), pltpu.SemaphoreType.DMA((nb, 7)), pltpu.SemaphoreType.DMA((nb,))],
    )(*blocks)


def _swap_sibling(srcs, name, half_cols=False):
    n = len(srcs)
    halves = [s.shape[1] // 2 if half_cols else s.shape[1] for s in srcs]

    def body(*refs):
        src_refs, out_refs, send_sems, recv_sems = refs[:n], refs[n:2 * n], refs[2 * n], refs[2 * n + 1]
        x, y, c = _place()
        copies = []
        for k in range(n):
            part = src_refs[k].at[:, pl.ds(pl.multiple_of((1 - c) * halves[k], LANE), halves[k])] if half_cols else src_refs[k]
            copies.append(pltpu.make_async_remote_copy(src_ref=part, dst_ref=out_refs[k], send_sem=send_sems.at[k], recv_sem=recv_sems.at[k],
                                                       device_id=(x, y, 1 - c), device_id_type=MESH))
        for cp in copies:
            cp.start()
        for cp in copies:
            cp.wait()

    return pl.pallas_call(
        body, name=name, out_shape=[_sds((s.shape[0], h), s.dtype) for s, h in zip(srcs, halves)], in_specs=[ANY] * n, out_specs=[ANY] * n,
        scratch_shapes=[pltpu.SemaphoreType.DMA((n,)), pltpu.SemaphoreType.DMA((n,))],
    )(*srcs)


def _return_and_gather(mines, rep_block):
    n = len(mines)

    def body(*refs):
        src_refs, rep_ref, out_refs, rep_out = refs[:n], refs[n], refs[n + 1:2 * n + 1], refs[2 * n + 1]
        send_sems, recv_sems, g_send, g_recv, g_local = refs[2 * n + 2:]
        x, y, c = _place()
        copies = [pltpu.make_async_remote_copy(src_ref=src_refs[k], dst_ref=out_refs[k], send_sem=send_sems.at[k], recv_sem=recv_sems.at[k],
                                               device_id=(x, y, 1 - c), device_id_type=MESH) for k in range(n)]
        gather = _Gather8(rep_ref, rep_out, g_send, g_recv, g_local)
        for cp in copies:
            cp.start()
        gather.start()
        gather.forward()
        gather.finish()
        for cp in copies:
            cp.wait()

    return pl.pallas_call(
        body, name="rs_return", out_shape=[_sds(m.shape, m.dtype) for m in mines] + [_sds((8,) + rep_block.shape, rep_block.dtype)],
        in_specs=[ANY] * (n + 1), out_specs=[ANY] * (n + 1),
        scratch_shapes=[pltpu.SemaphoreType.DMA((n,)), pltpu.SemaphoreType.DMA((n,))] + GATHER_SEMS,
    )(*mines, rep_block)


def _scatter_shapes(lay, group, half):
    return [_sds((3, lay.c_rows[group], half), BF16), _sds((lay.c_rows[group], half), F32)]


def _scatter_chips(part16, part32, lay, group, name):
    def body(p16_ref, p32_ref, got_ref, own_ref, send_sems, recv_sems, local_sem):
        sc = _Scatter(p16_ref, p32_ref, got_ref, own_ref, send_sems, recv_sems, local_sem, lay, G_GROUPS[group])
        sc.start()
        sc.finish()

    return pl.pallas_call(
        body, name=name, out_shape=_scatter_shapes(lay, group, part16.shape[1]), in_specs=[ANY, ANY], out_specs=[ANY, ANY],
        scratch_shapes=SCATTER_SEMS,
    )(part16, part32)


def _sum_sibling(gbuf, got, cidx, name):
    rows, d = gbuf.shape
    half = d // 2
    rb = _row_block(rows)

    def body(c_ref, g_ref, r_ref, o32_ref, o16_ref):
        del c_ref
        s = g_ref[...] + r_ref[...]
        o32_ref[...] = s
        o16_ref[...] = s.astype(BF16)

    plain = pl.BlockSpec((rb, half), lambda i, c: (i, 0))
    return pl.pallas_call(
        body, name=name,
        grid_spec=pltpu.PrefetchScalarGridSpec(num_scalar_prefetch=1, grid=(rows // rb,),
                                               in_specs=[pl.BlockSpec((rb, half), lambda i, c: (i, c[0])), plain], out_specs=[plain, plain]),
        out_shape=[_sds((rows, half)), _sds((rows, half), BF16)], compiler_params=_params(1),
    )(cidx, gbuf, got)


def _sum_devices(own, sib, got, name):
    rows, half = own.shape
    rb = _row_block(rows)
    n = got.shape[0]

    def body(a_ref, s_ref, b_ref, o_ref):
        acc = a_ref[...] + s_ref[...]
        for k in range(n):
            acc = acc + b_ref[k].astype(F32)
        o_ref[...] = acc

    spec = pl.BlockSpec((rb, half), lambda i: (i, 0))
    return pl.pallas_call(
        body, name=name, grid=(rows // rb,), in_specs=[spec, spec, pl.BlockSpec((n, rb, half), lambda i: (0, i, 0))], out_specs=spec,
        out_shape=_sds((rows, half)), compiler_params=_params(1),
    )(own, sib, got)


def _sum_chips(own, got, name):
    rows, half = own.shape
    rb = _row_block(rows)

    def body(a_ref, b_ref, o_ref):
        o_ref[...] = ((a_ref[...] + b_ref[0].astype(F32)) + b_ref[1].astype(F32)) + b_ref[2].astype(F32)

    spec = pl.BlockSpec((rb, half), lambda i: (i, 0))
    return pl.pallas_call(
        body, name=name, grid=(rows // rb,), in_specs=[spec, pl.BlockSpec((3, rb, half), lambda i: (0, i, 0))], out_specs=spec,
        out_shape=_sds((rows, half)), compiler_params=_params(1),
    )(own, got)


def _adamw(w, g, m, v):
    m = ADAM_B1 * m + (1.0 - ADAM_B1) * g
    v = ADAM_B2 * v + (1.0 - ADAM_B2) * (g * g)
    m_hat = m / (1.0 - ADAM_B1 ** ADAM_STEP)
    v_hat = v / (1.0 - ADAM_B2 ** ADAM_STEP)
    return -ADAM_LR * (m_hat / (jnp.sqrt(v_hat) + ADAM_EPS) + ADAM_WD * w), m, v


def _adamw_rows(name, w, g, m, v):
    _, rows, cols = w.shape
    rb = _row_block(rows, 256)

    def body(w_ref, g_ref, m_ref, v_ref, d_ref, mo_ref, vo_ref):
        d_ref[...], mo_ref[...], vo_ref[...] = _adamw(w_ref[...], g_ref[...], m_ref[...], v_ref[...])

    spec = pl.BlockSpec((1, rb, cols), lambda i: (0, i, 0))
    return pl.pallas_call(
        body, name=name, grid=(rows // rb,), in_specs=[spec] * 4, out_specs=[spec] * 3, out_shape=[_sds(w.shape)] * 3,
        compiler_params=_params(1),
    )(w, g, m, v)


def _adamw_group(ws, gs, ms, vs):
    n = len(ws)

    def body(*refs):
        for k in range(n):
            w_ref, g_ref, m_ref, v_ref = (refs[j * n + k] for j in range(4))
            outs = _adamw(w_ref[...], g_ref[...], m_ref[...], v_ref[...])
            for j in range(3):
                refs[(4 + j) * n + k][...] = outs[j]

    outs = pl.pallas_call(
        body, name="adamw_small", out_shape=[_sds(w.shape) for w in ws] * 3,
        compiler_params=pltpu.CompilerParams(vmem_limit_bytes=VMEM_LIMIT),
    )(*ws, *gs, *ms, *vs)
    return outs[:n], outs[n:2 * n], outs[2 * n:]


def _gather_weights(sh, lay):
    x, y, c = _place()
    d = lay.d
    uq = sh["w_uq"][0].astype(BF16)
    parts = {
        "in_b": sh["w_in_b"][0].T.astype(BF16), "in_a": sh["w_in_a"][0].T.astype(BF16), "out_a": sh["w_out_a"][0].astype(BF16),
        "out_b": sh["w_out_b"][0].astype(BF16), "uk": sh["w_uk"].astype(BF16).reshape(-1, d), "uv": sh["w_uv"].astype(BF16).reshape(-1, d),
        "uq_n": uq[:, :, :QK_NOPE].reshape(-1, d), "uq_r": jnp.pad(uq[:, :, QK_NOPE:], ((0, 0), (0, 0), (0, LANE - QK_ROPE))).reshape(-1, d),
        "dkv": jnp.pad(sh["w_dkv"].astype(BF16), ((0, 0), (0, LANE - QK_ROPE))).reshape(-1, d),
    }
    halves = {}
    for group, order in W_GROUPS.items():
        stack = jnp.concatenate([parts[k] for k in order], axis=0).reshape(2, lay.w_rows[group] // 2, d)
        halves[group] = lax.dynamic_index_in_dim(stack, c, 0, keepdims=False)
    small = jnp.concatenate([sh[k].reshape(-1) for k in SMALL])
    n_small = small.shape[0]
    width = _round_up(n_small, 2 * SUBLANE * LANE) // (2 * SUBLANE)
    small = jnp.pad(small, (0, 2 * SUBLANE * width - n_small)).reshape(2, SUBLANE, width)
    wg, sg = _all_gather8([halves["a"], lax.dynamic_index_in_dim(small, c, 0, keepdims=False)], "ag_weights")
    wg = wg.reshape(N_CHIPS, lay.w_rows["a"], d)
    sg = sg.reshape(N_CHIPS, 2 * SUBLANE * width)
    full, off = {}, 0
    for k in SMALL:
        n = sh[k].size
        piece = sg[:, off:off + n]
        off += n
        if k == "conv_w":
            full[k] = piece.reshape(N_CHIPS, 4, n // 4).transpose(1, 0, 2).reshape(4, n)
        else:
            full[k] = piece.reshape(1, N_CHIPS * n)
    return wg, halves["b"], full


def _chip_split(g, taps=False):
    if taps:
        n = g.shape[1] // N_CHIPS
        return g.reshape(4, N_CHIPS, n).transpose(1, 0, 2).reshape(N_CHIPS, 4 * n)
    return g.reshape(N_CHIPS, -1)


def kernel(x, norm_a, w_in_a, conv_w, conv_b, w_rg, b_rg, w_ig, b_ig, lru_lambda, w_out_a, norm_kv, w_dkv, kv_norm, w_uk, w_uv, norm_b, w_in_b, q_norm, w_uq, w_out_b, final_norm, loss_target, m_norm_a, m_w_in_a, m_conv_w, m_conv_b, m_w_rg, m_b_rg, m_w_ig, m_b_ig, m_lru_lambda, m_w_out_a, m_norm_kv, m_w_dkv, m_kv_norm, m_w_uk, m_w_uv, m_norm_b, m_w_in_b, m_q_norm, m_w_uq, m_w_out_b, m_final_norm, v_norm_a, v_w_in_a, v_conv_w, v_conv_b, v_w_rg, v_b_rg, v_w_ig, v_b_ig, v_lru_lambda, v_w_out_a, v_norm_kv, v_w_dkv, v_kv_norm, v_w_uk, v_w_uv, v_norm_b, v_w_in_b, v_q_norm, v_w_uq, v_w_out_b, v_final_norm):
    given = dict(locals())
    sh = {k: given[k] for k in WEIGHTS}
    xi, yi, ci = _place()
    nb, seq, d = x.shape
    t_all = nb * seq
    tb_a, tb_b, ta, bt = min(TOKENS_A, seq), min(TOKENS_B, seq), min(TOKENS_ATTN, seq), min(TOKENS_MM, t_all)
    dr = conv_b.shape[1] * N_CHIPS
    qr, kvr, nheads = q_norm.shape[1], kv_norm.shape[0], w_uk.shape[1]
    hv = nheads * LANE
    n_small = sum(sh[k].size for k in SMALL)
    n_repl = sum(sh[k].size for k in REPL)
    lay = _Layout(d, dr, qr, kvr, hv, n_small, n_repl)
    half = d // 2

    wga, wb_half, w = _gather_weights(sh, lay)
    w.update({"w_rg": w_rg[0].astype(BF16), "w_ig": w_ig[0].astype(BF16), "norm_kv": norm_kv[None, :],
              "kv_norm": kv_norm[None, :], "final_norm": final_norm[None, :], "norm_b": norm_b, "q_norm": q_norm})
    cos_t, sin_t = _rope_tables(seq)
    cidx = jnp.reshape(ci, (1,)).astype(jnp.int32)

    x0 = x.reshape(t_all, d)
    x1, u, hs, h, y, xb, wgb = _fa_fwd(x0, wga, wb_half, w, lay, seq, tb_a)
    wgb = wgb.reshape(N_CHIPS, lay.w_rows["b"], d)
    w["w_dkv_p"] = wgb[:, lay.w_off["dkv"]:lay.w_off["dkv"] + lay.rows["dkv"], :].reshape(d, kvr + LANE)
    qn, qrp, kn, kr, v, ub, ckr, hb, hk, cq, ckv = _fb_fwd(x1, wgb, w, lay, cos_t, sin_t, seq, tb_b)
    o, lse = _attn_fwd(qn, qrp, kn, kr, v, seq, ta)
    loss, g_final_norm, yb, dx2, do, dgate, delta = _head(o, ub, x1, loss_target.reshape(t_all, d), wgb, w, lay, tb_b)
    dqn, dqr, dkn, dkr, dv = _attn_bwd(qn, qrp, kn, kr, v, do, lse, delta, seq, ta)
    dx1, dqr_pre, dqn_pre, dub, dckr, g_q_norm, g_norm_b, g_kv_norm, g_norm_kv = _fb_bwd(
        dqn, dqr, dkn, dkr, dv, dgate, ub, ckr, x1, dx2, wgb, w, lay, cos_t, sin_t, seq, tb_b)
    loss = lax.psum(loss[0, 0], ("x", "y", "c"))

    gbufs = [lax.empty((lay.g_rows["early"], d), F32), lax.empty((lay.g_rows["early"], d), BF16)]
    for key, a, b in (("in_b", dub, hb), ("out_a", y, dx1), ("out_b", yb, dx2), ("uk", ckv, dkn), ("uv", ckv, dv), ("uq_n", cq, dqn_pre),
                      ("uq_r", cq, dqr_pre)):
        gbufs = _mm_into(gbufs, a, b, lay.g_off[key], "dw_" + key, bt)
    g_dkv = _mm_tn(hk, dckr, "dw_dkv", bt)
    gx, du, g_norm_a, g_conv_w, g_conv_b, g_b_rg, g_b_ig, g_lam, g_w_rg, g_w_ig, others, sib, own = _fa_bwd(
        dx1, x0, u, xb, hs, wga, gbufs[1], gbufs[0], w, lay, seq, tb_a)
    mine_early = _sum_devices(own, sib, others, "rs_sum_early")

    small = jnp.concatenate([_chip_split(g_norm_a), _chip_split(g_conv_w, taps=True), _chip_split(g_conv_b), _chip_split(g_b_rg),
                             _chip_split(g_b_ig), _chip_split(g_lam)], axis=1)
    small = jnp.pad(small, ((0, 0), (0, lay.small_rows * d - small.shape[1]))).reshape(N_CHIPS, lay.small_rows, d)
    repl_parts = {"w_rg": g_w_rg, "w_ig": g_w_ig, "norm_kv": g_norm_kv, "kv_norm": g_kv_norm, "norm_b": g_norm_b, "q_norm": g_q_norm,
                  "final_norm": g_final_norm}
    repl = jnp.concatenate([repl_parts[k].reshape(-1) for k in REPL])
    repl = jnp.pad(repl, (0, N_CHIPS * lay.repl_rows * d - n_repl)).reshape(N_CHIPS, lay.repl_rows, d)
    pad_rows = lay.rows["rest"] - lay.rows["dkv"] - lay.small_rows - lay.repl_rows
    rest = jnp.concatenate([g_dkv.reshape(N_CHIPS, lay.rows["dkv"], d), small, repl, jnp.zeros((N_CHIPS, pad_rows, d), F32)], axis=1)
    others, own = _dw_in_a_exchange(du, h, rest.reshape(N_CHIPS * lay.rows["rest"], d), lay, bt)
    mine_late = _sum_chips(own, others, "rs_sum_chips_late")

    r0 = lay.c_off["rest"] + lay.rows["dkv"] + lay.small_rows
    theirs_early, theirs_late, rep_all = _return_and_gather([mine_early, mine_late], mine_late[r0:r0 + lay.repl_rows])
    red = {}
    for group, mine, theirs in (("early", mine_early, theirs_early), ("late", mine_late, theirs_late)):
        red[group] = jnp.concatenate([jnp.where(ci == 0, mine, theirs), jnp.where(ci == 0, theirs, mine)], axis=1)
    rep_flat =rep_all.reshape(N_CHIPS, 2, lay.repl_rows, half).transpose(0, 2, 1, 3).reshape(-1)

    def rows(key):
        group = "late" if key in G_GROUPS["late"] else "early"
        return red[group][lay.c_off[key]:lay.c_off[key] + lay.rows[key]]

    grads = {"w_in_b": rows("in_b").T[None], "w_in_a": rows("in_a").T[None], "w_out_a": rows("out_a")[None], "w_out_b": rows("out_b")[None],
             "w_uk": rows("uk").reshape(w_uk.shape), "w_uv": rows("uv").reshape(w_uv.shape)}
    uq_n = rows("uq_n").reshape(qr // N_CHIPS, nheads, LANE)
    uq_r = rows("uq_r").reshape(qr // N_CHIPS, nheads, LANE)[:, :, :QK_ROPE]
    grads["w_uq"] = jnp.concatenate([uq_n, uq_r], axis=2)[None]
    rest_red = rows("rest")
    grads["w_dkv"] = rest_red[:lay.rows["dkv"]].reshape(d // N_CHIPS, kvr + LANE)[:, :kvr + QK_ROPE]
    small_red = rest_red[lay.rows["dkv"]:lay.rows["dkv"] + lay.small_rows].reshape(-1)
    off = 0
    for k in SMALL:
        n = sh[k].size
        grads[k] = small_red[off:off + n].reshape(sh[k].shape)
        off += n
    off = 0
    for k in REPL:
        n = sh[k].size
        grads[k] = rep_flat[off:off + n].reshape(sh[k].shape)
        off += n

    new = {}
    for k in ("w_in_a", "w_in_b", "w_out_a", "w_out_b"):
        view = (lambda a: jnp.swapaxes(a, 1, 2)) if k in TRANSPOSED else (lambda a: a)
        outs = _adamw_rows("adamw_" + k, view(sh[k]), view(grads[k]), view(given["m_" + k]), view(given["v_" + k]))
        new[k] = tuple(view(a) for a in outs)
    rest_names = [k for k in WEIGHTS if k not in new]

    def as2d(k, a):
        return a.T if k in TRANSPOSED else a[None, :] if a.ndim == 1 else a

    ds, ms, vs = _adamw_group([as2d(k, sh[k]) for k in rest_names], [as2d(k, grads[k]) for k in rest_names],
                              [as2d(k, given["m_" + k]) for k in rest_names], [as2d(k, given["v_" + k]) for k in rest_names])
    for n, k in enumerate(rest_names):
        new[k] = tuple((a.T if k in TRANSPOSED else a).reshape(sh[k].shape) for a in (ds[n], ms[n], vs[n]))
    return (loss, gx.reshape(nb, seq, d), *[grads[k] for k in WEIGHTS], *[new[k][0] for k in WEIGHTS], *[new[k][1] for k in WEIGHTS],
            *[new[k][2] for k in WEIGHTS])
```

```python
import jax
import jax.numpy as jnp
from jax import lax
from jax.experimental import pallas as pl
from jax.experimental.pallas import tpu as pltpu

F32, BF16 = jnp.float32, jnp.bfloat16
EPS = 1e-6
LRU_C = 8.0
ROPE_THETA = 10000.0
QK_NOPE, QK_ROPE = 128, 64
ATTN_SCALE = (QK_NOPE + QK_ROPE) ** -0.5
LN2 = 0.6931471805599453
Q_SCALE = ATTN_SCALE / LN2
ATTN_HEADS, ATTN_HEADS_BWD = 4, 2
ATTN_ROWS = 64
LANE = 128
SUBLANE = 8
ROW_ALIGN = 32
VMEM_LIMIT = 60000 * 1024
ADAM_LR, ADAM_B1, ADAM_B2, ADAM_EPS, ADAM_WD, ADAM_STEP = 0.001, 0.9, 0.999, 1e-08, 0.01, 10
MESH = pl.DeviceIdType.MESH
ANY = pl.BlockSpec(memory_space=pl.ANY)
N_CHIPS = 4
TOKENS_A, TOKENS_B, TOKENS_ATTN, TOKENS_MM = 256, 512, 512, 2048

SMALL = ("norm_a", "conv_w", "conv_b", "b_rg", "b_ig", "lru_lambda")
REPL = ("w_rg", "w_ig", "norm_kv", "kv_norm", "norm_b", "q_norm", "final_norm")
TRANSPOSED = ("w_in_b", "w_dkv")
WEIGHTS = ("norm_a", "w_in_a", "conv_w", "conv_b", "w_rg", "b_rg", "w_ig", "b_ig", "lru_lambda", "w_out_a", "norm_kv",
           "w_dkv", "kv_norm", "w_uk", "w_uv", "norm_b", "w_in_b", "q_norm", "w_uq", "w_out_b", "final_norm")
W_GROUPS = {"a": ("in_a", "out_a"), "b": ("in_b", "out_b", "uk", "uv", "uq_n", "uq_r", "dkv")}
G_GROUPS = {"early": ("in_b", "out_a", "out_b", "uk", "uv", "uq_n", "uq_r"), "late": ("in_a", "rest")}


def _sds(shape, dtype=F32):
    return jax.ShapeDtypeStruct(tuple(shape), dtype)


def _params(n_grid):
    return pltpu.CompilerParams(dimension_semantics=("arbitrary",) * n_grid, vmem_limit_bytes=VMEM_LIMIT)


def _full(shape):
    nd = len(shape)
    return pl.BlockSpec(tuple(shape), lambda *g: (0,) * nd)


def _round_up(n, k):
    return -(-n // k) * k


def _row_block(rows, cap=512):
    best = SUBLANE
    for r in range(SUBLANE, min(rows, cap) + 1, SUBLANE):
        if rows % r == 0:
            best = r
    return best


def _place():
    return lax.axis_index("x"), lax.axis_index("y"), lax.axis_index("c")


class _Layout:
    def __init__(self, d, dr, qr, kvr, hv, n_small, n_repl):
        assert hv == d, "the packed rows are D_MODEL wide, which must equal heads * 128"
        self.d, self.dr, self.qr, self.kvr, self.hv = d, dr, qr, kvr, hv
        per_chip = {"in_b": (qr + hv) // N_CHIPS, "in_a": 2 * dr // N_CHIPS, "out_a": dr // N_CHIPS, "out_b": hv // N_CHIPS,
                    "uk": kvr // N_CHIPS, "uv": kvr // N_CHIPS, "uq_n": qr // N_CHIPS, "uq_r": qr // N_CHIPS,
                    "dkv": (d // N_CHIPS) * (kvr + LANE) // d}
        assert all(r % ROW_ALIGN == 0 for r in per_chip.values()), per_chip
        self.small_rows = _round_up(-(-n_small // d), SUBLANE)
        self.repl_rows = _round_up(-(-n_repl // (N_CHIPS * d)), SUBLANE)
        per_chip["rest"] = _round_up(per_chip["dkv"] + self.small_rows + self.repl_rows, ROW_ALIGN)
        self.rows = per_chip
        self.w_off, self.w_rows = {}, {}
        for group, order in W_GROUPS.items():
            off = 0
            for k in order:
                self.w_off[k] = off
                off += per_chip[k]
            assert off % ROW_ALIGN == 0, (group, off)
            self.w_rows[group] = off
        self.g_off, self.c_off, self.c_rows, self.g_rows = {}, {}, {}, {}
        for group, order in G_GROUPS.items():
            off = 0
            for k in order:
                self.c_off[k] = off
                self.g_off[k] = N_CHIPS * off
                off += per_chip[k]
            self.c_rows[group] = off
            self.g_rows[group] = N_CHIPS * off


def _dot(a, b):
    return jnp.dot(a, b, preferred_element_type=F32)


def _dot_nt(a, b):
    return lax.dot_general(a, b, (((1,), (1,)), ((), ())), preferred_element_type=F32)


def _dot_tn(a, b):
    return lax.dot_general(a, b, (((0,), (0,)), ((), ())), preferred_element_type=F32)


def _rinv(x):
    return lax.rsqrt(jnp.mean(x * x, axis=-1, keepdims=True) + EPS)


def _rms_bwd(x, rinv, g, dy):
    z = dy * g
    dx = rinv * z - x * (rinv * rinv * rinv) * jnp.mean(z * x, axis=-1, keepdims=True)
    dg = jnp.sum(dy * (x * rinv), axis=0, keepdims=True)
    return dx, dg


def _softplus(z):
    return jnp.maximum(z, 0.0) + jnp.log1p(jnp.exp(-jnp.abs(z)))


def _sigmoid(x):
    return 0.5 * jnp.tanh(0.5 * x) + 0.5


def _decay(log_a):
    a = jnp.exp(log_a)
    a2 = a * a
    return a, a2, -jnp.tanh(log_a) * (a2 + 1.0)


def _swap_halves(x):
    w = x.shape[1]
    lane = lax.broadcasted_iota(jnp.int32, x.shape, 1)
    return jnp.where(lane % QK_ROPE < QK_ROPE // 2, pltpu.roll(x, w - QK_ROPE // 2, 1), pltpu.roll(x, QK_ROPE // 2, 1))


def _rope_tables(seq):
    pos = jnp.arange(seq, dtype=F32)
    inv = ROPE_THETA ** (-jnp.arange(0, QK_ROPE, 2, dtype=F32) / QK_ROPE)
    ang = pos[:, None] * inv[None, :]
    cos, sin = jnp.cos(ang), jnp.sin(ang)
    zero = jnp.zeros((seq, LANE - QK_ROPE), F32)
    return jnp.concatenate([cos, cos, zero], 1), jnp.concatenate([-sin, sin, zero], 1)


def _fetch(wg_ref, lay, key, dst, sems, k0):
    rows = lay.rows[key]
    return [pltpu.make_async_copy(wg_ref.at[p, pl.ds(lay.w_off[key], rows), :], dst.at[pl.ds(p * rows, rows), :], sems.at[k0 + p])
            for p in range(N_CHIPS)]


def _gates(xb, wrg_ref, brg, wig_ref, big, nblocks):
    xbb = xb.astype(BF16)
    rg = [_dot(xbb[:, n * LANE:(n + 1) * LANE], wrg_ref[n]) for n in range(nblocks)]
    ig = [_dot(xbb[:, n * LANE:(n + 1) * LANE], wig_ref[n]) for n in range(nblocks)]
    r = _sigmoid(jnp.concatenate(rg, axis=1) + brg)
    i = _sigmoid(jnp.concatenate(ig, axis=1) + big)
    return r, i


def _conv(xpad, cw_ref, cb, tb):
    return (cb + cw_ref[3:4, :] * xpad[pl.ds(8, tb), :] + cw_ref[2:3, :] * xpad[pl.ds(7, tb), :]
            + cw_ref[1:2, :] * xpad[pl.ds(6, tb), :] + cw_ref[0:1, :] * xpad[pl.ds(5, tb), :])


def _fa_inproj(x, wa_half, w, lay, tb):
    t_all, d = x.shape
    rows = lay.rows["in_a"]
    nt = t_all // tb
    xi, yi, _ = _place()
    order = jnp.stack([2 * xi + yi, 2 * (1 - xi) + yi, 2 * xi + (1 - yi), 2 * (1 - xi) + (1 - yi)]).astype(jnp.int32)

    assert lay.w_off["in_a"] == 0
    half_rows = wa_half.shape[0]

    def body(order_ref, x_ref, wah_ref, na, u_ref, wa_ref, wbuf, fetch_sems, send_sems, recv_sems, local_sem):
        g, t = pl.program_id(0), pl.program_id(1)
        gather = _Gather8(wah_ref, wa_ref, send_sems, recv_sems, local_sem)

        def fetch(gg):
            chip, first = order_ref[gg], min(rows, half_rows)
            cps = [pltpu.make_async_copy(wa_ref.at[2 * chip, pl.ds(0, first), :], wbuf.at[pl.ds(0, first), :], fetch_sems.at[0])]
            if rows > first:
                cps.append(pltpu.make_async_copy(wa_ref.at[2 * chip + 1, pl.ds(0, rows - first), :], wbuf.at[pl.ds(first, rows - first), :],
                                                 fetch_sems.at[1]))
            for cp in cps:
                cp.start()
            for cp in cps:
                cp.wait()

        @pl.when((g == 0) & (t == 0))
        def _():
            gather.start()
            gather.wait_own_chip()
            fetch(0)

        for gg in range(1, N_CHIPS):
            @pl.when((g == gg) & (t == 0))
            def _():
                gather.wait_chip(gg - 1)
                fetch(gg)

        xv = x_ref[...]
        u_ref[...] = _dot_nt((xv * _rinv(xv) * na[...]).astype(BF16), wbuf[...])

        @pl.when((g == N_CHIPS - 1) & (t == nt - 1))
        def _():
            gather.wait_sends()

    return pl.pallas_call(
        body, name="fa_inproj",
        grid_spec=pltpu.PrefetchScalarGridSpec(
            num_scalar_prefetch=1, grid=(N_CHIPS, nt),
            in_specs=[pl.BlockSpec((tb, d), lambda g, t, order: (t, 0)), ANY, _full((1, d))],
            out_specs=[pl.BlockSpec((tb, rows), lambda g, t, order: (t, order[g])), ANY],
            scratch_shapes=[pltpu.VMEM((rows, d), BF16), pltpu.SemaphoreType.DMA((2,))] + GATHER_SEMS),
        out_shape=[_sds((t_all, N_CHIPS * rows)), _sds((8,) + wa_half.shape, BF16)],
        compiler_params=_params(2),
    )(order, x, wa_half, w["norm_a"])


def _fa_fwd(x, u, wg, wb_half, w, lay, seq, tb):
    t_all, d = x.shape
    dr = lay.dr
    nblocks = w["w_rg"].shape[0]
    nblk = seq // tb
    nt = tb // SUBLANE
    nsteps = (t_all // seq) * nblk

    def body(x_ref, u_ref, wg_ref, wbh_ref, cw, cb, wrg, brg, wig, big, lam, x1_ref, hs_ref, y_ref, xb_ref, wb_ref,
             wout, xpad, a_s, b_s, carry, sems, send_sems, recv_sems, local_sem):
        step_no = pl.program_id(0) * nblk + pl.program_id(1)
        gather = _Gather8(wbh_ref, wb_ref, send_sems, recv_sems, local_sem)

        @pl.when(step_no == 0)
        def _():
            gather.start()
            cps = _fetch(wg_ref, lay, "out_a", wout, sems, 0)
            for cp in cps:
                cp.start()
            for cp in cps:
                cp.wait()

        @pl.when(step_no == nsteps // 2)
        def _():
            gather.forward()

        @pl.when(pl.program_id(1) == 0)
        def _():
            xpad[pl.ds(0, 8), :] = jnp.zeros((8, dr), F32)
            carry[...] = jnp.zeros((8, dr), F32)

        xpre, gate = u_ref[:, :dr], u_ref[:, dr:]
        xpad[pl.ds(8, tb), :] = xpre
        xb = _conv(xpad, cw, cb[...], tb)
        xb_ref[...] = xb
        xpad[pl.ds(0, 8), :] = xpre[tb - 8:, :]
        r, i = _gates(xb, wrg, brg[...], wig, big[...], nblocks)
        log_a = -LRU_C * r * _softplus(-lam[...])
        a, _, nem = _decay(log_a)
        a_s[...] = a
        b_s[...] = jnp.sqrt(nem) * (i * xb)
        row = lax.broadcasted_iota(jnp.int32, (8, dr), 0)

        def step(t, c):
            r0 = pl.multiple_of(t * 8, 8)
            a = a_s[pl.ds(r0, 8), :]
            b = b_s[pl.ds(r0, 8), :]
            for s in (1, 2, 4):
                m = row >= s
                a_sh = jnp.where(m, pltpu.roll(a, s, 0), 1.0)
                b_sh = jnp.where(m, pltpu.roll(b, s, 0), 0.0)
                b = a * b_sh + b
                a = a * a_sh
            hh = b + a * c
            hs_ref[pl.ds(r0, 8), :] = hh
            return jnp.broadcast_to(hh[7:8, :], hh.shape)

        carry[...] = lax.fori_loop(0, nt, step, carry[...])
        y = (hs_ref[...] * (gate * _sigmoid(gate))).astype(BF16)
        y_ref[...] = y
        x1_ref[...] = x_ref[...] + _dot(y, wout[...])

        @pl.when(step_no == nsteps - 1)
        def _():
            gather.finish()

    tok = lambda c: pl.BlockSpec((tb, c), lambda b, j: (b * nblk + j, 0))
    consts = [w["conv_w"], w["conv_b"], w["w_rg"], w["b_rg"], w["w_ig"], w["b_ig"], w["lru_lambda"]]
    return pl.pallas_call(
        body, name="fa_fwd", grid=(t_all // seq, nblk),
        in_specs=[tok(d), tok(2 * dr), ANY, ANY] + [_full(c.shape) for c in consts],
        out_specs=[tok(d), tok(dr), tok(dr), tok(dr), ANY],
        out_shape=[_sds((t_all, d)), _sds((t_all, dr)), _sds((t_all, dr), BF16), _sds((t_all, dr)), _sds((8,) + wb_half.shape, BF16)],
        scratch_shapes=[pltpu.VMEM((dr, d), BF16), pltpu.VMEM((tb + 8, dr), F32), pltpu.VMEM((tb, dr), F32),
                        pltpu.VMEM((tb, dr), F32), pltpu.VMEM((8, dr), F32), pltpu.SemaphoreType.DMA((N_CHIPS,))] + GATHER_SEMS,
        compiler_params=_params(2),
    )(x, u, wg, wb_half, *consts)


def _fb_fwd(x1, wg, w, lay, cos_t, sin_t, seq, tb):
    t_all, d = x1.shape
    kvr, qr, hv = lay.kvr, lay.qr, lay.hv
    nheads = hv // LANE
    npos = seq // tb

    def body(x_ref, wg_ref, nkv, nb, wdkv, kvn, qn, cos_ref, sin_ref,
             qn_o, qr_o, kn_o, kr_o, v_o, ub_o, ckr_o, hb_o, hk_o, cq_o, ckv_o, winb, wuk, wuv, wuqn, wuqr, sems):
        @pl.when(pl.program_id(0) == 0)
        def _():
            cps = []
            for n, (key, dst) in enumerate((("in_b", winb), ("uk", wuk), ("uv", wuv), ("uq_n", wuqn), ("uq_r", wuqr))):
                cps += _fetch(wg_ref, lay, key, dst, sems, n * N_CHIPS)
            for cp in cps:
                cp.start()
            for cp in cps:
                cp.wait()

        xv = x_ref[...]
        xh = xv * _rinv(xv)
        hk = (xh * nkv[...]).astype(BF16)
        hb = (xh * nb[...]).astype(BF16)
        hk_o[...] = hk
        hb_o[...] = hb
        cos, sin = cos_ref[...], sin_ref[...]
        ckr = _dot(hk, wdkv[...])
        ckr_o[...] = ckr
        ckv_pre = ckr[:, :kvr]
        ckv = (ckv_pre * _rinv(ckv_pre) * kvn[...]).astype(BF16)
        ckv_o[...] = ckv
        kr = ckr[:, kvr:]
        kr_o[...] = (kr * cos + _swap_halves(kr) * sin).astype(BF16)
        kn_o[...] = _dot(ckv, wuk[...]).astype(BF16)
        v_o[...] = _dot(ckv, wuv[...]).astype(BF16)
        ub = _dot_nt(hb, winb[...])
        ub_o[...] = ub
        cq_pre = ub[:, :qr]
        cq = (cq_pre * _rinv(cq_pre) * qn[...]).astype(BF16)
        cq_o[...] = cq
        qn_o[...] = (_dot(cq, wuqn[...]) * Q_SCALE).astype(BF16)
        qrope = _dot(cq, wuqr[...]) * Q_SCALE
        qr_o[...] = (qrope * jnp.tile(cos, (1, nheads)) + _swap_halves(qrope) * jnp.tile(sin, (1, nheads))).astype(BF16)

    tok = lambda c: pl.BlockSpec((tb, c), lambda i: (i, 0))
    pos = pl.BlockSpec((tb, LANE), lambda i: (i % npos, 0))
    consts = [w["norm_kv"], w["norm_b"], w["w_dkv_p"], w["kv_norm"], w["q_norm"]]
    outs = [(hv, BF16), (hv, BF16), (hv, BF16), (LANE, BF16), (hv, BF16), (qr + hv, F32), (kvr + LANE, F32), (d, BF16), (d, BF16), (qr, BF16), (kvr, BF16)]
    return pl.pallas_call(
        body, name="fb_fwd", grid=(t_all // tb,),
        in_specs=[tok(d), ANY] + [_full(c.shape) for c in consts] + [pos, pos],
        out_specs=[tok(c) for c, _ in outs],
        out_shape=[_sds((t_all, c), dt) for c, dt in outs],
        scratch_shapes=[pltpu.VMEM((qr + hv, d), BF16), pltpu.VMEM((kvr, d), BF16), pltpu.VMEM((kvr, d), BF16), pltpu.VMEM((qr, d), BF16),
                        pltpu.VMEM((qr, d), BF16), pltpu.SemaphoreType.DMA((5 * N_CHIPS,))],
        compiler_params=_params(1),
    )(x1, wg, *consts, cos_t, sin_t)


def _causal_mask(row0, col0, nrows, ncols):
    rows = row0 + lax.broadcasted_iota(jnp.int32, (nrows, ncols), 0)
    cols = col0 + lax.broadcasted_iota(jnp.int32, (nrows, ncols), 1)
    return cols <= rows


def _attn_fwd(qn, qr, kn, kr, v, seq, ta):
    t_all, hv = qn.shape
    nheads, nb, na = hv // LANE, t_all // seq, seq // ta

    reps = ta // LANE
    hp = ATTN_HEADS
    wide = hp * LANE

    def body(qn_ref, qr_ref, kn_ref, kr_ref, v_ref, o_ref, lse_ref, m_s, l_s, acc_s):
        i = pl.program_id(2)
        m_s[...] = jnp.full((ta, wide), -1e30, F32)
        l_s[...] = jnp.zeros((ta, wide), F32)
        acc_s[...] = jnp.zeros((ta, wide), F32)
        heads = [slice(n * LANE, (n + 1) * LANE) for n in range(hp)]
        qs = [jnp.concatenate([qn_ref[:, hd], qr_ref[:, hd]], axis=1) for hd in heads]

        def tile(j, diagonal):
            cols = pl.ds(pl.multiple_of(j * ta, ta), ta)
            k_rope = kr_ref[cols, :]
            for q, hd in zip(qs, heads):
                k = jnp.concatenate([kn_ref[cols, hd], k_rope], axis=1)
                s = _dot_nt(q, k)
                if diagonal:
                    s = jnp.where(_causal_mask(0, 0, ta, ta), s, -1e30)
                m_prev = m_s[:, hd]
                m_new = jnp.maximum(m_prev, jnp.max(s, axis=1, keepdims=True))
                p = jnp.exp2(s - jnp.tile(m_new, (1, reps)))
                alpha = jnp.exp2(m_prev - m_new)
                l_s[:, hd] = alpha * l_s[:, hd] + jnp.sum(p, axis=1, keepdims=True)
                acc_s[:, hd] = alpha * acc_s[:, hd] + _dot(p.astype(BF16), v_ref[cols, hd])
                m_s[:, hd] = m_new

        def off_diagonal(j, carry):
            tile(j, False)
            return carry

        lax.fori_loop(0, i, off_diagonal, 0)
        tile(i, True)
        o_ref[...] = (acc_s[...] / l_s[...]).astype(BF16)
        lse_ref[...] = m_s[...] + jnp.log2(l_s[...])

    qspec = pl.BlockSpec((ta, wide), lambda b, h, i: (b * na + i, h))
    kspec = pl.BlockSpec((seq, wide), lambda b, h, i: (b, h))
    krspec = pl.BlockSpec((seq, LANE), lambda b, h, i: (b, 0))
    return pl.pallas_call(
        body, name="attn_fwd", grid=(nb, nheads // hp, na),
        in_specs=[qspec, qspec, kspec, krspec, kspec],
        out_specs=[qspec, qspec],
        out_shape=[_sds((t_all, hv), BF16), _sds((t_all, hv))],
        scratch_shapes=[pltpu.VMEM((ta, wide), F32)] * 3,
        compiler_params=_params(3),
    )(qn, qr, kn, kr, v)


def _attn_bwd(qn, qr, kn, kr, v, do, lse, delta, seq, ta):
    t_all, hv = qn.shape
    nheads, nb, na = hv // LANE, t_all // seq, seq // ta

    reps = ta // LANE
    nchunks = ta // ATTN_ROWS

    hp = ATTN_HEADS_BWD
    wide = hp * LANE
    heads = [slice(n * LANE, (n + 1) * LANE) for n in range(hp)]

    def body(qn_ref, qr_ref, kn_ref, kr_ref, v_ref, do_ref, lse_ref, dl_ref, dqn_out, dqr_out, dkn_ref, dkr_ref, dv_ref,
             s_s, dp_s, p_s, ds_s, dk_s, dv_s, dqn_ref, dqr_ref):
        j = pl.program_id(2)

        @pl.when(j == 0)
        def _():
            dqn_ref[...] = jnp.zeros((seq, wide), F32)
            dqr_ref[...] = jnp.zeros((seq, wide), F32)

        dk_s[...] = jnp.zeros((hp, ta, 2 * LANE), F32)
        dv_s[...] = jnp.zeros((hp, ta, LANE), F32)
        k_rope = kr_ref[...]
        ks = [jnp.concatenate([kn_ref[:, hd], k_rope], axis=1) for hd in heads]

        def tile(i, diagonal):
            rows_i = pl.ds(pl.multiple_of(i * ta, ta), ta)
            for n, hd in enumerate(heads):
                q = jnp.concatenate([qn_ref[rows_i, hd], qr_ref[rows_i, hd]], axis=1)
                do_b = do_ref[rows_i, hd]
                s_s[n] = _dot_nt(q, ks[n])
                dp_s[n] = _dot_nt(do_b, v_ref[:, hd])
                for c in range(nchunks):
                    rows = pl.ds(c * ATTN_ROWS, ATTN_ROWS)
                    seq_rows = pl.ds(pl.multiple_of(i * ta + c * ATTN_ROWS, ATTN_ROWS), ATTN_ROWS)
                    s = s_s[n, rows, :]
                    if diagonal:
                        s = jnp.where(_causal_mask(c * ATTN_ROWS, 0, ATTN_ROWS, ta), s, -1e30)
                    p = jnp.exp2(s - jnp.tile(lse_ref[seq_rows, hd], (1, reps)))
                    p_s[n, rows, :] = p.astype(BF16)
                    ds_s[n, rows, :] = (p * (dp_s[n, rows, :] - jnp.tile(dl_ref[seq_rows, hd], (1, reps)))).astype(BF16)
                dv_s[n] += _dot_tn(p_s[n], do_b)
                ds = ds_s[n]
                dk_s[n] += _dot_tn(ds, q)
                dq = _dot(ds, ks[n])
                dqn_ref[rows_i, hd] += dq[:, :LANE]
                dqr_ref[rows_i, hd] += dq[:, LANE:]

        def off_diagonal(i, carry):
            tile(i, False)
            return carry

        tile(j, True)
        lax.fori_loop(j + 1, na, off_diagonal, 0)
        for n, hd in enumerate(heads):
            dkn_ref[:, hd] = (dk_s[n, :, :LANE] * LN2).astype(BF16)
            dkr_ref[:, hd] = (dk_s[n, :, LANE:] * LN2).astype(BF16)
            dv_ref[:, hd] = dv_s[n].astype(BF16)

        @pl.when(j == na - 1)
        def _():
            dqn_out[...] = dqn_ref[...].astype(BF16)
            dqr_out[...] = dqr_ref[...].astype(BF16)

    qspec = pl.BlockSpec((seq, wide), lambda b, h, j: (b, h))
    kspec = pl.BlockSpec((ta, wide), lambda b, h, j: (b * na + j, h))
    krspec = pl.BlockSpec((ta, LANE), lambda b, h, j: (b * na + j, 0))
    return pl.pallas_call(
        body, name="attn_bwd", grid=(nb, nheads // hp, na),
        in_specs=[qspec, qspec, kspec, krspec, kspec, qspec, qspec, qspec],
        out_specs=[qspec, qspec, kspec, kspec, kspec],
        out_shape=[_sds((t_all, hv), BF16)] * 5,
        scratch_shapes=[pltpu.VMEM((hp, ta, ta), F32), pltpu.VMEM((hp, ta, ta), F32), pltpu.VMEM((hp, ta, ta), BF16), pltpu.VMEM((hp, ta, ta), BF16),
                        pltpu.VMEM((hp, ta, 2 * LANE), F32), pltpu.VMEM((hp, ta, LANE), F32), pltpu.VMEM((seq, wide), F32), pltpu.VMEM((seq, wide), F32)],
        compiler_params=_params(3),
    )(qn, qr, kn, kr, v, do, lse, delta)


def _head(o, ub, x1, target, wg, w, lay, tb):
    t_all, d = x1.shape
    hv, qr = lay.hv, lay.qr
    nheads = hv // LANE

    def body(o_ref, ub_ref, x1_ref, tg_ref, wg_ref, gf, loss_ref, dgf_ref, yb_ref, dx2_ref, do_ref, dg_ref, dl_ref, wob, sems):
        @pl.when(pl.program_id(0) == 0)
        def _():
            cps = _fetch(wg_ref, lay, "out_b", wob, sems, 0)
            for cp in cps:
                cp.start()
            loss_ref[...] = jnp.zeros((1, LANE), F32)
            dgf_ref[...] = jnp.zeros((1, d), F32)
            for cp in cps:
                cp.wait()

        ov = o_ref[...].astype(F32)
        g = ub_ref[:, qr:]
        sg = _sigmoid(g)
        silu = g * sg
        yb = (ov * silu).astype(BF16)
        yb_ref[...] = yb
        x2 = x1_ref[...] + _dot(yb, wob[...])
        rinv = _rinv(x2)
        err = x2 * rinv * gf[...] - tg_ref[...]
        loss_ref[...] += (0.5 / d) * jnp.sum(jnp.sum(err * err, axis=1, keepdims=True), axis=0, keepdims=True)
        dx2, dgf = _rms_bwd(x2, rinv, gf[...], err * (1.0 / d))
        dgf_ref[...] += dgf
        dx2_ref[...] = dx2
        dyb = _dot_nt(dx2.astype(BF16), wob[...])
        dov = dyb * silu
        do_ref[...] = dov.astype(BF16)
        dg_ref[...] = (dyb * ov * (sg * (1.0 + g * (1.0 - sg)))).astype(BF16)
        prod = dov * ov
        dl_ref[...] = jnp.concatenate(
            [jnp.broadcast_to(jnp.sum(prod[:, n * LANE:(n + 1) * LANE], axis=1, keepdims=True), (tb, LANE)) for n in range(nheads)], axis=1)

    tok = lambda c: pl.BlockSpec((tb, c), lambda i: (i, 0))
    return pl.pallas_call(
        body, name="head", grid=(t_all // tb,),
        in_specs=[tok(hv), tok(qr + hv), tok(d), tok(d), ANY, _full((1, d))],
        out_specs=[_full((1, LANE)), _full((1, d)), tok(hv), tok(d), tok(hv), tok(hv), tok(hv)],
        out_shape=[_sds((1, LANE)), _sds((1, d)), _sds((t_all, hv), BF16), _sds((t_all, d)), _sds((t_all, hv), BF16), _sds((t_all, hv), BF16),
                   _sds((t_all, hv))],
        scratch_shapes=[pltpu.VMEM((hv, d), BF16), pltpu.SemaphoreType.DMA((N_CHIPS,))],
        compiler_params=_params(1),
    )(o, ub, x1, target, wg, w["final_norm"])


def _fb_bwd(dqn, dqr, dkn, dkr, dv, dgate, ub, ckr, x1, dx2, wg, w, lay, cos_t, sin_t, seq, tb):
    t_all, d = x1.shape
    hv, qr, kvr = lay.hv, lay.qr, lay.kvr
    nheads = hv // LANE
    npos = seq // tb

    def body(dqn_ref, dqr_ref, dkn_ref, dkr_ref, dv_ref, dg_ref, ub_ref, ckr_ref, x1_ref, dx2_ref, wg_ref,
             qn, nb, kvn, wdkv, nkv, cos_ref, sin_ref,
             dx1_ref, dqrp_ref, dqnp_ref, dub_ref, dckr_ref, dqn_g, dnb_g, dkvn_g, dnkv_g, winb, wuk, wuv, wuqn, wuqr, sems):
        @pl.when(pl.program_id(0) == 0)
        def _():
            cps = []
            for n, (key, dst) in enumerate((("in_b", winb), ("uk", wuk), ("uv", wuv), ("uq_n", wuqn), ("uq_r", wuqr))):
                cps += _fetch(wg_ref, lay, key, dst, sems, n * N_CHIPS)
            for cp in cps:
                cp.start()
            dqn_g[...] = jnp.zeros((1, qr), F32)
            dnb_g[...] = jnp.zeros((1, d), F32)
            dkvn_g[...] = jnp.zeros((1, kvr), F32)
            dnkv_g[...] = jnp.zeros((1, d), F32)
            for cp in cps:
                cp.wait()

        cos, sin = cos_ref[...], sin_ref[...]
        xv = x1_ref[...]
        rinv1 = _rinv(xv)
        dqr_v = dqr_ref[...].astype(F32) * ATTN_SCALE
        dqr_pre = (dqr_v * jnp.tile(cos, (1, nheads)) + _swap_halves(dqr_v * jnp.tile(sin, (1, nheads)))).astype(BF16)
        dqrp_ref[...] = dqr_pre
        dqn_pre = (dqn_ref[...].astype(F32) * ATTN_SCALE).astype(BF16)
        dqnp_ref[...] = dqn_pre
        dcq = _dot_nt(dqn_pre, wuqn[...]) + _dot_nt(dqr_pre, wuqr[...])
        cq_pre = ub_ref[:, :qr]
        dcq_pre, g1 = _rms_bwd(cq_pre, _rinv(cq_pre), qn[...], dcq)
        dqn_g[...] += g1
        dub = jnp.concatenate([dcq_pre.astype(BF16), dg_ref[...]], axis=1)
        dub_ref[...] = dub
        dx1_b, g2 = _rms_bwd(xv, rinv1, nb[...], _dot(dub, winb[...]))
        dnb_g[...] += g2
        dkr_all = dkr_ref[...].astype(F32)
        dkr_sum = dkr_all[:, :LANE]
        for n in range(1, nheads):
            dkr_sum = dkr_sum + dkr_all[:, n * LANE:(n + 1) * LANE]
        dckr_rope = dkr_sum * cos + _swap_halves(dkr_sum * sin)
        dckv = _dot_nt(dkn_ref[...].astype(BF16), wuk[...]) + _dot_nt(dv_ref[...].astype(BF16), wuv[...])
        ckv_pre = ckr_ref[:, :kvr]
        dckv_pre, g3 = _rms_bwd(ckv_pre, _rinv(ckv_pre), kvn[...], dckv)
        dkvn_g[...] += g3
        dckr = jnp.concatenate([dckv_pre, dckr_rope], axis=1).astype(BF16)
        dckr_ref[...] = dckr
        dx1_kv, g4 = _rms_bwd(xv, rinv1, nkv[...], _dot_nt(dckr, wdkv[...]))
        dnkv_g[...] += g4
        dx1_ref[...] = dx2_ref[...] + dx1_b + dx1_kv

    tok = lambda c: pl.BlockSpec((tb, c), lambda i: (i, 0))
    pos = pl.BlockSpec((tb, LANE), lambda i: (i % npos, 0))
    consts = [w["q_norm"], w["norm_b"], w["kv_norm"], w["w_dkv_p"], w["norm_kv"]]
    return pl.pallas_call(
        body, name="fb_bwd", grid=(t_all // tb,),
        in_specs=[tok(hv)] * 6 + [tok(qr + hv), tok(kvr + LANE), tok(d), tok(d), ANY] + [_full(c.shape) for c in consts] + [pos, pos],
        out_specs=[tok(d), tok(hv), tok(hv), tok(qr + hv), tok(kvr + LANE), _full((1, qr)), _full((1, d)), _full((1, kvr)), _full((1, d))],
        out_shape=[_sds((t_all, d)), _sds((t_all, hv), BF16), _sds((t_all, hv), BF16), _sds((t_all, qr + hv), BF16), _sds((t_all, kvr + LANE), BF16),
                   _sds((1, qr)), _sds((1, d)), _sds((1, kvr)), _sds((1, d))],
        scratch_shapes=[pltpu.VMEM((qr + hv, d), BF16), pltpu.VMEM((kvr, d), BF16), pltpu.VMEM((kvr, d), BF16), pltpu.VMEM((qr, d), BF16),
                        pltpu.VMEM((qr, d), BF16), pltpu.SemaphoreType.DMA((5 * N_CHIPS,))],
        compiler_params=_params(1),
    )(dqn, dqr, dkn, dkr, dv, dgate, ub, ckr, x1, dx2, wg, *consts, cos_t, sin_t)


def _fa_bwd(dx1, x, u, xb, hs, wg, g16, g32, w, lay, seq, tb):
    t_all, d = x.shape
    dr = lay.dr
    nblocks = w["w_rg"].shape[0]
    nblk = seq // tb
    nt = tb // SUBLANE
    per8 = tb // 8

    def body(dx1_ref, x_ref, u_ref, xb_ref, hs_ref, hh_ref, wg_ref, g16_ref, g32_ref, na, cw, wrg, brg, wig, big, lam,
             gx_ref, du_ref, h_ref, dna_g, dcw_g, dcb_g, dbrg_g, dbig_g, dlam_g, dwrg_g, dwig_g, got_ref, sib_ref, own_ref,
             wint, wout, hpad, a_s, d_s, g_s, dxpad, carry, sems, send_sems, recv_sems, local_sem):
        b, jj = pl.program_id(0), pl.program_id(1)
        first_block = jj == nblk - 1
        scatter = _ScatterDirect(g16_ref, g32_ref, got_ref, sib_ref, own_ref, send_sems, recv_sems, local_sem, lay, G_GROUPS["early"])

        @pl.when((b == 0) & (jj == 0))
        def _():
            scatter.start()
            cps = _fetch(wg_ref, lay, "in_a", wint, sems, 0) + _fetch(wg_ref, lay, "out_a", wout, sems, N_CHIPS)
            for cp in cps:
                cp.start()
            dna_g[...] = jnp.zeros((1, d), F32)
            dcw_g[...] = jnp.zeros((4, dr), F32)
            dcb_g[...] = jnp.zeros((1, dr), F32)
            dbrg_g[...] = jnp.zeros((1, dr), F32)
            dbig_g[...] = jnp.zeros((1, dr), F32)
            dlam_g[...] = jnp.zeros((1, dr), F32)
            dwrg_g[...] = jnp.zeros((nblocks, LANE, LANE), F32)
            dwig_g[...] = jnp.zeros((nblocks, LANE, LANE), F32)
            for cp in cps:
                cp.wait()

        @pl.when(jj == 0)
        def _():
            dxpad[pl.ds(tb, 8), :] = jnp.zeros((8, dr), F32)
            carry[...] = jnp.zeros((8, dr), F32)

        keep = jnp.where(first_block, 0.0, 1.0)
        dx1v = dx1_ref[...]
        gate = u_ref[:, dr:]
        xpre = u_ref[:, :dr]
        hpad[pl.ds(0, 8), :] = hh_ref[...] * keep
        hpad[pl.ds(8, tb), :] = hs_ref[...]
        xb = xb_ref[...]
        xbb = xb.astype(BF16)
        r, i = _gates(xb, wrg, brg[...], wig, big[...], nblocks)
        sp = _softplus(-lam[...])
        log_a = -LRU_C * r * sp
        a, a2, nem = _decay(log_a)
        mult = jnp.sqrt(nem)
        sg = _sigmoid(gate)
        dy = _dot_nt(dx1v.astype(BF16), wout[...])
        hsv = hs_ref[...]
        dgate = dy * hsv * (sg * (1.0 + gate * (1.0 - sg)))
        a_s[...] = a
        d_s[...] = dy * (gate * sg)
        row = lax.broadcasted_iota(jnp.int32, (8, dr), 0)

        def step(k, c):
            r0 = pl.multiple_of((nt - 1 - k) * 8, 8)
            av = a_s[pl.ds(r0, 8), :]
            dv = d_s[pl.ds(r0, 8), :]
            qv = av * dv
            for s in (1, 2, 4):
                m = row < 8 - s
                a_sh = jnp.where(m, pltpu.roll(av, 8 - s, 0), 1.0)
                q_sh = jnp.where(m, pltpu.roll(qv, 8 - s, 0), 0.0)
                qv = qv + av * q_sh
                av = av * a_sh
            qv = qv + av * c
            g_s[pl.ds(r0, 8), :] = dv + jnp.where(row < 7, pltpu.roll(qv, 7, 0), c)
            return jnp.broadcast_to(qv[0:1, :], qv.shape)

        carry[...] = lax.fori_loop(0, nt, step, carry[...])
        g = g_s[...]
        ix = i * xb
        dlog_a = g * (hpad[pl.ds(7, tb), :] * a - ix * (a2 * lax.rsqrt(nem)))
        dix = g * mult
        dlam_g[...] += -jax.nn.sigmoid(-lam[...]) * jnp.sum(dlog_a * (-LRU_C * r), axis=0, keepdims=True)
        drg = dlog_a * (-LRU_C * sp) * r * (1.0 - r)
        dig = dix * xb * i * (1.0 - i)
        dbrg_g[...] += jnp.sum(drg, axis=0, keepdims=True)
        dbig_g[...] += jnp.sum(dig, axis=0, keepdims=True)
        drgb, digb = drg.astype(BF16), dig.astype(BF16)
        back = []
        for n in range(nblocks):
            cols = slice(n * LANE, (n + 1) * LANE)
            dwrg_g[n] += _dot_tn(xbb[:, cols], drgb[:, cols])
            dwig_g[n] += _dot_tn(xbb[:, cols], digb[:, cols])
            back.append(_dot_nt(drgb[:, cols], wrg[n]) + _dot_nt(digb[:, cols], wig[n]))
        dxb = dix * i + jnp.concatenate(back, axis=1)
        dcb_g[...] += jnp.sum(dxb, axis=0, keepdims=True)
        dxpad[pl.ds(0, tb), :] = dxb
        later = [dxb, dxpad[pl.ds(1, tb), :], dxpad[pl.ds(2, tb), :], dxpad[pl.ds(3, tb), :]]
        dxpad[pl.ds(tb, 8), :] = dxb[:8, :]
        dxpre = cw[3:4, :] * later[0] + cw[2:3, :] * later[1] + cw[1:2, :] * later[2] + cw[0:1, :] * later[3]
        for m in range(4):
            dcw_g[3 - m:4 - m, :] += jnp.sum(later[m] * xpre, axis=0, keepdims=True)
        du = jnp.concatenate([dxpre, dgate], axis=1).astype(BF16)
        du_ref[...] = du
        xv = x_ref[...]
        rinv = _rinv(xv)
        h_ref[...] = (xv * rinv * na[...]).astype(BF16)
        dxa, g1 = _rms_bwd(xv, rinv, na[...], _dot(du, wint[...]))
        dna_g[...] += g1
        gx_ref[...] = dx1v + dxa

        @pl.when((b == t_all // seq - 1) & (jj == nblk - 1))
        def _():
            scatter.finish()

    blk = lambda b, j: b * nblk + (nblk - 1 - j)
    tok = lambda c: pl.BlockSpec((tb, c), lambda b, j: (blk(b, j), 0))
    halo = pl.BlockSpec((8, dr), lambda b, j: (jnp.maximum(blk(b, j) * per8 - 1, 0), 0))
    consts = [w["norm_a"], w["conv_w"], w["w_rg"], w["b_rg"], w["w_ig"], w["b_ig"], w["lru_lambda"]]
    vec = lambda c: _full((1, c))
    blocks3 = (nblocks, LANE, LANE)
    return pl.pallas_call(
        body, name="fa_bwd", grid=(t_all // seq, nblk),
        in_specs=[tok(d), tok(d), tok(2 * dr), tok(dr), tok(dr), halo, ANY, ANY, ANY] + [_full(c.shape) for c in consts],
        out_specs=[tok(d), tok(2 * dr), tok(d), vec(d), _full((4, dr)), vec(dr), vec(dr), vec(dr), vec(dr), _full(blocks3), _full(blocks3), ANY, ANY, ANY],
        out_shape=[_sds((t_all, d)), _sds((t_all, 2 * dr), BF16), _sds((t_all, d), BF16), _sds((1, d)), _sds((4, dr)), _sds((1, dr)), _sds((1, dr)), _sds((1, dr)),
                   _sds((1, dr)), _sds(blocks3), _sds(blocks3)] + _scatter_direct_shapes(lay, "early"),
        scratch_shapes=[pltpu.VMEM((2 * dr, d), BF16), pltpu.VMEM((dr, d), BF16), pltpu.VMEM((tb + 8, dr), F32),
                        pltpu.VMEM((tb, dr), F32), pltpu.VMEM((tb, dr), F32), pltpu.VMEM((tb, dr), F32), pltpu.VMEM((tb + 8, dr), F32),
                        pltpu.VMEM((8, dr), F32), pltpu.SemaphoreType.DMA((2 * N_CHIPS,))] + SCATTER_DIRECT_SEMS,
        compiler_params=_params(2),
    )(dx1, x, u, xb, hs, hs, wg, g16, g32, *consts)


def _mm_into(gbufs, a, b, off, name, bt):
    t_all, m = a.shape
    n = b.shape[1]
    nsplit = 2 if m >= 1024 and (m // 2) % LANE == 0 else 1
    mh = m // nsplit
    nt = t_all // bt
    nbuf = len(gbufs)
    twin = nbuf == 2

    def body(a_ref, b_ref, *refs):
        outs, acc, sems = refs[nbuf:2 * nbuf], refs[2 * nbuf], refs[-1]
        acc16 = refs[2 * nbuf + 1] if twin else None
        part, t = pl.program_id(0), pl.program_id(1)

        def out_copies(h):
            dst = pl.ds(off + h * mh, mh)
            copies = [pltpu.make_async_copy(acc.at[h], outs[0].at[dst, :], sems.at[0, h])]
            if twin:
                copies.append(pltpu.make_async_copy(acc16.at[h], outs[1].at[dst, :], sems.at[1, h]))
            return copies

        prod = _dot_tn(a_ref[...].astype(BF16), b_ref[...].astype(BF16))
        for h in range(nsplit):
            @pl.when((part == h) & (t == 0))
            def _():
                acc[h] = prod

            @pl.when((part == h) & (t > 0))
            def _():
                acc[h] += prod

            @pl.when((part == h) & (t == nt - 1))
            def _():
                if twin:
                    acc16[h] = acc[h].astype(BF16)
                for cp in out_copies(h):
                    cp.start()

        @pl.when((part == nsplit - 1) & (t == nt - 1))
        def _():
            for h in range(nsplit):
                for cp in out_copies(h):
                    cp.wait()

    scratch = [pltpu.VMEM((nsplit, mh, n), F32)] + ([pltpu.VMEM((nsplit, mh, n), BF16)] if twin else []) + [pltpu.SemaphoreType.DMA((2, nsplit))]
    return pl.pallas_call(
        body, name=name, grid=(nsplit, nt),
        in_specs=[pl.BlockSpec((bt, mh), lambda h, t: (t, h)), pl.BlockSpec((bt, n), lambda h, t: (t, 0))] + [ANY] * nbuf,
        out_specs=[ANY] * nbuf, out_shape=[_sds(g.shape, g.dtype) for g in gbufs], input_output_aliases={2 + k: k for k in range(nbuf)},
        scratch_shapes=scratch, compiler_params=_params(2),
    )(a, b, *gbufs)


def _dw_in_a_exchange(du, h, rest, lay, bt):
    t_all, d = h.shape
    half = d // 2
    rows, rest_rows, c_rows = lay.rows["in_a"], lay.rows["rest"], lay.c_rows["late"]
    c_in, c_rest = lay.c_off["in_a"], lay.c_off["rest"]
    nt = t_all // bt
    xi, yi, _ = _place()
    order = jnp.stack([2 * (1 - xi) + yi, 2 * xi + (1 - yi), 2 * (1 - xi) + (1 - yi), 2 * xi + yi]).astype(jnp.int32)

    def body(order_ref, a_ref, b_ref, rest_ref, got_ref, own_ref, acc, sibbuf, part16, restv, rest_sib, rest_p, rest16, own_v, own_r,
             d2d_send, d2d_recv, ici_send, ici_recv, local_sems):
        x, y, c = _place()
        chips = [(1 - x, y), (x, 1 - y), (1 - x, 1 - y)]
        g, t = pl.program_id(0), pl.program_id(1)
        their_cols = pl.ds(pl.multiple_of((1 - c) * half, LANE), half)

        def my_half(v):
            return jnp.where(c == 0, v[:, :half], v[:, half:])

        def d2d(src, dst, k):
            return pltpu.make_async_remote_copy(src_ref=src, dst_ref=dst, send_sem=d2d_send.at[k], recv_sem=d2d_recv.at[k],
                                                device_id=(x, y, 1 - c), device_id_type=MESH)

        def group_swap(gg):
            return d2d(acc.at[gg % 2, :, their_cols], sibbuf.at[gg], gg)

        def rest_swap():
            return d2d(restv.at[:, their_cols], rest_sib, 4)

        def to_chip(k, src, off, nrows):
            px, py = chips[k]
            return pltpu.make_async_remote_copy(src_ref=src, dst_ref=got_ref.at[k, pl.ds(off, nrows), :], send_sem=ici_send.at[k],
                                                recv_sem=ici_recv.at[k], device_id=(px, py, c), device_id_type=MESH)

        def own_copy(src, off, nrows, k):
            return pltpu.make_async_copy(src, own_ref.at[pl.ds(off, nrows), :], local_sems.at[k])

        def finish_group(gg):
            group_swap(gg).wait()
            part = my_half(acc[gg % 2]) + sibbuf[gg]
            if gg < 3:
                part16[gg] = part.astype(BF16)
                to_chip(gg, part16.at[gg], c_in, rows).start()
            else:
                own_v[...] = part
                own_copy(own_v, c_in, rows, 1).start()

        @pl.when((g == 0) & (t == 0))
        def _():
            load = pltpu.make_async_copy(rest_ref, restv, local_sems.at[0])
            load.start()
            load.wait()
            rest_swap().start()

        prod = _dot_tn(a_ref[...], b_ref[...])
        for gg in range(4):
            @pl.when((g == gg) & (t == 0))
            def _():
                acc[gg % 2] = prod

            @pl.when((g == gg) & (t > 0))
            def _():
                acc[gg % 2] += prod

            @pl.when((g == gg) & (t == nt - 1))
            def _():
                group_swap(gg).start()
                if gg == 0:
                    rest_swap().wait()
                    rest_p[...] = my_half(restv[...]) + rest_sib[...]
                    for k in range(3):
                        chip_rows = pl.ds(pl.multiple_of(order_ref[k] * rest_rows, SUBLANE), rest_rows)
                        rest16[k] = rest_p[chip_rows, :].astype(BF16)
                        to_chip(k, rest16.at[k], c_rest, rest_rows).start()
                    own_r[...] = rest_p[pl.ds(pl.multiple_of(order_ref[3] * rest_rows, SUBLANE), rest_rows), :]
                    own_copy(own_r, c_rest, rest_rows, 2).start()
                else:
                    finish_group(gg - 1)
                if gg == 3:
                    finish_group(3)
                    for k, (px, py) in enumerate(chips):
                        pltpu.make_async_remote_copy(src_ref=got_ref.at[k], dst_ref=got_ref.at[k], send_sem=ici_send.at[k], recv_sem=ici_recv.at[k],
                                                     device_id=(px, py, c), device_id_type=MESH).wait()
                    own_copy(own_v, c_in, rows, 1).wait()
                    own_copy(own_r, c_rest, rest_rows, 2).wait()

    return pl.pallas_call(
        body, name="dw_in_a",
        grid_spec=pltpu.PrefetchScalarGridSpec(
            num_scalar_prefetch=1, grid=(N_CHIPS, nt),
            in_specs=[pl.BlockSpec((bt, rows), lambda g, t, order: (t, order[g])), pl.BlockSpec((bt, d), lambda g, t, order: (t, 0)), ANY],
            out_specs=[ANY, ANY],
            scratch_shapes=[pltpu.VMEM((2, rows, d), F32), pltpu.VMEM((N_CHIPS, rows, half), F32), pltpu.VMEM((3, rows, half), BF16),
                            pltpu.VMEM((N_CHIPS * rest_rows, d), F32), pltpu.VMEM((N_CHIPS * rest_rows, half), F32),
                            pltpu.VMEM((N_CHIPS * rest_rows, half), F32), pltpu.VMEM((3, rest_rows, half), BF16),
                            pltpu.VMEM((rows, half), F32), pltpu.VMEM((rest_rows, half), F32),
                            pltpu.SemaphoreType.DMA((5,)), pltpu.SemaphoreType.DMA((5,)), pltpu.SemaphoreType.DMA((3,)), pltpu.SemaphoreType.DMA((3,)),
                            pltpu.SemaphoreType.DMA((3,))]),
        out_shape=_scatter_shapes(lay, "late", half), compiler_params=_params(2),
    )(order, du, h, rest)


def _mm_tn(a, b, name, bt):
    t_all, m = a.shape
    n = b.shape[1]

    def body(a_ref, b_ref, o_ref):
        @pl.when(pl.program_id(0) == 0)
        def _():
            o_ref[...] = jnp.zeros((m, n), F32)

        o_ref[...] += _dot_tn(a_ref[...].astype(BF16), b_ref[...].astype(BF16))

    return pl.pallas_call(
        body, name=name, grid=(t_all // bt,),
        in_specs=[pl.BlockSpec((bt, m), lambda t: (t, 0)), pl.BlockSpec((bt, n), lambda t: (t, 0))],
        out_specs=_full((m, n)), out_shape=_sds((m, n)),
        compiler_params=_params(1),
    )(a, b)


class _Gather8:
    def __init__(self, x_ref, out_ref, send_sems, recv_sems, local_sem):
        x, y, c = _place()
        self.c, self.me, self.sibling = c, (x, y, c), (x, y, 1 - c)
        self.chips = [(1 - x, y), (x, 1 - y), (1 - x, 1 - y)]
        self.x_ref, self.out_ref, self.send_sems, self.recv_sems, self.local_sem = x_ref, out_ref, send_sems, recv_sems, local_sem

    def _slot(self, px, py, pc):
        return self.out_ref.at[4 * px + 2 * py + pc]

    def _copy(self, k, blk, to, src=None):
        return pltpu.make_async_remote_copy(
            src_ref=self._slot(*blk) if src is None else src, dst_ref=self._slot(*blk), send_sem=self.send_sems.at[k],
            recv_sem=self.recv_sems.at[k], device_id=to, device_id_type=MESH)

    def _mine(self):
        return pltpu.make_async_copy(self.x_ref, self._slot(*self.me), self.local_sem)

    def _first(self):
        return [self._copy(0, self.me, self.sibling, src=self.x_ref)] + [
            self._copy(1 + j, self.me, (*chip, self.c), src=self.x_ref) for j, chip in enumerate(self.chips)]

    def _passed(self):
        return [self._copy(4 + j, (*chip, self.c), self.sibling) for j, chip in enumerate(self.chips)]

    def start(self):
        self._mine().start()
        for cp in self._first():
            cp.start()

    def forward(self):
        passed = self._passed()
        for j, chip in enumerate(self.chips):
            self._copy(1 + j, (*chip, self.c), self.me).wait_recv()
            passed[j].start()

    def finish(self):
        self._copy(0, self.sibling, self.me).wait_recv()
        for j, chip in enumerate(self.chips):
            self._copy(4 + j, (*chip, 1 - self.c), self.me).wait_recv()
        for cp in self._first() + self._passed():
            cp.wait_send()
        self._mine().wait()

    def wait_own_chip(self):
        self._mine().wait()
        self._copy(0, self.sibling, self.me).wait_recv()

    def wait_chip(self, j):
        self._copy(1 + j, (*self.chips[j], self.c), self.me).wait_recv()
        self._passed()[j].start()
        self._copy(4 + j, (*self.chips[j], 1 - self.c), self.me).wait_recv()

    def wait_sends(self):
        for cp in self._first() + self._passed():
            cp.wait_send()


class _Scatter:
    def __init__(self, p16_ref, p32_ref, got_ref, own_ref, send_sems, recv_sems, local_sem, lay, order):
        self.x, self.y, self.c = _place()
        self.chips = [(1 - self.x, self.y), (self.x, 1 - self.y), (1 - self.x, 1 - self.y)]
        self.refs = (p16_ref, p32_ref, got_ref, own_ref, send_sems, recv_sems, local_sem)
        self.lay, self.order = lay, order

    def _rows_of(self, ref, key, chip):
        start = pl.multiple_of(self.lay.g_off[key] + chip * self.lay.rows[key], ROW_ALIGN)
        return ref.at[pl.ds(start, self.lay.rows[key]), :]

    def _compact(self, ref, key):
        return ref.at[pl.ds(self.lay.c_off[key], self.lay.rows[key]), :]

    def start(self):
        p16_ref, p32_ref, got_ref, own_ref, send_sems, recv_sems, local_sem = self.refs
        for key in self.order:
            pltpu.make_async_copy(self._rows_of(p32_ref, key, 2 * self.x + self.y), self._compact(own_ref, key), local_sem).start()
        for k, (px, py) in enumerate(self.chips):
            for key in self.order:
                pltpu.make_async_remote_copy(
                    src_ref=self._rows_of(p16_ref, key, 2 * px + py), dst_ref=self._compact(got_ref.at[k], key), send_sem=send_sems.at[k],
                    recv_sem=recv_sems.at[k], device_id=(px, py, self.c), device_id_type=MESH).start()

    def finish(self):
        _, _, got_ref, own_ref, send_sems, recv_sems, local_sem = self.refs
        for k, (px, py) in enumerate(self.chips):
            pltpu.make_async_remote_copy(src_ref=got_ref.at[k], dst_ref=got_ref.at[k], send_sem=send_sems.at[k], recv_sem=recv_sems.at[k],
                                         device_id=(px, py, self.c), device_id_type=MESH).wait()
        pltpu.make_async_copy(own_ref, own_ref, local_sem).wait()


class _ScatterDirect:
    def __init__(self, g16_ref, g32_ref, got_ref, sib_ref, own_ref, send_sems, recv_sems, local_sem, lay, order):
        self.x, self.y, self.c = _place()
        self.chips = [(1 - self.x, self.y), (self.x, 1 - self.y), (1 - self.x, 1 - self.y)]
        self.refs = (g16_ref, g32_ref, got_ref, sib_ref, own_ref, send_sems, recv_sems, local_sem)
        self.lay, self.order, self.half = lay, order, lay.d // 2

    def _src(self, ref, key, chip, h):
        start = pl.multiple_of(self.lay.g_off[key] + chip * self.lay.rows[key], ROW_ALIGN)
        return ref.at[pl.ds(start, self.lay.rows[key]), pl.ds(pl.multiple_of(h * self.half, LANE), self.half)]

    def _compact(self, ref, key):
        return ref.at[pl.ds(self.lay.c_off[key], self.lay.rows[key]), :]

    def start(self):
        g16_ref, g32_ref, got_ref, sib_ref, own_ref, send_sems, recv_sems, local_sem = self.refs
        x, y, c = self.x, self.y, self.c
        for key in self.order:
            pltpu.make_async_copy(self._src(g32_ref, key, 2 * x + y, c), self._compact(own_ref, key), local_sem).start()
            pltpu.make_async_remote_copy(
                src_ref=self._src(g32_ref, key, 2 * x + y, 1 - c), dst_ref=self._compact(sib_ref, key), send_sem=send_sems.at[6],
                recv_sem=recv_sems.at[6], device_id=(x, y, 1 - c), device_id_type=MESH).start()
        for k, (px, py) in enumerate(self.chips):
            for h in range(2):
                for key in self.order:
                    pltpu.make_async_remote_copy(
                        src_ref=self._src(g16_ref, key, 2 * px + py, h), dst_ref=self._compact(got_ref.at[2 * k + c], key),
                        send_sem=send_sems.at[2 * k + h], recv_sem=recv_sems.at[2 * k + c], device_id=(px, py, h), device_id_type=MESH).start()

    def finish(self):
        _, _, got_ref, sib_ref, own_ref, send_sems, recv_sems, local_sem = self.refs
        x, y, c = self.x, self.y, self.c
        for k, (px, py) in enumerate(self.chips):
            for h in range(2):
                whole = pltpu.make_async_remote_copy(src_ref=got_ref.at[2 * k + h], dst_ref=got_ref.at[2 * k + h], send_sem=send_sems.at[2 * k + h],
                                                     recv_sem=recv_sems.at[2 * k + h], device_id=(px, py, h), device_id_type=MESH)
                whole.wait_send()
                whole.wait_recv()
        pltpu.make_async_remote_copy(src_ref=sib_ref, dst_ref=sib_ref, send_sem=send_sems.at[6], recv_sem=recv_sems.at[6],
                                     device_id=(x, y, 1 - c), device_id_type=MESH).wait()
        pltpu.make_async_copy(own_ref, own_ref, local_sem).wait()


def _scatter_direct_shapes(lay, group):
    rows, half = lay.c_rows[group], lay.d // 2
    return [_sds((6, rows, half), BF16), _sds((rows, half), F32), _sds((rows, half), F32)]


SCATTER_DIRECT_SEMS = [pltpu.SemaphoreType.DMA((7,)), pltpu.SemaphoreType.DMA((7,)), pltpu.SemaphoreType.DMA]
SCATTER_SEMS = [pltpu.SemaphoreType.DMA((3,)), pltpu.SemaphoreType.DMA((3,)), pltpu.SemaphoreType.DMA]
GATHER_SEMS = [pltpu.SemaphoreType.DMA((7,)), pltpu.SemaphoreType.DMA((7,)), pltpu.SemaphoreType.DMA]


def _all_gather8(blocks, name):
    nb = len(blocks)

    def body(*refs):
        x_refs, out_refs = refs[:nb], refs[nb:2 * nb]
        send_sems, recv_sems, local_sems = refs[2 * nb:]
        gathers = [_Gather8(x_refs[n], out_refs[n], send_sems.at[n], recv_sems.at[n], local_sems.at[n]) for n in range(nb)]
        for g in gathers:
            g.start()
        for g in gathers:
            g.forward()
        for g in gathers:
            g.finish()

    return pl.pallas_call(
        body, name=name, out_shape=[_sds((8,) + b.shape, b.dtype) for b in blocks], in_specs=[ANY] * nb, out_specs=[ANY] * nb,
        scratch_shapes=[pltpu.SemaphoreType.DMA((nb, 7)), pltpu.SemaphoreType.DMA((nb, 7)), pltpu.SemaphoreType.DMA((nb,))],
    )(*blocks)


def _swap_sibling(srcs, name, half_cols=False):
    n = len(srcs)
    halves = [s.shape[1] // 2 if half_cols else s.shape[1] for s in srcs]

    def body(*refs):
        src_refs, out_refs, send_sems, recv_sems = refs[:n], refs[n:2 * n], refs[2 * n], refs[2 * n + 1]
        x, y, c = _place()
        copies = []
        for k in range(n):
            part = src_refs[k].at[:, pl.ds(pl.multiple_of((1 - c) * halves[k], LANE), halves[k])] if half_cols else src_refs[k]
            copies.append(pltpu.make_async_remote_copy(src_ref=part, dst_ref=out_refs[k], send_sem=send_sems.at[k], recv_sem=recv_sems.at[k],
                                                       device_id=(x, y, 1 - c), device_id_type=MESH))
        for cp in copies:
            cp.start()
        for cp in copies:
            cp.wait()

    return pl.pallas_call(
        body, name=name, out_shape=[_sds((s.shape[0], h), s.dtype) for s, h in zip(srcs, halves)], in_specs=[ANY] * n, out_specs=[ANY] * n,
        scratch_shapes=[pltpu.SemaphoreType.DMA((n,)), pltpu.SemaphoreType.DMA((n,))],
    )(*srcs)


def _return_and_gather(mines, rep_block):
    n = len(mines)

    def body(*refs):
        src_refs, rep_ref, out_refs, rep_out = refs[:n], refs[n], refs[n + 1:2 * n + 1], refs[2 * n + 1]
        send_sems, recv_sems, g_send, g_recv, g_local = refs[2 * n + 2:]
        x, y, c = _place()
        copies = [pltpu.make_async_remote_copy(src_ref=src_refs[k], dst_ref=out_refs[k], send_sem=send_sems.at[k], recv_sem=recv_sems.at[k],
                                               device_id=(x, y, 1 - c), device_id_type=MESH) for k in range(n)]
        gather = _Gather8(rep_ref, rep_out, g_send, g_recv, g_local)
        for cp in copies:
            cp.start()
        gather.start()
        gather.forward()
        gather.finish()
        for cp in copies:
            cp.wait()

    return pl.pallas_call(
        body, name="rs_return", out_shape=[_sds(m.shape, m.dtype) for m in mines] + [_sds((8,) + rep_block.shape, rep_block.dtype)],
        in_specs=[ANY] * (n + 1), out_specs=[ANY] * (n + 1),
        scratch_shapes=[pltpu.SemaphoreType.DMA((n,)), pltpu.SemaphoreType.DMA((n,))] + GATHER_SEMS,
    )(*mines, rep_block)


def _scatter_shapes(lay, group, half):
    return [_sds((3, lay.c_rows[group], half), BF16), _sds((lay.c_rows[group], half), F32)]


def _scatter_chips(part16, part32, lay, group, name):
    def body(p16_ref, p32_ref, got_ref, own_ref, send_sems, recv_sems, local_sem):
        sc = _Scatter(p16_ref, p32_ref, got_ref, own_ref, send_sems, recv_sems, local_sem, lay, G_GROUPS[group])
        sc.start()
        sc.finish()

    return pl.pallas_call(
        body, name=name, out_shape=_scatter_shapes(lay, group, part16.shape[1]), in_specs=[ANY, ANY], out_specs=[ANY, ANY],
        scratch_shapes=SCATTER_SEMS,
    )(part16, part32)


def _sum_sibling(gbuf, got, cidx, name):
    rows, d = gbuf.shape
    half = d // 2
    rb = _row_block(rows)

    def body(c_ref, g_ref, r_ref, o32_ref, o16_ref):
        del c_ref
        s = g_ref[...] + r_ref[...]
        o32_ref[...] = s
        o16_ref[...] = s.astype(BF16)

    plain = pl.BlockSpec((rb, half), lambda i, c: (i, 0))
    return pl.pallas_call(
        body, name=name,
        grid_spec=pltpu.PrefetchScalarGridSpec(num_scalar_prefetch=1, grid=(rows // rb,),
                                               in_specs=[pl.BlockSpec((rb, half), lambda i, c: (i, c[0])), plain], out_specs=[plain, plain]),
        out_shape=[_sds((rows, half)), _sds((rows, half), BF16)], compiler_params=_params(1),
    )(cidx, gbuf, got)


def _sum_devices(own, sib, got, name):
    rows, half = own.shape
    rb = _row_block(rows)
    n = got.shape[0]

    def body(a_ref, s_ref, b_ref, o_ref):
        acc = a_ref[...] + s_ref[...]
        for k in range(n):
            acc = acc + b_ref[k].astype(F32)
        o_ref[...] = acc

    spec = pl.BlockSpec((rb, half), lambda i: (i, 0))
    return pl.pallas_call(
        body, name=name, grid=(rows // rb,), in_specs=[spec, spec, pl.BlockSpec((n, rb, half), lambda i: (0, i, 0))], out_specs=spec,
        out_shape=_sds((rows, half)), compiler_params=_params(1),
    )(own, sib, got)


def _sum_chips(own, got, name):
    rows, half = own.shape
    rb = _row_block(rows)

    def body(a_ref, b_ref, o_ref):
        o_ref[...] = ((a_ref[...] + b_ref[0].astype(F32)) + b_ref[1].astype(F32)) + b_ref[2].astype(F32)

    spec = pl.BlockSpec((rb, half), lambda i: (i, 0))
    return pl.pallas_call(
        body, name=name, grid=(rows // rb,), in_specs=[spec, pl.BlockSpec((3, rb, half), lambda i: (0, i, 0))], out_specs=spec,
        out_shape=_sds((rows, half)), compiler_params=_params(1),
    )(own, got)


def _adamw(w, g, m, v):
    m = ADAM_B1 * m + (1.0 - ADAM_B1) * g
    v = ADAM_B2 * v + (1.0 - ADAM_B2) * (g * g)
    m_hat = m / (1.0 - ADAM_B1 ** ADAM_STEP)
    v_hat = v / (1.0 - ADAM_B2 ** ADAM_STEP)
    return -ADAM_LR * (m_hat / (jnp.sqrt(v_hat) + ADAM_EPS) + ADAM_WD * w), m, v


def _adamw_rows(name, w, g, m, v):
    _, rows, cols = w.shape
    rb = _row_block(rows, 256)

    def body(w_ref, g_ref, m_ref, v_ref, d_ref, mo_ref, vo_ref):
        d_ref[...], mo_ref[...], vo_ref[...] = _adamw(w_ref[...], g_ref[...], m_ref[...], v_ref[...])

    spec = pl.BlockSpec((1, rb, cols), lambda i: (0, i, 0))
    return pl.pallas_call(
        body, name=name, grid=(rows // rb,), in_specs=[spec] * 4, out_specs=[spec] * 3, out_shape=[_sds(w.shape)] * 3,
        compiler_params=_params(1),
    )(w, g, m, v)


def _adamw_group(ws, gs, ms, vs):
    n = len(ws)

    def body(*refs):
        for k in range(n):
            w_ref, g_ref, m_ref, v_ref = (refs[j * n + k] for j in range(4))
            outs = _adamw(w_ref[...], g_ref[...], m_ref[...], v_ref[...])
            for j in range(3):
                refs[(4 + j) * n + k][...] = outs[j]

    outs = pl.pallas_call(
        body, name="adamw_small", out_shape=[_sds(w.shape) for w in ws] * 3,
        compiler_params=pltpu.CompilerParams(vmem_limit_bytes=VMEM_LIMIT),
    )(*ws, *gs, *ms, *vs)
    return outs[:n], outs[n:2 * n], outs[2 * n:]


def _gather_weights(sh, lay):
    x, y, c = _place()
    d = lay.d
    uq = sh["w_uq"][0].astype(BF16)
    parts = {
        "in_b": sh["w_in_b"][0].T.astype(BF16), "in_a": sh["w_in_a"][0].T.astype(BF16), "out_a": sh["w_out_a"][0].astype(BF16),
        "out_b": sh["w_out_b"][0].astype(BF16), "uk": sh["w_uk"].astype(BF16).reshape(-1, d), "uv": sh["w_uv"].astype(BF16).reshape(-1, d),
        "uq_n": uq[:, :, :QK_NOPE].reshape(-1, d), "uq_r": jnp.pad(uq[:, :, QK_NOPE:], ((0, 0), (0, 0), (0, LANE - QK_ROPE))).reshape(-1, d),
        "dkv": jnp.pad(sh["w_dkv"].astype(BF16), ((0, 0), (0, LANE - QK_ROPE))).reshape(-1, d),
    }
    halves = {}
    for group, order in W_GROUPS.items():
        stack = jnp.concatenate([parts[k] for k in order], axis=0).reshape(2, lay.w_rows[group] // 2, d)
        halves[group] = lax.dynamic_index_in_dim(stack, c, 0, keepdims=False)
    small = jnp.concatenate([sh[k].reshape(-1) for k in SMALL])
    n_small = small.shape[0]
    width = _round_up(n_small, 2 * SUBLANE * LANE) // (2 * SUBLANE)
    small = jnp.pad(small, (0, 2 * SUBLANE * width - n_small)).reshape(2, SUBLANE, width)
    (sg,) = _all_gather8([lax.dynamic_index_in_dim(small, c, 0, keepdims=False)], "ag_small")
    sg = sg.reshape(N_CHIPS, 2 * SUBLANE * width)
    full, off = {}, 0
    for k in SMALL:
        n = sh[k].size
        piece = sg[:, off:off + n]
        off += n
        if k == "conv_w":
            full[k] = piece.reshape(N_CHIPS, 4, n // 4).transpose(1, 0, 2).reshape(4, n)
        else:
            full[k] = piece.reshape(1, N_CHIPS * n)
    return halves["a"], halves["b"], full


def _chip_split(g, taps=False):
    if taps:
        n = g.shape[1] // N_CHIPS
        return g.reshape(4, N_CHIPS, n).transpose(1, 0, 2).reshape(N_CHIPS, 4 * n)
    return g.reshape(N_CHIPS, -1)


def kernel(x, norm_a, w_in_a, conv_w, conv_b, w_rg, b_rg, w_ig, b_ig, lru_lambda, w_out_a, norm_kv, w_dkv, kv_norm, w_uk, w_uv, norm_b, w_in_b, q_norm, w_uq, w_out_b, final_norm, loss_target, m_norm_a, m_w_in_a, m_conv_w, m_conv_b, m_w_rg, m_b_rg, m_w_ig, m_b_ig, m_lru_lambda, m_w_out_a, m_norm_kv, m_w_dkv, m_kv_norm, m_w_uk, m_w_uv, m_norm_b, m_w_in_b, m_q_norm, m_w_uq, m_w_out_b, m_final_norm, v_norm_a, v_w_in_a, v_conv_w, v_conv_b, v_w_rg, v_b_rg, v_w_ig, v_b_ig, v_lru_lambda, v_w_out_a, v_norm_kv, v_w_dkv, v_kv_norm, v_w_uk, v_w_uv, v_norm_b, v_w_in_b, v_q_norm, v_w_uq, v_w_out_b, v_final_norm):
    given = dict(locals())
    sh = {k: given[k] for k in WEIGHTS}
    xi, yi, ci = _place()
    nb, seq, d = x.shape
    t_all = nb * seq
    tb_a, tb_b, ta, bt = min(TOKENS_A, seq), min(TOKENS_B, seq), min(TOKENS_ATTN, seq), min(TOKENS_MM, t_all)
    dr = conv_b.shape[1] * N_CHIPS
    qr, kvr, nheads = q_norm.shape[1], kv_norm.shape[0], w_uk.shape[1]
    hv = nheads * LANE
    n_small = sum(sh[k].size for k in SMALL)
    n_repl = sum(sh[k].size for k in REPL)
    lay = _Layout(d, dr, qr, kvr, hv, n_small, n_repl)
    half = d // 2

    wa_half, wb_half, w = _gather_weights(sh, lay)
    w.update({"w_rg": w_rg[0].astype(BF16), "w_ig": w_ig[0].astype(BF16), "norm_kv": norm_kv[None, :],
              "kv_norm": kv_norm[None, :], "final_norm": final_norm[None, :], "norm_b": norm_b, "q_norm": q_norm})
    cos_t, sin_t = _rope_tables(seq)
    cidx = jnp.reshape(ci, (1,)).astype(jnp.int32)

    x0 = x.reshape(t_all, d)
    u, wga = _fa_inproj(x0, wa_half, w, lay, tb_b)
    wga = wga.reshape(N_CHIPS, lay.w_rows["a"], d)
    x1, hs, y, xb, wgb = _fa_fwd(x0, u, wga, wb_half, w, lay, seq, tb_a)
    wgb = wgb.reshape(N_CHIPS, lay.w_rows["b"], d)
    w["w_dkv_p"] = wgb[:, lay.w_off["dkv"]:lay.w_off["dkv"] + lay.rows["dkv"], :].reshape(d, kvr + LANE)
    qn, qrp, kn, kr, v, ub, ckr, hb, hk, cq, ckv = _fb_fwd(x1, wgb, w, lay, cos_t, sin_t, seq, tb_b)
    o, lse = _attn_fwd(qn, qrp, kn, kr, v, seq, ta)
    loss, g_final_norm, yb, dx2, do, dgate, delta = _head(o, ub, x1, loss_target.reshape(t_all, d), wgb, w, lay, tb_b)
    dqn, dqr, dkn, dkr, dv = _attn_bwd(qn, qrp, kn, kr, v, do, lse, delta, seq, ta)
    dx1, dqr_pre, dqn_pre, dub, dckr, g_q_norm, g_norm_b, g_kv_norm, g_norm_kv = _fb_bwd(
        dqn, dqr, dkn, dkr, dv, dgate, ub, ckr, x1, dx2, wgb, w, lay, cos_t, sin_t, seq, tb_b)
    loss = lax.psum(loss[0, 0], ("x", "y", "c"))

    gbufs = [lax.empty((lay.g_rows["early"], d), F32), lax.empty((lay.g_rows["early"], d), BF16)]
    for key, a, b in (("in_b", dub, hb), ("out_a", y, dx1), ("out_b", yb, dx2), ("uk", ckv, dkn), ("uv", ckv, dv), ("uq_n", cq, dqn_pre),
                      ("uq_r", cq, dqr_pre)):
        gbufs = _mm_into(gbufs, a, b, lay.g_off[key], "dw_" + key, bt)
    g_dkv = _mm_tn(hk, dckr, "dw_dkv", bt)
    gx, du, h, g_norm_a, g_conv_w, g_conv_b, g_b_rg, g_b_ig, g_lam, g_w_rg, g_w_ig, others, sib, own = _fa_bwd(
        dx1, x0, u, xb, hs, wga, gbufs[1], gbufs[0], w, lay, seq, tb_a)
    mine_early = _sum_devices(own, sib, others, "rs_sum_early")

    small = jnp.concatenate([_chip_split(g_norm_a), _chip_split(g_conv_w, taps=True), _chip_split(g_conv_b), _chip_split(g_b_rg),
                             _chip_split(g_b_ig), _chip_split(g_lam)], axis=1)
    small = jnp.pad(small, ((0, 0), (0, lay.small_rows * d - small.shape[1]))).reshape(N_CHIPS, lay.small_rows, d)
    repl_parts = {"w_rg": g_w_rg, "w_ig": g_w_ig, "norm_kv": g_norm_kv, "kv_norm": g_kv_norm, "norm_b": g_norm_b, "q_norm": g_q_norm,
                  "final_norm": g_final_norm}
    repl = jnp.concatenate([repl_parts[k].reshape(-1) for k in REPL])
    repl = jnp.pad(repl, (0, N_CHIPS * lay.repl_rows * d - n_repl)).reshape(N_CHIPS, lay.repl_rows, d)
    pad_rows = lay.rows["rest"] - lay.rows["dkv"] - lay.small_rows - lay.repl_rows
    rest = jnp.concatenate([g_dkv.reshape(N_CHIPS, lay.rows["dkv"], d), small, repl, jnp.zeros((N_CHIPS, pad_rows, d), F32)], axis=1)
    others, own = _dw_in_a_exchange(du, h, rest.reshape(N_CHIPS * lay.rows["rest"], d), lay, bt)
    mine_late = _sum_chips(own, others, "rs_sum_chips_late")

    r0 = lay.c_off["rest"] + lay.rows["dkv"] + lay.small_rows
    theirs_early, theirs_late, rep_all = _return_and_gather([mine_early, mine_late], mine_late[r0:r0 + lay.repl_rows])
    red = {}
    for group, mine, theirs in (("early", mine_early, theirs_early), ("late", mine_late, theirs_late)):
        red[group] = jnp.concatenate([jnp.where(ci == 0, mine, theirs), jnp.where(ci == 0, theirs, mine)], axis=1)
    rep_flat =rep_all.reshape(N_CHIPS, 2, lay.repl_rows, half).transpose(0, 2, 1, 3).reshape(-1)

    def rows(key):
        group = "late" if key in G_GROUPS["late"] else "early"
        return red[group][lay.c_off[key]:lay.c_off[key] + lay.rows[key]]

    grads = {"w_in_b": rows("in_b").T[None], "w_in_a": rows("in_a").T[None], "w_out_a": rows("out_a")[None], "w_out_b": rows("out_b")[None],
             "w_uk": rows("uk").reshape(w_uk.shape), "w_uv": rows("uv").reshape(w_uv.shape)}
    uq_n = rows("uq_n").reshape(qr // N_CHIPS, nheads, LANE)
    uq_r = rows("uq_r").reshape(qr // N_CHIPS, nheads, LANE)[:, :, :QK_ROPE]
    grads["w_uq"] = jnp.concatenate([uq_n, uq_r], axis=2)[None]
    rest_red = rows("rest")
    grads["w_dkv"] = rest_red[:lay.rows["dkv"]].reshape(d // N_CHIPS, kvr + LANE)[:, :kvr + QK_ROPE]
    small_red = rest_red[lay.rows["dkv"]:lay.rows["dkv"] + lay.small_rows].reshape(-1)
    off = 0
    for k in SMALL:
        n = sh[k].size
        grads[k] = small_red[off:off + n].reshape(sh[k].shape)
        off += n
    off = 0
    for k in REPL:
        n = sh[k].size
        grads[k] = rep_flat[off:off + n].reshape(sh[k].shape)
        off += n

    new = {}
    for k in ("w_in_a", "w_in_b", "w_out_a", "w_out_b"):
        view = (lambda a: jnp.swapaxes(a, 1, 2)) if k in TRANSPOSED else (lambda a: a)
        outs = _adamw_rows("adamw_" + k, view(sh[k]), view(grads[k]), view(given["m_" + k]), view(given["v_" + k]))
        new[k] = tuple(view(a) for a in outs)
    rest_names = [k for k in WEIGHTS if k not in new]

    def as2d(k, a):
        return a.T if k in TRANSPOSED else a[None, :] if a.ndim == 1 else a

    ds, ms, vs = _adamw_group([as2d(k, sh[k]) for k in rest_names], [as2d(k, grads[k]) for k in rest_names],
                              [as2d(k, given["m_" + k]) for k in rest_names], [as2d(k, given["v_" + k]) for k in rest_names])
    for n, k in enumerate(rest_names):
        new[k] = tuple((a.T if k in TRANSPOSED else a).reshape(sh[k].shape) for a in (ds[n], ms[n], vs[n]))
    return (loss, gx.reshape(nb, seq, d), *[grads[k] for k in WEIGHTS], *[new[k][0] for k in WEIGHTS], *[new[k][1] for k in WEIGHTS],
            *[new[k][2] for k in WEIGHTS])
```

```python
import jax
import jax.numpy as jnp
from jax import lax
from jax.experimental import pallas as pl
from jax.experimental.pallas import tpu as pltpu

F32, BF16 = jnp.float32, jnp.bfloat16
EPS = 1e-6
LRU_C = 8.0
ROPE_THETA = 10000.0
QK_NOPE, QK_ROPE = 128, 64
ATTN_SCALE = (QK_NOPE + QK_ROPE) ** -0.5
LN2 = 0.6931471805599453
Q_SCALE = ATTN_SCALE / LN2
ATTN_HEADS, ATTN_HEADS_BWD = 4, 2
ATTN_ROWS = 64
LANE = 128
SUBLANE = 8
ROW_ALIGN = 32
VMEM_LIMIT = 60000 * 1024
ADAM_LR, ADAM_B1, ADAM_B2, ADAM_EPS, ADAM_WD, ADAM_STEP = 0.001, 0.9, 0.999, 1e-08, 0.01, 10
MESH = pl.DeviceIdType.MESH
ANY = pl.BlockSpec(memory_space=pl.ANY)
N_CHIPS = 4
TOKENS_A, TOKENS_B, TOKENS_ATTN, TOKENS_MM = 256, 512, 512, 2048
TOKENS_A_FWD = 512

SMALL = ("norm_a", "conv_w", "conv_b", "b_rg", "b_ig", "lru_lambda")
REPL = ("w_rg", "w_ig", "norm_kv", "kv_norm", "norm_b", "q_norm", "final_norm")
TRANSPOSED = ("w_in_b", "w_dkv")
WEIGHTS = ("norm_a", "w_in_a", "conv_w", "conv_b", "w_rg", "b_rg", "w_ig", "b_ig", "lru_lambda", "w_out_a", "norm_kv",
           "w_dkv", "kv_norm", "w_uk", "w_uv", "norm_b", "w_in_b", "q_norm", "w_uq", "w_out_b", "final_norm")
W_GROUPS = {"a": ("in_a", "out_a"), "b": ("in_b", "out_b", "uk", "uv", "uq_n", "uq_r", "dkv")}
G_GROUPS = {"early": ("in_b", "out_a", "out_b", "uk", "uv", "uq_n", "uq_r"), "late": ("in_a", "rest")}


def _sds(shape, dtype=F32):
    return jax.ShapeDtypeStruct(tuple(shape), dtype)


def _params(n_grid):
    return pltpu.CompilerParams(dimension_semantics=("arbitrary",) * n_grid, vmem_limit_bytes=VMEM_LIMIT)


def _full(shape):
    nd = len(shape)
    return pl.BlockSpec(tuple(shape), lambda *g: (0,) * nd)


def _round_up(n, k):
    return -(-n // k) * k


def _row_block(rows, cap=512):
    best = SUBLANE
    for r in range(SUBLANE, min(rows, cap) + 1, SUBLANE):
        if rows % r == 0:
            best = r
    return best


def _place():
    return lax.axis_index("x"), lax.axis_index("y"), lax.axis_index("c")


class _Layout:
    def __init__(self, d, dr, qr, kvr, hv, n_small, n_repl):
        assert hv == d, "the packed rows are D_MODEL wide, which must equal heads * 128"
        self.d, self.dr, self.qr, self.kvr, self.hv = d, dr, qr, kvr, hv
        per_chip = {"in_b": (qr + hv) // N_CHIPS, "in_a": 2 * dr // N_CHIPS, "out_a": dr // N_CHIPS, "out_b": hv // N_CHIPS,
                    "uk": kvr // N_CHIPS, "uv": kvr // N_CHIPS, "uq_n": qr // N_CHIPS, "uq_r": qr // N_CHIPS,
                    "dkv": (d // N_CHIPS) * (kvr + LANE) // d}
        assert all(r % ROW_ALIGN == 0 for r in per_chip.values()), per_chip
        self.small_rows = _round_up(-(-n_small // d), SUBLANE)
        self.repl_rows = _round_up(-(-n_repl // (N_CHIPS * d)), SUBLANE)
        per_chip["rest"] = _round_up(per_chip["dkv"] + self.small_rows + self.repl_rows, ROW_ALIGN)
        self.rows = per_chip
        self.w_off, self.w_rows = {}, {}
        for group, order in W_GROUPS.items():
            off = 0
            for k in order:
                self.w_off[k] = off
                off += per_chip[k]
            assert off % ROW_ALIGN == 0, (group, off)
            self.w_rows[group] = off
        self.g_off, self.c_off, self.c_rows, self.g_rows = {}, {}, {}, {}
        for group, order in G_GROUPS.items():
            off = 0
            for k in order:
                self.c_off[k] = off
                self.g_off[k] = N_CHIPS * off
                off += per_chip[k]
            self.c_rows[group] = off
            self.g_rows[group] = N_CHIPS * off


def _dot(a, b):
    return jnp.dot(a, b, preferred_element_type=F32)


def _dot_nt(a, b):
    return lax.dot_general(a, b, (((1,), (1,)), ((), ())), preferred_element_type=F32)


def _dot_tn(a, b):
    return lax.dot_general(a, b, (((0,), (0,)), ((), ())), preferred_element_type=F32)


def _rinv(x):
    return lax.rsqrt(jnp.mean(x * x, axis=-1, keepdims=True) + EPS)


def _rms_bwd(x, rinv, g, dy):
    z = dy * g
    dx = rinv * z - x * (rinv * rinv * rinv) * jnp.mean(z * x, axis=-1, keepdims=True)
    dg = jnp.sum(dy * (x * rinv), axis=0, keepdims=True)
    return dx, dg


def _softplus(z):
    return jnp.maximum(z, 0.0) + jnp.log1p(jnp.exp(-jnp.abs(z)))


def _sigmoid(x):
    return 0.5 * jnp.tanh(0.5 * x) + 0.5


def _decay(log_a):
    a = jnp.exp(log_a)
    a2 = a * a
    return a, a2, -jnp.tanh(log_a) * (a2 + 1.0)


def _swap_halves(x):
    w = x.shape[1]
    lane = lax.broadcasted_iota(jnp.int32, x.shape, 1)
    return jnp.where(lane % QK_ROPE < QK_ROPE // 2, pltpu.roll(x, w - QK_ROPE // 2, 1), pltpu.roll(x, QK_ROPE // 2, 1))


def _rope_tables(seq):
    pos = jnp.arange(seq, dtype=F32)
    inv = ROPE_THETA ** (-jnp.arange(0, QK_ROPE, 2, dtype=F32) / QK_ROPE)
    ang = pos[:, None] * inv[None, :]
    cos, sin = jnp.cos(ang), jnp.sin(ang)
    zero = jnp.zeros((seq, LANE - QK_ROPE), F32)
    return jnp.concatenate([cos, cos, zero], 1), jnp.concatenate([-sin, sin, zero], 1)


def _fetch(wg_ref, lay, key, dst, sems, k0):
    rows = lay.rows[key]
    return [pltpu.make_async_copy(wg_ref.at[p, pl.ds(lay.w_off[key], rows), :], dst.at[pl.ds(p * rows, rows), :], sems.at[k0 + p])
            for p in range(N_CHIPS)]


def _gates(xb, wrg_ref, brg, wig_ref, big, nblocks):
    xbb = xb.astype(BF16)
    rg = [_dot(xbb[:, n * LANE:(n + 1) * LANE], wrg_ref[n]) for n in range(nblocks)]
    ig = [_dot(xbb[:, n * LANE:(n + 1) * LANE], wig_ref[n]) for n in range(nblocks)]
    r = _sigmoid(jnp.concatenate(rg, axis=1) + brg)
    i = _sigmoid(jnp.concatenate(ig, axis=1) + big)
    return r, i


def _conv(xpad, cw_ref, cb, tb):
    return (cb + cw_ref[3:4, :] * xpad[pl.ds(8, tb), :] + cw_ref[2:3, :] * xpad[pl.ds(7, tb), :]
            + cw_ref[1:2, :] * xpad[pl.ds(6, tb), :] + cw_ref[0:1, :] * xpad[pl.ds(5, tb), :])


def _fa_fwd(x, wg, wb_half, w, lay, seq, tb):
    t_all, d = x.shape
    dr = lay.dr
    nblocks = w["w_rg"].shape[0]
    nblk = seq // tb
    nt = tb // SUBLANE
    nsteps = (t_all // seq) * nblk

    def body(x_ref, wg_ref, wbh_ref, na, cw, cb, wrg, brg, wig, big, lam, x1_ref, u_ref, hs_ref, h_ref, y_ref, xb_ref, wb_ref,
             wint, wout, xpad, a_s, b_s, carry, sems, send_sems, recv_sems, local_sem):
        step_no = pl.program_id(0) * nblk + pl.program_id(1)
        gather = _Gather8(wbh_ref, wb_ref, send_sems, recv_sems, local_sem)

        @pl.when(step_no == 0)
        def _():
            gather.start()
            cps = _fetch(wg_ref, lay, "in_a", wint, sems, 0) + _fetch(wg_ref, lay, "out_a", wout, sems, N_CHIPS)
            for cp in cps:
                cp.start()
            for cp in cps:
                cp.wait()

        @pl.when(step_no == nsteps // 2)
        def _():
            gather.forward()

        @pl.when(pl.program_id(1) == 0)
        def _():
            xpad[pl.ds(0, 8), :] = jnp.zeros((8, dr), F32)
            carry[...] = jnp.zeros((8, dr), F32)

        xv = x_ref[...]
        h = (xv * _rinv(xv) * na[...]).astype(BF16)
        h_ref[...] = h
        u = _dot_nt(h, wint[...])
        u_ref[...] = u
        xpre, gate = u[:, :dr], u[:, dr:]
        xpad[pl.ds(8, tb), :] = xpre
        xb = _conv(xpad, cw, cb[...], tb)
        xb_ref[...] = xb
        xpad[pl.ds(0, 8), :] = xpre[tb - 8:, :]
        r, i = _gates(xb, wrg, brg[...], wig, big[...], nblocks)
        log_a = -LRU_C * r * _softplus(-lam[...])
        a, _, nem = _decay(log_a)
        a_s[...] = a
        b_s[...] = jnp.sqrt(nem) * (i * xb)
        row = lax.broadcasted_iota(jnp.int32, (8, dr), 0)

        def step(t, c):
            r0 = pl.multiple_of(t * 8, 8)
            a = a_s[pl.ds(r0, 8), :]
            b = b_s[pl.ds(r0, 8), :]
            for s in (1, 2, 4):
                m = row >= s
                a_sh = jnp.where(m, pltpu.roll(a, s, 0), 1.0)
                b_sh = jnp.where(m, pltpu.roll(b, s, 0), 0.0)
                b = a * b_sh + b
                a = a * a_sh
            hh = b + a * c
            hs_ref[pl.ds(r0, 8), :] = hh
            return jnp.broadcast_to(hh[7:8, :], hh.shape)

        carry[...] = lax.fori_loop(0, nt, step, carry[...])
        y = (hs_ref[...] * (gate * _sigmoid(gate))).astype(BF16)
        y_ref[...] = y
        x1_ref[...] = xv + _dot(y, wout[...])

        @pl.when(step_no == nsteps - 1)
        def _():
            gather.finish()

    tok = lambda c: pl.BlockSpec((tb, c), lambda b, j: (b * nblk + j, 0))
    consts = [w["norm_a"], w["conv_w"], w["conv_b"], w["w_rg"], w["b_rg"], w["w_ig"], w["b_ig"], w["lru_lambda"]]
    return pl.pallas_call(
        body, name="fa_fwd", grid=(t_all // seq, nblk),
        in_specs=[tok(d), ANY, ANY] + [_full(c.shape) for c in consts],
        out_specs=[tok(d), tok(2 * dr), tok(dr), tok(d), tok(dr), tok(dr), ANY],
        out_shape=[_sds((t_all, d)), _sds((t_all, 2 * dr)), _sds((t_all, dr)), _sds((t_all, d), BF16), _sds((t_all, dr), BF16),
                   _sds((t_all, dr)), _sds((8,) + wb_half.shape, BF16)],
        scratch_shapes=[pltpu.VMEM((2 * dr, d), BF16), pltpu.VMEM((dr, d), BF16), pltpu.VMEM((tb + 8, dr), F32), pltpu.VMEM((tb, dr), F32),
                        pltpu.VMEM((tb, dr), F32), pltpu.VMEM((8, dr), F32), pltpu.SemaphoreType.DMA((2 * N_CHIPS,))] + GATHER_SEMS,
        compiler_params=_params(2),
    )(x, wg, wb_half, *consts)


def _fb_fwd(x1, wg, w, lay, cos_t, sin_t, seq, tb):
    t_all, d = x1.shape
    kvr, qr, hv = lay.kvr, lay.qr, lay.hv
    nheads = hv // LANE
    npos = seq // tb

    def body(x_ref, wg_ref, nkv, nb, wdkv, kvn, qn, cos_ref, sin_ref,
             qn_o, qr_o, kn_o, kr_o, v_o, ub_o, ckr_o, hb_o, hk_o, cq_o, ckv_o, winb, wuk, wuv, wuqn, wuqr, sems):
        @pl.when(pl.program_id(0) == 0)
        def _():
            cps = []
            for n, (key, dst) in enumerate((("in_b", winb), ("uk", wuk), ("uv", wuv), ("uq_n", wuqn), ("uq_r", wuqr))):
                cps += _fetch(wg_ref, lay, key, dst, sems, n * N_CHIPS)
            for cp in cps:
                cp.start()
            for cp in cps:
                cp.wait()

        xv = x_ref[...]
        xh = xv * _rinv(xv)
        hk = (xh * nkv[...]).astype(BF16)
        hb = (xh * nb[...]).astype(BF16)
        hk_o[...] = hk
        hb_o[...] = hb
        cos, sin = cos_ref[...], sin_ref[...]
        ckr = _dot(hk, wdkv[...])
        ckr_o[...] = ckr
        ckv_pre = ckr[:, :kvr]
        ckv = (ckv_pre * _rinv(ckv_pre) * kvn[...]).astype(BF16)
        ckv_o[...] = ckv
        kr = ckr[:, kvr:]
        kr_o[...] = (kr * cos + _swap_halves(kr) * sin).astype(BF16)
        kn_o[...] = _dot(ckv, wuk[...]).astype(BF16)
        v_o[...] = _dot(ckv, wuv[...]).astype(BF16)
        ub = _dot_nt(hb, winb[...])
        ub_o[...] = ub
        cq_pre = ub[:, :qr]
        cq = (cq_pre * _rinv(cq_pre) * qn[...]).astype(BF16)
        cq_o[...] = cq
        qn_o[...] = (_dot(cq, wuqn[...]) * Q_SCALE).astype(BF16)
        qrope = _dot(cq, wuqr[...]) * Q_SCALE
        qr_o[...] = (qrope * jnp.tile(cos, (1, nheads)) + _swap_halves(qrope) * jnp.tile(sin, (1, nheads))).astype(BF16)

    tok = lambda c: pl.BlockSpec((tb, c), lambda i: (i, 0))
    pos = pl.BlockSpec((tb, LANE), lambda i: (i % npos, 0))
    consts = [w["norm_kv"], w["norm_b"], w["w_dkv_p"], w["kv_norm"], w["q_norm"]]
    outs = [(hv, BF16), (hv, BF16), (hv, BF16), (LANE, BF16), (hv, BF16), (qr + hv, F32), (kvr + LANE, F32), (d, BF16), (d, BF16), (qr, BF16), (kvr, BF16)]
    return pl.pallas_call(
        body, name="fb_fwd", grid=(t_all // tb,),
        in_specs=[tok(d), ANY] + [_full(c.shape) for c in consts] + [pos, pos],
        out_specs=[tok(c) for c, _ in outs],
        out_shape=[_sds((t_all, c), dt) for c, dt in outs],
        scratch_shapes=[pltpu.VMEM((qr + hv, d), BF16), pltpu.VMEM((kvr, d), BF16), pltpu.VMEM((kvr, d), BF16), pltpu.VMEM((qr, d), BF16),
                        pltpu.VMEM((qr, d), BF16), pltpu.SemaphoreType.DMA((5 * N_CHIPS,))],
        compiler_params=_params(1),
    )(x1, wg, *consts, cos_t, sin_t)


def _causal_mask(row0, col0, nrows, ncols):
    rows = row0 + lax.broadcasted_iota(jnp.int32, (nrows, ncols), 0)
    cols = col0 + lax.broadcasted_iota(jnp.int32, (nrows, ncols), 1)
    return cols <= rows


def _attn_fwd(qn, qr, kn, kr, v, seq, ta):
    t_all, hv = qn.shape
    nheads, nb, na = hv // LANE, t_all // seq, seq // ta

    reps = ta // LANE
    hp = ATTN_HEADS
    wide = hp * LANE

    def body(qn_ref, qr_ref, kn_ref, kr_ref, v_ref, o_ref, lse_ref, m_s, l_s, acc_s):
        i = pl.program_id(2)
        m_s[...] = jnp.full((ta, wide), -1e30, F32)
        l_s[...] = jnp.zeros((ta, wide), F32)
        acc_s[...] = jnp.zeros((ta, wide), F32)
        heads = [slice(n * LANE, (n + 1) * LANE) for n in range(hp)]
        qs = [jnp.concatenate([qn_ref[:, hd], qr_ref[:, hd]], axis=1) for hd in heads]

        def tile(j, diagonal):
            cols = pl.ds(pl.multiple_of(j * ta, ta), ta)
            k_rope = kr_ref[cols, :]
            for q, hd in zip(qs, heads):
                k = jnp.concatenate([kn_ref[cols, hd], k_rope], axis=1)
                s = _dot_nt(q, k)
                if diagonal:
                    s = jnp.where(_causal_mask(0, 0, ta, ta), s, -1e30)
                m_prev = m_s[:, hd]
                m_new = jnp.maximum(m_prev, jnp.max(s, axis=1, keepdims=True))
                p = jnp.exp2(s - jnp.tile(m_new, (1, reps)))
                alpha = jnp.exp2(m_prev - m_new)
                l_s[:, hd] = alpha * l_s[:, hd] + jnp.sum(p, axis=1, keepdims=True)
                acc_s[:, hd] = alpha * acc_s[:, hd] + _dot(p.astype(BF16), v_ref[cols, hd])
                m_s[:, hd] = m_new

        def off_diagonal(j, carry):
            tile(j, False)
            return carry

        lax.fori_loop(0, i, off_diagonal, 0)
        tile(i, True)
        o_ref[...] = (acc_s[...] / l_s[...]).astype(BF16)
        lse_ref[...] = m_s[...] + jnp.log2(l_s[...])

    qspec = pl.BlockSpec((ta, wide), lambda b, h, i: (b * na + i, h))
    kspec = pl.BlockSpec((seq, wide), lambda b, h, i: (b, h))
    krspec = pl.BlockSpec((seq, LANE), lambda b, h, i: (b, 0))
    return pl.pallas_call(
        body, name="attn_fwd", grid=(nb, nheads // hp, na),
        in_specs=[qspec, qspec, kspec, krspec, kspec],
        out_specs=[qspec, qspec],
        out_shape=[_sds((t_all, hv), BF16), _sds((t_all, hv))],
        scratch_shapes=[pltpu.VMEM((ta, wide), F32)] * 3,
        compiler_params=_params(3),
    )(qn, qr, kn, kr, v)


def _attn_bwd(qn, qr, kn, kr, v, do, lse, delta, seq, ta):
    t_all, hv = qn.shape
    nheads, nb, na = hv // LANE, t_all // seq, seq // ta

    reps = ta // LANE
    nchunks = ta // ATTN_ROWS

    hp = ATTN_HEADS_BWD
    wide = hp * LANE
    heads = [slice(n * LANE, (n + 1) * LANE) for n in range(hp)]

    def body(qn_ref, qr_ref, kn_ref, kr_ref, v_ref, do_ref, lse_ref, dl_ref, dqn_out, dqr_out, dkn_ref, dkr_ref, dv_ref,
             s_s, dp_s, p_s, ds_s, dk_s, dv_s, dqn_ref, dqr_ref):
        j = pl.program_id(2)

        @pl.when(j == 0)
        def _():
            dqn_ref[...] = jnp.zeros((seq, wide), F32)
            dqr_ref[...] = jnp.zeros((seq, wide), F32)

        dk_s[...] = jnp.zeros((hp, ta, 2 * LANE), F32)
        dv_s[...] = jnp.zeros((hp, ta, LANE), F32)
        k_rope = kr_ref[...]
        ks = [jnp.concatenate([kn_ref[:, hd], k_rope], axis=1) for hd in heads]

        def tile(i, diagonal):
            rows_i = pl.ds(pl.multiple_of(i * ta, ta), ta)
            for n, hd in enumerate(heads):
                q = jnp.concatenate([qn_ref[rows_i, hd], qr_ref[rows_i, hd]], axis=1)
                do_b = do_ref[rows_i, hd]
                s_s[n] = _dot_nt(q, ks[n])
                dp_s[n] = _dot_nt(do_b, v_ref[:, hd])
                for c in range(nchunks):
                    rows = pl.ds(c * ATTN_ROWS, ATTN_ROWS)
                    seq_rows = pl.ds(pl.multiple_of(i * ta + c * ATTN_ROWS, ATTN_ROWS), ATTN_ROWS)
                    s = s_s[n, rows, :]
                    if diagonal:
                        s = jnp.where(_causal_mask(c * ATTN_ROWS, 0, ATTN_ROWS, ta), s, -1e30)
                    p = jnp.exp2(s - jnp.tile(lse_ref[seq_rows, hd], (1, reps)))
                    p_s[n, rows, :] = p.astype(BF16)
                    ds_s[n, rows, :] = (p * (dp_s[n, rows, :] - jnp.tile(dl_ref[seq_rows, hd], (1, reps)))).astype(BF16)
                dv_s[n] += _dot_tn(p_s[n], do_b)
                ds = ds_s[n]
                dk_s[n] += _dot_tn(ds, q)
                dq = _dot(ds, ks[n])
                dqn_ref[rows_i, hd] += dq[:, :LANE]
                dqr_ref[rows_i, hd] += dq[:, LANE:]

        def off_diagonal(i, carry):
            tile(i, False)
            return carry

        tile(j, True)
        lax.fori_loop(j + 1, na, off_diagonal, 0)
        for n, hd in enumerate(heads):
            dkn_ref[:, hd] = (dk_s[n, :, :LANE] * LN2).astype(BF16)
            dkr_ref[:, hd] = (dk_s[n, :, LANE:] * LN2).astype(BF16)
            dv_ref[:, hd] = dv_s[n].astype(BF16)

        @pl.when(j == na - 1)
        def _():
            dqn_out[...] = dqn_ref[...].astype(BF16)
            dqr_out[...] = dqr_ref[...].astype(BF16)

    qspec = pl.BlockSpec((seq, wide), lambda b, h, j: (b, h))
    kspec = pl.BlockSpec((ta, wide), lambda b, h, j: (b * na + j, h))
    krspec = pl.BlockSpec((ta, LANE), lambda b, h, j: (b * na + j, 0))
    return pl.pallas_call(
        body, name="attn_bwd", grid=(nb, nheads // hp, na),
        in_specs=[qspec, qspec, kspec, krspec, kspec, qspec, qspec, qspec],
        out_specs=[qspec, qspec, kspec, kspec, kspec],
        out_shape=[_sds((t_all, hv), BF16)] * 5,
        scratch_shapes=[pltpu.VMEM((hp, ta, ta), F32), pltpu.VMEM((hp, ta, ta), F32), pltpu.VMEM((hp, ta, ta), BF16), pltpu.VMEM((hp, ta, ta), BF16),
                        pltpu.VMEM((hp, ta, 2 * LANE), F32), pltpu.VMEM((hp, ta, LANE), F32), pltpu.VMEM((seq, wide), F32), pltpu.VMEM((seq, wide), F32)],
        compiler_params=_params(3),
    )(qn, qr, kn, kr, v, do, lse, delta)


def _head(o, ub, x1, target, wg, w, lay, tb):
    t_all, d = x1.shape
    hv, qr = lay.hv, lay.qr
    nheads = hv // LANE

    def body(o_ref, ub_ref, x1_ref, tg_ref, wg_ref, gf, loss_ref, dgf_ref, yb_ref, dx2_ref, do_ref, dg_ref, dl_ref, wob, sems):
        @pl.when(pl.program_id(0) == 0)
        def _():
            cps = _fetch(wg_ref, lay, "out_b", wob, sems, 0)
            for cp in cps:
                cp.start()
            loss_ref[...] = jnp.zeros((1, LANE), F32)
            dgf_ref[...] = jnp.zeros((1, d), F32)
            for cp in cps:
                cp.wait()

        ov = o_ref[...].astype(F32)
        g = ub_ref[:, qr:]
        sg = _sigmoid(g)
        silu = g * sg
        yb = (ov * silu).astype(BF16)
        yb_ref[...] = yb
        x2 = x1_ref[...] + _dot(yb, wob[...])
        rinv = _rinv(x2)
        err = x2 * rinv * gf[...] - tg_ref[...]
        loss_ref[...] += (0.5 / d) * jnp.sum(jnp.sum(err * err, axis=1, keepdims=True), axis=0, keepdims=True)
        dx2, dgf = _rms_bwd(x2, rinv, gf[...], err * (1.0 / d))
        dgf_ref[...] += dgf
        dx2_ref[...] = dx2
        dyb = _dot_nt(dx2.astype(BF16), wob[...])
        dov = dyb * silu
        do_ref[...] = dov.astype(BF16)
        dg_ref[...] = (dyb * ov * (sg * (1.0 + g * (1.0 - sg)))).astype(BF16)
        prod = dov * ov
        dl_ref[...] = jnp.concatenate(
            [jnp.broadcast_to(jnp.sum(prod[:, n * LANE:(n + 1) * LANE], axis=1, keepdims=True), (tb, LANE)) for n in range(nheads)], axis=1)

    tok = lambda c: pl.BlockSpec((tb, c), lambda i: (i, 0))
    return pl.pallas_call(
        body, name="head", grid=(t_all // tb,),
        in_specs=[tok(hv), tok(qr + hv), tok(d), tok(d), ANY, _full((1, d))],
        out_specs=[_full((1, LANE)), _full((1, d)), tok(hv), tok(d), tok(hv), tok(hv), tok(hv)],
        out_shape=[_sds((1, LANE)), _sds((1, d)), _sds((t_all, hv), BF16), _sds((t_all, d)), _sds((t_all, hv), BF16), _sds((t_all, hv), BF16),
                   _sds((t_all, hv))],
        scratch_shapes=[pltpu.VMEM((hv, d), BF16), pltpu.SemaphoreType.DMA((N_CHIPS,))],
        compiler_params=_params(1),
    )(o, ub, x1, target, wg, w["final_norm"])


def _fb_bwd(dqn, dqr, dkn, dkr, dv, dgate, ub, ckr, x1, dx2, wg, w, lay, cos_t, sin_t, seq, tb):
    t_all, d = x1.shape
    hv, qr, kvr = lay.hv, lay.qr, lay.kvr
    nheads = hv // LANE
    npos = seq // tb

    def body(dqn_ref, dqr_ref, dkn_ref, dkr_ref, dv_ref, dg_ref, ub_ref, ckr_ref, x1_ref, dx2_ref, wg_ref,
             qn, nb, kvn, wdkv, nkv, cos_ref, sin_ref,
             dx1_ref, dqrp_ref, dqnp_ref, dub_ref, dckr_ref, dqn_g, dnb_g, dkvn_g, dnkv_g, winb, wuk, wuv, wuqn, wuqr, sems):
        @pl.when(pl.program_id(0) == 0)
        def _():
            cps = []
            for n, (key, dst) in enumerate((("in_b", winb), ("uk", wuk), ("uv", wuv), ("uq_n", wuqn), ("uq_r", wuqr))):
                cps += _fetch(wg_ref, lay, key, dst, sems, n * N_CHIPS)
            for cp in cps:
                cp.start()
            dqn_g[...] = jnp.zeros((1, qr), F32)
            dnb_g[...] = jnp.zeros((1, d), F32)
            dkvn_g[...] = jnp.zeros((1, kvr), F32)
            dnkv_g[...] = jnp.zeros((1, d), F32)
            for cp in cps:
                cp.wait()

        cos, sin = cos_ref[...], sin_ref[...]
        xv = x1_ref[...]
        rinv1 = _rinv(xv)
        dqr_v = dqr_ref[...].astype(F32) * ATTN_SCALE
        dqr_pre = (dqr_v * jnp.tile(cos, (1, nheads)) + _swap_halves(dqr_v * jnp.tile(sin, (1, nheads)))).astype(BF16)
        dqrp_ref[...] = dqr_pre
        dqn_pre = (dqn_ref[...].astype(F32) * ATTN_SCALE).astype(BF16)
        dqnp_ref[...] = dqn_pre
        dcq = _dot_nt(dqn_pre, wuqn[...]) + _dot_nt(dqr_pre, wuqr[...])
        cq_pre = ub_ref[:, :qr]
        dcq_pre, g1 = _rms_bwd(cq_pre, _rinv(cq_pre), qn[...], dcq)
        dqn_g[...] += g1
        dub = jnp.concatenate([dcq_pre.astype(BF16), dg_ref[...]], axis=1)
        dub_ref[...] = dub
        dx1_b, g2 = _rms_bwd(xv, rinv1, nb[...], _dot(dub, winb[...]))
        dnb_g[...] += g2
        dkr_all = dkr_ref[...].astype(F32)
        dkr_sum = dkr_all[:, :LANE]
        for n in range(1, nheads):
            dkr_sum = dkr_sum + dkr_all[:, n * LANE:(n + 1) * LANE]
        dckr_rope = dkr_sum * cos + _swap_halves(dkr_sum * sin)
        dckv = _dot_nt(dkn_ref[...].astype(BF16), wuk[...]) + _dot_nt(dv_ref[...].astype(BF16), wuv[...])
        ckv_pre = ckr_ref[:, :kvr]
        dckv_pre, g3 = _rms_bwd(ckv_pre, _rinv(ckv_pre), kvn[...], dckv)
        dkvn_g[...] += g3
        dckr = jnp.concatenate([dckv_pre, dckr_rope], axis=1).astype(BF16)
        dckr_ref[...] = dckr
        dx1_kv, g4 = _rms_bwd(xv, rinv1, nkv[...], _dot_nt(dckr, wdkv[...]))
        dnkv_g[...] += g4
        dx1_ref[...] = dx2_ref[...] + dx1_b + dx1_kv

    tok = lambda c: pl.BlockSpec((tb, c), lambda i: (i, 0))
    pos = pl.BlockSpec((tb, LANE), lambda i: (i % npos, 0))
    consts = [w["q_norm"], w["norm_b"], w["kv_norm"], w["w_dkv_p"], w["norm_kv"]]
    return pl.pallas_call(
        body, name="fb_bwd", grid=(t_all // tb,),
        in_specs=[tok(hv)] * 6 + [tok(qr + hv), tok(kvr + LANE), tok(d), tok(d), ANY] + [_full(c.shape) for c in consts] + [pos, pos],
        out_specs=[tok(d), tok(hv), tok(hv), tok(qr + hv), tok(kvr + LANE), _full((1, qr)), _full((1, d)), _full((1, kvr)), _full((1, d))],
        out_shape=[_sds((t_all, d)), _sds((t_all, hv), BF16), _sds((t_all, hv), BF16), _sds((t_all, qr + hv), BF16), _sds((t_all, kvr + LANE), BF16),
                   _sds((1, qr)), _sds((1, d)), _sds((1, kvr)), _sds((1, d))],
        scratch_shapes=[pltpu.VMEM((qr + hv, d), BF16), pltpu.VMEM((kvr, d), BF16), pltpu.VMEM((kvr, d), BF16), pltpu.VMEM((qr, d), BF16),
                        pltpu.VMEM((qr, d), BF16), pltpu.SemaphoreType.DMA((5 * N_CHIPS,))],
        compiler_params=_params(1),
    )(dqn, dqr, dkn, dkr, dv, dgate, ub, ckr, x1, dx2, wg, *consts, cos_t, sin_t)


def _fa_bwd(dx1, x, u, xb, hs, wg, g16, g32, w, lay, seq, tb):
    t_all, d = x.shape
    dr = lay.dr
    nblocks = w["w_rg"].shape[0]
    nblk = seq // tb
    nt = tb // SUBLANE
    per8 = tb // 8

    def body(dx1_ref, x_ref, u_ref, xb_ref, hs_ref, hh_ref, wg_ref, g16_ref, g32_ref, na, cw, wrg, brg, wig, big, lam,
             gx_ref, du_ref, dna_g, dcw_g, dcb_g, dbrg_g, dbig_g, dlam_g, dwrg_g, dwig_g, got_ref, sib_ref, own_ref,
             wint, wout, hpad, a_s, d_s, g_s, dxpad, carry, sems, send_sems, recv_sems, local_sem):
        b, jj = pl.program_id(0), pl.program_id(1)
        first_block = jj == nblk - 1
        scatter = _ScatterDirect(g16_ref, g32_ref, got_ref, sib_ref, own_ref, send_sems, recv_sems, local_sem, lay, G_GROUPS["early"])

        @pl.when((b == 0) & (jj == 0))
        def _():
            scatter.start()
            cps = _fetch(wg_ref, lay, "in_a", wint, sems, 0) + _fetch(wg_ref, lay, "out_a", wout, sems, N_CHIPS)
            for cp in cps:
                cp.start()
            dna_g[...] = jnp.zeros((1, d), F32)
            dcw_g[...] = jnp.zeros((4, dr), F32)
            dcb_g[...] = jnp.zeros((1, dr), F32)
            dbrg_g[...] = jnp.zeros((1, dr), F32)
            dbig_g[...] = jnp.zeros((1, dr), F32)
            dlam_g[...] = jnp.zeros((1, dr), F32)
            dwrg_g[...] = jnp.zeros((nblocks, LANE, LANE), F32)
            dwig_g[...] = jnp.zeros((nblocks, LANE, LANE), F32)
            for cp in cps:
                cp.wait()

        @pl.when(jj == 0)
        def _():
            dxpad[pl.ds(tb, 8), :] = jnp.zeros((8, dr), F32)
            carry[...] = jnp.zeros((8, dr), F32)

        keep = jnp.where(first_block, 0.0, 1.0)
        dx1v = dx1_ref[...]
        gate = u_ref[:, dr:]
        xpre = u_ref[:, :dr]
        hpad[pl.ds(0, 8), :] = hh_ref[...] * keep
        hpad[pl.ds(8, tb), :] = hs_ref[...]
        xb = xb_ref[...]
        xbb = xb.astype(BF16)
        r, i = _gates(xb, wrg, brg[...], wig, big[...], nblocks)
        sp = _softplus(-lam[...])
        log_a = -LRU_C * r * sp
        a, a2, nem = _decay(log_a)
        mult = jnp.sqrt(nem)
        sg = _sigmoid(gate)
        dy = _dot_nt(dx1v.astype(BF16), wout[...])
        hsv = hs_ref[...]
        dgate = dy * hsv * (sg * (1.0 + gate * (1.0 - sg)))
        a_s[...] = a
        d_s[...] = dy * (gate * sg)
        row = lax.broadcasted_iota(jnp.int32, (8, dr), 0)

        def step(k, c):
            r0 = pl.multiple_of((nt - 1 - k) * 8, 8)
            av = a_s[pl.ds(r0, 8), :]
            dv = d_s[pl.ds(r0, 8), :]
            qv = av * dv
            for s in (1, 2, 4):
                m = row < 8 - s
                a_sh = jnp.where(m, pltpu.roll(av, 8 - s, 0), 1.0)
                q_sh = jnp.where(m, pltpu.roll(qv, 8 - s, 0), 0.0)
                qv = qv + av * q_sh
                av = av * a_sh
            qv = qv + av * c
            g_s[pl.ds(r0, 8), :] = dv + jnp.where(row < 7, pltpu.roll(qv, 7, 0), c)
            return jnp.broadcast_to(qv[0:1, :], qv.shape)

        carry[...] = lax.fori_loop(0, nt, step, carry[...])
        g = g_s[...]
        ix = i * xb
        dlog_a = g * (hpad[pl.ds(7, tb), :] * a - ix * (a2 * lax.rsqrt(nem)))
        dix = g * mult
        dlam_g[...] += -jax.nn.sigmoid(-lam[...]) * jnp.sum(dlog_a * (-LRU_C * r), axis=0, keepdims=True)
        drg = dlog_a * (-LRU_C * sp) * r * (1.0 - r)
        dig = dix * xb * i * (1.0 - i)
        dbrg_g[...] += jnp.sum(drg, axis=0, keepdims=True)
        dbig_g[...] += jnp.sum(dig, axis=0, keepdims=True)
        drgb, digb = drg.astype(BF16), dig.astype(BF16)
        back = []
        for n in range(nblocks):
            cols = slice(n * LANE, (n + 1) * LANE)
            dwrg_g[n] += _dot_tn(xbb[:, cols], drgb[:, cols])
            dwig_g[n] += _dot_tn(xbb[:, cols], digb[:, cols])
            back.append(_dot_nt(drgb[:, cols], wrg[n]) + _dot_nt(digb[:, cols], wig[n]))
        dxb = dix * i + jnp.concatenate(back, axis=1)
        dcb_g[...] += jnp.sum(dxb, axis=0, keepdims=True)
        dxpad[pl.ds(0, tb), :] = dxb
        later = [dxb, dxpad[pl.ds(1, tb), :], dxpad[pl.ds(2, tb), :], dxpad[pl.ds(3, tb), :]]
        dxpad[pl.ds(tb, 8), :] = dxb[:8, :]
        dxpre = cw[3:4, :] * later[0] + cw[2:3, :] * later[1] + cw[1:2, :] * later[2] + cw[0:1, :] * later[3]
        for m in range(4):
            dcw_g[3 - m:4 - m, :] += jnp.sum(later[m] * xpre, axis=0, keepdims=True)
        du = jnp.concatenate([dxpre, dgate], axis=1).astype(BF16)
        du_ref[...] = du
        xv = x_ref[...]
        dxa, g1 = _rms_bwd(xv, _rinv(xv), na[...], _dot(du, wint[...]))
        dna_g[...] += g1
        gx_ref[...] = dx1v + dxa

        @pl.when((b == t_all // seq - 1) & (jj == nblk - 1))
        def _():
            scatter.finish()

    blk = lambda b, j: b * nblk + (nblk - 1 - j)
    tok = lambda c: pl.BlockSpec((tb, c), lambda b, j: (blk(b, j), 0))
    halo = pl.BlockSpec((8, dr), lambda b, j: (jnp.maximum(blk(b, j) * per8 - 1, 0), 0))
    consts = [w["norm_a"], w["conv_w"], w["w_rg"], w["b_rg"], w["w_ig"], w["b_ig"], w["lru_lambda"]]
    vec = lambda c: _full((1, c))
    blocks3 = (nblocks, LANE, LANE)
    return pl.pallas_call(
        body, name="fa_bwd", grid=(t_all // seq, nblk),
        in_specs=[tok(d), tok(d), tok(2 * dr), tok(dr), tok(dr), halo, ANY, ANY, ANY] + [_full(c.shape) for c in consts],
        out_specs=[tok(d), tok(2 * dr), vec(d), _full((4, dr)), vec(dr), vec(dr), vec(dr), vec(dr), _full(blocks3), _full(blocks3), ANY, ANY, ANY],
        out_shape=[_sds((t_all, d)), _sds((t_all, 2 * dr), BF16), _sds((1, d)), _sds((4, dr)), _sds((1, dr)), _sds((1, dr)), _sds((1, dr)),
                   _sds((1, dr)), _sds(blocks3), _sds(blocks3)] + _scatter_direct_shapes(lay, "early"),
        scratch_shapes=[pltpu.VMEM((2 * dr, d), BF16), pltpu.VMEM((dr, d), BF16), pltpu.VMEM((tb + 8, dr), F32),
                        pltpu.VMEM((tb, dr), F32), pltpu.VMEM((tb, dr), F32), pltpu.VMEM((tb, dr), F32), pltpu.VMEM((tb + 8, dr), F32),
                        pltpu.VMEM((8, dr), F32), pltpu.SemaphoreType.DMA((2 * N_CHIPS,))] + SCATTER_DIRECT_SEMS,
        compiler_params=_params(2),
    )(dx1, x, u, xb, hs, hs, wg, g16, g32, *consts)


def _mm_into(gbufs, a, bs, offs, name, bt):
    t_all, m = a.shape
    n = bs[0].shape[1]
    nb = len(bs)
    nsplit = nb if nb > 1 else 2 if m >= 1024 and (m // 2) % LANE == 0 else 1
    mh = m if nb > 1 else m // nsplit
    starts = list(offs) if nb > 1 else [offs[0] + h * mh for h in range(nsplit)]
    nt = t_all // bt
    nbuf = len(gbufs)
    twin = nbuf == 2

    def body(a_ref, *refs):
        b_refs, outs, acc, sems = refs[:nb], refs[nb + nbuf:nb + 2 * nbuf], refs[nb + 2 * nbuf], refs[-1]
        acc16 = refs[nb + 2 * nbuf + 1] if twin else None
        part, t = pl.program_id(0), pl.program_id(1)

        def out_copies(h):
            dst = pl.ds(starts[h], mh)
            copies = [pltpu.make_async_copy(acc.at[h], outs[0].at[dst, :], sems.at[0, h])]
            if twin:
                copies.append(pltpu.make_async_copy(acc16.at[h], outs[1].at[dst, :], sems.at[1, h]))
            return copies

        for h in range(nsplit):
            @pl.when(part == h)
            def _():
                prod = _dot_tn(a_ref[...].astype(BF16), b_refs[h if nb > 1 else 0][...].astype(BF16))

                @pl.when(t == 0)
                def _():
                    acc[h] = prod

                @pl.when(t > 0)
                def _():
                    acc[h] += prod

                @pl.when(t == nt - 1)
                def _():
                    if twin:
                        acc16[h] = acc[h].astype(BF16)
                    for cp in out_copies(h):
                        cp.start()

        @pl.when((part == nsplit - 1) & (t == nt - 1))
        def _():
            for h in range(nsplit):
                for cp in out_copies(h):
                    cp.wait()

    if nb > 1:
        a_spec = pl.BlockSpec((bt, mh), lambda h, t: (t, 0))
        b_specs = [pl.BlockSpec((bt, n), lambda h, t, k=k: (jnp.where(h == k, t, 0), 0)) for k in range(nb)]
    else:
        a_spec = pl.BlockSpec((bt, mh), lambda h, t: (t, h))
        b_specs = [pl.BlockSpec((bt, n), lambda h, t: (t, 0))]
    scratch = [pltpu.VMEM((nsplit, mh, n), F32)] + ([pltpu.VMEM((nsplit, mh, n), BF16)] if twin else []) + [pltpu.SemaphoreType.DMA((2, nsplit))]
    return pl.pallas_call(
        body, name=name, grid=(nsplit, nt),
        in_specs=[a_spec] + b_specs + [ANY] * nbuf,
        out_specs=[ANY] * nbuf, out_shape=[_sds(g.shape, g.dtype) for g in gbufs], input_output_aliases={1 + nb + k: k for k in range(nbuf)},
        scratch_shapes=scratch, compiler_params=_params(2),
    )(a, *bs, *gbufs)


def _dw_in_a_exchange(du, h, rest, lay, bt):
    t_all, d = h.shape
    half = d // 2
    rows, rest_rows, c_rows = lay.rows["in_a"], lay.rows["rest"], lay.c_rows["late"]
    c_in, c_rest = lay.c_off["in_a"], lay.c_off["rest"]
    nt = t_all // bt
    xi, yi, _ = _place()
    order = jnp.stack([2 * (1 - xi) + yi, 2 * xi + (1 - yi), 2 * (1 - xi) + (1 - yi), 2 * xi + yi]).astype(jnp.int32)

    def body(order_ref, a_ref, b_ref, rest_ref, got_ref, own_ref, acc, sibbuf, part16, restv, rest_sib, rest_p, rest16, own_v, own_r,
             d2d_send, d2d_recv, ici_send, ici_recv, local_sems):
        x, y, c = _place()
        chips = [(1 - x, y), (x, 1 - y), (1 - x, 1 - y)]
        g, t = pl.program_id(0), pl.program_id(1)
        their_cols = pl.ds(pl.multiple_of((1 - c) * half, LANE), half)

        def my_half(v):
            return jnp.where(c == 0, v[:, :half], v[:, half:])

        def d2d(src, dst, k):
            return pltpu.make_async_remote_copy(src_ref=src, dst_ref=dst, send_sem=d2d_send.at[k], recv_sem=d2d_recv.at[k],
                                                device_id=(x, y, 1 - c), device_id_type=MESH)

        def group_swap(gg):
            return d2d(acc.at[gg % 2, :, their_cols], sibbuf.at[gg], gg)

        def rest_swap():
            return d2d(restv.at[:, their_cols], rest_sib, 4)

        def to_chip(k, src, off, nrows):
            px, py = chips[k]
            return pltpu.make_async_remote_copy(src_ref=src, dst_ref=got_ref.at[k, pl.ds(off, nrows), :], send_sem=ici_send.at[k],
                                                recv_sem=ici_recv.at[k], device_id=(px, py, c), device_id_type=MESH)

        def own_copy(src, off, nrows, k):
            return pltpu.make_async_copy(src, own_ref.at[pl.ds(off, nrows), :], local_sems.at[k])

        def finish_group(gg):
            group_swap(gg).wait()
            part = my_half(acc[gg % 2]) + sibbuf[gg]
            if gg < 3:
                part16[gg] = part.astype(BF16)
                to_chip(gg, part16.at[gg], c_in, rows).start()
            else:
                own_v[...] = part
                own_copy(own_v, c_in, rows, 1).start()

        @pl.when((g == 0) & (t == 0))
        def _():
            load = pltpu.make_async_copy(rest_ref, restv, local_sems.at[0])
            load.start()
            load.wait()
            rest_swap().start()

        prod = _dot_tn(a_ref[...], b_ref[...])
        for gg in range(4):
            @pl.when((g == gg) & (t == 0))
            def _():
                acc[gg % 2] = prod

            @pl.when((g == gg) & (t > 0))
            def _():
                acc[gg % 2] += prod

            @pl.when((g == gg) & (t == nt - 1))
            def _():
                group_swap(gg).start()
                if gg == 0:
                    rest_swap().wait()
                    rest_p[...] = my_half(restv[...]) + rest_sib[...]
                    for k in range(3):
                        chip_rows = pl.ds(pl.multiple_of(order_ref[k] * rest_rows, SUBLANE), rest_rows)
                        rest16[k] = rest_p[chip_rows, :].astype(BF16)
                        to_chip(k, rest16.at[k], c_rest, rest_rows).start()
                    own_r[...] = rest_p[pl.ds(pl.multiple_of(order_ref[3] * rest_rows, SUBLANE), rest_rows), :]
                    own_copy(own_r, c_rest, rest_rows, 2).start()
                else:
                    finish_group(gg - 1)
                if gg == 3:
                    finish_group(3)
                    for k, (px, py) in enumerate(chips):
                        pltpu.make_async_remote_copy(src_ref=got_ref.at[k], dst_ref=got_ref.at[k], send_sem=ici_send.at[k], recv_sem=ici_recv.at[k],
                                                     device_id=(px, py, c), device_id_type=MESH).wait()
                    own_copy(own_v, c_in, rows, 1).wait()
                    own_copy(own_r, c_rest, rest_rows, 2).wait()

    return pl.pallas_call(
        body, name="dw_in_a",
        grid_spec=pltpu.PrefetchScalarGridSpec(
            num_scalar_prefetch=1, grid=(N_CHIPS, nt),
            in_specs=[pl.BlockSpec((bt, rows), lambda g, t, order: (t, order[g])), pl.BlockSpec((bt, d), lambda g, t, order: (t, 0)), ANY],
            out_specs=[ANY, ANY],
            scratch_shapes=[pltpu.VMEM((2, rows, d), F32), pltpu.VMEM((N_CHIPS, rows, half), F32), pltpu.VMEM((3, rows, half), BF16),
                            pltpu.VMEM((N_CHIPS * rest_rows, d), F32), pltpu.VMEM((N_CHIPS * rest_rows, half), F32),
                            pltpu.VMEM((N_CHIPS * rest_rows, half), F32), pltpu.VMEM((3, rest_rows, half), BF16),
                            pltpu.VMEM((rows, half), F32), pltpu.VMEM((rest_rows, half), F32),
                            pltpu.SemaphoreType.DMA((5,)), pltpu.SemaphoreType.DMA((5,)), pltpu.SemaphoreType.DMA((3,)), pltpu.SemaphoreType.DMA((3,)),
                            pltpu.SemaphoreType.DMA((3,))]),
        out_shape=_scatter_shapes(lay, "late", half), compiler_params=_params(2),
    )(order, du, h, rest)


def _mm_tn(a, b, name, bt):
    t_all, m = a.shape
    n = b.shape[1]

    def body(a_ref, b_ref, o_ref):
        @pl.when(pl.program_id(0) == 0)
        def _():
            o_ref[...] = jnp.zeros((m, n), F32)

        o_ref[...] += _dot_tn(a_ref[...].astype(BF16), b_ref[...].astype(BF16))

    return pl.pallas_call(
        body, name=name, grid=(t_all // bt,),
        in_specs=[pl.BlockSpec((bt, m), lambda t: (t, 0)), pl.BlockSpec((bt, n), lambda t: (t, 0))],
        out_specs=_full((m, n)), out_shape=_sds((m, n)),
        compiler_params=_params(1),
    )(a, b)


class _Gather8:
    def __init__(self, x_ref, out_ref, send_sems, recv_sems, local_sem):
        x, y, c = _place()
        self.c, self.me, self.sibling = c, (x, y, c), (x, y, 1 - c)
        self.chips = [(1 - x, y), (x, 1 - y), (1 - x, 1 - y)]
        self.x_ref, self.out_ref, self.send_sems, self.recv_sems, self.local_sem = x_ref, out_ref, send_sems, recv_sems, local_sem

    def _slot(self, px, py, pc):
        return self.out_ref.at[4 * px + 2 * py + pc]

    def _copy(self, k, blk, to, src=None):
        return pltpu.make_async_remote_copy(
            src_ref=self._slot(*blk) if src is None else src, dst_ref=self._slot(*blk), send_sem=self.send_sems.at[k],
            recv_sem=self.recv_sems.at[k], device_id=to, device_id_type=MESH)

    def _mine(self):
        return pltpu.make_async_copy(self.x_ref, self._slot(*self.me), self.local_sem)

    def _first(self):
        return [self._copy(0, self.me, self.sibling, src=self.x_ref)] + [
            self._copy(1 + j, self.me, (*chip, self.c), src=self.x_ref) for j, chip in enumerate(self.chips)]

    def _passed(self):
        return [self._copy(4 + j, (*chip, self.c), self.sibling) for j, chip in enumerate(self.chips)]

    def start(self):
        self._mine().start()
        for cp in self._first():
            cp.start()

    def forward(self):
        passed = self._passed()
        for j, chip in enumerate(self.chips):
            self._copy(1 + j, (*chip, self.c), self.me).wait_recv()
            passed[j].start()

    def finish(self):
        self._copy(0, self.sibling, self.me).wait_recv()
        for j, chip in enumerate(self.chips):
            self._copy(4 + j, (*chip, 1 - self.c), self.me).wait_recv()
        for cp in self._first() + self._passed():
            cp.wait_send()
        self._mine().wait()


class _Scatter:
    def __init__(self, p16_ref, p32_ref, got_ref, own_ref, send_sems, recv_sems, local_sem, lay, order):
        self.x, self.y, self.c = _place()
        self.chips = [(1 - self.x, self.y), (self.x, 1 - self.y), (1 - self.x, 1 - self.y)]
        self.refs = (p16_ref, p32_ref, got_ref, own_ref, send_sems, recv_sems, local_sem)
        self.lay, self.order = lay, order

    def _rows_of(self, ref, key, chip):
        start = pl.multiple_of(self.lay.g_off[key] + chip * self.lay.rows[key], ROW_ALIGN)
        return ref.at[pl.ds(start, self.lay.rows[key]), :]

    def _compact(self, ref, key):
        return ref.at[pl.ds(self.lay.c_off[key], self.lay.rows[key]), :]

    def start(self):
        p16_ref, p32_ref, got_ref, own_ref, send_sems, recv_sems, local_sem = self.refs
        for key in self.order:
            pltpu.make_async_copy(self._rows_of(p32_ref, key, 2 * self.x + self.y), self._compact(own_ref, key), local_sem).start()
        for k, (px, py) in enumerate(self.chips):
            for key in self.order:
                pltpu.make_async_remote_copy(
                    src_ref=self._rows_of(p16_ref, key, 2 * px + py), dst_ref=self._compact(got_ref.at[k], key), send_sem=send_sems.at[k],
                    recv_sem=recv_sems.at[k], device_id=(px, py, self.c), device_id_type=MESH).start()

    def finish(self):
        _, _, got_ref, own_ref, send_sems, recv_sems, local_sem = self.refs
        for k, (px, py) in enumerate(self.chips):
            pltpu.make_async_remote_copy(src_ref=got_ref.at[k], dst_ref=got_ref.at[k], send_sem=send_sems.at[k], recv_sem=recv_sems.at[k],
                                         device_id=(px, py, self.c), device_id_type=MESH).wait()
        pltpu.make_async_copy(own_ref, own_ref, local_sem).wait()


class _ScatterDirect:
    def __init__(self, g16_ref, g32_ref, got_ref, sib_ref, own_ref, send_sems, recv_sems, local_sem, lay, order):
        self.x, self.y, self.c = _place()
        self.chips = [(1 - self.x, self.y), (self.x, 1 - self.y), (1 - self.x, 1 - self.y)]
        self.refs = (g16_ref, g32_ref, got_ref, sib_ref, own_ref, send_sems, recv_sems, local_sem)
        self.lay, self.order, self.half = lay, order, lay.d // 2

    def _src(self, ref, key, chip, h):
        start = pl.multiple_of(self.lay.g_off[key] + chip * self.lay.rows[key], ROW_ALIGN)
        return ref.at[pl.ds(start, self.lay.rows[key]), pl.ds(pl.multiple_of(h * self.half, LANE), self.half)]

    def _compact(self, ref, key):
        return ref.at[pl.ds(self.lay.c_off[key], self.lay.rows[key]), :]

    def start(self):
        g16_ref, g32_ref, got_ref, sib_ref, own_ref, send_sems, recv_sems, local_sem = self.refs
        x, y, c = self.x, self.y, self.c
        for key in self.order:
            pltpu.make_async_copy(self._src(g32_ref, key, 2 * x + y, c), self._compact(own_ref, key), local_sem).start()
            pltpu.make_async_remote_copy(
                src_ref=self._src(g32_ref, key, 2 * x + y, 1 - c), dst_ref=self._compact(sib_ref, key), send_sem=send_sems.at[6],
                recv_sem=recv_sems.at[6], device_id=(x, y, 1 - c), device_id_type=MESH).start()
        for k, (px, py) in enumerate(self.chips):
            for h in range(2):
                for key in self.order:
                    pltpu.make_async_remote_copy(
                        src_ref=self._src(g16_ref, key, 2 * px + py, h), dst_ref=self._compact(got_ref.at[2 * k + c], key),
                        send_sem=send_sems.at[2 * k + h], recv_sem=recv_sems.at[2 * k + c], device_id=(px, py, h), device_id_type=MESH).start()

    def finish(self):
        _, _, got_ref, sib_ref, own_ref, send_sems, recv_sems, local_sem = self.refs
        x, y, c = self.x, self.y, self.c
        for k, (px, py) in enumerate(self.chips):
            for h in range(2):
                whole = pltpu.make_async_remote_copy(src_ref=got_ref.at[2 * k + h], dst_ref=got_ref.at[2 * k + h], send_sem=send_sems.at[2 * k + h],
                                                     recv_sem=recv_sems.at[2 * k + h], device_id=(px, py, h), device_id_type=MESH)
                whole.wait_send()
                whole.wait_recv()
        pltpu.make_async_remote_copy(src_ref=sib_ref, dst_ref=sib_ref, send_sem=send_sems.at[6], recv_sem=recv_sems.at[6],
                                     device_id=(x, y, 1 - c), device_id_type=MESH).wait()
        pltpu.make_async_copy(own_ref, own_ref, local_sem).wait()


def _scatter_direct_shapes(lay, group):
    rows, half = lay.c_rows[group], lay.d // 2
    return [_sds((6, rows, half), BF16), _sds((rows, half), F32), _sds((rows, half), F32)]


SCATTER_DIRECT_SEMS = [pltpu.SemaphoreType.DMA((7,)), pltpu.SemaphoreType.DMA((7,)), pltpu.SemaphoreType.DMA]
SCATTER_SEMS = [pltpu.SemaphoreType.DMA((3,)), pltpu.SemaphoreType.DMA((3,)), pltpu.SemaphoreType.DMA]
GATHER_SEMS = [pltpu.SemaphoreType.DMA((7,)), pltpu.SemaphoreType.DMA((7,)), pltpu.SemaphoreType.DMA]


def _all_gather8(blocks, name):
    nb = len(blocks)

    def body(*refs):
        x_refs, out_refs = refs[:nb], refs[nb:2 * nb]
        send_sems, recv_sems, local_sems = refs[2 * nb:]
        gathers = [_Gather8(x_refs[n], out_refs[n], send_sems.at[n], recv_sems.at[n], local_sems.at[n]) for n in range(nb)]
        for g in gathers:
            g.start()
        for g in gathers:
            g.forward()
        for g in gathers:
            g.finish()

    return pl.pallas_call(
        body, name=name, out_shape=[_sds((8,) + b.shape, b.dtype) for b in blocks], in_specs=[ANY] * nb, out_specs=[ANY] * nb,
        scratch_shapes=[pltpu.SemaphoreType.DMA((nb, 7)), pltpu.SemaphoreType.DMA((nb, 7)), pltpu.SemaphoreType.DMA((nb,))],
    )(*blocks)


def _swap_sibling(srcs, name, half_cols=False):
    n = len(srcs)
    halves = [s.shape[1] // 2 if half_cols else s.shape[1] for s in srcs]

    def body(*refs):
        src_refs, out_refs, send_sems, recv_sems = refs[:n], refs[n:2 * n], refs[2 * n], refs[2 * n + 1]
        x, y, c = _place()
        copies = []
        for k in range(n):
            part = src_refs[k].at[:, pl.ds(pl.multiple_of((1 - c) * halves[k], LANE), halves[k])] if half_cols else src_refs[k]
            copies.append(pltpu.make_async_remote_copy(src_ref=part, dst_ref=out_refs[k], send_sem=send_sems.at[k], recv_sem=recv_sems.at[k],
                                                       device_id=(x, y, 1 - c), device_id_type=MESH))
        for cp in copies:
            cp.start()
        for cp in copies:
            cp.wait()

    return pl.pallas_call(
        body, name=name, out_shape=[_sds((s.shape[0], h), s.dtype) for s, h in zip(srcs, halves)], in_specs=[ANY] * n, out_specs=[ANY] * n,
        scratch_shapes=[pltpu.SemaphoreType.DMA((n,)), pltpu.SemaphoreType.DMA((n,))],
    )(*srcs)


def _return_and_gather(mines, rep_block):
    n = len(mines)

    def body(*refs):
        src_refs, rep_ref, out_refs, rep_out = refs[:n], refs[n], refs[n + 1:2 * n + 1], refs[2 * n + 1]
        send_sems, recv_sems, g_send, g_recv, g_local = refs[2 * n + 2:]
        x, y, c = _place()
        copies = [pltpu.make_async_remote_copy(src_ref=src_refs[k], dst_ref=out_refs[k], send_sem=send_sems.at[k], recv_sem=recv_sems.at[k],
                                               device_id=(x, y, 1 - c), device_id_type=MESH) for k in range(n)]
        gather = _Gather8(rep_ref, rep_out, g_send, g_recv, g_local)
        for cp in copies:
            cp.start()
        gather.start()
        gather.forward()
        gather.finish()
        for cp in copies:
            cp.wait()

    return pl.pallas_call(
        body, name="rs_return", out_shape=[_sds(m.shape, m.dtype) for m in mines] + [_sds((8,) + rep_block.shape, rep_block.dtype)],
        in_specs=[ANY] * (n + 1), out_specs=[ANY] * (n + 1),
        scratch_shapes=[pltpu.SemaphoreType.DMA((n,)), pltpu.SemaphoreType.DMA((n,))] + GATHER_SEMS,
    )(*mines, rep_block)


def _scatter_shapes(lay, group, half):
    return [_sds((3, lay.c_rows[group], half), BF16), _sds((lay.c_rows[group], half), F32)]


def _scatter_chips(part16, part32, lay, group, name):
    def body(p16_ref, p32_ref, got_ref, own_ref, send_sems, recv_sems, local_sem):
        sc = _Scatter(p16_ref, p32_ref, got_ref, own_ref, send_sems, recv_sems, local_sem, lay, G_GROUPS[group])
        sc.start()
        sc.finish()

    return pl.pallas_call(
        body, name=name, out_shape=_scatter_shapes(lay, group, part16.shape[1]), in_specs=[ANY, ANY], out_specs=[ANY, ANY],
        scratch_shapes=SCATTER_SEMS,
    )(part16, part32)


def _sum_sibling(gbuf, got, cidx, name):
    rows, d = gbuf.shape
    half = d // 2
    rb = _row_block(rows)

    def body(c_ref, g_ref, r_ref, o32_ref, o16_ref):
        del c_ref
        s = g_ref[...] + r_ref[...]
        o32_ref[...] = s
        o16_ref[...] = s.astype(BF16)

    plain = pl.BlockSpec((rb, half), lambda i, c: (i, 0))
    return pl.pallas_call(
        body, name=name,
        grid_spec=pltpu.PrefetchScalarGridSpec(num_scalar_prefetch=1, grid=(rows // rb,),
                                               in_specs=[pl.BlockSpec((rb, half), lambda i, c: (i, c[0])), plain], out_specs=[plain, plain]),
        out_shape=[_sds((rows, half)), _sds((rows, half), BF16)], compiler_params=_params(1),
    )(cidx, gbuf, got)


def _sum_devices(own, sib, got, name):
    rows, half = own.shape
    rb = _row_block(rows)
    n = got.shape[0]

    def body(a_ref, s_ref, b_ref, o_ref):
        acc = a_ref[...] + s_ref[...]
        for k in range(n):
            acc = acc + b_ref[k].astype(F32)
        o_ref[...] = acc

    spec = pl.BlockSpec((rb, half), lambda i: (i, 0))
    return pl.pallas_call(
        body, name=name, grid=(rows // rb,), in_specs=[spec, spec, pl.BlockSpec((n, rb, half), lambda i: (0, i, 0))], out_specs=spec,
        out_shape=_sds((rows, half)), compiler_params=_params(1),
    )(own, sib, got)


def _sum_chips(own, got, name):
    rows, half = own.shape
    rb = _row_block(rows)

    def body(a_ref, b_ref, o_ref):
        o_ref[...] = ((a_ref[...] + b_ref[0].astype(F32)) + b_ref[1].astype(F32)) + b_ref[2].astype(F32)

    spec = pl.BlockSpec((rb, half), lambda i: (i, 0))
    return pl.pallas_call(
        body, name=name, grid=(rows // rb,), in_specs=[spec, pl.BlockSpec((3, rb, half), lambda i: (0, i, 0))], out_specs=spec,
        out_shape=_sds((rows, half)), compiler_params=_params(1),
    )(own, got)


def _adamw(w, g, m, v):
    m = ADAM_B1 * m + (1.0 - ADAM_B1) * g
    v = ADAM_B2 * v + (1.0 - ADAM_B2) * (g * g)
    m_hat = m / (1.0 - ADAM_B1 ** ADAM_STEP)
    v_hat = v / (1.0 - ADAM_B2 ** ADAM_STEP)
    return -ADAM_LR * (m_hat / (jnp.sqrt(v_hat) + ADAM_EPS) + ADAM_WD * w), m, v


def _adamw_rows(name, w, g, m, v):
    _, rows, cols = w.shape
    rb = _row_block(rows, 256)

    def body(w_ref, g_ref, m_ref, v_ref, d_ref, mo_ref, vo_ref):
        d_ref[...], mo_ref[...], vo_ref[...] = _adamw(w_ref[...], g_ref[...], m_ref[...], v_ref[...])

    spec = pl.BlockSpec((1, rb, cols), lambda i: (0, i, 0))
    return pl.pallas_call(
        body, name=name, grid=(rows // rb,), in_specs=[spec] * 4, out_specs=[spec] * 3, out_shape=[_sds(w.shape)] * 3,
        compiler_params=_params(1),
    )(w, g, m, v)


def _adamw_group(ws, gs, ms, vs):
    n = len(ws)

    def body(*refs):
        for k in range(n):
            w_ref, g_ref, m_ref, v_ref = (refs[j * n + k] for j in range(4))
            outs = _adamw(w_ref[...], g_ref[...], m_ref[...], v_ref[...])
            for j in range(3):
                refs[(4 + j) * n + k][...] = outs[j]

    outs = pl.pallas_call(
        body, name="adamw_small", out_shape=[_sds(w.shape) for w in ws] * 3,
        compiler_params=pltpu.CompilerParams(vmem_limit_bytes=VMEM_LIMIT),
    )(*ws, *gs, *ms, *vs)
    return outs[:n], outs[n:2 * n], outs[2 * n:]


def _gather_weights(sh, lay):
    x, y, c = _place()
    d = lay.d
    uq = sh["w_uq"][0].astype(BF16)
    parts = {
        "in_b": sh["w_in_b"][0].T.astype(BF16), "in_a": sh["w_in_a"][0].T.astype(BF16), "out_a": sh["w_out_a"][0].astype(BF16),
        "out_b": sh["w_out_b"][0].astype(BF16), "uk": sh["w_uk"].astype(BF16).reshape(-1, d), "uv": sh["w_uv"].astype(BF16).reshape(-1, d),
        "uq_n": uq[:, :, :QK_NOPE].reshape(-1, d), "uq_r": jnp.pad(uq[:, :, QK_NOPE:], ((0, 0), (0, 0), (0, LANE - QK_ROPE))).reshape(-1, d),
        "dkv": jnp.pad(sh["w_dkv"].astype(BF16), ((0, 0), (0, LANE - QK_ROPE))).reshape(-1, d),
    }
    halves = {}
    for group, order in W_GROUPS.items():
        stack = jnp.concatenate([parts[k] for k in order], axis=0).reshape(2, lay.w_rows[group] // 2, d)
        halves[group] = lax.dynamic_index_in_dim(stack, c, 0, keepdims=False)
    small = jnp.concatenate([sh[k].reshape(-1) for k in SMALL])
    n_small = small.shape[0]
    width = _round_up(n_small, 2 * SUBLANE * LANE) // (2 * SUBLANE)
    small = jnp.pad(small, (0, 2 * SUBLANE * width - n_small)).reshape(2, SUBLANE, width)
    wg, sg = _all_gather8([halves["a"], lax.dynamic_index_in_dim(small, c, 0, keepdims=False)], "ag_weights")
    wg = wg.reshape(N_CHIPS, lay.w_rows["a"], d)
    sg = sg.reshape(N_CHIPS, 2 * SUBLANE * width)
    full, off = {}, 0
    for k in SMALL:
        n = sh[k].size
        piece = sg[:, off:off + n]
        off += n
        if k == "conv_w":
            full[k] = piece.reshape(N_CHIPS, 4, n // 4).transpose(1, 0, 2).reshape(4, n)
        else:
            full[k] = piece.reshape(1, N_CHIPS * n)
    return wg, halves["b"], full


def _chip_split(g, taps=False):
    if taps:
        n = g.shape[1] // N_CHIPS
        return g.reshape(4, N_CHIPS, n).transpose(1, 0, 2).reshape(N_CHIPS, 4 * n)
    return g.reshape(N_CHIPS, -1)


def kernel(x, norm_a, w_in_a, conv_w, conv_b, w_rg, b_rg, w_ig, b_ig, lru_lambda, w_out_a, norm_kv, w_dkv, kv_norm, w_uk, w_uv, norm_b, w_in_b, q_norm, w_uq, w_out_b, final_norm, loss_target, m_norm_a, m_w_in_a, m_conv_w, m_conv_b, m_w_rg, m_b_rg, m_w_ig, m_b_ig, m_lru_lambda, m_w_out_a, m_norm_kv, m_w_dkv, m_kv_norm, m_w_uk, m_w_uv, m_norm_b, m_w_in_b, m_q_norm, m_w_uq, m_w_out_b, m_final_norm, v_norm_a, v_w_in_a, v_conv_w, v_conv_b, v_w_rg, v_b_rg, v_w_ig, v_b_ig, v_lru_lambda, v_w_out_a, v_norm_kv, v_w_dkv, v_kv_norm, v_w_uk, v_w_uv, v_norm_b, v_w_in_b, v_q_norm, v_w_uq, v_w_out_b, v_final_norm):
    given = dict(locals())
    sh = {k: given[k] for k in WEIGHTS}
    xi, yi, ci = _place()
    nb, seq, d = x.shape
    t_all = nb * seq
    tb_a, tb_b, ta, bt = min(TOKENS_A, seq), min(TOKENS_B, seq), min(TOKENS_ATTN, seq), min(TOKENS_MM, t_all)
    dr = conv_b.shape[1] * N_CHIPS
    qr, kvr, nheads = q_norm.shape[1], kv_norm.shape[0], w_uk.shape[1]
    hv = nheads * LANE
    n_small = sum(sh[k].size for k in SMALL)
    n_repl = sum(sh[k].size for k in REPL)
    lay = _Layout(d, dr, qr, kvr, hv, n_small, n_repl)
    half = d // 2

    wga, wb_half, w = _gather_weights(sh, lay)
    w.update({"w_rg": w_rg[0].astype(BF16), "w_ig": w_ig[0].astype(BF16), "norm_kv": norm_kv[None, :],
              "kv_norm": kv_norm[None, :], "final_norm": final_norm[None, :], "norm_b": norm_b, "q_norm": q_norm})
    cos_t, sin_t = _rope_tables(seq)
    cidx = jnp.reshape(ci, (1,)).astype(jnp.int32)

    x0 = x.reshape(t_all, d)
    x1, u, hs, h, y, xb, wgb = _fa_fwd(x0, wga, wb_half, w, lay, seq, min(TOKENS_A_FWD, seq))
    wgb = wgb.reshape(N_CHIPS, lay.w_rows["b"], d)
    w["w_dkv_p"] = wgb[:, lay.w_off["dkv"]:lay.w_off["dkv"] + lay.rows["dkv"], :].reshape(d, kvr + LANE)
    qn, qrp, kn, kr, v, ub, ckr, hb, hk, cq, ckv = _fb_fwd(x1, wgb, w, lay, cos_t, sin_t, seq, tb_b)
    o, lse = _attn_fwd(qn, qrp, kn, kr, v, seq, ta)
    loss, g_final_norm, yb, dx2, do, dgate, delta = _head(o, ub, x1, loss_target.reshape(t_all, d), wgb, w, lay, tb_b)
    dqn, dqr, dkn, dkr, dv = _attn_bwd(qn, qrp, kn, kr, v, do, lse, delta, seq, ta)
    dx1, dqr_pre, dqn_pre, dub, dckr, g_q_norm, g_norm_b, g_kv_norm, g_norm_kv = _fb_bwd(
        dqn, dqr, dkn, dkr, dv, dgate, ub, ckr, x1, dx2, wgb, w, lay, cos_t, sin_t, seq, tb_b)
    loss = lax.psum(loss[0, 0], ("x", "y", "c"))

    gbufs = [lax.empty((lay.g_rows["early"], d), F32), lax.empty((lay.g_rows["early"], d), BF16)]
    for keys, a, bs in ((("in_b",), dub, (hb,)), (("out_a",), y, (dx1,)), (("out_b",), yb, (dx2,)), (("uk", "uv"), ckv, (dkn, dv)),
                        (("uq_n", "uq_r"), cq, (dqn_pre, dqr_pre))):
        gbufs = _mm_into(gbufs, a, bs, [lay.g_off[k] for k in keys], "dw_" + "_".join(keys), bt)
    g_dkv = _mm_tn(hk, dckr, "dw_dkv", bt)
    gx, du, g_norm_a, g_conv_w, g_conv_b, g_b_rg, g_b_ig, g_lam, g_w_rg, g_w_ig, others, sib, own = _fa_bwd(
        dx1, x0, u, xb, hs, wga, gbufs[1], gbufs[0], w, lay, seq, tb_a)
    mine_early = _sum_devices(own, sib, others, "rs_sum_early")

    small = jnp.concatenate([_chip_split(g_norm_a), _chip_split(g_conv_w, taps=True), _chip_split(g_conv_b), _chip_split(g_b_rg),
                             _chip_split(g_b_ig), _chip_split(g_lam)], axis=1)
    small = jnp.pad(small, ((0, 0), (0, lay.small_rows * d - small.shape[1]))).reshape(N_CHIPS, lay.small_rows, d)
    repl_parts = {"w_rg": g_w_rg, "w_ig": g_w_ig, "norm_kv": g_norm_kv, "kv_norm": g_kv_norm, "norm_b": g_norm_b, "q_norm": g_q_norm,
                  "final_norm": g_final_norm}
    repl = jnp.concatenate([repl_parts[k].reshape(-1) for k in REPL])
    repl = jnp.pad(repl, (0, N_CHIPS * lay.repl_rows * d - n_repl)).reshape(N_CHIPS, lay.repl_rows, d)
    pad_rows = lay.rows["rest"] - lay.rows["dkv"] - lay.small_rows - lay.repl_rows
    rest = jnp.concatenate([g_dkv.reshape(N_CHIPS, lay.rows["dkv"], d), small, repl, jnp.zeros((N_CHIPS, pad_rows, d), F32)], axis=1)
    others, own = _dw_in_a_exchange(du, h, rest.reshape(N_CHIPS * lay.rows["rest"], d), lay, bt)
    mine_late = _sum_chips(own, others, "rs_sum_chips_late")

    r0 = lay.c_off["rest"] + lay.rows["dkv"] + lay.small_rows
    theirs_early, theirs_late, rep_all = _return_and_gather([mine_early, mine_late], mine_late[r0:r0 + lay.repl_rows])
    red = {}
    for group, mine, theirs in (("early", mine_early, theirs_early), ("late", mine_late, theirs_late)):
        red[group] = jnp.concatenate([jnp.where(ci == 0, mine, theirs), jnp.where(ci == 0, theirs, mine)], axis=1)
    rep_flat =rep_all.reshape(N_CHIPS, 2, lay.repl_rows, half).transpose(0, 2, 1, 3).reshape(-1)

    def rows(key):
        group = "late" if key in G_GROUPS["late"] else "early"
        return red[group][lay.c_off[key]:lay.c_off[key] + lay.rows[key]]

    grads = {"w_in_b": rows("in_b").T[None], "w_in_a": rows("in_a").T[None], "w_out_a": rows("out_a")[None], "w_out_b": rows("out_b")[None],
             "w_uk": rows("uk").reshape(w_uk.shape), "w_uv": rows("uv").reshape(w_uv.shape)}
    uq_n = rows("uq_n").reshape(qr // N_CHIPS, nheads, LANE)
    uq_r = rows("uq_r").reshape(qr // N_CHIPS, nheads, LANE)[:, :, :QK_ROPE]
    grads["w_uq"] = jnp.concatenate([uq_n, uq_r], axis=2)[None]
    rest_red = rows("rest")
    grads["w_dkv"] = rest_red[:lay.rows["dkv"]].reshape(d // N_CHIPS, kvr + LANE)[:, :kvr + QK_ROPE]
    small_red = rest_red[lay.rows["dkv"]:lay.rows["dkv"] + lay.small_rows].reshape(-1)
    off = 0
    for k in SMALL:
        n = sh[k].size
        grads[k] = small_red[off:off + n].reshape(sh[k].shape)
        off += n
    off = 0
    for k in REPL:
        n = sh[k].size
        grads[k] = rep_flat[off:off + n].reshape(sh[k].shape)
        off += n

    new = {}
    for k in ("w_in_a", "w_in_b", "w_out_a", "w_out_b"):
        view = (lambda a: jnp.swapaxes(a, 1, 2)) if k in TRANSPOSED else (lambda a: a)
        outs = _adamw_rows("adamw_" + k, view(sh[k]), view(grads[k]), view(given["m_" + k]), view(given["v_" + k]))
        new[k] = tuple(view(a) for a in outs)
    rest_names = [k for k in WEIGHTS if k not in new]

    def as2d(k, a):
        return a.T if k in TRANSPOSED else a[None, :] if a.ndim == 1 else a

    ds, ms, vs = _adamw_group([as2d(k, sh[k]) for k in rest_names], [as2d(k, grads[k]) for k in rest_names],
                              [as2d(k, given["m_" + k]) for k in rest_names], [as2d(k, given["v_" + k]) for k in rest_names])
    for n, k in enumerate(rest_names):
        new[k] = tuple((a.T if k in TRANSPOSED else a).reshape(sh[k].shape) for a in (ds[n], ms[n], vs[n]))
    return (loss, gx.reshape(nb, seq, d), *[grads[k] for k in WEIGHTS], *[new[k][0] for k in WEIGHTS], *[new[k][1] for k in WEIGHTS],
            *[new[k][2] for k in WEIGHTS])
```

```python
import jax
import jax.numpy as jnp
from jax import lax
from jax.experimental import pallas as pl
from jax.experimental.pallas import tpu as pltpu

F32, BF16 = jnp.float32, jnp.bfloat16
EPS = 1e-6
LRU_C = 8.0
ROPE_THETA = 10000.0
QK_NOPE, QK_ROPE = 128, 64
ATTN_SCALE = (QK_NOPE + QK_ROPE) ** -0.5
LN2 = 0.6931471805599453
Q_SCALE = ATTN_SCALE / LN2
ATTN_HEADS, ATTN_HEADS_BWD = 4, 2
ATTN_ROWS = 64
LANE = 128
SUBLANE = 8
ROW_ALIGN = 32
VMEM_LIMIT = 60000 * 1024
ADAM_LR, ADAM_B1, ADAM_B2, ADAM_EPS, ADAM_WD, ADAM_STEP = 0.001, 0.9, 0.999, 1e-08, 0.01, 10
MESH = pl.DeviceIdType.MESH
ANY = pl.BlockSpec(memory_space=pl.ANY)
N_CHIPS = 4
TOKENS_A, TOKENS_B, TOKENS_ATTN, TOKENS_MM = 256, 512, 512, 2048
TOKENS_A_FWD = 512

SMALL = ("norm_a", "conv_w", "conv_b", "b_rg", "b_ig", "lru_lambda")
REPL = ("w_rg", "w_ig", "norm_kv", "kv_norm", "norm_b", "q_norm", "final_norm")
TRANSPOSED = ("w_in_b", "w_dkv")
WEIGHTS = ("norm_a", "w_in_a", "conv_w", "conv_b", "w_rg", "b_rg", "w_ig", "b_ig", "lru_lambda", "w_out_a", "norm_kv",
           "w_dkv", "kv_norm", "w_uk", "w_uv", "norm_b", "w_in_b", "q_norm", "w_uq", "w_out_b", "final_norm")
W_GROUPS = {"a": ("in_a", "out_a"), "b": ("in_b", "out_b", "uk", "uv", "uq_n", "uq_r", "dkv")}
G_GROUPS = {"early": ("in_b", "out_a", "out_b", "uk", "uv", "uq_n", "uq_r"), "late": ("in_a", "rest")}


def _sds(shape, dtype=F32):
    return jax.ShapeDtypeStruct(tuple(shape), dtype)


def _params(n_grid):
    return pltpu.CompilerParams(dimension_semantics=("arbitrary",) * n_grid, vmem_limit_bytes=VMEM_LIMIT)


def _full(shape):
    nd = len(shape)
    return pl.BlockSpec(tuple(shape), lambda *g: (0,) * nd)


def _round_up(n, k):
    return -(-n // k) * k


def _row_block(rows, cap=512):
    best = SUBLANE
    for r in range(SUBLANE, min(rows, cap) + 1, SUBLANE):
        if rows % r == 0:
            best = r
    return best


def _place():
    return lax.axis_index("x"), lax.axis_index("y"), lax.axis_index("c")


class _Layout:
    def __init__(self, d, dr, qr, kvr, hv, n_small, n_repl):
        assert hv == d, "the packed rows are D_MODEL wide, which must equal heads * 128"
        self.d, self.dr, self.qr, self.kvr, self.hv = d, dr, qr, kvr, hv
        per_chip = {"in_b": (qr + hv) // N_CHIPS, "in_a": 2 * dr // N_CHIPS, "out_a": dr // N_CHIPS, "out_b": hv // N_CHIPS,
                    "uk": kvr // N_CHIPS, "uv": kvr // N_CHIPS, "uq_n": qr // N_CHIPS, "uq_r": qr // N_CHIPS,
                    "dkv": (d // N_CHIPS) * (kvr + LANE) // d}
        assert all(r % ROW_ALIGN == 0 for r in per_chip.values()), per_chip
        self.small_rows = _round_up(-(-n_small // d) + 1, SUBLANE)
        self.repl_rows = _round_up(-(-n_repl // (N_CHIPS * d)), SUBLANE)
        per_chip["rest"] = _round_up(per_chip["dkv"] + self.small_rows + self.repl_rows, ROW_ALIGN)
        self.rows = per_chip
        self.w_off, self.w_rows = {}, {}
        for group, order in W_GROUPS.items():
            off = 0
            for k in order:
                self.w_off[k] = off
                off += per_chip[k]
            assert off % ROW_ALIGN == 0, (group, off)
            self.w_rows[group] = off
        self.g_off, self.c_off, self.c_rows, self.g_rows = {}, {}, {}, {}
        for group, order in G_GROUPS.items():
            off = 0
            for k in order:
                self.c_off[k] = off
                self.g_off[k] = N_CHIPS * off
                off += per_chip[k]
            self.c_rows[group] = off
            self.g_rows[group] = N_CHIPS * off


def _dot(a, b):
    return jnp.dot(a, b, preferred_element_type=F32)


def _dot_nt(a, b):
    return lax.dot_general(a, b, (((1,), (1,)), ((), ())), preferred_element_type=F32)


def _dot_tn(a, b):
    return lax.dot_general(a, b, (((0,), (0,)), ((), ())), preferred_element_type=F32)


def _rinv(x):
    return lax.rsqrt(jnp.mean(x * x, axis=-1, keepdims=True) + EPS)


def _rms_bwd(x, rinv, g, dy):
    z = dy * g
    dx = rinv * z - x * (rinv * rinv * rinv) * jnp.mean(z * x, axis=-1, keepdims=True)
    dg = jnp.sum(dy * (x * rinv), axis=0, keepdims=True)
    return dx, dg


def _softplus(z):
    return jnp.maximum(z, 0.0) + jnp.log1p(jnp.exp(-jnp.abs(z)))


def _sigmoid(x):
    return 0.5 * jnp.tanh(0.5 * x) + 0.5


def _decay(log_a):
    a = jnp.exp(log_a)
    a2 = a * a
    return a, a2, -jnp.tanh(log_a) * (a2 + 1.0)


def _swap_halves(x):
    w = x.shape[1]
    lane = lax.broadcasted_iota(jnp.int32, x.shape, 1)
    return jnp.where(lane % QK_ROPE < QK_ROPE // 2, pltpu.roll(x, w - QK_ROPE // 2, 1), pltpu.roll(x, QK_ROPE // 2, 1))


def _rope_tables(seq):
    pos = jnp.arange(seq, dtype=F32)
    inv = ROPE_THETA ** (-jnp.arange(0, QK_ROPE, 2, dtype=F32) / QK_ROPE)
    ang = pos[:, None] * inv[None, :]
    cos, sin = jnp.cos(ang), jnp.sin(ang)
    zero = jnp.zeros((seq, LANE - QK_ROPE), F32)
    return jnp.concatenate([cos, cos, zero], 1), jnp.concatenate([-sin, sin, zero], 1)


def _fetch(wg_ref, lay, key, dst, sems, k0):
    rows = lay.rows[key]
    return [pltpu.make_async_copy(wg_ref.at[p, pl.ds(lay.w_off[key], rows), :], dst.at[pl.ds(p * rows, rows), :], sems.at[k0 + p])
            for p in range(N_CHIPS)]


def _gates(xb, wrg_ref, brg, wig_ref, big, nblocks):
    xbb = xb.astype(BF16)
    rg = [_dot(xbb[:, n * LANE:(n + 1) * LANE], wrg_ref[n]) for n in range(nblocks)]
    ig = [_dot(xbb[:, n * LANE:(n + 1) * LANE], wig_ref[n]) for n in range(nblocks)]
    r = _sigmoid(jnp.concatenate(rg, axis=1) + brg)
    i = _sigmoid(jnp.concatenate(ig, axis=1) + big)
    return r, i


def _conv(xpad, cw_ref, cb, tb):
    return (cb + cw_ref[3:4, :] * xpad[pl.ds(8, tb), :] + cw_ref[2:3, :] * xpad[pl.ds(7, tb), :]
            + cw_ref[1:2, :] * xpad[pl.ds(6, tb), :] + cw_ref[0:1, :] * xpad[pl.ds(5, tb), :])


def _fa_fwd(x, wg, wb_half, w, lay, seq, tb):
    t_all, d = x.shape
    dr = lay.dr
    nblocks = w["w_rg"].shape[0]
    nblk = seq // tb
    nt = tb // SUBLANE
    nsteps = (t_all // seq) * nblk

    def body(x_ref, wg_ref, wbh_ref, na, cw, cb, wrg, brg, wig, big, lam, x1_ref, u_ref, hs_ref, h_ref, y_ref, xb_ref, wb_ref,
             wint, wout, xpad, a_s, b_s, carry, sems, send_sems, recv_sems, local_sem):
        step_no = pl.program_id(0) * nblk + pl.program_id(1)
        gather = _Gather8(wbh_ref, wb_ref, send_sems, recv_sems, local_sem)

        @pl.when(step_no == 0)
        def _():
            gather.start()
            cps = _fetch(wg_ref, lay, "in_a", wint, sems, 0) + _fetch(wg_ref, lay, "out_a", wout, sems, N_CHIPS)
            for cp in cps:
                cp.start()
            for cp in cps:
                cp.wait()

        @pl.when(step_no == nsteps // 2)
        def _():
            gather.forward()

        @pl.when(pl.program_id(1) == 0)
        def _():
            xpad[pl.ds(0, 8), :] = jnp.zeros((8, dr), F32)
            carry[...] = jnp.zeros((8, dr), F32)

        xv = x_ref[...]
        h = (xv * _rinv(xv) * na[...]).astype(BF16)
        h_ref[...] = h
        u = _dot_nt(h, wint[...])
        u_ref[...] = u
        xpre, gate = u[:, :dr], u[:, dr:]
        xpad[pl.ds(8, tb), :] = xpre
        xb = _conv(xpad, cw, cb[...], tb)
        xb_ref[...] = xb
        xpad[pl.ds(0, 8), :] = xpre[tb - 8:, :]
        r, i = _gates(xb, wrg, brg[...], wig, big[...], nblocks)
        log_a = -LRU_C * r * _softplus(-lam[...])
        a, _, nem = _decay(log_a)
        a_s[...] = a
        b_s[...] = jnp.sqrt(nem) * (i * xb)
        row = lax.broadcasted_iota(jnp.int32, (8, dr), 0)

        def step(t, c):
            r0 = pl.multiple_of(t * 8, 8)
            a = a_s[pl.ds(r0, 8), :]
            b = b_s[pl.ds(r0, 8), :]
            for s in (1, 2, 4):
                m = row >= s
                a_sh = jnp.where(m, pltpu.roll(a, s, 0), 1.0)
                b_sh = jnp.where(m, pltpu.roll(b, s, 0), 0.0)
                b = a * b_sh + b
                a = a * a_sh
            hh = b + a * c
            hs_ref[pl.ds(r0, 8), :] = hh
            return jnp.broadcast_to(hh[7:8, :], hh.shape)

        carry[...] = lax.fori_loop(0, nt, step, carry[...])
        y = (hs_ref[...] * (gate * _sigmoid(gate))).astype(BF16)
        y_ref[...] = y
        x1_ref[...] = xv + _dot(y, wout[...])

        @pl.when(step_no == nsteps - 1)
        def _():
            gather.finish()

    tok = lambda c: pl.BlockSpec((tb, c), lambda b, j: (b * nblk + j, 0))
    consts = [w["norm_a"], w["conv_w"], w["conv_b"], w["w_rg"], w["b_rg"], w["w_ig"], w["b_ig"], w["lru_lambda"]]
    return pl.pallas_call(
        body, name="fa_fwd", grid=(t_all // seq, nblk),
        in_specs=[tok(d), ANY, ANY] + [_full(c.shape) for c in consts],
        out_specs=[tok(d), tok(2 * dr), tok(dr), tok(d), tok(dr), tok(dr), ANY],
        out_shape=[_sds((t_all, d)), _sds((t_all, 2 * dr)), _sds((t_all, dr)), _sds((t_all, d), BF16), _sds((t_all, dr), BF16),
                   _sds((t_all, dr)), _sds((8,) + wb_half.shape, BF16)],
        scratch_shapes=[pltpu.VMEM((2 * dr, d), BF16), pltpu.VMEM((dr, d), BF16), pltpu.VMEM((tb + 8, dr), F32), pltpu.VMEM((tb, dr), F32),
                        pltpu.VMEM((tb, dr), F32), pltpu.VMEM((8, dr), F32), pltpu.SemaphoreType.DMA((2 * N_CHIPS,))] + GATHER_SEMS,
        compiler_params=_params(2),
    )(x, wg, wb_half, *consts)


def _fb_fwd(x1, wg, w, lay, cos_t, sin_t, seq, tb):
    t_all, d = x1.shape
    kvr, qr, hv = lay.kvr, lay.qr, lay.hv
    nheads = hv // LANE
    npos = seq // tb

    def body(x_ref, wg_ref, nkv, nb, wdkv, kvn, qn, cos_ref, sin_ref,
             qn_o, qr_o, kn_o, kr_o, v_o, ub_o, ckr_o, hb_o, hk_o, cq_o, ckv_o, winb, wuk, wuv, wuqn, wuqr, sems):
        @pl.when(pl.program_id(0) == 0)
        def _():
            cps = []
            for n, (key, dst) in enumerate((("in_b", winb), ("uk", wuk), ("uv", wuv), ("uq_n", wuqn), ("uq_r", wuqr))):
                cps += _fetch(wg_ref, lay, key, dst, sems, n * N_CHIPS)
            for cp in cps:
                cp.start()
            for cp in cps:
                cp.wait()

        xv = x_ref[...]
        xh = xv * _rinv(xv)
        hk = (xh * nkv[...]).astype(BF16)
        hb = (xh * nb[...]).astype(BF16)
        hk_o[...] = hk
        hb_o[...] = hb
        cos, sin = cos_ref[...], sin_ref[...]
        ckr = _dot(hk, wdkv[...])
        ckr_o[...] = ckr
        ckv_pre = ckr[:, :kvr]
        ckv = (ckv_pre * _rinv(ckv_pre) * kvn[...]).astype(BF16)
        ckv_o[...] = ckv
        kr = ckr[:, kvr:]
        kr_o[...] = (kr * cos + _swap_halves(kr) * sin).astype(BF16)
        kn_o[...] = _dot(ckv, wuk[...]).astype(BF16)
        v_o[...] = _dot(ckv, wuv[...]).astype(BF16)
        ub = _dot_nt(hb, winb[...])
        ub_o[...] = ub
        cq_pre = ub[:, :qr]
        cq = (cq_pre * _rinv(cq_pre) * qn[...]).astype(BF16)
        cq_o[...] = cq
        qn_o[...] = (_dot(cq, wuqn[...]) * Q_SCALE).astype(BF16)
        qrope = _dot(cq, wuqr[...]) * Q_SCALE
        qr_o[...] = (qrope * jnp.tile(cos, (1, nheads)) + _swap_halves(qrope) * jnp.tile(sin, (1, nheads))).astype(BF16)

    tok = lambda c: pl.BlockSpec((tb, c), lambda i: (i, 0))
    pos = pl.BlockSpec((tb, LANE), lambda i: (i % npos, 0))
    consts = [w["norm_kv"], w["norm_b"], w["w_dkv_p"], w["kv_norm"], w["q_norm"]]
    outs = [(hv, BF16), (hv, BF16), (hv, BF16), (LANE, BF16), (hv, BF16), (qr + hv, F32), (kvr + LANE, F32), (d, BF16), (d, BF16), (qr, BF16), (kvr, BF16)]
    return pl.pallas_call(
        body, name="fb_fwd", grid=(t_all // tb,),
        in_specs=[tok(d), ANY] + [_full(c.shape) for c in consts] + [pos, pos],
        out_specs=[tok(c) for c, _ in outs],
        out_shape=[_sds((t_all, c), dt) for c, dt in outs],
        scratch_shapes=[pltpu.VMEM((qr + hv, d), BF16), pltpu.VMEM((kvr, d), BF16), pltpu.VMEM((kvr, d), BF16), pltpu.VMEM((qr, d), BF16),
                        pltpu.VMEM((qr, d), BF16), pltpu.SemaphoreType.DMA((5 * N_CHIPS,))],
        compiler_params=_params(1),
    )(x1, wg, *consts, cos_t, sin_t)


def _causal_mask(row0, col0, nrows, ncols):
    rows = row0 + lax.broadcasted_iota(jnp.int32, (nrows, ncols), 0)
    cols = col0 + lax.broadcasted_iota(jnp.int32, (nrows, ncols), 1)
    return cols <= rows


def _attn_fwd(qn, qr, kn, kr, v, seq, ta):
    t_all, hv = qn.shape
    nheads, nb, na = hv // LANE, t_all // seq, seq // ta

    reps = ta // LANE
    hp = ATTN_HEADS
    wide = hp * LANE

    def body(qn_ref, qr_ref, kn_ref, kr_ref, v_ref, o_ref, lse_ref, m_s, l_s, acc_s):
        i = pl.program_id(2)
        m_s[...] = jnp.full((ta, wide), -1e30, F32)
        l_s[...] = jnp.zeros((ta, wide), F32)
        acc_s[...] = jnp.zeros((ta, wide), F32)
        heads = [slice(n * LANE, (n + 1) * LANE) for n in range(hp)]
        qs = [jnp.concatenate([qn_ref[:, hd], qr_ref[:, hd]], axis=1) for hd in heads]

        def tile(j, diagonal):
            cols = pl.ds(pl.multiple_of(j * ta, ta), ta)
            k_rope = kr_ref[cols, :]
            for q, hd in zip(qs, heads):
                k = jnp.concatenate([kn_ref[cols, hd], k_rope], axis=1)
                s = _dot_nt(q, k)
                if diagonal:
                    s = jnp.where(_causal_mask(0, 0, ta, ta), s, -1e30)
                m_prev = m_s[:, hd]
                m_new = jnp.maximum(m_prev, jnp.max(s, axis=1, keepdims=True))
                p = jnp.exp2(s - jnp.tile(m_new, (1, reps)))
                alpha = jnp.exp2(m_prev - m_new)
                l_s[:, hd] = alpha * l_s[:, hd] + jnp.sum(p, axis=1, keepdims=True)
                acc_s[:, hd] = alpha * acc_s[:, hd] + _dot(p.astype(BF16), v_ref[cols, hd])
                m_s[:, hd] = m_new

        def off_diagonal(j, carry):
            tile(j, False)
            return carry

        lax.fori_loop(0, i, off_diagonal, 0)
        tile(i, True)
        o_ref[...] = (acc_s[...] / l_s[...]).astype(BF16)
        lse_ref[...] = m_s[...] + jnp.log2(l_s[...])

    qspec = pl.BlockSpec((ta, wide), lambda b, h, i: (b * na + i, h))
    kspec = pl.BlockSpec((seq, wide), lambda b, h, i: (b, h))
    krspec = pl.BlockSpec((seq, LANE), lambda b, h, i: (b, 0))
    return pl.pallas_call(
        body, name="attn_fwd", grid=(nb, nheads // hp, na),
        in_specs=[qspec, qspec, kspec, krspec, kspec],
        out_specs=[qspec, qspec],
        out_shape=[_sds((t_all, hv), BF16), _sds((t_all, hv))],
        scratch_shapes=[pltpu.VMEM((ta, wide), F32)] * 3,
        compiler_params=_params(3),
    )(qn, qr, kn, kr, v)


def _attn_bwd(qn, qr, kn, kr, v, do, lse, delta, seq, ta):
    t_all, hv = qn.shape
    nheads, nb, na = hv // LANE, t_all // seq, seq // ta

    reps = ta // LANE
    nchunks = ta // ATTN_ROWS

    hp = ATTN_HEADS_BWD
    wide = hp * LANE
    heads = [slice(n * LANE, (n + 1) * LANE) for n in range(hp)]

    def body(qn_ref, qr_ref, kn_ref, kr_ref, v_ref, do_ref, lse_ref, dl_ref, dqn_out, dqr_out, dkn_ref, dkr_ref, dv_ref,
             s_s, dp_s, p_s, ds_s, dk_s, dv_s, dqn_ref, dqr_ref):
        j = pl.program_id(2)

        @pl.when(j == 0)
        def _():
            dqn_ref[...] = jnp.zeros((seq, wide), F32)
            dqr_ref[...] = jnp.zeros((seq, wide), F32)

        dk_s[...] = jnp.zeros((hp, ta, 2 * LANE), F32)
        dv_s[...] = jnp.zeros((hp, ta, LANE), F32)
        k_rope = kr_ref[...]
        ks = [jnp.concatenate([kn_ref[:, hd], k_rope], axis=1) for hd in heads]

        def tile(i, diagonal):
            rows_i = pl.ds(pl.multiple_of(i * ta, ta), ta)
            for n, hd in enumerate(heads):
                q = jnp.concatenate([qn_ref[rows_i, hd], qr_ref[rows_i, hd]], axis=1)
                do_b = do_ref[rows_i, hd]
                s_s[n] = _dot_nt(q, ks[n])
                dp_s[n] = _dot_nt(do_b, v_ref[:, hd])
                for c in range(nchunks):
                    rows = pl.ds(c * ATTN_ROWS, ATTN_ROWS)
                    seq_rows = pl.ds(pl.multiple_of(i * ta + c * ATTN_ROWS, ATTN_ROWS), ATTN_ROWS)
                    s = s_s[n, rows, :]
                    if diagonal:
                        s = jnp.where(_causal_mask(c * ATTN_ROWS, 0, ATTN_ROWS, ta), s, -1e30)
                    p = jnp.exp2(s - jnp.tile(lse_ref[seq_rows, hd], (1, reps)))
                    p_s[n, rows, :] = p.astype(BF16)
                    ds_s[n, rows, :] = (p * (dp_s[n, rows, :] - jnp.tile(dl_ref[seq_rows, hd], (1, reps)))).astype(BF16)
                dv_s[n] += _dot_tn(p_s[n], do_b)
                ds = ds_s[n]
                dk_s[n] += _dot_tn(ds, q)
                dq = _dot(ds, ks[n])
                dqn_ref[rows_i, hd] += dq[:, :LANE]
                dqr_ref[rows_i, hd] += dq[:, LANE:]

        def off_diagonal(i, carry):
            tile(i, False)
            return carry

        tile(j, True)
        lax.fori_loop(j + 1, na, off_diagonal, 0)
        for n, hd in enumerate(heads):
            dkn_ref[:, hd] = (dk_s[n, :, :LANE] * LN2).astype(BF16)
            dkr_ref[:, hd] = (dk_s[n, :, LANE:] * LN2).astype(BF16)
            dv_ref[:, hd] = dv_s[n].astype(BF16)

        @pl.when(j == na - 1)
        def _():
            dqn_out[...] = dqn_ref[...].astype(BF16)
            dqr_out[...] = dqr_ref[...].astype(BF16)

    qspec = pl.BlockSpec((seq, wide), lambda b, h, j: (b, h))
    kspec = pl.BlockSpec((ta, wide), lambda b, h, j: (b * na + j, h))
    krspec = pl.BlockSpec((ta, LANE), lambda b, h, j: (b * na + j, 0))
    return pl.pallas_call(
        body, name="attn_bwd", grid=(nb, nheads // hp, na),
        in_specs=[qspec, qspec, kspec, krspec, kspec, qspec, qspec, qspec],
        out_specs=[qspec, qspec, kspec, kspec, kspec],
        out_shape=[_sds((t_all, hv), BF16)] * 5,
        scratch_shapes=[pltpu.VMEM((hp, ta, ta), F32), pltpu.VMEM((hp, ta, ta), F32), pltpu.VMEM((hp, ta, ta), BF16), pltpu.VMEM((hp, ta, ta), BF16),
                        pltpu.VMEM((hp, ta, 2 * LANE), F32), pltpu.VMEM((hp, ta, LANE), F32), pltpu.VMEM((seq, wide), F32), pltpu.VMEM((seq, wide), F32)],
        compiler_params=_params(3),
    )(qn, qr, kn, kr, v, do, lse, delta)


def _head(o, ub, x1, target, wg, w, lay, tb):
    t_all, d = x1.shape
    hv, qr = lay.hv, lay.qr
    nheads = hv // LANE

    def body(o_ref, ub_ref, x1_ref, tg_ref, wg_ref, gf, loss_ref, dgf_ref, yb_ref, dx2_ref, do_ref, dg_ref, dl_ref, wob, sems):
        @pl.when(pl.program_id(0) == 0)
        def _():
            cps = _fetch(wg_ref, lay, "out_b", wob, sems, 0)
            for cp in cps:
                cp.start()
            loss_ref[...] = jnp.zeros((1, LANE), F32)
            dgf_ref[...] = jnp.zeros((1, d), F32)
            for cp in cps:
                cp.wait()

        ov = o_ref[...].astype(F32)
        g = ub_ref[:, qr:]
        sg = _sigmoid(g)
        silu = g * sg
        yb = (ov * silu).astype(BF16)
        yb_ref[...] = yb
        x2 = x1_ref[...] + _dot(yb, wob[...])
        rinv = _rinv(x2)
        err = x2 * rinv * gf[...] - tg_ref[...]
        loss_ref[...] += (0.5 / d) * jnp.sum(jnp.sum(err * err, axis=1, keepdims=True), axis=0, keepdims=True)
        dx2, dgf = _rms_bwd(x2, rinv, gf[...], err * (1.0 / d))
        dgf_ref[...] += dgf
        dx2_ref[...] = dx2
        dyb = _dot_nt(dx2.astype(BF16), wob[...])
        dov = dyb * silu
        do_ref[...] = dov.astype(BF16)
        dg_ref[...] = (dyb * ov * (sg * (1.0 + g * (1.0 - sg)))).astype(BF16)
        prod = dov * ov
        dl_ref[...] = jnp.concatenate(
            [jnp.broadcast_to(jnp.sum(prod[:, n * LANE:(n + 1) * LANE], axis=1, keepdims=True), (tb, LANE)) for n in range(nheads)], axis=1)

    tok = lambda c: pl.BlockSpec((tb, c), lambda i: (i, 0))
    return pl.pallas_call(
        body, name="head", grid=(t_all // tb,),
        in_specs=[tok(hv), tok(qr + hv), tok(d), tok(d), ANY, _full((1, d))],
        out_specs=[_full((1, LANE)), _full((1, d)), tok(hv), tok(d), tok(hv), tok(hv), tok(hv)],
        out_shape=[_sds((1, LANE)), _sds((1, d)), _sds((t_all, hv), BF16), _sds((t_all, d)), _sds((t_all, hv), BF16), _sds((t_all, hv), BF16),
                   _sds((t_all, hv))],
        scratch_shapes=[pltpu.VMEM((hv, d), BF16), pltpu.SemaphoreType.DMA((N_CHIPS,))],
        compiler_params=_params(1),
    )(o, ub, x1, target, wg, w["final_norm"])


def _fb_bwd(dqn, dqr, dkn, dkr, dv, dgate, ub, ckr, x1, dx2, wg, w, lay, cos_t, sin_t, seq, tb):
    t_all, d = x1.shape
    hv, qr, kvr = lay.hv, lay.qr, lay.kvr
    nheads = hv // LANE
    npos = seq // tb

    def body(dqn_ref, dqr_ref, dkn_ref, dkr_ref, dv_ref, dg_ref, ub_ref, ckr_ref, x1_ref, dx2_ref, wg_ref,
             qn, nb, kvn, wdkv, nkv, cos_ref, sin_ref,
             dx1_ref, dqrp_ref, dqnp_ref, dub_ref, dckr_ref, dqn_g, dnb_g, dkvn_g, dnkv_g, winb, wuk, wuv, wuqn, wuqr, sems):
        @pl.when(pl.program_id(0) == 0)
        def _():
            cps = []
            for n, (key, dst) in enumerate((("in_b", winb), ("uk", wuk), ("uv", wuv), ("uq_n", wuqn), ("uq_r", wuqr))):
                cps += _fetch(wg_ref, lay, key, dst, sems, n * N_CHIPS)
            for cp in cps:
                cp.start()
            dqn_g[...] = jnp.zeros((1, qr), F32)
            dnb_g[...] = jnp.zeros((1, d), F32)
            dkvn_g[...] = jnp.zeros((1, kvr), F32)
            dnkv_g[...] = jnp.zeros((1, d), F32)
            for cp in cps:
                cp.wait()

        cos, sin = cos_ref[...], sin_ref[...]
        xv = x1_ref[...]
        rinv1 = _rinv(xv)
        dqr_v = dqr_ref[...].astype(F32) * ATTN_SCALE
        dqr_pre = (dqr_v * jnp.tile(cos, (1, nheads)) + _swap_halves(dqr_v * jnp.tile(sin, (1, nheads)))).astype(BF16)
        dqrp_ref[...] = dqr_pre
        dqn_pre = (dqn_ref[...].astype(F32) * ATTN_SCALE).astype(BF16)
        dqnp_ref[...] = dqn_pre
        dcq = _dot_nt(dqn_pre, wuqn[...]) + _dot_nt(dqr_pre, wuqr[...])
        cq_pre = ub_ref[:, :qr]
        dcq_pre, g1 = _rms_bwd(cq_pre, _rinv(cq_pre), qn[...], dcq)
        dqn_g[...] += g1
        dub = jnp.concatenate([dcq_pre.astype(BF16), dg_ref[...]], axis=1)
        dub_ref[...] = dub
        dx1_b, g2 = _rms_bwd(xv, rinv1, nb[...], _dot(dub, winb[...]))
        dnb_g[...] += g2
        dkr_all = dkr_ref[...].astype(F32)
        dkr_sum = dkr_all[:, :LANE]
        for n in range(1, nheads):
            dkr_sum = dkr_sum + dkr_all[:, n * LANE:(n + 1) * LANE]
        dckr_rope = dkr_sum * cos + _swap_halves(dkr_sum * sin)
        dckv = _dot_nt(dkn_ref[...].astype(BF16), wuk[...]) + _dot_nt(dv_ref[...].astype(BF16), wuv[...])
        ckv_pre = ckr_ref[:, :kvr]
        dckv_pre, g3 = _rms_bwd(ckv_pre, _rinv(ckv_pre), kvn[...], dckv)
        dkvn_g[...] += g3
        dckr = jnp.concatenate([dckv_pre, dckr_rope], axis=1).astype(BF16)
        dckr_ref[...] = dckr
        dx1_kv, g4 = _rms_bwd(xv, rinv1, nkv[...], _dot_nt(dckr, wdkv[...]))
        dnkv_g[...] += g4
        dx1_ref[...] = dx2_ref[...] + dx1_b + dx1_kv

    tok = lambda c: pl.BlockSpec((tb, c), lambda i: (i, 0))
    pos = pl.BlockSpec((tb, LANE), lambda i: (i % npos, 0))
    consts = [w["q_norm"], w["norm_b"], w["kv_norm"], w["w_dkv_p"], w["norm_kv"]]
    return pl.pallas_call(
        body, name="fb_bwd", grid=(t_all // tb,),
        in_specs=[tok(hv)] * 6 + [tok(qr + hv), tok(kvr + LANE), tok(d), tok(d), ANY] + [_full(c.shape) for c in consts] + [pos, pos],
        out_specs=[tok(d), tok(hv), tok(hv), tok(qr + hv), tok(kvr + LANE), _full((1, qr)), _full((1, d)), _full((1, kvr)), _full((1, d))],
        out_shape=[_sds((t_all, d)), _sds((t_all, hv), BF16), _sds((t_all, hv), BF16), _sds((t_all, qr + hv), BF16), _sds((t_all, kvr + LANE), BF16),
                   _sds((1, qr)), _sds((1, d)), _sds((1, kvr)), _sds((1, d))],
        scratch_shapes=[pltpu.VMEM((qr + hv, d), BF16), pltpu.VMEM((kvr, d), BF16), pltpu.VMEM((kvr, d), BF16), pltpu.VMEM((qr, d), BF16),
                        pltpu.VMEM((qr, d), BF16), pltpu.SemaphoreType.DMA((5 * N_CHIPS,))],
        compiler_params=_params(1),
    )(dqn, dqr, dkn, dkr, dv, dgate, ub, ckr, x1, dx2, wg, *consts, cos_t, sin_t)


def _fa_bwd(dx1, x, u, xb, hs, wg, g16, g32, w, lay, seq, tb):
    t_all, d = x.shape
    dr = lay.dr
    nblocks = w["w_rg"].shape[0]
    nblk = seq // tb
    nt = tb // SUBLANE
    per8 = tb // 8

    def body(dx1_ref, x_ref, u_ref, xb_ref, hs_ref, hh_ref, wg_ref, g16_ref, g32_ref, na, cw, wrg, brg, wig, big, lam,
             gx_ref, du_ref, dna_g, dcw_g, dcb_g, dbrg_g, dbig_g, dlam_g, dwrg_g, dwig_g, got_ref, sib_ref, own_ref,
             wint, wout, hpad, a_s, d_s, g_s, dxpad, carry, sems, send_sems, recv_sems, local_sem):
        b, jj = pl.program_id(0), pl.program_id(1)
        first_block = jj == nblk - 1
        scatter = _ScatterDirect(g16_ref, g32_ref, got_ref, sib_ref, own_ref, send_sems, recv_sems, local_sem, lay, G_GROUPS["early"])

        @pl.when((b == 0) & (jj == 0))
        def _():
            scatter.start()
            cps = _fetch(wg_ref, lay, "in_a", wint, sems, 0) + _fetch(wg_ref, lay, "out_a", wout, sems, N_CHIPS)
            for cp in cps:
                cp.start()
            dna_g[...] = jnp.zeros((1, d), F32)
            dcw_g[...] = jnp.zeros((4, dr), F32)
            dcb_g[...] = jnp.zeros((1, dr), F32)
            dbrg_g[...] = jnp.zeros((1, dr), F32)
            dbig_g[...] = jnp.zeros((1, dr), F32)
            dlam_g[...] = jnp.zeros((1, dr), F32)
            dwrg_g[...] = jnp.zeros((nblocks, LANE, LANE), F32)
            dwig_g[...] = jnp.zeros((nblocks, LANE, LANE), F32)
            for cp in cps:
                cp.wait()

        @pl.when(jj == 0)
        def _():
            dxpad[pl.ds(tb, 8), :] = jnp.zeros((8, dr), F32)
            carry[...] = jnp.zeros((8, dr), F32)

        keep = jnp.where(first_block, 0.0, 1.0)
        dx1v = dx1_ref[...]
        gate = u_ref[:, dr:]
        xpre = u_ref[:, :dr]
        hpad[pl.ds(0, 8), :] = hh_ref[...] * keep
        hpad[pl.ds(8, tb), :] = hs_ref[...]
        xb = xb_ref[...]
        xbb = xb.astype(BF16)
        r, i = _gates(xb, wrg, brg[...], wig, big[...], nblocks)
        sp = _softplus(-lam[...])
        log_a = -LRU_C * r * sp
        a, a2, nem = _decay(log_a)
        mult = jnp.sqrt(nem)
        sg = _sigmoid(gate)
        dy = _dot_nt(dx1v.astype(BF16), wout[...])
        hsv = hs_ref[...]
        dgate = dy * hsv * (sg * (1.0 + gate * (1.0 - sg)))
        a_s[...] = a
        d_s[...] = dy * (gate * sg)
        row = lax.broadcasted_iota(jnp.int32, (8, dr), 0)

        def step(k, c):
            r0 = pl.multiple_of((nt - 1 - k) * 8, 8)
            av = a_s[pl.ds(r0, 8), :]
            dv = d_s[pl.ds(r0, 8), :]
            qv = av * dv
            for s in (1, 2, 4):
                m = row < 8 - s
                a_sh = jnp.where(m, pltpu.roll(av, 8 - s, 0), 1.0)
                q_sh = jnp.where(m, pltpu.roll(qv, 8 - s, 0), 0.0)
                qv = qv + av * q_sh
                av = av * a_sh
            qv = qv + av * c
            g_s[pl.ds(r0, 8), :] = dv + jnp.where(row < 7, pltpu.roll(qv, 7, 0), c)
            return jnp.broadcast_to(qv[0:1, :], qv.shape)

        carry[...] = lax.fori_loop(0, nt, step, carry[...])
        g = g_s[...]
        ix = i * xb
        dlog_a = g * (hpad[pl.ds(7, tb), :] * a - ix * (a2 * lax.rsqrt(nem)))
        dix = g * mult
        dlam_g[...] += -jax.nn.sigmoid(-lam[...]) * jnp.sum(dlog_a * (-LRU_C * r), axis=0, keepdims=True)
        drg = dlog_a * (-LRU_C * sp) * r * (1.0 - r)
        dig = dix * xb * i * (1.0 - i)
        dbrg_g[...] += jnp.sum(drg, axis=0, keepdims=True)
        dbig_g[...] += jnp.sum(dig, axis=0, keepdims=True)
        drgb, digb = drg.astype(BF16), dig.astype(BF16)
        back = []
        for n in range(nblocks):
            cols = slice(n * LANE, (n + 1) * LANE)
            dwrg_g[n] += _dot_tn(xbb[:, cols], drgb[:, cols])
            dwig_g[n] += _dot_tn(xbb[:, cols], digb[:, cols])
            back.append(_dot_nt(drgb[:, cols], wrg[n]) + _dot_nt(digb[:, cols], wig[n]))
        dxb = dix * i + jnp.concatenate(back, axis=1)
        dcb_g[...] += jnp.sum(dxb, axis=0, keepdims=True)
        dxpad[pl.ds(0, tb), :] = dxb
        later = [dxb, dxpad[pl.ds(1, tb), :], dxpad[pl.ds(2, tb), :], dxpad[pl.ds(3, tb), :]]
        dxpad[pl.ds(tb, 8), :] = dxb[:8, :]
        dxpre = cw[3:4, :] * later[0] + cw[2:3, :] * later[1] + cw[1:2, :] * later[2] + cw[0:1, :] * later[3]
        for m in range(4):
            dcw_g[3 - m:4 - m, :] += jnp.sum(later[m] * xpre, axis=0, keepdims=True)
        du = jnp.concatenate([dxpre, dgate], axis=1).astype(BF16)
        du_ref[...] = du
        xv = x_ref[...]
        dxa, g1 = _rms_bwd(xv, _rinv(xv), na[...], _dot(du, wint[...]))
        dna_g[...] += g1
        gx_ref[...] = dx1v + dxa

        @pl.when((b == t_all // seq - 1) & (jj == nblk - 1))
        def _():
            scatter.finish()

    blk = lambda b, j: b * nblk + (nblk - 1 - j)
    tok = lambda c: pl.BlockSpec((tb, c), lambda b, j: (blk(b, j), 0))
    halo = pl.BlockSpec((8, dr), lambda b, j: (jnp.maximum(blk(b, j) * per8 - 1, 0), 0))
    consts = [w["norm_a"], w["conv_w"], w["w_rg"], w["b_rg"], w["w_ig"], w["b_ig"], w["lru_lambda"]]
    vec = lambda c: _full((1, c))
    blocks3 = (nblocks, LANE, LANE)
    return pl.pallas_call(
        body, name="fa_bwd", grid=(t_all // seq, nblk),
        in_specs=[tok(d), tok(d), tok(2 * dr), tok(dr), tok(dr), halo, ANY, ANY, ANY] + [_full(c.shape) for c in consts],
        out_specs=[tok(d), tok(2 * dr), vec(d), _full((4, dr)), vec(dr), vec(dr), vec(dr), vec(dr), _full(blocks3), _full(blocks3), ANY, ANY, ANY],
        out_shape=[_sds((t_all, d)), _sds((t_all, 2 * dr), BF16), _sds((1, d)), _sds((4, dr)), _sds((1, dr)), _sds((1, dr)), _sds((1, dr)),
                   _sds((1, dr)), _sds(blocks3), _sds(blocks3)] + _scatter_direct_shapes(lay, "early"),
        scratch_shapes=[pltpu.VMEM((2 * dr, d), BF16), pltpu.VMEM((dr, d), BF16), pltpu.VMEM((tb + 8, dr), F32),
                        pltpu.VMEM((tb, dr), F32), pltpu.VMEM((tb, dr), F32), pltpu.VMEM((tb, dr), F32), pltpu.VMEM((tb + 8, dr), F32),
                        pltpu.VMEM((8, dr), F32), pltpu.SemaphoreType.DMA((2 * N_CHIPS,))] + SCATTER_DIRECT_SEMS,
        compiler_params=_params(2),
    )(dx1, x, u, xb, hs, hs, wg, g16, g32, *consts)


def _mm_into(gbufs, a, bs, offs, name, bt):
    t_all, m = a.shape
    n = bs[0].shape[1]
    nb = len(bs)
    nsplit = nb if nb > 1 else 2 if m >= 1024 and (m // 2) % LANE == 0 else 1
    mh = m if nb > 1 else m // nsplit
    starts = list(offs) if nb > 1 else [offs[0] + h * mh for h in range(nsplit)]
    nt = t_all // bt
    nbuf = len(gbufs)
    twin = nbuf == 2

    def body(a_ref, *refs):
        b_refs, outs, acc, sems = refs[:nb], refs[nb + nbuf:nb + 2 * nbuf], refs[nb + 2 * nbuf], refs[-1]
        acc16 = refs[nb + 2 * nbuf + 1] if twin else None
        part, t = pl.program_id(0), pl.program_id(1)

        def out_copies(h):
            dst = pl.ds(starts[h], mh)
            copies = [pltpu.make_async_copy(acc.at[h], outs[0].at[dst, :], sems.at[0, h])]
            if twin:
                copies.append(pltpu.make_async_copy(acc16.at[h], outs[1].at[dst, :], sems.at[1, h]))
            return copies

        for h in range(nsplit):
            @pl.when(part == h)
            def _():
                prod = _dot_tn(a_ref[...].astype(BF16), b_refs[h if nb > 1 else 0][...].astype(BF16))

                @pl.when(t == 0)
                def _():
                    acc[h] = prod

                @pl.when(t > 0)
                def _():
                    acc[h] += prod

                @pl.when(t == nt - 1)
                def _():
                    if twin:
                        acc16[h] = acc[h].astype(BF16)
                    for cp in out_copies(h):
                        cp.start()

        @pl.when((part == nsplit - 1) & (t == nt - 1))
        def _():
            for h in range(nsplit):
                for cp in out_copies(h):
                    cp.wait()

    if nb > 1:
        a_spec = pl.BlockSpec((bt, mh), lambda h, t: (t, 0))
        b_specs = [pl.BlockSpec((bt, n), lambda h, t, k=k: (jnp.where(h == k, t, 0), 0)) for k in range(nb)]
    else:
        a_spec = pl.BlockSpec((bt, mh), lambda h, t: (t, h))
        b_specs = [pl.BlockSpec((bt, n), lambda h, t: (t, 0))]
    scratch = [pltpu.VMEM((nsplit, mh, n), F32)] + ([pltpu.VMEM((nsplit, mh, n), BF16)] if twin else []) + [pltpu.SemaphoreType.DMA((2, nsplit))]
    return pl.pallas_call(
        body, name=name, grid=(nsplit, nt),
        in_specs=[a_spec] + b_specs + [ANY] * nbuf,
        out_specs=[ANY] * nbuf, out_shape=[_sds(g.shape, g.dtype) for g in gbufs], input_output_aliases={1 + nb + k: k for k in range(nbuf)},
        scratch_shapes=scratch, compiler_params=_params(2),
    )(a, *bs, *gbufs)


def _dw_in_a_exchange(du, h, rest, lay, bt):
    t_all, d = h.shape
    half = d // 2
    rows, rest_rows, c_rows = lay.rows["in_a"], lay.rows["rest"], lay.c_rows["late"]
    c_in, c_rest = lay.c_off["in_a"], lay.c_off["rest"]
    nt = t_all // bt
    xi, yi, _ = _place()
    order = jnp.stack([2 * (1 - xi) + yi, 2 * xi + (1 - yi), 2 * (1 - xi) + (1 - yi), 2 * xi + yi]).astype(jnp.int32)

    def body(order_ref, a_ref, b_ref, rest_ref, got_ref, own_ref, acc, sibbuf, part16, restv, rest_sib, rest_p, rest16, own_v, own_r,
             d2d_send, d2d_recv, ici_send, ici_recv, local_sems):
        x, y, c = _place()
        chips = [(1 - x, y), (x, 1 - y), (1 - x, 1 - y)]
        g, t = pl.program_id(0), pl.program_id(1)
        their_cols = pl.ds(pl.multiple_of((1 - c) * half, LANE), half)

        def my_half(v):
            return jnp.where(c == 0, v[:, :half], v[:, half:])

        def d2d(src, dst, k):
            return pltpu.make_async_remote_copy(src_ref=src, dst_ref=dst, send_sem=d2d_send.at[k], recv_sem=d2d_recv.at[k],
                                                device_id=(x, y, 1 - c), device_id_type=MESH)

        def group_swap(gg):
            return d2d(acc.at[gg % 2, :, their_cols], sibbuf.at[gg], gg)

        def rest_swap():
            return d2d(restv.at[:, their_cols], rest_sib, 4)

        def to_chip(k, src, off, nrows):
            px, py = chips[k]
            return pltpu.make_async_remote_copy(src_ref=src, dst_ref=got_ref.at[k, pl.ds(off, nrows), :], send_sem=ici_send.at[k],
                                                recv_sem=ici_recv.at[k], device_id=(px, py, c), device_id_type=MESH)

        def own_copy(src, off, nrows, k):
            return pltpu.make_async_copy(src, own_ref.at[pl.ds(off, nrows), :], local_sems.at[k])

        def finish_group(gg):
            group_swap(gg).wait()
            part = my_half(acc[gg % 2]) + sibbuf[gg]
            if gg < 3:
                part16[gg] = part.astype(BF16)
                to_chip(gg, part16.at[gg], c_in, rows).start()
            else:
                own_v[...] = part
                own_copy(own_v, c_in, rows, 1).start()

        @pl.when((g == 0) & (t == 0))
        def _():
            load = pltpu.make_async_copy(rest_ref, restv, local_sems.at[0])
            load.start()
            load.wait()
            rest_swap().start()

        prod = _dot_tn(a_ref[...], b_ref[...])
        for gg in range(4):
            @pl.when((g == gg) & (t == 0))
            def _():
                acc[gg % 2] = prod

            @pl.when((g == gg) & (t > 0))
            def _():
                acc[gg % 2] += prod

            @pl.when((g == gg) & (t == nt - 1))
            def _():
                group_swap(gg).start()
                if gg == 0:
                    rest_swap().wait()
                    rest_p[...] = my_half(restv[...]) + rest_sib[...]
                    for k in range(3):
                        chip_rows = pl.ds(pl.multiple_of(order_ref[k] * rest_rows, SUBLANE), rest_rows)
                        rest16[k] = rest_p[chip_rows, :].astype(BF16)
                        to_chip(k, rest16.at[k], c_rest, rest_rows).start()
                    own_r[...] = rest_p[pl.ds(pl.multiple_of(order_ref[3] * rest_rows, SUBLANE), rest_rows), :]
                    own_copy(own_r, c_rest, rest_rows, 2).start()
                else:
                    finish_group(gg - 1)
                if gg == 3:
                    finish_group(3)
                    for k, (px, py) in enumerate(chips):
                        pltpu.make_async_remote_copy(src_ref=got_ref.at[k], dst_ref=got_ref.at[k], send_sem=ici_send.at[k], recv_sem=ici_recv.at[k],
                                                     device_id=(px, py, c), device_id_type=MESH).wait()
                    own_copy(own_v, c_in, rows, 1).wait()
                    own_copy(own_r, c_rest, rest_rows, 2).wait()

    return pl.pallas_call(
        body, name="dw_in_a",
        grid_spec=pltpu.PrefetchScalarGridSpec(
            num_scalar_prefetch=1, grid=(N_CHIPS, nt),
            in_specs=[pl.BlockSpec((bt, rows), lambda g, t, order: (t, order[g])), pl.BlockSpec((bt, d), lambda g, t, order: (t, 0)), ANY],
            out_specs=[ANY, ANY],
            scratch_shapes=[pltpu.VMEM((2, rows, d), F32), pltpu.VMEM((N_CHIPS, rows, half), F32), pltpu.VMEM((3, rows, half), BF16),
                            pltpu.VMEM((N_CHIPS * rest_rows, d), F32), pltpu.VMEM((N_CHIPS * rest_rows, half), F32),
                            pltpu.VMEM((N_CHIPS * rest_rows, half), F32), pltpu.VMEM((3, rest_rows, half), BF16),
                            pltpu.VMEM((rows, half), F32), pltpu.VMEM((rest_rows, half), F32),
                            pltpu.SemaphoreType.DMA((5,)), pltpu.SemaphoreType.DMA((5,)), pltpu.SemaphoreType.DMA((3,)), pltpu.SemaphoreType.DMA((3,)),
                            pltpu.SemaphoreType.DMA((3,))]),
        out_shape=_scatter_shapes(lay, "late", half), compiler_params=_params(2),
    )(order, du, h, rest)


def _mm_tn(a, b, name, bt):
    t_all, m = a.shape
    n = b.shape[1]

    def body(a_ref, b_ref, o_ref):
        @pl.when(pl.program_id(0) == 0)
        def _():
            o_ref[...] = jnp.zeros((m, n), F32)

        o_ref[...] += _dot_tn(a_ref[...].astype(BF16), b_ref[...].astype(BF16))

    return pl.pallas_call(
        body, name=name, grid=(t_all // bt,),
        in_specs=[pl.BlockSpec((bt, m), lambda t: (t, 0)), pl.BlockSpec((bt, n), lambda t: (t, 0))],
        out_specs=_full((m, n)), out_shape=_sds((m, n)),
        compiler_params=_params(1),
    )(a, b)


class _Gather8:
    def __init__(self, x_ref, out_ref, send_sems, recv_sems, local_sem):
        x, y, c = _place()
        self.c, self.me, self.sibling = c, (x, y, c), (x, y, 1 - c)
        self.chips = [(1 - x, y), (x, 1 - y), (1 - x, 1 - y)]
        self.x_ref, self.out_ref, self.send_sems, self.recv_sems, self.local_sem = x_ref, out_ref, send_sems, recv_sems, local_sem

    def _slot(self, px, py, pc):
        return self.out_ref.at[4 * px + 2 * py + pc]

    def _copy(self, k, blk, to, src=None):
        return pltpu.make_async_remote_copy(
            src_ref=self._slot(*blk) if src is None else src, dst_ref=self._slot(*blk), send_sem=self.send_sems.at[k],
            recv_sem=self.recv_sems.at[k], device_id=to, device_id_type=MESH)

    def _mine(self):
        return pltpu.make_async_copy(self.x_ref, self._slot(*self.me), self.local_sem)

    def _first(self):
        return [self._copy(0, self.me, self.sibling, src=self.x_ref)] + [
            self._copy(1 + j, self.me, (*chip, self.c), src=self.x_ref) for j, chip in enumerate(self.chips)]

    def _passed(self):
        return [self._copy(4 + j, (*chip, self.c), self.sibling) for j, chip in enumerate(self.chips)]

    def start(self):
        self._mine().start()
        for cp in self._first():
            cp.start()

    def forward(self):
        passed = self._passed()
        for j, chip in enumerate(self.chips):
            self._copy(1 + j, (*chip, self.c), self.me).wait_recv()
            passed[j].start()

    def finish(self):
        self._copy(0, self.sibling, self.me).wait_recv()
        for j, chip in enumerate(self.chips):
            self._copy(4 + j, (*chip, 1 - self.c), self.me).wait_recv()
        for cp in self._first() + self._passed():
            cp.wait_send()
        self._mine().wait()


class _Scatter:
    def __init__(self, p16_ref, p32_ref, got_ref, own_ref, send_sems, recv_sems, local_sem, lay, order):
        self.x, self.y, self.c = _place()
        self.chips = [(1 - self.x, self.y), (self.x, 1 - self.y), (1 - self.x, 1 - self.y)]
        self.refs = (p16_ref, p32_ref, got_ref, own_ref, send_sems, recv_sems, local_sem)
        self.lay, self.order = lay, order

    def _rows_of(self, ref, key, chip):
        start = pl.multiple_of(self.lay.g_off[key] + chip * self.lay.rows[key], ROW_ALIGN)
        return ref.at[pl.ds(start, self.lay.rows[key]), :]

    def _compact(self, ref, key):
        return ref.at[pl.ds(self.lay.c_off[key], self.lay.rows[key]), :]

    def start(self):
        p16_ref, p32_ref, got_ref, own_ref, send_sems, recv_sems, local_sem = self.refs
        for key in self.order:
            pltpu.make_async_copy(self._rows_of(p32_ref, key, 2 * self.x + self.y), self._compact(own_ref, key), local_sem).start()
        for k, (px, py) in enumerate(self.chips):
            for key in self.order:
                pltpu.make_async_remote_copy(
                    src_ref=self._rows_of(p16_ref, key, 2 * px + py), dst_ref=self._compact(got_ref.at[k], key), send_sem=send_sems.at[k],
                    recv_sem=recv_sems.at[k], device_id=(px, py, self.c), device_id_type=MESH).start()

    def finish(self):
        _, _, got_ref, own_ref, send_sems, recv_sems, local_sem = self.refs
        for k, (px, py) in enumerate(self.chips):
            pltpu.make_async_remote_copy(src_ref=got_ref.at[k], dst_ref=got_ref.at[k], send_sem=send_sems.at[k], recv_sem=recv_sems.at[k],
                                         device_id=(px, py, self.c), device_id_type=MESH).wait()
        pltpu.make_async_copy(own_ref, own_ref, local_sem).wait()


class _ScatterDirect:
    def __init__(self, g16_ref, g32_ref, got_ref, sib_ref, own_ref, send_sems, recv_sems, local_sem, lay, order):
        self.x, self.y, self.c = _place()
        self.chips = [(1 - self.x, self.y), (self.x, 1 - self.y), (1 - self.x, 1 - self.y)]
        self.refs = (g16_ref, g32_ref, got_ref, sib_ref, own_ref, send_sems, recv_sems, local_sem)
        self.lay, self.order, self.half = lay, order, lay.d // 2

    def _src(self, ref, key, chip, h):
        start = pl.multiple_of(self.lay.g_off[key] + chip * self.lay.rows[key], ROW_ALIGN)
        return ref.at[pl.ds(start, self.lay.rows[key]), pl.ds(pl.multiple_of(h * self.half, LANE), self.half)]

    def _compact(self, ref, key):
        return ref.at[pl.ds(self.lay.c_off[key], self.lay.rows[key]), :]

    def start(self):
        g16_ref, g32_ref, got_ref, sib_ref, own_ref, send_sems, recv_sems, local_sem = self.refs
        x, y, c = self.x, self.y, self.c
        for key in self.order:
            pltpu.make_async_copy(self._src(g32_ref, key, 2 * x + y, c), self._compact(own_ref, key), local_sem).start()
            pltpu.make_async_remote_copy(
                src_ref=self._src(g32_ref, key, 2 * x + y, 1 - c), dst_ref=self._compact(sib_ref, key), send_sem=send_sems.at[6],
                recv_sem=recv_sems.at[6], device_id=(x, y, 1 - c), device_id_type=MESH).start()
        for k, (px, py) in enumerate(self.chips):
            for h in range(2):
                for key in self.order:
                    pltpu.make_async_remote_copy(
                        src_ref=self._src(g16_ref, key, 2 * px + py, h), dst_ref=self._compact(got_ref.at[2 * k + c], key),
                        send_sem=send_sems.at[2 * k + h], recv_sem=recv_sems.at[2 * k + c], device_id=(px, py, h), device_id_type=MESH).start()

    def finish(self):
        _, _, got_ref, sib_ref, own_ref, send_sems, recv_sems, local_sem = self.refs
        x, y, c = self.x, self.y, self.c
        for k, (px, py) in enumerate(self.chips):
            for h in range(2):
                whole = pltpu.make_async_remote_copy(src_ref=got_ref.at[2 * k + h], dst_ref=got_ref.at[2 * k + h], send_sem=send_sems.at[2 * k + h],
                                                     recv_sem=recv_sems.at[2 * k + h], device_id=(px, py, h), device_id_type=MESH)
                whole.wait_send()
                whole.wait_recv()
        pltpu.make_async_remote_copy(src_ref=sib_ref, dst_ref=sib_ref, send_sem=send_sems.at[6], recv_sem=recv_sems.at[6],
                                     device_id=(x, y, 1 - c), device_id_type=MESH).wait()
        pltpu.make_async_copy(own_ref, own_ref, local_sem).wait()


def _scatter_direct_shapes(lay, group):
    rows, half = lay.c_rows[group], lay.d // 2
    return [_sds((6, rows, half), BF16), _sds((rows, half), F32), _sds((rows, half), F32)]


SCATTER_DIRECT_SEMS = [pltpu.SemaphoreType.DMA((7,)), pltpu.SemaphoreType.DMA((7,)), pltpu.SemaphoreType.DMA]
SCATTER_SEMS = [pltpu.SemaphoreType.DMA((3,)), pltpu.SemaphoreType.DMA((3,)), pltpu.SemaphoreType.DMA]
GATHER_SEMS = [pltpu.SemaphoreType.DMA((7,)), pltpu.SemaphoreType.DMA((7,)), pltpu.SemaphoreType.DMA]


def _all_gather8(blocks, name):
    nb = len(blocks)

    def body(*refs):
        x_refs, out_refs = refs[:nb], refs[nb:2 * nb]
        send_sems, recv_sems, local_sems = refs[2 * nb:]
        gathers = [_Gather8(x_refs[n], out_refs[n], send_sems.at[n], recv_sems.at[n], local_sems.at[n]) for n in range(nb)]
        for g in gathers:
            g.start()
        for g in gathers:
            g.forward()
        for g in gathers:
            g.finish()

    return pl.pallas_call(
        body, name=name, out_shape=[_sds((8,) + b.shape, b.dtype) for b in blocks], in_specs=[ANY] * nb, out_specs=[ANY] * nb,
        scratch_shapes=[pltpu.SemaphoreType.DMA((nb, 7)), pltpu.SemaphoreType.DMA((nb, 7)), pltpu.SemaphoreType.DMA((nb,))],
    )(*blocks)


def _swap_sibling(srcs, name, half_cols=False):
    n = len(srcs)
    halves = [s.shape[1] // 2 if half_cols else s.shape[1] for s in srcs]

    def body(*refs):
        src_refs, out_refs, send_sems, recv_sems = refs[:n], refs[n:2 * n], refs[2 * n], refs[2 * n + 1]
        x, y, c = _place()
        copies = []
        for k in range(n):
            part = src_refs[k].at[:, pl.ds(pl.multiple_of((1 - c) * halves[k], LANE), halves[k])] if half_cols else src_refs[k]
            copies.append(pltpu.make_async_remote_copy(src_ref=part, dst_ref=out_refs[k], send_sem=send_sems.at[k], recv_sem=recv_sems.at[k],
                                                       device_id=(x, y, 1 - c), device_id_type=MESH))
        for cp in copies:
            cp.start()
        for cp in copies:
            cp.wait()

    return pl.pallas_call(
        body, name=name, out_shape=[_sds((s.shape[0], h), s.dtype) for s, h in zip(srcs, halves)], in_specs=[ANY] * n, out_specs=[ANY] * n,
        scratch_shapes=[pltpu.SemaphoreType.DMA((n,)), pltpu.SemaphoreType.DMA((n,))],
    )(*srcs)


def _return_and_gather(mines, rep_block):
    n = len(mines)

    def body(*refs):
        src_refs, rep_ref, out_refs, rep_out = refs[:n], refs[n], refs[n + 1:2 * n + 1], refs[2 * n + 1]
        send_sems, recv_sems, g_send, g_recv, g_local = refs[2 * n + 2:]
        x, y, c = _place()
        copies = [pltpu.make_async_remote_copy(src_ref=src_refs[k], dst_ref=out_refs[k], send_sem=send_sems.at[k], recv_sem=recv_sems.at[k],
                                               device_id=(x, y, 1 - c), device_id_type=MESH) for k in range(n)]
        gather = _Gather8(rep_ref, rep_out, g_send, g_recv, g_local)
        for cp in copies:
            cp.start()
        gather.start()
        gather.forward()
        gather.finish()
        for cp in copies:
            cp.wait()

    return pl.pallas_call(
        body, name="rs_return", out_shape=[_sds(m.shape, m.dtype) for m in mines] + [_sds((8,) + rep_block.shape, rep_block.dtype)],
        in_specs=[ANY] * (n + 1), out_specs=[ANY] * (n + 1),
        scratch_shapes=[pltpu.SemaphoreType.DMA((n,)), pltpu.SemaphoreType.DMA((n,))] + GATHER_SEMS,
    )(*mines, rep_block)


def _scatter_shapes(lay, group, half):
    return [_sds((3, lay.c_rows[group], half), BF16), _sds((lay.c_rows[group], half), F32)]


def _scatter_chips(part16, part32, lay, group, name):
    def body(p16_ref, p32_ref, got_ref, own_ref, send_sems, recv_sems, local_sem):
        sc = _Scatter(p16_ref, p32_ref, got_ref, own_ref, send_sems, recv_sems, local_sem, lay, G_GROUPS[group])
        sc.start()
        sc.finish()

    return pl.pallas_call(
        body, name=name, out_shape=_scatter_shapes(lay, group, part16.shape[1]), in_specs=[ANY, ANY], out_specs=[ANY, ANY],
        scratch_shapes=SCATTER_SEMS,
    )(part16, part32)


def _sum_sibling(gbuf, got, cidx, name):
    rows, d = gbuf.shape
    half = d // 2
    rb = _row_block(rows)

    def body(c_ref, g_ref, r_ref, o32_ref, o16_ref):
        del c_ref
        s = g_ref[...] + r_ref[...]
        o32_ref[...] = s
        o16_ref[...] = s.astype(BF16)

    plain = pl.BlockSpec((rb, half), lambda i, c: (i, 0))
    return pl.pallas_call(
        body, name=name,
        grid_spec=pltpu.PrefetchScalarGridSpec(num_scalar_prefetch=1, grid=(rows // rb,),
                                               in_specs=[pl.BlockSpec((rb, half), lambda i, c: (i, c[0])), plain], out_specs=[plain, plain]),
        out_shape=[_sds((rows, half)), _sds((rows, half), BF16)], compiler_params=_params(1),
    )(cidx, gbuf, got)


def _sum_devices(own, sib, got, name):
    rows, half = own.shape
    rb = _row_block(rows)
    n = got.shape[0]

    def body(a_ref, s_ref, b_ref, o_ref):
        acc = a_ref[...] + s_ref[...]
        for k in range(n):
            acc = acc + b_ref[k].astype(F32)
        o_ref[...] = acc

    spec = pl.BlockSpec((rb, half), lambda i: (i, 0))
    return pl.pallas_call(
        body, name=name, grid=(rows // rb,), in_specs=[spec, spec, pl.BlockSpec((n, rb, half), lambda i: (0, i, 0))], out_specs=spec,
        out_shape=_sds((rows, half)), compiler_params=_params(1),
    )(own, sib, got)


def _sum_chips(own, got, name):
    rows, half = own.shape
    rb = _row_block(rows)

    def body(a_ref, b_ref, o_ref):
        o_ref[...] = ((a_ref[...] + b_ref[0].astype(F32)) + b_ref[1].astype(F32)) + b_ref[2].astype(F32)

    spec = pl.BlockSpec((rb, half), lambda i: (i, 0))
    return pl.pallas_call(
        body, name=name, grid=(rows // rb,), in_specs=[spec, pl.BlockSpec((3, rb, half), lambda i: (0, i, 0))], out_specs=spec,
        out_shape=_sds((rows, half)), compiler_params=_params(1),
    )(own, got)


def _adamw(w, g, m, v):
    m = ADAM_B1 * m + (1.0 - ADAM_B1) * g
    v = ADAM_B2 * v + (1.0 - ADAM_B2) * (g * g)
    m_hat = m / (1.0 - ADAM_B1 ** ADAM_STEP)
    v_hat = v / (1.0 - ADAM_B2 ** ADAM_STEP)
    return -ADAM_LR * (m_hat / (jnp.sqrt(v_hat) + ADAM_EPS) + ADAM_WD * w), m, v


def _adamw_rows(name, w, g, m, v):
    _, rows, cols = w.shape
    rb = _row_block(rows, 256)

    def body(w_ref, g_ref, m_ref, v_ref, d_ref, mo_ref, vo_ref):
        d_ref[...], mo_ref[...], vo_ref[...] = _adamw(w_ref[...], g_ref[...], m_ref[...], v_ref[...])

    spec = pl.BlockSpec((1, rb, cols), lambda i: (0, i, 0))
    return pl.pallas_call(
        body, name=name, grid=(rows // rb,), in_specs=[spec] * 4, out_specs=[spec] * 3, out_shape=[_sds(w.shape)] * 3,
        compiler_params=_params(1),
    )(w, g, m, v)


def _adamw_group(ws, gs, ms, vs):
    n = len(ws)

    def body(*refs):
        for k in range(n):
            w_ref, g_ref, m_ref, v_ref = (refs[j * n + k] for j in range(4))
            outs = _adamw(w_ref[...], g_ref[...], m_ref[...], v_ref[...])
            for j in range(3):
                refs[(4 + j) * n + k][...] = outs[j]

    outs = pl.pallas_call(
        body, name="adamw_small", out_shape=[_sds(w.shape) for w in ws] * 3,
        compiler_params=pltpu.CompilerParams(vmem_limit_bytes=VMEM_LIMIT),
    )(*ws, *gs, *ms, *vs)
    return outs[:n], outs[n:2 * n], outs[2 * n:]


def _gather_weights(sh, lay):
    x, y, c = _place()
    d = lay.d
    uq = sh["w_uq"][0].astype(BF16)
    parts = {
        "in_b": sh["w_in_b"][0].T.astype(BF16), "in_a": sh["w_in_a"][0].T.astype(BF16), "out_a": sh["w_out_a"][0].astype(BF16),
        "out_b": sh["w_out_b"][0].astype(BF16), "uk": sh["w_uk"].astype(BF16).reshape(-1, d), "uv": sh["w_uv"].astype(BF16).reshape(-1, d),
        "uq_n": uq[:, :, :QK_NOPE].reshape(-1, d), "uq_r": jnp.pad(uq[:, :, QK_NOPE:], ((0, 0), (0, 0), (0, LANE - QK_ROPE))).reshape(-1, d),
        "dkv": jnp.pad(sh["w_dkv"].astype(BF16), ((0, 0), (0, LANE - QK_ROPE))).reshape(-1, d),
    }
    halves = {}
    for group, order in W_GROUPS.items():
        stack = jnp.concatenate([parts[k] for k in order], axis=0).reshape(2, lay.w_rows[group] // 2, d)
        halves[group] = lax.dynamic_index_in_dim(stack, c, 0, keepdims=False)
    small = jnp.concatenate([sh[k].reshape(-1) for k in SMALL])
    n_small = small.shape[0]
    width = _round_up(n_small, 2 * SUBLANE * LANE) // (2 * SUBLANE)
    small = jnp.pad(small, (0, 2 * SUBLANE * width - n_small)).reshape(2, SUBLANE, width)
    wg, sg = _all_gather8([halves["a"], lax.dynamic_index_in_dim(small, c, 0, keepdims=False)], "ag_weights")
    wg = wg.reshape(N_CHIPS, lay.w_rows["a"], d)
    sg = sg.reshape(N_CHIPS, 2 * SUBLANE * width)
    full, off = {}, 0
    for k in SMALL:
        n = sh[k].size
        piece = sg[:, off:off + n]
        off += n
        if k == "conv_w":
            full[k] = piece.reshape(N_CHIPS, 4, n // 4).transpose(1, 0, 2).reshape(4, n)
        else:
            full[k] = piece.reshape(1, N_CHIPS * n)
    return wg, halves["b"], full


def _chip_split(g, taps=False):
    if taps:
        n = g.shape[1] // N_CHIPS
        return g.reshape(4, N_CHIPS, n).transpose(1, 0, 2).reshape(N_CHIPS, 4 * n)
    return g.reshape(N_CHIPS, -1)


def kernel(x, norm_a, w_in_a, conv_w, conv_b, w_rg, b_rg, w_ig, b_ig, lru_lambda, w_out_a, norm_kv, w_dkv, kv_norm, w_uk, w_uv, norm_b, w_in_b, q_norm, w_uq, w_out_b, final_norm, loss_target, m_norm_a, m_w_in_a, m_conv_w, m_conv_b, m_w_rg, m_b_rg, m_w_ig, m_b_ig, m_lru_lambda, m_w_out_a, m_norm_kv, m_w_dkv, m_kv_norm, m_w_uk, m_w_uv, m_norm_b, m_w_in_b, m_q_norm, m_w_uq, m_w_out_b, m_final_norm, v_norm_a, v_w_in_a, v_conv_w, v_conv_b, v_w_rg, v_b_rg, v_w_ig, v_b_ig, v_lru_lambda, v_w_out_a, v_norm_kv, v_w_dkv, v_kv_norm, v_w_uk, v_w_uv, v_norm_b, v_w_in_b, v_q_norm, v_w_uq, v_w_out_b, v_final_norm):
    given = dict(locals())
    sh = {k: given[k] for k in WEIGHTS}
    xi, yi, ci = _place()
    nb, seq, d = x.shape
    t_all = nb * seq
    tb_a, tb_b, ta, bt = min(TOKENS_A, seq), min(TOKENS_B, seq), min(TOKENS_ATTN, seq), min(TOKENS_MM, t_all)
    dr = conv_b.shape[1] * N_CHIPS
    qr, kvr, nheads = q_norm.shape[1], kv_norm.shape[0], w_uk.shape[1]
    hv = nheads * LANE
    n_small = sum(sh[k].size for k in SMALL)
    n_repl = sum(sh[k].size for k in REPL)
    lay = _Layout(d, dr, qr, kvr, hv, n_small, n_repl)
    half = d // 2

    wga, wb_half, w = _gather_weights(sh, lay)
    w.update({"w_rg": w_rg[0].astype(BF16), "w_ig": w_ig[0].astype(BF16), "norm_kv": norm_kv[None, :],
              "kv_norm": kv_norm[None, :], "final_norm": final_norm[None, :], "norm_b": norm_b, "q_norm": q_norm})
    cos_t, sin_t = _rope_tables(seq)
    cidx = jnp.reshape(ci, (1,)).astype(jnp.int32)

    x0 = x.reshape(t_all, d)
    x1, u, hs, h, y, xb, wgb = _fa_fwd(x0, wga, wb_half, w, lay, seq, min(TOKENS_A_FWD, seq))
    wgb = wgb.reshape(N_CHIPS, lay.w_rows["b"], d)
    w["w_dkv_p"] = wgb[:, lay.w_off["dkv"]:lay.w_off["dkv"] + lay.rows["dkv"], :].reshape(d, kvr + LANE)
    qn, qrp, kn, kr, v, ub, ckr, hb, hk, cq, ckv = _fb_fwd(x1, wgb, w, lay, cos_t, sin_t, seq, tb_b)
    o, lse = _attn_fwd(qn, qrp, kn, kr, v, seq, ta)
    loss, g_final_norm, yb, dx2, do, dgate, delta = _head(o, ub, x1, loss_target.reshape(t_all, d), wgb, w, lay, tb_b)
    dqn, dqr, dkn, dkr, dv = _attn_bwd(qn, qrp, kn, kr, v, do, lse, delta, seq, ta)
    dx1, dqr_pre, dqn_pre, dub, dckr, g_q_norm, g_norm_b, g_kv_norm, g_norm_kv = _fb_bwd(
        dqn, dqr, dkn, dkr, dv, dgate, ub, ckr, x1, dx2, wgb, w, lay, cos_t, sin_t, seq, tb_b)
    loss_here = loss[0, 0]

    gbufs = [lax.empty((lay.g_rows["early"], d), F32), lax.empty((lay.g_rows["early"], d), BF16)]
    for keys, a, bs in ((("in_b",), dub, (hb,)), (("out_a",), y, (dx1,)), (("out_b",), yb, (dx2,)), (("uk", "uv"), ckv, (dkn, dv)),
                        (("uq_n", "uq_r"), cq, (dqn_pre, dqr_pre))):
        gbufs = _mm_into(gbufs, a, bs, [lay.g_off[k] for k in keys], "dw_" + "_".join(keys), bt)
    g_dkv = _mm_tn(hk, dckr, "dw_dkv", bt)
    gx, du, g_norm_a, g_conv_w, g_conv_b, g_b_rg, g_b_ig, g_lam, g_w_rg, g_w_ig, others, sib, own = _fa_bwd(
        dx1, x0, u, xb, hs, wga, gbufs[1], gbufs[0], w, lay, seq, tb_a)
    mine_early = _sum_devices(own, sib, others, "rs_sum_early")

    small = jnp.concatenate([_chip_split(g_norm_a), _chip_split(g_conv_w, taps=True), _chip_split(g_conv_b), _chip_split(g_b_rg),
                             _chip_split(g_b_ig), _chip_split(g_lam)], axis=1)
    small = jnp.pad(small, ((0, 0), (0, lay.small_rows * d - small.shape[1]))).reshape(N_CHIPS, lay.small_rows, d)
    repl_parts = {"w_rg": g_w_rg, "w_ig": g_w_ig, "norm_kv": g_norm_kv, "kv_norm": g_kv_norm, "norm_b": g_norm_b, "q_norm": g_q_norm,
                  "final_norm": g_final_norm}
    repl = jnp.concatenate([repl_parts[k].reshape(-1) for k in REPL])
    repl = jnp.pad(repl, (0, N_CHIPS * lay.repl_rows * d - n_repl)).reshape(N_CHIPS, lay.repl_rows, d)
    pad_rows = lay.rows["rest"] - lay.rows["dkv"] - lay.small_rows - lay.repl_rows
    rest = jnp.concatenate([g_dkv.reshape(N_CHIPS, lay.rows["dkv"], d), small, repl, jnp.zeros((N_CHIPS, pad_rows, d), F32)], axis=1)
    others, own = _dw_in_a_exchange(du, h, rest.reshape(N_CHIPS * lay.rows["rest"], d), lay, bt)
    mine_late = _sum_chips(own, others, "rs_sum_chips_late")

    r0 = lay.c_off["rest"] + lay.rows["dkv"]
    tail = mine_late[r0:r0 + lay.small_rows + lay.repl_rows].at[lay.small_rows - 1, 0].set(loss_here)
    theirs_early, theirs_late, rep_all = _return_and_gather([mine_early, mine_late], tail)
    loss = jnp.sum(rep_all[:, lay.small_rows - 1, 0])
    rep_all = rep_all[:, lay.small_rows:]
    red = {}
    for group, mine, theirs in (("early", mine_early, theirs_early), ("late", mine_late, theirs_late)):
        red[group] = jnp.concatenate([jnp.where(ci == 0, mine, theirs), jnp.where(ci == 0, theirs, mine)], axis=1)
    rep_flat =rep_all.reshape(N_CHIPS, 2, lay.repl_rows, half).transpose(0, 2, 1, 3).reshape(-1)

    def rows(key):
        group = "late" if key in G_GROUPS["late"] else "early"
        return red[group][lay.c_off[key]:lay.c_off[key] + lay.rows[key]]

    grads = {"w_in_b": rows("in_b").T[None], "w_in_a": rows("in_a").T[None], "w_out_a": rows("out_a")[None], "w_out_b": rows("out_b")[None],
             "w_uk": rows("uk").reshape(w_uk.shape), "w_uv": rows("uv").reshape(w_uv.shape)}
    uq_n = rows("uq_n").reshape(qr // N_CHIPS, nheads, LANE)
    uq_r = rows("uq_r").reshape(qr // N_CHIPS, nheads, LANE)[:, :, :QK_ROPE]
    grads["w_uq"] = jnp.concatenate([uq_n, uq_r], axis=2)[None]
    rest_red = rows("rest")
    grads["w_dkv"] = rest_red[:lay.rows["dkv"]].reshape(d // N_CHIPS, kvr + LANE)[:, :kvr + QK_ROPE]
    small_red = rest_red[lay.rows["dkv"]:lay.rows["dkv"] + lay.small_rows].reshape(-1)
    off = 0
    for k in SMALL:
        n = sh[k].size
        grads[k] = small_red[off:off + n].reshape(sh[k].shape)
        off += n
    off = 0
    for k in REPL:
        n = sh[k].size
        grads[k] = rep_flat[off:off + n].reshape(sh[k].shape)
        off += n

    new = {}
    for k in ("w_in_a", "w_in_b", "w_out_a", "w_out_b"):
        view = (lambda a: jnp.swapaxes(a, 1, 2)) if k in TRANSPOSED else (lambda a: a)
        outs = _adamw_rows("adamw_" + k, view(sh[k]), view(grads[k]), view(given["m_" + k]), view(given["v_" + k]))
        new[k] = tuple(view(a) for a in outs)
    rest_names = [k for k in WEIGHTS if k not in new]

    def as2d(k, a):
        return a.T if k in TRANSPOSED else a[None, :] if a.ndim == 1 else a

    ds, ms, vs = _adamw_group([as2d(k, sh[k]) for k in rest_names], [as2d(k, grads[k]) for k in rest_names],
                              [as2d(k, given["m_" + k]) for k in rest_names], [as2d(k, given["v_" + k]) for k in rest_names])
    for n, k in enumerate(rest_names):
        new[k] = tuple((a.T if k in TRANSPOSED else a).reshape(sh[k].shape) for a in (ds[n], ms[n], vs[n]))
    return (loss, gx.reshape(nb, seq, d), *[grads[k] for k in WEIGHTS], *[new[k][0] for k in WEIGHTS], *[new[k][1] for k in WEIGHTS],
            *[new[k][2] for k in WEIGHTS])
```

```python
import jax
import jax.numpy as jnp
from jax import lax
from jax.experimental import pallas as pl
from jax.experimental.pallas import tpu as pltpu

F32, BF16 = jnp.float32, jnp.bfloat16
EPS = 1e-6
LRU_C = 8.0
ROPE_THETA = 10000.0
QK_NOPE, QK_ROPE = 128, 64
ATTN_SCALE = (QK_NOPE + QK_ROPE) ** -0.5
LN2 = 0.6931471805599453
Q_SCALE = ATTN_SCALE / LN2
ATTN_HEADS, ATTN_HEADS_BWD = 4, 2
ATTN_ROWS = 64
LANE = 128
SUBLANE = 8
ROW_ALIGN = 32
VMEM_LIMIT = 60000 * 1024
ADAM_LR, ADAM_B1, ADAM_B2, ADAM_EPS, ADAM_WD, ADAM_STEP = 0.001, 0.9, 0.999, 1e-08, 0.01, 10
MESH = pl.DeviceIdType.MESH
ANY = pl.BlockSpec(memory_space=pl.ANY)
N_CHIPS = 4
TOKENS_A, TOKENS_B, TOKENS_ATTN, TOKENS_MM = 256, 512, 512, 2048
TOKENS_A_FWD = 512

SMALL = ("norm_a", "conv_w", "conv_b", "b_rg", "b_ig", "lru_lambda")
REPL = ("w_rg", "w_ig", "norm_kv", "kv_norm", "norm_b", "q_norm", "final_norm")
TRANSPOSED = ("w_in_b", "w_dkv")
WEIGHTS = ("norm_a", "w_in_a", "conv_w", "conv_b", "w_rg", "b_rg", "w_ig", "b_ig", "lru_lambda", "w_out_a", "norm_kv",
           "w_dkv", "kv_norm", "w_uk", "w_uv", "norm_b", "w_in_b", "q_norm", "w_uq", "w_out_b", "final_norm")
W_GROUPS = {"a": ("in_a", "out_a"), "b": ("in_b", "out_b", "uk", "uv", "uq_n", "uq_r", "dkv")}
G_GROUPS = {"early": ("in_b", "out_a", "out_b", "uk", "uv", "uq_n", "uq_r"), "late": ("in_a", "rest")}


def _sds(shape, dtype=F32):
    return jax.ShapeDtypeStruct(tuple(shape), dtype)


def _params(n_grid):
    return pltpu.CompilerParams(dimension_semantics=("arbitrary",) * n_grid, vmem_limit_bytes=VMEM_LIMIT)


def _full(shape):
    nd = len(shape)
    return pl.BlockSpec(tuple(shape), lambda *g: (0,) * nd)


def _round_up(n, k):
    return -(-n // k) * k


def _row_block(rows, cap=512):
    best = SUBLANE
    for r in range(SUBLANE, min(rows, cap) + 1, SUBLANE):
        if rows % r == 0:
            best = r
    return best


def _place():
    return lax.axis_index("x"), lax.axis_index("y"), lax.axis_index("c")


class _Layout:
    def __init__(self, d, dr, qr, kvr, hv, n_small, n_repl):
        assert hv == d, "the packed rows are D_MODEL wide, which must equal heads * 128"
        self.d, self.dr, self.qr, self.kvr, self.hv = d, dr, qr, kvr, hv
        per_chip = {"in_b": (qr + hv) // N_CHIPS, "in_a": 2 * dr // N_CHIPS, "out_a": dr // N_CHIPS, "out_b": hv // N_CHIPS,
                    "uk": kvr // N_CHIPS, "uv": kvr // N_CHIPS, "uq_n": qr // N_CHIPS, "uq_r": qr // N_CHIPS,
                    "dkv": (d // N_CHIPS) * (kvr + LANE) // d}
        assert all(r % ROW_ALIGN == 0 for r in per_chip.values()), per_chip
        self.small_rows = _round_up(-(-n_small // d), SUBLANE)
        self.repl_rows = _round_up(-(-n_repl // (N_CHIPS * d)), SUBLANE)
        per_chip["rest"] = _round_up(per_chip["dkv"] + self.small_rows + self.repl_rows, ROW_ALIGN)
        self.rows = per_chip
        self.w_off, self.w_rows = {}, {}
        for group, order in W_GROUPS.items():
            off = 0
            for k in order:
                self.w_off[k] = off
                off += per_chip[k]
            assert off % ROW_ALIGN == 0, (group, off)
            self.w_rows[group] = off
        self.g_off, self.c_off, self.c_rows, self.g_rows = {}, {}, {}, {}
        for group, order in G_GROUPS.items():
            off = 0
            for k in order:
                self.c_off[k] = off
                self.g_off[k] = N_CHIPS * off
                off += per_chip[k]
            self.c_rows[group] = off
            self.g_rows[group] = N_CHIPS * off


def _dot(a, b):
    return jnp.dot(a, b, preferred_element_type=F32)


def _dot_nt(a, b):
    return lax.dot_general(a, b, (((1,), (1,)), ((), ())), preferred_element_type=F32)


def _dot_tn(a, b):
    return lax.dot_general(a, b, (((0,), (0,)), ((), ())), preferred_element_type=F32)


def _rinv(x):
    return lax.rsqrt(jnp.mean(x * x, axis=-1, keepdims=True) + EPS)


def _rms_bwd(x, rinv, g, dy):
    z = dy * g
    dx = rinv * z - x * (rinv * rinv * rinv) * jnp.mean(z * x, axis=-1, keepdims=True)
    dg = jnp.sum(dy * (x * rinv), axis=0, keepdims=True)
    return dx, dg


def _softplus(z):
    return jnp.maximum(z, 0.0) + jnp.log1p(jnp.exp(-jnp.abs(z)))


def _sigmoid(x):
    return 0.5 * jnp.tanh(0.5 * x) + 0.5


def _decay(log_a):
    a = jnp.exp(log_a)
    a2 = a * a
    return a, a2, -jnp.tanh(log_a) * (a2 + 1.0)


def _swap_halves(x):
    w = x.shape[1]
    lane = lax.broadcasted_iota(jnp.int32, x.shape, 1)
    return jnp.where(lane % QK_ROPE < QK_ROPE // 2, pltpu.roll(x, w - QK_ROPE // 2, 1), pltpu.roll(x, QK_ROPE // 2, 1))


def _rope_tables(seq):
    pos = jnp.arange(seq, dtype=F32)
    inv = ROPE_THETA ** (-jnp.arange(0, QK_ROPE, 2, dtype=F32) / QK_ROPE)
    ang = pos[:, None] * inv[None, :]
    cos, sin = jnp.cos(ang), jnp.sin(ang)
    zero = jnp.zeros((seq, LANE - QK_ROPE), F32)
    return jnp.concatenate([cos, cos, zero], 1), jnp.concatenate([-sin, sin, zero], 1)


def _fetch(wg_ref, lay, key, dst, sems, k0):
    rows = lay.rows[key]
    return [pltpu.make_async_copy(wg_ref.at[p, pl.ds(lay.w_off[key], rows), :], dst.at[pl.ds(p * rows, rows), :], sems.at[k0 + p])
            for p in range(N_CHIPS)]


def _gates(xb, wrg_ref, brg, wig_ref, big, nblocks):
    xbb = xb.astype(BF16)
    rg = [_dot(xbb[:, n * LANE:(n + 1) * LANE], wrg_ref[n]) for n in range(nblocks)]
    ig = [_dot(xbb[:, n * LANE:(n + 1) * LANE], wig_ref[n]) for n in range(nblocks)]
    r = _sigmoid(jnp.concatenate(rg, axis=1) + brg)
    i = _sigmoid(jnp.concatenate(ig, axis=1) + big)
    return r, i


def _conv(xpad, cw_ref, cb, tb):
    return (cb + cw_ref[3:4, :] * xpad[pl.ds(8, tb), :] + cw_ref[2:3, :] * xpad[pl.ds(7, tb), :]
            + cw_ref[1:2, :] * xpad[pl.ds(6, tb), :] + cw_ref[0:1, :] * xpad[pl.ds(5, tb), :])


def _fa_fwd(x, wg, wb_half, w, lay, seq, tb):
    t_all, d = x.shape
    dr = lay.dr
    nblocks = w["w_rg"].shape[0]
    nblk = seq // tb
    nt = tb // SUBLANE
    nsteps = (t_all // seq) * nblk

    def body(x_ref, wg_ref, wbh_ref, na, cw, cb, wrg, brg, wig, big, lam, x1_ref, u_ref, hs_ref, h_ref, y_ref, xb_ref, wb_ref,
             wint, wout, xpad, a_s, b_s, carry, sems, send_sems, recv_sems, local_sem):
        step_no = pl.program_id(0) * nblk + pl.program_id(1)
        gather = _Gather8(wbh_ref, wb_ref, send_sems, recv_sems, local_sem)

        @pl.when(step_no == 0)
        def _():
            gather.start()
            cps = _fetch(wg_ref, lay, "in_a", wint, sems, 0) + _fetch(wg_ref, lay, "out_a", wout, sems, N_CHIPS)
            for cp in cps:
                cp.start()
            for cp in cps:
                cp.wait()

        @pl.when(step_no == nsteps // 2)
        def _():
            gather.forward()

        @pl.when(pl.program_id(1) == 0)
        def _():
            xpad[pl.ds(0, 8), :] = jnp.zeros((8, dr), F32)
            carry[...] = jnp.zeros((8, dr), F32)

        xv = x_ref[...]
        h = (xv * _rinv(xv) * na[...]).astype(BF16)
        h_ref[...] = h
        u = _dot_nt(h, wint[...])
        u_ref[...] = u
        xpre, gate = u[:, :dr], u[:, dr:]
        xpad[pl.ds(8, tb), :] = xpre
        xb = _conv(xpad, cw, cb[...], tb)
        xb_ref[...] = xb
        xpad[pl.ds(0, 8), :] = xpre[tb - 8:, :]
        r, i = _gates(xb, wrg, brg[...], wig, big[...], nblocks)
        log_a = -LRU_C * r * _softplus(-lam[...])
        a, _, nem = _decay(log_a)
        a_s[...] = a
        b_s[...] = jnp.sqrt(nem) * (i * xb)
        row = lax.broadcasted_iota(jnp.int32, (8, dr), 0)

        def step(t, c):
            r0 = pl.multiple_of(t * 8, 8)
            a = a_s[pl.ds(r0, 8), :]
            b = b_s[pl.ds(r0, 8), :]
            for s in (1, 2, 4):
                m = row >= s
                a_sh = jnp.where(m, pltpu.roll(a, s, 0), 1.0)
                b_sh = jnp.where(m, pltpu.roll(b, s, 0), 0.0)
                b = a * b_sh + b
                a = a * a_sh
            hh = b + a * c
            hs_ref[pl.ds(r0, 8), :] = hh
            return jnp.broadcast_to(hh[7:8, :], hh.shape)

        carry[...] = lax.fori_loop(0, nt, step, carry[...])
        y = (hs_ref[...] * (gate * _sigmoid(gate))).astype(BF16)
        y_ref[...] = y
        x1_ref[...] = xv + _dot(y, wout[...])

        @pl.when(step_no == nsteps - 1)
        def _():
            gather.finish()

    tok = lambda c: pl.BlockSpec((tb, c), lambda b, j: (b * nblk + j, 0))
    consts = [w["norm_a"], w["conv_w"], w["conv_b"], w["w_rg"], w["b_rg"], w["w_ig"], w["b_ig"], w["lru_lambda"]]
    return pl.pallas_call(
        body, name="fa_fwd", grid=(t_all // seq, nblk),
        in_specs=[tok(d), ANY, ANY] + [_full(c.shape) for c in consts],
        out_specs=[tok(d), tok(2 * dr), tok(dr), tok(d), tok(dr), tok(dr), ANY],
        out_shape=[_sds((t_all, d)), _sds((t_all, 2 * dr)), _sds((t_all, dr)), _sds((t_all, d), BF16), _sds((t_all, dr), BF16),
                   _sds((t_all, dr)), _sds((8,) + wb_half.shape, BF16)],
        scratch_shapes=[pltpu.VMEM((2 * dr, d), BF16), pltpu.VMEM((dr, d), BF16), pltpu.VMEM((tb + 8, dr), F32), pltpu.VMEM((tb, dr), F32),
                        pltpu.VMEM((tb, dr), F32), pltpu.VMEM((8, dr), F32), pltpu.SemaphoreType.DMA((2 * N_CHIPS,))] + GATHER_SEMS,
        compiler_params=_params(2),
    )(x, wg, wb_half, *consts)


def _fb_fwd(x1, wg, w, lay, cos_t, sin_t, seq, tb):
    t_all, d = x1.shape
    kvr, qr, hv = lay.kvr, lay.qr, lay.hv
    nheads = hv // LANE
    npos = seq // tb

    def body(x_ref, wg_ref, nkv, nb, wdkv, kvn, qn, cos_ref, sin_ref,
             qn_o, qr_o, kn_o, kr_o, v_o, ub_o, ckr_o, hb_o, hk_o, cq_o, ckv_o, winb, wuk, wuv, wuqn, wuqr, sems):
        @pl.when(pl.program_id(0) == 0)
        def _():
            cps = []
            for n, (key, dst) in enumerate((("in_b", winb), ("uk", wuk), ("uv", wuv), ("uq_n", wuqn), ("uq_r", wuqr))):
                cps += _fetch(wg_ref, lay, key, dst, sems, n * N_CHIPS)
            for cp in cps:
                cp.start()
            for cp in cps:
                cp.wait()

        xv = x_ref[...]
        xh = xv * _rinv(xv)
        hk = (xh * nkv[...]).astype(BF16)
        hb = (xh * nb[...]).astype(BF16)
        hk_o[...] = hk
        hb_o[...] = hb
        cos, sin = cos_ref[...], sin_ref[...]
        ckr = _dot(hk, wdkv[...])
        ckr_o[...] = ckr
        ckv_pre = ckr[:, :kvr]
        ckv = (ckv_pre * _rinv(ckv_pre) * kvn[...]).astype(BF16)
        ckv_o[...] = ckv
        kr = ckr[:, kvr:]
        kr_o[...] = (kr * cos + _swap_halves(kr) * sin).astype(BF16)
        kn_o[...] = _dot(ckv, wuk[...]).astype(BF16)
        v_o[...] = _dot(ckv, wuv[...]).astype(BF16)
        ub = _dot_nt(hb, winb[...])
        ub_o[...] = ub
        cq_pre = ub[:, :qr]
        cq = (cq_pre * _rinv(cq_pre) * qn[...]).astype(BF16)
        cq_o[...] = cq
        qn_o[...] = (_dot(cq, wuqn[...]) * Q_SCALE).astype(BF16)
        qrope = _dot(cq, wuqr[...]) * Q_SCALE
        qr_o[...] = (qrope * jnp.tile(cos, (1, nheads)) + _swap_halves(qrope) * jnp.tile(sin, (1, nheads))).astype(BF16)

    tok = lambda c: pl.BlockSpec((tb, c), lambda i: (i, 0))
    pos = pl.BlockSpec((tb, LANE), lambda i: (i % npos, 0))
    consts = [w["norm_kv"], w["norm_b"], w["w_dkv_p"], w["kv_norm"], w["q_norm"]]
    outs = [(hv, BF16), (hv, BF16), (hv, BF16), (LANE, BF16), (hv, BF16), (qr + hv, F32), (kvr + LANE, F32), (d, BF16), (d, BF16), (qr, BF16), (kvr, BF16)]
    return pl.pallas_call(
        body, name="fb_fwd", grid=(t_all // tb,),
        in_specs=[tok(d), ANY] + [_full(c.shape) for c in consts] + [pos, pos],
        out_specs=[tok(c) for c, _ in outs],
        out_shape=[_sds((t_all, c), dt) for c, dt in outs],
        scratch_shapes=[pltpu.VMEM((qr + hv, d), BF16), pltpu.VMEM((kvr, d), BF16), pltpu.VMEM((kvr, d), BF16), pltpu.VMEM((qr, d), BF16),
                        pltpu.VMEM((qr, d), BF16), pltpu.SemaphoreType.DMA((5 * N_CHIPS,))],
        compiler_params=_params(1),
    )(x1, wg, *consts, cos_t, sin_t)


def _causal_mask(row0, col0, nrows, ncols):
    rows = row0 + lax.broadcasted_iota(jnp.int32, (nrows, ncols), 0)
    cols = col0 + lax.broadcasted_iota(jnp.int32, (nrows, ncols), 1)
    return cols <= rows


def _attn_fwd(qn, qr, kn, kr, v, seq, ta):
    t_all, hv = qn.shape
    nheads, nb, na = hv // LANE, t_all // seq, seq // ta

    reps = ta // LANE
    hp = ATTN_HEADS
    wide = hp * LANE

    def body(qn_ref, qr_ref, kn_ref, kr_ref, v_ref, o_ref, lse_ref, m_s, l_s, acc_s):
        i = pl.program_id(2)
        m_s[...] = jnp.full((ta, wide), -1e30, F32)
        l_s[...] = jnp.zeros((ta, wide), F32)
        acc_s[...] = jnp.zeros((ta, wide), F32)
        heads = [slice(n * LANE, (n + 1) * LANE) for n in range(hp)]
        qs = [jnp.concatenate([qn_ref[:, hd], qr_ref[:, hd]], axis=1) for hd in heads]

        def tile(j, diagonal):
            cols = pl.ds(pl.multiple_of(j * ta, ta), ta)
            k_rope = kr_ref[cols, :]
            for q, hd in zip(qs, heads):
                k = jnp.concatenate([kn_ref[cols, hd], k_rope], axis=1)
                s = _dot_nt(q, k)
                if diagonal:
                    s = jnp.where(_causal_mask(0, 0, ta, ta), s, -1e30)
                m_prev = m_s[:, hd]
                m_new = jnp.maximum(m_prev, jnp.max(s, axis=1, keepdims=True))
                p = jnp.exp2(s - jnp.tile(m_new, (1, reps)))
                alpha = jnp.exp2(m_prev - m_new)
                l_s[:, hd] = alpha * l_s[:, hd] + jnp.sum(p, axis=1, keepdims=True)
                acc_s[:, hd] = alpha * acc_s[:, hd] + _dot(p.astype(BF16), v_ref[cols, hd])
                m_s[:, hd] = m_new

        def off_diagonal(j, carry):
            tile(j, False)
            return carry

        lax.fori_loop(0, i, off_diagonal, 0)
        tile(i, True)
        o_ref[...] = (acc_s[...] / l_s[...]).astype(BF16)
        lse_ref[...] = m_s[...] + jnp.log2(l_s[...])

    qspec = pl.BlockSpec((ta, wide), lambda b, h, i: (b * na + i, h))
    kspec = pl.BlockSpec((seq, wide), lambda b, h, i: (b, h))
    krspec = pl.BlockSpec((seq, LANE), lambda b, h, i: (b, 0))
    return pl.pallas_call(
        body, name="attn_fwd", grid=(nb, nheads // hp, na),
        in_specs=[qspec, qspec, kspec, krspec, kspec],
        out_specs=[qspec, qspec],
        out_shape=[_sds((t_all, hv), BF16), _sds((t_all, hv))],
        scratch_shapes=[pltpu.VMEM((ta, wide), F32)] * 3,
        compiler_params=_params(3),
    )(qn, qr, kn, kr, v)


def _attn_bwd(qn, qr, kn, kr, v, do, lse, delta, seq, ta):
    t_all, hv = qn.shape
    nheads, nb, na = hv // LANE, t_all // seq, seq // ta

    reps = ta // LANE
    nchunks = ta // ATTN_ROWS

    hp = ATTN_HEADS_BWD
    wide = hp * LANE
    heads = [slice(n * LANE, (n + 1) * LANE) for n in range(hp)]

    def body(qn_ref, qr_ref, kn_ref, kr_ref, v_ref, do_ref, lse_ref, dl_ref, dqn_out, dqr_out, dkn_ref, dkr_ref, dv_ref,
             s_s, dp_s, p_s, ds_s, dk_s, dv_s, dqn_ref, dqr_ref):
        j = pl.program_id(2)

        @pl.when(j == 0)
        def _():
            dqn_ref[...] = jnp.zeros((seq, wide), F32)
            dqr_ref[...] = jnp.zeros((seq, wide), F32)

        dk_s[...] = jnp.zeros((hp, ta, 2 * LANE), F32)
        dv_s[...] = jnp.zeros((hp, ta, LANE), F32)
        k_rope = kr_ref[...]
        ks = [jnp.concatenate([kn_ref[:, hd], k_rope], axis=1) for hd in heads]

        def tile(i, diagonal):
            rows_i = pl.ds(pl.multiple_of(i * ta, ta), ta)
            for n, hd in enumerate(heads):
                q = jnp.concatenate([qn_ref[rows_i, hd], qr_ref[rows_i, hd]], axis=1)
                do_b = do_ref[rows_i, hd]
                s_s[n] = _dot_nt(q, ks[n])
                dp_s[n] = _dot_nt(do_b, v_ref[:, hd])
                for c in range(nchunks):
                    rows = pl.ds(c * ATTN_ROWS, ATTN_ROWS)
                    seq_rows = pl.ds(pl.multiple_of(i * ta + c * ATTN_ROWS, ATTN_ROWS), ATTN_ROWS)
                    s = s_s[n, rows, :]
                    if diagonal:
                        s = jnp.where(_causal_mask(c * ATTN_ROWS, 0, ATTN_ROWS, ta), s, -1e30)
                    p = jnp.exp2(s - jnp.tile(lse_ref[seq_rows, hd], (1, reps)))
                    p_s[n, rows, :] = p.astype(BF16)
                    ds_s[n, rows, :] = (p * (dp_s[n, rows, :] - jnp.tile(dl_ref[seq_rows, hd], (1, reps)))).astype(BF16)
                dv_s[n] += _dot_tn(p_s[n], do_b)
                ds = ds_s[n]
                dk_s[n] += _dot_tn(ds, q)
                dq = _dot(ds, ks[n])
                dqn_ref[rows_i, hd] += dq[:, :LANE]
                dqr_ref[rows_i, hd] += dq[:, LANE:]

        def off_diagonal(i, carry):
            tile(i, False)
            return carry

        tile(j, True)
        lax.fori_loop(j + 1, na, off_diagonal, 0)
        for n, hd in enumerate(heads):
            dkn_ref[:, hd] = (dk_s[n, :, :LANE] * LN2).astype(BF16)
            dkr_ref[:, hd] = (dk_s[n, :, LANE:] * LN2).astype(BF16)
            dv_ref[:, hd] = dv_s[n].astype(BF16)

        @pl.when(j == na - 1)
        def _():
            dqn_out[...] = dqn_ref[...].astype(BF16)
            dqr_out[...] = dqr_ref[...].astype(BF16)

    qspec = pl.BlockSpec((seq, wide), lambda b, h, j: (b, h))
    kspec = pl.BlockSpec((ta, wide), lambda b, h, j: (b * na + j, h))
    krspec = pl.BlockSpec((ta, LANE), lambda b, h, j: (b * na + j, 0))
    return pl.pallas_call(
        body, name="attn_bwd", grid=(nb, nheads // hp, na),
        in_specs=[qspec, qspec, kspec, krspec, kspec, qspec, qspec, qspec],
        out_specs=[qspec, qspec, kspec, kspec, kspec],
        out_shape=[_sds((t_all, hv), BF16)] * 5,
        scratch_shapes=[pltpu.VMEM((hp, ta, ta), F32), pltpu.VMEM((hp, ta, ta), F32), pltpu.VMEM((hp, ta, ta), BF16), pltpu.VMEM((hp, ta, ta), BF16),
                        pltpu.VMEM((hp, ta, 2 * LANE), F32), pltpu.VMEM((hp, ta, LANE), F32), pltpu.VMEM((seq, wide), F32), pltpu.VMEM((seq, wide), F32)],
        compiler_params=_params(3),
    )(qn, qr, kn, kr, v, do, lse, delta)


def _head(o, ub, x1, target, wg, w, lay, tb):
    t_all, d = x1.shape
    hv, qr = lay.hv, lay.qr
    nheads = hv // LANE

    def body(o_ref, ub_ref, x1_ref, tg_ref, wg_ref, gf, loss_ref, dgf_ref, yb_ref, dx2_ref, do_ref, dg_ref, dl_ref, wob, sems):
        @pl.when(pl.program_id(0) == 0)
        def _():
            cps = _fetch(wg_ref, lay, "out_b", wob, sems, 0)
            for cp in cps:
                cp.start()
            loss_ref[...] = jnp.zeros((1, LANE), F32)
            dgf_ref[...] = jnp.zeros((1, d), F32)
            for cp in cps:
                cp.wait()

        ov = o_ref[...].astype(F32)
        g = ub_ref[:, qr:]
        sg = _sigmoid(g)
        silu = g * sg
        yb = (ov * silu).astype(BF16)
        yb_ref[...] = yb
        x2 = x1_ref[...] + _dot(yb, wob[...])
        rinv = _rinv(x2)
        err = x2 * rinv * gf[...] - tg_ref[...]
        loss_ref[...] += (0.5 / d) * jnp.sum(jnp.sum(err * err, axis=1, keepdims=True), axis=0, keepdims=True)
        dx2, dgf = _rms_bwd(x2, rinv, gf[...], err * (1.0 / d))
        dgf_ref[...] += dgf
        dx2_ref[...] = dx2
        dyb = _dot_nt(dx2.astype(BF16), wob[...])
        dov = dyb * silu
        do_ref[...] = dov.astype(BF16)
        dg_ref[...] = (dyb * ov * (sg * (1.0 + g * (1.0 - sg)))).astype(BF16)
        prod = dov * ov
        dl_ref[...] = jnp.concatenate(
            [jnp.broadcast_to(jnp.sum(prod[:, n * LANE:(n + 1) * LANE], axis=1, keepdims=True), (tb, LANE)) for n in range(nheads)], axis=1)

    tok = lambda c: pl.BlockSpec((tb, c), lambda i: (i, 0))
    return pl.pallas_call(
        body, name="head", grid=(t_all // tb,),
        in_specs=[tok(hv), tok(qr + hv), tok(d), tok(d), ANY, _full((1, d))],
        out_specs=[_full((1, LANE)), _full((1, d)), tok(hv), tok(d), tok(hv), tok(hv), tok(hv)],
        out_shape=[_sds((1, LANE)), _sds((1, d)), _sds((t_all, hv), BF16), _sds((t_all, d)), _sds((t_all, hv), BF16), _sds((t_all, hv), BF16),
                   _sds((t_all, hv))],
        scratch_shapes=[pltpu.VMEM((hv, d), BF16), pltpu.SemaphoreType.DMA((N_CHIPS,))],
        compiler_params=_params(1),
    )(o, ub, x1, target, wg, w["final_norm"])


def _fb_bwd(dqn, dqr, dkn, dkr, dv, dgate, ub, ckr, x1, dx2, wg, w, lay, cos_t, sin_t, seq, tb):
    t_all, d = x1.shape
    hv, qr, kvr = lay.hv, lay.qr, lay.kvr
    nheads = hv // LANE
    npos = seq // tb

    def body(dqn_ref, dqr_ref, dkn_ref, dkr_ref, dv_ref, dg_ref, ub_ref, ckr_ref, x1_ref, dx2_ref, wg_ref,
             qn, nb, kvn, wdkv, nkv, cos_ref, sin_ref,
             dx1_ref, dqrp_ref, dqnp_ref, dub_ref, dckr_ref, dqn_g, dnb_g, dkvn_g, dnkv_g, winb, wuk, wuv, wuqn, wuqr, sems):
        @pl.when(pl.program_id(0) == 0)
        def _():
            cps = []
            for n, (key, dst) in enumerate((("in_b", winb), ("uk", wuk), ("uv", wuv), ("uq_n", wuqn), ("uq_r", wuqr))):
                cps += _fetch(wg_ref, lay, key, dst, sems, n * N_CHIPS)
            for cp in cps:
                cp.start()
            dqn_g[...] = jnp.zeros((1, qr), F32)
            dnb_g[...] = jnp.zeros((1, d), F32)
            dkvn_g[...] = jnp.zeros((1, kvr), F32)
            dnkv_g[...] = jnp.zeros((1, d), F32)
            for cp in cps:
                cp.wait()

        cos, sin = cos_ref[...], sin_ref[...]
        xv = x1_ref[...]
        rinv1 = _rinv(xv)
        dqr_v = dqr_ref[...].astype(F32) * ATTN_SCALE
        dqr_pre = (dqr_v * jnp.tile(cos, (1, nheads)) + _swap_halves(dqr_v * jnp.tile(sin, (1, nheads)))).astype(BF16)
        dqrp_ref[...] = dqr_pre
        dqn_pre = (dqn_ref[...].astype(F32) * ATTN_SCALE).astype(BF16)
        dqnp_ref[...] = dqn_pre
        dcq = _dot_nt(dqn_pre, wuqn[...]) + _dot_nt(dqr_pre, wuqr[...])
        cq_pre = ub_ref[:, :qr]
        dcq_pre, g1 = _rms_bwd(cq_pre, _rinv(cq_pre), qn[...], dcq)
        dqn_g[...] += g1
        dub = jnp.concatenate([dcq_pre.astype(BF16), dg_ref[...]], axis=1)
        dub_ref[...] = dub
        dx1_b, g2 = _rms_bwd(xv, rinv1, nb[...], _dot(dub, winb[...]))
        dnb_g[...] += g2
        dkr_all = dkr_ref[...].astype(F32)
        dkr_sum = dkr_all[:, :LANE]
        for n in range(1, nheads):
            dkr_sum = dkr_sum + dkr_all[:, n * LANE:(n + 1) * LANE]
        dckr_rope = dkr_sum * cos + _swap_halves(dkr_sum * sin)
        dckv = _dot_nt(dkn_ref[...].astype(BF16), wuk[...]) + _dot_nt(dv_ref[...].astype(BF16), wuv[...])
        ckv_pre = ckr_ref[:, :kvr]
        dckv_pre, g3 = _rms_bwd(ckv_pre, _rinv(ckv_pre), kvn[...], dckv)
        dkvn_g[...] += g3
        dckr = jnp.concatenate([dckv_pre, dckr_rope], axis=1).astype(BF16)
        dckr_ref[...] = dckr
        dx1_kv, g4 = _rms_bwd(xv, rinv1, nkv[...], _dot_nt(dckr, wdkv[...]))
        dnkv_g[...] += g4
        dx1_ref[...] = dx2_ref[...] + dx1_b + dx1_kv

    tok = lambda c: pl.BlockSpec((tb, c), lambda i: (i, 0))
    pos = pl.BlockSpec((tb, LANE), lambda i: (i % npos, 0))
    consts = [w["q_norm"], w["norm_b"], w["kv_norm"], w["w_dkv_p"], w["norm_kv"]]
    return pl.pallas_call(
        body, name="fb_bwd", grid=(t_all // tb,),
        in_specs=[tok(hv)] * 6 + [tok(qr + hv), tok(kvr + LANE), tok(d), tok(d), ANY] + [_full(c.shape) for c in consts] + [pos, pos],
        out_specs=[tok(d), tok(hv), tok(hv), tok(qr + hv), tok(kvr + LANE), _full((1, qr)), _full((1, d)), _full((1, kvr)), _full((1, d))],
        out_shape=[_sds((t_all, d)), _sds((t_all, hv), BF16), _sds((t_all, hv), BF16), _sds((t_all, qr + hv), BF16), _sds((t_all, kvr + LANE), BF16),
                   _sds((1, qr)), _sds((1, d)), _sds((1, kvr)), _sds((1, d))],
        scratch_shapes=[pltpu.VMEM((qr + hv, d), BF16), pltpu.VMEM((kvr, d), BF16), pltpu.VMEM((kvr, d), BF16), pltpu.VMEM((qr, d), BF16),
                        pltpu.VMEM((qr, d), BF16), pltpu.SemaphoreType.DMA((5 * N_CHIPS,))],
        compiler_params=_params(1),
    )(dqn, dqr, dkn, dkr, dv, dgate, ub, ckr, x1, dx2, wg, *consts, cos_t, sin_t)


def _fa_bwd(dx1, x, u, xb, hs, wg, g16, g32, w, lay, seq, tb):
    t_all, d = x.shape
    dr = lay.dr
    nblocks = w["w_rg"].shape[0]
    nblk = seq // tb
    nt = tb // SUBLANE
    per8 = tb // 8

    def body(dx1_ref, x_ref, u_ref, xb_ref, hs_ref, hh_ref, wg_ref, g16_ref, g32_ref, na, cw, wrg, brg, wig, big, lam,
             gx_ref, du_ref, dna_g, dcw_g, dcb_g, dbrg_g, dbig_g, dlam_g, dwrg_g, dwig_g, got_ref, sib_ref, own_ref,
             wint, wout, hpad, a_s, d_s, g_s, dxpad, carry, sems, send_sems, recv_sems, local_sem):
        b, jj = pl.program_id(0), pl.program_id(1)
        first_block = jj == nblk - 1
        scatter = _ScatterDirect(g16_ref, g32_ref, got_ref, sib_ref, own_ref, send_sems, recv_sems, local_sem, lay, G_GROUPS["early"])

        @pl.when((b == 0) & (jj == 0))
        def _():
            scatter.start()
            cps = _fetch(wg_ref, lay, "in_a", wint, sems, 0) + _fetch(wg_ref, lay, "out_a", wout, sems, N_CHIPS)
            for cp in cps:
                cp.start()
            dna_g[...] = jnp.zeros((1, d), F32)
            dcw_g[...] = jnp.zeros((4, dr), F32)
            dcb_g[...] = jnp.zeros((1, dr), F32)
            dbrg_g[...] = jnp.zeros((1, dr), F32)
            dbig_g[...] = jnp.zeros((1, dr), F32)
            dlam_g[...] = jnp.zeros((1, dr), F32)
            dwrg_g[...] = jnp.zeros((nblocks, LANE, LANE), F32)
            dwig_g[...] = jnp.zeros((nblocks, LANE, LANE), F32)
            for cp in cps:
                cp.wait()

        @pl.when(jj == 0)
        def _():
            dxpad[pl.ds(tb, 8), :] = jnp.zeros((8, dr), F32)
            carry[...] = jnp.zeros((8, dr), F32)

        keep = jnp.where(first_block, 0.0, 1.0)
        dx1v = dx1_ref[...]
        gate = u_ref[:, dr:]
        xpre = u_ref[:, :dr]
        hpad[pl.ds(0, 8), :] = hh_ref[...] * keep
        hpad[pl.ds(8, tb), :] = hs_ref[...]
        xb = xb_ref[...]
        xbb = xb.astype(BF16)
        r, i = _gates(xb, wrg, brg[...], wig, big[...], nblocks)
        sp = _softplus(-lam[...])
        log_a = -LRU_C * r * sp
        a, a2, nem = _decay(log_a)
        mult = jnp.sqrt(nem)
        sg = _sigmoid(gate)
        dy = _dot_nt(dx1v.astype(BF16), wout[...])
        hsv = hs_ref[...]
        dgate = dy * hsv * (sg * (1.0 + gate * (1.0 - sg)))
        a_s[...] = a
        d_s[...] = dy * (gate * sg)
        row = lax.broadcasted_iota(jnp.int32, (8, dr), 0)

        def step(k, c):
            r0 = pl.multiple_of((nt - 1 - k) * 8, 8)
            av = a_s[pl.ds(r0, 8), :]
            dv = d_s[pl.ds(r0, 8), :]
            qv = av * dv
            for s in (1, 2, 4):
                m = row < 8 - s
                a_sh = jnp.where(m, pltpu.roll(av, 8 - s, 0), 1.0)
                q_sh = jnp.where(m, pltpu.roll(qv, 8 - s, 0), 0.0)
                qv = qv + av * q_sh
                av = av * a_sh
            qv = qv + av * c
            g_s[pl.ds(r0, 8), :] = dv + jnp.where(row < 7, pltpu.roll(qv, 7, 0), c)
            return jnp.broadcast_to(qv[0:1, :], qv.shape)

        carry[...] = lax.fori_loop(0, nt, step, carry[...])
        g = g_s[...]
        ix = i * xb
        dlog_a = g * (hpad[pl.ds(7, tb), :] * a - ix * (a2 * lax.rsqrt(nem)))
        dix = g * mult
        dlam_g[...] += -jax.nn.sigmoid(-lam[...]) * jnp.sum(dlog_a * (-LRU_C * r), axis=0, keepdims=True)
        drg = dlog_a * (-LRU_C * sp) * r * (1.0 - r)
        dig = dix * xb * i * (1.0 - i)
        dbrg_g[...] += jnp.sum(drg, axis=0, keepdims=True)
        dbig_g[...] += jnp.sum(dig, axis=0, keepdims=True)
        drgb, digb = drg.astype(BF16), dig.astype(BF16)
        back = []
        for n in range(nblocks):
            cols = slice(n * LANE, (n + 1) * LANE)
            dwrg_g[n] += _dot_tn(xbb[:, cols], drgb[:, cols])
            dwig_g[n] += _dot_tn(xbb[:, cols], digb[:, cols])
            back.append(_dot_nt(drgb[:, cols], wrg[n]) + _dot_nt(digb[:, cols], wig[n]))
        dxb = dix * i + jnp.concatenate(back, axis=1)
        dcb_g[...] += jnp.sum(dxb, axis=0, keepdims=True)
        dxpad[pl.ds(0, tb), :] = dxb
        later = [dxb, dxpad[pl.ds(1, tb), :], dxpad[pl.ds(2, tb), :], dxpad[pl.ds(3, tb), :]]
        dxpad[pl.ds(tb, 8), :] = dxb[:8, :]
        dxpre = cw[3:4, :] * later[0] + cw[2:3, :] * later[1] + cw[1:2, :] * later[2] + cw[0:1, :] * later[3]
        for m in range(4):
            dcw_g[3 - m:4 - m, :] += jnp.sum(later[m] * xpre, axis=0, keepdims=True)
        du = jnp.concatenate([dxpre, dgate], axis=1).astype(BF16)
        du_ref[...] = du
        xv = x_ref[...]
        dxa, g1 = _rms_bwd(xv, _rinv(xv), na[...], _dot(du, wint[...]))
        dna_g[...] += g1
        gx_ref[...] = dx1v + dxa

        @pl.when((b == t_all // seq - 1) & (jj == nblk - 1))
        def _():
            scatter.finish()

    blk = lambda b, j: b * nblk + (nblk - 1 - j)
    tok = lambda c: pl.BlockSpec((tb, c), lambda b, j: (blk(b, j), 0))
    halo = pl.BlockSpec((8, dr), lambda b, j: (jnp.maximum(blk(b, j) * per8 - 1, 0), 0))
    consts = [w["norm_a"], w["conv_w"], w["w_rg"], w["b_rg"], w["w_ig"], w["b_ig"], w["lru_lambda"]]
    vec = lambda c: _full((1, c))
    blocks3 = (nblocks, LANE, LANE)
    return pl.pallas_call(
        body, name="fa_bwd", grid=(t_all // seq, nblk),
        in_specs=[tok(d), tok(d), tok(2 * dr), tok(dr), tok(dr), halo, ANY, ANY, ANY] + [_full(c.shape) for c in consts],
        out_specs=[tok(d), tok(2 * dr), vec(d), _full((4, dr)), vec(dr), vec(dr), vec(dr), vec(dr), _full(blocks3), _full(blocks3), ANY, ANY, ANY],
        out_shape=[_sds((t_all, d)), _sds((t_all, 2 * dr), BF16), _sds((1, d)), _sds((4, dr)), _sds((1, dr)), _sds((1, dr)), _sds((1, dr)),
                   _sds((1, dr)), _sds(blocks3), _sds(blocks3)] + _scatter_direct_shapes(lay, "early"),
        scratch_shapes=[pltpu.VMEM((2 * dr, d), BF16), pltpu.VMEM((dr, d), BF16), pltpu.VMEM((tb + 8, dr), F32),
                        pltpu.VMEM((tb, dr), F32), pltpu.VMEM((tb, dr), F32), pltpu.VMEM((tb, dr), F32), pltpu.VMEM((tb + 8, dr), F32),
                        pltpu.VMEM((8, dr), F32), pltpu.SemaphoreType.DMA((2 * N_CHIPS,))] + SCATTER_DIRECT_SEMS,
        compiler_params=_params(2),
    )(dx1, x, u, xb, hs, hs, wg, g16, g32, *consts)


def _mm_into(gbufs, a, bs, offs, name, bt):
    t_all, m = a.shape
    n = bs[0].shape[1]
    nb = len(bs)
    nsplit = nb if nb > 1 else 2 if m >= 1024 and (m // 2) % LANE == 0 else 1
    mh = m if nb > 1 else m // nsplit
    starts = list(offs) if nb > 1 else [offs[0] + h * mh for h in range(nsplit)]
    nt = t_all // bt
    nbuf = len(gbufs)
    twin = nbuf == 2

    def body(a_ref, *refs):
        b_refs, outs, acc, sems = refs[:nb], refs[nb + nbuf:nb + 2 * nbuf], refs[nb + 2 * nbuf], refs[-1]
        acc16 = refs[nb + 2 * nbuf + 1] if twin else None
        part, t = pl.program_id(0), pl.program_id(1)

        def out_copies(h):
            dst = pl.ds(starts[h], mh)
            copies = [pltpu.make_async_copy(acc.at[h], outs[0].at[dst, :], sems.at[0, h])]
            if twin:
                copies.append(pltpu.make_async_copy(acc16.at[h], outs[1].at[dst, :], sems.at[1, h]))
            return copies

        for h in range(nsplit):
            @pl.when(part == h)
            def _():
                prod = _dot_tn(a_ref[...].astype(BF16), b_refs[h if nb > 1 else 0][...].astype(BF16))

                @pl.when(t == 0)
                def _():
                    acc[h] = prod

                @pl.when(t > 0)
                def _():
                    acc[h] += prod

                @pl.when(t == nt - 1)
                def _():
                    if twin:
                        acc16[h] = acc[h].astype(BF16)
                    for cp in out_copies(h):
                        cp.start()

        @pl.when((part == nsplit - 1) & (t == nt - 1))
        def _():
            for h in range(nsplit):
                for cp in out_copies(h):
                    cp.wait()

    if nb > 1:
        a_spec = pl.BlockSpec((bt, mh), lambda h, t: (t, 0))
        b_specs = [pl.BlockSpec((bt, n), lambda h, t, k=k: (jnp.where(h == k, t, 0), 0)) for k in range(nb)]
    else:
        a_spec = pl.BlockSpec((bt, mh), lambda h, t: (t, h))
        b_specs = [pl.BlockSpec((bt, n), lambda h, t: (t, 0))]
    scratch = [pltpu.VMEM((nsplit, mh, n), F32)] + ([pltpu.VMEM((nsplit, mh, n), BF16)] if twin else []) + [pltpu.SemaphoreType.DMA((2, nsplit))]
    return pl.pallas_call(
        body, name=name, grid=(nsplit, nt),
        in_specs=[a_spec] + b_specs + [ANY] * nbuf,
        out_specs=[ANY] * nbuf, out_shape=[_sds(g.shape, g.dtype) for g in gbufs], input_output_aliases={1 + nb + k: k for k in range(nbuf)},
        scratch_shapes=scratch, compiler_params=_params(2),
    )(a, *bs, *gbufs)


def _dw_in_a_exchange(du, h, rest, lay, bt):
    t_all, d = h.shape
    half = d // 2
    rows, rest_rows = lay.rows["in_a"], lay.rows["rest"]
    c_in, c_rest = lay.c_off["in_a"], lay.c_off["rest"]
    nt = t_all // bt
    xi, yi, _ = _place()
    order = jnp.stack([2 * (1 - xi) + yi, 2 * xi + (1 - yi), 2 * (1 - xi) + (1 - yi), 2 * xi + yi]).astype(jnp.int32)

    def body(order_ref, a_ref, b_ref, rest_ref, got_ref, own_ref, acc, sibbuf, part16, restv, rest_sib, rest_p, rest16, own_v, own_r,
             d2d_send, d2d_recv, ici_send, ici_recv, local_sems):
        x, y, c = _place()
        chips = [(1 - x, y), (x, 1 - y), (1 - x, 1 - y)]
        g, t = pl.program_id(0), pl.program_id(1)
        their_cols = pl.ds(pl.multiple_of((1 - c) * half, LANE), half)

        def my_half(v):
            return jnp.where(c == 0, v[:, :half], v[:, half:])

        def d2d(src, dst, k):
            return pltpu.make_async_remote_copy(src_ref=src, dst_ref=dst, send_sem=d2d_send.at[k], recv_sem=d2d_recv.at[k],
                                                device_id=(x, y, 1 - c), device_id_type=MESH)

        def group_swap(gg):
            return d2d(acc.at[gg % 2, :, their_cols], sibbuf.at[gg], gg)

        def rest_swap():
            return d2d(restv.at[:, their_cols], rest_sib, 4)

        def to_chip(k, src, off, nrows):
            px, py = chips[k]
            return pltpu.make_async_remote_copy(src_ref=src, dst_ref=got_ref.at[k, pl.ds(off, nrows), :], send_sem=ici_send.at[k],
                                                recv_sem=ici_recv.at[k], device_id=(px, py, c), device_id_type=MESH)

        def own_copy(src, off, nrows, k):
            return pltpu.make_async_copy(src, own_ref.at[pl.ds(off, nrows), :], local_sems.at[k])

        def finish_group(gg):
            group_swap(gg).wait()
            part = my_half(acc[gg % 2]) + sibbuf[gg]
            if gg < 3:
                part16[gg] = part.astype(BF16)
                to_chip(gg, part16.at[gg], c_in, rows).start()
            else:
                own_v[...] = part
                own_copy(own_v, c_in, rows, 1).start()

        @pl.when((g == 0) & (t == 0))
        def _():
            load = pltpu.make_async_copy(rest_ref, restv, local_sems.at[0])
            load.start()
            load.wait()
            rest_swap().start()

        @pl.when(t == 0)
        def _():
            acc[g % 2] = _dot_tn(a_ref[...], b_ref[...])

        @pl.when(t > 0)
        def _():
            acc[g % 2] += _dot_tn(a_ref[...], b_ref[...])

        for gg in range(4):
            if gg > 0:
                @pl.when((g == gg) & (t == 0))
                def _():
                    finish_group(gg - 1)

            @pl.when((g == gg) & (t == nt - 1))
            def _():
                group_swap(gg).start()
                if gg == 0:
                    rest_swap().wait()
                    rest_p[...] = my_half(restv[...]) + rest_sib[...]
                    for k in range(3):
                        chip_rows = pl.ds(pl.multiple_of(order_ref[k] * rest_rows, SUBLANE), rest_rows)
                        rest16[k] = rest_p[chip_rows, :].astype(BF16)
                        to_chip(k, rest16.at[k], c_rest, rest_rows).start()
                    own_r[...] = rest_p[pl.ds(pl.multiple_of(order_ref[3] * rest_rows, SUBLANE), rest_rows), :]
                    own_copy(own_r, c_rest, rest_rows, 2).start()
                if gg == 3:
                    finish_group(3)
                    for k, (px, py) in enumerate(chips):
                        pltpu.make_async_remote_copy(src_ref=got_ref.at[k], dst_ref=got_ref.at[k], send_sem=ici_send.at[k], recv_sem=ici_recv.at[k],
                                                     device_id=(px, py, c), device_id_type=MESH).wait()
                    own_copy(own_v, c_in, rows, 1).wait()
                    own_copy(own_r, c_rest, rest_rows, 2).wait()

    return pl.pallas_call(
        body, name="dw_in_a",
        grid_spec=pltpu.PrefetchScalarGridSpec(
            num_scalar_prefetch=1, grid=(N_CHIPS, nt),
            in_specs=[pl.BlockSpec((bt, rows), lambda g, t, order: (t, order[g])), pl.BlockSpec((bt, d), lambda g, t, order: (t, 0)), ANY],
            out_specs=[ANY, ANY],
            scratch_shapes=[pltpu.VMEM((2, rows, d), F32), pltpu.VMEM((N_CHIPS, rows, half), F32), pltpu.VMEM((3, rows, half), BF16),
                            pltpu.VMEM((N_CHIPS * rest_rows, d), F32), pltpu.VMEM((N_CHIPS * rest_rows, half), F32),
                            pltpu.VMEM((N_CHIPS * rest_rows, half), F32), pltpu.VMEM((3, rest_rows, half), BF16),
                            pltpu.VMEM((rows, half), F32), pltpu.VMEM((rest_rows, half), F32),
                            pltpu.SemaphoreType.DMA((5,)), pltpu.SemaphoreType.DMA((5,)), pltpu.SemaphoreType.DMA((3,)), pltpu.SemaphoreType.DMA((3,)),
                            pltpu.SemaphoreType.DMA((3,))]),
        out_shape=_scatter_shapes(lay, "late", half), compiler_params=_params(2),
    )(order, du, h, rest)


def _mm_tn(a, b, name, bt):
    t_all, m = a.shape
    n = b.shape[1]

    def body(a_ref, b_ref, o_ref):
        @pl.when(pl.program_id(0) == 0)
        def _():
            o_ref[...] = jnp.zeros((m, n), F32)

        o_ref[...] += _dot_tn(a_ref[...].astype(BF16), b_ref[...].astype(BF16))

    return pl.pallas_call(
        body, name=name, grid=(t_all // bt,),
        in_specs=[pl.BlockSpec((bt, m), lambda t: (t, 0)), pl.BlockSpec((bt, n), lambda t: (t, 0))],
        out_specs=_full((m, n)), out_shape=_sds((m, n)),
        compiler_params=_params(1),
    )(a, b)


class _Gather8:
    def __init__(self, x_ref, out_ref, send_sems, recv_sems, local_sem):
        x, y, c = _place()
        self.c, self.me, self.sibling = c, (x, y, c), (x, y, 1 - c)
        self.chips = [(1 - x, y), (x, 1 - y), (1 - x, 1 - y)]
        self.x_ref, self.out_ref, self.send_sems, self.recv_sems, self.local_sem = x_ref, out_ref, send_sems, recv_sems, local_sem

    def _slot(self, px, py, pc):
        return self.out_ref.at[4 * px + 2 * py + pc]

    def _copy(self, k, blk, to, src=None):
        return pltpu.make_async_remote_copy(
            src_ref=self._slot(*blk) if src is None else src, dst_ref=self._slot(*blk), send_sem=self.send_sems.at[k],
            recv_sem=self.recv_sems.at[k], device_id=to, device_id_type=MESH)

    def _mine(self):
        return pltpu.make_async_copy(self.x_ref, self._slot(*self.me), self.local_sem)

    def _first(self):
        return [self._copy(0, self.me, self.sibling, src=self.x_ref)] + [
            self._copy(1 + j, self.me, (*chip, self.c), src=self.x_ref) for j, chip in enumerate(self.chips)]

    def _passed(self):
        return [self._copy(4 + j, (*chip, self.c), self.sibling) for j, chip in enumerate(self.chips)]

    def start(self):
        self._mine().start()
        for cp in self._first():
            cp.start()

    def forward(self):
        passed = self._passed()
        for j, chip in enumerate(self.chips):
            self._copy(1 + j, (*chip, self.c), self.me).wait_recv()
            passed[j].start()

    def finish(self):
        self._copy(0, self.sibling, self.me).wait_recv()
        for j, chip in enumerate(self.chips):
            self._copy(4 + j, (*chip, 1 - self.c), self.me).wait_recv()
        for cp in self._first() + self._passed():
            cp.wait_send()
        self._mine().wait()


class _ScatterDirect:
    def __init__(self, g16_ref, g32_ref, got_ref, sib_ref, own_ref, send_sems, recv_sems, local_sem, lay, order):
        self.x, self.y, self.c = _place()
        self.chips = [(1 - self.x, self.y), (self.x, 1 - self.y), (1 - self.x, 1 - self.y)]
        self.refs = (g16_ref, g32_ref, got_ref, sib_ref, own_ref, send_sems, recv_sems, local_sem)
        self.lay, self.order, self.half = lay, order, lay.d // 2

    def _src(self, ref, key, chip, h):
        start = pl.multiple_of(self.lay.g_off[key] + chip * self.lay.rows[key], ROW_ALIGN)
        return ref.at[pl.ds(start, self.lay.rows[key]), pl.ds(pl.multiple_of(h * self.half, LANE), self.half)]

    def _compact(self, ref, key):
        return ref.at[pl.ds(self.lay.c_off[key], self.lay.rows[key]), :]

    def start(self):
        g16_ref, g32_ref, got_ref, sib_ref, own_ref, send_sems, recv_sems, local_sem = self.refs
        x, y, c = self.x, self.y, self.c
        for key in self.order:
            pltpu.make_async_copy(self._src(g32_ref, key, 2 * x + y, c), self._compact(own_ref, key), local_sem).start()
            pltpu.make_async_remote_copy(
                src_ref=self._src(g32_ref, key, 2 * x + y, 1 - c), dst_ref=self._compact(sib_ref, key), send_sem=send_sems.at[6],
                recv_sem=recv_sems.at[6], device_id=(x, y, 1 - c), device_id_type=MESH).start()
        for k, (px, py) in enumerate(self.chips):
            for h in range(2):
                for key in self.order:
                    pltpu.make_async_remote_copy(
                        src_ref=self._src(g16_ref, key, 2 * px + py, h), dst_ref=self._compact(got_ref.at[2 * k + c], key),
                        send_sem=send_sems.at[2 * k + h], recv_sem=recv_sems.at[2 * k + c], device_id=(px, py, h), device_id_type=MESH).start()

    def finish(self):
        _, _, got_ref, sib_ref, own_ref, send_sems, recv_sems, local_sem = self.refs
        x, y, c = self.x, self.y, self.c
        for k, (px, py) in enumerate(self.chips):
            for h in range(2):
                whole = pltpu.make_async_remote_copy(src_ref=got_ref.at[2 * k + h], dst_ref=got_ref.at[2 * k + h], send_sem=send_sems.at[2 * k + h],
                                                     recv_sem=recv_sems.at[2 * k + h], device_id=(px, py, h), device_id_type=MESH)
                whole.wait_send()
                whole.wait_recv()
        pltpu.make_async_remote_copy(src_ref=sib_ref, dst_ref=sib_ref, send_sem=send_sems.at[6], recv_sem=recv_sems.at[6],
                                     device_id=(x, y, 1 - c), device_id_type=MESH).wait()
        pltpu.make_async_copy(own_ref, own_ref, local_sem).wait()


def _scatter_direct_shapes(lay, group):
    rows, half = lay.c_rows[group], lay.d // 2
    return [_sds((6, rows, half), BF16), _sds((rows, half), F32), _sds((rows, half), F32)]


SCATTER_DIRECT_SEMS = [pltpu.SemaphoreType.DMA((7,)), pltpu.SemaphoreType.DMA((7,)), pltpu.SemaphoreType.DMA]
GATHER_SEMS = [pltpu.SemaphoreType.DMA((7,)), pltpu.SemaphoreType.DMA((7,)), pltpu.SemaphoreType.DMA]


def _all_gather8(blocks, name):
    nb = len(blocks)

    def body(*refs):
        x_refs, out_refs = refs[:nb], refs[nb:2 * nb]
        send_sems, recv_sems, local_sems = refs[2 * nb:]
        gathers = [_Gather8(x_refs[n], out_refs[n], send_sems.at[n], recv_sems.at[n], local_sems.at[n]) for n in range(nb)]
        for g in gathers:
            g.start()
        for g in gathers:
            g.forward()
        for g in gathers:
            g.finish()

    return pl.pallas_call(
        body, name=name, out_shape=[_sds((8,) + b.shape, b.dtype) for b in blocks], in_specs=[ANY] * nb, out_specs=[ANY] * nb,
        scratch_shapes=[pltpu.SemaphoreType.DMA((nb, 7)), pltpu.SemaphoreType.DMA((nb, 7)), pltpu.SemaphoreType.DMA((nb,))],
    )(*blocks)


def _return_and_gather(mines, rep_block):
    n = len(mines)

    def body(*refs):
        src_refs, rep_ref, out_refs, rep_out = refs[:n], refs[n], refs[n + 1:2 * n + 1], refs[2 * n + 1]
        send_sems, recv_sems, g_send, g_recv, g_local = refs[2 * n + 2:]
        x, y, c = _place()
        copies = [pltpu.make_async_remote_copy(src_ref=src_refs[k], dst_ref=out_refs[k], send_sem=send_sems.at[k], recv_sem=recv_sems.at[k],
                                               device_id=(x, y, 1 - c), device_id_type=MESH) for k in range(n)]
        gather = _Gather8(rep_ref, rep_out, g_send, g_recv, g_local)
        for cp in copies:
            cp.start()
        gather.start()
        gather.forward()
        gather.finish()
        for cp in copies:
            cp.wait()

    return pl.pallas_call(
        body, name="rs_return", out_shape=[_sds(m.shape, m.dtype) for m in mines] + [_sds((8,) + rep_block.shape, rep_block.dtype)],
        in_specs=[ANY] * (n + 1), out_specs=[ANY] * (n + 1),
        scratch_shapes=[pltpu.SemaphoreType.DMA((n,)), pltpu.SemaphoreType.DMA((n,))] + GATHER_SEMS,
    )(*mines, rep_block)


def _scatter_shapes(lay, group, half):
    return [_sds((3, lay.c_rows[group], half), BF16), _sds((lay.c_rows[group], half), F32)]


def _sum_devices(own, sib, got, name):
    rows, half = own.shape
    rb = _row_block(rows)
    n = got.shape[0]

    def body(a_ref, s_ref, b_ref, o_ref):
        acc = a_ref[...] + s_ref[...]
        for k in range(n):
            acc = acc + b_ref[k].astype(F32)
        o_ref[...] = acc

    spec = pl.BlockSpec((rb, half), lambda i: (i, 0))
    return pl.pallas_call(
        body, name=name, grid=(rows // rb,), in_specs=[spec, spec, pl.BlockSpec((n, rb, half), lambda i: (0, i, 0))], out_specs=spec,
        out_shape=_sds((rows, half)), compiler_params=_params(1),
    )(own, sib, got)


def _sum_chips(own, got, name):
    rows, half = own.shape
    rb = _row_block(rows)

    def body(a_ref, b_ref, o_ref):
        o_ref[...] = ((a_ref[...] + b_ref[0].astype(F32)) + b_ref[1].astype(F32)) + b_ref[2].astype(F32)

    spec = pl.BlockSpec((rb, half), lambda i: (i, 0))
    return pl.pallas_call(
        body, name=name, grid=(rows // rb,), in_specs=[spec, pl.BlockSpec((3, rb, half), lambda i: (0, i, 0))], out_specs=spec,
        out_shape=_sds((rows, half)), compiler_params=_params(1),
    )(own, got)


def _adamw(w, g, m, v):
    m = ADAM_B1 * m + (1.0 - ADAM_B1) * g
    v = ADAM_B2 * v + (1.0 - ADAM_B2) * (g * g)
    m_hat = m / (1.0 - ADAM_B1 ** ADAM_STEP)
    v_hat = v / (1.0 - ADAM_B2 ** ADAM_STEP)
    return -ADAM_LR * (m_hat / (jnp.sqrt(v_hat) + ADAM_EPS) + ADAM_WD * w), m, v


def _adamw_rows(name, w, g, m, v):
    _, rows, cols = w.shape
    rb = _row_block(rows, 256)

    def body(w_ref, g_ref, m_ref, v_ref, d_ref, mo_ref, vo_ref):
        d_ref[...], mo_ref[...], vo_ref[...] = _adamw(w_ref[...], g_ref[...], m_ref[...], v_ref[...])

    spec = pl.BlockSpec((1, rb, cols), lambda i: (0, i, 0))
    return pl.pallas_call(
        body, name=name, grid=(rows // rb,), in_specs=[spec] * 4, out_specs=[spec] * 3, out_shape=[_sds(w.shape)] * 3,
        compiler_params=_params(1),
    )(w, g, m, v)


def _adamw_group(ws, gs, ms, vs):
    n = len(ws)

    def body(*refs):
        for k in range(n):
            w_ref, g_ref, m_ref, v_ref = (refs[j * n + k] for j in range(4))
            outs = _adamw(w_ref[...], g_ref[...], m_ref[...], v_ref[...])
            for j in range(3):
                refs[(4 + j) * n + k][...] = outs[j]

    outs = pl.pallas_call(
        body, name="adamw_small", out_shape=[_sds(w.shape) for w in ws] * 3,
        compiler_params=pltpu.CompilerParams(vmem_limit_bytes=VMEM_LIMIT),
    )(*ws, *gs, *ms, *vs)
    return outs[:n], outs[n:2 * n], outs[2 * n:]


def _gather_weights(sh, lay):
    c = lax.axis_index("c")
    d = lay.d
    uq = sh["w_uq"][0].astype(BF16)
    parts = {
        "in_b": sh["w_in_b"][0].T.astype(BF16), "in_a": sh["w_in_a"][0].T.astype(BF16), "out_a": sh["w_out_a"][0].astype(BF16),
        "out_b": sh["w_out_b"][0].astype(BF16), "uk": sh["w_uk"].astype(BF16).reshape(-1, d), "uv": sh["w_uv"].astype(BF16).reshape(-1, d),
        "uq_n": uq[:, :, :QK_NOPE].reshape(-1, d), "uq_r": jnp.pad(uq[:, :, QK_NOPE:], ((0, 0), (0, 0), (0, LANE - QK_ROPE))).reshape(-1, d),
        "dkv": jnp.pad(sh["w_dkv"].astype(BF16), ((0, 0), (0, LANE - QK_ROPE))).reshape(-1, d),
    }
    halves = {}
    for group, order in W_GROUPS.items():
        stack = jnp.concatenate([parts[k] for k in order], axis=0).reshape(2, lay.w_rows[group] // 2, d)
        halves[group] = lax.dynamic_index_in_dim(stack, c, 0, keepdims=False)
    small = jnp.concatenate([sh[k].reshape(-1) for k in SMALL])
    n_small = small.shape[0]
    width = _round_up(n_small, 2 * SUBLANE * LANE) // (2 * SUBLANE)
    small = jnp.pad(small, (0, 2 * SUBLANE * width - n_small)).reshape(2, SUBLANE, width)
    wg, sg = _all_gather8([halves["a"], lax.dynamic_index_in_dim(small, c, 0, keepdims=False)], "ag_weights")
    wg = wg.reshape(N_CHIPS, lay.w_rows["a"], d)
    sg = sg.reshape(N_CHIPS, 2 * SUBLANE * width)
    full, off = {}, 0
    for k in SMALL:
        n = sh[k].size
        piece = sg[:, off:off + n]
        off += n
        if k == "conv_w":
            full[k] = piece.reshape(N_CHIPS, 4, n // 4).transpose(1, 0, 2).reshape(4, n)
        else:
            full[k] = piece.reshape(1, N_CHIPS * n)
    return wg, halves["b"], full


def _chip_split(g, taps=False):
    if taps:
        n = g.shape[1] // N_CHIPS
        return g.reshape(4, N_CHIPS, n).transpose(1, 0, 2).reshape(N_CHIPS, 4 * n)
    return g.reshape(N_CHIPS, -1)


def kernel(x, norm_a, w_in_a, conv_w, conv_b, w_rg, b_rg, w_ig, b_ig, lru_lambda, w_out_a, norm_kv, w_dkv, kv_norm, w_uk, w_uv, norm_b, w_in_b, q_norm, w_uq, w_out_b, final_norm, loss_target, m_norm_a, m_w_in_a, m_conv_w, m_conv_b, m_w_rg, m_b_rg, m_w_ig, m_b_ig, m_lru_lambda, m_w_out_a, m_norm_kv, m_w_dkv, m_kv_norm, m_w_uk, m_w_uv, m_norm_b, m_w_in_b, m_q_norm, m_w_uq, m_w_out_b, m_final_norm, v_norm_a, v_w_in_a, v_conv_w, v_conv_b, v_w_rg, v_b_rg, v_w_ig, v_b_ig, v_lru_lambda, v_w_out_a, v_norm_kv, v_w_dkv, v_kv_norm, v_w_uk, v_w_uv, v_norm_b, v_w_in_b, v_q_norm, v_w_uq, v_w_out_b, v_final_norm):
    given = dict(locals())
    sh = {k: given[k] for k in WEIGHTS}
    ci = lax.axis_index("c")
    nb, seq, d = x.shape
    t_all = nb * seq
    tb_a, tb_b, ta, bt = min(TOKENS_A, seq), min(TOKENS_B, seq), min(TOKENS_ATTN, seq), min(TOKENS_MM, t_all)
    dr = conv_b.shape[1] * N_CHIPS
    qr, kvr, nheads = q_norm.shape[1], kv_norm.shape[0], w_uk.shape[1]
    hv = nheads * LANE
    n_small = sum(sh[k].size for k in SMALL)
    n_repl = sum(sh[k].size for k in REPL)
    lay = _Layout(d, dr, qr, kvr, hv, n_small, n_repl)
    half = d // 2

    wga, wb_half, w = _gather_weights(sh, lay)
    w.update({"w_rg": w_rg[0].astype(BF16), "w_ig": w_ig[0].astype(BF16), "norm_kv": norm_kv[None, :],
              "kv_norm": kv_norm[None, :], "final_norm": final_norm[None, :], "norm_b": norm_b, "q_norm": q_norm})
    cos_t, sin_t = _rope_tables(seq)

    x0 = x.reshape(t_all, d)
    x1, u, hs, h, y, xb, wgb = _fa_fwd(x0, wga, wb_half, w, lay, seq, min(TOKENS_A_FWD, seq))
    wgb = wgb.reshape(N_CHIPS, lay.w_rows["b"], d)
    w["w_dkv_p"] = wgb[:, lay.w_off["dkv"]:lay.w_off["dkv"] + lay.rows["dkv"], :].reshape(d, kvr + LANE)
    qn, qrp, kn, kr, v, ub, ckr, hb, hk, cq, ckv = _fb_fwd(x1, wgb, w, lay, cos_t, sin_t, seq, tb_b)
    o, lse = _attn_fwd(qn, qrp, kn, kr, v, seq, ta)
    loss, g_final_norm, yb, dx2, do, dgate, delta = _head(o, ub, x1, loss_target.reshape(t_all, d), wgb, w, lay, tb_b)
    dqn, dqr, dkn, dkr, dv = _attn_bwd(qn, qrp, kn, kr, v, do, lse, delta, seq, ta)
    dx1, dqr_pre, dqn_pre, dub, dckr, g_q_norm, g_norm_b, g_kv_norm, g_norm_kv = _fb_bwd(
        dqn, dqr, dkn, dkr, dv, dgate, ub, ckr, x1, dx2, wgb, w, lay, cos_t, sin_t, seq, tb_b)
    loss = lax.psum(loss[0, 0], ("x", "y", "c"))

    gbufs = [lax.empty((lay.g_rows["early"], d), F32), lax.empty((lay.g_rows["early"], d), BF16)]
    for keys, a, bs in ((("in_b",), dub, (hb,)), (("out_a",), y, (dx1,)), (("out_b",), yb, (dx2,)), (("uk", "uv"), ckv, (dkn, dv)),
                        (("uq_n", "uq_r"), cq, (dqn_pre, dqr_pre))):
        gbufs = _mm_into(gbufs, a, bs, [lay.g_off[k] for k in keys], "dw_" + "_".join(keys), bt)
    g_dkv = _mm_tn(hk, dckr, "dw_dkv", bt)
    gx, du, g_norm_a, g_conv_w, g_conv_b, g_b_rg, g_b_ig, g_lam, g_w_rg, g_w_ig, others, sib, own = _fa_bwd(
        dx1, x0, u, xb, hs, wga, gbufs[1], gbufs[0], w, lay, seq, tb_a)
    mine_early = _sum_devices(own, sib, others, "rs_sum_early")

    small = jnp.concatenate([_chip_split(g_norm_a), _chip_split(g_conv_w, taps=True), _chip_split(g_conv_b), _chip_split(g_b_rg),
                             _chip_split(g_b_ig), _chip_split(g_lam)], axis=1)
    small = jnp.pad(small, ((0, 0), (0, lay.small_rows * d - small.shape[1]))).reshape(N_CHIPS, lay.small_rows, d)
    repl_parts = {"w_rg": g_w_rg, "w_ig": g_w_ig, "norm_kv": g_norm_kv, "kv_norm": g_kv_norm, "norm_b": g_norm_b, "q_norm": g_q_norm,
                  "final_norm": g_final_norm}
    repl = jnp.concatenate([repl_parts[k].reshape(-1) for k in REPL])
    repl = jnp.pad(repl, (0, N_CHIPS * lay.repl_rows * d - n_repl)).reshape(N_CHIPS, lay.repl_rows, d)
    pad_rows = lay.rows["rest"] - lay.rows["dkv"] - lay.small_rows - lay.repl_rows
    rest = jnp.concatenate([g_dkv.reshape(N_CHIPS, lay.rows["dkv"], d), small, repl, jnp.zeros((N_CHIPS, pad_rows, d), F32)], axis=1)
    others, own = _dw_in_a_exchange(du, h, rest.reshape(N_CHIPS * lay.rows["rest"], d), lay, bt)
    mine_late = _sum_chips(own, others, "rs_sum_chips_late")

    r0 = lay.c_off["rest"] + lay.rows["dkv"] + lay.small_rows
    theirs_early, theirs_late, rep_all = _return_and_gather([mine_early, mine_late], mine_late[r0:r0 + lay.repl_rows])
    red = {}
    for group, mine, theirs in (("early", mine_early, theirs_early), ("late", mine_late, theirs_late)):
        red[group] = jnp.concatenate([jnp.where(ci == 0, mine, theirs), jnp.where(ci == 0, theirs, mine)], axis=1)
    rep_flat =rep_all.reshape(N_CHIPS, 2, lay.repl_rows, half).transpose(0, 2, 1, 3).reshape(-1)

    def rows(key):
        group = "late" if key in G_GROUPS["late"] else "early"
        return red[group][lay.c_off[key]:lay.c_off[key] + lay.rows[key]]

    grads = {"w_in_b": rows("in_b").T[None], "w_in_a": rows("in_a").T[None], "w_out_a": rows("out_a")[None], "w_out_b": rows("out_b")[None],
             "w_uk": rows("uk").reshape(w_uk.shape), "w_uv": rows("uv").reshape(w_uv.shape)}
    uq_n = rows("uq_n").reshape(qr // N_CHIPS, nheads, LANE)
    uq_r = rows("uq_r").reshape(qr // N_CHIPS, nheads, LANE)[:, :, :QK_ROPE]
    grads["w_uq"] = jnp.concatenate([uq_n, uq_r], axis=2)[None]
    rest_red = rows("rest")
    grads["w_dkv"] = rest_red[:lay.rows["dkv"]].reshape(d // N_CHIPS, kvr + LANE)[:, :kvr + QK_ROPE]
    small_red = rest_red[lay.rows["dkv"]:lay.rows["dkv"] + lay.small_rows].reshape(-1)
    off = 0
    for k in SMALL:
        n = sh[k].size
        grads[k] = small_red[off:off + n].reshape(sh[k].shape)
        off += n
    off = 0
    for k in REPL:
        n = sh[k].size
        grads[k] = rep_flat[off:off + n].reshape(sh[k].shape)
        off += n

    new = {}
    for k in ("w_in_a", "w_in_b", "w_out_a", "w_out_b"):
        view = (lambda a: jnp.swapaxes(a, 1, 2)) if k in TRANSPOSED else (lambda a: a)
        outs = _adamw_rows("adamw_" + k, view(sh[k]), view(grads[k]), view(given["m_" + k]), view(given["v_" + k]))
        new[k] = tuple(view(a) for a in outs)
    rest_names = [k for k in WEIGHTS if k not in new]

    def as2d(k, a):
        return a.T if k in TRANSPOSED else a[None, :] if a.ndim == 1 else a

    ds, ms, vs = _adamw_group([as2d(k, sh[k]) for k in rest_names], [as2d(k, grads[k]) for k in rest_names],
                              [as2d(k, given["m_" + k]) for k in rest_names], [as2d(k, given["v_" + k]) for k in rest_names])
    for n, k in enumerate(rest_names):
        new[k] = tuple((a.T if k in TRANSPOSED else a).reshape(sh[k].shape) for a in (ds[n], ms[n], vs[n]))
    return (loss, gx.reshape(nb, seq, d), *[grads[k] for k in WEIGHTS], *[new[k][0] for k in WEIGHTS], *[new[k][1] for k in WEIGHTS],
            *[new[k][2] for k in WEIGHTS])
```

```python
import jax
import jax.numpy as jnp
from jax import lax
from jax.experimental import pallas as pl
from jax.experimental.pallas import tpu as pltpu

F32, BF16 = jnp.float32, jnp.bfloat16
EPS = 1e-6
LRU_C = 8.0
ROPE_THETA = 10000.0
QK_NOPE, QK_ROPE = 128, 64
ATTN_SCALE = (QK_NOPE + QK_ROPE) ** -0.5
LN2 = 0.6931471805599453
Q_SCALE = ATTN_SCALE / LN2
ATTN_HEADS, ATTN_HEADS_BWD = 4, 2
ATTN_ROWS = 64
LANE = 128
SUBLANE = 8
ROW_ALIGN = 32
VMEM_LIMIT = 60000 * 1024
ADAM_LR, ADAM_B1, ADAM_B2, ADAM_EPS, ADAM_WD, ADAM_STEP = 0.001, 0.9, 0.999, 1e-08, 0.01, 10
MESH = pl.DeviceIdType.MESH
ANY = pl.BlockSpec(memory_space=pl.ANY)
N_CHIPS = 4
TOKENS_A, TOKENS_B, TOKENS_ATTN, TOKENS_MM = 256, 512, 512, 2048
TOKENS_A_FWD = 512

SMALL = ("norm_a", "conv_w", "conv_b", "b_rg", "b_ig", "lru_lambda")
REPL = ("w_rg", "w_ig", "norm_kv", "kv_norm", "norm_b", "q_norm", "final_norm")
TRANSPOSED = ("w_in_b", "w_dkv")
WEIGHTS = ("norm_a", "w_in_a", "conv_w", "conv_b", "w_rg", "b_rg", "w_ig", "b_ig", "lru_lambda", "w_out_a", "norm_kv",
           "w_dkv", "kv_norm", "w_uk", "w_uv", "norm_b", "w_in_b", "q_norm", "w_uq", "w_out_b", "final_norm")
W_GROUPS = {"a": ("in_a", "out_a"), "b": ("in_b", "out_b", "uk", "uv", "uq_n", "uq_r", "dkv")}
G_GROUPS = {"early": ("in_b", "out_a", "out_b", "uk", "uv", "uq_n", "uq_r"), "late": ("in_a", "rest")}


def _sds(shape, dtype=F32):
    return jax.ShapeDtypeStruct(tuple(shape), dtype)


def _params(n_grid):
    return pltpu.CompilerParams(dimension_semantics=("arbitrary",) * n_grid, vmem_limit_bytes=VMEM_LIMIT)


def _full(shape):
    nd = len(shape)
    return pl.BlockSpec(tuple(shape), lambda *g: (0,) * nd)


def _round_up(n, k):
    return -(-n // k) * k


def _row_block(rows, cap=512):
    best = SUBLANE
    for r in range(SUBLANE, min(rows, cap) + 1, SUBLANE):
        if rows % r == 0:
            best = r
    return best


def _place():
    return lax.axis_index("x"), lax.axis_index("y"), lax.axis_index("c")


class _Layout:
    def __init__(self, d, dr, qr, kvr, hv, n_small, n_repl):
        assert hv == d, "the packed rows are D_MODEL wide, which must equal heads * 128"
        self.d, self.dr, self.qr, self.kvr, self.hv = d, dr, qr, kvr, hv
        per_chip = {"in_b": (qr + hv) // N_CHIPS, "in_a": 2 * dr // N_CHIPS, "out_a": dr // N_CHIPS, "out_b": hv // N_CHIPS,
                    "uk": kvr // N_CHIPS, "uv": kvr // N_CHIPS, "uq_n": qr // N_CHIPS, "uq_r": qr // N_CHIPS,
                    "dkv": (d // N_CHIPS) * (kvr + LANE) // d}
        assert all(r % ROW_ALIGN == 0 for r in per_chip.values()), per_chip
        self.small_rows = _round_up(-(-n_small // d), SUBLANE)
        self.repl_rows = _round_up(-(-n_repl // (N_CHIPS * d)), SUBLANE)
        per_chip["rest"] = _round_up(per_chip["dkv"] + self.small_rows + self.repl_rows, ROW_ALIGN)
        self.rows = per_chip
        self.w_off, self.w_rows = {}, {}
        for group, order in W_GROUPS.items():
            off = 0
            for k in order:
                self.w_off[k] = off
                off += per_chip[k]
            assert off % ROW_ALIGN == 0, (group, off)
            self.w_rows[group] = off
        self.g_off, self.c_off, self.c_rows, self.g_rows = {}, {}, {}, {}
        for group, order in G_GROUPS.items():
            off = 0
            for k in order:
                self.c_off[k] = off
                self.g_off[k] = N_CHIPS * off
                off += per_chip[k]
            self.c_rows[group] = off
            self.g_rows[group] = N_CHIPS * off


def _dot(a, b):
    return jnp.dot(a, b, preferred_element_type=F32)


def _dot_nt(a, b):
    return lax.dot_general(a, b, (((1,), (1,)), ((), ())), preferred_element_type=F32)


def _dot_tn(a, b):
    return lax.dot_general(a, b, (((0,), (0,)), ((), ())), preferred_element_type=F32)


def _rinv(x):
    return lax.rsqrt(jnp.mean(x * x, axis=-1, keepdims=True) + EPS)


def _rms_bwd(x, rinv, g, dy):
    z = dy * g
    dx = rinv * z - x * (rinv * rinv * rinv) * jnp.mean(z * x, axis=-1, keepdims=True)
    dg = jnp.sum(dy * (x * rinv), axis=0, keepdims=True)
    return dx, dg


def _softplus(z):
    return jnp.maximum(z, 0.0) + jnp.log1p(jnp.exp(-jnp.abs(z)))


def _sigmoid(x):
    return 0.5 * jnp.tanh(0.5 * x) + 0.5


def _decay(log_a):
    a = jnp.exp(log_a)
    a2 = a * a
    return a, a2, -jnp.tanh(log_a) * (a2 + 1.0)


def _swap_halves(x):
    w = x.shape[1]
    lane = lax.broadcasted_iota(jnp.int32, x.shape, 1)
    return jnp.where(lane % QK_ROPE < QK_ROPE // 2, pltpu.roll(x, w - QK_ROPE // 2, 1), pltpu.roll(x, QK_ROPE // 2, 1))


def _rope_tables(seq):
    pos = jnp.arange(seq, dtype=F32)
    inv = ROPE_THETA ** (-jnp.arange(0, QK_ROPE, 2, dtype=F32) / QK_ROPE)
    ang = pos[:, None] * inv[None, :]
    cos, sin = jnp.cos(ang), jnp.sin(ang)
    zero = jnp.zeros((seq, LANE - QK_ROPE), F32)
    return jnp.concatenate([cos, cos, zero], 1), jnp.concatenate([-sin, sin, zero], 1)


def _fetch(wg_ref, lay, key, dst, sems, k0):
    rows = lay.rows[key]
    return [pltpu.make_async_copy(wg_ref.at[p, pl.ds(lay.w_off[key], rows), :], dst.at[pl.ds(p * rows, rows), :], sems.at[k0 + p])
            for p in range(N_CHIPS)]


def _gates(xb, wrg_ref, brg, wig_ref, big, nblocks):
    xbb = xb.astype(BF16)
    rg = [_dot(xbb[:, n * LANE:(n + 1) * LANE], wrg_ref[n]) for n in range(nblocks)]
    ig = [_dot(xbb[:, n * LANE:(n + 1) * LANE], wig_ref[n]) for n in range(nblocks)]
    r = _sigmoid(jnp.concatenate(rg, axis=1) + brg)
    i = _sigmoid(jnp.concatenate(ig, axis=1) + big)
    return r, i


def _conv(xpad, cw_ref, cb, tb):
    return (cb + cw_ref[3:4, :] * xpad[pl.ds(8, tb), :] + cw_ref[2:3, :] * xpad[pl.ds(7, tb), :]
            + cw_ref[1:2, :] * xpad[pl.ds(6, tb), :] + cw_ref[0:1, :] * xpad[pl.ds(5, tb), :])


def _fa_fwd(x, wg, wb_half, w, lay, seq, tb):
    t_all, d = x.shape
    dr = lay.dr
    nblocks = w["w_rg"].shape[0]
    nblk = seq // tb
    nt = tb // SUBLANE
    nsteps = (t_all // seq) * nblk

    def body(x_ref, wg_ref, wbh_ref, na, cw, cb, wrg, brg, wig, big, lam, x1_ref, u_ref, hs_ref, h_ref, y_ref, xb_ref, wb_ref,
             wint, wout, xpad, a_s, b_s, carry, sems, send_sems, recv_sems, local_sem):
        step_no = pl.program_id(0) * nblk + pl.program_id(1)
        gather = _Gather8(wbh_ref, wb_ref, send_sems, recv_sems, local_sem)

        @pl.when(step_no == 0)
        def _():
            gather.start()
            cps = _fetch(wg_ref, lay, "in_a", wint, sems, 0) + _fetch(wg_ref, lay, "out_a", wout, sems, N_CHIPS)
            for cp in cps:
                cp.start()
            for cp in cps:
                cp.wait()

        @pl.when(step_no == nsteps // 2)
        def _():
            gather.forward()

        @pl.when(pl.program_id(1) == 0)
        def _():
            xpad[pl.ds(0, 8), :] = jnp.zeros((8, dr), F32)
            carry[...] = jnp.zeros((8, dr), F32)

        xv = x_ref[...]
        h = (xv * _rinv(xv) * na[...]).astype(BF16)
        h_ref[...] = h
        u = _dot_nt(h, wint[...])
        u_ref[...] = u
        xpre, gate = u[:, :dr], u[:, dr:]
        xpad[pl.ds(8, tb), :] = xpre
        xb = _conv(xpad, cw, cb[...], tb)
        xb_ref[...] = xb
        xpad[pl.ds(0, 8), :] = xpre[tb - 8:, :]
        r, i = _gates(xb, wrg, brg[...], wig, big[...], nblocks)
        log_a = -LRU_C * r * _softplus(-lam[...])
        a, _, nem = _decay(log_a)
        a_s[...] = a
        b_s[...] = jnp.sqrt(nem) * (i * xb)
        row = lax.broadcasted_iota(jnp.int32, (8, dr), 0)

        def step(t, c):
            r0 = pl.multiple_of(t * 8, 8)
            a = a_s[pl.ds(r0, 8), :]
            b = b_s[pl.ds(r0, 8), :]
            for s in (1, 2, 4):
                m = row >= s
                a_sh = jnp.where(m, pltpu.roll(a, s, 0), 1.0)
                b_sh = jnp.where(m, pltpu.roll(b, s, 0), 0.0)
                b = a * b_sh + b
                a = a * a_sh
            hh = b + a * c
            hs_ref[pl.ds(r0, 8), :] = hh
            return jnp.broadcast_to(hh[7:8, :], hh.shape)

        carry[...] = lax.fori_loop(0, nt, step, carry[...])
        y = (hs_ref[...] * (gate * _sigmoid(gate))).astype(BF16)
        y_ref[...] = y
        x1_ref[...] = xv + _dot(y, wout[...])

        @pl.when(step_no == nsteps - 1)
        def _():
            gather.finish()

    tok = lambda c: pl.BlockSpec((tb, c), lambda b, j: (b * nblk + j, 0))
    consts = [w["norm_a"], w["conv_w"], w["conv_b"], w["w_rg"], w["b_rg"], w["w_ig"], w["b_ig"], w["lru_lambda"]]
    return pl.pallas_call(
        body, name="fa_fwd", grid=(t_all // seq, nblk),
        in_specs=[tok(d), ANY, ANY] + [_full(c.shape) for c in consts],
        out_specs=[tok(d), tok(2 * dr), tok(dr), tok(d), tok(dr), tok(dr), ANY],
        out_shape=[_sds((t_all, d)), _sds((t_all, 2 * dr)), _sds((t_all, dr)), _sds((t_all, d), BF16), _sds((t_all, dr), BF16),
                   _sds((t_all, dr)), _sds((8,) + wb_half.shape, BF16)],
        scratch_shapes=[pltpu.VMEM((2 * dr, d), BF16), pltpu.VMEM((dr, d), BF16), pltpu.VMEM((tb + 8, dr), F32), pltpu.VMEM((tb, dr), F32),
                        pltpu.VMEM((tb, dr), F32), pltpu.VMEM((8, dr), F32), pltpu.SemaphoreType.DMA((2 * N_CHIPS,))] + GATHER_SEMS,
        compiler_params=_params(2),
    )(x, wg, wb_half, *consts)


def _fb_fwd(x1, wg, w, lay, cos_t, sin_t, seq, tb):
    t_all, d = x1.shape
    kvr, qr, hv = lay.kvr, lay.qr, lay.hv
    nheads = hv // LANE
    npos = seq // tb

    def body(x_ref, wg_ref, nkv, nb, wdkv, kvn, qn, cos_ref, sin_ref,
             qn_o, qr_o, kn_o, kr_o, v_o, ub_o, ckr_o, hb_o, hk_o, cq_o, ckv_o, winb, wuk, wuv, wuqn, wuqr, sems):
        @pl.when(pl.program_id(0) == 0)
        def _():
            cps = []
            for n, (key, dst) in enumerate((("in_b", winb), ("uk", wuk), ("uv", wuv), ("uq_n", wuqn), ("uq_r", wuqr))):
                cps += _fetch(wg_ref, lay, key, dst, sems, n * N_CHIPS)
            for cp in cps:
                cp.start()
            for cp in cps:
                cp.wait()

        xv = x_ref[...]
        xh = xv * _rinv(xv)
        hk = (xh * nkv[...]).astype(BF16)
        hb = (xh * nb[...]).astype(BF16)
        hk_o[...] = hk
        hb_o[...] = hb
        cos, sin = cos_ref[...], sin_ref[...]
        ckr = _dot(hk, wdkv[...])
        ckr_o[...] = ckr
        ckv_pre = ckr[:, :kvr]
        ckv = (ckv_pre * _rinv(ckv_pre) * kvn[...]).astype(BF16)
        ckv_o[...] = ckv
        kr = ckr[:, kvr:]
        kr_o[...] = (kr * cos + _swap_halves(kr) * sin).astype(BF16)
        kn_o[...] = _dot(ckv, wuk[...]).astype(BF16)
        v_o[...] = _dot(ckv, wuv[...]).astype(BF16)
        ub = _dot_nt(hb, winb[...])
        ub_o[...] = ub
        cq_pre = ub[:, :qr]
        cq = (cq_pre * _rinv(cq_pre) * qn[...]).astype(BF16)
        cq_o[...] = cq
        qn_o[...] = (_dot(cq, wuqn[...]) * Q_SCALE).astype(BF16)
        qrope = _dot(cq, wuqr[...]) * Q_SCALE
        qr_o[...] = (qrope * jnp.tile(cos, (1, nheads)) + _swap_halves(qrope) * jnp.tile(sin, (1, nheads))).astype(BF16)

    tok = lambda c: pl.BlockSpec((tb, c), lambda i: (i, 0))
    pos = pl.BlockSpec((tb, LANE), lambda i: (i % npos, 0))
    consts = [w["norm_kv"], w["norm_b"], w["w_dkv_p"], w["kv_norm"], w["q_norm"]]
    outs = [(hv, BF16), (hv, BF16), (hv, BF16), (LANE, BF16), (hv, BF16), (qr + hv, F32), (kvr + LANE, F32), (d, BF16), (d, BF16), (qr, BF16), (kvr, BF16)]
    return pl.pallas_call(
        body, name="fb_fwd", grid=(t_all // tb,),
        in_specs=[tok(d), ANY] + [_full(c.shape) for c in consts] + [pos, pos],
        out_specs=[tok(c) for c, _ in outs],
        out_shape=[_sds((t_all, c), dt) for c, dt in outs],
        scratch_shapes=[pltpu.VMEM((qr + hv, d), BF16), pltpu.VMEM((kvr, d), BF16), pltpu.VMEM((kvr, d), BF16), pltpu.VMEM((qr, d), BF16),
                        pltpu.VMEM((qr, d), BF16), pltpu.SemaphoreType.DMA((5 * N_CHIPS,))],
        compiler_params=_params(1),
    )(x1, wg, *consts, cos_t, sin_t)


def _causal_mask(row0, col0, nrows, ncols):
    rows = row0 + lax.broadcasted_iota(jnp.int32, (nrows, ncols), 0)
    cols = col0 + lax.broadcasted_iota(jnp.int32, (nrows, ncols), 1)
    return cols <= rows


def _attn_fwd(qn, qr, kn, kr, v, seq, ta):
    t_all, hv = qn.shape
    nheads, nb, na = hv // LANE, t_all // seq, seq // ta

    reps = ta // LANE
    hp = ATTN_HEADS
    wide = hp * LANE

    def body(qn_ref, qr_ref, kn_ref, kr_ref, v_ref, o_ref, lse_ref, m_s, l_s, acc_s):
        i = pl.program_id(2)
        m_s[...] = jnp.full((ta, wide), -1e30, F32)
        l_s[...] = jnp.zeros((ta, wide), F32)
        acc_s[...] = jnp.zeros((ta, wide), F32)
        heads = [slice(n * LANE, (n + 1) * LANE) for n in range(hp)]
        qs = [jnp.concatenate([qn_ref[:, hd], qr_ref[:, hd]], axis=1) for hd in heads]

        def tile(j, diagonal):
            cols = pl.ds(pl.multiple_of(j * ta, ta), ta)
            k_rope = kr_ref[cols, :]
            for q, hd in zip(qs, heads):
                k = jnp.concatenate([kn_ref[cols, hd], k_rope], axis=1)
                s = _dot_nt(q, k)
                if diagonal:
                    s = jnp.where(_causal_mask(0, 0, ta, ta), s, -1e30)
                m_prev = m_s[:, hd]
                m_new = jnp.maximum(m_prev, jnp.max(s, axis=1, keepdims=True))
                p = jnp.exp2(s - jnp.tile(m_new, (1, reps)))
                alpha = jnp.exp2(m_prev - m_new)
                l_s[:, hd] = alpha * l_s[:, hd] + jnp.sum(p, axis=1, keepdims=True)
                acc_s[:, hd] = alpha * acc_s[:, hd] + _dot(p.astype(BF16), v_ref[cols, hd])
                m_s[:, hd] = m_new

        def off_diagonal(j, carry):
            tile(j, False)
            return carry

        lax.fori_loop(0, i, off_diagonal, 0)
        tile(i, True)
        o_ref[...] = (acc_s[...] / l_s[...]).astype(BF16)
        lse_ref[...] = m_s[...] + jnp.log2(l_s[...])

    qspec = pl.BlockSpec((ta, wide), lambda b, h, i: (b * na + i, h))
    kspec = pl.BlockSpec((seq, wide), lambda b, h, i: (b, h))
    krspec = pl.BlockSpec((seq, LANE), lambda b, h, i: (b, 0))
    return pl.pallas_call(
        body, name="attn_fwd", grid=(nb, nheads // hp, na),
        in_specs=[qspec, qspec, kspec, krspec, kspec],
        out_specs=[qspec, qspec],
        out_shape=[_sds((t_all, hv), BF16), _sds((t_all, hv))],
        scratch_shapes=[pltpu.VMEM((ta, wide), F32)] * 3,
        compiler_params=_params(3),
    )(qn, qr, kn, kr, v)


def _attn_bwd(qn, qr, kn, kr, v, do, lse, delta, seq, ta):
    t_all, hv = qn.shape
    nheads, nb, na = hv // LANE, t_all // seq, seq // ta

    reps = ta // LANE
    nchunks = ta // ATTN_ROWS

    hp = ATTN_HEADS_BWD
    wide = hp * LANE
    heads = [slice(n * LANE, (n + 1) * LANE) for n in range(hp)]

    def body(qn_ref, qr_ref, kn_ref, kr_ref, v_ref, do_ref, lse_ref, dl_ref, dqn_out, dqr_out, dkn_ref, dkr_ref, dv_ref,
             s_s, dp_s, p_s, ds_s, dk_s, dv_s, dqn_ref, dqr_ref):
        j = pl.program_id(2)

        @pl.when(j == 0)
        def _():
            dqn_ref[...] = jnp.zeros((seq, wide), F32)
            dqr_ref[...] = jnp.zeros((seq, wide), F32)

        dk_s[...] = jnp.zeros((hp, ta, 2 * LANE), F32)
        dv_s[...] = jnp.zeros((hp, ta, LANE), F32)
        k_rope = kr_ref[...]
        ks = [jnp.concatenate([kn_ref[:, hd], k_rope], axis=1) for hd in heads]

        def tile(i, diagonal):
            rows_i = pl.ds(pl.multiple_of(i * ta, ta), ta)
            for n, hd in enumerate(heads):
                q = jnp.concatenate([qn_ref[rows_i, hd], qr_ref[rows_i, hd]], axis=1)
                do_b = do_ref[rows_i, hd]
                s_s[n] = _dot_nt(q, ks[n])
                dp_s[n] = _dot_nt(do_b, v_ref[:, hd])
                for c in range(nchunks):
                    rows = pl.ds(c * ATTN_ROWS, ATTN_ROWS)
                    seq_rows = pl.ds(pl.multiple_of(i * ta + c * ATTN_ROWS, ATTN_ROWS), ATTN_ROWS)
                    s = s_s[n, rows, :]
                    if diagonal:
                        s = jnp.where(_causal_mask(c * ATTN_ROWS, 0, ATTN_ROWS, ta), s, -1e30)
                    p = jnp.exp2(s - jnp.tile(lse_ref[seq_rows, hd], (1, reps)))
                    p_s[n, rows, :] = p.astype(BF16)
                    ds_s[n, rows, :] = (p * (dp_s[n, rows, :] - jnp.tile(dl_ref[seq_rows, hd], (1, reps)))).astype(BF16)
                dv_s[n] += _dot_tn(p_s[n], do_b)
                ds = ds_s[n]
                dk_s[n] += _dot_tn(ds, q)
                dq = _dot(ds, ks[n])
                dqn_ref[rows_i, hd] += dq[:, :LANE]
                dqr_ref[rows_i, hd] += dq[:, LANE:]

        def off_diagonal(i, carry):
            tile(i, False)
            return carry

        tile(j, True)
        lax.fori_loop(j + 1, na, off_diagonal, 0)
        for n, hd in enumerate(heads):
            dkn_ref[:, hd] = (dk_s[n, :, :LANE] * LN2).astype(BF16)
            dkr_ref[:, hd] = (dk_s[n, :, LANE:] * LN2).astype(BF16)
            dv_ref[:, hd] = dv_s[n].astype(BF16)

        @pl.when(j == na - 1)
        def _():
            dqn_out[...] = dqn_ref[...].astype(BF16)
            dqr_out[...] = dqr_ref[...].astype(BF16)

    qspec = pl.BlockSpec((seq, wide), lambda b, h, j: (b, h))
    kspec = pl.BlockSpec((ta, wide), lambda b, h, j: (b * na + j, h))
    krspec = pl.BlockSpec((ta, LANE), lambda b, h, j: (b * na + j, 0))
    return pl.pallas_call(
        body, name="attn_bwd", grid=(nb, nheads // hp, na),
        in_specs=[qspec, qspec, kspec, krspec, kspec, qspec, qspec, qspec],
        out_specs=[qspec, qspec, kspec, kspec, kspec],
        out_shape=[_sds((t_all, hv), BF16)] * 5,
        scratch_shapes=[pltpu.VMEM((hp, ta, ta), F32), pltpu.VMEM((hp, ta, ta), F32), pltpu.VMEM((hp, ta, ta), BF16), pltpu.VMEM((hp, ta, ta), BF16),
                        pltpu.VMEM((hp, ta, 2 * LANE), F32), pltpu.VMEM((hp, ta, LANE), F32), pltpu.VMEM((seq, wide), F32), pltpu.VMEM((seq, wide), F32)],
        compiler_params=_params(3),
    )(qn, qr, kn, kr, v, do, lse, delta)


def _head(o, ub, x1, target, wg, w, lay, tb):
    t_all, d = x1.shape
    hv, qr = lay.hv, lay.qr
    nheads = hv // LANE

    def body(o_ref, ub_ref, x1_ref, tg_ref, wg_ref, gf, loss_ref, dgf_ref, yb_ref, dx2_ref, do_ref, dg_ref, dl_ref, wob, sems):
        @pl.when(pl.program_id(0) == 0)
        def _():
            cps = _fetch(wg_ref, lay, "out_b", wob, sems, 0)
            for cp in cps:
                cp.start()
            loss_ref[...] = jnp.zeros((1, LANE), F32)
            dgf_ref[...] = jnp.zeros((1, d), F32)
            for cp in cps:
                cp.wait()

        ov = o_ref[...].astype(F32)
        g = ub_ref[:, qr:]
        sg = _sigmoid(g)
        silu = g * sg
        yb = (ov * silu).astype(BF16)
        yb_ref[...] = yb
        x2 = x1_ref[...] + _dot(yb, wob[...])
        rinv = _rinv(x2)
        err = x2 * rinv * gf[...] - tg_ref[...]
        loss_ref[...] += (0.5 / d) * jnp.sum(jnp.sum(err * err, axis=1, keepdims=True), axis=0, keepdims=True)
        dx2, dgf = _rms_bwd(x2, rinv, gf[...], err * (1.0 / d))
        dgf_ref[...] += dgf
        dx2_ref[...] = dx2
        dyb = _dot_nt(dx2.astype(BF16), wob[...])
        dov = dyb * silu
        do_ref[...] = dov.astype(BF16)
        dg_ref[...] = (dyb * ov * (sg * (1.0 + g * (1.0 - sg)))).astype(BF16)
        prod = dov * ov
        dl_ref[...] = jnp.concatenate(
            [jnp.broadcast_to(jnp.sum(prod[:, n * LANE:(n + 1) * LANE], axis=1, keepdims=True), (tb, LANE)) for n in range(nheads)], axis=1)

    tok = lambda c: pl.BlockSpec((tb, c), lambda i: (i, 0))
    return pl.pallas_call(
        body, name="head", grid=(t_all // tb,),
        in_specs=[tok(hv), tok(qr + hv), tok(d), tok(d), ANY, _full((1, d))],
        out_specs=[_full((1, LANE)), _full((1, d)), tok(hv), tok(d), tok(hv), tok(hv), tok(hv)],
        out_shape=[_sds((1, LANE)), _sds((1, d)), _sds((t_all, hv), BF16), _sds((t_all, d)), _sds((t_all, hv), BF16), _sds((t_all, hv), BF16),
                   _sds((t_all, hv))],
        scratch_shapes=[pltpu.VMEM((hv, d), BF16), pltpu.SemaphoreType.DMA((N_CHIPS,))],
        compiler_params=_params(1),
    )(o, ub, x1, target, wg, w["final_norm"])


def _fb_bwd(dqn, dqr, dkn, dkr, dv, dgate, ub, ckr, x1, dx2, wg, w, lay, cos_t, sin_t, seq, tb):
    t_all, d = x1.shape
    hv, qr, kvr = lay.hv, lay.qr, lay.kvr
    nheads = hv // LANE
    npos = seq // tb

    def body(dqn_ref, dqr_ref, dkn_ref, dkr_ref, dv_ref, dg_ref, ub_ref, ckr_ref, x1_ref, dx2_ref, wg_ref,
             qn, nb, kvn, wdkv, nkv, cos_ref, sin_ref,
             dx1_ref, dqrp_ref, dqnp_ref, dub_ref, dckr_ref, dqn_g, dnb_g, dkvn_g, dnkv_g, winb, wuk, wuv, wuqn, wuqr, sems):
        @pl.when(pl.program_id(0) == 0)
        def _():
            cps = []
            for n, (key, dst) in enumerate((("in_b", winb), ("uk", wuk), ("uv", wuv), ("uq_n", wuqn), ("uq_r", wuqr))):
                cps += _fetch(wg_ref, lay, key, dst, sems, n * N_CHIPS)
            for cp in cps:
                cp.start()
            dqn_g[...] = jnp.zeros((1, qr), F32)
            dnb_g[...] = jnp.zeros((1, d), F32)
            dkvn_g[...] = jnp.zeros((1, kvr), F32)
            dnkv_g[...] = jnp.zeros((1, d), F32)
            for cp in cps:
                cp.wait()

        cos, sin = cos_ref[...], sin_ref[...]
        xv = x1_ref[...]
        rinv1 = _rinv(xv)
        dqr_v = dqr_ref[...].astype(F32) * ATTN_SCALE
        dqr_pre = (dqr_v * jnp.tile(cos, (1, nheads)) + _swap_halves(dqr_v * jnp.tile(sin, (1, nheads)))).astype(BF16)
        dqrp_ref[...] = dqr_pre
        dqn_pre = (dqn_ref[...].astype(F32) * ATTN_SCALE).astype(BF16)
        dqnp_ref[...] = dqn_pre
        dcq = _dot_nt(dqn_pre, wuqn[...]) + _dot_nt(dqr_pre, wuqr[...])
        cq_pre = ub_ref[:, :qr]
        dcq_pre, g1 = _rms_bwd(cq_pre, _rinv(cq_pre), qn[...], dcq)
        dqn_g[...] += g1
        dub = jnp.concatenate([dcq_pre.astype(BF16), dg_ref[...]], axis=1)
        dub_ref[...] = dub
        dx1_b, g2 = _rms_bwd(xv, rinv1, nb[...], _dot(dub, winb[...]))
        dnb_g[...] += g2
        dkr_all = dkr_ref[...].astype(F32)
        dkr_sum = dkr_all[:, :LANE]
        for n in range(1, nheads):
            dkr_sum = dkr_sum + dkr_all[:, n * LANE:(n + 1) * LANE]
        dckr_rope = dkr_sum * cos + _swap_halves(dkr_sum * sin)
        dckv = _dot_nt(dkn_ref[...].astype(BF16), wuk[...]) + _dot_nt(dv_ref[...].astype(BF16), wuv[...])
        ckv_pre = ckr_ref[:, :kvr]
        dckv_pre, g3 = _rms_bwd(ckv_pre, _rinv(ckv_pre), kvn[...], dckv)
        dkvn_g[...] += g3
        dckr = jnp.concatenate([dckv_pre, dckr_rope], axis=1).astype(BF16)
        dckr_ref[...] = dckr
        dx1_kv, g4 = _rms_bwd(xv, rinv1, nkv[...], _dot_nt(dckr, wdkv[...]))
        dnkv_g[...] += g4
        dx1_ref[...] = dx2_ref[...] + dx1_b + dx1_kv

    tok = lambda c: pl.BlockSpec((tb, c), lambda i: (i, 0))
    pos = pl.BlockSpec((tb, LANE), lambda i: (i % npos, 0))
    consts = [w["q_norm"], w["norm_b"], w["kv_norm"], w["w_dkv_p"], w["norm_kv"]]
    return pl.pallas_call(
        body, name="fb_bwd", grid=(t_all // tb,),
        in_specs=[tok(hv)] * 6 + [tok(qr + hv), tok(kvr + LANE), tok(d), tok(d), ANY] + [_full(c.shape) for c in consts] + [pos, pos],
        out_specs=[tok(d), tok(hv), tok(hv), tok(qr + hv), tok(kvr + LANE), _full((1, qr)), _full((1, d)), _full((1, kvr)), _full((1, d))],
        out_shape=[_sds((t_all, d)), _sds((t_all, hv), BF16), _sds((t_all, hv), BF16), _sds((t_all, qr + hv), BF16), _sds((t_all, kvr + LANE), BF16),
                   _sds((1, qr)), _sds((1, d)), _sds((1, kvr)), _sds((1, d))],
        scratch_shapes=[pltpu.VMEM((qr + hv, d), BF16), pltpu.VMEM((kvr, d), BF16), pltpu.VMEM((kvr, d), BF16), pltpu.VMEM((qr, d), BF16),
                        pltpu.VMEM((qr, d), BF16), pltpu.SemaphoreType.DMA((5 * N_CHIPS,))],
        compiler_params=_params(1),
    )(dqn, dqr, dkn, dkr, dv, dgate, ub, ckr, x1, dx2, wg, *consts, cos_t, sin_t)


def _fa_bwd(dx1, x, u, xb, hs, wg, g16, g32, w, lay, seq, tb):
    t_all, d = x.shape
    dr = lay.dr
    nblocks = w["w_rg"].shape[0]
    nblk = seq // tb
    nt = tb // SUBLANE
    per8 = tb // 8

    def body(dx1_ref, x_ref, u_ref, xb_ref, hs_ref, hh_ref, wg_ref, g16_ref, g32_ref, na, cw, wrg, brg, wig, big, lam,
             gx_ref, du_ref, dna_g, dcw_g, dcb_g, dbrg_g, dbig_g, dlam_g, dwrg_g, dwig_g, got_ref, sib_ref, own_ref,
             wint, wout, hpad, a_s, d_s, g_s, dxpad, carry, sems, send_sems, recv_sems, local_sem):
        b, jj = pl.program_id(0), pl.program_id(1)
        first_block = jj == nblk - 1
        scatter = _ScatterDirect(g16_ref, g32_ref, got_ref, sib_ref, own_ref, send_sems, recv_sems, local_sem, lay, G_GROUPS["early"])

        @pl.when((b == 0) & (jj == 0))
        def _():
            scatter.start()
            cps = _fetch(wg_ref, lay, "in_a", wint, sems, 0) + _fetch(wg_ref, lay, "out_a", wout, sems, N_CHIPS)
            for cp in cps:
                cp.start()
            dna_g[...] = jnp.zeros((1, d), F32)
            dcw_g[...] = jnp.zeros((4, dr), F32)
            dcb_g[...] = jnp.zeros((1, dr), F32)
            dbrg_g[...] = jnp.zeros((1, dr), F32)
            dbig_g[...] = jnp.zeros((1, dr), F32)
            dlam_g[...] = jnp.zeros((1, dr), F32)
            dwrg_g[...] = jnp.zeros((nblocks, LANE, LANE), F32)
            dwig_g[...] = jnp.zeros((nblocks, LANE, LANE), F32)
            for cp in cps:
                cp.wait()

        @pl.when(jj == 0)
        def _():
            dxpad[pl.ds(tb, 8), :] = jnp.zeros((8, dr), F32)
            carry[...] = jnp.zeros((8, dr), F32)

        keep = jnp.where(first_block, 0.0, 1.0)
        dx1v = dx1_ref[...]
        gate = u_ref[:, dr:]
        xpre = u_ref[:, :dr]
        hpad[pl.ds(0, 8), :] = hh_ref[...] * keep
        hpad[pl.ds(8, tb), :] = hs_ref[...]
        xb = xb_ref[...]
        xbb = xb.astype(BF16)
        r, i = _gates(xb, wrg, brg[...], wig, big[...], nblocks)
        sp = _softplus(-lam[...])
        log_a = -LRU_C * r * sp
        a, a2, nem = _decay(log_a)
        mult = jnp.sqrt(nem)
        sg = _sigmoid(gate)
        dy = _dot_nt(dx1v.astype(BF16), wout[...])
        hsv = hs_ref[...]
        dgate = dy * hsv * (sg * (1.0 + gate * (1.0 - sg)))
        a_s[...] = a
        d_s[...] = dy * (gate * sg)
        row = lax.broadcasted_iota(jnp.int32, (8, dr), 0)

        def step(k, c):
            r0 = pl.multiple_of((nt - 1 - k) * 8, 8)
            av = a_s[pl.ds(r0, 8), :]
            dv = d_s[pl.ds(r0, 8), :]
            qv = av * dv
            for s in (1, 2, 4):
                m = row < 8 - s
                a_sh = jnp.where(m, pltpu.roll(av, 8 - s, 0), 1.0)
                q_sh = jnp.where(m, pltpu.roll(qv, 8 - s, 0), 0.0)
                qv = qv + av * q_sh
                av = av * a_sh
            qv = qv + av * c
            g_s[pl.ds(r0, 8), :] = dv + jnp.where(row < 7, pltpu.roll(qv, 7, 0), c)
            return jnp.broadcast_to(qv[0:1, :], qv.shape)

        carry[...] = lax.fori_loop(0, nt, step, carry[...])
        g = g_s[...]
        ix = i * xb
        dlog_a = g * (hpad[pl.ds(7, tb), :] * a - ix * (a2 * lax.rsqrt(nem)))
        dix = g * mult
        dlam_g[...] += -jax.nn.sigmoid(-lam[...]) * jnp.sum(dlog_a * (-LRU_C * r), axis=0, keepdims=True)
        drg = dlog_a * (-LRU_C * sp) * r * (1.0 - r)
        dig = dix * xb * i * (1.0 - i)
        dbrg_g[...] += jnp.sum(drg, axis=0, keepdims=True)
        dbig_g[...] += jnp.sum(dig, axis=0, keepdims=True)
        drgb, digb = drg.astype(BF16), dig.astype(BF16)
        back = []
        for n in range(nblocks):
            cols = slice(n * LANE, (n + 1) * LANE)
            dwrg_g[n] += _dot_tn(xbb[:, cols], drgb[:, cols])
            dwig_g[n] += _dot_tn(xbb[:, cols], digb[:, cols])
            back.append(_dot_nt(drgb[:, cols], wrg[n]) + _dot_nt(digb[:, cols], wig[n]))
        dxb = dix * i + jnp.concatenate(back, axis=1)
        dcb_g[...] += jnp.sum(dxb, axis=0, keepdims=True)
        dxpad[pl.ds(0, tb), :] = dxb
        later = [dxb, dxpad[pl.ds(1, tb), :], dxpad[pl.ds(2, tb), :], dxpad[pl.ds(3, tb), :]]
        dxpad[pl.ds(tb, 8), :] = dxb[:8, :]
        dxpre = cw[3:4, :] * later[0] + cw[2:3, :] * later[1] + cw[1:2, :] * later[2] + cw[0:1, :] * later[3]
        for m in range(4):
            dcw_g[3 - m:4 - m, :] += jnp.sum(later[m] * xpre, axis=0, keepdims=True)
        du = jnp.concatenate([dxpre, dgate], axis=1).astype(BF16)
        du_ref[...] = du
        xv = x_ref[...]
        dxa, g1 = _rms_bwd(xv, _rinv(xv), na[...], _dot(du, wint[...]))
        dna_g[...] += g1
        gx_ref[...] = dx1v + dxa

        @pl.when((b == t_all // seq - 1) & (jj == nblk - 1))
        def _():
            scatter.finish()

    blk = lambda b, j: b * nblk + (nblk - 1 - j)
    tok = lambda c: pl.BlockSpec((tb, c), lambda b, j: (blk(b, j), 0))
    halo = pl.BlockSpec((8, dr), lambda b, j: (jnp.maximum(blk(b, j) * per8 - 1, 0), 0))
    consts = [w["norm_a"], w["conv_w"], w["w_rg"], w["b_rg"], w["w_ig"], w["b_ig"], w["lru_lambda"]]
    vec = lambda c: _full((1, c))
    blocks3 = (nblocks, LANE, LANE)
    return pl.pallas_call(
        body, name="fa_bwd", grid=(t_all // seq, nblk),
        in_specs=[tok(d), tok(d), tok(2 * dr), tok(dr), tok(dr), halo, ANY, ANY, ANY] + [_full(c.shape) for c in consts],
        out_specs=[tok(d), tok(2 * dr), vec(d), _full((4, dr)), vec(dr), vec(dr), vec(dr), vec(dr), _full(blocks3), _full(blocks3), ANY, ANY, ANY],
        out_shape=[_sds((t_all, d)), _sds((t_all, 2 * dr), BF16), _sds((1, d)), _sds((4, dr)), _sds((1, dr)), _sds((1, dr)), _sds((1, dr)),
                   _sds((1, dr)), _sds(blocks3), _sds(blocks3)] + _scatter_direct_shapes(lay, "early"),
        scratch_shapes=[pltpu.VMEM((2 * dr, d), BF16), pltpu.VMEM((dr, d), BF16), pltpu.VMEM((tb + 8, dr), F32),
                        pltpu.VMEM((tb, dr), F32), pltpu.VMEM((tb, dr), F32), pltpu.VMEM((tb, dr), F32), pltpu.VMEM((tb + 8, dr), F32),
                        pltpu.VMEM((8, dr), F32), pltpu.SemaphoreType.DMA((2 * N_CHIPS,))] + SCATTER_DIRECT_SEMS,
        compiler_params=_params(2),
    )(dx1, x, u, xb, hs, hs, wg, g16, g32, *consts)


def _mm_into(gbufs, a, bs, offs, name, bt):
    t_all, m = a.shape
    n = bs[0].shape[1]
    nb = len(bs)
    nsplit = nb if nb > 1 else 2 if m >= 1024 and (m // 2) % LANE == 0 else 1
    mh = m if nb > 1 else m // nsplit
    starts = list(offs) if nb > 1 else [offs[0] + h * mh for h in range(nsplit)]
    nt = t_all // bt
    nbuf = len(gbufs)
    twin = nbuf == 2

    def body(a_ref, *refs):
        b_refs, outs, acc, sems = refs[:nb], refs[nb + nbuf:nb + 2 * nbuf], refs[nb + 2 * nbuf], refs[-1]
        acc16 = refs[nb + 2 * nbuf + 1] if twin else None
        part, t = pl.program_id(0), pl.program_id(1)

        def out_copies(h):
            dst = pl.ds(starts[h], mh)
            copies = [pltpu.make_async_copy(acc.at[h], outs[0].at[dst, :], sems.at[0, h])]
            if twin:
                copies.append(pltpu.make_async_copy(acc16.at[h], outs[1].at[dst, :], sems.at[1, h]))
            return copies

        for h in range(nsplit):
            @pl.when(part == h)
            def _():
                prod = _dot_tn(a_ref[...].astype(BF16), b_refs[h if nb > 1 else 0][...].astype(BF16))

                @pl.when(t == 0)
                def _():
                    acc[h] = prod

                @pl.when(t > 0)
                def _():
                    acc[h] += prod

                @pl.when(t == nt - 1)
                def _():
                    if twin:
                        acc16[h] = acc[h].astype(BF16)
                    for cp in out_copies(h):
                        cp.start()

        @pl.when((part == nsplit - 1) & (t == nt - 1))
        def _():
            for h in range(nsplit):
                for cp in out_copies(h):
                    cp.wait()

    if nb > 1:
        a_spec = pl.BlockSpec((bt, mh), lambda h, t: (t, 0))
        b_specs = [pl.BlockSpec((bt, n), lambda h, t, k=k: (jnp.where(h == k, t, 0), 0)) for k in range(nb)]
    else:
        a_spec = pl.BlockSpec((bt, mh), lambda h, t: (t, h))
        b_specs = [pl.BlockSpec((bt, n), lambda h, t: (t, 0))]
    scratch = [pltpu.VMEM((nsplit, mh, n), F32)] + ([pltpu.VMEM((nsplit, mh, n), BF16)] if twin else []) + [pltpu.SemaphoreType.DMA((2, nsplit))]
    return pl.pallas_call(
        body, name=name, grid=(nsplit, nt),
        in_specs=[a_spec] + b_specs + [ANY] * nbuf,
        out_specs=[ANY] * nbuf, out_shape=[_sds(g.shape, g.dtype) for g in gbufs], input_output_aliases={1 + nb + k: k for k in range(nbuf)},
        scratch_shapes=scratch, compiler_params=_params(2),
    )(a, *bs, *gbufs)


def _dw_in_a_exchange(du, h, rest, lay, bt):
    t_all, d = h.shape
    half = d // 2
    rows, rest_rows = lay.rows["in_a"], lay.rows["rest"]
    c_in, c_rest = lay.c_off["in_a"], lay.c_off["rest"]
    nt = t_all // bt
    xi, yi, _ = _place()
    order = jnp.stack([2 * (1 - xi) + (1 - yi), 2 * (1 - xi) + yi, 2 * xi + (1 - yi), 2 * xi + yi]).astype(jnp.int32)

    def body(order_ref, a_ref, b_ref, rest_ref, got_ref, own_ref, acc, sibbuf, part16, restv, rest_sib, rest_p, rest16, own_v, own_r,
             d2d_send, d2d_recv, ici_send, ici_recv, local_sems):
        x, y, c = _place()
        chips = [(1 - x, 1 - y), (1 - x, y), (x, 1 - y)]
        g, t = pl.program_id(0), pl.program_id(1)
        their_cols = pl.ds(pl.multiple_of((1 - c) * half, LANE), half)

        def my_half(v):
            return jnp.where(c == 0, v[:, :half], v[:, half:])

        def d2d(src, dst, k):
            return pltpu.make_async_remote_copy(src_ref=src, dst_ref=dst, send_sem=d2d_send.at[k], recv_sem=d2d_recv.at[k],
                                                device_id=(x, y, 1 - c), device_id_type=MESH)

        def group_swap(gg):
            return d2d(acc.at[gg % 2, :, their_cols], sibbuf.at[gg], gg)

        def rest_swap():
            return d2d(restv.at[:, their_cols], rest_sib, 4)

        def to_chip(k, src, off, nrows):
            px, py = chips[k]
            return pltpu.make_async_remote_copy(src_ref=src, dst_ref=got_ref.at[k, pl.ds(off, nrows), :], send_sem=ici_send.at[k],
                                                recv_sem=ici_recv.at[k], device_id=(px, py, c), device_id_type=MESH)

        def own_copy(src, off, nrows, k):
            return pltpu.make_async_copy(src, own_ref.at[pl.ds(off, nrows), :], local_sems.at[k])

        def finish_group(gg):
            group_swap(gg).wait()
            part = my_half(acc[gg % 2]) + sibbuf[gg]
            if gg < 3:
                part16[gg] = part.astype(BF16)
                to_chip(gg, part16.at[gg], c_in, rows).start()
            else:
                own_v[...] = part
                own_copy(own_v, c_in, rows, 1).start()

        @pl.when((g == 0) & (t == 0))
        def _():
            load = pltpu.make_async_copy(rest_ref, restv, local_sems.at[0])
            load.start()
            load.wait()
            rest_swap().start()

        @pl.when(t == 0)
        def _():
            acc[g % 2] = _dot_tn(a_ref[...], b_ref[...])

        @pl.when(t > 0)
        def _():
            acc[g % 2] += _dot_tn(a_ref[...], b_ref[...])

        for gg in range(4):
            @pl.when((g == gg) & (t == 0))
            def _():
                if gg == 0:
                    rest_swap().wait()
                    rest_p[...] = my_half(restv[...]) + rest_sib[...]
                    for k in range(3):
                        chip_rows = pl.ds(pl.multiple_of(order_ref[k] * rest_rows, SUBLANE), rest_rows)
                        rest16[k] = rest_p[chip_rows, :].astype(BF16)
                        to_chip(k, rest16.at[k], c_rest, rest_rows).start()
                    own_r[...] = rest_p[pl.ds(pl.multiple_of(order_ref[3] * rest_rows, SUBLANE), rest_rows), :]
                    own_copy(own_r, c_rest, rest_rows, 2).start()
                else:
                    finish_group(gg - 1)

            @pl.when((g == gg) & (t == nt - 1))
            def _():
                group_swap(gg).start()
                if gg == 3:
                    finish_group(3)
                    for k, (px, py) in enumerate(chips):
                        pltpu.make_async_remote_copy(src_ref=got_ref.at[k], dst_ref=got_ref.at[k], send_sem=ici_send.at[k], recv_sem=ici_recv.at[k],
                                                     device_id=(px, py, c), device_id_type=MESH).wait()
                    own_copy(own_v, c_in, rows, 1).wait()
                    own_copy(own_r, c_rest, rest_rows, 2).wait()

    return pl.pallas_call(
        body, name="dw_in_a",
        grid_spec=pltpu.PrefetchScalarGridSpec(
            num_scalar_prefetch=1, grid=(N_CHIPS, nt),
            in_specs=[pl.BlockSpec((bt, rows), lambda g, t, order: (t, order[g])), pl.BlockSpec((bt, d), lambda g, t, order: (t, 0)), ANY],
            out_specs=[ANY, ANY],
            scratch_shapes=[pltpu.VMEM((2, rows, d), F32), pltpu.VMEM((N_CHIPS, rows, half), F32), pltpu.VMEM((3, rows, half), BF16),
                            pltpu.VMEM((N_CHIPS * rest_rows, d), F32), pltpu.VMEM((N_CHIPS * rest_rows, half), F32),
                            pltpu.VMEM((N_CHIPS * rest_rows, half), F32), pltpu.VMEM((3, rest_rows, half), BF16),
                            pltpu.VMEM((rows, half), F32), pltpu.VMEM((rest_rows, half), F32),
                            pltpu.SemaphoreType.DMA((5,)), pltpu.SemaphoreType.DMA((5,)), pltpu.SemaphoreType.DMA((3,)), pltpu.SemaphoreType.DMA((3,)),
                            pltpu.SemaphoreType.DMA((3,))]),
        out_shape=_scatter_shapes(lay, "late", half), compiler_params=_params(2),
    )(order, du, h, rest)


def _mm_tn(a, b, name, bt):
    t_all, m = a.shape
    n = b.shape[1]

    def body(a_ref, b_ref, o_ref):
        @pl.when(pl.program_id(0) == 0)
        def _():
            o_ref[...] = jnp.zeros((m, n), F32)

        o_ref[...] += _dot_tn(a_ref[...].astype(BF16), b_ref[...].astype(BF16))

    return pl.pallas_call(
        body, name=name, grid=(t_all // bt,),
        in_specs=[pl.BlockSpec((bt, m), lambda t: (t, 0)), pl.BlockSpec((bt, n), lambda t: (t, 0))],
        out_specs=_full((m, n)), out_shape=_sds((m, n)),
        compiler_params=_params(1),
    )(a, b)


class _Gather8:
    def __init__(self, x_ref, out_ref, send_sems, recv_sems, local_sem):
        x, y, c = _place()
        self.c, self.me, self.sibling = c, (x, y, c), (x, y, 1 - c)
        self.chips = [(1 - x, y), (x, 1 - y), (1 - x, 1 - y)]
        self.x_ref, self.out_ref, self.send_sems, self.recv_sems, self.local_sem = x_ref, out_ref, send_sems, recv_sems, local_sem

    def _slot(self, px, py, pc):
        return self.out_ref.at[4 * px + 2 * py + pc]

    def _copy(self, k, blk, to, src=None):
        return pltpu.make_async_remote_copy(
            src_ref=self._slot(*blk) if src is None else src, dst_ref=self._slot(*blk), send_sem=self.send_sems.at[k],
            recv_sem=self.recv_sems.at[k], device_id=to, device_id_type=MESH)

    def _mine(self):
        return pltpu.make_async_copy(self.x_ref, self._slot(*self.me), self.local_sem)

    def _first(self):
        return [self._copy(0, self.me, self.sibling, src=self.x_ref)] + [
            self._copy(1 + j, self.me, (*chip, self.c), src=self.x_ref) for j, chip in enumerate(self.chips)]

    def _passed(self):
        return [self._copy(4 + j, (*chip, self.c), self.sibling) for j, chip in enumerate(self.chips)]

    def start(self):
        self._mine().start()
        for cp in self._first():
            cp.start()

    def forward(self):
        passed = self._passed()
        for j, chip in enumerate(self.chips):
            self._copy(1 + j, (*chip, self.c), self.me).wait_recv()
            passed[j].start()

    def finish(self):
        self._copy(0, self.sibling, self.me).wait_recv()
        for j, chip in enumerate(self.chips):
            self._copy(4 + j, (*chip, 1 - self.c), self.me).wait_recv()
        for cp in self._first() + self._passed():
            cp.wait_send()
        self._mine().wait()


class _ScatterDirect:
    def __init__(self, g16_ref, g32_ref, got_ref, sib_ref, own_ref, send_sems, recv_sems, local_sem, lay, order):
        self.x, self.y, self.c = _place()
        self.chips = [(1 - self.x, self.y), (self.x, 1 - self.y), (1 - self.x, 1 - self.y)]
        self.refs = (g16_ref, g32_ref, got_ref, sib_ref, own_ref, send_sems, recv_sems, local_sem)
        self.lay, self.order, self.half = lay, order, lay.d // 2

    def _src(self, ref, key, chip, h):
        start = pl.multiple_of(self.lay.g_off[key] + chip * self.lay.rows[key], ROW_ALIGN)
        return ref.at[pl.ds(start, self.lay.rows[key]), pl.ds(pl.multiple_of(h * self.half, LANE), self.half)]

    def _compact(self, ref, key):
        return ref.at[pl.ds(self.lay.c_off[key], self.lay.rows[key]), :]

    def start(self):
        g16_ref, g32_ref, got_ref, sib_ref, own_ref, send_sems, recv_sems, local_sem = self.refs
        x, y, c = self.x, self.y, self.c
        for key in self.order:
            pltpu.make_async_copy(self._src(g32_ref, key, 2 * x + y, c), self._compact(own_ref, key), local_sem).start()
            pltpu.make_async_remote_copy(
                src_ref=self._src(g32_ref, key, 2 * x + y, 1 - c), dst_ref=self._compact(sib_ref, key), send_sem=send_sems.at[6],
                recv_sem=recv_sems.at[6], device_id=(x, y, 1 - c), device_id_type=MESH).start()
        for k, (px, py) in enumerate(self.chips):
            for h in range(2):
                for key in self.order:
                    pltpu.make_async_remote_copy(
                        src_ref=self._src(g16_ref, key, 2 * px + py, h), dst_ref=self._compact(got_ref.at[2 * k + c], key),
                        send_sem=send_sems.at[2 * k + h], recv_sem=recv_sems.at[2 * k + c], device_id=(px, py, h), device_id_type=MESH).start()

    def finish(self):
        _, _, got_ref, sib_ref, own_ref, send_sems, recv_sems, local_sem = self.refs
        x, y, c = self.x, self.y, self.c
        for k, (px, py) in enumerate(self.chips):
            for h in range(2):
                whole = pltpu.make_async_remote_copy(src_ref=got_ref.at[2 * k + h], dst_ref=got_ref.at[2 * k + h], send_sem=send_sems.at[2 * k + h],
                                                     recv_sem=recv_sems.at[2 * k + h], device_id=(px, py, h), device_id_type=MESH)
                whole.wait_send()
                whole.wait_recv()
        pltpu.make_async_remote_copy(src_ref=sib_ref, dst_ref=sib_ref, send_sem=send_sems.at[6], recv_sem=recv_sems.at[6],
                                     device_id=(x, y, 1 - c), device_id_type=MESH).wait()
        pltpu.make_async_copy(own_ref, own_ref, local_sem).wait()


def _scatter_direct_shapes(lay, group):
    rows, half = lay.c_rows[group], lay.d // 2
    return [_sds((6, rows, half), BF16), _sds((rows, half), F32), _sds((rows, half), F32)]


SCATTER_DIRECT_SEMS = [pltpu.SemaphoreType.DMA((7,)), pltpu.SemaphoreType.DMA((7,)), pltpu.SemaphoreType.DMA]
GATHER_SEMS = [pltpu.SemaphoreType.DMA((7,)), pltpu.SemaphoreType.DMA((7,)), pltpu.SemaphoreType.DMA]


def _all_gather8(blocks, name):
    nb = len(blocks)

    def body(*refs):
        x_refs, out_refs = refs[:nb], refs[nb:2 * nb]
        send_sems, recv_sems, local_sems = refs[2 * nb:]
        gathers = [_Gather8(x_refs[n], out_refs[n], send_sems.at[n], recv_sems.at[n], local_sems.at[n]) for n in range(nb)]
        for g in gathers:
            g.start()
        for g in gathers:
            g.forward()
        for g in gathers:
            g.finish()

    return pl.pallas_call(
        body, name=name, out_shape=[_sds((8,) + b.shape, b.dtype) for b in blocks], in_specs=[ANY] * nb, out_specs=[ANY] * nb,
        scratch_shapes=[pltpu.SemaphoreType.DMA((nb, 7)), pltpu.SemaphoreType.DMA((nb, 7)), pltpu.SemaphoreType.DMA((nb,))],
    )(*blocks)


def _return_and_gather(mines, rep_block):
    n = len(mines)

    def body(*refs):
        src_refs, rep_ref, out_refs, rep_out = refs[:n], refs[n], refs[n + 1:2 * n + 1], refs[2 * n + 1]
        send_sems, recv_sems, g_send, g_recv, g_local = refs[2 * n + 2:]
        x, y, c = _place()
        copies = [pltpu.make_async_remote_copy(src_ref=src_refs[k], dst_ref=out_refs[k], send_sem=send_sems.at[k], recv_sem=recv_sems.at[k],
                                               device_id=(x, y, 1 - c), device_id_type=MESH) for k in range(n)]
        gather = _Gather8(rep_ref, rep_out, g_send, g_recv, g_local)
        for cp in copies:
            cp.start()
        gather.start()
        gather.forward()
        gather.finish()
        for cp in copies:
            cp.wait()

    return pl.pallas_call(
        body, name="rs_return", out_shape=[_sds(m.shape, m.dtype) for m in mines] + [_sds((8,) + rep_block.shape, rep_block.dtype)],
        in_specs=[ANY] * (n + 1), out_specs=[ANY] * (n + 1),
        scratch_shapes=[pltpu.SemaphoreType.DMA((n,)), pltpu.SemaphoreType.DMA((n,))] + GATHER_SEMS,
    )(*mines, rep_block)


def _scatter_shapes(lay, group, half):
    return [_sds((3, lay.c_rows[group], half), BF16), _sds((lay.c_rows[group], half), F32)]


def _sum_devices(own, sib, got, name):
    rows, half = own.shape
    rb = _row_block(rows)
    n = got.shape[0]

    def body(a_ref, s_ref, b_ref, o_ref):
        acc = a_ref[...] + s_ref[...]
        for k in range(n):
            acc = acc + b_ref[k].astype(F32)
        o_ref[...] = acc

    spec = pl.BlockSpec((rb, half), lambda i: (i, 0))
    return pl.pallas_call(
        body, name=name, grid=(rows // rb,), in_specs=[spec, spec, pl.BlockSpec((n, rb, half), lambda i: (0, i, 0))], out_specs=spec,
        out_shape=_sds((rows, half)), compiler_params=_params(1),
    )(own, sib, got)


def _sum_chips(own, got, name):
    rows, half = own.shape
    rb = _row_block(rows)

    def body(a_ref, b_ref, o_ref):
        o_ref[...] = ((a_ref[...] + b_ref[0].astype(F32)) + b_ref[1].astype(F32)) + b_ref[2].astype(F32)

    spec = pl.BlockSpec((rb, half), lambda i: (i, 0))
    return pl.pallas_call(
        body, name=name, grid=(rows // rb,), in_specs=[spec, pl.BlockSpec((3, rb, half), lambda i: (0, i, 0))], out_specs=spec,
        out_shape=_sds((rows, half)), compiler_params=_params(1),
    )(own, got)


def _adamw(w, g, m, v):
    m = ADAM_B1 * m + (1.0 - ADAM_B1) * g
    v = ADAM_B2 * v + (1.0 - ADAM_B2) * (g * g)
    m_hat = m / (1.0 - ADAM_B1 ** ADAM_STEP)
    v_hat = v / (1.0 - ADAM_B2 ** ADAM_STEP)
    return -ADAM_LR * (m_hat / (jnp.sqrt(v_hat) + ADAM_EPS) + ADAM_WD * w), m, v


def _adamw_rows(name, w, g, m, v):
    _, rows, cols = w.shape
    rb = _row_block(rows, 256)

    def body(w_ref, g_ref, m_ref, v_ref, d_ref, mo_ref, vo_ref):
        d_ref[...], mo_ref[...], vo_ref[...] = _adamw(w_ref[...], g_ref[...], m_ref[...], v_ref[...])

    spec = pl.BlockSpec((1, rb, cols), lambda i: (0, i, 0))
    return pl.pallas_call(
        body, name=name, grid=(rows // rb,), in_specs=[spec] * 4, out_specs=[spec] * 3, out_shape=[_sds(w.shape)] * 3,
        compiler_params=_params(1),
    )(w, g, m, v)


def _adamw_group(ws, gs, ms, vs):
    n = len(ws)

    def body(*refs):
        for k in range(n):
            w_ref, g_ref, m_ref, v_ref = (refs[j * n + k] for j in range(4))
            outs = _adamw(w_ref[...], g_ref[...], m_ref[...], v_ref[...])
            for j in range(3):
                refs[(4 + j) * n + k][...] = outs[j]

    outs = pl.pallas_call(
        body, name="adamw_small", out_shape=[_sds(w.shape) for w in ws] * 3,
        compiler_params=pltpu.CompilerParams(vmem_limit_bytes=VMEM_LIMIT),
    )(*ws, *gs, *ms, *vs)
    return outs[:n], outs[n:2 * n], outs[2 * n:]


def _gather_weights(sh, lay):
    c = lax.axis_index("c")
    d = lay.d
    uq = sh["w_uq"][0].astype(BF16)
    parts = {
        "in_b": sh["w_in_b"][0].T.astype(BF16), "in_a": sh["w_in_a"][0].T.astype(BF16), "out_a": sh["w_out_a"][0].astype(BF16),
        "out_b": sh["w_out_b"][0].astype(BF16), "uk": sh["w_uk"].astype(BF16).reshape(-1, d), "uv": sh["w_uv"].astype(BF16).reshape(-1, d),
        "uq_n": uq[:, :, :QK_NOPE].reshape(-1, d), "uq_r": jnp.pad(uq[:, :, QK_NOPE:], ((0, 0), (0, 0), (0, LANE - QK_ROPE))).reshape(-1, d),
        "dkv": jnp.pad(sh["w_dkv"].astype(BF16), ((0, 0), (0, LANE - QK_ROPE))).reshape(-1, d),
    }
    halves = {}
    for group, order in W_GROUPS.items():
        stack = jnp.concatenate([parts[k] for k in order], axis=0).reshape(2, lay.w_rows[group] // 2, d)
        halves[group] = lax.dynamic_index_in_dim(stack, c, 0, keepdims=False)
    small = jnp.concatenate([sh[k].reshape(-1) for k in SMALL])
    n_small = small.shape[0]
    width = _round_up(n_small, 2 * SUBLANE * LANE) // (2 * SUBLANE)
    small = jnp.pad(small, (0, 2 * SUBLANE * width - n_small)).reshape(2, SUBLANE, width)
    wg, sg = _all_gather8([halves["a"], lax.dynamic_index_in_dim(small, c, 0, keepdims=False)], "ag_weights")
    wg = wg.reshape(N_CHIPS, lay.w_rows["a"], d)
    sg = sg.reshape(N_CHIPS, 2 * SUBLANE * width)
    full, off = {}, 0
    for k in SMALL:
        n = sh[k].size
        piece = sg[:, off:off + n]
        off += n
        if k == "conv_w":
            full[k] = piece.reshape(N_CHIPS, 4, n // 4).transpose(1, 0, 2).reshape(4, n)
        else:
            full[k] = piece.reshape(1, N_CHIPS * n)
    return wg, halves["b"], full


def _chip_split(g, taps=False):
    if taps:
        n = g.shape[1] // N_CHIPS
        return g.reshape(4, N_CHIPS, n).transpose(1, 0, 2).reshape(N_CHIPS, 4 * n)
    return g.reshape(N_CHIPS, -1)


def kernel(x, norm_a, w_in_a, conv_w, conv_b, w_rg, b_rg, w_ig, b_ig, lru_lambda, w_out_a, norm_kv, w_dkv, kv_norm, w_uk, w_uv, norm_b, w_in_b, q_norm, w_uq, w_out_b, final_norm, loss_target, m_norm_a, m_w_in_a, m_conv_w, m_conv_b, m_w_rg, m_b_rg, m_w_ig, m_b_ig, m_lru_lambda, m_w_out_a, m_norm_kv, m_w_dkv, m_kv_norm, m_w_uk, m_w_uv, m_norm_b, m_w_in_b, m_q_norm, m_w_uq, m_w_out_b, m_final_norm, v_norm_a, v_w_in_a, v_conv_w, v_conv_b, v_w_rg, v_b_rg, v_w_ig, v_b_ig, v_lru_lambda, v_w_out_a, v_norm_kv, v_w_dkv, v_kv_norm, v_w_uk, v_w_uv, v_norm_b, v_w_in_b, v_q_norm, v_w_uq, v_w_out_b, v_final_norm):
    given = dict(locals())
    sh = {k: given[k] for k in WEIGHTS}
    ci = lax.axis_index("c")
    nb, seq, d = x.shape
    t_all = nb * seq
    tb_a, tb_b, ta, bt = min(TOKENS_A, seq), min(TOKENS_B, seq), min(TOKENS_ATTN, seq), min(TOKENS_MM, t_all)
    dr = conv_b.shape[1] * N_CHIPS
    qr, kvr, nheads = q_norm.shape[1], kv_norm.shape[0], w_uk.shape[1]
    hv = nheads * LANE
    n_small = sum(sh[k].size for k in SMALL)
    n_repl = sum(sh[k].size for k in REPL)
    lay = _Layout(d, dr, qr, kvr, hv, n_small, n_repl)
    half = d // 2

    wga, wb_half, w = _gather_weights(sh, lay)
    w.update({"w_rg": w_rg[0].astype(BF16), "w_ig": w_ig[0].astype(BF16), "norm_kv": norm_kv[None, :],
              "kv_norm": kv_norm[None, :], "final_norm": final_norm[None, :], "norm_b": norm_b, "q_norm": q_norm})
    cos_t, sin_t = _rope_tables(seq)

    x0 = x.reshape(t_all, d)
    x1, u, hs, h, y, xb, wgb = _fa_fwd(x0, wga, wb_half, w, lay, seq, min(TOKENS_A_FWD, seq))
    wgb = wgb.reshape(N_CHIPS, lay.w_rows["b"], d)
    w["w_dkv_p"] = wgb[:, lay.w_off["dkv"]:lay.w_off["dkv"] + lay.rows["dkv"], :].reshape(d, kvr + LANE)
    qn, qrp, kn, kr, v, ub, ckr, hb, hk, cq, ckv = _fb_fwd(x1, wgb, w, lay, cos_t, sin_t, seq, tb_b)
    o, lse = _attn_fwd(qn, qrp, kn, kr, v, seq, ta)
    loss, g_final_norm, yb, dx2, do, dgate, delta = _head(o, ub, x1, loss_target.reshape(t_all, d), wgb, w, lay, tb_b)
    dqn, dqr, dkn, dkr, dv = _attn_bwd(qn, qrp, kn, kr, v, do, lse, delta, seq, ta)
    dx1, dqr_pre, dqn_pre, dub, dckr, g_q_norm, g_norm_b, g_kv_norm, g_norm_kv = _fb_bwd(
        dqn, dqr, dkn, dkr, dv, dgate, ub, ckr, x1, dx2, wgb, w, lay, cos_t, sin_t, seq, tb_b)
    loss = lax.psum(loss[0, 0], ("x", "y", "c"))

    gbufs = [lax.empty((lay.g_rows["early"], d), F32), lax.empty((lay.g_rows["early"], d), BF16)]
    for keys, a, bs in ((("in_b",), dub, (hb,)), (("out_a",), y, (dx1,)), (("out_b",), yb, (dx2,)), (("uk", "uv"), ckv, (dkn, dv)),
                        (("uq_n", "uq_r"), cq, (dqn_pre, dqr_pre))):
        gbufs = _mm_into(gbufs, a, bs, [lay.g_off[k] for k in keys], "dw_" + "_".join(keys), bt)
    g_dkv = _mm_tn(hk, dckr, "dw_dkv", bt)
    gx, du, g_norm_a, g_conv_w, g_conv_b, g_b_rg, g_b_ig, g_lam, g_w_rg, g_w_ig, others, sib, own = _fa_bwd(
        dx1, x0, u, xb, hs, wga, gbufs[1], gbufs[0], w, lay, seq, tb_a)
    mine_early = _sum_devices(own, sib, others, "rs_sum_early")

    small = jnp.concatenate([_chip_split(g_norm_a), _chip_split(g_conv_w, taps=True), _chip_split(g_conv_b), _chip_split(g_b_rg),
                             _chip_split(g_b_ig), _chip_split(g_lam)], axis=1)
    small = jnp.pad(small, ((0, 0), (0, lay.small_rows * d - small.shape[1]))).reshape(N_CHIPS, lay.small_rows, d)
    repl_parts = {"w_rg": g_w_rg, "w_ig": g_w_ig, "norm_kv": g_norm_kv, "kv_norm": g_kv_norm, "norm_b": g_norm_b, "q_norm": g_q_norm,
                  "final_norm": g_final_norm}
    repl = jnp.concatenate([repl_parts[k].reshape(-1) for k in REPL])
    repl = jnp.pad(repl, (0, N_CHIPS * lay.repl_rows * d - n_repl)).reshape(N_CHIPS, lay.repl_rows, d)
    pad_rows = lay.rows["rest"] - lay.rows["dkv"] - lay.small_rows - lay.repl_rows
    rest = jnp.concatenate([g_dkv.reshape(N_CHIPS, lay.rows["dkv"], d), small, repl, jnp.zeros((N_CHIPS, pad_rows, d), F32)], axis=1)
    others, own = _dw_in_a_exchange(du, h, rest.reshape(N_CHIPS * lay.rows["rest"], d), lay, bt)
    mine_late = _sum_chips(own, others, "rs_sum_chips_late")

    r0 = lay.c_off["rest"] + lay.rows["dkv"] + lay.small_rows
    theirs_early, theirs_late, rep_all = _return_and_gather([mine_early, mine_late], mine_late[r0:r0 + lay.repl_rows])
    red = {}
    for group, mine, theirs in (("early", mine_early, theirs_early), ("late", mine_late, theirs_late)):
        red[group] = jnp.concatenate([jnp.where(ci == 0, mine, theirs), jnp.where(ci == 0, theirs, mine)], axis=1)
    rep_flat =rep_all.reshape(N_CHIPS, 2, lay.repl_rows, half).transpose(0, 2, 1, 3).reshape(-1)

    def rows(key):
        group = "late" if key in G_GROUPS["late"] else "early"
        return red[group][lay.c_off[key]:lay.c_off[key] + lay.rows[key]]

    grads = {"w_in_b": rows("in_b").T[None], "w_in_a": rows("in_a").T[None], "w_out_a": rows("out_a")[None], "w_out_b": rows("out_b")[None],
             "w_uk": rows("uk").reshape(w_uk.shape), "w_uv": rows("uv").reshape(w_uv.shape)}
    uq_n = rows("uq_n").reshape(qr // N_CHIPS, nheads, LANE)
    uq_r = rows("uq_r").reshape(qr // N_CHIPS, nheads, LANE)[:, :, :QK_ROPE]
    grads["w_uq"] = jnp.concatenate([uq_n, uq_r], axis=2)[None]
    rest_red = rows("rest")
    grads["w_dkv"] = rest_red[:lay.rows["dkv"]].reshape(d // N_CHIPS, kvr + LANE)[:, :kvr + QK_ROPE]
    small_red = rest_red[lay.rows["dkv"]:lay.rows["dkv"] + lay.small_rows].reshape(-1)
    off = 0
    for k in SMALL:
        n = sh[k].size
        grads[k] = small_red[off:off + n].reshape(sh[k].shape)
        off += n
    off = 0
    for k in REPL:
        n = sh[k].size
        grads[k] = rep_flat[off:off + n].reshape(sh[k].shape)
        off += n

    new = {}
    for k in ("w_in_a", "w_in_b", "w_out_a", "w_out_b"):
        view = (lambda a: jnp.swapaxes(a, 1, 2)) if k in TRANSPOSED else (lambda a: a)
        outs = _adamw_rows("adamw_" + k, view(sh[k]), view(grads[k]), view(given["m_" + k]), view(given["v_" + k]))
        new[k] = tuple(view(a) for a in outs)
    rest_names = [k for k in WEIGHTS if k not in new]

    def as2d(k, a):
        return a.T if k in TRANSPOSED else a[None, :] if a.ndim == 1 else a

    ds, ms, vs = _adamw_group([as2d(k, sh[k]) for k in rest_names], [as2d(k, grads[k]) for k in rest_names],
                              [as2d(k, given["m_" + k]) for k in rest_names], [as2d(k, given["v_" + k]) for k in rest_names])
    for n, k in enumerate(rest_names):
        new[k] = tuple((a.T if k in TRANSPOSED else a).reshape(sh[k].shape) for a in (ds[n], ms[n], vs[n]))
    return (loss, gx.reshape(nb, seq, d), *[grads[k] for k in WEIGHTS], *[new[k][0] for k in WEIGHTS], *[new[k][1] for k in WEIGHTS],
            *[new[k][2] for k in WEIGHTS])
```

```python
import jax
import jax.numpy as jnp
from jax import lax
from jax.experimental import pallas as pl
from jax.experimental.pallas import tpu as pltpu

F32, BF16 = jnp.float32, jnp.bfloat16
EPS = 1e-6
LRU_C = 8.0
ROPE_THETA = 10000.0
QK_NOPE, QK_ROPE = 128, 64
ATTN_SCALE = (QK_NOPE + QK_ROPE) ** -0.5
LN2 = 0.6931471805599453
Q_SCALE = ATTN_SCALE / LN2
ATTN_HEADS, ATTN_HEADS_BWD = 4, 2
ATTN_ROWS = 64
LANE = 128
SUBLANE = 8
ROW_ALIGN = 32
VMEM_LIMIT = 60000 * 1024
ADAM_LR, ADAM_B1, ADAM_B2, ADAM_EPS, ADAM_WD, ADAM_STEP = 0.001, 0.9, 0.999, 1e-08, 0.01, 10
MESH = pl.DeviceIdType.MESH
ANY = pl.BlockSpec(memory_space=pl.ANY)
N_CHIPS = 4
TOKENS_A, TOKENS_B, TOKENS_ATTN, TOKENS_MM = 256, 512, 512, 2048
TOKENS_A_FWD = 512

SMALL = ("norm_a", "conv_w", "conv_b", "b_rg", "b_ig", "lru_lambda")
REPL = ("w_rg", "w_ig", "norm_kv", "kv_norm", "norm_b", "q_norm", "final_norm")
TRANSPOSED = ("w_in_b", "w_dkv")
WEIGHTS = ("norm_a", "w_in_a", "conv_w", "conv_b", "w_rg", "b_rg", "w_ig", "b_ig", "lru_lambda", "w_out_a", "norm_kv",
           "w_dkv", "kv_norm", "w_uk", "w_uv", "norm_b", "w_in_b", "q_norm", "w_uq", "w_out_b", "final_norm")
W_GROUPS = {"a": ("in_a", "out_a"), "b": ("in_b", "out_b", "uk", "uv", "uq_n", "uq_r", "dkv")}
G_GROUPS = {"early": ("in_b", "out_a", "out_b", "uk", "uv", "uq_n", "uq_r"), "late": ("in_a", "rest")}


def _sds(shape, dtype=F32):
    return jax.ShapeDtypeStruct(tuple(shape), dtype)


def _params(n_grid):
    return pltpu.CompilerParams(dimension_semantics=("arbitrary",) * n_grid, vmem_limit_bytes=VMEM_LIMIT)


def _full(shape):
    nd = len(shape)
    return pl.BlockSpec(tuple(shape), lambda *g: (0,) * nd)


def _round_up(n, k):
    return -(-n // k) * k


def _row_block(rows, cap=512):
    best = SUBLANE
    for r in range(SUBLANE, min(rows, cap) + 1, SUBLANE):
        if rows % r == 0:
            best = r
    return best


def _place():
    return lax.axis_index("x"), lax.axis_index("y"), lax.axis_index("c")


class _Layout:
    def __init__(self, d, dr, qr, kvr, hv, n_small, n_repl):
        assert hv == d, "the packed rows are D_MODEL wide, which must equal heads * 128"
        self.d, self.dr, self.qr, self.kvr, self.hv = d, dr, qr, kvr, hv
        per_chip = {"in_b": (qr + hv) // N_CHIPS, "in_a": 2 * dr // N_CHIPS, "out_a": dr // N_CHIPS, "out_b": hv // N_CHIPS,
                    "uk": kvr // N_CHIPS, "uv": kvr // N_CHIPS, "uq_n": qr // N_CHIPS, "uq_r": qr // N_CHIPS,
                    "dkv": (d // N_CHIPS) * (kvr + LANE) // d}
        assert all(r % ROW_ALIGN == 0 for r in per_chip.values()), per_chip
        self.small_rows = _round_up(-(-n_small // d), SUBLANE)
        self.repl_rows = _round_up(-(-n_repl // (N_CHIPS * d)), SUBLANE)
        per_chip["rest"] = _round_up(per_chip["dkv"] + self.small_rows + self.repl_rows, ROW_ALIGN)
        self.rows = per_chip
        self.w_off, self.w_rows = {}, {}
        for group, order in W_GROUPS.items():
            off = 0
            for k in order:
                self.w_off[k] = off
                off += per_chip[k]
            assert off % ROW_ALIGN == 0, (group, off)
            self.w_rows[group] = off
        self.g_off, self.c_off, self.c_rows, self.g_rows = {}, {}, {}, {}
        for group, order in G_GROUPS.items():
            off = 0
            for k in order:
                self.c_off[k] = off
                self.g_off[k] = N_CHIPS * off
                off += per_chip[k]
            self.c_rows[group] = off
            self.g_rows[group] = N_CHIPS * off


def _dot(a, b):
    return jnp.dot(a, b, preferred_element_type=F32)


def _dot_nt(a, b):
    return lax.dot_general(a, b, (((1,), (1,)), ((), ())), preferred_element_type=F32)


def _dot_tn(a, b):
    return lax.dot_general(a, b, (((0,), (0,)), ((), ())), preferred_element_type=F32)


def _rinv(x):
    return lax.rsqrt(jnp.mean(x * x, axis=-1, keepdims=True) + EPS)


def _rms_bwd(x, rinv, g, dy):
    z = dy * g
    dx = rinv * z - x * (rinv * rinv * rinv) * jnp.mean(z * x, axis=-1, keepdims=True)
    dg = jnp.sum(dy * (x * rinv), axis=0, keepdims=True)
    return dx, dg


def _softplus(z):
    return jnp.maximum(z, 0.0) + jnp.log1p(jnp.exp(-jnp.abs(z)))


def _sigmoid(x):
    return 0.5 * jnp.tanh(0.5 * x) + 0.5


def _decay(log_a):
    a = jnp.exp(log_a)
    a2 = a * a
    return a, a2, -jnp.tanh(log_a) * (a2 + 1.0)


def _swap_halves(x):
    w = x.shape[1]
    lane = lax.broadcasted_iota(jnp.int32, x.shape, 1)
    return jnp.where(lane % QK_ROPE < QK_ROPE // 2, pltpu.roll(x, w - QK_ROPE // 2, 1), pltpu.roll(x, QK_ROPE // 2, 1))


def _rope_tables(seq):
    pos = jnp.arange(seq, dtype=F32)
    inv = ROPE_THETA ** (-jnp.arange(0, QK_ROPE, 2, dtype=F32) / QK_ROPE)
    ang = pos[:, None] * inv[None, :]
    cos, sin = jnp.cos(ang), jnp.sin(ang)
    zero = jnp.zeros((seq, LANE - QK_ROPE), F32)
    return jnp.concatenate([cos, cos, zero], 1), jnp.concatenate([-sin, sin, zero], 1)


def _fetch(wg_ref, lay, key, dst, sems, k0):
    rows = lay.rows[key]
    return [pltpu.make_async_copy(wg_ref.at[p, pl.ds(lay.w_off[key], rows), :], dst.at[pl.ds(p * rows, rows), :], sems.at[k0 + p])
            for p in range(N_CHIPS)]


def _gates(xb, wrg_ref, brg, wig_ref, big, nblocks):
    xbb = xb.astype(BF16)
    rg = [_dot(xbb[:, n * LANE:(n + 1) * LANE], wrg_ref[n]) for n in range(nblocks)]
    ig = [_dot(xbb[:, n * LANE:(n + 1) * LANE], wig_ref[n]) for n in range(nblocks)]
    r = _sigmoid(jnp.concatenate(rg, axis=1) + brg)
    i = _sigmoid(jnp.concatenate(ig, axis=1) + big)
    return r, i


def _conv(xpad, cw_ref, cb, tb):
    return (cb + cw_ref[3:4, :] * xpad[pl.ds(8, tb), :] + cw_ref[2:3, :] * xpad[pl.ds(7, tb), :]
            + cw_ref[1:2, :] * xpad[pl.ds(6, tb), :] + cw_ref[0:1, :] * xpad[pl.ds(5, tb), :])


def _fa_fwd(x, wg, wb_half, w, lay, seq, tb):
    t_all, d = x.shape
    dr = lay.dr
    nblocks = w["w_rg"].shape[0]
    nblk = seq // tb
    nt = tb // SUBLANE
    nsteps = (t_all // seq) * nblk

    def body(x_ref, wg_ref, wbh_ref, na, cw, cb, wrg, brg, wig, big, lam, x1_ref, u_ref, hs_ref, h_ref, y_ref, xb_ref, wb_ref,
             wint, wout, xpad, a_s, b_s, carry, sems, send_sems, recv_sems, local_sem):
        step_no = pl.program_id(0) * nblk + pl.program_id(1)
        gather = _Gather8(wbh_ref, wb_ref, send_sems, recv_sems, local_sem)

        @pl.when(step_no == 0)
        def _():
            gather.start()
            cps = _fetch(wg_ref, lay, "in_a", wint, sems, 0) + _fetch(wg_ref, lay, "out_a", wout, sems, N_CHIPS)
            for cp in cps:
                cp.start()
            for cp in cps:
                cp.wait()

        @pl.when(step_no == nsteps // 2)
        def _():
            gather.forward()

        @pl.when(pl.program_id(1) == 0)
        def _():
            xpad[pl.ds(0, 8), :] = jnp.zeros((8, dr), F32)
            carry[...] = jnp.zeros((8, dr), F32)

        xv = x_ref[...]
        h = (xv * _rinv(xv) * na[...]).astype(BF16)
        h_ref[...] = h
        u = _dot_nt(h, wint[...])
        u_ref[...] = u
        xpre, gate = u[:, :dr], u[:, dr:]
        xpad[pl.ds(8, tb), :] = xpre
        xb = _conv(xpad, cw, cb[...], tb)
        xb_ref[...] = xb
        xpad[pl.ds(0, 8), :] = xpre[tb - 8:, :]
        r, i = _gates(xb, wrg, brg[...], wig, big[...], nblocks)
        log_a = -LRU_C * r * _softplus(-lam[...])
        a, _, nem = _decay(log_a)
        a_s[...] = a
        b_s[...] = jnp.sqrt(nem) * (i * xb)
        row = lax.broadcasted_iota(jnp.int32, (8, dr), 0)

        def step(t, c):
            r0 = pl.multiple_of(t * 8, 8)
            a = a_s[pl.ds(r0, 8), :]
            b = b_s[pl.ds(r0, 8), :]
            for s in (1, 2, 4):
                m = row >= s
                a_sh = jnp.where(m, pltpu.roll(a, s, 0), 1.0)
                b_sh = jnp.where(m, pltpu.roll(b, s, 0), 0.0)
                b = a * b_sh + b
                a = a * a_sh
            hh = b + a * c
            hs_ref[pl.ds(r0, 8), :] = hh
            return jnp.broadcast_to(hh[7:8, :], hh.shape)

        carry[...] = lax.fori_loop(0, nt, step, carry[...])
        y = (hs_ref[...] * (gate * _sigmoid(gate))).astype(BF16)
        y_ref[...] = y
        x1_ref[...] = xv + _dot(y, wout[...])

        @pl.when(step_no == nsteps - 1)
        def _():
            gather.finish()

    tok = lambda c: pl.BlockSpec((tb, c), lambda b, j: (b * nblk + j, 0))
    consts = [w["norm_a"], w["conv_w"], w["conv_b"], w["w_rg"], w["b_rg"], w["w_ig"], w["b_ig"], w["lru_lambda"]]
    return pl.pallas_call(
        body, name="fa_fwd", grid=(t_all // seq, nblk),
        in_specs=[tok(d), ANY, ANY] + [_full(c.shape) for c in consts],
        out_specs=[tok(d), tok(2 * dr), tok(dr), tok(d), tok(dr), tok(dr), ANY],
        out_shape=[_sds((t_all, d)), _sds((t_all, 2 * dr)), _sds((t_all, dr)), _sds((t_all, d), BF16), _sds((t_all, dr), BF16),
                   _sds((t_all, dr)), _sds((8,) + wb_half.shape, BF16)],
        scratch_shapes=[pltpu.VMEM((2 * dr, d), BF16), pltpu.VMEM((dr, d), BF16), pltpu.VMEM((tb + 8, dr), F32), pltpu.VMEM((tb, dr), F32),
                        pltpu.VMEM((tb, dr), F32), pltpu.VMEM((8, dr), F32), pltpu.SemaphoreType.DMA((2 * N_CHIPS,))] + GATHER_SEMS,
        compiler_params=_params(2),
    )(x, wg, wb_half, *consts)


def _fb_fwd(x1, wg, w, lay, cos_t, sin_t, seq, tb):
    t_all, d = x1.shape
    kvr, qr, hv = lay.kvr, lay.qr, lay.hv
    nheads = hv // LANE
    npos = seq // tb

    def body(x_ref, wg_ref, nkv, nb, wdkv, kvn, qn, cos_ref, sin_ref,
             qn_o, qr_o, kn_o, kr_o, v_o, ub_o, ckr_o, hb_o, hk_o, cq_o, ckv_o, winb, wuk, wuv, wuqn, wuqr, sems):
        @pl.when(pl.program_id(0) == 0)
        def _():
            cps = []
            for n, (key, dst) in enumerate((("in_b", winb), ("uk", wuk), ("uv", wuv), ("uq_n", wuqn), ("uq_r", wuqr))):
                cps += _fetch(wg_ref, lay, key, dst, sems, n * N_CHIPS)
            for cp in cps:
                cp.start()
            for cp in cps:
                cp.wait()

        xv = x_ref[...]
        xh = xv * _rinv(xv)
        hk = (xh * nkv[...]).astype(BF16)
        hb = (xh * nb[...]).astype(BF16)
        hk_o[...] = hk
        hb_o[...] = hb
        cos, sin = cos_ref[...], sin_ref[...]
        ckr = _dot(hk, wdkv[...])
        ckr_o[...] = ckr
        ckv_pre = ckr[:, :kvr]
        ckv = (ckv_pre * _rinv(ckv_pre) * kvn[...]).astype(BF16)
        ckv_o[...] = ckv
        kr = ckr[:, kvr:]
        kr_o[...] = (kr * cos + _swap_halves(kr) * sin).astype(BF16)
        kn_o[...] = _dot(ckv, wuk[...]).astype(BF16)
        v_o[...] = _dot(ckv, wuv[...]).astype(BF16)
        ub = _dot_nt(hb, winb[...])
        ub_o[...] = ub
        cq_pre = ub[:, :qr]
        cq = (cq_pre * _rinv(cq_pre) * qn[...]).astype(BF16)
        cq_o[...] = cq
        qn_o[...] = (_dot(cq, wuqn[...]) * Q_SCALE).astype(BF16)
        qrope = _dot(cq, wuqr[...]) * Q_SCALE
        qr_o[...] = (qrope * jnp.tile(cos, (1, nheads)) + _swap_halves(qrope) * jnp.tile(sin, (1, nheads))).astype(BF16)

    tok = lambda c: pl.BlockSpec((tb, c), lambda i: (i, 0))
    pos = pl.BlockSpec((tb, LANE), lambda i: (i % npos, 0))
    consts = [w["norm_kv"], w["norm_b"], w["w_dkv_p"], w["kv_norm"], w["q_norm"]]
    outs = [(hv, BF16), (hv, BF16), (hv, BF16), (LANE, BF16), (hv, BF16), (qr + hv, F32), (kvr + LANE, F32), (d, BF16), (d, BF16), (qr, BF16), (kvr, BF16)]
    return pl.pallas_call(
        body, name="fb_fwd", grid=(t_all // tb,),
        in_specs=[tok(d), ANY] + [_full(c.shape) for c in consts] + [pos, pos],
        out_specs=[tok(c) for c, _ in outs],
        out_shape=[_sds((t_all, c), dt) for c, dt in outs],
        scratch_shapes=[pltpu.VMEM((qr + hv, d), BF16), pltpu.VMEM((kvr, d), BF16), pltpu.VMEM((kvr, d), BF16), pltpu.VMEM((qr, d), BF16),
                        pltpu.VMEM((qr, d), BF16), pltpu.SemaphoreType.DMA((5 * N_CHIPS,))],
        compiler_params=_params(1),
    )(x1, wg, *consts, cos_t, sin_t)


def _causal_mask(row0, col0, nrows, ncols):
    rows = row0 + lax.broadcasted_iota(jnp.int32, (nrows, ncols), 0)
    cols = col0 + lax.broadcasted_iota(jnp.int32, (nrows, ncols), 1)
    return cols <= rows


def _attn_fwd(qn, qr, kn, kr, v, seq, ta):
    t_all, hv = qn.shape
    nheads, nb, na = hv // LANE, t_all // seq, seq // ta

    reps = ta // LANE
    hp = ATTN_HEADS
    wide = hp * LANE

    def body(qn_ref, qr_ref, kn_ref, kr_ref, v_ref, o_ref, lse_ref, m_s, l_s, acc_s):
        i = pl.program_id(2)
        m_s[...] = jnp.full((ta, wide), -1e30, F32)
        l_s[...] = jnp.zeros((ta, wide), F32)
        acc_s[...] = jnp.zeros((ta, wide), F32)
        heads = [slice(n * LANE, (n + 1) * LANE) for n in range(hp)]
        qs = [jnp.concatenate([qn_ref[:, hd], qr_ref[:, hd]], axis=1) for hd in heads]

        def tile(j, diagonal):
            cols = pl.ds(pl.multiple_of(j * ta, ta), ta)
            k_rope = kr_ref[cols, :]
            for q, hd in zip(qs, heads):
                k = jnp.concatenate([kn_ref[cols, hd], k_rope], axis=1)
                s = _dot_nt(q, k)
                if diagonal:
                    s = jnp.where(_causal_mask(0, 0, ta, ta), s, -1e30)
                m_prev = m_s[:, hd]
                m_new = jnp.maximum(m_prev, jnp.max(s, axis=1, keepdims=True))
                p = jnp.exp2(s - jnp.tile(m_new, (1, reps)))
                alpha = jnp.exp2(m_prev - m_new)
                l_s[:, hd] = alpha * l_s[:, hd] + jnp.sum(p, axis=1, keepdims=True)
                acc_s[:, hd] = alpha * acc_s[:, hd] + _dot(p.astype(BF16), v_ref[cols, hd])
                m_s[:, hd] = m_new

        def off_diagonal(j, carry):
            tile(j, False)
            return carry

        lax.fori_loop(0, i, off_diagonal, 0)
        tile(i, True)
        o_ref[...] = (acc_s[...] / l_s[...]).astype(BF16)
        lse_ref[...] = m_s[...] + jnp.log2(l_s[...])

    qspec = pl.BlockSpec((ta, wide), lambda b, h, i: (b * na + i, h))
    kspec = pl.BlockSpec((seq, wide), lambda b, h, i: (b, h))
    krspec = pl.BlockSpec((seq, LANE), lambda b, h, i: (b, 0))
    return pl.pallas_call(
        body, name="attn_fwd", grid=(nb, nheads // hp, na),
        in_specs=[qspec, qspec, kspec, krspec, kspec],
        out_specs=[qspec, qspec],
        out_shape=[_sds((t_all, hv), BF16), _sds((t_all, hv))],
        scratch_shapes=[pltpu.VMEM((ta, wide), F32)] * 3,
        compiler_params=_params(3),
    )(qn, qr, kn, kr, v)


def _attn_bwd(qn, qr, kn, kr, v, do, lse, delta, seq, ta):
    t_all, hv = qn.shape
    nheads, nb, na = hv // LANE, t_all // seq, seq // ta

    reps = ta // LANE
    nchunks = ta // ATTN_ROWS

    hp = ATTN_HEADS_BWD
    wide = hp * LANE
    heads = [slice(n * LANE, (n + 1) * LANE) for n in range(hp)]

    def body(qn_ref, qr_ref, kn_ref, kr_ref, v_ref, do_ref, lse_ref, dl_ref, dqn_out, dqr_out, dkn_ref, dkr_ref, dv_ref,
             s_s, dp_s, p_s, ds_s, dk_s, dv_s, dqn_ref, dqr_ref):
        j = pl.program_id(2)

        @pl.when(j == 0)
        def _():
            dqn_ref[...] = jnp.zeros((seq, wide), F32)
            dqr_ref[...] = jnp.zeros((seq, wide), F32)

        dk_s[...] = jnp.zeros((hp, ta, 2 * LANE), F32)
        dv_s[...] = jnp.zeros((hp, ta, LANE), F32)
        k_rope = kr_ref[...]
        ks = [jnp.concatenate([kn_ref[:, hd], k_rope], axis=1) for hd in heads]

        def tile(i, diagonal):
            rows_i = pl.ds(pl.multiple_of(i * ta, ta), ta)
            for n, hd in enumerate(heads):
                q = jnp.concatenate([qn_ref[rows_i, hd], qr_ref[rows_i, hd]], axis=1)
                do_b = do_ref[rows_i, hd]
                s_s[n] = _dot_nt(q, ks[n])
                dp_s[n] = _dot_nt(do_b, v_ref[:, hd])
                for c in range(nchunks):
                    rows = pl.ds(c * ATTN_ROWS, ATTN_ROWS)
                    seq_rows = pl.ds(pl.multiple_of(i * ta + c * ATTN_ROWS, ATTN_ROWS), ATTN_ROWS)
                    s = s_s[n, rows, :]
                    if diagonal:
                        s = jnp.where(_causal_mask(c * ATTN_ROWS, 0, ATTN_ROWS, ta), s, -1e30)
                    p = jnp.exp2(s - jnp.tile(lse_ref[seq_rows, hd], (1, reps)))
                    p_s[n, rows, :] = p.astype(BF16)
                    ds_s[n, rows, :] = (p * (dp_s[n, rows, :] - jnp.tile(dl_ref[seq_rows, hd], (1, reps)))).astype(BF16)
                dv_s[n] += _dot_tn(p_s[n], do_b)
                ds = ds_s[n]
                dk_s[n] += _dot_tn(ds, q)
                dq = _dot(ds, ks[n])
                dqn_ref[rows_i, hd] += dq[:, :LANE]
                dqr_ref[rows_i, hd] += dq[:, LANE:]

        def off_diagonal(i, carry):
            tile(i, False)
            return carry

        tile(j, True)
        lax.fori_loop(j + 1, na, off_diagonal, 0)
        for n, hd in enumerate(heads):
            dkn_ref[:, hd] = (dk_s[n, :, :LANE] * LN2).astype(BF16)
            dkr_ref[:, hd] = (dk_s[n, :, LANE:] * LN2).astype(BF16)
            dv_ref[:, hd] = dv_s[n].astype(BF16)

        @pl.when(j == na - 1)
        def _():
            dqn_out[...] = dqn_ref[...].astype(BF16)
            dqr_out[...] = dqr_ref[...].astype(BF16)

    qspec = pl.BlockSpec((seq, wide), lambda b, h, j: (b, h))
    kspec = pl.BlockSpec((ta, wide), lambda b, h, j: (b * na + j, h))
    krspec = pl.BlockSpec((ta, LANE), lambda b, h, j: (b * na + j, 0))
    return pl.pallas_call(
        body, name="attn_bwd", grid=(nb, nheads // hp, na),
        in_specs=[qspec, qspec, kspec, krspec, kspec, qspec, qspec, qspec],
        out_specs=[qspec, qspec, kspec, kspec, kspec],
        out_shape=[_sds((t_all, hv), BF16)] * 5,
        scratch_shapes=[pltpu.VMEM((hp, ta, ta), F32), pltpu.VMEM((hp, ta, ta), F32), pltpu.VMEM((hp, ta, ta), BF16), pltpu.VMEM((hp, ta, ta), BF16),
                        pltpu.VMEM((hp, ta, 2 * LANE), F32), pltpu.VMEM((hp, ta, LANE), F32), pltpu.VMEM((seq, wide), F32), pltpu.VMEM((seq, wide), F32)],
        compiler_params=_params(3),
    )(qn, qr, kn, kr, v, do, lse, delta)


def _head(o, ub, x1, target, wg, w, lay, tb):
    t_all, d = x1.shape
    hv, qr = lay.hv, lay.qr
    nheads = hv // LANE

    def body(o_ref, ub_ref, x1_ref, tg_ref, wg_ref, gf, loss_ref, dgf_ref, yb_ref, dx2_ref, do_ref, dg_ref, dl_ref, wob, sems):
        @pl.when(pl.program_id(0) == 0)
        def _():
            cps = _fetch(wg_ref, lay, "out_b", wob, sems, 0)
            for cp in cps:
                cp.start()
            loss_ref[...] = jnp.zeros((1, LANE), F32)
            dgf_ref[...] = jnp.zeros((1, d), F32)
            for cp in cps:
                cp.wait()

        ov = o_ref[...].astype(F32)
        g = ub_ref[:, qr:]
        sg = _sigmoid(g)
        silu = g * sg
        yb = (ov * silu).astype(BF16)
        yb_ref[...] = yb
        x2 = x1_ref[...] + _dot(yb, wob[...])
        rinv = _rinv(x2)
        err = x2 * rinv * gf[...] - tg_ref[...]
        loss_ref[...] += (0.5 / d) * jnp.sum(jnp.sum(err * err, axis=1, keepdims=True), axis=0, keepdims=True)
        dx2, dgf = _rms_bwd(x2, rinv, gf[...], err * (1.0 / d))
        dgf_ref[...] += dgf
        dx2_ref[...] = dx2
        dyb = _dot_nt(dx2.astype(BF16), wob[...])
        dov = dyb * silu
        do_ref[...] = dov.astype(BF16)
        dg_ref[...] = (dyb * ov * (sg * (1.0 + g * (1.0 - sg)))).astype(BF16)
        prod = dov * ov
        dl_ref[...] = jnp.concatenate(
            [jnp.broadcast_to(jnp.sum(prod[:, n * LANE:(n + 1) * LANE], axis=1, keepdims=True), (tb, LANE)) for n in range(nheads)], axis=1)

    tok = lambda c: pl.BlockSpec((tb, c), lambda i: (i, 0))
    return pl.pallas_call(
        body, name="head", grid=(t_all // tb,),
        in_specs=[tok(hv), tok(qr + hv), tok(d), tok(d), ANY, _full((1, d))],
        out_specs=[_full((1, LANE)), _full((1, d)), tok(hv), tok(d), tok(hv), tok(hv), tok(hv)],
        out_shape=[_sds((1, LANE)), _sds((1, d)), _sds((t_all, hv), BF16), _sds((t_all, d)), _sds((t_all, hv), BF16), _sds((t_all, hv), BF16),
                   _sds((t_all, hv))],
        scratch_shapes=[pltpu.VMEM((hv, d), BF16), pltpu.SemaphoreType.DMA((N_CHIPS,))],
        compiler_params=_params(1),
    )(o, ub, x1, target, wg, w["final_norm"])


def _fb_bwd(dqn, dqr, dkn, dkr, dv, dgate, ub, ckr, x1, dx2, wg, w, lay, cos_t, sin_t, seq, tb):
    t_all, d = x1.shape
    hv, qr, kvr = lay.hv, lay.qr, lay.kvr
    nheads = hv // LANE
    npos = seq // tb

    def body(dqn_ref, dqr_ref, dkn_ref, dkr_ref, dv_ref, dg_ref, ub_ref, ckr_ref, x1_ref, dx2_ref, wg_ref,
             qn, nb, kvn, wdkv, nkv, cos_ref, sin_ref,
             dx1_ref, dqrp_ref, dqnp_ref, dub_ref, dckr_ref, dqn_g, dnb_g, dkvn_g, dnkv_g, winb, wuk, wuv, wuqn, wuqr, sems):
        @pl.when(pl.program_id(0) == 0)
        def _():
            cps = []
            for n, (key, dst) in enumerate((("in_b", winb), ("uk", wuk), ("uv", wuv), ("uq_n", wuqn), ("uq_r", wuqr))):
                cps += _fetch(wg_ref, lay, key, dst, sems, n * N_CHIPS)
            for cp in cps:
                cp.start()
            dqn_g[...] = jnp.zeros((1, qr), F32)
            dnb_g[...] = jnp.zeros((1, d), F32)
            dkvn_g[...] = jnp.zeros((1, kvr), F32)
            dnkv_g[...] = jnp.zeros((1, d), F32)
            for cp in cps:
                cp.wait()

        cos, sin = cos_ref[...], sin_ref[...]
        xv = x1_ref[...]
        rinv1 = _rinv(xv)
        dqr_v = dqr_ref[...].astype(F32) * ATTN_SCALE
        dqr_pre = (dqr_v * jnp.tile(cos, (1, nheads)) + _swap_halves(dqr_v * jnp.tile(sin, (1, nheads)))).astype(BF16)
        dqrp_ref[...] = dqr_pre
        dqn_pre = (dqn_ref[...].astype(F32) * ATTN_SCALE).astype(BF16)
        dqnp_ref[...] = dqn_pre
        dcq = _dot_nt(dqn_pre, wuqn[...]) + _dot_nt(dqr_pre, wuqr[...])
        cq_pre = ub_ref[:, :qr]
        dcq_pre, g1 = _rms_bwd(cq_pre, _rinv(cq_pre), qn[...], dcq)
        dqn_g[...] += g1
        dub = jnp.concatenate([dcq_pre.astype(BF16), dg_ref[...]], axis=1)
        dub_ref[...] = dub
        dx1_b, g2 = _rms_bwd(xv, rinv1, nb[...], _dot(dub, winb[...]))
        dnb_g[...] += g2
        dkr_all = dkr_ref[...].astype(F32)
        dkr_sum = dkr_all[:, :LANE]
        for n in range(1, nheads):
            dkr_sum = dkr_sum + dkr_all[:, n * LANE:(n + 1) * LANE]
        dckr_rope = dkr_sum * cos + _swap_halves(dkr_sum * sin)
        dckv = _dot_nt(dkn_ref[...].astype(BF16), wuk[...]) + _dot_nt(dv_ref[...].astype(BF16), wuv[...])
        ckv_pre = ckr_ref[:, :kvr]
        dckv_pre, g3 = _rms_bwd(ckv_pre, _rinv(ckv_pre), kvn[...], dckv)
        dkvn_g[...] += g3
        dckr = jnp.concatenate([dckv_pre, dckr_rope], axis=1).astype(BF16)
        dckr_ref[...] = dckr
        dx1_kv, g4 = _rms_bwd(xv, rinv1, nkv[...], _dot_nt(dckr, wdkv[...]))
        dnkv_g[...] += g4
        dx1_ref[...] = dx2_ref[...] + dx1_b + dx1_kv

    tok = lambda c: pl.BlockSpec((tb, c), lambda i: (i, 0))
    pos = pl.BlockSpec((tb, LANE), lambda i: (i % npos, 0))
    consts = [w["q_norm"], w["norm_b"], w["kv_norm"], w["w_dkv_p"], w["norm_kv"]]
    return pl.pallas_call(
        body, name="fb_bwd", grid=(t_all // tb,),
        in_specs=[tok(hv)] * 6 + [tok(qr + hv), tok(kvr + LANE), tok(d), tok(d), ANY] + [_full(c.shape) for c in consts] + [pos, pos],
        out_specs=[tok(d), tok(hv), tok(hv), tok(qr + hv), tok(kvr + LANE), _full((1, qr)), _full((1, d)), _full((1, kvr)), _full((1, d))],
        out_shape=[_sds((t_all, d)), _sds((t_all, hv), BF16), _sds((t_all, hv), BF16), _sds((t_all, qr + hv), BF16), _sds((t_all, kvr + LANE), BF16),
                   _sds((1, qr)), _sds((1, d)), _sds((1, kvr)), _sds((1, d))],
        scratch_shapes=[pltpu.VMEM((qr + hv, d), BF16), pltpu.VMEM((kvr, d), BF16), pltpu.VMEM((kvr, d), BF16), pltpu.VMEM((qr, d), BF16),
                        pltpu.VMEM((qr, d), BF16), pltpu.SemaphoreType.DMA((5 * N_CHIPS,))],
        compiler_params=_params(1),
    )(dqn, dqr, dkn, dkr, dv, dgate, ub, ckr, x1, dx2, wg, *consts, cos_t, sin_t)


def _fa_bwd(dx1, x, u, xb, hs, wg, g16, g32, w, lay, seq, tb):
    t_all, d = x.shape
    dr = lay.dr
    nblocks = w["w_rg"].shape[0]
    nblk = seq // tb
    nt = tb // SUBLANE
    per8 = tb // 8

    def body(dx1_ref, x_ref, u_ref, xb_ref, hs_ref, hh_ref, wg_ref, g16_ref, g32_ref, na, cw, wrg, brg, wig, big, lam,
             gx_ref, du_ref, dna_g, dcw_g, dcb_g, dbrg_g, dbig_g, dlam_g, dwrg_g, dwig_g, got_ref, sib_ref, own_ref,
             wint, wout, hpad, a_s, d_s, g_s, dxpad, carry, sems, send_sems, recv_sems, local_sem):
        b, jj = pl.program_id(0), pl.program_id(1)
        first_block = jj == nblk - 1
        scatter = _ScatterDirect(g16_ref, g32_ref, got_ref, sib_ref, own_ref, send_sems, recv_sems, local_sem, lay, G_GROUPS["early"])

        @pl.when((b == 0) & (jj == 0))
        def _():
            scatter.start()
            cps = _fetch(wg_ref, lay, "in_a", wint, sems, 0) + _fetch(wg_ref, lay, "out_a", wout, sems, N_CHIPS)
            for cp in cps:
                cp.start()
            dna_g[...] = jnp.zeros((1, d), F32)
            dcw_g[...] = jnp.zeros((4, dr), F32)
            dcb_g[...] = jnp.zeros((1, dr), F32)
            dbrg_g[...] = jnp.zeros((1, dr), F32)
            dbig_g[...] = jnp.zeros((1, dr), F32)
            dlam_g[...] = jnp.zeros((1, dr), F32)
            dwrg_g[...] = jnp.zeros((nblocks, LANE, LANE), F32)
            dwig_g[...] = jnp.zeros((nblocks, LANE, LANE), F32)
            for cp in cps:
                cp.wait()

        @pl.when(jj == 0)
        def _():
            dxpad[pl.ds(tb, 8), :] = jnp.zeros((8, dr), F32)
            carry[...] = jnp.zeros((8, dr), F32)

        keep = jnp.where(first_block, 0.0, 1.0)
        dx1v = dx1_ref[...]
        gate = u_ref[:, dr:]
        xpre = u_ref[:, :dr]
        hpad[pl.ds(0, 8), :] = hh_ref[...] * keep
        hpad[pl.ds(8, tb), :] = hs_ref[...]
        xb = xb_ref[...]
        xbb = xb.astype(BF16)
        r, i = _gates(xb, wrg, brg[...], wig, big[...], nblocks)
        sp = _softplus(-lam[...])
        log_a = -LRU_C * r * sp
        a, a2, nem = _decay(log_a)
        mult = jnp.sqrt(nem)
        sg = _sigmoid(gate)
        dy = _dot_nt(dx1v.astype(BF16), wout[...])
        hsv = hs_ref[...]
        dgate = dy * hsv * (sg * (1.0 + gate * (1.0 - sg)))
        a_s[...] = a
        d_s[...] = dy * (gate * sg)
        row = lax.broadcasted_iota(jnp.int32, (8, dr), 0)

        def step(k, c):
            r0 = pl.multiple_of((nt - 1 - k) * 8, 8)
            av = a_s[pl.ds(r0, 8), :]
            dv = d_s[pl.ds(r0, 8), :]
            qv = av * dv
            for s in (1, 2, 4):
                m = row < 8 - s
                a_sh = jnp.where(m, pltpu.roll(av, 8 - s, 0), 1.0)
                q_sh = jnp.where(m, pltpu.roll(qv, 8 - s, 0), 0.0)
                qv = qv + av * q_sh
                av = av * a_sh
            qv = qv + av * c
            g_s[pl.ds(r0, 8), :] = dv + jnp.where(row < 7, pltpu.roll(qv, 7, 0), c)
            return jnp.broadcast_to(qv[0:1, :], qv.shape)

        carry[...] = lax.fori_loop(0, nt, step, carry[...])
        g = g_s[...]
        ix = i * xb
        dlog_a = g * (hpad[pl.ds(7, tb), :] * a - ix * (a2 * lax.rsqrt(nem)))
        dix = g * mult
        dlam_g[...] += -jax.nn.sigmoid(-lam[...]) * jnp.sum(dlog_a * (-LRU_C * r), axis=0, keepdims=True)
        drg = dlog_a * (-LRU_C * sp) * r * (1.0 - r)
        dig = dix * xb * i * (1.0 - i)
        dbrg_g[...] += jnp.sum(drg, axis=0, keepdims=True)
        dbig_g[...] += jnp.sum(dig, axis=0, keepdims=True)
        drgb, digb = drg.astype(BF16), dig.astype(BF16)
        back = []
        for n in range(nblocks):
            cols = slice(n * LANE, (n + 1) * LANE)
            dwrg_g[n] += _dot_tn(xbb[:, cols], drgb[:, cols])
            dwig_g[n] += _dot_tn(xbb[:, cols], digb[:, cols])
            back.append(_dot_nt(drgb[:, cols], wrg[n]) + _dot_nt(digb[:, cols], wig[n]))
        dxb = dix * i + jnp.concatenate(back, axis=1)
        dcb_g[...] += jnp.sum(dxb, axis=0, keepdims=True)
        dxpad[pl.ds(0, tb), :] = dxb
        later = [dxb, dxpad[pl.ds(1, tb), :], dxpad[pl.ds(2, tb), :], dxpad[pl.ds(3, tb), :]]
        dxpad[pl.ds(tb, 8), :] = dxb[:8, :]
        dxpre = cw[3:4, :] * later[0] + cw[2:3, :] * later[1] + cw[1:2, :] * later[2] + cw[0:1, :] * later[3]
        for m in range(4):
            dcw_g[3 - m:4 - m, :] += jnp.sum(later[m] * xpre, axis=0, keepdims=True)
        du = jnp.concatenate([dxpre, dgate], axis=1).astype(BF16)
        du_ref[...] = du
        xv = x_ref[...]
        dxa, g1 = _rms_bwd(xv, _rinv(xv), na[...], _dot(du, wint[...]))
        dna_g[...] += g1
        gx_ref[...] = dx1v + dxa

        @pl.when((b == t_all // seq - 1) & (jj == nblk - 1))
        def _():
            scatter.finish()

    blk = lambda b, j: b * nblk + (nblk - 1 - j)
    tok = lambda c: pl.BlockSpec((tb, c), lambda b, j: (blk(b, j), 0))
    halo = pl.BlockSpec((8, dr), lambda b, j: (jnp.maximum(blk(b, j) * per8 - 1, 0), 0))
    consts = [w["norm_a"], w["conv_w"], w["w_rg"], w["b_rg"], w["w_ig"], w["b_ig"], w["lru_lambda"]]
    vec = lambda c: _full((1, c))
    blocks3 = (nblocks, LANE, LANE)
    return pl.pallas_call(
        body, name="fa_bwd", grid=(t_all // seq, nblk),
        in_specs=[tok(d), tok(d), tok(2 * dr), tok(dr), tok(dr), halo, ANY, ANY, ANY] + [_full(c.shape) for c in consts],
        out_specs=[tok(d), tok(2 * dr), vec(d), _full((4, dr)), vec(dr), vec(dr), vec(dr), vec(dr), _full(blocks3), _full(blocks3), ANY, ANY, ANY],
        out_shape=[_sds((t_all, d)), _sds((t_all, 2 * dr), BF16), _sds((1, d)), _sds((4, dr)), _sds((1, dr)), _sds((1, dr)), _sds((1, dr)),
                   _sds((1, dr)), _sds(blocks3), _sds(blocks3)] + _scatter_direct_shapes(lay, "early"),
        scratch_shapes=[pltpu.VMEM((2 * dr, d), BF16), pltpu.VMEM((dr, d), BF16), pltpu.VMEM((tb + 8, dr), F32),
                        pltpu.VMEM((tb, dr), F32), pltpu.VMEM((tb, dr), F32), pltpu.VMEM((tb, dr), F32), pltpu.VMEM((tb + 8, dr), F32),
                        pltpu.VMEM((8, dr), F32), pltpu.SemaphoreType.DMA((2 * N_CHIPS,))] + SCATTER_DIRECT_SEMS,
        compiler_params=_params(2),
    )(dx1, x, u, xb, hs, hs, wg, g16, g32, *consts)


def _mm_into(gbufs, a, bs, offs, name, bt):
    t_all, m = a.shape
    n = bs[0].shape[1]
    nb = len(bs)
    nsplit = nb if nb > 1 else 2 if m >= 1024 and (m // 2) % LANE == 0 else 1
    mh = m if nb > 1 else m // nsplit
    starts = list(offs) if nb > 1 else [offs[0] + h * mh for h in range(nsplit)]
    nt = t_all // bt
    nbuf = len(gbufs)
    twin = nbuf == 2

    def body(a_ref, *refs):
        b_refs, outs, acc, sems = refs[:nb], refs[nb + nbuf:nb + 2 * nbuf], refs[nb + 2 * nbuf], refs[-1]
        acc16 = refs[nb + 2 * nbuf + 1] if twin else None
        part, t = pl.program_id(0), pl.program_id(1)

        def out_copies(h):
            dst = pl.ds(starts[h], mh)
            copies = [pltpu.make_async_copy(acc.at[h], outs[0].at[dst, :], sems.at[0, h])]
            if twin:
                copies.append(pltpu.make_async_copy(acc16.at[h], outs[1].at[dst, :], sems.at[1, h]))
            return copies

        for h in range(nsplit):
            @pl.when(part == h)
            def _():
                def prod():
                    return _dot_tn(a_ref[...].astype(BF16), b_refs[h if nb > 1 else 0][...].astype(BF16))

                @pl.when(t == 0)
                def _():
                    acc[h] = prod()

                @pl.when(t > 0)
                def _():
                    acc[h] += prod()

                @pl.when(t == nt - 1)
                def _():
                    if twin:
                        acc16[h] = acc[h].astype(BF16)
                    for cp in out_copies(h):
                        cp.start()

        @pl.when((part == nsplit - 1) & (t == nt - 1))
        def _():
            for h in range(nsplit):
                for cp in out_copies(h):
                    cp.wait()

    if nb > 1:
        a_spec = pl.BlockSpec((bt, mh), lambda h, t: (t, 0))
        b_specs = [pl.BlockSpec((bt, n), lambda h, t, k=k: (jnp.where(h == k, t, 0), 0)) for k in range(nb)]
    else:
        a_spec = pl.BlockSpec((bt, mh), lambda h, t: (t, h))
        b_specs = [pl.BlockSpec((bt, n), lambda h, t: (t, 0))]
    scratch = [pltpu.VMEM((nsplit, mh, n), F32)] + ([pltpu.VMEM((nsplit, mh, n), BF16)] if twin else []) + [pltpu.SemaphoreType.DMA((2, nsplit))]
    return pl.pallas_call(
        body, name=name, grid=(nsplit, nt),
        in_specs=[a_spec] + b_specs + [ANY] * nbuf,
        out_specs=[ANY] * nbuf, out_shape=[_sds(g.shape, g.dtype) for g in gbufs], input_output_aliases={1 + nb + k: k for k in range(nbuf)},
        scratch_shapes=scratch, compiler_params=_params(2),
    )(a, *bs, *gbufs)


def _dw_in_a_exchange(du, h, rest, lay, bt):
    t_all, d = h.shape
    half = d // 2
    rows, rest_rows = lay.rows["in_a"], lay.rows["rest"]
    c_in, c_rest = lay.c_off["in_a"], lay.c_off["rest"]
    nt = t_all // bt
    xi, yi, _ = _place()
    order = jnp.stack([2 * (1 - xi) + (1 - yi), 2 * (1 - xi) + yi, 2 * xi + (1 - yi), 2 * xi + yi]).astype(jnp.int32)

    def body(order_ref, a_ref, b_ref, rest_ref, got_ref, own_ref, acc, sibbuf, part16, restv, rest_sib, rest_p, rest16, own_v, own_r,
             d2d_send, d2d_recv, ici_send, ici_recv, local_sems):
        x, y, c = _place()
        chips = [(1 - x, 1 - y), (1 - x, y), (x, 1 - y)]
        g, t = pl.program_id(0), pl.program_id(1)
        their_cols = pl.ds(pl.multiple_of((1 - c) * half, LANE), half)

        def my_half(v):
            return jnp.where(c == 0, v[:, :half], v[:, half:])

        def d2d(src, dst, k):
            return pltpu.make_async_remote_copy(src_ref=src, dst_ref=dst, send_sem=d2d_send.at[k], recv_sem=d2d_recv.at[k],
                                                device_id=(x, y, 1 - c), device_id_type=MESH)

        def group_swap(gg):
            return d2d(acc.at[gg % 2, :, their_cols], sibbuf.at[gg], gg)

        def rest_swap():
            return d2d(restv.at[:, their_cols], rest_sib, 4)

        def to_chip(k, src, off, nrows):
            px, py = chips[k]
            return pltpu.make_async_remote_copy(src_ref=src, dst_ref=got_ref.at[k, pl.ds(off, nrows), :], send_sem=ici_send.at[k],
                                                recv_sem=ici_recv.at[k], device_id=(px, py, c), device_id_type=MESH)

        def own_copy(src, off, nrows, k):
            return pltpu.make_async_copy(src, own_ref.at[pl.ds(off, nrows), :], local_sems.at[k])

        def finish_group(gg):
            group_swap(gg).wait()
            part = my_half(acc[gg % 2]) + sibbuf[gg]
            if gg < 3:
                part16[gg] = part.astype(BF16)
                to_chip(gg, part16.at[gg], c_in, rows).start()
            else:
                own_v[...] = part
                own_copy(own_v, c_in, rows, 1).start()

        @pl.when((g == 0) & (t == 0))
        def _():
            load = pltpu.make_async_copy(rest_ref, restv, local_sems.at[0])
            load.start()
            load.wait()
            rest_swap().start()

        @pl.when(t == 0)
        def _():
            acc[g % 2] = _dot_tn(a_ref[...], b_ref[...])

        @pl.when(t > 0)
        def _():
            acc[g % 2] += _dot_tn(a_ref[...], b_ref[...])

        for gg in range(4):
            @pl.when((g == gg) & (t == 0))
            def _():
                if gg == 0:
                    rest_swap().wait()
                    rest_p[...] = my_half(restv[...]) + rest_sib[...]
                    for k in range(3):
                        chip_rows = pl.ds(pl.multiple_of(order_ref[k] * rest_rows, SUBLANE), rest_rows)
                        rest16[k] = rest_p[chip_rows, :].astype(BF16)
                        to_chip(k, rest16.at[k], c_rest, rest_rows).start()
                    own_r[...] = rest_p[pl.ds(pl.multiple_of(order_ref[3] * rest_rows, SUBLANE), rest_rows), :]
                    own_copy(own_r, c_rest, rest_rows, 2).start()
                else:
                    finish_group(gg - 1)

            @pl.when((g == gg) & (t == nt - 1))
            def _():
                group_swap(gg).start()
                if gg == 3:
                    finish_group(3)
                    for k, (px, py) in enumerate(chips):
                        pltpu.make_async_remote_copy(src_ref=got_ref.at[k], dst_ref=got_ref.at[k], send_sem=ici_send.at[k], recv_sem=ici_recv.at[k],
                                                     device_id=(px, py, c), device_id_type=MESH).wait()
                    own_copy(own_v, c_in, rows, 1).wait()
                    own_copy(own_r, c_rest, rest_rows, 2).wait()

    return pl.pallas_call(
        body, name="dw_in_a",
        grid_spec=pltpu.PrefetchScalarGridSpec(
            num_scalar_prefetch=1, grid=(N_CHIPS, nt),
            in_specs=[pl.BlockSpec((bt, rows), lambda g, t, order: (t, order[g])), pl.BlockSpec((bt, d), lambda g, t, order: (t, 0)), ANY],
            out_specs=[ANY, ANY],
            scratch_shapes=[pltpu.VMEM((2, rows, d), F32), pltpu.VMEM((N_CHIPS, rows, half), F32), pltpu.VMEM((3, rows, half), BF16),
                            pltpu.VMEM((N_CHIPS * rest_rows, d), F32), pltpu.VMEM((N_CHIPS * rest_rows, half), F32),
                            pltpu.VMEM((N_CHIPS * rest_rows, half), F32), pltpu.VMEM((3, rest_rows, half), BF16),
                            pltpu.VMEM((rows, half), F32), pltpu.VMEM((rest_rows, half), F32),
                            pltpu.SemaphoreType.DMA((5,)), pltpu.SemaphoreType.DMA((5,)), pltpu.SemaphoreType.DMA((3,)), pltpu.SemaphoreType.DMA((3,)),
                            pltpu.SemaphoreType.DMA((3,))]),
        out_shape=_scatter_shapes(lay, "late", half), compiler_params=_params(2),
    )(order, du, h, rest)


def _mm_tn(a, b, name, bt):
    t_all, m = a.shape
    n = b.shape[1]

    def body(a_ref, b_ref, o_ref):
        @pl.when(pl.program_id(0) == 0)
        def _():
            o_ref[...] = jnp.zeros((m, n), F32)

        o_ref[...] += _dot_tn(a_ref[...].astype(BF16), b_ref[...].astype(BF16))

    return pl.pallas_call(
        body, name=name, grid=(t_all // bt,),
        in_specs=[pl.BlockSpec((bt, m), lambda t: (t, 0)), pl.BlockSpec((bt, n), lambda t: (t, 0))],
        out_specs=_full((m, n)), out_shape=_sds((m, n)),
        compiler_params=_params(1),
    )(a, b)


class _Gather8:
    def __init__(self, x_ref, out_ref, send_sems, recv_sems, local_sem):
        x, y, c = _place()
        self.c, self.me, self.sibling = c, (x, y, c), (x, y, 1 - c)
        self.chips = [(1 - x, 1 - y), (1 - x, y), (x, 1 - y)]
        self.x_ref, self.out_ref, self.send_sems, self.recv_sems, self.local_sem = x_ref, out_ref, send_sems, recv_sems, local_sem

    def _slot(self, px, py, pc):
        return self.out_ref.at[4 * px + 2 * py + pc]

    def _copy(self, k, blk, to, src=None):
        return pltpu.make_async_remote_copy(
            src_ref=self._slot(*blk) if src is None else src, dst_ref=self._slot(*blk), send_sem=self.send_sems.at[k],
            recv_sem=self.recv_sems.at[k], device_id=to, device_id_type=MESH)

    def _mine(self):
        return pltpu.make_async_copy(self.x_ref, self._slot(*self.me), self.local_sem)

    def _first(self):
        return [self._copy(0, self.me, self.sibling, src=self.x_ref)] + [
            self._copy(1 + j, self.me, (*chip, self.c), src=self.x_ref) for j, chip in enumerate(self.chips)]

    def _passed(self):
        return [self._copy(4 + j, (*chip, self.c), self.sibling) for j, chip in enumerate(self.chips)]

    def start(self):
        self._mine().start()
        for cp in self._first():
            cp.start()

    def forward(self):
        passed = self._passed()
        for j, chip in enumerate(self.chips):
            self._copy(1 + j, (*chip, self.c), self.me).wait_recv()
            passed[j].start()

    def finish(self):
        self._copy(0, self.sibling, self.me).wait_recv()
        for j, chip in enumerate(self.chips):
            self._copy(4 + j, (*chip, 1 - self.c), self.me).wait_recv()
        for cp in self._first() + self._passed():
            cp.wait_send()
        self._mine().wait()


class _ScatterDirect:
    def __init__(self, g16_ref, g32_ref, got_ref, sib_ref, own_ref, send_sems, recv_sems, local_sem, lay, order):
        self.x, self.y, self.c = _place()
        self.chips = [(1 - self.x, self.y), (self.x, 1 - self.y), (1 - self.x, 1 - self.y)]
        self.refs = (g16_ref, g32_ref, got_ref, sib_ref, own_ref, send_sems, recv_sems, local_sem)
        self.lay, self.order, self.half = lay, order, lay.d // 2

    def _src(self, ref, key, chip, h):
        start = pl.multiple_of(self.lay.g_off[key] + chip * self.lay.rows[key], ROW_ALIGN)
        return ref.at[pl.ds(start, self.lay.rows[key]), pl.ds(pl.multiple_of(h * self.half, LANE), self.half)]

    def _compact(self, ref, key):
        return ref.at[pl.ds(self.lay.c_off[key], self.lay.rows[key]), :]

    def start(self):
        g16_ref, g32_ref, got_ref, sib_ref, own_ref, send_sems, recv_sems, local_sem = self.refs
        x, y, c = self.x, self.y, self.c
        for key in self.order:
            pltpu.make_async_copy(self._src(g32_ref, key, 2 * x + y, c), self._compact(own_ref, key), local_sem).start()
            pltpu.make_async_remote_copy(
                src_ref=self._src(g32_ref, key, 2 * x + y, 1 - c), dst_ref=self._compact(sib_ref, key), send_sem=send_sems.at[6],
                recv_sem=recv_sems.at[6], device_id=(x, y, 1 - c), device_id_type=MESH).start()
        for k, (px, py) in enumerate(self.chips):
            for h in range(2):
                for key in self.order:
                    pltpu.make_async_remote_copy(
                        src_ref=self._src(g16_ref, key, 2 * px + py, h), dst_ref=self._compact(got_ref.at[2 * k + c], key),
                        send_sem=send_sems.at[2 * k + h], recv_sem=recv_sems.at[2 * k + c], device_id=(px, py, h), device_id_type=MESH).start()

    def finish(self):
        _, _, got_ref, sib_ref, own_ref, send_sems, recv_sems, local_sem = self.refs
        x, y, c = self.x, self.y, self.c
        for k, (px, py) in enumerate(self.chips):
            for h in range(2):
                whole = pltpu.make_async_remote_copy(src_ref=got_ref.at[2 * k + h], dst_ref=got_ref.at[2 * k + h], send_sem=send_sems.at[2 * k + h],
                                                     recv_sem=recv_sems.at[2 * k + h], device_id=(px, py, h), device_id_type=MESH)
                whole.wait_send()
                whole.wait_recv()
        pltpu.make_async_remote_copy(src_ref=sib_ref, dst_ref=sib_ref, send_sem=send_sems.at[6], recv_sem=recv_sems.at[6],
                                     device_id=(x, y, 1 - c), device_id_type=MESH).wait()
        pltpu.make_async_copy(own_ref, own_ref, local_sem).wait()


def _scatter_direct_shapes(lay, group):
    rows, half = lay.c_rows[group], lay.d // 2
    return [_sds((6, rows, half), BF16), _sds((rows, half), F32), _sds((rows, half), F32)]


SCATTER_DIRECT_SEMS = [pltpu.SemaphoreType.DMA((7,)), pltpu.SemaphoreType.DMA((7,)), pltpu.SemaphoreType.DMA]
GATHER_SEMS = [pltpu.SemaphoreType.DMA((7,)), pltpu.SemaphoreType.DMA((7,)), pltpu.SemaphoreType.DMA]


def _all_gather8(blocks, name):
    nb = len(blocks)

    def body(*refs):
        x_refs, out_refs = refs[:nb], refs[nb:2 * nb]
        send_sems, recv_sems, local_sems = refs[2 * nb:]
        gathers = [_Gather8(x_refs[n], out_refs[n], send_sems.at[n], recv_sems.at[n], local_sems.at[n]) for n in range(nb)]
        for g in gathers:
            g.start()
        for g in gathers:
            g.forward()
        for g in gathers:
            g.finish()

    return pl.pallas_call(
        body, name=name, out_shape=[_sds((8,) + b.shape, b.dtype) for b in blocks], in_specs=[ANY] * nb, out_specs=[ANY] * nb,
        scratch_shapes=[pltpu.SemaphoreType.DMA((nb, 7)), pltpu.SemaphoreType.DMA((nb, 7)), pltpu.SemaphoreType.DMA((nb,))],
    )(*blocks)


def _return_and_gather(mines, rep_block):
    n = len(mines)

    def body(*refs):
        src_refs, rep_ref, out_refs, rep_out = refs[:n], refs[n], refs[n + 1:2 * n + 1], refs[2 * n + 1]
        send_sems, recv_sems, g_send, g_recv, g_local = refs[2 * n + 2:]
        x, y, c = _place()
        copies = [pltpu.make_async_remote_copy(src_ref=src_refs[k], dst_ref=out_refs[k], send_sem=send_sems.at[k], recv_sem=recv_sems.at[k],
                                               device_id=(x, y, 1 - c), device_id_type=MESH) for k in range(n)]
        gather = _Gather8(rep_ref, rep_out, g_send, g_recv, g_local)
        for cp in copies:
            cp.start()
        gather.start()
        gather.forward()
        gather.finish()
        for cp in copies:
            cp.wait()

    return pl.pallas_call(
        body, name="rs_return", out_shape=[_sds(m.shape, m.dtype) for m in mines] + [_sds((8,) + rep_block.shape, rep_block.dtype)],
        in_specs=[ANY] * (n + 1), out_specs=[ANY] * (n + 1),
        scratch_shapes=[pltpu.SemaphoreType.DMA((n,)), pltpu.SemaphoreType.DMA((n,))] + GATHER_SEMS,
    )(*mines, rep_block)


def _scatter_shapes(lay, group, half):
    return [_sds((3, lay.c_rows[group], half), BF16), _sds((lay.c_rows[group], half), F32)]


def _sum_devices(own, sib, got, name):
    rows, half = own.shape
    rb = _row_block(rows)
    n = got.shape[0]

    def body(a_ref, s_ref, b_ref, o_ref):
        acc = a_ref[...] + s_ref[...]
        for k in range(n):
            acc = acc + b_ref[k].astype(F32)
        o_ref[...] = acc

    spec = pl.BlockSpec((rb, half), lambda i: (i, 0))
    return pl.pallas_call(
        body, name=name, grid=(rows // rb,), in_specs=[spec, spec, pl.BlockSpec((n, rb, half), lambda i: (0, i, 0))], out_specs=spec,
        out_shape=_sds((rows, half)), compiler_params=_params(1),
    )(own, sib, got)


def _sum_chips(own, got, name):
    rows, half = own.shape
    rb = _row_block(rows)

    def body(a_ref, b_ref, o_ref):
        o_ref[...] = ((a_ref[...] + b_ref[0].astype(F32)) + b_ref[1].astype(F32)) + b_ref[2].astype(F32)

    spec = pl.BlockSpec((rb, half), lambda i: (i, 0))
    return pl.pallas_call(
        body, name=name, grid=(rows // rb,), in_specs=[spec, pl.BlockSpec((3, rb, half), lambda i: (0, i, 0))], out_specs=spec,
        out_shape=_sds((rows, half)), compiler_params=_params(1),
    )(own, got)


def _adamw(w, g, m, v):
    m = ADAM_B1 * m + (1.0 - ADAM_B1) * g
    v = ADAM_B2 * v + (1.0 - ADAM_B2) * (g * g)
    m_hat = m / (1.0 - ADAM_B1 ** ADAM_STEP)
    v_hat = v / (1.0 - ADAM_B2 ** ADAM_STEP)
    return -ADAM_LR * (m_hat / (jnp.sqrt(v_hat) + ADAM_EPS) + ADAM_WD * w), m, v


def _adamw_rows(name, w, g, m, v):
    _, rows, cols = w.shape
    rb = _row_block(rows, 256)

    def body(w_ref, g_ref, m_ref, v_ref, d_ref, mo_ref, vo_ref):
        d_ref[...], mo_ref[...], vo_ref[...] = _adamw(w_ref[...], g_ref[...], m_ref[...], v_ref[...])

    spec = pl.BlockSpec((1, rb, cols), lambda i: (0, i, 0))
    return pl.pallas_call(
        body, name=name, grid=(rows // rb,), in_specs=[spec] * 4, out_specs=[spec] * 3, out_shape=[_sds(w.shape)] * 3,
        compiler_params=_params(1),
    )(w, g, m, v)


def _adamw_group(ws, gs, ms, vs):
    n = len(ws)

    def body(*refs):
        for k in range(n):
            w_ref, g_ref, m_ref, v_ref = (refs[j * n + k] for j in range(4))
            outs = _adamw(w_ref[...], g_ref[...], m_ref[...], v_ref[...])
            for j in range(3):
                refs[(4 + j) * n + k][...] = outs[j]

    outs = pl.pallas_call(
        body, name="adamw_small", out_shape=[_sds(w.shape) for w in ws] * 3,
        compiler_params=pltpu.CompilerParams(vmem_limit_bytes=VMEM_LIMIT),
    )(*ws, *gs, *ms, *vs)
    return outs[:n], outs[n:2 * n], outs[2 * n:]


def _gather_weights(sh, lay):
    c = lax.axis_index("c")
    d = lay.d
    uq = sh["w_uq"][0].astype(BF16)
    parts = {
        "in_b": sh["w_in_b"][0].T.astype(BF16), "in_a": sh["w_in_a"][0].T.astype(BF16), "out_a": sh["w_out_a"][0].astype(BF16),
        "out_b": sh["w_out_b"][0].astype(BF16), "uk": sh["w_uk"].astype(BF16).reshape(-1, d), "uv": sh["w_uv"].astype(BF16).reshape(-1, d),
        "uq_n": uq[:, :, :QK_NOPE].reshape(-1, d), "uq_r": jnp.pad(uq[:, :, QK_NOPE:], ((0, 0), (0, 0), (0, LANE - QK_ROPE))).reshape(-1, d),
        "dkv": jnp.pad(sh["w_dkv"].astype(BF16), ((0, 0), (0, LANE - QK_ROPE))).reshape(-1, d),
    }
    halves = {}
    for group, order in W_GROUPS.items():
        stack = jnp.concatenate([parts[k] for k in order], axis=0).reshape(2, lay.w_rows[group] // 2, d)
        halves[group] = lax.dynamic_index_in_dim(stack, c, 0, keepdims=False)
    small = jnp.concatenate([sh[k].reshape(-1) for k in SMALL])
    n_small = small.shape[0]
    width = _round_up(n_small, 2 * SUBLANE * LANE) // (2 * SUBLANE)
    small = jnp.pad(small, (0, 2 * SUBLANE * width - n_small)).reshape(2, SUBLANE, width)
    wg, sg = _all_gather8([halves["a"], lax.dynamic_index_in_dim(small, c, 0, keepdims=False)], "ag_weights")
    wg = wg.reshape(N_CHIPS, lay.w_rows["a"], d)
    sg = sg.reshape(N_CHIPS, 2 * SUBLANE * width)
    full, off = {}, 0
    for k in SMALL:
        n = sh[k].size
        piece = sg[:, off:off + n]
        off += n
        if k == "conv_w":
            full[k] = piece.reshape(N_CHIPS, 4, n // 4).transpose(1, 0, 2).reshape(4, n)
        else:
            full[k] = piece.reshape(1, N_CHIPS * n)
    return wg, halves["b"], full


def _chip_split(g, taps=False):
    if taps:
        n = g.shape[1] // N_CHIPS
        return g.reshape(4, N_CHIPS, n).transpose(1, 0, 2).reshape(N_CHIPS, 4 * n)
    return g.reshape(N_CHIPS, -1)


def kernel(x, norm_a, w_in_a, conv_w, conv_b, w_rg, b_rg, w_ig, b_ig, lru_lambda, w_out_a, norm_kv, w_dkv, kv_norm, w_uk, w_uv, norm_b, w_in_b, q_norm, w_uq, w_out_b, final_norm, loss_target, m_norm_a, m_w_in_a, m_conv_w, m_conv_b, m_w_rg, m_b_rg, m_w_ig, m_b_ig, m_lru_lambda, m_w_out_a, m_norm_kv, m_w_dkv, m_kv_norm, m_w_uk, m_w_uv, m_norm_b, m_w_in_b, m_q_norm, m_w_uq, m_w_out_b, m_final_norm, v_norm_a, v_w_in_a, v_conv_w, v_conv_b, v_w_rg, v_b_rg, v_w_ig, v_b_ig, v_lru_lambda, v_w_out_a, v_norm_kv, v_w_dkv, v_kv_norm, v_w_uk, v_w_uv, v_norm_b, v_w_in_b, v_q_norm, v_w_uq, v_w_out_b, v_final_norm):
    given = dict(locals())
    sh = {k: given[k] for k in WEIGHTS}
    ci = lax.axis_index("c")
    nb, seq, d = x.shape
    t_all = nb * seq
    tb_a, tb_b, ta, bt = min(TOKENS_A, seq), min(TOKENS_B, seq), min(TOKENS_ATTN, seq), min(TOKENS_MM, t_all)
    dr = conv_b.shape[1] * N_CHIPS
    qr, kvr, nheads = q_norm.shape[1], kv_norm.shape[0], w_uk.shape[1]
    hv = nheads * LANE
    n_small = sum(sh[k].size for k in SMALL)
    n_repl = sum(sh[k].size for k in REPL)
    lay = _Layout(d, dr, qr, kvr, hv, n_small, n_repl)
    half = d // 2

    wga, wb_half, w = _gather_weights(sh, lay)
    w.update({"w_rg": w_rg[0].astype(BF16), "w_ig": w_ig[0].astype(BF16), "norm_kv": norm_kv[None, :],
              "kv_norm": kv_norm[None, :], "final_norm": final_norm[None, :], "norm_b": norm_b, "q_norm": q_norm})
    cos_t, sin_t = _rope_tables(seq)

    x0 = x.reshape(t_all, d)
    x1, u, hs, h, y, xb, wgb = _fa_fwd(x0, wga, wb_half, w, lay, seq, min(TOKENS_A_FWD, seq))
    wgb = wgb.reshape(N_CHIPS, lay.w_rows["b"], d)
    w["w_dkv_p"] = wgb[:, lay.w_off["dkv"]:lay.w_off["dkv"] + lay.rows["dkv"], :].reshape(d, kvr + LANE)
    qn, qrp, kn, kr, v, ub, ckr, hb, hk, cq, ckv = _fb_fwd(x1, wgb, w, lay, cos_t, sin_t, seq, tb_b)
    o, lse = _attn_fwd(qn, qrp, kn, kr, v, seq, ta)
    loss, g_final_norm, yb, dx2, do, dgate, delta = _head(o, ub, x1, loss_target.reshape(t_all, d), wgb, w, lay, tb_b)
    dqn, dqr, dkn, dkr, dv = _attn_bwd(qn, qrp, kn, kr, v, do, lse, delta, seq, ta)
    dx1, dqr_pre, dqn_pre, dub, dckr, g_q_norm, g_norm_b, g_kv_norm, g_norm_kv = _fb_bwd(
        dqn, dqr, dkn, dkr, dv, dgate, ub, ckr, x1, dx2, wgb, w, lay, cos_t, sin_t, seq, tb_b)
    loss = lax.psum(loss[0, 0], ("x", "y", "c"))

    gbufs = [lax.empty((lay.g_rows["early"], d), F32), lax.empty((lay.g_rows["early"], d), BF16)]
    for keys, a, bs in ((("in_b",), dub, (hb,)), (("out_a",), y, (dx1,)), (("out_b",), yb, (dx2,)), (("uk", "uv"), ckv, (dkn, dv)),
                        (("uq_n", "uq_r"), cq, (dqn_pre, dqr_pre))):
        gbufs = _mm_into(gbufs, a, bs, [lay.g_off[k] for k in keys], "dw_" + "_".join(keys), bt)
    g_dkv = _mm_tn(hk, dckr, "dw_dkv", bt)
    gx, du, g_norm_a, g_conv_w, g_conv_b, g_b_rg, g_b_ig, g_lam, g_w_rg, g_w_ig, others, sib, own = _fa_bwd(
        dx1, x0, u, xb, hs, wga, gbufs[1], gbufs[0], w, lay, seq, tb_a)
    mine_early = _sum_devices(own, sib, others, "rs_sum_early")

    small = jnp.concatenate([_chip_split(g_norm_a), _chip_split(g_conv_w, taps=True), _chip_split(g_conv_b), _chip_split(g_b_rg),
                             _chip_split(g_b_ig), _chip_split(g_lam)], axis=1)
    small = jnp.pad(small, ((0, 0), (0, lay.small_rows * d - small.shape[1]))).reshape(N_CHIPS, lay.small_rows, d)
    repl_parts = {"w_rg": g_w_rg, "w_ig": g_w_ig, "norm_kv": g_norm_kv, "kv_norm": g_kv_norm, "norm_b": g_norm_b, "q_norm": g_q_norm,
                  "final_norm": g_final_norm}
    repl = jnp.concatenate([repl_parts[k].reshape(-1) for k in REPL])
    repl = jnp.pad(repl, (0, N_CHIPS * lay.repl_rows * d - n_repl)).reshape(N_CHIPS, lay.repl_rows, d)
    pad_rows = lay.rows["rest"] - lay.rows["dkv"] - lay.small_rows - lay.repl_rows
    rest = jnp.concatenate([g_dkv.reshape(N_CHIPS, lay.rows["dkv"], d), small, repl, jnp.zeros((N_CHIPS, pad_rows, d), F32)], axis=1)
    others, own = _dw_in_a_exchange(du, h, rest.reshape(N_CHIPS * lay.rows["rest"], d), lay, bt)
    mine_late = _sum_chips(own, others, "rs_sum_chips_late")

    r0 = lay.c_off["rest"] + lay.rows["dkv"] + lay.small_rows
    theirs_early, theirs_late, rep_all = _return_and_gather([mine_early, mine_late], mine_late[r0:r0 + lay.repl_rows])
    red = {}
    for group, mine, theirs in (("early", mine_early, theirs_early), ("late", mine_late, theirs_late)):
        red[group] = jnp.concatenate([jnp.where(ci == 0, mine, theirs), jnp.where(ci == 0, theirs, mine)], axis=1)
    rep_flat =rep_all.reshape(N_CHIPS, 2, lay.repl_rows, half).transpose(0, 2, 1, 3).reshape(-1)

    def rows(key):
        group = "late" if key in G_GROUPS["late"] else "early"
        return red[group][lay.c_off[key]:lay.c_off[key] + lay.rows[key]]

    grads = {"w_in_b": rows("in_b").T[None], "w_in_a": rows("in_a").T[None], "w_out_a": rows("out_a")[None], "w_out_b": rows("out_b")[None],
             "w_uk": rows("uk").reshape(w_uk.shape), "w_uv": rows("uv").reshape(w_uv.shape)}
    uq_n = rows("uq_n").reshape(qr // N_CHIPS, nheads, LANE)
    uq_r = rows("uq_r").reshape(qr // N_CHIPS, nheads, LANE)[:, :, :QK_ROPE]
    grads["w_uq"] = jnp.concatenate([uq_n, uq_r], axis=2)[None]
    rest_red = rows("rest")
    grads["w_dkv"] = rest_red[:lay.rows["dkv"]].reshape(d // N_CHIPS, kvr + LANE)[:, :kvr + QK_ROPE]
    small_red = rest_red[lay.rows["dkv"]:lay.rows["dkv"] + lay.small_rows].reshape(-1)
    off = 0
    for k in SMALL:
        n = sh[k].size
        grads[k] = small_red[off:off + n].reshape(sh[k].shape)
        off += n
    off = 0
    for k in REPL:
        n = sh[k].size
        grads[k] = rep_flat[off:off + n].reshape(sh[k].shape)
        off += n

    new = {}
    for k in ("w_in_a", "w_in_b", "w_out_a", "w_out_b"):
        view = (lambda a: jnp.swapaxes(a, 1, 2)) if k in TRANSPOSED else (lambda a: a)
        outs = _adamw_rows("adamw_" + k, view(sh[k]), view(grads[k]), view(given["m_" + k]), view(given["v_" + k]))
        new[k] = tuple(view(a) for a in outs)
    rest_names = [k for k in WEIGHTS if k not in new]

    def as2d(k, a):
        return a.T if k in TRANSPOSED else a[None, :] if a.ndim == 1 else a

    ds, ms, vs = _adamw_group([as2d(k, sh[k]) for k in rest_names], [as2d(k, grads[k]) for k in rest_names],
                              [as2d(k, given["m_" + k]) for k in rest_names], [as2d(k, given["v_" + k]) for k in rest_names])
    for n, k in enumerate(rest_names):
        new[k] = tuple((a.T if k in TRANSPOSED else a).reshape(sh[k].shape) for a in (ds[n], ms[n], vs[n]))
    return (loss, gx.reshape(nb, seq, d), *[grads[k] for k in WEIGHTS], *[new[k][0] for k in WEIGHTS], *[new[k][1] for k in WEIGHTS],
            *[new[k][2] for k in WEIGHTS])
```

```python
import jax
import jax.numpy as jnp
from jax import lax
from jax.experimental import pallas as pl
from jax.experimental.pallas import tpu as pltpu

F32, BF16 = jnp.float32, jnp.bfloat16
EPS = 1e-6
LRU_C = 8.0
ROPE_THETA = 10000.0
QK_NOPE, QK_ROPE = 128, 64
ATTN_SCALE = (QK_NOPE + QK_ROPE) ** -0.5
LN2 = 0.6931471805599453
Q_SCALE = ATTN_SCALE / LN2
ATTN_HEADS, ATTN_HEADS_BWD = 4, 2
ATTN_ROWS = 64
LANE = 128
SUBLANE = 8
ROW_ALIGN = 32
VMEM_LIMIT = 60000 * 1024
ADAM_LR, ADAM_B1, ADAM_B2, ADAM_EPS, ADAM_WD, ADAM_STEP = 0.001, 0.9, 0.999, 1e-08, 0.01, 10
MESH = pl.DeviceIdType.MESH
ANY = pl.BlockSpec(memory_space=pl.ANY)
N_CHIPS = 4
TOKENS_A, TOKENS_B, TOKENS_ATTN, TOKENS_MM = 256, 512, 512, 2048
TOKENS_A_FWD = 512

SMALL = ("norm_a", "conv_w", "conv_b", "b_rg", "b_ig", "lru_lambda")
REPL = ("w_rg", "w_ig", "norm_kv", "kv_norm", "norm_b", "q_norm", "final_norm")
TRANSPOSED = ("w_in_b", "w_dkv")
WEIGHTS = ("norm_a", "w_in_a", "conv_w", "conv_b", "w_rg", "b_rg", "w_ig", "b_ig", "lru_lambda", "w_out_a", "norm_kv",
           "w_dkv", "kv_norm", "w_uk", "w_uv", "norm_b", "w_in_b", "q_norm", "w_uq", "w_out_b", "final_norm")
W_GROUPS = {"a": ("in_a", "out_a"), "b": ("in_b", "out_b", "uk", "uv", "uq_n", "uq_r", "dkv")}
G_GROUPS = {"early": ("in_b", "out_a", "out_b", "uk", "uv", "uq_n", "uq_r"), "late": ("in_a", "rest")}


def _sds(shape, dtype=F32):
    return jax.ShapeDtypeStruct(tuple(shape), dtype)


def _params(n_grid):
    return pltpu.CompilerParams(dimension_semantics=("arbitrary",) * n_grid, vmem_limit_bytes=VMEM_LIMIT)


def _full(shape):
    nd = len(shape)
    return pl.BlockSpec(tuple(shape), lambda *g: (0,) * nd)


def _round_up(n, k):
    return -(-n // k) * k


def _row_block(rows, cap=512):
    best = SUBLANE
    for r in range(SUBLANE, min(rows, cap) + 1, SUBLANE):
        if rows % r == 0:
            best = r
    return best


def _place():
    return lax.axis_index("x"), lax.axis_index("y"), lax.axis_index("c")


class _Layout:
    def __init__(self, d, dr, qr, kvr, hv, n_small, n_repl):
        assert hv == d, "the packed rows are D_MODEL wide, which must equal heads * 128"
        self.d, self.dr, self.qr, self.kvr, self.hv = d, dr, qr, kvr, hv
        per_chip = {"in_b": (qr + hv) // N_CHIPS, "in_a": 2 * dr // N_CHIPS, "out_a": dr // N_CHIPS, "out_b": hv // N_CHIPS,
                    "uk": kvr // N_CHIPS, "uv": kvr // N_CHIPS, "uq_n": qr // N_CHIPS, "uq_r": qr // N_CHIPS,
                    "dkv": (d // N_CHIPS) * (kvr + LANE) // d}
        assert all(r % ROW_ALIGN == 0 for r in per_chip.values()), per_chip
        self.small_rows = _round_up(-(-n_small // d), SUBLANE)
        self.repl_rows = _round_up(-(-n_repl // (N_CHIPS * d)), SUBLANE)
        per_chip["rest"] = _round_up(per_chip["dkv"] + self.small_rows + self.repl_rows, ROW_ALIGN)
        self.rows = per_chip
        self.w_off, self.w_rows = {}, {}
        for group, order in W_GROUPS.items():
            off = 0
            for k in order:
                self.w_off[k] = off
                off += per_chip[k]
            assert off % ROW_ALIGN == 0, (group, off)
            self.w_rows[group] = off
        self.g_off, self.c_off, self.c_rows, self.g_rows = {}, {}, {}, {}
        for group, order in G_GROUPS.items():
            off = 0
            for k in order:
                self.c_off[k] = off
                self.g_off[k] = N_CHIPS * off
                off += per_chip[k]
            self.c_rows[group] = off
            self.g_rows[group] = N_CHIPS * off


def _dot(a, b):
    return jnp.dot(a, b, preferred_element_type=F32)


def _dot_nt(a, b):
    return lax.dot_general(a, b, (((1,), (1,)), ((), ())), preferred_element_type=F32)


def _dot_tn(a, b):
    return lax.dot_general(a, b, (((0,), (0,)), ((), ())), preferred_element_type=F32)


def _rinv(x):
    return lax.rsqrt(jnp.mean(x * x, axis=-1, keepdims=True) + EPS)


def _rms_bwd(x, rinv, g, dy):
    z = dy * g
    dx = rinv * z - x * (rinv * rinv * rinv) * jnp.mean(z * x, axis=-1, keepdims=True)
    dg = jnp.sum(dy * (x * rinv), axis=0, keepdims=True)
    return dx, dg


def _softplus(z):
    return jnp.maximum(z, 0.0) + jnp.log1p(jnp.exp(-jnp.abs(z)))


def _sigmoid(x):
    return 0.5 * jnp.tanh(0.5 * x) + 0.5


def _decay(log_a):
    a = jnp.exp(log_a)
    a2 = a * a
    return a, a2, -jnp.tanh(log_a) * (a2 + 1.0)


def _swap_halves(x):
    w = x.shape[1]
    lane = lax.broadcasted_iota(jnp.int32, x.shape, 1)
    return jnp.where(lane % QK_ROPE < QK_ROPE // 2, pltpu.roll(x, w - QK_ROPE // 2, 1), pltpu.roll(x, QK_ROPE // 2, 1))


def _rope_tables(seq):
    pos = jnp.arange(seq, dtype=F32)
    inv = ROPE_THETA ** (-jnp.arange(0, QK_ROPE, 2, dtype=F32) / QK_ROPE)
    ang = pos[:, None] * inv[None, :]
    cos, sin = jnp.cos(ang), jnp.sin(ang)
    zero = jnp.zeros((seq, LANE - QK_ROPE), F32)
    return jnp.concatenate([cos, cos, zero], 1), jnp.concatenate([-sin, sin, zero], 1)


def _fetch(wg_ref, lay, key, dst, sems, k0):
    rows = lay.rows[key]
    return [pltpu.make_async_copy(wg_ref.at[p, pl.ds(lay.w_off[key], rows), :], dst.at[pl.ds(p * rows, rows), :], sems.at[k0 + p])
            for p in range(N_CHIPS)]


def _gates(xb, wrg_ref, brg, wig_ref, big, nblocks):
    xbb = xb.astype(BF16)
    rg = [_dot(xbb[:, n * LANE:(n + 1) * LANE], wrg_ref[n]) for n in range(nblocks)]
    ig = [_dot(xbb[:, n * LANE:(n + 1) * LANE], wig_ref[n]) for n in range(nblocks)]
    r = _sigmoid(jnp.concatenate(rg, axis=1) + brg)
    i = _sigmoid(jnp.concatenate(ig, axis=1) + big)
    return r, i


def _conv(xpad, cw_ref, cb, tb):
    return (cb + cw_ref[3:4, :] * xpad[pl.ds(8, tb), :] + cw_ref[2:3, :] * xpad[pl.ds(7, tb), :]
            + cw_ref[1:2, :] * xpad[pl.ds(6, tb), :] + cw_ref[0:1, :] * xpad[pl.ds(5, tb), :])


def _fa_fwd(x, wg, wb_half, w, lay, seq, tb):
    t_all, d = x.shape
    dr = lay.dr
    nblocks = w["w_rg"].shape[0]
    nblk = seq // tb
    nt = tb // SUBLANE
    nsteps = (t_all // seq) * nblk

    def body(x_ref, wg_ref, wbh_ref, na, cw, cb, wrg, brg, wig, big, lam, x1_ref, u_ref, hs_ref, h_ref, y_ref, xb_ref, wb_ref,
             wint, wout, xpad, a_s, b_s, carry, sems, send_sems, recv_sems, local_sem):
        step_no = pl.program_id(0) * nblk + pl.program_id(1)
        gather = _Gather8(wbh_ref, wb_ref, send_sems, recv_sems, local_sem)

        @pl.when(step_no == 0)
        def _():
            gather.start()
            cps = _fetch(wg_ref, lay, "in_a", wint, sems, 0) + _fetch(wg_ref, lay, "out_a", wout, sems, N_CHIPS)
            for cp in cps:
                cp.start()
            for cp in cps:
                cp.wait()

        @pl.when(step_no == nsteps // 2)
        def _():
            gather.forward()

        @pl.when(pl.program_id(1) == 0)
        def _():
            xpad[pl.ds(0, 8), :] = jnp.zeros((8, dr), F32)
            carry[...] = jnp.zeros((8, dr), F32)

        xv = x_ref[...]
        h = (xv * _rinv(xv) * na[...]).astype(BF16)
        h_ref[...] = h
        u = _dot_nt(h, wint[...])
        u_ref[...] = u
        xpre, gate = u[:, :dr], u[:, dr:]
        xpad[pl.ds(8, tb), :] = xpre
        xb = _conv(xpad, cw, cb[...], tb)
        xb_ref[...] = xb
        xpad[pl.ds(0, 8), :] = xpre[tb - 8:, :]
        r, i = _gates(xb, wrg, brg[...], wig, big[...], nblocks)
        log_a = -LRU_C * r * _softplus(-lam[...])
        a, _, nem = _decay(log_a)
        a_s[...] = a
        b_s[...] = jnp.sqrt(nem) * (i * xb)
        row = lax.broadcasted_iota(jnp.int32, (8, dr), 0)

        def step(t, c):
            r0 = pl.multiple_of(t * 8, 8)
            a = a_s[pl.ds(r0, 8), :]
            b = b_s[pl.ds(r0, 8), :]
            for s in (1, 2, 4):
                m = row >= s
                a_sh = jnp.where(m, pltpu.roll(a, s, 0), 1.0)
                b_sh = jnp.where(m, pltpu.roll(b, s, 0), 0.0)
                b = a * b_sh + b
                a = a * a_sh
            hh = b + a * c
            hs_ref[pl.ds(r0, 8), :] = hh
            return jnp.broadcast_to(hh[7:8, :], hh.shape)

        carry[...] = lax.fori_loop(0, nt, step, carry[...])
        y = (hs_ref[...] * (gate * _sigmoid(gate))).astype(BF16)
        y_ref[...] = y
        x1_ref[...] = xv + _dot(y, wout[...])

        @pl.when(step_no == nsteps - 1)
        def _():
            gather.finish()

    tok = lambda c: pl.BlockSpec((tb, c), lambda b, j: (b * nblk + j, 0))
    consts = [w["norm_a"], w["conv_w"], w["conv_b"], w["w_rg"], w["b_rg"], w["w_ig"], w["b_ig"], w["lru_lambda"]]
    return pl.pallas_call(
        body, name="fa_fwd", grid=(t_all // seq, nblk),
        in_specs=[tok(d), ANY, ANY] + [_full(c.shape) for c in consts],
        out_specs=[tok(d), tok(2 * dr), tok(dr), tok(d), tok(dr), tok(dr), ANY],
        out_shape=[_sds((t_all, d)), _sds((t_all, 2 * dr)), _sds((t_all, dr)), _sds((t_all, d), BF16), _sds((t_all, dr), BF16),
                   _sds((t_all, dr)), _sds((8,) + wb_half.shape, BF16)],
        scratch_shapes=[pltpu.VMEM((2 * dr, d), BF16), pltpu.VMEM((dr, d), BF16), pltpu.VMEM((tb + 8, dr), F32), pltpu.VMEM((tb, dr), F32),
                        pltpu.VMEM((tb, dr), F32), pltpu.VMEM((8, dr), F32), pltpu.SemaphoreType.DMA((2 * N_CHIPS,))] + GATHER_SEMS,
        compiler_params=_params(2),
    )(x, wg, wb_half, *consts)


def _fb_fwd(x1, wg, w, lay, cos_t, sin_t, seq, tb):
    t_all, d = x1.shape
    kvr, qr, hv = lay.kvr, lay.qr, lay.hv
    nheads = hv // LANE
    npos = seq // tb

    def body(x_ref, wg_ref, nkv, nb, wdkv, kvn, qn, cos_ref, sin_ref,
             qn_o, qr_o, kn_o, kr_o, v_o, ub_o, ckr_o, hb_o, hk_o, cq_o, ckv_o, winb, wuk, wuv, wuqn, wuqr, sems):
        @pl.when(pl.program_id(0) == 0)
        def _():
            cps = []
            for n, (key, dst) in enumerate((("in_b", winb), ("uk", wuk), ("uv", wuv), ("uq_n", wuqn), ("uq_r", wuqr))):
                cps += _fetch(wg_ref, lay, key, dst, sems, n * N_CHIPS)
            for cp in cps:
                cp.start()
            for cp in cps:
                cp.wait()

        xv = x_ref[...]
        xh = xv * _rinv(xv)
        hk = (xh * nkv[...]).astype(BF16)
        hb = (xh * nb[...]).astype(BF16)
        hk_o[...] = hk
        hb_o[...] = hb
        cos, sin = cos_ref[...], sin_ref[...]
        ckr = _dot(hk, wdkv[...])
        ckr_o[...] = ckr
        ckv_pre = ckr[:, :kvr]
        ckv = (ckv_pre * _rinv(ckv_pre) * kvn[...]).astype(BF16)
        ckv_o[...] = ckv
        kr = ckr[:, kvr:]
        kr_o[...] = (kr * cos + _swap_halves(kr) * sin).astype(BF16)
        kn_o[...] = _dot(ckv, wuk[...]).astype(BF16)
        v_o[...] = _dot(ckv, wuv[...]).astype(BF16)
        ub = _dot_nt(hb, winb[...])
        ub_o[...] = ub
        cq_pre = ub[:, :qr]
        cq = (cq_pre * _rinv(cq_pre) * qn[...]).astype(BF16)
        cq_o[...] = cq
        qn_o[...] = (_dot(cq, wuqn[...]) * Q_SCALE).astype(BF16)
        qrope = _dot(cq, wuqr[...]) * Q_SCALE
        qr_o[...] = (qrope * jnp.tile(cos, (1, nheads)) + _swap_halves(qrope) * jnp.tile(sin, (1, nheads))).astype(BF16)

    tok = lambda c: pl.BlockSpec((tb, c), lambda i: (i, 0))
    pos = pl.BlockSpec((tb, LANE), lambda i: (i % npos, 0))
    consts = [w["norm_kv"], w["norm_b"], w["w_dkv_p"], w["kv_norm"], w["q_norm"]]
    outs = [(hv, BF16), (hv, BF16), (hv, BF16), (LANE, BF16), (hv, BF16), (qr + hv, F32), (kvr + LANE, F32), (d, BF16), (d, BF16), (qr, BF16), (kvr, BF16)]
    return pl.pallas_call(
        body, name="fb_fwd", grid=(t_all // tb,),
        in_specs=[tok(d), ANY] + [_full(c.shape) for c in consts] + [pos, pos],
        out_specs=[tok(c) for c, _ in outs],
        out_shape=[_sds((t_all, c), dt) for c, dt in outs],
        scratch_shapes=[pltpu.VMEM((qr + hv, d), BF16), pltpu.VMEM((kvr, d), BF16), pltpu.VMEM((kvr, d), BF16), pltpu.VMEM((qr, d), BF16),
                        pltpu.VMEM((qr, d), BF16), pltpu.SemaphoreType.DMA((5 * N_CHIPS,))],
        compiler_params=_params(1),
    )(x1, wg, *consts, cos_t, sin_t)


def _causal_mask(row0, col0, nrows, ncols):
    rows = row0 + lax.broadcasted_iota(jnp.int32, (nrows, ncols), 0)
    cols = col0 + lax.broadcasted_iota(jnp.int32, (nrows, ncols), 1)
    return cols <= rows


def _attn_fwd(qn, qr, kn, kr, v, seq, ta):
    t_all, hv = qn.shape
    nheads, nb, na = hv // LANE, t_all // seq, seq // ta

    reps = ta // LANE
    hp = ATTN_HEADS
    wide = hp * LANE

    def body(qn_ref, qr_ref, kn_ref, kr_ref, v_ref, o_ref, lse_ref, m_s, l_s, acc_s):
        i = pl.program_id(2)
        m_s[...] = jnp.full((ta, wide), -1e30, F32)
        l_s[...] = jnp.zeros((ta, wide), F32)
        acc_s[...] = jnp.zeros((ta, wide), F32)
        heads = [slice(n * LANE, (n + 1) * LANE) for n in range(hp)]
        qs = [jnp.concatenate([qn_ref[:, hd], qr_ref[:, hd]], axis=1) for hd in heads]

        def tile(j, diagonal):
            cols = pl.ds(pl.multiple_of(j * ta, ta), ta)
            k_rope = kr_ref[cols, :]
            for q, hd in zip(qs, heads):
                k = jnp.concatenate([kn_ref[cols, hd], k_rope], axis=1)
                s = _dot_nt(q, k)
                if diagonal:
                    s = jnp.where(_causal_mask(0, 0, ta, ta), s, -1e30)
                m_prev = m_s[:, hd]
                m_new = jnp.maximum(m_prev, jnp.max(s, axis=1, keepdims=True))
                p = jnp.exp2(s - jnp.tile(m_new, (1, reps)))
                alpha = jnp.exp2(m_prev - m_new)
                l_s[:, hd] = alpha * l_s[:, hd] + jnp.sum(p, axis=1, keepdims=True)
                acc_s[:, hd] = alpha * acc_s[:, hd] + _dot(p.astype(BF16), v_ref[cols, hd])
                m_s[:, hd] = m_new

        def off_diagonal(j, carry):
            tile(j, False)
            return carry

        lax.fori_loop(0, i, off_diagonal, 0)
        tile(i, True)
        o_ref[...] = (acc_s[...] / l_s[...]).astype(BF16)
        lse_ref[...] = m_s[...] + jnp.log2(l_s[...])

    qspec = pl.BlockSpec((ta, wide), lambda b, h, i: (b * na + i, h))
    kspec = pl.BlockSpec((seq, wide), lambda b, h, i: (b, h))
    krspec = pl.BlockSpec((seq, LANE), lambda b, h, i: (b, 0))
    return pl.pallas_call(
        body, name="attn_fwd", grid=(nb, nheads // hp, na),
        in_specs=[qspec, qspec, kspec, krspec, kspec],
        out_specs=[qspec, qspec],
        out_shape=[_sds((t_all, hv), BF16), _sds((t_all, hv))],
        scratch_shapes=[pltpu.VMEM((ta, wide), F32)] * 3,
        compiler_params=_params(3),
    )(qn, qr, kn, kr, v)


def _attn_bwd(qn, qr, kn, kr, v, do, lse, delta, seq, ta):
    t_all, hv = qn.shape
    nheads, nb, na = hv // LANE, t_all // seq, seq // ta

    reps = ta // LANE
    nchunks = ta // ATTN_ROWS

    hp = ATTN_HEADS_BWD
    wide = hp * LANE
    heads = [slice(n * LANE, (n + 1) * LANE) for n in range(hp)]

    def body(qn_ref, qr_ref, kn_ref, kr_ref, v_ref, do_ref, lse_ref, dl_ref, dqn_out, dqr_out, dkn_ref, dkr_ref, dv_ref,
             s_s, dp_s, p_s, ds_s, dk_s, dv_s, dqn_ref, dqr_ref):
        j = pl.program_id(2)

        @pl.when(j == 0)
        def _():
            dqn_ref[...] = jnp.zeros((seq, wide), F32)
            dqr_ref[...] = jnp.zeros((seq, wide), F32)

        dk_s[...] = jnp.zeros((hp, ta, 2 * LANE), F32)
        dv_s[...] = jnp.zeros((hp, ta, LANE), F32)
        k_rope = kr_ref[...]
        ks = [jnp.concatenate([kn_ref[:, hd], k_rope], axis=1) for hd in heads]

        def tile(i, diagonal):
            rows_i = pl.ds(pl.multiple_of(i * ta, ta), ta)
            for n, hd in enumerate(heads):
                q = jnp.concatenate([qn_ref[rows_i, hd], qr_ref[rows_i, hd]], axis=1)
                do_b = do_ref[rows_i, hd]
                s_s[n] = _dot_nt(q, ks[n])
                dp_s[n] = _dot_nt(do_b, v_ref[:, hd])
                for c in range(nchunks):
                    rows = pl.ds(c * ATTN_ROWS, ATTN_ROWS)
                    seq_rows = pl.ds(pl.multiple_of(i * ta + c * ATTN_ROWS, ATTN_ROWS), ATTN_ROWS)
                    s = s_s[n, rows, :]
                    if diagonal:
                        s = jnp.where(_causal_mask(c * ATTN_ROWS, 0, ATTN_ROWS, ta), s, -1e30)
                    p = jnp.exp2(s - jnp.tile(lse_ref[seq_rows, hd], (1, reps)))
                    p_s[n, rows, :] = p.astype(BF16)
                    ds_s[n, rows, :] = (p * (dp_s[n, rows, :] - jnp.tile(dl_ref[seq_rows, hd], (1, reps)))).astype(BF16)
                dv_s[n] += _dot_tn(p_s[n], do_b)
                ds = ds_s[n]
                dk_s[n] += _dot_tn(ds, q)
                dq = _dot(ds, ks[n])
                dqn_ref[rows_i, hd] += dq[:, :LANE]
                dqr_ref[rows_i, hd] += dq[:, LANE:]

        def off_diagonal(i, carry):
            tile(i, False)
            return carry

        tile(j, True)
        lax.fori_loop(j + 1, na, off_diagonal, 0)
        for n, hd in enumerate(heads):
            dkn_ref[:, hd] = (dk_s[n, :, :LANE] * LN2).astype(BF16)
            dkr_ref[:, hd] = (dk_s[n, :, LANE:] * LN2).astype(BF16)
            dv_ref[:, hd] = dv_s[n].astype(BF16)

        @pl.when(j == na - 1)
        def _():
            dqn_out[...] = dqn_ref[...].astype(BF16)
            dqr_out[...] = dqr_ref[...].astype(BF16)

    qspec = pl.BlockSpec((seq, wide), lambda b, h, j: (b, h))
    kspec = pl.BlockSpec((ta, wide), lambda b, h, j: (b * na + j, h))
    krspec = pl.BlockSpec((ta, LANE), lambda b, h, j: (b * na + j, 0))
    return pl.pallas_call(
        body, name="attn_bwd", grid=(nb, nheads // hp, na),
        in_specs=[qspec, qspec, kspec, krspec, kspec, qspec, qspec, qspec],
        out_specs=[qspec, qspec, kspec, kspec, kspec],
        out_shape=[_sds((t_all, hv), BF16)] * 5,
        scratch_shapes=[pltpu.VMEM((hp, ta, ta), F32), pltpu.VMEM((hp, ta, ta), F32), pltpu.VMEM((hp, ta, ta), BF16), pltpu.VMEM((hp, ta, ta), BF16),
                        pltpu.VMEM((hp, ta, 2 * LANE), F32), pltpu.VMEM((hp, ta, LANE), F32), pltpu.VMEM((seq, wide), F32), pltpu.VMEM((seq, wide), F32)],
        compiler_params=_params(3),
    )(qn, qr, kn, kr, v, do, lse, delta)


def _head(o, ub, x1, target, wg, w, lay, tb):
    t_all, d = x1.shape
    hv, qr = lay.hv, lay.qr
    nheads = hv // LANE

    def body(o_ref, ub_ref, x1_ref, tg_ref, wg_ref, gf, loss_ref, dgf_ref, yb_ref, dx2_ref, do_ref, dg_ref, dl_ref, wob, sems):
        @pl.when(pl.program_id(0) == 0)
        def _():
            cps = _fetch(wg_ref, lay, "out_b", wob, sems, 0)
            for cp in cps:
                cp.start()
            loss_ref[...] = jnp.zeros((1, LANE), F32)
            dgf_ref[...] = jnp.zeros((1, d), F32)
            for cp in cps:
                cp.wait()

        ov = o_ref[...].astype(F32)
        g = ub_ref[:, qr:]
        sg = _sigmoid(g)
        silu = g * sg
        yb = (ov * silu).astype(BF16)
        yb_ref[...] = yb
        x2 = x1_ref[...] + _dot(yb, wob[...])
        rinv = _rinv(x2)
        err = x2 * rinv * gf[...] - tg_ref[...]
        loss_ref[...] += (0.5 / d) * jnp.sum(jnp.sum(err * err, axis=1, keepdims=True), axis=0, keepdims=True)
        dx2, dgf = _rms_bwd(x2, rinv, gf[...], err * (1.0 / d))
        dgf_ref[...] += dgf
        dx2_ref[...] = dx2
        dyb = _dot_nt(dx2.astype(BF16), wob[...])
        dov = dyb * silu
        do_ref[...] = dov.astype(BF16)
        dg_ref[...] = (dyb * ov * (sg * (1.0 + g * (1.0 - sg)))).astype(BF16)
        prod = dov * ov
        dl_ref[...] = jnp.concatenate(
            [jnp.broadcast_to(jnp.sum(prod[:, n * LANE:(n + 1) * LANE], axis=1, keepdims=True), (tb, LANE)) for n in range(nheads)], axis=1)

    tok = lambda c: pl.BlockSpec((tb, c), lambda i: (i, 0))
    return pl.pallas_call(
        body, name="head", grid=(t_all // tb,),
        in_specs=[tok(hv), tok(qr + hv), tok(d), tok(d), ANY, _full((1, d))],
        out_specs=[_full((1, LANE)), _full((1, d)), tok(hv), tok(d), tok(hv), tok(hv), tok(hv)],
        out_shape=[_sds((1, LANE)), _sds((1, d)), _sds((t_all, hv), BF16), _sds((t_all, d)), _sds((t_all, hv), BF16), _sds((t_all, hv), BF16),
                   _sds((t_all, hv))],
        scratch_shapes=[pltpu.VMEM((hv, d), BF16), pltpu.SemaphoreType.DMA((N_CHIPS,))],
        compiler_params=_params(1),
    )(o, ub, x1, target, wg, w["final_norm"])


def _fb_bwd(dqn, dqr, dkn, dkr, dv, dgate, ub, ckr, x1, dx2, wg, w, lay, cos_t, sin_t, seq, tb):
    t_all, d = x1.shape
    hv, qr, kvr = lay.hv, lay.qr, lay.kvr
    nheads = hv // LANE
    npos = seq // tb

    def body(dqn_ref, dqr_ref, dkn_ref, dkr_ref, dv_ref, dg_ref, ub_ref, ckr_ref, x1_ref, dx2_ref, wg_ref,
             qn, nb, kvn, wdkv, nkv, cos_ref, sin_ref,
             dx1_ref, dqrp_ref, dqnp_ref, dub_ref, dckr_ref, dqn_g, dnb_g, dkvn_g, dnkv_g, winb, wuk, wuv, wuqn, wuqr, sems):
        @pl.when(pl.program_id(0) == 0)
        def _():
            cps = []
            for n, (key, dst) in enumerate((("in_b", winb), ("uk", wuk), ("uv", wuv), ("uq_n", wuqn), ("uq_r", wuqr))):
                cps += _fetch(wg_ref, lay, key, dst, sems, n * N_CHIPS)
            for cp in cps:
                cp.start()
            dqn_g[...] = jnp.zeros((1, qr), F32)
            dnb_g[...] = jnp.zeros((1, d), F32)
            dkvn_g[...] = jnp.zeros((1, kvr), F32)
            dnkv_g[...] = jnp.zeros((1, d), F32)
            for cp in cps:
                cp.wait()

        cos, sin = cos_ref[...], sin_ref[...]
        xv = x1_ref[...]
        rinv1 = _rinv(xv)
        dqr_v = dqr_ref[...].astype(F32) * ATTN_SCALE
        dqr_pre = (dqr_v * jnp.tile(cos, (1, nheads)) + _swap_halves(dqr_v * jnp.tile(sin, (1, nheads)))).astype(BF16)
        dqrp_ref[...] = dqr_pre
        dqn_pre = (dqn_ref[...].astype(F32) * ATTN_SCALE).astype(BF16)
        dqnp_ref[...] = dqn_pre
        dcq = _dot_nt(dqn_pre, wuqn[...]) + _dot_nt(dqr_pre, wuqr[...])
        cq_pre = ub_ref[:, :qr]
        dcq_pre, g1 = _rms_bwd(cq_pre, _rinv(cq_pre), qn[...], dcq)
        dqn_g[...] += g1
        dub = jnp.concatenate([dcq_pre.astype(BF16), dg_ref[...]], axis=1)
        dub_ref[...] = dub
        dx1_b, g2 = _rms_bwd(xv, rinv1, nb[...], _dot(dub, winb[...]))
        dnb_g[...] += g2
        dkr_all = dkr_ref[...].astype(F32)
        dkr_sum = dkr_all[:, :LANE]
        for n in range(1, nheads):
            dkr_sum = dkr_sum + dkr_all[:, n * LANE:(n + 1) * LANE]
        dckr_rope = dkr_sum * cos + _swap_halves(dkr_sum * sin)
        dckv = _dot_nt(dkn_ref[...].astype(BF16), wuk[...]) + _dot_nt(dv_ref[...].astype(BF16), wuv[...])
        ckv_pre = ckr_ref[:, :kvr]
        dckv_pre, g3 = _rms_bwd(ckv_pre, _rinv(ckv_pre), kvn[...], dckv)
        dkvn_g[...] += g3
        dckr = jnp.concatenate([dckv_pre, dckr_rope], axis=1).astype(BF16)
        dckr_ref[...] = dckr
        dx1_kv, g4 = _rms_bwd(xv, rinv1, nkv[...], _dot_nt(dckr, wdkv[...]))
        dnkv_g[...] += g4
        dx1_ref[...] = dx2_ref[...] + dx1_b + dx1_kv

    tok = lambda c: pl.BlockSpec((tb, c), lambda i: (i, 0))
    pos = pl.BlockSpec((tb, LANE), lambda i: (i % npos, 0))
    consts = [w["q_norm"], w["norm_b"], w["kv_norm"], w["w_dkv_p"], w["norm_kv"]]
    return pl.pallas_call(
        body, name="fb_bwd", grid=(t_all // tb,),
        in_specs=[tok(hv)] * 6 + [tok(qr + hv), tok(kvr + LANE), tok(d), tok(d), ANY] + [_full(c.shape) for c in consts] + [pos, pos],
        out_specs=[tok(d), tok(hv), tok(hv), tok(qr + hv), tok(kvr + LANE), _full((1, qr)), _full((1, d)), _full((1, kvr)), _full((1, d))],
        out_shape=[_sds((t_all, d)), _sds((t_all, hv), BF16), _sds((t_all, hv), BF16), _sds((t_all, qr + hv), BF16), _sds((t_all, kvr + LANE), BF16),
                   _sds((1, qr)), _sds((1, d)), _sds((1, kvr)), _sds((1, d))],
        scratch_shapes=[pltpu.VMEM((qr + hv, d), BF16), pltpu.VMEM((kvr, d), BF16), pltpu.VMEM((kvr, d), BF16), pltpu.VMEM((qr, d), BF16),
                        pltpu.VMEM((qr, d), BF16), pltpu.SemaphoreType.DMA((5 * N_CHIPS,))],
        compiler_params=_params(1),
    )(dqn, dqr, dkn, dkr, dv, dgate, ub, ckr, x1, dx2, wg, *consts, cos_t, sin_t)


def _fa_bwd(dx1, x, u, xb, hs, wg, g16, g32, w, lay, seq, tb):
    t_all, d = x.shape
    dr = lay.dr
    nblocks = w["w_rg"].shape[0]
    nblk = seq // tb
    nt = tb // SUBLANE
    per8 = tb // 8

    def body(dx1_ref, x_ref, u_ref, xb_ref, hs_ref, hh_ref, wg_ref, g16_ref, g32_ref, na, cw, wrg, brg, wig, big, lam,
             gx_ref, du_ref, dna_g, dcw_g, dcb_g, dbrg_g, dbig_g, dlam_g, dwrg_g, dwig_g, got_ref, sib_ref, own_ref,
             wint, wout, hpad, a_s, d_s, g_s, dxpad, carry, sems, send_sems, recv_sems, local_sem):
        b, jj = pl.program_id(0), pl.program_id(1)
        first_block = jj == nblk - 1
        scatter = _ScatterDirect(g16_ref, g32_ref, got_ref, sib_ref, own_ref, send_sems, recv_sems, local_sem, lay, G_GROUPS["early"])

        @pl.when((b == 0) & (jj == 0))
        def _():
            scatter.start()
            cps = _fetch(wg_ref, lay, "in_a", wint, sems, 0) + _fetch(wg_ref, lay, "out_a", wout, sems, N_CHIPS)
            for cp in cps:
                cp.start()
            dna_g[...] = jnp.zeros((1, d), F32)
            dcw_g[...] = jnp.zeros((4, dr), F32)
            dcb_g[...] = jnp.zeros((1, dr), F32)
            dbrg_g[...] = jnp.zeros((1, dr), F32)
            dbig_g[...] = jnp.zeros((1, dr), F32)
            dlam_g[...] = jnp.zeros((1, dr), F32)
            dwrg_g[...] = jnp.zeros((nblocks, LANE, LANE), F32)
            dwig_g[...] = jnp.zeros((nblocks, LANE, LANE), F32)
            for cp in cps:
                cp.wait()

        @pl.when(jj == 0)
        def _():
            dxpad[pl.ds(tb, 8), :] = jnp.zeros((8, dr), F32)
            carry[...] = jnp.zeros((8, dr), F32)

        keep = jnp.where(first_block, 0.0, 1.0)
        dx1v = dx1_ref[...]
        gate = u_ref[:, dr:]
        xpre = u_ref[:, :dr]
        hpad[pl.ds(0, 8), :] = hh_ref[...] * keep
        hpad[pl.ds(8, tb), :] = hs_ref[...]
        xb = xb_ref[...]
        xbb = xb.astype(BF16)
        r, i = _gates(xb, wrg, brg[...], wig, big[...], nblocks)
        sp = _softplus(-lam[...])
        log_a = -LRU_C * r * sp
        a, a2, nem = _decay(log_a)
        mult = jnp.sqrt(nem)
        sg = _sigmoid(gate)
        dy = _dot_nt(dx1v.astype(BF16), wout[...])
        hsv = hs_ref[...]
        dgate = dy * hsv * (sg * (1.0 + gate * (1.0 - sg)))
        a_s[...] = a
        d_s[...] = dy * (gate * sg)
        row = lax.broadcasted_iota(jnp.int32, (8, dr), 0)

        def step(k, c):
            r0 = pl.multiple_of((nt - 1 - k) * 8, 8)
            av = a_s[pl.ds(r0, 8), :]
            dv = d_s[pl.ds(r0, 8), :]
            qv = av * dv
            for s in (1, 2, 4):
                m = row < 8 - s
                a_sh = jnp.where(m, pltpu.roll(av, 8 - s, 0), 1.0)
                q_sh = jnp.where(m, pltpu.roll(qv, 8 - s, 0), 0.0)
                qv = qv + av * q_sh
                av = av * a_sh
            qv = qv + av * c
            g_s[pl.ds(r0, 8), :] = dv + jnp.where(row < 7, pltpu.roll(qv, 7, 0), c)
            return jnp.broadcast_to(qv[0:1, :], qv.shape)

        carry[...] = lax.fori_loop(0, nt, step, carry[...])
        g = g_s[...]
        ix = i * xb
        dlog_a = g * (hpad[pl.ds(7, tb), :] * a - ix * (a2 * lax.rsqrt(nem)))
        dix = g * mult
        dlam_g[...] += -jax.nn.sigmoid(-lam[...]) * jnp.sum(dlog_a * (-LRU_C * r), axis=0, keepdims=True)
        drg = dlog_a * (-LRU_C * sp) * r * (1.0 - r)
        dig = dix * xb * i * (1.0 - i)
        dbrg_g[...] += jnp.sum(drg, axis=0, keepdims=True)
        dbig_g[...] += jnp.sum(dig, axis=0, keepdims=True)
        drgb, digb = drg.astype(BF16), dig.astype(BF16)
        back = []
        for n in range(nblocks):
            cols = slice(n * LANE, (n + 1) * LANE)
            dwrg_g[n] += _dot_tn(xbb[:, cols], drgb[:, cols])
            dwig_g[n] += _dot_tn(xbb[:, cols], digb[:, cols])
            back.append(_dot_nt(drgb[:, cols], wrg[n]) + _dot_nt(digb[:, cols], wig[n]))
        dxb = dix * i + jnp.concatenate(back, axis=1)
        dcb_g[...] += jnp.sum(dxb, axis=0, keepdims=True)
        dxpad[pl.ds(0, tb), :] = dxb
        later = [dxb, dxpad[pl.ds(1, tb), :], dxpad[pl.ds(2, tb), :], dxpad[pl.ds(3, tb), :]]
        dxpad[pl.ds(tb, 8), :] = dxb[:8, :]
        dxpre = cw[3:4, :] * later[0] + cw[2:3, :] * later[1] + cw[1:2, :] * later[2] + cw[0:1, :] * later[3]
        for m in range(4):
            dcw_g[3 - m:4 - m, :] += jnp.sum(later[m] * xpre, axis=0, keepdims=True)
        du = jnp.concatenate([dxpre, dgate], axis=1).astype(BF16)
        du_ref[...] = du
        xv = x_ref[...]
        dxa, g1 = _rms_bwd(xv, _rinv(xv), na[...], _dot(du, wint[...]))
        dna_g[...] += g1
        gx_ref[...] = dx1v + dxa

        @pl.when((b == t_all // seq - 1) & (jj == nblk - 1))
        def _():
            scatter.finish()

    blk = lambda b, j: b * nblk + (nblk - 1 - j)
    tok = lambda c: pl.BlockSpec((tb, c), lambda b, j: (blk(b, j), 0))
    halo = pl.BlockSpec((8, dr), lambda b, j: (jnp.maximum(blk(b, j) * per8 - 1, 0), 0))
    consts = [w["norm_a"], w["conv_w"], w["w_rg"], w["b_rg"], w["w_ig"], w["b_ig"], w["lru_lambda"]]
    vec = lambda c: _full((1, c))
    blocks3 = (nblocks, LANE, LANE)
    return pl.pallas_call(
        body, name="fa_bwd", grid=(t_all // seq, nblk),
        in_specs=[tok(d), tok(d), tok(2 * dr), tok(dr), tok(dr), halo, ANY, ANY, ANY] + [_full(c.shape) for c in consts],
        out_specs=[tok(d), tok(2 * dr), vec(d), _full((4, dr)), vec(dr), vec(dr), vec(dr), vec(dr), _full(blocks3), _full(blocks3), ANY, ANY, ANY],
        out_shape=[_sds((t_all, d)), _sds((t_all, 2 * dr), BF16), _sds((1, d)), _sds((4, dr)), _sds((1, dr)), _sds((1, dr)), _sds((1, dr)),
                   _sds((1, dr)), _sds(blocks3), _sds(blocks3)] + _scatter_direct_shapes(lay, "early"),
        scratch_shapes=[pltpu.VMEM((2 * dr, d), BF16), pltpu.VMEM((dr, d), BF16), pltpu.VMEM((tb + 8, dr), F32),
                        pltpu.VMEM((tb, dr), F32), pltpu.VMEM((tb, dr), F32), pltpu.VMEM((tb, dr), F32), pltpu.VMEM((tb + 8, dr), F32),
                        pltpu.VMEM((8, dr), F32), pltpu.SemaphoreType.DMA((2 * N_CHIPS,))] + SCATTER_DIRECT_SEMS,
        compiler_params=_params(2),
    )(dx1, x, u, xb, hs, hs, wg, g16, g32, *consts)


def _mm_into(gbufs, a, bs, offs, name, bt):
    t_all, m = a.shape
    n = bs[0].shape[1]
    nb = len(bs)
    nsplit = nb if nb > 1 else 2 if m >= 1024 and (m // 2) % LANE == 0 else 1
    mh = m if nb > 1 else m // nsplit
    starts = list(offs) if nb > 1 else [offs[0] + h * mh for h in range(nsplit)]
    nt = t_all // bt
    nbuf = len(gbufs)
    twin = nbuf == 2

    def body(a_ref, *refs):
        b_refs, outs, acc, sems = refs[:nb], refs[nb + nbuf:nb + 2 * nbuf], refs[nb + 2 * nbuf], refs[-1]
        acc16 = refs[nb + 2 * nbuf + 1] if twin else None
        part, t = pl.program_id(0), pl.program_id(1)

        def out_copies(h):
            dst = pl.ds(starts[h], mh)
            copies = [pltpu.make_async_copy(acc.at[h], outs[0].at[dst, :], sems.at[0, h])]
            if twin:
                copies.append(pltpu.make_async_copy(acc16.at[h], outs[1].at[dst, :], sems.at[1, h]))
            return copies

        for h in range(nsplit):
            @pl.when(part == h)
            def _():
                prod = _dot_tn(a_ref[...].astype(BF16), b_refs[h if nb > 1 else 0][...].astype(BF16))

                @pl.when(t == 0)
                def _():
                    acc[h] = prod

                @pl.when(t > 0)
                def _():
                    acc[h] += prod

                @pl.when(t == nt - 1)
                def _():
                    if twin:
                        acc16[h] = acc[h].astype(BF16)
                    for cp in out_copies(h):
                        cp.start()

        @pl.when((part == nsplit - 1) & (t == nt - 1))
        def _():
            for h in range(nsplit):
                for cp in out_copies(h):
                    cp.wait()

    if nb > 1:
        a_spec = pl.BlockSpec((bt, mh), lambda h, t: (t, 0))
        b_specs = [pl.BlockSpec((bt, n), lambda h, t, k=k: (jnp.where(h == k, t, 0), 0)) for k in range(nb)]
    else:
        a_spec = pl.BlockSpec((bt, mh), lambda h, t: (t, h))
        b_specs = [pl.BlockSpec((bt, n), lambda h, t: (t, 0))]
    scratch = [pltpu.VMEM((nsplit, mh, n), F32)] + ([pltpu.VMEM((nsplit, mh, n), BF16)] if twin else []) + [pltpu.SemaphoreType.DMA((2, nsplit))]
    return pl.pallas_call(
        body, name=name, grid=(nsplit, nt),
        in_specs=[a_spec] + b_specs + [ANY] * nbuf,
        out_specs=[ANY] * nbuf, out_shape=[_sds(g.shape, g.dtype) for g in gbufs], input_output_aliases={1 + nb + k: k for k in range(nbuf)},
        scratch_shapes=scratch, compiler_params=_params(2),
    )(a, *bs, *gbufs)


def _dw_in_a_exchange(du, h, rest, lay, bt):
    t_all, d = h.shape
    half = d // 2
    rows, rest_rows = lay.rows["in_a"], lay.rows["rest"]
    c_in, c_rest = lay.c_off["in_a"], lay.c_off["rest"]
    nt = t_all // bt
    xi, yi, _ = _place()
    order = jnp.stack([2 * (1 - xi) + (1 - yi), 2 * (1 - xi) + yi, 2 * xi + (1 - yi), 2 * xi + yi]).astype(jnp.int32)

    def body(order_ref, a_ref, b_ref, rest_ref, own_ref, got_ref, acc, sibbuf, part16, restv, rest_sib, rest_p, rest16, own_v, own_r,
             d2d_send, d2d_recv, ici_send, ici_recv, local_sems):
        x, y, c = _place()
        chips = [(1 - x, 1 - y), (1 - x, y), (x, 1 - y)]
        g, t = pl.program_id(0), pl.program_id(1)
        their_cols = pl.ds(pl.multiple_of((1 - c) * half, LANE), half)

        def my_half(v):
            return jnp.where(c == 0, v[:, :half], v[:, half:])

        def d2d(src, dst, k):
            return pltpu.make_async_remote_copy(src_ref=src, dst_ref=dst, send_sem=d2d_send.at[k], recv_sem=d2d_recv.at[k],
                                                device_id=(x, y, 1 - c), device_id_type=MESH)

        def group_swap(gg):
            return d2d(acc.at[gg % 2, :, their_cols], sibbuf.at[gg], gg)

        def rest_swap():
            return d2d(restv.at[:, their_cols], rest_sib, 4)

        def to_chip(k, src, off, nrows):
            px, py = chips[k]
            return pltpu.make_async_remote_copy(src_ref=src, dst_ref=got_ref.at[k, pl.ds(off, nrows), :], send_sem=ici_send.at[k],
                                                recv_sem=ici_recv.at[k], device_id=(px, py, c), device_id_type=MESH)

        def own_copy(src, off, nrows, k):
            return pltpu.make_async_copy(src, own_ref.at[pl.ds(off, nrows), :], local_sems.at[k])

        def finish_group(gg):
            group_swap(gg).wait()
            part = my_half(acc[gg % 2]) + sibbuf[gg]
            if gg < 3:
                part16[gg] = part.astype(BF16)
                to_chip(gg, part16.at[gg], c_in, rows).start()
            else:
                own_v[...] = part

        @pl.when((g == 0) & (t == 0))
        def _():
            load = pltpu.make_async_copy(rest_ref, restv, local_sems.at[0])
            load.start()
            load.wait()
            rest_swap().start()

        @pl.when(t == 0)
        def _():
            acc[g % 2] = _dot_tn(a_ref[...], b_ref[...])

        @pl.when(t > 0)
        def _():
            acc[g % 2] += _dot_tn(a_ref[...], b_ref[...])

        for gg in range(4):
            @pl.when((g == gg) & (t == 0))
            def _():
                if gg == 0:
                    rest_swap().wait()
                    rest_p[...] = my_half(restv[...]) + rest_sib[...]
                    for k in range(3):
                        chip_rows = pl.ds(pl.multiple_of(order_ref[k] * rest_rows, SUBLANE), rest_rows)
                        rest16[k] = rest_p[chip_rows, :].astype(BF16)
                        to_chip(k, rest16.at[k], c_rest, rest_rows).start()
                    own_r[...] = rest_p[pl.ds(pl.multiple_of(order_ref[3] * rest_rows, SUBLANE), rest_rows), :]
                else:
                    finish_group(gg - 1)

            @pl.when((g == gg) & (t == nt - 1))
            def _():
                group_swap(gg).start()
                if gg == 3:
                    finish_group(3)
                    for k, (px, py) in enumerate(chips):
                        pltpu.make_async_remote_copy(src_ref=got_ref.at[k], dst_ref=got_ref.at[k], send_sem=ici_send.at[k], recv_sem=ici_recv.at[k],
                                                     device_id=(px, py, c), device_id_type=MESH).wait()
                    for buf, off, nrows, k in ((own_v, c_in, rows, 1), (own_r, c_rest, rest_rows, 2)):
                        there = pl.ds(off, nrows)
                        buf[...] = ((buf[...] + got_ref[0, there, :].astype(F32)) + got_ref[1, there, :].astype(F32)) + got_ref[2, there, :].astype(F32)
                        own_copy(buf, off, nrows, k).start()
                    own_copy(own_v, c_in, rows, 1).wait()
                    own_copy(own_r, c_rest, rest_rows, 2).wait()

    return pl.pallas_call(
        body, name="dw_in_a",
        grid_spec=pltpu.PrefetchScalarGridSpec(
            num_scalar_prefetch=1, grid=(N_CHIPS, nt),
            in_specs=[pl.BlockSpec((bt, rows), lambda g, t, order: (t, order[g])), pl.BlockSpec((bt, d), lambda g, t, order: (t, 0)), ANY],
            out_specs=ANY,
            scratch_shapes=[pltpu.VMEM((3, lay.c_rows["late"], half), BF16), pltpu.VMEM((2, rows, d), F32), pltpu.VMEM((N_CHIPS, rows, half), F32), pltpu.VMEM((3, rows, half), BF16),
                            pltpu.VMEM((N_CHIPS * rest_rows, d), F32), pltpu.VMEM((N_CHIPS * rest_rows, half), F32),
                            pltpu.VMEM((N_CHIPS * rest_rows, half), F32), pltpu.VMEM((3, rest_rows, half), BF16),
                            pltpu.VMEM((rows, half), F32), pltpu.VMEM((rest_rows, half), F32),
                            pltpu.SemaphoreType.DMA((5,)), pltpu.SemaphoreType.DMA((5,)), pltpu.SemaphoreType.DMA((3,)), pltpu.SemaphoreType.DMA((3,)),
                            pltpu.SemaphoreType.DMA((3,))]),
        out_shape=_sds((lay.c_rows["late"], half)), compiler_params=_params(2),
    )(order, du, h, rest)


def _mm_tn(a, b, name, bt):
    t_all, m = a.shape
    n = b.shape[1]

    def body(a_ref, b_ref, o_ref):
        @pl.when(pl.program_id(0) == 0)
        def _():
            o_ref[...] = jnp.zeros((m, n), F32)

        o_ref[...] += _dot_tn(a_ref[...].astype(BF16), b_ref[...].astype(BF16))

    return pl.pallas_call(
        body, name=name, grid=(t_all // bt,),
        in_specs=[pl.BlockSpec((bt, m), lambda t: (t, 0)), pl.BlockSpec((bt, n), lambda t: (t, 0))],
        out_specs=_full((m, n)), out_shape=_sds((m, n)),
        compiler_params=_params(1),
    )(a, b)


class _Gather8:
    def __init__(self, x_ref, out_ref, send_sems, recv_sems, local_sem):
        x, y, c = _place()
        self.c, self.me, self.sibling = c, (x, y, c), (x, y, 1 - c)
        self.chips = [(1 - x, y), (x, 1 - y), (1 - x, 1 - y)]
        self.x_ref, self.out_ref, self.send_sems, self.recv_sems, self.local_sem = x_ref, out_ref, send_sems, recv_sems, local_sem

    def _slot(self, px, py, pc):
        return self.out_ref.at[4 * px + 2 * py + pc]

    def _copy(self, k, blk, to, src=None):
        return pltpu.make_async_remote_copy(
            src_ref=self._slot(*blk) if src is None else src, dst_ref=self._slot(*blk), send_sem=self.send_sems.at[k],
            recv_sem=self.recv_sems.at[k], device_id=to, device_id_type=MESH)

    def _mine(self):
        return pltpu.make_async_copy(self.x_ref, self._slot(*self.me), self.local_sem)

    def _first(self):
        return [self._copy(0, self.me, self.sibling, src=self.x_ref)] + [
            self._copy(1 + j, self.me, (*chip, self.c), src=self.x_ref) for j, chip in enumerate(self.chips)]

    def _passed(self):
        return [self._copy(4 + j, (*chip, self.c), self.sibling) for j, chip in enumerate(self.chips)]

    def start(self):
        self._mine().start()
        for cp in self._first():
            cp.start()

    def forward(self):
        passed = self._passed()
        for j, chip in enumerate(self.chips):
            self._copy(1 + j, (*chip, self.c), self.me).wait_recv()
            passed[j].start()

    def finish(self):
        self._copy(0, self.sibling, self.me).wait_recv()
        for j, chip in enumerate(self.chips):
            self._copy(4 + j, (*chip, 1 - self.c), self.me).wait_recv()
        for cp in self._first() + self._passed():
            cp.wait_send()
        self._mine().wait()


class _ScatterDirect:
    def __init__(self, g16_ref, g32_ref, got_ref, sib_ref, own_ref, send_sems, recv_sems, local_sem, lay, order):
        self.x, self.y, self.c = _place()
        self.chips = [(1 - self.x, self.y), (self.x, 1 - self.y), (1 - self.x, 1 - self.y)]
        self.refs = (g16_ref, g32_ref, got_ref, sib_ref, own_ref, send_sems, recv_sems, local_sem)
        self.lay, self.order, self.half = lay, order, lay.d // 2

    def _src(self, ref, key, chip, h):
        start = pl.multiple_of(self.lay.g_off[key] + chip * self.lay.rows[key], ROW_ALIGN)
        return ref.at[pl.ds(start, self.lay.rows[key]), pl.ds(pl.multiple_of(h * self.half, LANE), self.half)]

    def _compact(self, ref, key):
        return ref.at[pl.ds(self.lay.c_off[key], self.lay.rows[key]), :]

    def start(self):
        g16_ref, g32_ref, got_ref, sib_ref, own_ref, send_sems, recv_sems, local_sem = self.refs
        x, y, c = self.x, self.y, self.c
        for key in self.order:
            pltpu.make_async_copy(self._src(g32_ref, key, 2 * x + y, c), self._compact(own_ref, key), local_sem).start()
            pltpu.make_async_remote_copy(
                src_ref=self._src(g32_ref, key, 2 * x + y, 1 - c), dst_ref=self._compact(sib_ref, key), send_sem=send_sems.at[6],
                recv_sem=recv_sems.at[6], device_id=(x, y, 1 - c), device_id_type=MESH).start()
        for k, (px, py) in enumerate(self.chips):
            for h in range(2):
                for key in self.order:
                    pltpu.make_async_remote_copy(
                        src_ref=self._src(g16_ref, key, 2 * px + py, h), dst_ref=self._compact(got_ref.at[2 * k + c], key),
                        send_sem=send_sems.at[2 * k + h], recv_sem=recv_sems.at[2 * k + c], device_id=(px, py, h), device_id_type=MESH).start()

    def finish(self):
        _, _, got_ref, sib_ref, own_ref, send_sems, recv_sems, local_sem = self.refs
        x, y, c = self.x, self.y, self.c
        for k, (px, py) in enumerate(self.chips):
            for h in range(2):
                whole = pltpu.make_async_remote_copy(src_ref=got_ref.at[2 * k + h], dst_ref=got_ref.at[2 * k + h], send_sem=send_sems.at[2 * k + h],
                                                     recv_sem=recv_sems.at[2 * k + h], device_id=(px, py, h), device_id_type=MESH)
                whole.wait_send()
                whole.wait_recv()
        pltpu.make_async_remote_copy(src_ref=sib_ref, dst_ref=sib_ref, send_sem=send_sems.at[6], recv_sem=recv_sems.at[6],
                                     device_id=(x, y, 1 - c), device_id_type=MESH).wait()
        pltpu.make_async_copy(own_ref, own_ref, local_sem).wait()


def _scatter_direct_shapes(lay, group):
    rows, half = lay.c_rows[group], lay.d // 2
    return [_sds((6, rows, half), BF16), _sds((rows, half), F32), _sds((rows, half), F32)]


SCATTER_DIRECT_SEMS = [pltpu.SemaphoreType.DMA((7,)), pltpu.SemaphoreType.DMA((7,)), pltpu.SemaphoreType.DMA]
GATHER_SEMS = [pltpu.SemaphoreType.DMA((7,)), pltpu.SemaphoreType.DMA((7,)), pltpu.SemaphoreType.DMA]


def _all_gather8(blocks, name):
    nb = len(blocks)

    def body(*refs):
        x_refs, out_refs = refs[:nb], refs[nb:2 * nb]
        send_sems, recv_sems, local_sems = refs[2 * nb:]
        gathers = [_Gather8(x_refs[n], out_refs[n], send_sems.at[n], recv_sems.at[n], local_sems.at[n]) for n in range(nb)]
        for g in gathers:
            g.start()
        for g in gathers:
            g.forward()
        for g in gathers:
            g.finish()

    return pl.pallas_call(
        body, name=name, out_shape=[_sds((8,) + b.shape, b.dtype) for b in blocks], in_specs=[ANY] * nb, out_specs=[ANY] * nb,
        scratch_shapes=[pltpu.SemaphoreType.DMA((nb, 7)), pltpu.SemaphoreType.DMA((nb, 7)), pltpu.SemaphoreType.DMA((nb,))],
    )(*blocks)


def _return_and_gather(mines, rep_block):
    n = len(mines)

    def body(*refs):
        src_refs, rep_ref, out_refs, rep_out = refs[:n], refs[n], refs[n + 1:2 * n + 1], refs[2 * n + 1]
        send_sems, recv_sems, g_send, g_recv, g_local = refs[2 * n + 2:]
        x, y, c = _place()
        copies = [pltpu.make_async_remote_copy(src_ref=src_refs[k], dst_ref=out_refs[k], send_sem=send_sems.at[k], recv_sem=recv_sems.at[k],
                                               device_id=(x, y, 1 - c), device_id_type=MESH) for k in range(n)]
        gather = _Gather8(rep_ref, rep_out, g_send, g_recv, g_local)
        for cp in copies:
            cp.start()
        gather.start()
        gather.forward()
        gather.finish()
        for cp in copies:
            cp.wait()

    return pl.pallas_call(
        body, name="rs_return", out_shape=[_sds(m.shape, m.dtype) for m in mines] + [_sds((8,) + rep_block.shape, rep_block.dtype)],
        in_specs=[ANY] * (n + 1), out_specs=[ANY] * (n + 1),
        scratch_shapes=[pltpu.SemaphoreType.DMA((n,)), pltpu.SemaphoreType.DMA((n,))] + GATHER_SEMS,
    )(*mines, rep_block)


def _sum_devices(own, sib, got, name):
    rows, half = own.shape
    rb = _row_block(rows)
    n = got.shape[0]

    def body(a_ref, s_ref, b_ref, o_ref):
        acc = a_ref[...] + s_ref[...]
        for k in range(n):
            acc = acc + b_ref[k].astype(F32)
        o_ref[...] = acc

    spec = pl.BlockSpec((rb, half), lambda i: (i, 0))
    return pl.pallas_call(
        body, name=name, grid=(rows // rb,), in_specs=[spec, spec, pl.BlockSpec((n, rb, half), lambda i: (0, i, 0))], out_specs=spec,
        out_shape=_sds((rows, half)), compiler_params=_params(1),
    )(own, sib, got)


def _adamw(w, g, m, v):
    m = ADAM_B1 * m + (1.0 - ADAM_B1) * g
    v = ADAM_B2 * v + (1.0 - ADAM_B2) * (g * g)
    m_hat = m / (1.0 - ADAM_B1 ** ADAM_STEP)
    v_hat = v / (1.0 - ADAM_B2 ** ADAM_STEP)
    return -ADAM_LR * (m_hat / (jnp.sqrt(v_hat) + ADAM_EPS) + ADAM_WD * w), m, v


def _adamw_rows(name, w, g, m, v):
    _, rows, cols = w.shape
    rb = _row_block(rows, 256)

    def body(w_ref, g_ref, m_ref, v_ref, d_ref, mo_ref, vo_ref):
        d_ref[...], mo_ref[...], vo_ref[...] = _adamw(w_ref[...], g_ref[...], m_ref[...], v_ref[...])

    spec = pl.BlockSpec((1, rb, cols), lambda i: (0, i, 0))
    return pl.pallas_call(
        body, name=name, grid=(rows // rb,), in_specs=[spec] * 4, out_specs=[spec] * 3, out_shape=[_sds(w.shape)] * 3,
        compiler_params=_params(1),
    )(w, g, m, v)


def _adamw_group(ws, gs, ms, vs):
    n = len(ws)

    def body(*refs):
        for k in range(n):
            w_ref, g_ref, m_ref, v_ref = (refs[j * n + k] for j in range(4))
            outs = _adamw(w_ref[...], g_ref[...], m_ref[...], v_ref[...])
            for j in range(3):
                refs[(4 + j) * n + k][...] = outs[j]

    outs = pl.pallas_call(
        body, name="adamw_small", out_shape=[_sds(w.shape) for w in ws] * 3,
        compiler_params=pltpu.CompilerParams(vmem_limit_bytes=VMEM_LIMIT),
    )(*ws, *gs, *ms, *vs)
    return outs[:n], outs[n:2 * n], outs[2 * n:]


def _gather_weights(sh, lay):
    c = lax.axis_index("c")
    d = lay.d
    uq = sh["w_uq"][0].astype(BF16)
    parts = {
        "in_b": sh["w_in_b"][0].T.astype(BF16), "in_a": sh["w_in_a"][0].T.astype(BF16), "out_a": sh["w_out_a"][0].astype(BF16),
        "out_b": sh["w_out_b"][0].astype(BF16), "uk": sh["w_uk"].astype(BF16).reshape(-1, d), "uv": sh["w_uv"].astype(BF16).reshape(-1, d),
        "uq_n": uq[:, :, :QK_NOPE].reshape(-1, d), "uq_r": jnp.pad(uq[:, :, QK_NOPE:], ((0, 0), (0, 0), (0, LANE - QK_ROPE))).reshape(-1, d),
        "dkv": jnp.pad(sh["w_dkv"].astype(BF16), ((0, 0), (0, LANE - QK_ROPE))).reshape(-1, d),
    }
    halves = {}
    for group, order in W_GROUPS.items():
        stack = jnp.concatenate([parts[k] for k in order], axis=0).reshape(2, lay.w_rows[group] // 2, d)
        halves[group] = lax.dynamic_index_in_dim(stack, c, 0, keepdims=False)
    small = jnp.concatenate([sh[k].reshape(-1) for k in SMALL])
    n_small = small.shape[0]
    width = _round_up(n_small, 2 * SUBLANE * LANE) // (2 * SUBLANE)
    small = jnp.pad(small, (0, 2 * SUBLANE * width - n_small)).reshape(2, SUBLANE, width)
    wg, sg = _all_gather8([halves["a"], lax.dynamic_index_in_dim(small, c, 0, keepdims=False)], "ag_weights")
    wg = wg.reshape(N_CHIPS, lay.w_rows["a"], d)
    sg = sg.reshape(N_CHIPS, 2 * SUBLANE * width)
    full, off = {}, 0
    for k in SMALL:
        n = sh[k].size
        piece = sg[:, off:off + n]
        off += n
        if k == "conv_w":
            full[k] = piece.reshape(N_CHIPS, 4, n // 4).transpose(1, 0, 2).reshape(4, n)
        else:
            full[k] = piece.reshape(1, N_CHIPS * n)
    return wg, halves["b"], full


def _chip_split(g, taps=False):
    if taps:
        n = g.shape[1] // N_CHIPS
        return g.reshape(4, N_CHIPS, n).transpose(1, 0, 2).reshape(N_CHIPS, 4 * n)
    return g.reshape(N_CHIPS, -1)


def kernel(x, norm_a, w_in_a, conv_w, conv_b, w_rg, b_rg, w_ig, b_ig, lru_lambda, w_out_a, norm_kv, w_dkv, kv_norm, w_uk, w_uv, norm_b, w_in_b, q_norm, w_uq, w_out_b, final_norm, loss_target, m_norm_a, m_w_in_a, m_conv_w, m_conv_b, m_w_rg, m_b_rg, m_w_ig, m_b_ig, m_lru_lambda, m_w_out_a, m_norm_kv, m_w_dkv, m_kv_norm, m_w_uk, m_w_uv, m_norm_b, m_w_in_b, m_q_norm, m_w_uq, m_w_out_b, m_final_norm, v_norm_a, v_w_in_a, v_conv_w, v_conv_b, v_w_rg, v_b_rg, v_w_ig, v_b_ig, v_lru_lambda, v_w_out_a, v_norm_kv, v_w_dkv, v_kv_norm, v_w_uk, v_w_uv, v_norm_b, v_w_in_b, v_q_norm, v_w_uq, v_w_out_b, v_final_norm):
    given = dict(locals())
    sh = {k: given[k] for k in WEIGHTS}
    ci = lax.axis_index("c")
    nb, seq, d = x.shape
    t_all = nb * seq
    tb_a, tb_b, ta, bt = min(TOKENS_A, seq), min(TOKENS_B, seq), min(TOKENS_ATTN, seq), min(TOKENS_MM, t_all)
    dr = conv_b.shape[1] * N_CHIPS
    qr, kvr, nheads = q_norm.shape[1], kv_norm.shape[0], w_uk.shape[1]
    hv = nheads * LANE
    n_small = sum(sh[k].size for k in SMALL)
    n_repl = sum(sh[k].size for k in REPL)
    lay = _Layout(d, dr, qr, kvr, hv, n_small, n_repl)
    half = d // 2

    wga, wb_half, w = _gather_weights(sh, lay)
    w.update({"w_rg": w_rg[0].astype(BF16), "w_ig": w_ig[0].astype(BF16), "norm_kv": norm_kv[None, :],
              "kv_norm": kv_norm[None, :], "final_norm": final_norm[None, :], "norm_b": norm_b, "q_norm": q_norm})
    cos_t, sin_t = _rope_tables(seq)

    x0 = x.reshape(t_all, d)
    x1, u, hs, h, y, xb, wgb = _fa_fwd(x0, wga, wb_half, w, lay, seq, min(TOKENS_A_FWD, seq))
    wgb = wgb.reshape(N_CHIPS, lay.w_rows["b"], d)
    w["w_dkv_p"] = wgb[:, lay.w_off["dkv"]:lay.w_off["dkv"] + lay.rows["dkv"], :].reshape(d, kvr + LANE)
    qn, qrp, kn, kr, v, ub, ckr, hb, hk, cq, ckv = _fb_fwd(x1, wgb, w, lay, cos_t, sin_t, seq, tb_b)
    o, lse = _attn_fwd(qn, qrp, kn, kr, v, seq, ta)
    loss, g_final_norm, yb, dx2, do, dgate, delta = _head(o, ub, x1, loss_target.reshape(t_all, d), wgb, w, lay, tb_b)
    dqn, dqr, dkn, dkr, dv = _attn_bwd(qn, qrp, kn, kr, v, do, lse, delta, seq, ta)
    dx1, dqr_pre, dqn_pre, dub, dckr, g_q_norm, g_norm_b, g_kv_norm, g_norm_kv = _fb_bwd(
        dqn, dqr, dkn, dkr, dv, dgate, ub, ckr, x1, dx2, wgb, w, lay, cos_t, sin_t, seq, tb_b)
    loss = lax.psum(loss[0, 0], ("x", "y", "c"))

    gbufs = [lax.empty((lay.g_rows["early"], d), F32), lax.empty((lay.g_rows["early"], d), BF16)]
    for keys, a, bs in ((("in_b",), dub, (hb,)), (("out_a",), y, (dx1,)), (("out_b",), yb, (dx2,)), (("uk", "uv"), ckv, (dkn, dv)),
                        (("uq_n", "uq_r"), cq, (dqn_pre, dqr_pre))):
        gbufs = _mm_into(gbufs, a, bs, [lay.g_off[k] for k in keys], "dw_" + "_".join(keys), bt)
    g_dkv = _mm_tn(hk, dckr, "dw_dkv", bt)
    gx, du, g_norm_a, g_conv_w, g_conv_b, g_b_rg, g_b_ig, g_lam, g_w_rg, g_w_ig, others, sib, own = _fa_bwd(
        dx1, x0, u, xb, hs, wga, gbufs[1], gbufs[0], w, lay, seq, tb_a)
    mine_early = _sum_devices(own, sib, others, "rs_sum_early")

    small = jnp.concatenate([_chip_split(g_norm_a), _chip_split(g_conv_w, taps=True), _chip_split(g_conv_b), _chip_split(g_b_rg),
                             _chip_split(g_b_ig), _chip_split(g_lam)], axis=1)
    small = jnp.pad(small, ((0, 0), (0, lay.small_rows * d - small.shape[1]))).reshape(N_CHIPS, lay.small_rows, d)
    repl_parts = {"w_rg": g_w_rg, "w_ig": g_w_ig, "norm_kv": g_norm_kv, "kv_norm": g_kv_norm, "norm_b": g_norm_b, "q_norm": g_q_norm,
                  "final_norm": g_final_norm}
    repl = jnp.concatenate([repl_parts[k].reshape(-1) for k in REPL])
    repl = jnp.pad(repl, (0, N_CHIPS * lay.repl_rows * d - n_repl)).reshape(N_CHIPS, lay.repl_rows, d)
    pad_rows = lay.rows["rest"] - lay.rows["dkv"] - lay.small_rows - lay.repl_rows
    rest = jnp.concatenate([g_dkv.reshape(N_CHIPS, lay.rows["dkv"], d), small, repl, jnp.zeros((N_CHIPS, pad_rows, d), F32)], axis=1)
    mine_late = _dw_in_a_exchange(du, h, rest.reshape(N_CHIPS * lay.rows["rest"], d), lay, bt)

    r0 = lay.c_off["rest"] + lay.rows["dkv"] + lay.small_rows
    theirs_early, theirs_late, rep_all = _return_and_gather([mine_early, mine_late], mine_late[r0:r0 + lay.repl_rows])
    red = {}
    for group, mine, theirs in (("early", mine_early, theirs_early), ("late", mine_late, theirs_late)):
        red[group] = jnp.concatenate([jnp.where(ci == 0, mine, theirs), jnp.where(ci == 0, theirs, mine)], axis=1)
    rep_flat =rep_all.reshape(N_CHIPS, 2, lay.repl_rows, half).transpose(0, 2, 1, 3).reshape(-1)

    def rows(key):
        group = "late" if key in G_GROUPS["late"] else "early"
        return red[group][lay.c_off[key]:lay.c_off[key] + lay.rows[key]]

    grads = {"w_in_b": rows("in_b").T[None], "w_in_a": rows("in_a").T[None], "w_out_a": rows("out_a")[None], "w_out_b": rows("out_b")[None],
             "w_uk": rows("uk").reshape(w_uk.shape), "w_uv": rows("uv").reshape(w_uv.shape)}
    uq_n = rows("uq_n").reshape(qr // N_CHIPS, nheads, LANE)
    uq_r = rows("uq_r").reshape(qr // N_CHIPS, nheads, LANE)[:, :, :QK_ROPE]
    grads["w_uq"] = jnp.concatenate([uq_n, uq_r], axis=2)[None]
    rest_red = rows("rest")
    grads["w_dkv"] = rest_red[:lay.rows["dkv"]].reshape(d // N_CHIPS, kvr + LANE)[:, :kvr + QK_ROPE]
    small_red = rest_red[lay.rows["dkv"]:lay.rows["dkv"] + lay.small_rows].reshape(-1)
    off = 0
    for k in SMALL:
        n = sh[k].size
        grads[k] = small_red[off:off + n].reshape(sh[k].shape)
        off += n
    off = 0
    for k in REPL:
        n = sh[k].size
        grads[k] = rep_flat[off:off + n].reshape(sh[k].shape)
        off += n

    new = {}
    for k in ("w_in_a", "w_in_b", "w_out_a", "w_out_b"):
        view = (lambda a: jnp.swapaxes(a, 1, 2)) if k in TRANSPOSED else (lambda a: a)
        outs = _adamw_rows("adamw_" + k, view(sh[k]), view(grads[k]), view(given["m_" + k]), view(given["v_" + k]))
        new[k] = tuple(view(a) for a in outs)
    rest_names = [k for k in WEIGHTS if k not in new]

    def as2d(k, a):
        return a.T if k in TRANSPOSED else a[None, :] if a.ndim == 1 else a

    ds, ms, vs = _adamw_group([as2d(k, sh[k]) for k in rest_names], [as2d(k, grads[k]) for k in rest_names],
                              [as2d(k, given["m_" + k]) for k in rest_names], [as2d(k, given["v_" + k]) for k in rest_names])
    for n, k in enumerate(rest_names):
        new[k] = tuple((a.T if k in TRANSPOSED else a).reshape(sh[k].shape) for a in (ds[n], ms[n], vs[n]))
    return (loss, gx.reshape(nb, seq, d), *[grads[k] for k in WEIGHTS], *[new[k][0] for k in WEIGHTS], *[new[k][1] for k in WEIGHTS],
            *[new[k][2] for k in WEIGHTS])
```

```python
import jax
import jax.numpy as jnp
from jax import lax
from jax.experimental import pallas as pl
from jax.experimental.pallas import tpu as pltpu

F32, BF16 = jnp.float32, jnp.bfloat16
EPS = 1e-6
LRU_C = 8.0
ROPE_THETA = 10000.0
QK_NOPE, QK_ROPE = 128, 64
ATTN_SCALE = (QK_NOPE + QK_ROPE) ** -0.5
LN2 = 0.6931471805599453
Q_SCALE = ATTN_SCALE / LN2
ATTN_HEADS, ATTN_HEADS_BWD = 4, 2
ATTN_ROWS = 64
LANE = 128
SUBLANE = 8
ROW_ALIGN = 32
VMEM_LIMIT = 60000 * 1024
ADAM_LR, ADAM_B1, ADAM_B2, ADAM_EPS, ADAM_WD, ADAM_STEP = 0.001, 0.9, 0.999, 1e-08, 0.01, 10
MESH = pl.DeviceIdType.MESH
ANY = pl.BlockSpec(memory_space=pl.ANY)
N_CHIPS = 4
TOKENS_A, TOKENS_B, TOKENS_ATTN, TOKENS_MM = 256, 512, 512, 2048
TOKENS_A_FWD = 512

SMALL = ("norm_a", "conv_w", "conv_b", "b_rg", "b_ig", "lru_lambda")
REPL = ("w_rg", "w_ig", "norm_kv", "kv_norm", "norm_b", "q_norm", "final_norm")
TRANSPOSED = ("w_in_b", "w_dkv")
WEIGHTS = ("norm_a", "w_in_a", "conv_w", "conv_b", "w_rg", "b_rg", "w_ig", "b_ig", "lru_lambda", "w_out_a", "norm_kv",
           "w_dkv", "kv_norm", "w_uk", "w_uv", "norm_b", "w_in_b", "q_norm", "w_uq", "w_out_b", "final_norm")
W_GROUPS = {"a": ("in_a", "out_a"), "b": ("in_b", "out_b", "uk", "uv", "uq_n", "uq_r", "dkv")}
G_GROUPS = {"early": ("in_b", "out_a", "out_b", "uk", "uv", "uq_n", "uq_r"), "late": ("in_a", "rest")}


def _sds(shape, dtype=F32):
    return jax.ShapeDtypeStruct(tuple(shape), dtype)


def _params(n_grid):
    return pltpu.CompilerParams(dimension_semantics=("arbitrary",) * n_grid, vmem_limit_bytes=VMEM_LIMIT)


def _full(shape):
    nd = len(shape)
    return pl.BlockSpec(tuple(shape), lambda *g: (0,) * nd)


def _round_up(n, k):
    return -(-n // k) * k


def _row_block(rows, cap=512):
    best = SUBLANE
    for r in range(SUBLANE, min(rows, cap) + 1, SUBLANE):
        if rows % r == 0:
            best = r
    return best


def _place():
    return lax.axis_index("x"), lax.axis_index("y"), lax.axis_index("c")


class _Layout:
    def __init__(self, d, dr, qr, kvr, hv, n_small, n_repl):
        assert hv == d, "the packed rows are D_MODEL wide, which must equal heads * 128"
        self.d, self.dr, self.qr, self.kvr, self.hv = d, dr, qr, kvr, hv
        per_chip = {"in_b": (qr + hv) // N_CHIPS, "in_a": 2 * dr // N_CHIPS, "out_a": dr // N_CHIPS, "out_b": hv // N_CHIPS,
                    "uk": kvr // N_CHIPS, "uv": kvr // N_CHIPS, "uq_n": qr // N_CHIPS, "uq_r": qr // N_CHIPS,
                    "dkv": (d // N_CHIPS) * (kvr + LANE) // d}
        assert all(r % ROW_ALIGN == 0 for r in per_chip.values()), per_chip
        self.small_rows = _round_up(-(-n_small // d), SUBLANE)
        self.repl_rows = _round_up(-(-n_repl // (N_CHIPS * d)), SUBLANE)
        per_chip["rest"] = _round_up(per_chip["dkv"] + self.small_rows + self.repl_rows, ROW_ALIGN)
        self.rows = per_chip
        self.w_off, self.w_rows = {}, {}
        for group, order in W_GROUPS.items():
            off = 0
            for k in order:
                self.w_off[k] = off
                off += per_chip[k]
            assert off % ROW_ALIGN == 0, (group, off)
            self.w_rows[group] = off
        self.g_off, self.c_off, self.c_rows, self.g_rows = {}, {}, {}, {}
        for group, order in G_GROUPS.items():
            off = 0
            for k in order:
                self.c_off[k] = off
                self.g_off[k] = N_CHIPS * off
                off += per_chip[k]
            self.c_rows[group] = off
            self.g_rows[group] = N_CHIPS * off


def _dot(a, b):
    return jnp.dot(a, b, preferred_element_type=F32)


def _dot_nt(a, b):
    return lax.dot_general(a, b, (((1,), (1,)), ((), ())), preferred_element_type=F32)


def _dot_tn(a, b):
    return lax.dot_general(a, b, (((0,), (0,)), ((), ())), preferred_element_type=F32)


def _rinv(x):
    return lax.rsqrt(jnp.mean(x * x, axis=-1, keepdims=True) + EPS)


def _rms_bwd(x, rinv, g, dy):
    z = dy * g
    dx = rinv * z - x * (rinv * rinv * rinv) * jnp.mean(z * x, axis=-1, keepdims=True)
    dg = jnp.sum(dy * (x * rinv), axis=0, keepdims=True)
    return dx, dg


def _softplus(z):
    return jnp.maximum(z, 0.0) + jnp.log1p(jnp.exp(-jnp.abs(z)))


def _sigmoid(x):
    return 0.5 * jnp.tanh(0.5 * x) + 0.5


def _decay(log_a):
    a = jnp.exp(log_a)
    a2 = a * a
    return a, a2, -jnp.tanh(log_a) * (a2 + 1.0)


def _swap_halves(x):
    w = x.shape[1]
    lane = lax.broadcasted_iota(jnp.int32, x.shape, 1)
    return jnp.where(lane % QK_ROPE < QK_ROPE // 2, pltpu.roll(x, w - QK_ROPE // 2, 1), pltpu.roll(x, QK_ROPE // 2, 1))


def _rope_tables(seq):
    pos = jnp.arange(seq, dtype=F32)
    inv = ROPE_THETA ** (-jnp.arange(0, QK_ROPE, 2, dtype=F32) / QK_ROPE)
    ang = pos[:, None] * inv[None, :]
    cos, sin = jnp.cos(ang), jnp.sin(ang)
    zero = jnp.zeros((seq, LANE - QK_ROPE), F32)
    return jnp.concatenate([cos, cos, zero], 1), jnp.concatenate([-sin, sin, zero], 1)


def _fetch(wg_ref, lay, key, dst, sems, k0):
    rows = lay.rows[key]
    return [pltpu.make_async_copy(wg_ref.at[p, pl.ds(lay.w_off[key], rows), :], dst.at[pl.ds(p * rows, rows), :], sems.at[k0 + p])
            for p in range(N_CHIPS)]


def _gates(xb, wrg_ref, brg, wig_ref, big, nblocks):
    xbb = xb.astype(BF16)
    rg = [_dot(xbb[:, n * LANE:(n + 1) * LANE], wrg_ref[n]) for n in range(nblocks)]
    ig = [_dot(xbb[:, n * LANE:(n + 1) * LANE], wig_ref[n]) for n in range(nblocks)]
    r = _sigmoid(jnp.concatenate(rg, axis=1) + brg)
    i = _sigmoid(jnp.concatenate(ig, axis=1) + big)
    return r, i


def _conv(xpad, cw_ref, cb, tb):
    return (cb + cw_ref[3:4, :] * xpad[pl.ds(8, tb), :] + cw_ref[2:3, :] * xpad[pl.ds(7, tb), :]
            + cw_ref[1:2, :] * xpad[pl.ds(6, tb), :] + cw_ref[0:1, :] * xpad[pl.ds(5, tb), :])


def _fa_fwd(x, wg, wb_half, w, lay, seq, tb):
    t_all, d = x.shape
    dr = lay.dr
    nblocks = w["w_rg"].shape[0]
    nblk = seq // tb
    nt = tb // SUBLANE
    nsteps = (t_all // seq) * nblk

    def body(x_ref, wg_ref, wbh_ref, na, cw, cb, wrg, brg, wig, big, lam, x1_ref, u_ref, hs_ref, h_ref, y_ref, xb_ref, wb_ref,
             wint, wout, xpad, a_s, b_s, carry, sems, send_sems, recv_sems, local_sem):
        step_no = pl.program_id(0) * nblk + pl.program_id(1)
        gather = _Gather8(wbh_ref, wb_ref, send_sems, recv_sems, local_sem)

        @pl.when(step_no == 0)
        def _():
            gather.start()
            cps = _fetch(wg_ref, lay, "in_a", wint, sems, 0) + _fetch(wg_ref, lay, "out_a", wout, sems, N_CHIPS)
            for cp in cps:
                cp.start()
            for cp in cps:
                cp.wait()

        @pl.when(step_no == nsteps // 2)
        def _():
            gather.forward()

        @pl.when(pl.program_id(1) == 0)
        def _():
            xpad[pl.ds(0, 8), :] = jnp.zeros((8, dr), F32)
            carry[...] = jnp.zeros((8, dr), F32)

        xv = x_ref[...]
        h = (xv * _rinv(xv) * na[...]).astype(BF16)
        h_ref[...] = h
        u = _dot_nt(h, wint[...])
        u_ref[...] = u
        xpre, gate = u[:, :dr], u[:, dr:]
        xpad[pl.ds(8, tb), :] = xpre
        xb = _conv(xpad, cw, cb[...], tb)
        xb_ref[...] = xb
        xpad[pl.ds(0, 8), :] = xpre[tb - 8:, :]
        r, i = _gates(xb, wrg, brg[...], wig, big[...], nblocks)
        log_a = -LRU_C * r * _softplus(-lam[...])
        a, _, nem = _decay(log_a)
        a_s[...] = a
        b_s[...] = jnp.sqrt(nem) * (i * xb)
        row = lax.broadcasted_iota(jnp.int32, (8, dr), 0)

        def step(t, c):
            r0 = pl.multiple_of(t * 8, 8)
            a = a_s[pl.ds(r0, 8), :]
            b = b_s[pl.ds(r0, 8), :]
            for s in (1, 2, 4):
                m = row >= s
                a_sh = jnp.where(m, pltpu.roll(a, s, 0), 1.0)
                b_sh = jnp.where(m, pltpu.roll(b, s, 0), 0.0)
                b = a * b_sh + b
                a = a * a_sh
            hh = b + a * c
            hs_ref[pl.ds(r0, 8), :] = hh
            return jnp.broadcast_to(hh[7:8, :], hh.shape)

        carry[...] = lax.fori_loop(0, nt, step, carry[...])
        y = (hs_ref[...] * (gate * _sigmoid(gate))).astype(BF16)
        y_ref[...] = y
        x1_ref[...] = xv + _dot(y, wout[...])

        @pl.when(step_no == nsteps - 1)
        def _():
            gather.finish()

    tok = lambda c: pl.BlockSpec((tb, c), lambda b, j: (b * nblk + j, 0))
    consts = [w["norm_a"], w["conv_w"], w["conv_b"], w["w_rg"], w["b_rg"], w["w_ig"], w["b_ig"], w["lru_lambda"]]
    return pl.pallas_call(
        body, name="fa_fwd", grid=(t_all // seq, nblk),
        in_specs=[tok(d), ANY, ANY] + [_full(c.shape) for c in consts],
        out_specs=[tok(d), tok(2 * dr), tok(dr), tok(d), tok(dr), tok(dr), ANY],
        out_shape=[_sds((t_all, d)), _sds((t_all, 2 * dr)), _sds((t_all, dr)), _sds((t_all, d), BF16), _sds((t_all, dr), BF16),
                   _sds((t_all, dr)), _sds((8,) + wb_half.shape, BF16)],
        scratch_shapes=[pltpu.VMEM((2 * dr, d), BF16), pltpu.VMEM((dr, d), BF16), pltpu.VMEM((tb + 8, dr), F32), pltpu.VMEM((tb, dr), F32),
                        pltpu.VMEM((tb, dr), F32), pltpu.VMEM((8, dr), F32), pltpu.SemaphoreType.DMA((2 * N_CHIPS,))] + GATHER_SEMS,
        compiler_params=_params(2),
    )(x, wg, wb_half, *consts)


def _fb_fwd(x1, wg, w, lay, cos_t, sin_t, seq, tb):
    t_all, d = x1.shape
    kvr, qr, hv = lay.kvr, lay.qr, lay.hv
    nheads = hv // LANE
    npos = seq // tb

    def body(x_ref, wg_ref, nkv, nb, wdkv, kvn, qn, cos_ref, sin_ref,
             qn_o, qr_o, kn_o, kr_o, v_o, ub_o, ckr_o, hb_o, hk_o, cq_o, ckv_o, winb, wuk, wuv, wuqn, wuqr, sems):
        @pl.when(pl.program_id(0) == 0)
        def _():
            cps = []
            for n, (key, dst) in enumerate((("in_b", winb), ("uk", wuk), ("uv", wuv), ("uq_n", wuqn), ("uq_r", wuqr))):
                cps += _fetch(wg_ref, lay, key, dst, sems, n * N_CHIPS)
            for cp in cps:
                cp.start()
            for cp in cps:
                cp.wait()

        xv = x_ref[...]
        xh = xv * _rinv(xv)
        hk = (xh * nkv[...]).astype(BF16)
        hb = (xh * nb[...]).astype(BF16)
        hk_o[...] = hk
        hb_o[...] = hb
        cos, sin = cos_ref[...], sin_ref[...]
        ckr = _dot(hk, wdkv[...])
        ckr_o[...] = ckr
        ckv_pre = ckr[:, :kvr]
        ckv = (ckv_pre * _rinv(ckv_pre) * kvn[...]).astype(BF16)
        ckv_o[...] = ckv
        kr = ckr[:, kvr:]
        kr_o[...] = (kr * cos + _swap_halves(kr) * sin).astype(BF16)
        kn_o[...] = _dot(ckv, wuk[...]).astype(BF16)
        v_o[...] = _dot(ckv, wuv[...]).astype(BF16)
        ub = _dot_nt(hb, winb[...])
        ub_o[...] = ub
        cq_pre = ub[:, :qr]
        cq = (cq_pre * _rinv(cq_pre) * qn[...]).astype(BF16)
        cq_o[...] = cq
        qn_o[...] = (_dot(cq, wuqn[...]) * Q_SCALE).astype(BF16)
        qrope = _dot(cq, wuqr[...]) * Q_SCALE
        qr_o[...] = (qrope * jnp.tile(cos, (1, nheads)) + _swap_halves(qrope) * jnp.tile(sin, (1, nheads))).astype(BF16)

    tok = lambda c: pl.BlockSpec((tb, c), lambda i: (i, 0))
    pos = pl.BlockSpec((tb, LANE), lambda i: (i % npos, 0))
    consts = [w["norm_kv"], w["norm_b"], w["w_dkv_p"], w["kv_norm"], w["q_norm"]]
    outs = [(hv, BF16), (hv, BF16), (hv, BF16), (LANE, BF16), (hv, BF16), (qr + hv, F32), (kvr + LANE, F32), (d, BF16), (d, BF16), (qr, BF16), (kvr, BF16)]
    return pl.pallas_call(
        body, name="fb_fwd", grid=(t_all // tb,),
        in_specs=[tok(d), ANY] + [_full(c.shape) for c in consts] + [pos, pos],
        out_specs=[tok(c) for c, _ in outs],
        out_shape=[_sds((t_all, c), dt) for c, dt in outs],
        scratch_shapes=[pltpu.VMEM((qr + hv, d), BF16), pltpu.VMEM((kvr, d), BF16), pltpu.VMEM((kvr, d), BF16), pltpu.VMEM((qr, d), BF16),
                        pltpu.VMEM((qr, d), BF16), pltpu.SemaphoreType.DMA((5 * N_CHIPS,))],
        compiler_params=_params(1),
    )(x1, wg, *consts, cos_t, sin_t)


def _causal_mask(row0, col0, nrows, ncols):
    rows = row0 + lax.broadcasted_iota(jnp.int32, (nrows, ncols), 0)
    cols = col0 + lax.broadcasted_iota(jnp.int32, (nrows, ncols), 1)
    return cols <= rows


def _attn_fwd(qn, qr, kn, kr, v, seq, ta):
    t_all, hv = qn.shape
    nheads, nb, na = hv // LANE, t_all // seq, seq // ta

    reps = ta // LANE
    hp = ATTN_HEADS
    wide = hp * LANE

    def body(qn_ref, qr_ref, kn_ref, kr_ref, v_ref, o_ref, lse_ref, m_s, l_s, acc_s):
        i = pl.program_id(2)
        m_s[...] = jnp.full((ta, wide), -1e30, F32)
        l_s[...] = jnp.zeros((ta, wide), F32)
        acc_s[...] = jnp.zeros((ta, wide), F32)
        heads = [slice(n * LANE, (n + 1) * LANE) for n in range(hp)]
        qs = [jnp.concatenate([qn_ref[:, hd], qr_ref[:, hd]], axis=1) for hd in heads]

        def tile(j, diagonal):
            cols = pl.ds(pl.multiple_of(j * ta, ta), ta)
            k_rope = kr_ref[cols, :]
            for q, hd in zip(qs, heads):
                k = jnp.concatenate([kn_ref[cols, hd], k_rope], axis=1)
                s = _dot_nt(q, k)
                if diagonal:
                    s = jnp.where(_causal_mask(0, 0, ta, ta), s, -1e30)
                m_prev = m_s[:, hd]
                m_new = jnp.maximum(m_prev, jnp.max(s, axis=1, keepdims=True))
                p = jnp.exp2(s - jnp.tile(m_new, (1, reps)))
                alpha = jnp.exp2(m_prev - m_new)
                l_s[:, hd] = alpha * l_s[:, hd] + jnp.sum(p, axis=1, keepdims=True)
                acc_s[:, hd] = alpha * acc_s[:, hd] + _dot(p.astype(BF16), v_ref[cols, hd])
                m_s[:, hd] = m_new

        def off_diagonal(j, carry):
            tile(j, False)
            return carry

        lax.fori_loop(0, i, off_diagonal, 0)
        tile(i, True)
        o_ref[...] = (acc_s[...] / l_s[...]).astype(BF16)
        lse_ref[...] = m_s[...] + jnp.log2(l_s[...])

    qspec = pl.BlockSpec((ta, wide), lambda b, h, i: (b * na + i, h))
    kspec = pl.BlockSpec((seq, wide), lambda b, h, i: (b, h))
    krspec = pl.BlockSpec((seq, LANE), lambda b, h, i: (b, 0))
    return pl.pallas_call(
        body, name="attn_fwd", grid=(nb, nheads // hp, na),
        in_specs=[qspec, qspec, kspec, krspec, kspec],
        out_specs=[qspec, qspec],
        out_shape=[_sds((t_all, hv), BF16), _sds((t_all, hv))],
        scratch_shapes=[pltpu.VMEM((ta, wide), F32)] * 3,
        compiler_params=_params(3),
    )(qn, qr, kn, kr, v)


def _attn_bwd(qn, qr, kn, kr, v, do, lse, delta, seq, ta):
    t_all, hv = qn.shape
    nheads, nb, na = hv // LANE, t_all // seq, seq // ta

    reps = ta // LANE
    nchunks = ta // ATTN_ROWS

    hp = ATTN_HEADS_BWD
    wide = hp * LANE
    heads = [slice(n * LANE, (n + 1) * LANE) for n in range(hp)]

    def body(qn_ref, qr_ref, kn_ref, kr_ref, v_ref, do_ref, lse_ref, dl_ref, dqn_out, dqr_out, dkn_ref, dkr_ref, dv_ref,
             s_s, dp_s, p_s, ds_s, dk_s, dv_s, dqn_ref, dqr_ref):
        j = pl.program_id(2)

        @pl.when(j == 0)
        def _():
            dqn_ref[...] = jnp.zeros((seq, wide), F32)
            dqr_ref[...] = jnp.zeros((seq, wide), F32)

        dk_s[...] = jnp.zeros((hp, ta, 2 * LANE), F32)
        dv_s[...] = jnp.zeros((hp, ta, LANE), F32)
        k_rope = kr_ref[...]
        ks = [jnp.concatenate([kn_ref[:, hd], k_rope], axis=1) for hd in heads]

        def tile(i, diagonal):
            rows_i = pl.ds(pl.multiple_of(i * ta, ta), ta)
            for n, hd in enumerate(heads):
                q = jnp.concatenate([qn_ref[rows_i, hd], qr_ref[rows_i, hd]], axis=1)
                do_b = do_ref[rows_i, hd]
                s_s[n] = _dot_nt(q, ks[n])
                dp_s[n] = _dot_nt(do_b, v_ref[:, hd])
                for c in range(nchunks):
                    rows = pl.ds(c * ATTN_ROWS, ATTN_ROWS)
                    seq_rows = pl.ds(pl.multiple_of(i * ta + c * ATTN_ROWS, ATTN_ROWS), ATTN_ROWS)
                    s = s_s[n, rows, :]
                    if diagonal:
                        s = jnp.where(_causal_mask(c * ATTN_ROWS, 0, ATTN_ROWS, ta), s, -1e30)
                    p = jnp.exp2(s - jnp.tile(lse_ref[seq_rows, hd], (1, reps)))
                    p_s[n, rows, :] = p.astype(BF16)
                    ds_s[n, rows, :] = (p * (dp_s[n, rows, :] - jnp.tile(dl_ref[seq_rows, hd], (1, reps)))).astype(BF16)
                dv_s[n] += _dot_tn(p_s[n], do_b)
                ds = ds_s[n]
                dk_s[n] += _dot_tn(ds, q)
                dq = _dot(ds, ks[n])
                dqn_ref[rows_i, hd] += dq[:, :LANE]
                dqr_ref[rows_i, hd] += dq[:, LANE:]

        def off_diagonal(i, carry):
            tile(i, False)
            return carry

        tile(j, True)
        lax.fori_loop(j + 1, na, off_diagonal, 0)
        for n, hd in enumerate(heads):
            dkn_ref[:, hd] = (dk_s[n, :, :LANE] * LN2).astype(BF16)
            dkr_ref[:, hd] = (dk_s[n, :, LANE:] * LN2).astype(BF16)
            dv_ref[:, hd] = dv_s[n].astype(BF16)

        @pl.when(j == na - 1)
        def _():
            dqn_out[...] = dqn_ref[...].astype(BF16)
            dqr_out[...] = dqr_ref[...].astype(BF16)

    qspec = pl.BlockSpec((seq, wide), lambda b, h, j: (b, h))
    kspec = pl.BlockSpec((ta, wide), lambda b, h, j: (b * na + j, h))
    krspec = pl.BlockSpec((ta, LANE), lambda b, h, j: (b * na + j, 0))
    return pl.pallas_call(
        body, name="attn_bwd", grid=(nb, nheads // hp, na),
        in_specs=[qspec, qspec, kspec, krspec, kspec, qspec, qspec, qspec],
        out_specs=[qspec, qspec, kspec, kspec, kspec],
        out_shape=[_sds((t_all, hv), BF16)] * 5,
        scratch_shapes=[pltpu.VMEM((hp, ta, ta), F32), pltpu.VMEM((hp, ta, ta), F32), pltpu.VMEM((hp, ta, ta), BF16), pltpu.VMEM((hp, ta, ta), BF16),
                        pltpu.VMEM((hp, ta, 2 * LANE), F32), pltpu.VMEM((hp, ta, LANE), F32), pltpu.VMEM((seq, wide), F32), pltpu.VMEM((seq, wide), F32)],
        compiler_params=_params(3),
    )(qn, qr, kn, kr, v, do, lse, delta)


def _head(o, ub, x1, target, wg, w, lay, tb):
    t_all, d = x1.shape
    hv, qr = lay.hv, lay.qr
    nheads = hv // LANE

    def body(o_ref, ub_ref, x1_ref, tg_ref, wg_ref, gf, loss_ref, dgf_ref, yb_ref, dx2_ref, do_ref, dg_ref, dl_ref, wob, sems):
        @pl.when(pl.program_id(0) == 0)
        def _():
            cps = _fetch(wg_ref, lay, "out_b", wob, sems, 0)
            for cp in cps:
                cp.start()
            loss_ref[...] = jnp.zeros((1, LANE), F32)
            dgf_ref[...] = jnp.zeros((1, d), F32)
            for cp in cps:
                cp.wait()

        ov = o_ref[...].astype(F32)
        g = ub_ref[:, qr:]
        sg = _sigmoid(g)
        silu = g * sg
        yb = (ov * silu).astype(BF16)
        yb_ref[...] = yb
        x2 = x1_ref[...] + _dot(yb, wob[...])
        rinv = _rinv(x2)
        err = x2 * rinv * gf[...] - tg_ref[...]
        loss_ref[...] += (0.5 / d) * jnp.sum(jnp.sum(err * err, axis=1, keepdims=True), axis=0, keepdims=True)
        dx2, dgf = _rms_bwd(x2, rinv, gf[...], err * (1.0 / d))
        dgf_ref[...] += dgf
        dx2_ref[...] = dx2
        dyb = _dot_nt(dx2.astype(BF16), wob[...])
        dov = dyb * silu
        do_ref[...] = dov.astype(BF16)
        dg_ref[...] = (dyb * ov * (sg * (1.0 + g * (1.0 - sg)))).astype(BF16)
        prod = dov * ov
        dl_ref[...] = jnp.concatenate(
            [jnp.broadcast_to(jnp.sum(prod[:, n * LANE:(n + 1) * LANE], axis=1, keepdims=True), (tb, LANE)) for n in range(nheads)], axis=1)

    tok = lambda c: pl.BlockSpec((tb, c), lambda i: (i, 0))
    return pl.pallas_call(
        body, name="head", grid=(t_all // tb,),
        in_specs=[tok(hv), tok(qr + hv), tok(d), tok(d), ANY, _full((1, d))],
        out_specs=[_full((1, LANE)), _full((1, d)), tok(hv), tok(d), tok(hv), tok(hv), tok(hv)],
        out_shape=[_sds((1, LANE)), _sds((1, d)), _sds((t_all, hv), BF16), _sds((t_all, d)), _sds((t_all, hv), BF16), _sds((t_all, hv), BF16),
                   _sds((t_all, hv))],
        scratch_shapes=[pltpu.VMEM((hv, d), BF16), pltpu.SemaphoreType.DMA((N_CHIPS,))],
        compiler_params=_params(1),
    )(o, ub, x1, target, wg, w["final_norm"])


def _fb_bwd(dqn, dqr, dkn, dkr, dv, dgate, ub, ckr, x1, dx2, wg, w, lay, cos_t, sin_t, seq, tb):
    t_all, d = x1.shape
    hv, qr, kvr = lay.hv, lay.qr, lay.kvr
    nheads = hv // LANE
    npos = seq // tb

    def body(dqn_ref, dqr_ref, dkn_ref, dkr_ref, dv_ref, dg_ref, ub_ref, ckr_ref, x1_ref, dx2_ref, wg_ref,
             qn, nb, kvn, wdkv, nkv, cos_ref, sin_ref,
             dx1_ref, dqrp_ref, dqnp_ref, dub_ref, dckr_ref, dqn_g, dnb_g, dkvn_g, dnkv_g, winb, wuk, wuv, wuqn, wuqr, sems):
        @pl.when(pl.program_id(0) == 0)
        def _():
            cps = []
            for n, (key, dst) in enumerate((("in_b", winb), ("uk", wuk), ("uv", wuv), ("uq_n", wuqn), ("uq_r", wuqr))):
                cps += _fetch(wg_ref, lay, key, dst, sems, n * N_CHIPS)
            for cp in cps:
                cp.start()
            dqn_g[...] = jnp.zeros((1, qr), F32)
            dnb_g[...] = jnp.zeros((1, d), F32)
            dkvn_g[...] = jnp.zeros((1, kvr), F32)
            dnkv_g[...] = jnp.zeros((1, d), F32)
            for cp in cps:
                cp.wait()

        cos, sin = cos_ref[...], sin_ref[...]
        xv = x1_ref[...]
        rinv1 = _rinv(xv)
        dqr_v = dqr_ref[...].astype(F32) * ATTN_SCALE
        dqr_pre = (dqr_v * jnp.tile(cos, (1, nheads)) + _swap_halves(dqr_v * jnp.tile(sin, (1, nheads)))).astype(BF16)
        dqrp_ref[...] = dqr_pre
        dqn_pre = (dqn_ref[...].astype(F32) * ATTN_SCALE).astype(BF16)
        dqnp_ref[...] = dqn_pre
        dcq = _dot_nt(dqn_pre, wuqn[...]) + _dot_nt(dqr_pre, wuqr[...])
        cq_pre = ub_ref[:, :qr]
        dcq_pre, g1 = _rms_bwd(cq_pre, _rinv(cq_pre), qn[...], dcq)
        dqn_g[...] += g1
        dub = jnp.concatenate([dcq_pre.astype(BF16), dg_ref[...]], axis=1)
        dub_ref[...] = dub
        dx1_b, g2 = _rms_bwd(xv, rinv1, nb[...], _dot(dub, winb[...]))
        dnb_g[...] += g2
        dkr_all = dkr_ref[...].astype(F32)
        dkr_sum = dkr_all[:, :LANE]
        for n in range(1, nheads):
            dkr_sum = dkr_sum + dkr_all[:, n * LANE:(n + 1) * LANE]
        dckr_rope = dkr_sum * cos + _swap_halves(dkr_sum * sin)
        dckv = _dot_nt(dkn_ref[...].astype(BF16), wuk[...]) + _dot_nt(dv_ref[...].astype(BF16), wuv[...])
        ckv_pre = ckr_ref[:, :kvr]
        dckv_pre, g3 = _rms_bwd(ckv_pre, _rinv(ckv_pre), kvn[...], dckv)
        dkvn_g[...] += g3
        dckr = jnp.concatenate([dckv_pre, dckr_rope], axis=1).astype(BF16)
        dckr_ref[...] = dckr
        dx1_kv, g4 = _rms_bwd(xv, rinv1, nkv[...], _dot_nt(dckr, wdkv[...]))
        dnkv_g[...] += g4
        dx1_ref[...] = dx2_ref[...] + dx1_b + dx1_kv

    tok = lambda c: pl.BlockSpec((tb, c), lambda i: (i, 0))
    pos = pl.BlockSpec((tb, LANE), lambda i: (i % npos, 0))
    consts = [w["q_norm"], w["norm_b"], w["kv_norm"], w["w_dkv_p"], w["norm_kv"]]
    return pl.pallas_call(
        body, name="fb_bwd", grid=(t_all // tb,),
        in_specs=[tok(hv)] * 6 + [tok(qr + hv), tok(kvr + LANE), tok(d), tok(d), ANY] + [_full(c.shape) for c in consts] + [pos, pos],
        out_specs=[tok(d), tok(hv), tok(hv), tok(qr + hv), tok(kvr + LANE), _full((1, qr)), _full((1, d)), _full((1, kvr)), _full((1, d))],
        out_shape=[_sds((t_all, d)), _sds((t_all, hv), BF16), _sds((t_all, hv), BF16), _sds((t_all, qr + hv), BF16), _sds((t_all, kvr + LANE), BF16),
                   _sds((1, qr)), _sds((1, d)), _sds((1, kvr)), _sds((1, d))],
        scratch_shapes=[pltpu.VMEM((qr + hv, d), BF16), pltpu.VMEM((kvr, d), BF16), pltpu.VMEM((kvr, d), BF16), pltpu.VMEM((qr, d), BF16),
                        pltpu.VMEM((qr, d), BF16), pltpu.SemaphoreType.DMA((5 * N_CHIPS,))],
        compiler_params=_params(1),
    )(dqn, dqr, dkn, dkr, dv, dgate, ub, ckr, x1, dx2, wg, *consts, cos_t, sin_t)


def _fa_bwd(dx1, x, u, xb, hs, wg, g16, g32, w, lay, seq, tb):
    t_all, d = x.shape
    dr = lay.dr
    nblocks = w["w_rg"].shape[0]
    nblk = seq // tb
    nt = tb // SUBLANE
    per8 = tb // 8

    def body(dx1_ref, x_ref, u_ref, xb_ref, hs_ref, hh_ref, wg_ref, g16_ref, g32_ref, na, cw, wrg, brg, wig, big, lam,
             gx_ref, du_ref, dna_g, dcw_g, dcb_g, dbrg_g, dbig_g, dlam_g, dwrg_g, dwig_g, got_ref, sib_ref, own_ref,
             wint, wout, hpad, a_s, d_s, g_s, dxpad, carry, sems, send_sems, recv_sems, local_sem):
        b, jj = pl.program_id(0), pl.program_id(1)
        first_block = jj == nblk - 1
        scatter = _ScatterDirect(g16_ref, g32_ref, got_ref, sib_ref, own_ref, send_sems, recv_sems, local_sem, lay, G_GROUPS["early"])

        @pl.when((b == 0) & (jj == 0))
        def _():
            scatter.start()
            cps = _fetch(wg_ref, lay, "in_a", wint, sems, 0) + _fetch(wg_ref, lay, "out_a", wout, sems, N_CHIPS)
            for cp in cps:
                cp.start()
            dna_g[...] = jnp.zeros((1, d), F32)
            dcw_g[...] = jnp.zeros((4, dr), F32)
            dcb_g[...] = jnp.zeros((1, dr), F32)
            dbrg_g[...] = jnp.zeros((1, dr), F32)
            dbig_g[...] = jnp.zeros((1, dr), F32)
            dlam_g[...] = jnp.zeros((1, dr), F32)
            dwrg_g[...] = jnp.zeros((nblocks, LANE, LANE), F32)
            dwig_g[...] = jnp.zeros((nblocks, LANE, LANE), F32)
            for cp in cps:
                cp.wait()

        @pl.when(jj == 0)
        def _():
            dxpad[pl.ds(tb, 8), :] = jnp.zeros((8, dr), F32)
            carry[...] = jnp.zeros((8, dr), F32)

        keep = jnp.where(first_block, 0.0, 1.0)
        dx1v = dx1_ref[...]
        gate = u_ref[:, dr:]
        xpre = u_ref[:, :dr]
        hpad[pl.ds(0, 8), :] = hh_ref[...] * keep
        hpad[pl.ds(8, tb), :] = hs_ref[...]
        xb = xb_ref[...]
        xbb = xb.astype(BF16)
        r, i = _gates(xb, wrg, brg[...], wig, big[...], nblocks)
        sp = _softplus(-lam[...])
        log_a = -LRU_C * r * sp
        a, a2, nem = _decay(log_a)
        mult = jnp.sqrt(nem)
        sg = _sigmoid(gate)
        dy = _dot_nt(dx1v.astype(BF16), wout[...])
        hsv = hs_ref[...]
        dgate = dy * hsv * (sg * (1.0 + gate * (1.0 - sg)))
        a_s[...] = a
        d_s[...] = dy * (gate * sg)
        row = lax.broadcasted_iota(jnp.int32, (8, dr), 0)

        def step(k, c):
            r0 = pl.multiple_of((nt - 1 - k) * 8, 8)
            av = a_s[pl.ds(r0, 8), :]
            dv = d_s[pl.ds(r0, 8), :]
            qv = av * dv
            for s in (1, 2, 4):
                m = row < 8 - s
                a_sh = jnp.where(m, pltpu.roll(av, 8 - s, 0), 1.0)
                q_sh = jnp.where(m, pltpu.roll(qv, 8 - s, 0), 0.0)
                qv = qv + av * q_sh
                av = av * a_sh
            qv = qv + av * c
            g_s[pl.ds(r0, 8), :] = dv + jnp.where(row < 7, pltpu.roll(qv, 7, 0), c)
            return jnp.broadcast_to(qv[0:1, :], qv.shape)

        carry[...] = lax.fori_loop(0, nt, step, carry[...])
        g = g_s[...]
        ix = i * xb
        dlog_a = g * (hpad[pl.ds(7, tb), :] * a - ix * (a2 * lax.rsqrt(nem)))
        dix = g * mult
        dlam_g[...] += -jax.nn.sigmoid(-lam[...]) * jnp.sum(dlog_a * (-LRU_C * r), axis=0, keepdims=True)
        drg = dlog_a * (-LRU_C * sp) * r * (1.0 - r)
        dig = dix * xb * i * (1.0 - i)
        dbrg_g[...] += jnp.sum(drg, axis=0, keepdims=True)
        dbig_g[...] += jnp.sum(dig, axis=0, keepdims=True)
        drgb, digb = drg.astype(BF16), dig.astype(BF16)
        back = []
        for n in range(nblocks):
            cols = slice(n * LANE, (n + 1) * LANE)
            dwrg_g[n] += _dot_tn(xbb[:, cols], drgb[:, cols])
            dwig_g[n] += _dot_tn(xbb[:, cols], digb[:, cols])
            back.append(_dot_nt(drgb[:, cols], wrg[n]) + _dot_nt(digb[:, cols], wig[n]))
        dxb = dix * i + jnp.concatenate(back, axis=1)
        dcb_g[...] += jnp.sum(dxb, axis=0, keepdims=True)
        dxpad[pl.ds(0, tb), :] = dxb
        later = [dxb, dxpad[pl.ds(1, tb), :], dxpad[pl.ds(2, tb), :], dxpad[pl.ds(3, tb), :]]
        dxpad[pl.ds(tb, 8), :] = dxb[:8, :]
        dxpre = cw[3:4, :] * later[0] + cw[2:3, :] * later[1] + cw[1:2, :] * later[2] + cw[0:1, :] * later[3]
        for m in range(4):
            dcw_g[3 - m:4 - m, :] += jnp.sum(later[m] * xpre, axis=0, keepdims=True)
        du = jnp.concatenate([dxpre, dgate], axis=1).astype(BF16)
        du_ref[...] = du
        xv = x_ref[...]
        dxa, g1 = _rms_bwd(xv, _rinv(xv), na[...], _dot(du, wint[...]))
        dna_g[...] += g1
        gx_ref[...] = dx1v + dxa

        @pl.when((b == t_all // seq - 1) & (jj == nblk - 1))
        def _():
            scatter.finish()

    blk = lambda b, j: b * nblk + (nblk - 1 - j)
    tok = lambda c: pl.BlockSpec((tb, c), lambda b, j: (blk(b, j), 0))
    halo = pl.BlockSpec((8, dr), lambda b, j: (jnp.maximum(blk(b, j) * per8 - 1, 0), 0))
    consts = [w["norm_a"], w["conv_w"], w["w_rg"], w["b_rg"], w["w_ig"], w["b_ig"], w["lru_lambda"]]
    vec = lambda c: _full((1, c))
    blocks3 = (nblocks, LANE, LANE)
    return pl.pallas_call(
        body, name="fa_bwd", grid=(t_all // seq, nblk),
        in_specs=[tok(d), tok(d), tok(2 * dr), tok(dr), tok(dr), halo, ANY, ANY, ANY] + [_full(c.shape) for c in consts],
        out_specs=[tok(d), tok(2 * dr), vec(d), _full((4, dr)), vec(dr), vec(dr), vec(dr), vec(dr), _full(blocks3), _full(blocks3), ANY, ANY, ANY],
        out_shape=[_sds((t_all, d)), _sds((t_all, 2 * dr), BF16), _sds((1, d)), _sds((4, dr)), _sds((1, dr)), _sds((1, dr)), _sds((1, dr)),
                   _sds((1, dr)), _sds(blocks3), _sds(blocks3)] + _scatter_direct_shapes(lay, "early"),
        scratch_shapes=[pltpu.VMEM((2 * dr, d), BF16), pltpu.VMEM((dr, d), BF16), pltpu.VMEM((tb + 8, dr), F32),
                        pltpu.VMEM((tb, dr), F32), pltpu.VMEM((tb, dr), F32), pltpu.VMEM((tb, dr), F32), pltpu.VMEM((tb + 8, dr), F32),
                        pltpu.VMEM((8, dr), F32), pltpu.SemaphoreType.DMA((2 * N_CHIPS,))] + SCATTER_DIRECT_SEMS,
        compiler_params=_params(2),
    )(dx1, x, u, xb, hs, hs, wg, g16, g32, *consts)


def _mm_into(gbufs, a, bs, offs, name, bt):
    t_all, m = a.shape
    n = bs[0].shape[1]
    nb = len(bs)
    nsplit = nb if nb > 1 else 2 if m >= 1024 and (m // 2) % LANE == 0 else 1
    mh = m if nb > 1 else m // nsplit
    starts = list(offs) if nb > 1 else [offs[0] + h * mh for h in range(nsplit)]
    nt = t_all // bt
    nbuf = len(gbufs)
    twin = nbuf == 2

    def body(a_ref, *refs):
        b_refs, outs, acc, sems = refs[:nb], refs[nb + nbuf:nb + 2 * nbuf], refs[nb + 2 * nbuf], refs[-1]
        acc16 = refs[nb + 2 * nbuf + 1] if twin else None
        part, t = pl.program_id(0), pl.program_id(1)

        def out_copies(h):
            dst = pl.ds(starts[h], mh)
            copies = [pltpu.make_async_copy(acc.at[h], outs[0].at[dst, :], sems.at[0, h])]
            if twin:
                copies.append(pltpu.make_async_copy(acc16.at[h], outs[1].at[dst, :], sems.at[1, h]))
            return copies

        for h in range(nsplit):
            @pl.when(part == h)
            def _():
                prod = _dot_tn(a_ref[...].astype(BF16), b_refs[h if nb > 1 else 0][...].astype(BF16))

                @pl.when(t == 0)
                def _():
                    acc[h] = prod

                @pl.when(t > 0)
                def _():
                    acc[h] += prod

                @pl.when(t == nt - 1)
                def _():
                    if twin:
                        acc16[h] = acc[h].astype(BF16)
                    for cp in out_copies(h):
                        cp.start()

        @pl.when((part == nsplit - 1) & (t == nt - 1))
        def _():
            for h in range(nsplit):
                for cp in out_copies(h):
                    cp.wait()

    if nb > 1:
        a_spec = pl.BlockSpec((bt, mh), lambda h, t: (t, 0))
        b_specs = [pl.BlockSpec((bt, n), lambda h, t, k=k: (jnp.where(h == k, t, 0), 0)) for k in range(nb)]
    else:
        a_spec = pl.BlockSpec((bt, mh), lambda h, t: (t, h))
        b_specs = [pl.BlockSpec((bt, n), lambda h, t: (t, 0))]
    scratch = [pltpu.VMEM((nsplit, mh, n), F32)] + ([pltpu.VMEM((nsplit, mh, n), BF16)] if twin else []) + [pltpu.SemaphoreType.DMA((2, nsplit))]
    return pl.pallas_call(
        body, name=name, grid=(nsplit, nt),
        in_specs=[a_spec] + b_specs + [ANY] * nbuf,
        out_specs=[ANY] * nbuf, out_shape=[_sds(g.shape, g.dtype) for g in gbufs], input_output_aliases={1 + nb + k: k for k in range(nbuf)},
        scratch_shapes=scratch, compiler_params=_params(2),
    )(a, *bs, *gbufs)


def _dw_in_a_exchange(du, h, rest, early, lay, bt):
    t_all, d = h.shape
    half = d // 2
    rows, rest_rows = lay.rows["in_a"], lay.rows["rest"]
    c_in, c_rest = lay.c_off["in_a"], lay.c_off["rest"]
    nt = t_all // bt
    xi, yi, _ = _place()
    order = jnp.stack([2 * (1 - xi) + (1 - yi), 2 * (1 - xi) + yi, 2 * xi + (1 - yi), 2 * xi + yi]).astype(jnp.int32)

    def body(order_ref, a_ref, b_ref, rest_ref, early_own_ref, early_sib_ref, early_got_ref, own_ref, early_ref, got_ref, early_own, early_sib,
             early_got, acc, sibbuf, part16, restv, rest_sib, rest_p, rest16, own_v, own_r,
             d2d_send, d2d_recv, ici_send, ici_recv, local_sems):
        x, y, c = _place()
        chips = [(1 - x, 1 - y), (1 - x, y), (x, 1 - y)]
        g, t = pl.program_id(0), pl.program_id(1)
        their_cols = pl.ds(pl.multiple_of((1 - c) * half, LANE), half)

        def my_half(v):
            return jnp.where(c == 0, v[:, :half], v[:, half:])

        def d2d(src, dst, k):
            return pltpu.make_async_remote_copy(src_ref=src, dst_ref=dst, send_sem=d2d_send.at[k], recv_sem=d2d_recv.at[k],
                                                device_id=(x, y, 1 - c), device_id_type=MESH)

        def group_swap(gg):
            return d2d(acc.at[gg % 2, :, their_cols], sibbuf.at[gg], gg)

        def rest_swap():
            return d2d(restv.at[:, their_cols], rest_sib, 4)

        def to_chip(k, src, off, nrows):
            px, py = chips[k]
            return pltpu.make_async_remote_copy(src_ref=src, dst_ref=got_ref.at[k, pl.ds(off, nrows), :], send_sem=ici_send.at[k],
                                                recv_sem=ici_recv.at[k], device_id=(px, py, c), device_id_type=MESH)

        def own_copy(src, off, nrows, k):
            return pltpu.make_async_copy(src, own_ref.at[pl.ds(off, nrows), :], local_sems.at[k])

        def early_loads():
            return [pltpu.make_async_copy(src, dst, local_sems.at[3 + k])
                    for k, (src, dst) in enumerate(((early_own_ref, early_own), (early_sib_ref, early_sib), (early_got_ref, early_got)))]

        def early_store():
            return pltpu.make_async_copy(early_own, early_ref, local_sems.at[6])

        def early_sum():
            for cp in early_loads():
                cp.wait()

            def chunk(i, carry):
                there = pl.ds(pl.multiple_of(i * ROW_ALIGN, ROW_ALIGN), ROW_ALIGN)
                total = early_own[there, :] + early_sib[there, :]
                for k in range(early_got.shape[0]):
                    total = total + early_got[k, there, :].astype(F32)
                early_own[there, :] = total
                return carry

            lax.fori_loop(0, early_own.shape[0] // ROW_ALIGN, chunk, 0)
            early_store().start()

        def finish_group(gg):
            group_swap(gg).wait()
            part = my_half(acc[gg % 2]) + sibbuf[gg]
            if gg < 3:
                part16[gg] = part.astype(BF16)
                to_chip(gg, part16.at[gg], c_in, rows).start()
            else:
                own_v[...] = part

        @pl.when((g == 0) & (t == 0))
        def _():
            load = pltpu.make_async_copy(rest_ref, restv, local_sems.at[0])
            load.start()
            load.wait()
            rest_swap().start()
            for cp in early_loads():
                cp.start()

        @pl.when(t == 0)
        def _():
            acc[g % 2] = _dot_tn(a_ref[...], b_ref[...])

        @pl.when(t > 0)
        def _():
            acc[g % 2] += _dot_tn(a_ref[...], b_ref[...])

        for gg in range(4):
            @pl.when((g == gg) & (t == 0))
            def _():
                if gg == 0:
                    rest_swap().wait()
                    rest_p[...] = my_half(restv[...]) + rest_sib[...]
                    for k in range(3):
                        chip_rows = pl.ds(pl.multiple_of(order_ref[k] * rest_rows, SUBLANE), rest_rows)
                        rest16[k] = rest_p[chip_rows, :].astype(BF16)
                        to_chip(k, rest16.at[k], c_rest, rest_rows).start()
                    own_r[...] = rest_p[pl.ds(pl.multiple_of(order_ref[3] * rest_rows, SUBLANE), rest_rows), :]
                else:
                    finish_group(gg - 1)

            @pl.when((g == gg) & (t == nt - 1))
            def _():
                group_swap(gg).start()
                if gg == 3:
                    finish_group(3)
                    early_sum()
                    for k, (px, py) in enumerate(chips):
                        pltpu.make_async_remote_copy(src_ref=got_ref.at[k], dst_ref=got_ref.at[k], send_sem=ici_send.at[k], recv_sem=ici_recv.at[k],
                                                     device_id=(px, py, c), device_id_type=MESH).wait()
                    for buf, off, nrows, k in ((own_v, c_in, rows, 1), (own_r, c_rest, rest_rows, 2)):
                        there = pl.ds(off, nrows)
                        buf[...] = ((buf[...] + got_ref[0, there, :].astype(F32)) + got_ref[1, there, :].astype(F32)) + got_ref[2, there, :].astype(F32)
                        own_copy(buf, off, nrows, k).start()
                    own_copy(own_v, c_in, rows, 1).wait()
                    own_copy(own_r, c_rest, rest_rows, 2).wait()
                    early_store().wait()

    return pl.pallas_call(
        body, name="dw_in_a",
        grid_spec=pltpu.PrefetchScalarGridSpec(
            num_scalar_prefetch=1, grid=(N_CHIPS, nt),
            in_specs=[pl.BlockSpec((bt, rows), lambda g, t, order: (t, order[g])), pl.BlockSpec((bt, d), lambda g, t, order: (t, 0)), ANY, ANY, ANY, ANY],
            out_specs=[ANY, ANY],
            scratch_shapes=[pltpu.VMEM((3, lay.c_rows["late"], half), BF16), pltpu.VMEM(early[0].shape, F32), pltpu.VMEM(early[1].shape, F32),
                            pltpu.VMEM(early[2].shape, BF16), pltpu.VMEM((2, rows, d), F32), pltpu.VMEM((N_CHIPS, rows, half), F32), pltpu.VMEM((3, rows, half), BF16),
                            pltpu.VMEM((N_CHIPS * rest_rows, d), F32), pltpu.VMEM((N_CHIPS * rest_rows, half), F32),
                            pltpu.VMEM((N_CHIPS * rest_rows, half), F32), pltpu.VMEM((3, rest_rows, half), BF16),
                            pltpu.VMEM((rows, half), F32), pltpu.VMEM((rest_rows, half), F32),
                            pltpu.SemaphoreType.DMA((5,)), pltpu.SemaphoreType.DMA((5,)), pltpu.SemaphoreType.DMA((3,)), pltpu.SemaphoreType.DMA((3,)),
                            pltpu.SemaphoreType.DMA((7,))]),
        out_shape=[_sds((lay.c_rows["late"], half)), _sds(early[0].shape)], compiler_params=_params(2),
    )(order, du, h, rest, *early)


def _mm_tn(a, b, name, bt):
    t_all, m = a.shape
    n = b.shape[1]

    def body(a_ref, b_ref, o_ref):
        @pl.when(pl.program_id(0) == 0)
        def _():
            o_ref[...] = jnp.zeros((m, n), F32)

        o_ref[...] += _dot_tn(a_ref[...].astype(BF16), b_ref[...].astype(BF16))

    return pl.pallas_call(
        body, name=name, grid=(t_all // bt,),
        in_specs=[pl.BlockSpec((bt, m), lambda t: (t, 0)), pl.BlockSpec((bt, n), lambda t: (t, 0))],
        out_specs=_full((m, n)), out_shape=_sds((m, n)),
        compiler_params=_params(1),
    )(a, b)


class _Gather8:
    def __init__(self, x_ref, out_ref, send_sems, recv_sems, local_sem):
        x, y, c = _place()
        self.c, self.me, self.sibling = c, (x, y, c), (x, y, 1 - c)
        self.chips = [(1 - x, y), (x, 1 - y), (1 - x, 1 - y)]
        self.x_ref, self.out_ref, self.send_sems, self.recv_sems, self.local_sem = x_ref, out_ref, send_sems, recv_sems, local_sem

    def _slot(self, px, py, pc):
        return self.out_ref.at[4 * px + 2 * py + pc]

    def _copy(self, k, blk, to, src=None):
        return pltpu.make_async_remote_copy(
            src_ref=self._slot(*blk) if src is None else src, dst_ref=self._slot(*blk), send_sem=self.send_sems.at[k],
            recv_sem=self.recv_sems.at[k], device_id=to, device_id_type=MESH)

    def _mine(self):
        return pltpu.make_async_copy(self.x_ref, self._slot(*self.me), self.local_sem)

    def _first(self):
        return [self._copy(0, self.me, self.sibling, src=self.x_ref)] + [
            self._copy(1 + j, self.me, (*chip, self.c), src=self.x_ref) for j, chip in enumerate(self.chips)]

    def _passed(self):
        return [self._copy(4 + j, (*chip, self.c), self.sibling) for j, chip in enumerate(self.chips)]

    def start(self):
        self._mine().start()
        for cp in self._first():
            cp.start()

    def forward(self):
        passed = self._passed()
        for j, chip in enumerate(self.chips):
            self._copy(1 + j, (*chip, self.c), self.me).wait_recv()
            passed[j].start()

    def finish(self):
        self._copy(0, self.sibling, self.me).wait_recv()
        for j, chip in enumerate(self.chips):
            self._copy(4 + j, (*chip, 1 - self.c), self.me).wait_recv()
        for cp in self._first() + self._passed():
            cp.wait_send()
        self._mine().wait()


class _ScatterDirect:
    def __init__(self, g16_ref, g32_ref, got_ref, sib_ref, own_ref, send_sems, recv_sems, local_sem, lay, order):
        self.x, self.y, self.c = _place()
        self.chips = [(1 - self.x, self.y), (self.x, 1 - self.y), (1 - self.x, 1 - self.y)]
        self.refs = (g16_ref, g32_ref, got_ref, sib_ref, own_ref, send_sems, recv_sems, local_sem)
        self.lay, self.order, self.half = lay, order, lay.d // 2

    def _src(self, ref, key, chip, h):
        start = pl.multiple_of(self.lay.g_off[key] + chip * self.lay.rows[key], ROW_ALIGN)
        return ref.at[pl.ds(start, self.lay.rows[key]), pl.ds(pl.multiple_of(h * self.half, LANE), self.half)]

    def _compact(self, ref, key):
        return ref.at[pl.ds(self.lay.c_off[key], self.lay.rows[key]), :]

    def start(self):
        g16_ref, g32_ref, got_ref, sib_ref, own_ref, send_sems, recv_sems, local_sem = self.refs
        x, y, c = self.x, self.y, self.c
        for key in self.order:
            pltpu.make_async_copy(self._src(g32_ref, key, 2 * x + y, c), self._compact(own_ref, key), local_sem).start()
            pltpu.make_async_remote_copy(
                src_ref=self._src(g32_ref, key, 2 * x + y, 1 - c), dst_ref=self._compact(sib_ref, key), send_sem=send_sems.at[6],
                recv_sem=recv_sems.at[6], device_id=(x, y, 1 - c), device_id_type=MESH).start()
        for k, (px, py) in enumerate(self.chips):
            for h in range(2):
                for key in self.order:
                    pltpu.make_async_remote_copy(
                        src_ref=self._src(g16_ref, key, 2 * px + py, h), dst_ref=self._compact(got_ref.at[2 * k + c], key),
                        send_sem=send_sems.at[2 * k + h], recv_sem=recv_sems.at[2 * k + c], device_id=(px, py, h), device_id_type=MESH).start()

    def finish(self):
        _, _, got_ref, sib_ref, own_ref, send_sems, recv_sems, local_sem = self.refs
        x, y, c = self.x, self.y, self.c
        for k, (px, py) in enumerate(self.chips):
            for h in range(2):
                whole = pltpu.make_async_remote_copy(src_ref=got_ref.at[2 * k + h], dst_ref=got_ref.at[2 * k + h], send_sem=send_sems.at[2 * k + h],
                                                     recv_sem=recv_sems.at[2 * k + h], device_id=(px, py, h), device_id_type=MESH)
                whole.wait_send()
                whole.wait_recv()
        pltpu.make_async_remote_copy(src_ref=sib_ref, dst_ref=sib_ref, send_sem=send_sems.at[6], recv_sem=recv_sems.at[6],
                                     device_id=(x, y, 1 - c), device_id_type=MESH).wait()
        pltpu.make_async_copy(own_ref, own_ref, local_sem).wait()


def _scatter_direct_shapes(lay, group):
    rows, half = lay.c_rows[group], lay.d // 2
    return [_sds((6, rows, half), BF16), _sds((rows, half), F32), _sds((rows, half), F32)]


SCATTER_DIRECT_SEMS = [pltpu.SemaphoreType.DMA((7,)), pltpu.SemaphoreType.DMA((7,)), pltpu.SemaphoreType.DMA]
GATHER_SEMS = [pltpu.SemaphoreType.DMA((7,)), pltpu.SemaphoreType.DMA((7,)), pltpu.SemaphoreType.DMA]


def _all_gather8(blocks, name):
    nb = len(blocks)

    def body(*refs):
        x_refs, out_refs = refs[:nb], refs[nb:2 * nb]
        send_sems, recv_sems, local_sems = refs[2 * nb:]
        gathers = [_Gather8(x_refs[n], out_refs[n], send_sems.at[n], recv_sems.at[n], local_sems.at[n]) for n in range(nb)]
        for g in gathers:
            g.start()
        for g in gathers:
            g.forward()
        for g in gathers:
            g.finish()

    return pl.pallas_call(
        body, name=name, out_shape=[_sds((8,) + b.shape, b.dtype) for b in blocks], in_specs=[ANY] * nb, out_specs=[ANY] * nb,
        scratch_shapes=[pltpu.SemaphoreType.DMA((nb, 7)), pltpu.SemaphoreType.DMA((nb, 7)), pltpu.SemaphoreType.DMA((nb,))],
    )(*blocks)


def _return_and_gather(mines, rep_block):
    n = len(mines)

    def body(*refs):
        src_refs, rep_ref, out_refs, rep_out = refs[:n], refs[n], refs[n + 1:2 * n + 1], refs[2 * n + 1]
        send_sems, recv_sems, g_send, g_recv, g_local = refs[2 * n + 2:]
        x, y, c = _place()
        copies = [pltpu.make_async_remote_copy(src_ref=src_refs[k], dst_ref=out_refs[k], send_sem=send_sems.at[k], recv_sem=recv_sems.at[k],
                                               device_id=(x, y, 1 - c), device_id_type=MESH) for k in range(n)]
        gather = _Gather8(rep_ref, rep_out, g_send, g_recv, g_local)
        for cp in copies:
            cp.start()
        gather.start()
        gather.forward()
        gather.finish()
        for cp in copies:
            cp.wait()

    return pl.pallas_call(
        body, name="rs_return", out_shape=[_sds(m.shape, m.dtype) for m in mines] + [_sds((8,) + rep_block.shape, rep_block.dtype)],
        in_specs=[ANY] * (n + 1), out_specs=[ANY] * (n + 1),
        scratch_shapes=[pltpu.SemaphoreType.DMA((n,)), pltpu.SemaphoreType.DMA((n,))] + GATHER_SEMS,
    )(*mines, rep_block)


def _adamw(w, g, m, v):
    m = ADAM_B1 * m + (1.0 - ADAM_B1) * g
    v = ADAM_B2 * v + (1.0 - ADAM_B2) * (g * g)
    m_hat = m / (1.0 - ADAM_B1 ** ADAM_STEP)
    v_hat = v / (1.0 - ADAM_B2 ** ADAM_STEP)
    return -ADAM_LR * (m_hat / (jnp.sqrt(v_hat) + ADAM_EPS) + ADAM_WD * w), m, v


def _adamw_rows(name, w, g, m, v):
    _, rows, cols = w.shape
    rb = _row_block(rows, 256)

    def body(w_ref, g_ref, m_ref, v_ref, d_ref, mo_ref, vo_ref):
        d_ref[...], mo_ref[...], vo_ref[...] = _adamw(w_ref[...], g_ref[...], m_ref[...], v_ref[...])

    spec = pl.BlockSpec((1, rb, cols), lambda i: (0, i, 0))
    return pl.pallas_call(
        body, name=name, grid=(rows // rb,), in_specs=[spec] * 4, out_specs=[spec] * 3, out_shape=[_sds(w.shape)] * 3,
        compiler_params=_params(1),
    )(w, g, m, v)


def _adamw_group(ws, gs, ms, vs):
    n = len(ws)

    def body(*refs):
        for k in range(n):
            w_ref, g_ref, m_ref, v_ref = (refs[j * n + k] for j in range(4))
            outs = _adamw(w_ref[...], g_ref[...], m_ref[...], v_ref[...])
            for j in range(3):
                refs[(4 + j) * n + k][...] = outs[j]

    outs = pl.pallas_call(
        body, name="adamw_small", out_shape=[_sds(w.shape) for w in ws] * 3,
        compiler_params=pltpu.CompilerParams(vmem_limit_bytes=VMEM_LIMIT),
    )(*ws, *gs, *ms, *vs)
    return outs[:n], outs[n:2 * n], outs[2 * n:]


def _gather_weights(sh, lay):
    c = lax.axis_index("c")
    d = lay.d
    uq = sh["w_uq"][0].astype(BF16)
    parts = {
        "in_b": sh["w_in_b"][0].T.astype(BF16), "in_a": sh["w_in_a"][0].T.astype(BF16), "out_a": sh["w_out_a"][0].astype(BF16),
        "out_b": sh["w_out_b"][0].astype(BF16), "uk": sh["w_uk"].astype(BF16).reshape(-1, d), "uv": sh["w_uv"].astype(BF16).reshape(-1, d),
        "uq_n": uq[:, :, :QK_NOPE].reshape(-1, d), "uq_r": jnp.pad(uq[:, :, QK_NOPE:], ((0, 0), (0, 0), (0, LANE - QK_ROPE))).reshape(-1, d),
        "dkv": jnp.pad(sh["w_dkv"].astype(BF16), ((0, 0), (0, LANE - QK_ROPE))).reshape(-1, d),
    }
    halves = {}
    for group, order in W_GROUPS.items():
        stack = jnp.concatenate([parts[k] for k in order], axis=0).reshape(2, lay.w_rows[group] // 2, d)
        halves[group] = lax.dynamic_index_in_dim(stack, c, 0, keepdims=False)
    small = jnp.concatenate([sh[k].reshape(-1) for k in SMALL])
    n_small = small.shape[0]
    width = _round_up(n_small, 2 * SUBLANE * LANE) // (2 * SUBLANE)
    small = jnp.pad(small, (0, 2 * SUBLANE * width - n_small)).reshape(2, SUBLANE, width)
    wg, sg = _all_gather8([halves["a"], lax.dynamic_index_in_dim(small, c, 0, keepdims=False)], "ag_weights")
    wg = wg.reshape(N_CHIPS, lay.w_rows["a"], d)
    sg = sg.reshape(N_CHIPS, 2 * SUBLANE * width)
    full, off = {}, 0
    for k in SMALL:
        n = sh[k].size
        piece = sg[:, off:off + n]
        off += n
        if k == "conv_w":
            full[k] = piece.reshape(N_CHIPS, 4, n // 4).transpose(1, 0, 2).reshape(4, n)
        else:
            full[k] = piece.reshape(1, N_CHIPS * n)
    return wg, halves["b"], full


def _chip_split(g, taps=False):
    if taps:
        n = g.shape[1] // N_CHIPS
        return g.reshape(4, N_CHIPS, n).transpose(1, 0, 2).reshape(N_CHIPS, 4 * n)
    return g.reshape(N_CHIPS, -1)


def kernel(x, norm_a, w_in_a, conv_w, conv_b, w_rg, b_rg, w_ig, b_ig, lru_lambda, w_out_a, norm_kv, w_dkv, kv_norm, w_uk, w_uv, norm_b, w_in_b, q_norm, w_uq, w_out_b, final_norm, loss_target, m_norm_a, m_w_in_a, m_conv_w, m_conv_b, m_w_rg, m_b_rg, m_w_ig, m_b_ig, m_lru_lambda, m_w_out_a, m_norm_kv, m_w_dkv, m_kv_norm, m_w_uk, m_w_uv, m_norm_b, m_w_in_b, m_q_norm, m_w_uq, m_w_out_b, m_final_norm, v_norm_a, v_w_in_a, v_conv_w, v_conv_b, v_w_rg, v_b_rg, v_w_ig, v_b_ig, v_lru_lambda, v_w_out_a, v_norm_kv, v_w_dkv, v_kv_norm, v_w_uk, v_w_uv, v_norm_b, v_w_in_b, v_q_norm, v_w_uq, v_w_out_b, v_final_norm):
    given = dict(locals())
    sh = {k: given[k] for k in WEIGHTS}
    ci = lax.axis_index("c")
    nb, seq, d = x.shape
    t_all = nb * seq
    tb_a, tb_b, ta, bt = min(TOKENS_A, seq), min(TOKENS_B, seq), min(TOKENS_ATTN, seq), min(TOKENS_MM, t_all)
    dr = conv_b.shape[1] * N_CHIPS
    qr, kvr, nheads = q_norm.shape[1], kv_norm.shape[0], w_uk.shape[1]
    hv = nheads * LANE
    n_small = sum(sh[k].size for k in SMALL)
    n_repl = sum(sh[k].size for k in REPL)
    lay = _Layout(d, dr, qr, kvr, hv, n_small, n_repl)
    half = d // 2

    wga, wb_half, w = _gather_weights(sh, lay)
    w.update({"w_rg": w_rg[0].astype(BF16), "w_ig": w_ig[0].astype(BF16), "norm_kv": norm_kv[None, :],
              "kv_norm": kv_norm[None, :], "final_norm": final_norm[None, :], "norm_b": norm_b, "q_norm": q_norm})
    cos_t, sin_t = _rope_tables(seq)

    x0 = x.reshape(t_all, d)
    x1, u, hs, h, y, xb, wgb = _fa_fwd(x0, wga, wb_half, w, lay, seq, min(TOKENS_A_FWD, seq))
    wgb = wgb.reshape(N_CHIPS, lay.w_rows["b"], d)
    w["w_dkv_p"] = wgb[:, lay.w_off["dkv"]:lay.w_off["dkv"] + lay.rows["dkv"], :].reshape(d, kvr + LANE)
    qn, qrp, kn, kr, v, ub, ckr, hb, hk, cq, ckv = _fb_fwd(x1, wgb, w, lay, cos_t, sin_t, seq, tb_b)
    o, lse = _attn_fwd(qn, qrp, kn, kr, v, seq, ta)
    loss, g_final_norm, yb, dx2, do, dgate, delta = _head(o, ub, x1, loss_target.reshape(t_all, d), wgb, w, lay, tb_b)
    dqn, dqr, dkn, dkr, dv = _attn_bwd(qn, qrp, kn, kr, v, do, lse, delta, seq, ta)
    dx1, dqr_pre, dqn_pre, dub, dckr, g_q_norm, g_norm_b, g_kv_norm, g_norm_kv = _fb_bwd(
        dqn, dqr, dkn, dkr, dv, dgate, ub, ckr, x1, dx2, wgb, w, lay, cos_t, sin_t, seq, tb_b)
    loss = lax.psum(loss[0, 0], ("x", "y", "c"))

    gbufs = [lax.empty((lay.g_rows["early"], d), F32), lax.empty((lay.g_rows["early"], d), BF16)]
    for keys, a, bs in ((("in_b",), dub, (hb,)), (("out_a",), y, (dx1,)), (("out_b",), yb, (dx2,)), (("uk", "uv"), ckv, (dkn, dv)),
                        (("uq_n", "uq_r"), cq, (dqn_pre, dqr_pre))):
        gbufs = _mm_into(gbufs, a, bs, [lay.g_off[k] for k in keys], "dw_" + "_".join(keys), bt)
    g_dkv = _mm_tn(hk, dckr, "dw_dkv", bt)
    gx, du, g_norm_a, g_conv_w, g_conv_b, g_b_rg, g_b_ig, g_lam, g_w_rg, g_w_ig, others, sib, own = _fa_bwd(
        dx1, x0, u, xb, hs, wga, gbufs[1], gbufs[0], w, lay, seq, tb_a)

    small = jnp.concatenate([_chip_split(g_norm_a), _chip_split(g_conv_w, taps=True), _chip_split(g_conv_b), _chip_split(g_b_rg),
                             _chip_split(g_b_ig), _chip_split(g_lam)], axis=1)
    small = jnp.pad(small, ((0, 0), (0, lay.small_rows * d - small.shape[1]))).reshape(N_CHIPS, lay.small_rows, d)
    repl_parts = {"w_rg": g_w_rg, "w_ig": g_w_ig, "norm_kv": g_norm_kv, "kv_norm": g_kv_norm, "norm_b": g_norm_b, "q_norm": g_q_norm,
                  "final_norm": g_final_norm}
    repl = jnp.concatenate([repl_parts[k].reshape(-1) for k in REPL])
    repl = jnp.pad(repl, (0, N_CHIPS * lay.repl_rows * d - n_repl)).reshape(N_CHIPS, lay.repl_rows, d)
    pad_rows = lay.rows["rest"] - lay.rows["dkv"] - lay.small_rows - lay.repl_rows
    rest = jnp.concatenate([g_dkv.reshape(N_CHIPS, lay.rows["dkv"], d), small, repl, jnp.zeros((N_CHIPS, pad_rows, d), F32)], axis=1)
    mine_late, mine_early = _dw_in_a_exchange(du, h, rest.reshape(N_CHIPS * lay.rows["rest"], d), (own, sib, others), lay, bt)

    r0 = lay.c_off["rest"] + lay.rows["dkv"] + lay.small_rows
    theirs_early, theirs_late, rep_all = _return_and_gather([mine_early, mine_late], mine_late[r0:r0 + lay.repl_rows])
    red = {}
    for group, mine, theirs in (("early", mine_early, theirs_early), ("late", mine_late, theirs_late)):
        red[group] = jnp.concatenate([jnp.where(ci == 0, mine, theirs), jnp.where(ci == 0, theirs, mine)], axis=1)
    rep_flat =rep_all.reshape(N_CHIPS, 2, lay.repl_rows, half).transpose(0, 2, 1, 3).reshape(-1)

    def rows(key):
        group = "late" if key in G_GROUPS["late"] else "early"
        return red[group][lay.c_off[key]:lay.c_off[key] + lay.rows[key]]

    grads = {"w_in_b": rows("in_b").T[None], "w_in_a": rows("in_a").T[None], "w_out_a": rows("out_a")[None], "w_out_b": rows("out_b")[None],
             "w_uk": rows("uk").reshape(w_uk.shape), "w_uv": rows("uv").reshape(w_uv.shape)}
    uq_n = rows("uq_n").reshape(qr // N_CHIPS, nheads, LANE)
    uq_r = rows("uq_r").reshape(qr // N_CHIPS, nheads, LANE)[:, :, :QK_ROPE]
    grads["w_uq"] = jnp.concatenate([uq_n, uq_r], axis=2)[None]
    rest_red = rows("rest")
    grads["w_dkv"] = rest_red[:lay.rows["dkv"]].reshape(d // N_CHIPS, kvr + LANE)[:, :kvr + QK_ROPE]
    small_red = rest_red[lay.rows["dkv"]:lay.rows["dkv"] + lay.small_rows].reshape(-1)
    off = 0
    for k in SMALL:
        n = sh[k].size
        grads[k] = small_red[off:off + n].reshape(sh[k].shape)
        off += n
    off = 0
    for k in REPL:
        n = sh[k].size
        grads[k] = rep_flat[off:off + n].reshape(sh[k].shape)
        off += n

    new = {}
    for k in ("w_in_a", "w_in_b", "w_out_a", "w_out_b"):
        view = (lambda a: jnp.swapaxes(a, 1, 2)) if k in TRANSPOSED else (lambda a: a)
        outs = _adamw_rows("adamw_" + k, view(sh[k]), view(grads[k]), view(given["m_" + k]), view(given["v_" + k]))
        new[k] = tuple(view(a) for a in outs)
    rest_names = [k for k in WEIGHTS if k not in new]

    def as2d(k, a):
        return a.T if k in TRANSPOSED else a[None, :] if a.ndim == 1 else a

    ds, ms, vs = _adamw_group([as2d(k, sh[k]) for k in rest_names], [as2d(k, grads[k]) for k in rest_names],
                              [as2d(k, given["m_" + k]) for k in rest_names], [as2d(k, given["v_" + k]) for k in rest_names])
    for n, k in enumerate(rest_names):
        new[k] = tuple((a.T if k in TRANSPOSED else a).reshape(sh[k].shape) for a in (ds[n], ms[n], vs[n]))
    return (loss, gx.reshape(nb, seq, d), *[grads[k] for k in WEIGHTS], *[new[k][0] for k in WEIGHTS], *[new[k][1] for k in WEIGHTS],
            *[new[k][2] for k in WEIGHTS])
```
